```python
import math
import jax, jax.numpy as jnp
from jax import lax
import numpy as np

D_MODEL = 1024
BATCH = 4
SEQ = 4096
DEPTH = 1

N_MEM = 256
N_HEADS_A = 8
HEAD_DIM_A = 64
WIDTH_A = N_HEADS_A * HEAD_DIM_A
DILATED_PATTERNS = ((128, 1), (512, 4), (2048, 16))
N_HEADS_B = 8
QK_NOPE_DIM = 64
QK_ROPE_DIM = 32
V_DIM_B = 64
WIDTH_B = N_HEADS_B * V_DIM_B
Q_LORA_RANK = 256
KV_LORA_RANK = 128
ROPE_THETA = 10000.0
MIX_WIDTH = WIDTH_A + WIDTH_B
IN_PROJ_WIDTH = 3 * WIDTH_A + Q_LORA_RANK + KV_LORA_RANK + QK_ROPE_DIM
NUM_BUCKETS = 32
MAX_DISTANCE = 1024
N_HEADS_MEM = 4
HEAD_DIM_MEM = D_MODEL // N_HEADS_MEM
N_EXPERTS = 32
TOP_K = 4
D_EXPERT = D_MODEL
SWIGLU_LIMIT = 7.0
SWIGLU_ALPHA = 1.702
EXPERT_BLOCK = 256
Q_BLOCK = 128
EPS = 1e-6
NEG_INF = -1e30

kernel_name = "hybrid_dilated_mla_moe_encoder"


def rmsnorm(x, g):
    xf = x.astype(jnp.float32)
    y = xf * lax.rsqrt(jnp.mean(xf * xf, axis=-1, keepdims=True) + EPS)
    return (y * g.astype(jnp.float32)).astype(x.dtype)


def rope_cos_sin(positions):
    inv_freq = ROPE_THETA ** (-jnp.arange(0, QK_ROPE_DIM, 2, dtype=jnp.float32) / QK_ROPE_DIM)
    ang = positions.astype(jnp.float32)[..., None] * inv_freq
    return jnp.cos(ang), jnp.sin(ang)


def apply_rope(x, cos, sin):
    x1, x2 = jnp.split(x.astype(jnp.float32), 2, axis=-1)
    return jnp.concatenate([x1 * cos - x2 * sin, x1 * sin + x2 * cos], axis=-1).astype(x.dtype)


def rel_bucket(rel):
    nb = NUM_BUCKETS // 2
    max_exact = nb // 2
    ret = (rel > 0).astype(jnp.int32) * nb
    n = jnp.abs(rel)
    nf = jnp.maximum(n, max_exact).astype(jnp.float32)
    large = max_exact + (jnp.log(nf / max_exact) / math.log(MAX_DISTANCE / max_exact)
                         * (nb - max_exact)).astype(jnp.int32)
    large = jnp.minimum(large, nb - 1)
    return ret + jnp.where(n < max_exact, n, large)


def dilated_attention(q, k, v, positions, rel_bias):
    b, s = q.shape[:2]
    n_blocks = s // Q_BLOCK
    scale = HEAD_DIM_A ** -0.5

    def block(i):
        q0 = i * Q_BLOCK
        qb = lax.dynamic_slice_in_dim(q, q0, Q_BLOCK, axis=1)
        qpos = lax.dynamic_slice_in_dim(positions, q0, Q_BLOCK, axis=1)
        qidx = q0 + jnp.arange(Q_BLOCK)
        outs, lses = [], []
        for window, dil in DILATED_PATTERNS:
            half = window // (2 * dil)
            offs = jnp.arange(-half, half + 1) * dil
            kidx = qidx[:, None] + offs[None, :]
            valid = (kidx >= 0) & (kidx < s)
            kidx = jnp.clip(kidx, 0, s - 1)
            kg = k[:, kidx]
            vg = v[:, kidx]
            rel = positions[:, kidx] - qpos[:, :, None]
            bias = rel_bias[rel_bucket(rel)]
            sc = (jnp.einsum('bqhd,bqkhd->bqkh', qb, kg).astype(jnp.float32) * scale
                  + bias.astype(jnp.float32))
            sc = jnp.where(valid[None, :, :, None], sc, NEG_INF)
            lse = jax.nn.logsumexp(sc, axis=2)
            p = jnp.exp(sc - lse[:, :, None, :])
            outs.append(jnp.einsum('bqkh,bqkhd->bqhd', p, vg.astype(jnp.float32)))
            lses.append(lse)
        wts = jax.nn.softmax(jnp.stack(lses, axis=0), axis=0)
        o = jnp.sum(wts[..., None] * jnp.stack(outs, axis=0), axis=0)
        return o.astype(q.dtype)

    o = lax.map(block, jnp.arange(n_blocks))
    return o.transpose(1, 0, 2, 3, 4).reshape(b, s, WIDTH_A)


def mla_attention(q_nope, q_pe, k_nope, k_pe, v):
    b, s = q_nope.shape[:2]
    nb = s // Q_BLOCK
    scale = (QK_NOPE_DIM + QK_ROPE_DIM) ** -0.5

    def to_blocks(t):
        return t.reshape(b, nb, Q_BLOCK, *t.shape[2:]).swapaxes(0, 1)

    def block(args):
        qn, qp = args
        sc = (jnp.einsum('bqhd,bkhd->bhqk', qn, k_nope)
              + jnp.einsum('bqhr,bkr->bhqk', qp, k_pe)).astype(jnp.float32) * scale
        p = jax.nn.softmax(sc, axis=-1)
        return jnp.einsum('bhqk,bkhd->bqhd', p.astype(v.dtype), v)

    o = lax.map(block, (to_blocks(q_nope), to_blocks(q_pe)))
    return o.swapaxes(0, 1).reshape(b, s, WIDTH_B)


def hybrid_mixer(xn, positions, cos, sin, w_in, g_q_a, w_q_b, g_kv_a, w_kv_b,
                 rel_bias, g_out_a, g_out_b, w_o):
    b, s, _ = xn.shape
    z = xn @ w_in
    cuts = [WIDTH_A, 2 * WIDTH_A, 3 * WIDTH_A, 3 * WIDTH_A + Q_LORA_RANK,
            3 * WIDTH_A + Q_LORA_RANK + KV_LORA_RANK]
    qa, ka, va, zq, zkv, zkpe = jnp.split(z, cuts, axis=-1)
    shp_a = (b, s, N_HEADS_A, HEAD_DIM_A)
    o_a = dilated_attention(qa.reshape(shp_a), ka.reshape(shp_a), va.reshape(shp_a),
                            positions, rel_bias)
    q = (rmsnorm(zq, g_q_a) @ w_q_b).reshape(b, s, N_HEADS_B, QK_NOPE_DIM + QK_ROPE_DIM)
    q_nope, q_pe = jnp.split(q, [QK_NOPE_DIM], axis=-1)
    q_pe = apply_rope(q_pe, cos[:, :, None, :], sin[:, :, None, :])
    kv = (rmsnorm(zkv, g_kv_a) @ w_kv_b).reshape(b, s, N_HEADS_B, QK_NOPE_DIM + V_DIM_B)
    k_nope, v_b = jnp.split(kv, [QK_NOPE_DIM], axis=-1)
    k_pe = apply_rope(zkpe, cos, sin)
    o_b = mla_attention(q_nope, q_pe, k_nope, k_pe, v_b)
    o = jnp.concatenate([rmsnorm(o_a, g_out_a), rmsnorm(o_b, g_out_b)], axis=-1)
    return o @ w_o


def memory_cross_attention(hn, mem, g_mem, w_mq, w_mkv, w_mo):
    b, s, _ = hn.shape
    m = mem.shape[1]
    q = (hn @ w_mq).reshape(b, s, N_HEADS_MEM, HEAD_DIM_MEM)
    kv = (rmsnorm(mem, g_mem) @ w_mkv).reshape(b, m, 2, N_HEADS_MEM, HEAD_DIM_MEM)
    k, v = kv[:, :, 0], kv[:, :, 1]
    sc = jnp.einsum('bqhd,bmhd->bhqm', q, k).astype(jnp.float32) * HEAD_DIM_MEM ** -0.5
    p = jax.nn.softmax(sc, axis=-1)
    o = jnp.einsum('bhqm,bmhd->bqhd', p.astype(v.dtype), v).reshape(b, s, D_MODEL)
    return o @ w_mo


def moe_ffn(hn, w_router, b_router, w_gate_up, b_gate_up, w_down, b_down):
    b, s, d = hn.shape
    t = b * s
    xt = hn.reshape(t, d)
    logits = (xt @ w_router + b_router).astype(jnp.float32)
    top_val, top_idx = lax.top_k(logits, TOP_K)
    gates = jax.nn.softmax(top_val, axis=-1)
    n_assign = t * TOP_K
    flat_e = top_idx.reshape(-1)
    order = jnp.argsort(flat_e)
    sorted_e = flat_e[order]
    sorted_tok = (order // TOP_K).astype(jnp.int32)
    counts = jnp.bincount(flat_e, length=N_EXPERTS)
    padded = ((counts + EXPERT_BLOCK - 1) // EXPERT_BLOCK) * EXPERT_BLOCK
    ends = jnp.cumsum(padded)
    pad_start = ends - padded
    start = jnp.cumsum(counts) - counts
    dest = pad_start[sorted_e] + (jnp.arange(n_assign) - start[sorted_e])
    n_blk = -(-n_assign // EXPERT_BLOCK) + N_EXPERTS
    m_pad = n_blk * EXPERT_BLOCK
    tok_buf = jnp.zeros((m_pad,), jnp.int32).at[dest].set(sorted_tok)
    gate_buf = jnp.zeros((m_pad,), xt.dtype).at[dest].set(
        gates.reshape(-1)[order].astype(xt.dtype))
    block_expert = jnp.clip(
        jnp.searchsorted(ends, jnp.arange(n_blk) * EXPERT_BLOCK, side='right'),
        0, N_EXPERTS - 1)

    def expert_block(args):
        e, tok = args
        xb = xt[tok]
        gu = xb @ w_gate_up[e] + b_gate_up[e]
        gate, up = jnp.split(gu, 2, axis=-1)
        gate = jnp.minimum(gate, SWIGLU_LIMIT)
        up = jnp.clip(up, -SWIGLU_LIMIT, SWIGLU_LIMIT)
        hmid = (up + 1.0) * (gate * jax.nn.sigmoid(SWIGLU_ALPHA * gate))
        return hmid @ w_down[e] + b_down[e]

    out = lax.map(expert_block, (block_expert, tok_buf.reshape(n_blk, EXPERT_BLOCK)))
    out = out.reshape(m_pad, d) * gate_buf[:, None]
    y = jnp.zeros((t, d), xt.dtype).at[tok_buf].add(out)
    return y.reshape(b, s, d)


def setup_inputs(seed: int = 0) -> dict:
    key = jax.random.key(seed)
    ks = jax.random.split(key, 32)
    f32 = jnp.float32
    L = DEPTH

    def nrm(k, shape, scale):
        return jax.random.normal(k, shape, f32) * scale

    def gain(k, shape):
        return 1.0 + 0.05 * jax.random.normal(k, shape, f32)

    x = nrm(ks[0], (BATCH, SEQ, D_MODEL), 1.0)
    mem = nrm(ks[1], (BATCH, N_MEM, D_MODEL), 1.0)
    positions = (jnp.arange(SEQ, dtype=jnp.int32)[None, :]
                 + jax.random.randint(ks[2], (BATCH, 1), 0, 1024, dtype=jnp.int32))
    return {
        "x": x,
        "mem": mem,
        "positions": positions,
        "g_mix": gain(ks[3], (L, D_MODEL)),
        "w_in": nrm(ks[4], (L, D_MODEL, IN_PROJ_WIDTH), D_MODEL ** -0.5),
        "g_q_a": gain(ks[5], (L, Q_LORA_RANK)),
        "w_q_b": nrm(ks[6], (L, Q_LORA_RANK, N_HEADS_B * (QK_NOPE_DIM + QK_ROPE_DIM)), Q_LORA_RANK ** -0.5),
        "g_kv_a": gain(ks[7], (L, KV_LORA_RANK)),
        "w_kv_b": nrm(ks[8], (L, KV_LORA_RANK, N_HEADS_B * (QK_NOPE_DIM + V_DIM_B)), KV_LORA_RANK ** -0.5),
        "rel_bias": nrm(ks[9], (NUM_BUCKETS, N_HEADS_A), 0.5),
        "g_out_a": gain(ks[10], (L, WIDTH_A)),
        "g_out_b": gain(ks[11], (L, WIDTH_B)),
        "w_o": nrm(ks[12], (L, MIX_WIDTH, D_MODEL), MIX_WIDTH ** -0.5),
        "g_xattn": gain(ks[13], (L, D_MODEL)),
        "g_mem": gain(ks[14], (L, D_MODEL)),
        "w_mq": nrm(ks[15], (L, D_MODEL, D_MODEL), D_MODEL ** -0.5),
        "w_mkv": nrm(ks[16], (L, D_MODEL, 2 * D_MODEL), D_MODEL ** -0.5),
        "w_mo": nrm(ks[17], (L, D_MODEL, D_MODEL), D_MODEL ** -0.5),
        "g_moe": gain(ks[18], (L, D_MODEL)),
        "w_router": nrm(ks[19], (L, D_MODEL, N_EXPERTS), D_MODEL ** -0.5),
        "b_router": nrm(ks[20], (L, N_EXPERTS), 0.01),
        "w_gate_up": nrm(ks[21], (L, N_EXPERTS, D_MODEL, 2 * D_EXPERT), D_MODEL ** -0.5),
        "b_gate_up": nrm(ks[22], (L, N_EXPERTS, 2 * D_EXPERT), 0.02),
        "w_down": nrm(ks[23], (L, N_EXPERTS, D_EXPERT, D_MODEL), D_EXPERT ** -0.5),
        "b_down": nrm(ks[24], (L, N_EXPERTS, D_MODEL), 0.02),
        "g_final": gain(ks[25], (D_MODEL,)),
    }


def reference(x, mem, positions, g_mix, w_in, g_q_a, w_q_b, g_kv_a, w_kv_b, rel_bias,
              g_out_a, g_out_b, w_o, g_xattn, g_mem, w_mq, w_mkv, w_mo, g_moe,
              w_router, b_router, w_gate_up, b_gate_up, w_down, b_down, g_final):
    cos, sin = rope_cos_sin(positions)
    h = x
    for l in range(DEPTH):
        h = h + hybrid_mixer(rmsnorm(h, g_mix[l]), positions, cos, sin, w_in[l], g_q_a[l],
                             w_q_b[l], g_kv_a[l], w_kv_b[l], rel_bias, g_out_a[l],
                             g_out_b[l], w_o[l])
        h = h + memory_cross_attention(rmsnorm(h, g_xattn[l]), mem, g_mem[l], w_mq[l],
                                       w_mkv[l], w_mo[l])
        h = h + moe_ffn(rmsnorm(h, g_moe[l]), w_router[l], b_router[l], w_gate_up[l],
                        b_gate_up[l], w_down[l], b_down[l])
    return rmsnorm(h, g_final)
```

```python
import functools
import math

import jax
import jax.numpy as jnp
from jax import lax
from jax.experimental import pallas as pl
from jax.experimental.pallas import tpu as pltpu

F32 = jnp.float32
BF16 = jnp.bfloat16

LANES = 128
EPS = 1e-6
NEG_INF = -1e30

N_HEADS_A = 8
HEAD_DIM_A = 64
WIDTH_A = N_HEADS_A * HEAD_DIM_A
DILATED_PATTERNS = ((128, 1), (512, 4), (2048, 16))
N_HEADS_B = 8
QK_NOPE_DIM = 64
QK_ROPE_DIM = 32
V_DIM_B = 64
Q_LORA_RANK = 256
KV_LORA_RANK = 128
ROPE_THETA = 10000.0
NUM_BUCKETS = 32
MAX_DISTANCE = 1024
N_HEADS_MEM = 4
N_EXPERTS = 32
TOP_K = 4
SWIGLU_LIMIT = 7.0
SWIGLU_ALPHA = 1.702

Q_BLOCK = 128
K_WINDOW = 256
EXPERT_BLOCK = 256
VMEM_LIMIT = 56 * 1024 * 1024


def _rms(x, g):
    return x * lax.rsqrt(jnp.mean(x * x, axis=-1, keepdims=True) + EPS) * g


def _dot(a, b):
    return jnp.dot(a, b, preferred_element_type=F32)


def _dot_nt(a, b):
    return lax.dot_general(a, b, (((1,), (1,)), ((), ())), preferred_element_type=F32)


def _proj_kernel(x_ref, cos_ref, sin_ref, gmix_ref, win_ref, gq_ref, wq_ref, wqr_ref,
                 gkv_ref, wk_ref, wv_ref,
                 qa_ref, ka_ref, va_ref, qm_ref, km_ref, vm_ref):
    xn = _rms(x_ref[0], gmix_ref[...])
    z = _dot(xn.astype(BF16), win_ref[...])
    qa_ref[0] = (z[:, 0:WIDTH_A] * (HEAD_DIM_A ** -0.5)).astype(BF16)
    ka_ref[0] = z[:, WIDTH_A:2 * WIDTH_A].astype(BF16)
    va_ref[0] = z[:, 2 * WIDTH_A:3 * WIDTH_A].astype(BF16)
    c0 = 3 * WIDTH_A
    zqn = _rms(z[:, c0:c0 + Q_LORA_RANK], gq_ref[...]).astype(BF16)
    zkvn = _rms(z[:, c0 + Q_LORA_RANK:c0 + Q_LORA_RANK + KV_LORA_RANK], gkv_ref[...]).astype(BF16)
    c1 = c0 + Q_LORA_RANK + KV_LORA_RANK
    cos = cos_ref[0]
    sin = sin_ref[0]
    kpe = z[:, c1:c1 + LANES] * cos + z[:, c1 + LANES:c1 + 2 * LANES] * sin
    q = _dot(zqn, wq_ref[...])
    qr = _dot(zqn, wqr_ref[...])
    kn = _dot(zkvn, wk_ref[...])
    vv = _dot(zkvn, wv_ref[...])
    lane = lax.broadcasted_iota(jnp.int32, (1, LANES), 1)
    ones_col = jnp.where(lane == V_DIM_B, 1.0, 0.0).astype(F32)
    scale = (QK_NOPE_DIM + QK_ROPE_DIM) ** -0.5
    for h in range(N_HEADS_B):
        sl = slice(h * LANES, (h + 1) * LANES)
        qm_ref[0, h] = ((q[:, sl] * cos + qr[:, sl] * sin) * scale).astype(BF16)
        km_ref[0, h] = (kn[:, sl] + kpe).astype(BF16)
        vm_ref[0, h] = (vv[:, sl] + ones_col).astype(BF16)


def _proj(x, cos128, sin128, g_mix, w_in_ext, g_q, wq, wqr, g_kv, wk, wv, tm):
    b, s, d = x.shape
    n_in = w_in_ext.shape[1]
    hw = N_HEADS_B * LANES
    full = lambda shape: pl.BlockSpec(shape, lambda bi, i: (0,) * len(shape))
    row = lambda w: pl.BlockSpec((1, tm, w), lambda bi, i: (bi, i, 0))
    head = pl.BlockSpec((1, N_HEADS_B, tm, LANES), lambda bi, i: (bi, 0, i, 0))
    return pl.pallas_call(
        _proj_kernel,
        grid=(b, s // tm),
        in_specs=[row(d), row(LANES), row(LANES), full((1, d)), full((d, n_in)),
                  full((1, Q_LORA_RANK)), full((Q_LORA_RANK, hw)), full((Q_LORA_RANK, hw)),
                  full((1, KV_LORA_RANK)), full((KV_LORA_RANK, hw)), full((KV_LORA_RANK, hw))],
        out_specs=[row(WIDTH_A), row(WIDTH_A), row(WIDTH_A), head, head, head],
        out_shape=[jax.ShapeDtypeStruct((b, s, WIDTH_A), BF16)] * 3
        + [jax.ShapeDtypeStruct((b, N_HEADS_B, s, LANES), BF16)] * 3,
        compiler_params=pltpu.CompilerParams(
            dimension_semantics=("parallel", "parallel"), vmem_limit_bytes=VMEM_LIMIT),
        name="proj",
    )(x, cos128, sin128, g_mix, w_in_ext, g_q, wq, wqr, g_kv, wk, wv)


_LOG_BUCKET_STARTS = tuple(
    next(n for n in range(8, 4096)
         if int(math.log(n / 8) / math.log(MAX_DISTANCE / 8) * 8) >= t)
    for t in range(1, 8))


def _rel_bucket(rel):
    n = jnp.abs(rel)
    large = jnp.full(rel.shape, NUM_BUCKETS // 4, jnp.int32)
    for start in _LOG_BUCKET_STARTS:
        large = large + jnp.where(n >= start, 1, 0)
    mag = jnp.where(n < NUM_BUCKETS // 4, n, large)
    return mag + jnp.where(rel > 0, NUM_BUCKETS // 2, 0)


def _dilated_kernel(q_ref, k_ref, v_ref, qpos_ref, kpos_ref, tbl_ref, o_ref, lse_ref, *, sub_len):
    j = pl.program_id(2)
    kstart = jnp.clip(j * Q_BLOCK - (K_WINDOW - Q_BLOCK) // 2, 0, sub_len - K_WINDOW)
    kstart = pl.multiple_of(kstart, (K_WINDOW - Q_BLOCK) // 2)
    q = q_ref[0]
    kw = k_ref[0, pl.ds(kstart, K_WINDOW), :]
    vw = v_ref[0, pl.ds(kstart, K_WINDOW), :]
    rel = kpos_ref[0, 0] - qpos_ref[0]
    bucket = _rel_bucket(rel)
    qsub = j * Q_BLOCK + lax.broadcasted_iota(jnp.int32, (Q_BLOCK, 1), 0)
    ksub = kstart + lax.broadcasted_iota(jnp.int32, (1, K_WINDOW), 1)
    valid = jnp.abs(ksub - qsub) <= (K_WINDOW - Q_BLOCK) // 2
    lane = lax.broadcasted_iota(jnp.int32, (1, LANES), 1)
    first = lane < HEAD_DIM_A
    for p in range(N_HEADS_A // 2):
        sl = slice(p * LANES, (p + 1) * LANES)
        qp, kp, vp = q[:, sl], kw[:, sl], vw[:, sl]
        outs, lses = [], []
        for hh in range(2):
            h = 2 * p + hh
            qh = jnp.where(first if hh == 0 else jnp.logical_not(first), qp, jnp.zeros_like(qp))
            s = _dot_nt(qh, kp)
            tbl = jnp.broadcast_to(tbl_ref[h:h + 1, :], (Q_BLOCK, LANES))
            bias = jnp.concatenate(
                [jnp.take_along_axis(tbl, bucket[:, c * LANES:(c + 1) * LANES], axis=1)
                 for c in range(K_WINDOW // LANES)], axis=1)
            s = jnp.where(valid, s + bias, NEG_INF)
            m = jnp.max(s, axis=1, keepdims=True)
            e = jnp.exp(s - m)
            l = jnp.sum(e, axis=1, keepdims=True)
            outs.append(_dot(e.astype(BF16), vp) / l)
            lses.append(m + jnp.log(l))
        o_ref[0, :, sl] = jnp.where(first, outs[0], outs[1])
        lse_ref[0, :, sl] = jnp.where(first, lses[0], lses[1])


def _dilated(qa, ka, va, positions, tbl, dil):
    b, s, w = qa.shape
    sub_len = s // dil
    nq = sub_len // Q_BLOCK
    view = lambda t: t.reshape(b, sub_len, dil * w)
    pos_t = positions.reshape(b, sub_len, dil).transpose(0, 2, 1).reshape(b * dil, sub_len)
    half = (K_WINDOW - Q_BLOCK) // 2
    starts = jnp.clip(jnp.arange(nq) * Q_BLOCK - half, 0, sub_len - K_WINDOW)
    win = starts[:, None] + jnp.arange(K_WINDOW)[None, :]
    kpos = pos_t[:, win].reshape(b * dil, nq, 1, K_WINDOW)
    qpos = pos_t.reshape(b * dil, sub_len, 1)
    qspec = pl.BlockSpec((1, Q_BLOCK, w), lambda bi, r, j: (bi, j, r))
    kvspec = pl.BlockSpec((1, sub_len, w), lambda bi, r, j: (bi, 0, r))
    o, lse = pl.pallas_call(
        functools.partial(_dilated_kernel, sub_len=sub_len),
        grid=(b, dil, nq),
        in_specs=[qspec, kvspec, kvspec,
                  pl.BlockSpec((1, Q_BLOCK, 1), lambda bi, r, j: (bi * dil + r, j, 0)),
                  pl.BlockSpec((1, 1, 1, K_WINDOW), lambda bi, r, j: (bi * dil + r, j, 0, 0)),
                  pl.BlockSpec((N_HEADS_A, LANES), lambda bi, r, j: (0, 0))],
        out_specs=[qspec, qspec],
        out_shape=[jax.ShapeDtypeStruct((b, sub_len, dil * w), F32)] * 2,
        compiler_params=pltpu.CompilerParams(
            dimension_semantics=("parallel", "parallel", "arbitrary"),
            vmem_limit_bytes=VMEM_LIMIT),
        name=f"dilated{dil}",
    )(view(qa), view(ka), view(va), qpos, kpos, tbl)
    return o.reshape(b, s, w), lse.reshape(b, s, w)


def _mla_kernel(q_ref, k_ref, v_ref, o_ref, *, tk):
    q = q_ref[0, 0]
    tq = q.shape[0]
    n_kv = k_ref.shape[2] // tk

    def body(i, carry):
        m, acc = carry
        start = pl.multiple_of(i * tk, tk)
        k = k_ref[0, 0, pl.ds(start, tk), :]
        v = v_ref[0, 0, pl.ds(start, tk), :]
        s = _dot_nt(q, k)
        m_new = jnp.maximum(m, jnp.max(s, axis=1, keepdims=True))
        p = jnp.exp(s - m_new)
        acc = jnp.exp(m - m_new) * acc + _dot(p.astype(BF16), v)
        return m_new, acc

    m0 = jnp.full((tq, 1), NEG_INF, F32)
    _, acc = lax.fori_loop(0, n_kv, body, (m0, jnp.zeros((tq, LANES), F32)))
    lane = lax.broadcasted_iota(jnp.int32, (1, LANES), 1)
    denom = jnp.sum(jnp.where(lane == V_DIM_B, acc, 0.0), axis=1, keepdims=True)
    o_ref[0, 0] = jnp.where(lane < V_DIM_B, acc / denom, 0.0).astype(BF16)


def _mla(qm, km, vm, tq, tk):
    b, nh, s, _ = qm.shape
    qspec = pl.BlockSpec((1, 1, tq, LANES), lambda bi, h, i: (bi, h, i, 0))
    kvspec = pl.BlockSpec((1, 1, s, LANES), lambda bi, h, i: (bi, h, 0, 0))
    return pl.pallas_call(
        functools.partial(_mla_kernel, tk=tk),
        grid=(b, nh, s // tq),
        in_specs=[qspec, kvspec, kvspec],
        out_specs=qspec,
        out_shape=jax.ShapeDtypeStruct((b, nh, s, LANES), BF16),
        compiler_params=pltpu.CompilerParams(
            dimension_semantics=("parallel", "parallel", "arbitrary"),
            vmem_limit_bytes=VMEM_LIMIT),
        name="mla",
    )(qm, km, vm)


def _memkv_kernel(mem_ref, g_ref, w_ref, k_ref, v_ref):
    d = mem_ref.shape[2]
    kv = _dot(_rms(mem_ref[0], g_ref[...]).astype(BF16), w_ref[...])
    k_ref[0] = kv[:, :d].astype(BF16)
    v_ref[0] = kv[:, d:].astype(BF16)


def _memkv(mem, g_mem, w_mkv):
    b, m, d = mem.shape
    spec = pl.BlockSpec((1, m, d), lambda bi: (bi, 0, 0))
    return pl.pallas_call(
        _memkv_kernel,
        grid=(b,),
        in_specs=[spec, pl.BlockSpec((1, d), lambda bi: (0, 0)),
                  pl.BlockSpec((d, 2 * d), lambda bi: (0, 0))],
        out_specs=[spec, spec],
        out_shape=[jax.ShapeDtypeStruct((b, m, d), BF16)] * 2,
        compiler_params=pltpu.CompilerParams(
            dimension_semantics=("parallel",), vmem_limit_bytes=VMEM_LIMIT),
        name="memkv",
    )(mem, g_mem, w_mkv)


def _post_kernel(x_ref, o1_ref, o2_ref, o3_ref, l1_ref, l2_ref, l3_ref, ob_ref,
                 goa_ref, gob_ref, woa_ref, wob_ref, gx_ref, wmq_ref, km_ref, vm_ref, wmo_ref,
                 gmoe_ref, wr_ref, br_ref,
                 h_ref, xn_ref, logit_ref):
    l1, l2, l3 = l1_ref[0], l2_ref[0], l3_ref[0]
    mx = jnp.maximum(jnp.maximum(l1, l2), l3)
    w1, w2, w3 = jnp.exp(l1 - mx), jnp.exp(l2 - mx), jnp.exp(l3 - mx)
    oa = (w1 * o1_ref[0] + w2 * o2_ref[0] + w3 * o3_ref[0]) / (w1 + w2 + w3)
    oan = _rms(oa, goa_ref[...]).astype(BF16)
    ob = jnp.concatenate([ob_ref[0, h] for h in range(N_HEADS_B)], axis=1).astype(F32)
    ms_b = jnp.sum(ob * ob, axis=1, keepdims=True) * (1.0 / (N_HEADS_B * V_DIM_B))
    obn = (ob * lax.rsqrt(ms_b + EPS) * gob_ref[...]).astype(BF16)
    h1 = x_ref[0] + _dot(oan, woa_ref[...]) + _dot(obn, wob_ref[...])

    hn = _rms(h1, gx_ref[...]).astype(BF16)
    q = _dot(hn, wmq_ref[...]).astype(BF16)
    dh = q.shape[1] // N_HEADS_MEM
    heads = []
    for h in range(N_HEADS_MEM):
        sl = slice(h * dh, (h + 1) * dh)
        s = _dot_nt(q[:, sl], km_ref[0, :, sl])
        e = jnp.exp(s - jnp.max(s, axis=1, keepdims=True))
        p = e / jnp.sum(e, axis=1, keepdims=True)
        heads.append(_dot(p.astype(BF16), vm_ref[0, :, sl]))
    o = jnp.concatenate(heads, axis=1).astype(BF16)
    h2 = h1 + _dot(o, wmo_ref[...])
    h_ref[0] = h2

    xn = _rms(h2, gmoe_ref[...])
    xn_ref[0] = xn
    logit_ref[0] = jnp.dot(xn, wr_ref[...], preferred_element_type=F32,
                           precision=lax.Precision.HIGHEST) + br_ref[...]


def _post(x, o_pats, lse_pats, ob, g_out_a, g_out_b_pad, w_o_a, w_o_b_pad, g_xattn, w_mq, kmem,
          vmem, w_mo, g_moe, w_router_pad, b_router_pad, tm):
    b, s, d = x.shape
    m = kmem.shape[1]
    full = lambda shape: pl.BlockSpec(shape, lambda bi, i: (0,) * len(shape))
    row = lambda w: pl.BlockSpec((1, tm, w), lambda bi, i: (bi, i, 0))
    memspec = pl.BlockSpec((1, m, d), lambda bi, i: (bi, 0, 0))
    hw = N_HEADS_B * LANES
    return pl.pallas_call(
        _post_kernel,
        grid=(b, s // tm),
        in_specs=[row(d)] + [row(WIDTH_A)] * 6
        + [pl.BlockSpec((1, N_HEADS_B, tm, LANES), lambda bi, i: (bi, 0, i, 0)),
           full((1, WIDTH_A)), full((1, hw)), full((WIDTH_A, d)), full((hw, d)),
           full((1, d)), full((d, d)), memspec, memspec, full((d, d)),
           full((1, d)), full((d, LANES)), full((1, LANES))],
        out_specs=[row(d), row(d), row(LANES)],
        out_shape=[jax.ShapeDtypeStruct((b, s, d), F32), jax.ShapeDtypeStruct((b, s, d), F32),
                   jax.ShapeDtypeStruct((b, s, LANES), F32)],
        compiler_params=pltpu.CompilerParams(
            dimension_semantics=("parallel", "parallel"), vmem_limit_bytes=VMEM_LIMIT),
        name="post",
    )(x, *o_pats, *lse_pats, ob, g_out_a, g_out_b_pad, w_o_a, w_o_b_pad, g_xattn, w_mq, kmem,
      vmem, w_mo, g_moe, w_router_pad, b_router_pad)


def _route_kernel(logit_ref, gate_ref, eidx_ref, rank_ref, count_ref, carry_ref):
    @pl.when(pl.program_id(0) == 0)
    def _():
        carry_ref[...] = jnp.zeros_like(carry_ref)

    l = logit_ref[...]
    tr = l.shape[0]
    lane_i = lax.broadcasted_iota(jnp.int32, l.shape, 1)
    lane = lane_i.astype(F32)
    vals, idxs = [], []
    for _ in range(TOP_K):
        m = jnp.max(l, axis=1, keepdims=True)
        idx = jnp.min(jnp.where(l == m, lane, float(LANES)), axis=1, keepdims=True)
        vals.append(m)
        idxs.append(idx)
        l = jnp.where(lane == idx, -jnp.inf, l)
    exps = [jnp.exp(v - vals[0]) for v in vals]
    denom = exps[0] + exps[1] + exps[2] + exps[3]
    onehot = jnp.zeros(l.shape, F32)
    for idx in idxs:
        onehot = onehot + jnp.where(lane == idx, 1.0, 0.0)
    r = lax.broadcasted_iota(jnp.int32, (tr, tr), 0)
    c = lax.broadcasted_iota(jnp.int32, (tr, tr), 1)
    tri = jnp.where(c < r, 1.0, 0.0).astype(BF16)
    before = _dot(tri, onehot.astype(BF16)) + carry_ref[...]
    gate = jnp.zeros(l.shape, F32)
    eidx = jnp.zeros(l.shape, jnp.int32)
    rank = jnp.zeros(l.shape, jnp.int32)
    for k in range(TOP_K):
        rk = jnp.sum(jnp.where(lane == idxs[k], before, 0.0), axis=1, keepdims=True)
        gate = jnp.where(lane_i == k, exps[k] / denom, gate)
        eidx = jnp.where(lane_i == k, idxs[k].astype(jnp.int32), eidx)
        rank = jnp.where(lane_i == k, rk.astype(jnp.int32), rank)
    gate_ref[...] = gate
    eidx_ref[...] = eidx
    rank_ref[...] = rank
    total = carry_ref[...] + jnp.sum(onehot, axis=0, keepdims=True)
    carry_ref[...] = total
    count_ref[...] = total.astype(jnp.int32)


def _route(logits, tr):
    t = logits.shape[0]
    spec = pl.BlockSpec((tr, LANES), lambda i: (i, 0))
    return pl.pallas_call(
        _route_kernel,
        grid=(t // tr,),
        in_specs=[spec],
        out_specs=[spec, spec, spec, pl.BlockSpec((1, LANES), lambda i: (0, 0))],
        out_shape=[jax.ShapeDtypeStruct((t, LANES), F32), jax.ShapeDtypeStruct((t, LANES), jnp.int32),
                   jax.ShapeDtypeStruct((t, LANES), jnp.int32),
                   jax.ShapeDtypeStruct((1, LANES), jnp.int32)],
        scratch_shapes=[pltpu.VMEM((1, LANES), F32)],
        compiler_params=pltpu.CompilerParams(
            dimension_semantics=("arbitrary",), vmem_limit_bytes=VMEM_LIMIT),
        name="route",
    )(logits)


def _expert_kernel(be_ref, nused_ref, tok_ref, x_hbm, wgu_ref, bgu_ref, wd_ref, bd_ref,
                   out_ref, xbuf, wgu_bf, wd_bf, sems):
    i = pl.program_id(0)
    n_used = nused_ref[0]
    slot = lax.rem(i, 2)
    blk = xbuf.shape[1]

    def row_copy(src_row, dst_row, s):
        return pltpu.make_async_copy(x_hbm.at[pl.ds(src_row, 1)],
                                     xbuf.at[s, pl.ds(dst_row, 1)], sems.at[s])

    def issue(block, s):
        def body(r, carry):
            row_copy(tok_ref[block * blk + r], r, s).start()
            return carry
        lax.fori_loop(0, blk, body, 0, unroll=8)

    @pl.when(i == 0)
    def _():
        issue(0, 0)

    @pl.when(i + 1 < n_used)
    def _():
        issue(i + 1, 1 - slot)

    @pl.when(i < n_used)
    def _():
        pltpu.make_async_copy(x_hbm.at[pl.ds(0, blk)], xbuf.at[slot], sems.at[slot]).wait()

        @pl.when(jnp.logical_or(i == 0, be_ref[i] != be_ref[jnp.maximum(i - 1, 0)]))
        def _():
            wgu_bf[...] = wgu_ref[0].astype(BF16)
            wd_bf[...] = wd_ref[0].astype(BF16)

        x = xbuf[slot].astype(BF16)
        gu = _dot(x, wgu_bf[...]) + bgu_ref[0]
        de = gu.shape[1] // 2
        gate = jnp.minimum(gu[:, :de], SWIGLU_LIMIT)
        up = jnp.clip(gu[:, de:], -SWIGLU_LIMIT, SWIGLU_LIMIT)
        hmid = (up + 1.0) * (gate * jax.nn.sigmoid(SWIGLU_ALPHA * gate))
        out_ref[...] = _dot(hmid.astype(BF16), wd_bf[...]) + bd_ref[0]

    @pl.when(i >= n_used)
    def _():
        out_ref[...] = jnp.zeros_like(out_ref)


def _experts(block_expert, n_used, tok_buf, xn, w_gate_up, b_gate_up, w_down, b_down):
    t, d = xn.shape
    n_blk = block_expert.shape[0]
    de2 = w_gate_up.shape[2]
    blk = EXPERT_BLOCK
    grid_spec = pltpu.PrefetchScalarGridSpec(
        num_scalar_prefetch=3,
        grid=(n_blk,),
        in_specs=[pl.BlockSpec(memory_space=pl.ANY),
                  pl.BlockSpec((1, d, de2), lambda i, be, nu, tok: (be[i], 0, 0)),
                  pl.BlockSpec((1, 1, de2), lambda i, be, nu, tok: (be[i], 0, 0)),
                  pl.BlockSpec((1, de2 // 2, d), lambda i, be, nu, tok: (be[i], 0, 0)),
                  pl.BlockSpec((1, 1, d), lambda i, be, nu, tok: (be[i], 0, 0))],
        out_specs=pl.BlockSpec((blk, d), lambda i, be, nu, tok: (i, 0)),
        scratch_shapes=[pltpu.VMEM((2, blk, d), F32), pltpu.VMEM((d, de2), BF16),
                        pltpu.VMEM((de2 // 2, d), BF16), pltpu.SemaphoreType.DMA((2,))],
    )
    return pl.pallas_call(
        _expert_kernel,
        grid_spec=grid_spec,
        out_shape=jax.ShapeDtypeStruct((n_blk * blk, d), F32),
        compiler_params=pltpu.CompilerParams(
            dimension_semantics=("arbitrary",), vmem_limit_bytes=VMEM_LIMIT),
        name="experts",
    )(block_expert, n_used, tok_buf, xn, w_gate_up, b_gate_up.reshape(N_EXPERTS, 1, de2),
      w_down, b_down.reshape(N_EXPERTS, 1, d))


def _combine_kernel(pos_ref, h_ref, gate_ref, gfin_ref, y_hbm, out_ref, buf, sems):
    i = pl.program_id(0)
    n = pl.num_programs(0)
    slot = lax.rem(i, 2)
    tc = buf.shape[2]

    def issue(tile, s):
        def body(r, carry):
            for k in range(TOP_K):
                row = pos_ref[(tile * tc + r) * TOP_K + k]
                pltpu.make_async_copy(y_hbm.at[pl.ds(row, 1)], buf.at[s, k, pl.ds(r, 1)],
                                      sems.at[s]).start()
            return carry
        lax.fori_loop(0, tc, body, 0, unroll=2)

    @pl.when(i == 0)
    def _():
        issue(0, 0)

    @pl.when(i + 1 < n)
    def _():
        issue(i + 1, 1 - slot)

    for k in range(TOP_K):
        pltpu.make_async_copy(y_hbm.at[pl.ds(0, tc)], buf.at[slot, k], sems.at[slot]).wait()
    gate = gate_ref[...]
    y = h_ref[...]
    for k in range(TOP_K):
        y = y + gate[:, k:k + 1] * buf[slot, k]
    out_ref[...] = _rms(y, gfin_ref[...])


def _combine(pos, h2, gates, g_final, y_rows, tc):
    t, d = h2.shape
    grid_spec = pltpu.PrefetchScalarGridSpec(
        num_scalar_prefetch=1,
        grid=(t // tc,),
        in_specs=[pl.BlockSpec((tc, d), lambda i, pos: (i, 0)),
                  pl.BlockSpec((tc, LANES), lambda i, pos: (i, 0)),
                  pl.BlockSpec((1, d), lambda i, pos: (0, 0)),
                  pl.BlockSpec(memory_space=pl.ANY)],
        out_specs=pl.BlockSpec((tc, d), lambda i, pos: (i, 0)),
        scratch_shapes=[pltpu.VMEM((2, TOP_K, tc, d), F32), pltpu.SemaphoreType.DMA((2,))],
    )
    return pl.pallas_call(
        _combine_kernel,
        grid_spec=grid_spec,
        out_shape=jax.ShapeDtypeStruct((t, d), F32),
        compiler_params=pltpu.CompilerParams(
            dimension_semantics=("arbitrary",), vmem_limit_bytes=VMEM_LIMIT),
        name="combine",
    )(pos, h2, gates, g_final, y_rows)


def _pad_heads(w, n_heads, width, offset=0):
    k = w.shape[0]
    w = w.reshape(k, n_heads, width)
    w = jnp.pad(w, ((0, 0), (0, 0), (offset, LANES - width - offset)))
    return w.reshape(k, n_heads * LANES)


def _rot_cols(w):
    half = w.shape[-1] // 2
    return jnp.concatenate([-w[..., half:], w[..., :half]], axis=-1)


def kernel(x, mem, positions, g_mix, w_in, g_q_a, w_q_b, g_kv_a, w_kv_b, rel_bias, g_out_a, g_out_b,
           w_o, g_xattn, g_mem, w_mq, w_mkv, w_mo, g_moe, w_router, b_router, w_gate_up, b_gate_up,
           w_down, b_down, g_final):
    b, s, d = x.shape
    t = b * s
    assert g_mix.shape[0] == 1, "single-layer block: the final norm is fused into the last stage"

    inv_freq = ROPE_THETA ** (-jnp.arange(0, QK_ROPE_DIM, 2, dtype=F32) / QK_ROPE_DIM)
    ang = positions.astype(F32)[..., None] * inv_freq
    cos, sin = jnp.cos(ang), jnp.sin(ang)
    pad_lo, pad_hi = QK_NOPE_DIM, LANES - QK_NOPE_DIM - QK_ROPE_DIM
    cos128 = jnp.concatenate([jnp.ones((b, s, pad_lo), F32), cos, cos, jnp.ones((b, s, pad_hi), F32)], -1)
    sin128 = jnp.concatenate([jnp.zeros((b, s, pad_lo), F32), sin, sin, jnp.zeros((b, s, pad_hi), F32)], -1)
    tbl = jnp.pad(rel_bias.T.astype(F32), ((0, 0), (0, LANES - NUM_BUCKETS)))

    h = x
    for l in range(1):
        c0 = 3 * WIDTH_A + Q_LORA_RANK + KV_LORA_RANK
        w_kpe = w_in[l][:, c0:c0 + QK_ROPE_DIM]
        place = lambda w: jnp.pad(w, ((0, 0), (QK_NOPE_DIM, LANES - QK_NOPE_DIM - QK_ROPE_DIM)))
        w_in_ext = jnp.concatenate([w_in[l][:, :c0], place(w_kpe), place(_rot_cols(w_kpe))], 1).astype(BF16)
        dq = QK_NOPE_DIM + QK_ROPE_DIM
        wq3 = w_q_b[l].reshape(Q_LORA_RANK, N_HEADS_B, dq)
        wq = _pad_heads(wq3.reshape(Q_LORA_RANK, -1), N_HEADS_B, dq).astype(BF16)
        wq_rot3 = jnp.concatenate([jnp.zeros_like(wq3[..., :QK_NOPE_DIM]), _rot_cols(wq3[..., QK_NOPE_DIM:])], -1)
        wqr = _pad_heads(wq_rot3.reshape(Q_LORA_RANK, -1), N_HEADS_B, dq).astype(BF16)
        wkv3 = w_kv_b[l].reshape(KV_LORA_RANK, N_HEADS_B, QK_NOPE_DIM + V_DIM_B)
        wk = _pad_heads(wkv3[..., :QK_NOPE_DIM].reshape(KV_LORA_RANK, -1), N_HEADS_B, QK_NOPE_DIM).astype(BF16)
        wv = _pad_heads(wkv3[..., QK_NOPE_DIM:].reshape(KV_LORA_RANK, -1), N_HEADS_B, V_DIM_B).astype(BF16)

        qa, ka, va, qm, km, vm = _proj(h, cos128, sin128, g_mix[l][None], w_in_ext, g_q_a[l][None], wq, wqr,
                                       g_kv_a[l][None], wk, wv, tm=256)
        pats = [_dilated(qa, ka, va, positions, tbl, dil) for _, dil in DILATED_PATTERNS]
        ob = _mla(qm, km, vm, tq=512, tk=512)
        kmem, vmem = _memkv(mem, g_mem[l][None], w_mkv[l].astype(BF16))

        g_out_b_pad = _pad_heads(g_out_b[l][None], N_HEADS_B, V_DIM_B)
        w_o_b_pad = _pad_heads(w_o[l][WIDTH_A:].T, N_HEADS_B, V_DIM_B).T.astype(BF16)
        w_mq_s = (w_mq[l] * ((d // N_HEADS_MEM) ** -0.5)).astype(BF16)
        w_router_pad = jnp.pad(w_router[l], ((0, 0), (0, LANES - N_EXPERTS)))
        b_router_pad = jnp.pad(b_router[l][None], ((0, 0), (0, LANES - N_EXPERTS)), constant_values=NEG_INF)
        h2, xn, logits = _post(h, [p[0] for p in pats], [p[1] for p in pats], ob, g_out_a[l][None],
                               g_out_b_pad, w_o[l][:WIDTH_A].astype(BF16), w_o_b_pad, g_xattn[l][None],
                               w_mq_s, kmem, vmem, w_mo[l].astype(BF16), g_moe[l][None], w_router_pad,
                               b_router_pad, tm=256)

        gates, eidx, rank, counts = _route(logits.reshape(t, LANES), tr=256)
        counts = counts[0, :N_EXPERTS]
        blk = EXPERT_BLOCK
        padded = ((counts + blk - 1) // blk) * blk
        ends = jnp.cumsum(padded)
        pad_start = ends - padded
        eidx4, rank4 = eidx[:, :TOP_K], rank[:, :TOP_K]
        pos = (pad_start[eidx4] + rank4).astype(jnp.int32)
        n_blk = t * TOP_K // blk + N_EXPERTS
        tok_buf = jnp.zeros((n_blk * blk,), jnp.int32).at[pos.reshape(-1)].set(
            jnp.repeat(jnp.arange(t, dtype=jnp.int32), TOP_K))
        block_expert = jnp.clip(jnp.searchsorted(ends, jnp.arange(n_blk) * blk, side='right'),
                                0, N_EXPERTS - 1).astype(jnp.int32)
        n_used = (ends[-1] // blk).astype(jnp.int32)[None]
        y_rows = _experts(block_expert, n_used, tok_buf, xn.reshape(t, d), w_gate_up[l], b_gate_up[l],
                          w_down[l], b_down[l])
        out = _combine(pos.reshape(-1), h2.reshape(t, d), gates, g_final[None], y_rows, tc=128)
        h = out.reshape(b, s, d)
    return h
```

```python
import functools
import math

import jax
import jax.numpy as jnp
from jax import lax
from jax.experimental import pallas as pl
from jax.experimental.pallas import tpu as pltpu

F32 = jnp.float32
BF16 = jnp.bfloat16

LANES = 128
EPS = 1e-6
NEG_INF = -1e30
LOG2_E = math.log2(math.e)

N_HEADS_A = 8
HEAD_DIM_A = 64
WIDTH_A = N_HEADS_A * HEAD_DIM_A
DILATED_PATTERNS = ((128, 1), (512, 4), (2048, 16))
N_HEADS_B = 8
QK_NOPE_DIM = 64
QK_ROPE_DIM = 32
V_DIM_B = 64
Q_LORA_RANK = 256
KV_LORA_RANK = 128
ROPE_THETA = 10000.0
NUM_BUCKETS = 32
MAX_DISTANCE = 1024
N_HEADS_MEM = 4
N_EXPERTS = 32
TOP_K = 4
SWIGLU_LIMIT = 7.0
SWIGLU_ALPHA = 1.702

Q_BLOCK = 128
K_WINDOW = 256
EXPERT_BLOCK = 256
DISPATCH_CHUNK = 512
VMEM_LIMIT = 56 * 1024 * 1024


def _rms(x, g):
    return x * lax.rsqrt(jnp.mean(x * x, axis=-1, keepdims=True) + EPS) * g


def _dot(a, b):
    return jnp.dot(a, b, preferred_element_type=F32)


def _dot_nt(a, b):
    return lax.dot_general(a, b, (((1,), (1,)), ((), ())), preferred_element_type=F32)


def _proj_kernel(x_ref, cos_ref, sin_ref, gmix_ref, win_ref, gq_ref, wq_ref, wqr_ref,
                 gkv_ref, wk_ref, wv_ref,
                 qa_ref, ka_ref, va_ref, qm_ref, km_ref, vm_ref):
    xn = _rms(x_ref[0], gmix_ref[...])
    z = _dot(xn.astype(BF16), win_ref[...])
    qa_ref[0] = (z[:, 0:WIDTH_A] * (HEAD_DIM_A ** -0.5)).astype(BF16)
    ka_ref[0] = z[:, WIDTH_A:2 * WIDTH_A].astype(BF16)
    va_ref[0] = z[:, 2 * WIDTH_A:3 * WIDTH_A].astype(BF16)
    c0 = 3 * WIDTH_A
    zqn = _rms(z[:, c0:c0 + Q_LORA_RANK], gq_ref[...]).astype(BF16)
    zkvn = _rms(z[:, c0 + Q_LORA_RANK:c0 + Q_LORA_RANK + KV_LORA_RANK], gkv_ref[...]).astype(BF16)
    c1 = c0 + Q_LORA_RANK + KV_LORA_RANK
    cos = cos_ref[0]
    sin = sin_ref[0]
    kpe = z[:, c1:c1 + LANES] * cos + z[:, c1 + LANES:c1 + 2 * LANES] * sin
    q = _dot(zqn, wq_ref[...])
    qr = _dot(zqn, wqr_ref[...])
    kn = _dot(zkvn, wk_ref[...])
    vv = _dot(zkvn, wv_ref[...])
    lane = lax.broadcasted_iota(jnp.int32, (1, LANES), 1)
    ones_col = jnp.where(lane == V_DIM_B, 1.0, 0.0).astype(F32)
    scale = (QK_NOPE_DIM + QK_ROPE_DIM) ** -0.5 * LOG2_E
    for h in range(N_HEADS_B):
        sl = slice(h * LANES, (h + 1) * LANES)
        qm_ref[0, h] = ((q[:, sl] * cos + qr[:, sl] * sin) * scale).astype(BF16)
        km_ref[0, h] = (kn[:, sl] + kpe).astype(BF16)
        vm_ref[0, h] = (vv[:, sl] + ones_col).astype(BF16)


def _proj(x, cos128, sin128, g_mix, w_in_ext, g_q, wq, wqr, g_kv, wk, wv, tm):
    b, s, d = x.shape
    n_in = w_in_ext.shape[1]
    hw = N_HEADS_B * LANES
    full = lambda shape: pl.BlockSpec(shape, lambda bi, i: (0,) * len(shape))
    row = lambda w: pl.BlockSpec((1, tm, w), lambda bi, i: (bi, i, 0))
    head = pl.BlockSpec((1, N_HEADS_B, tm, LANES), lambda bi, i: (bi, 0, i, 0))
    return pl.pallas_call(
        _proj_kernel,
        grid=(b, s // tm),
        in_specs=[row(d), row(LANES), row(LANES), full((1, d)), full((d, n_in)),
                  full((1, Q_LORA_RANK)), full((Q_LORA_RANK, hw)), full((Q_LORA_RANK, hw)),
                  full((1, KV_LORA_RANK)), full((KV_LORA_RANK, hw)), full((KV_LORA_RANK, hw))],
        out_specs=[row(WIDTH_A), row(WIDTH_A), row(WIDTH_A), head, head, head],
        out_shape=[jax.ShapeDtypeStruct((b, s, WIDTH_A), BF16)] * 3
        + [jax.ShapeDtypeStruct((b, N_HEADS_B, s, LANES), BF16)] * 3,
        compiler_params=pltpu.CompilerParams(
            dimension_semantics=("parallel", "parallel"), vmem_limit_bytes=VMEM_LIMIT),
        name="proj",
    )(x, cos128, sin128, g_mix, w_in_ext, g_q, wq, wqr, g_kv, wk, wv)


_LOG_BUCKET_STARTS = tuple(
    next(n for n in range(8, 4096)
         if int(math.log(n / 8) / math.log(MAX_DISTANCE / 8) * 8) >= t)
    for t in range(1, 8))


def _rel_bucket(rel):
    n = jnp.abs(rel)
    large = jnp.full(rel.shape, NUM_BUCKETS // 4, jnp.int32)
    for start in _LOG_BUCKET_STARTS:
        large = large + jnp.where(n >= start, 1, 0)
    mag = jnp.where(n < NUM_BUCKETS // 4, n, large)
    return mag + jnp.where(rel > 0, NUM_BUCKETS // 2, 0)


def _dilated_kernel(q_ref, k_ref, v_ref, qpos_ref, kpos_ref, tbl_ref, o_ref, lse_ref, *, sub_len):
    j = pl.program_id(2)
    kstart = jnp.clip(j * Q_BLOCK - (K_WINDOW - Q_BLOCK) // 2, 0, sub_len - K_WINDOW)
    kstart = pl.multiple_of(kstart, (K_WINDOW - Q_BLOCK) // 2)
    q = q_ref[0]
    kw = k_ref[0, pl.ds(kstart, K_WINDOW), :]
    vw = v_ref[0, pl.ds(kstart, K_WINDOW), :]
    rel = kpos_ref[0, 0] - qpos_ref[0]
    bucket = _rel_bucket(rel)
    qsub = j * Q_BLOCK + lax.broadcasted_iota(jnp.int32, (Q_BLOCK, 1), 0)
    ksub = kstart + lax.broadcasted_iota(jnp.int32, (1, K_WINDOW), 1)
    valid = jnp.abs(ksub - qsub) <= (K_WINDOW - Q_BLOCK) // 2
    lane = lax.broadcasted_iota(jnp.int32, (1, LANES), 1)
    first = lane < HEAD_DIM_A
    for p in range(N_HEADS_A // 2):
        sl = slice(p * LANES, (p + 1) * LANES)
        qp, kp, vp = q[:, sl], kw[:, sl], vw[:, sl]
        outs, lses = [], []
        for hh in range(2):
            h = 2 * p + hh
            qh = jnp.where(first if hh == 0 else jnp.logical_not(first), qp, jnp.zeros_like(qp))
            s = _dot_nt(qh, kp)
            tbl = jnp.broadcast_to(tbl_ref[h:h + 1, :], (Q_BLOCK, LANES))
            bias = jnp.concatenate(
                [jnp.take_along_axis(tbl, bucket[:, c * LANES:(c + 1) * LANES], axis=1)
                 for c in range(K_WINDOW // LANES)], axis=1)
            s = jnp.where(valid, s + bias, NEG_INF)
            m = jnp.max(s, axis=1, keepdims=True)
            e = jnp.exp(s - m)
            l = jnp.sum(e, axis=1, keepdims=True)
            outs.append(_dot(e.astype(BF16), vp) / l)
            lses.append(m + jnp.log(l))
        o_ref[0, :, sl] = jnp.where(first, outs[0], outs[1])
        lse_ref[0, :, sl] = jnp.where(first, lses[0], lses[1])


def _dilated(qa, ka, va, positions, tbl, dil):
    b, s, w = qa.shape
    sub_len = s // dil
    nq = sub_len // Q_BLOCK
    view = lambda t: t.reshape(b, sub_len, dil * w)
    pos_t = positions.reshape(b, sub_len, dil).transpose(0, 2, 1).reshape(b * dil, sub_len)
    half = (K_WINDOW - Q_BLOCK) // 2
    starts = jnp.clip(jnp.arange(nq) * Q_BLOCK - half, 0, sub_len - K_WINDOW)
    win = starts[:, None] + jnp.arange(K_WINDOW)[None, :]
    kpos = pos_t[:, win].reshape(b * dil, nq, 1, K_WINDOW)
    qpos = pos_t.reshape(b * dil, sub_len, 1)
    qspec = pl.BlockSpec((1, Q_BLOCK, w), lambda bi, r, j: (bi, j, r))
    kvspec = pl.BlockSpec((1, sub_len, w), lambda bi, r, j: (bi, 0, r))
    o, lse = pl.pallas_call(
        functools.partial(_dilated_kernel, sub_len=sub_len),
        grid=(b, dil, nq),
        in_specs=[qspec, kvspec, kvspec,
                  pl.BlockSpec((1, Q_BLOCK, 1), lambda bi, r, j: (bi * dil + r, j, 0)),
                  pl.BlockSpec((1, 1, 1, K_WINDOW), lambda bi, r, j: (bi * dil + r, j, 0, 0)),
                  pl.BlockSpec((N_HEADS_A, LANES), lambda bi, r, j: (0, 0))],
        out_specs=[qspec, qspec],
        out_shape=[jax.ShapeDtypeStruct((b, sub_len, dil * w), F32)] * 2,
        compiler_params=pltpu.CompilerParams(
            dimension_semantics=("parallel", "parallel", "arbitrary"),
            vmem_limit_bytes=VMEM_LIMIT),
        name=f"dilated{dil}",
    )(view(qa), view(ka), view(va), qpos, kpos, tbl)
    return o.reshape(b, s, w), lse.reshape(b, s, w)


def _mla_kernel(q_ref, k_ref, v_ref, o_ref, *, tk):
    q = q_ref[0, 0]
    tq = q.shape[0]
    n_kv = k_ref.shape[2] // tk

    def body(i, carry):
        m, acc = carry
        start = pl.multiple_of(i * tk, tk)
        k = k_ref[0, 0, pl.ds(start, tk), :]
        v = v_ref[0, 0, pl.ds(start, tk), :]
        s = _dot_nt(q, k)
        m_new = jnp.maximum(m, jnp.max(s, axis=1, keepdims=True))
        p = jnp.exp2(s - m_new)
        acc = jnp.exp2(m - m_new) * acc + _dot(p.astype(BF16), v)
        return m_new, acc

    m0 = jnp.full((tq, 1), NEG_INF, F32)
    _, acc = lax.fori_loop(0, n_kv, body, (m0, jnp.zeros((tq, LANES), F32)), unroll=True)
    lane = lax.broadcasted_iota(jnp.int32, (1, LANES), 1)
    denom = jnp.sum(jnp.where(lane == V_DIM_B, acc, 0.0), axis=1, keepdims=True)
    o_ref[0, 0] = jnp.where(lane < V_DIM_B, acc / denom, 0.0).astype(BF16)


def _mla(qm, km, vm, tq, tk):
    b, nh, s, _ = qm.shape
    qspec = pl.BlockSpec((1, 1, tq, LANES), lambda bi, h, i: (bi, h, i, 0))
    kvspec = pl.BlockSpec((1, 1, s, LANES), lambda bi, h, i: (bi, h, 0, 0))
    return pl.pallas_call(
        functools.partial(_mla_kernel, tk=tk),
        grid=(b, nh, s // tq),
        in_specs=[qspec, kvspec, kvspec],
        out_specs=qspec,
        out_shape=jax.ShapeDtypeStruct((b, nh, s, LANES), BF16),
        compiler_params=pltpu.CompilerParams(
            dimension_semantics=("parallel", "parallel", "arbitrary"),
            vmem_limit_bytes=VMEM_LIMIT),
        name="mla",
    )(qm, km, vm)


def _memkv_kernel(mem_ref, g_ref, w_ref, k_ref, v_ref):
    d = mem_ref.shape[2]
    kv = _dot(_rms(mem_ref[0], g_ref[...]).astype(BF16), w_ref[...])
    k_ref[0] = kv[:, :d].astype(BF16)
    v_ref[0] = kv[:, d:].astype(BF16)


def _memkv(mem, g_mem, w_mkv):
    b, m, d = mem.shape
    spec = pl.BlockSpec((1, m, d), lambda bi: (bi, 0, 0))
    return pl.pallas_call(
        _memkv_kernel,
        grid=(b,),
        in_specs=[spec, pl.BlockSpec((1, d), lambda bi: (0, 0)),
                  pl.BlockSpec((d, 2 * d), lambda bi: (0, 0))],
        out_specs=[spec, spec],
        out_shape=[jax.ShapeDtypeStruct((b, m, d), BF16)] * 2,
        compiler_params=pltpu.CompilerParams(
            dimension_semantics=("parallel",), vmem_limit_bytes=VMEM_LIMIT),
        name="memkv",
    )(mem, g_mem, w_mkv)


def _post_kernel(x_ref, o1_ref, o2_ref, o3_ref, l1_ref, l2_ref, l3_ref, ob_ref,
                 goa_ref, gob_ref, woa_ref, wob_ref, gx_ref, wmq_ref, km_ref, vm_ref, wmo_ref,
                 gmoe_ref, wr_ref, br_ref,
                 h_ref, xn_ref, logit_ref):
    l1, l2, l3 = l1_ref[0], l2_ref[0], l3_ref[0]
    mx = jnp.maximum(jnp.maximum(l1, l2), l3)
    w1, w2, w3 = jnp.exp(l1 - mx), jnp.exp(l2 - mx), jnp.exp(l3 - mx)
    oa = (w1 * o1_ref[0] + w2 * o2_ref[0] + w3 * o3_ref[0]) / (w1 + w2 + w3)
    oan = _rms(oa, goa_ref[...]).astype(BF16)
    ob = jnp.concatenate([ob_ref[0, h] for h in range(N_HEADS_B)], axis=1).astype(F32)
    ms_b = jnp.sum(ob * ob, axis=1, keepdims=True) * (1.0 / (N_HEADS_B * V_DIM_B))
    obn = (ob * lax.rsqrt(ms_b + EPS) * gob_ref[...]).astype(BF16)
    h1 = x_ref[0] + _dot(oan, woa_ref[...]) + _dot(obn, wob_ref[...])

    hn = _rms(h1, gx_ref[...]).astype(BF16)
    q = _dot(hn, wmq_ref[...]).astype(BF16)
    dh = q.shape[1] // N_HEADS_MEM
    heads = []
    for h in range(N_HEADS_MEM):
        sl = slice(h * dh, (h + 1) * dh)
        s = _dot_nt(q[:, sl], km_ref[0, :, sl])
        e = jnp.exp(s - jnp.max(s, axis=1, keepdims=True))
        p = e / jnp.sum(e, axis=1, keepdims=True)
        heads.append(_dot(p.astype(BF16), vm_ref[0, :, sl]))
    o = jnp.concatenate(heads, axis=1).astype(BF16)
    h2 = h1 + _dot(o, wmo_ref[...])
    h_ref[0] = h2

    xn = _rms(h2, gmoe_ref[...])
    for c in range(xn_ref.shape[1]):
        xn_ref[:, c, :] = xn[:, c * LANES:(c + 1) * LANES]
    logit_ref[0] = jnp.dot(xn, wr_ref[...], preferred_element_type=F32,
                           precision=lax.Precision.HIGHEST) + br_ref[...]


def _post(x, o_pats, lse_pats, ob, g_out_a, g_out_b_pad, w_o_a, w_o_b_pad, g_xattn, w_mq, kmem,
          vmem, w_mo, g_moe, w_router_pad, b_router_pad, tm):
    b, s, d = x.shape
    m = kmem.shape[1]
    full = lambda shape: pl.BlockSpec(shape, lambda bi, i: (0,) * len(shape))
    row = lambda w: pl.BlockSpec((1, tm, w), lambda bi, i: (bi, i, 0))
    memspec = pl.BlockSpec((1, m, d), lambda bi, i: (bi, 0, 0))
    hw = N_HEADS_B * LANES
    return pl.pallas_call(
        _post_kernel,
        grid=(b, s // tm),
        in_specs=[row(d)] + [row(WIDTH_A)] * 6
        + [pl.BlockSpec((1, N_HEADS_B, tm, LANES), lambda bi, i: (bi, 0, i, 0)),
           full((1, WIDTH_A)), full((1, hw)), full((WIDTH_A, d)), full((hw, d)),
           full((1, d)), full((d, d)), memspec, memspec, full((d, d)),
           full((1, d)), full((d, LANES)), full((1, LANES))],
        out_specs=[row(d), pl.BlockSpec((tm, d // LANES, LANES), lambda bi, i: (bi * (s // tm) + i, 0, 0)),
                   row(LANES)],
        out_shape=[jax.ShapeDtypeStruct((b, s, d), F32),
                   jax.ShapeDtypeStruct((b * s, d // LANES, LANES), F32),
                   jax.ShapeDtypeStruct((b, s, LANES), F32)],
        compiler_params=pltpu.CompilerParams(
            dimension_semantics=("parallel", "parallel"), vmem_limit_bytes=VMEM_LIMIT),
        name="post",
    )(x, *o_pats, *lse_pats, ob, g_out_a, g_out_b_pad, w_o_a, w_o_b_pad, g_xattn, w_mq, kmem,
      vmem, w_mo, g_moe, w_router_pad, b_router_pad)


def _route_kernel(logit_ref, gate_ref, eidx_ref, rank_ref, count_ref, carry_ref):
    @pl.when(pl.program_id(0) == 0)
    def _():
        carry_ref[...] = jnp.zeros_like(carry_ref)

    l = logit_ref[...]
    tr = l.shape[0]
    lane_i = lax.broadcasted_iota(jnp.int32, l.shape, 1)
    lane = lane_i.astype(F32)
    vals, idxs = [], []
    for _ in range(TOP_K):
        m = jnp.max(l, axis=1, keepdims=True)
        idx = jnp.min(jnp.where(l == m, lane, float(LANES)), axis=1, keepdims=True)
        vals.append(m)
        idxs.append(idx)
        l = jnp.where(lane == idx, -jnp.inf, l)
    exps = [jnp.exp(v - vals[0]) for v in vals]
    denom = exps[0] + exps[1] + exps[2] + exps[3]
    onehot = jnp.zeros(l.shape, F32)
    for idx in idxs:
        onehot = onehot + jnp.where(lane == idx, 1.0, 0.0)
    r = lax.broadcasted_iota(jnp.int32, (tr, tr), 0)
    c = lax.broadcasted_iota(jnp.int32, (tr, tr), 1)
    tri = jnp.where(c < r, 1.0, 0.0).astype(BF16)
    before = _dot(tri, onehot.astype(BF16)) + carry_ref[...]
    gate = jnp.zeros(l.shape, F32)
    eidx = jnp.zeros(l.shape, jnp.int32)
    rank = jnp.zeros(l.shape, jnp.int32)
    for k in range(TOP_K):
        rk = jnp.sum(jnp.where(lane == idxs[k], before, 0.0), axis=1, keepdims=True)
        gate = jnp.where(lane_i == k, exps[k] / denom, gate)
        eidx = jnp.where(lane_i == k, idxs[k].astype(jnp.int32), eidx)
        rank = jnp.where(lane_i == k, rk.astype(jnp.int32), rank)
    gate_ref[...] = gate
    eidx_ref[...] = eidx
    rank_ref[...] = rank
    total = carry_ref[...] + jnp.sum(onehot, axis=0, keepdims=True)
    carry_ref[...] = total
    count_ref[...] = total.astype(jnp.int32)


def _route(logits, tr):
    t = logits.shape[0]
    spec = pl.BlockSpec((tr, LANES), lambda i: (i, 0))
    return pl.pallas_call(
        _route_kernel,
        grid=(t // tr,),
        in_specs=[spec],
        out_specs=[spec, spec, spec, pl.BlockSpec((1, LANES), lambda i: (0, 0))],
        out_shape=[jax.ShapeDtypeStruct((t, LANES), F32), jax.ShapeDtypeStruct((t, LANES), jnp.int32),
                   jax.ShapeDtypeStruct((t, LANES), jnp.int32),
                   jax.ShapeDtypeStruct((1, LANES), jnp.int32)],
        scratch_shapes=[pltpu.VMEM((1, LANES), F32)],
        compiler_params=pltpu.CompilerParams(
            dimension_semantics=("arbitrary",), vmem_limit_bytes=VMEM_LIMIT),
        name="route",
    )(logits)


def _dispatch_kernel(pos_ref, ends_ref, x_hbm, xs_hbm, zbuf, sem, zsem):
    c = pl.program_id(0)
    blk = zbuf.shape[0]
    n_rows = DISPATCH_CHUNK * TOP_K

    @pl.when(c == 0)
    def _():
        zbuf[...] = jnp.zeros_like(zbuf)

        def zero_copy(start):
            return pltpu.make_async_copy(zbuf, xs_hbm.at[pl.ds(start, blk)], zsem)

        used_end = ends_ref[N_EXPERTS]
        starts = [(ends_ref[e + 1] > ends_ref[e], ends_ref[e + 1] - blk) for e in range(N_EXPERTS)]
        starts += [(used_end + (j + 1) * blk <= xs_hbm.shape[0], used_end + j * blk)
                   for j in range(N_EXPERTS)]
        for cond, start in starts:
            @pl.when(cond)
            def _():
                zero_copy(start).start()
        for cond, start in starts:
            @pl.when(cond)
            def _():
                zero_copy(start).wait()

    def chunk_wait():
        pltpu.make_async_copy(x_hbm.at[pl.ds(0, n_rows)], xs_hbm.at[pl.ds(0, n_rows)], sem).wait()

    def body(t, carry):
        tok = c * DISPATCH_CHUNK + t
        for k in range(TOP_K):
            pltpu.make_async_copy(x_hbm.at[tok], xs_hbm.at[pos_ref[tok * TOP_K + k]], sem).start()
        return carry
    lax.fori_loop(0, DISPATCH_CHUNK, body, 0, unroll=4)

    @pl.when(c > 0)
    def _():
        chunk_wait()

    @pl.when(c == pl.num_programs(0) - 1)
    def _():
        chunk_wait()


def _dispatch(pos, ends0, xn3, n_rows):
    grid_spec = pltpu.PrefetchScalarGridSpec(
        num_scalar_prefetch=2,
        grid=(xn3.shape[0] // DISPATCH_CHUNK,),
        in_specs=[pl.BlockSpec(memory_space=pl.ANY)],
        out_specs=pl.BlockSpec(memory_space=pl.ANY),
        scratch_shapes=[pltpu.VMEM((EXPERT_BLOCK,) + xn3.shape[1:], F32),
                        pltpu.SemaphoreType.DMA(()), pltpu.SemaphoreType.DMA(())],
    )
    return pl.pallas_call(
        _dispatch_kernel,
        grid_spec=grid_spec,
        out_shape=jax.ShapeDtypeStruct((n_rows,) + xn3.shape[1:], F32),
        compiler_params=pltpu.CompilerParams(
            dimension_semantics=("arbitrary",), vmem_limit_bytes=VMEM_LIMIT),
        name="dispatch",
    )(pos, ends0, xn3)


def _expert_kernel(be_ref, nused_ref, xs_ref, wgu_ref, bgu_ref, wd_ref, bd_ref, y_ref, wgu_bf, wd_bf):
    i = pl.program_id(0)
    n_chunk = xs_ref.shape[1]

    @pl.when(i < nused_ref[0])
    def _():
        @pl.when(jnp.logical_or(i == 0, be_ref[i] != be_ref[jnp.maximum(i - 1, 0)]))
        def _():
            wgu_bf[...] = wgu_ref[0].astype(BF16)
            wd_bf[...] = wd_ref[0].astype(BF16)

        x = jnp.concatenate([xs_ref[:, c, :] for c in range(n_chunk)], axis=1).astype(BF16)
        gu = _dot(x, wgu_bf[...]) + bgu_ref[0]
        de = gu.shape[1] // 2
        gate = jnp.minimum(gu[:, :de], SWIGLU_LIMIT)
        up = jnp.clip(gu[:, de:], -SWIGLU_LIMIT, SWIGLU_LIMIT)
        hmid = (up + 1.0) * (gate * jax.nn.sigmoid(SWIGLU_ALPHA * gate))
        out = _dot(hmid.astype(BF16), wd_bf[...]) + bd_ref[0]
        for c in range(n_chunk):
            y_ref[:, c, :] = out[:, c * LANES:(c + 1) * LANES]

    @pl.when(i >= nused_ref[0])
    def _():
        y_ref[...] = jnp.zeros_like(y_ref)


def _experts(block_expert, n_used, xs3, w_gate_up, b_gate_up, w_down, b_down):
    n_rows, n_chunk, _ = xs3.shape
    d = n_chunk * LANES
    blk = EXPERT_BLOCK
    n_blk = n_rows // blk
    de2 = w_gate_up.shape[2]
    grid_spec = pltpu.PrefetchScalarGridSpec(
        num_scalar_prefetch=2,
        grid=(n_blk,),
        in_specs=[pl.BlockSpec((blk, n_chunk, LANES),
                               lambda i, be, nu: (jnp.minimum(i, nu[0] - 1), 0, 0)),
                  pl.BlockSpec((1, d, de2), lambda i, be, nu: (be[i], 0, 0)),
                  pl.BlockSpec((1, 1, de2), lambda i, be, nu: (be[i], 0, 0)),
                  pl.BlockSpec((1, de2 // 2, d), lambda i, be, nu: (be[i], 0, 0)),
                  pl.BlockSpec((1, 1, d), lambda i, be, nu: (be[i], 0, 0))],
        out_specs=pl.BlockSpec((blk, n_chunk, LANES), lambda i, be, nu: (i, 0, 0)),
        scratch_shapes=[pltpu.VMEM((d, de2), BF16), pltpu.VMEM((de2 // 2, d), BF16)],
    )
    return pl.pallas_call(
        _expert_kernel,
        grid_spec=grid_spec,
        out_shape=jax.ShapeDtypeStruct((n_rows, n_chunk, LANES), F32),
        compiler_params=pltpu.CompilerParams(
            dimension_semantics=("arbitrary",), vmem_limit_bytes=VMEM_LIMIT),
        name="experts",
    )(block_expert, n_used, xs3, w_gate_up, b_gate_up.reshape(N_EXPERTS, 1, de2),
      w_down, b_down.reshape(N_EXPERTS, 1, d))


def _combine_kernel(pos_ref, h_ref, gate_ref, gfin_ref, y_hbm, out_ref, buf, sems):
    i = pl.program_id(0)
    n = pl.num_programs(0)
    slot = lax.rem(i, 2)
    tc, n_chunk = buf.shape[2], buf.shape[3]

    def issue(tile, s):
        def body(r, carry):
            for k in range(TOP_K):
                row = pos_ref[(tile * tc + r) * TOP_K + k]
                pltpu.make_async_copy(y_hbm.at[row], buf.at[s, k, r], sems.at[s]).start()
            return carry
        lax.fori_loop(0, tc, body, 0, unroll=4)

    @pl.when(i == 0)
    def _():
        issue(0, 0)

    @pl.when(i + 1 < n)
    def _():
        issue(i + 1, 1 - slot)

    for k in range(TOP_K):
        pltpu.make_async_copy(y_hbm.at[pl.ds(0, tc)], buf.at[slot, k], sems.at[slot]).wait()
    gate = gate_ref[...]
    pieces = []
    for c in range(n_chunk):
        y = h_ref[:, c * LANES:(c + 1) * LANES]
        for k in range(TOP_K):
            y = y + gate[:, k:k + 1] * buf[slot, k, :, c, :]
        pieces.append(y)
    out_ref[...] = _rms(jnp.concatenate(pieces, axis=1), gfin_ref[...])


def _combine(pos, h2, gates, g_final, y3, tc):
    t, d = h2.shape
    n_chunk = y3.shape[1]
    grid_spec = pltpu.PrefetchScalarGridSpec(
        num_scalar_prefetch=1,
        grid=(t // tc,),
        in_specs=[pl.BlockSpec((tc, d), lambda i, pos: (i, 0)),
                  pl.BlockSpec((tc, LANES), lambda i, pos: (i, 0)),
                  pl.BlockSpec((1, d), lambda i, pos: (0, 0)),
                  pl.BlockSpec(memory_space=pl.ANY)],
        out_specs=pl.BlockSpec((tc, d), lambda i, pos: (i, 0)),
        scratch_shapes=[pltpu.VMEM((2, TOP_K, tc, n_chunk, LANES), F32),
                        pltpu.SemaphoreType.DMA((2,))],
    )
    return pl.pallas_call(
        _combine_kernel,
        grid_spec=grid_spec,
        out_shape=jax.ShapeDtypeStruct((t, d), F32),
        compiler_params=pltpu.CompilerParams(
            dimension_semantics=("arbitrary",), vmem_limit_bytes=VMEM_LIMIT),
        name="combine",
    )(pos, h2, gates, g_final, y3)


def _pad_heads(w, n_heads, width, offset=0):
    k = w.shape[0]
    w = w.reshape(k, n_heads, width)
    w = jnp.pad(w, ((0, 0), (0, 0), (offset, LANES - width - offset)))
    return w.reshape(k, n_heads * LANES)


def _rot_cols(w):
    half = w.shape[-1] // 2
    return jnp.concatenate([-w[..., half:], w[..., :half]], axis=-1)


def kernel(x, mem, positions, g_mix, w_in, g_q_a, w_q_b, g_kv_a, w_kv_b, rel_bias, g_out_a, g_out_b,
           w_o, g_xattn, g_mem, w_mq, w_mkv, w_mo, g_moe, w_router, b_router, w_gate_up, b_gate_up,
           w_down, b_down, g_final):
    b, s, d = x.shape
    t = b * s
    assert g_mix.shape[0] == 1, "single-layer block: the final norm is fused into the last stage"

    inv_freq = ROPE_THETA ** (-jnp.arange(0, QK_ROPE_DIM, 2, dtype=F32) / QK_ROPE_DIM)
    ang = positions.astype(F32)[..., None] * inv_freq
    cos, sin = jnp.cos(ang), jnp.sin(ang)
    pad_lo, pad_hi = QK_NOPE_DIM, LANES - QK_NOPE_DIM - QK_ROPE_DIM
    cos128 = jnp.concatenate([jnp.ones((b, s, pad_lo), F32), cos, cos, jnp.ones((b, s, pad_hi), F32)], -1)
    sin128 = jnp.concatenate([jnp.zeros((b, s, pad_lo), F32), sin, sin, jnp.zeros((b, s, pad_hi), F32)], -1)
    tbl = jnp.pad(rel_bias.T.astype(F32), ((0, 0), (0, LANES - NUM_BUCKETS)))

    h = x
    for l in range(1):
        c0 = 3 * WIDTH_A + Q_LORA_RANK + KV_LORA_RANK
        w_kpe = w_in[l][:, c0:c0 + QK_ROPE_DIM]
        place = lambda w: jnp.pad(w, ((0, 0), (QK_NOPE_DIM, LANES - QK_NOPE_DIM - QK_ROPE_DIM)))
        w_in_ext = jnp.concatenate([w_in[l][:, :c0], place(w_kpe), place(_rot_cols(w_kpe))], 1).astype(BF16)
        dq = QK_NOPE_DIM + QK_ROPE_DIM
        wq3 = w_q_b[l].reshape(Q_LORA_RANK, N_HEADS_B, dq)
        wq = _pad_heads(wq3.reshape(Q_LORA_RANK, -1), N_HEADS_B, dq).astype(BF16)
        wq_rot3 = jnp.concatenate([jnp.zeros_like(wq3[..., :QK_NOPE_DIM]), _rot_cols(wq3[..., QK_NOPE_DIM:])], -1)
        wqr = _pad_heads(wq_rot3.reshape(Q_LORA_RANK, -1), N_HEADS_B, dq).astype(BF16)
        wkv3 = w_kv_b[l].reshape(KV_LORA_RANK, N_HEADS_B, QK_NOPE_DIM + V_DIM_B)
        wk = _pad_heads(wkv3[..., :QK_NOPE_DIM].reshape(KV_LORA_RANK, -1), N_HEADS_B, QK_NOPE_DIM).astype(BF16)
        wv = _pad_heads(wkv3[..., QK_NOPE_DIM:].reshape(KV_LORA_RANK, -1), N_HEADS_B, V_DIM_B).astype(BF16)

        qa, ka, va, qm, km, vm = _proj(h, cos128, sin128, g_mix[l][None], w_in_ext, g_q_a[l][None], wq, wqr,
                                       g_kv_a[l][None], wk, wv, tm=256)
        pats = [_dilated(qa, ka, va, positions, tbl, dil) for _, dil in DILATED_PATTERNS]
        ob = _mla(qm, km, vm, tq=1024, tk=512)
        kmem, vmem = _memkv(mem, g_mem[l][None], w_mkv[l].astype(BF16))

        g_out_b_pad = _pad_heads(g_out_b[l][None], N_HEADS_B, V_DIM_B)
        w_o_b_pad = _pad_heads(w_o[l][WIDTH_A:].T, N_HEADS_B, V_DIM_B).T.astype(BF16)
        w_mq_s = (w_mq[l] * ((d // N_HEADS_MEM) ** -0.5)).astype(BF16)
        w_router_pad = jnp.pad(w_router[l], ((0, 0), (0, LANES - N_EXPERTS)))
        b_router_pad = jnp.pad(b_router[l][None], ((0, 0), (0, LANES - N_EXPERTS)), constant_values=NEG_INF)
        h2, xn3, logits = _post(h, [p[0] for p in pats], [p[1] for p in pats], ob, g_out_a[l][None],
                               g_out_b_pad, w_o[l][:WIDTH_A].astype(BF16), w_o_b_pad, g_xattn[l][None],
                               w_mq_s, kmem, vmem, w_mo[l].astype(BF16), g_moe[l][None], w_router_pad,
                               b_router_pad, tm=256)

        gates, eidx, rank, counts = _route(logits.reshape(t, LANES), tr=256)
        counts = counts[0, :N_EXPERTS]
        blk = EXPERT_BLOCK
        padded = ((counts + blk - 1) // blk) * blk
        ends = jnp.cumsum(padded)
        pad_start = ends - padded
        eidx4, rank4 = eidx[:, :TOP_K], rank[:, :TOP_K]
        expert_ids = jnp.arange(N_EXPERTS, dtype=jnp.int32)
        pos = (jnp.sum(jnp.where(eidx4[..., None] == expert_ids, pad_start, 0), axis=-1)
               + rank4).astype(jnp.int32).reshape(-1)
        n_blk = t * TOP_K // blk + N_EXPERTS
        block_expert = jnp.minimum(
            jnp.sum(ends[None, :] <= (jnp.arange(n_blk) * blk)[:, None], axis=1),
            N_EXPERTS - 1).astype(jnp.int32)
        n_used = (ends[-1] // blk).astype(jnp.int32)[None]
        ends0 = jnp.concatenate([jnp.zeros((1,), jnp.int32), ends.astype(jnp.int32)])
        xs3 = _dispatch(pos, ends0, xn3, n_blk * blk)
        y3 = _experts(block_expert, n_used, xs3, w_gate_up[l], b_gate_up[l], w_down[l], b_down[l])
        out = _combine(pos, h2.reshape(t, d), gates, g_final[None], y3, tc=128)
        h = out.reshape(b, s, d)
    return h
```

```python
import functools
import math

import jax
import jax.numpy as jnp
from jax import lax
from jax.experimental import pallas as pl
from jax.experimental.pallas import tpu as pltpu

F32 = jnp.float32
BF16 = jnp.bfloat16

LANES = 128
EPS = 1e-6
NEG_INF = -1e30
LOG2_E = math.log2(math.e)

N_HEADS_A = 8
HEAD_DIM_A = 64
WIDTH_A = N_HEADS_A * HEAD_DIM_A
DILATED_PATTERNS = ((128, 1), (512, 4), (2048, 16))
N_HEADS_B = 8
QK_NOPE_DIM = 64
QK_ROPE_DIM = 32
V_DIM_B = 64
Q_LORA_RANK = 256
KV_LORA_RANK = 128
ROPE_THETA = 10000.0
NUM_BUCKETS = 32
MAX_DISTANCE = 1024
N_HEADS_MEM = 4
N_EXPERTS = 32
TOP_K = 4
SWIGLU_LIMIT = 7.0
SWIGLU_ALPHA = 1.702

Q_BLOCK = 128
K_WINDOW = 256
EXPERT_BLOCK = 256
DISPATCH_CHUNK = 512
VMEM_LIMIT = 56 * 1024 * 1024


def _rms(x, g):
    return x * lax.rsqrt(jnp.mean(x * x, axis=-1, keepdims=True) + EPS) * g


def _dot(a, b):
    return jnp.dot(a, b, preferred_element_type=F32)


def _dot_nt(a, b):
    return lax.dot_general(a, b, (((1,), (1,)), ((), ())), preferred_element_type=F32)


def _proj_kernel(x_ref, cos_ref, sin_ref, gmix_ref, win_ref, gq_ref, wq_ref, wqr_ref,
                 gkv_ref, wk_ref, wv_ref,
                 qa_ref, ka_ref, va_ref, qm_ref, km_ref, vm_ref):
    xn = _rms(x_ref[0], gmix_ref[...])
    z = _dot(xn.astype(BF16), win_ref[...])
    qa_ref[0] = (z[:, 0:WIDTH_A] * (HEAD_DIM_A ** -0.5)).astype(BF16)
    ka_ref[0] = z[:, WIDTH_A:2 * WIDTH_A].astype(BF16)
    va_ref[0] = z[:, 2 * WIDTH_A:3 * WIDTH_A].astype(BF16)
    c0 = 3 * WIDTH_A
    zqn = _rms(z[:, c0:c0 + Q_LORA_RANK], gq_ref[...]).astype(BF16)
    zkvn = _rms(z[:, c0 + Q_LORA_RANK:c0 + Q_LORA_RANK + KV_LORA_RANK], gkv_ref[...]).astype(BF16)
    c1 = c0 + Q_LORA_RANK + KV_LORA_RANK
    cos = cos_ref[0]
    sin = sin_ref[0]
    kpe = z[:, c1:c1 + LANES] * cos + z[:, c1 + LANES:c1 + 2 * LANES] * sin
    q = _dot(zqn, wq_ref[...])
    qr = _dot(zqn, wqr_ref[...])
    kn = _dot(zkvn, wk_ref[...])
    vv = _dot(zkvn, wv_ref[...])
    lane = lax.broadcasted_iota(jnp.int32, (1, LANES), 1)
    ones_col = jnp.where(lane == V_DIM_B, 1.0, 0.0).astype(F32)
    scale = (QK_NOPE_DIM + QK_ROPE_DIM) ** -0.5 * LOG2_E
    for h in range(N_HEADS_B):
        sl = slice(h * LANES, (h + 1) * LANES)
        qm_ref[0, h] = ((q[:, sl] * cos + qr[:, sl] * sin) * scale).astype(BF16)
        km_ref[0, h] = (kn[:, sl] + kpe).astype(BF16)
        vm_ref[0, h] = (vv[:, sl] + ones_col).astype(BF16)


def _proj(x, cos128, sin128, g_mix, w_in_ext, g_q, wq, wqr, g_kv, wk, wv, tm):
    b, s, d = x.shape
    n_in = w_in_ext.shape[1]
    hw = N_HEADS_B * LANES
    full = lambda shape: pl.BlockSpec(shape, lambda bi, i: (0,) * len(shape))
    row = lambda w: pl.BlockSpec((1, tm, w), lambda bi, i: (bi, i, 0))
    head = pl.BlockSpec((1, N_HEADS_B, tm, LANES), lambda bi, i: (bi, 0, i, 0))
    return pl.pallas_call(
        _proj_kernel,
        grid=(b, s // tm),
        in_specs=[row(d), row(LANES), row(LANES), full((1, d)), full((d, n_in)),
                  full((1, Q_LORA_RANK)), full((Q_LORA_RANK, hw)), full((Q_LORA_RANK, hw)),
                  full((1, KV_LORA_RANK)), full((KV_LORA_RANK, hw)), full((KV_LORA_RANK, hw))],
        out_specs=[row(WIDTH_A), row(WIDTH_A), row(WIDTH_A), head, head, head],
        out_shape=[jax.ShapeDtypeStruct((b, s, WIDTH_A), BF16)] * 3
        + [jax.ShapeDtypeStruct((b, N_HEADS_B, s, LANES), BF16)] * 3,
        compiler_params=pltpu.CompilerParams(
            dimension_semantics=("parallel", "parallel"), vmem_limit_bytes=VMEM_LIMIT),
        name="proj",
    )(x, cos128, sin128, g_mix, w_in_ext, g_q, wq, wqr, g_kv, wk, wv)


_LOG_BUCKET_STARTS = tuple(
    next(n for n in range(8, 4096)
         if int(math.log(n / 8) / math.log(MAX_DISTANCE / 8) * 8) >= t)
    for t in range(1, 8))


def _rel_bucket(rel):
    n = jnp.abs(rel)
    large = jnp.full(rel.shape, NUM_BUCKETS // 4, jnp.int32)
    for start in _LOG_BUCKET_STARTS:
        large = large + jnp.where(n >= start, 1, 0)
    mag = jnp.where(n < NUM_BUCKETS // 4, n, large)
    return mag + jnp.where(rel > 0, NUM_BUCKETS // 2, 0)


def _dilated_kernel(q_ref, k_ref, v_ref, qpos_ref, kpos_ref, tbl_ref, o_ref, lse_ref, *, sub_len):
    j = pl.program_id(2)
    kstart = jnp.clip(j * Q_BLOCK - (K_WINDOW - Q_BLOCK) // 2, 0, sub_len - K_WINDOW)
    kstart = pl.multiple_of(kstart, (K_WINDOW - Q_BLOCK) // 2)
    q = q_ref[0]
    kw = k_ref[0, pl.ds(kstart, K_WINDOW), :]
    vw = v_ref[0, pl.ds(kstart, K_WINDOW), :]
    rel = kpos_ref[0, 0] - qpos_ref[0]
    bucket = _rel_bucket(rel)
    qsub = j * Q_BLOCK + lax.broadcasted_iota(jnp.int32, (Q_BLOCK, 1), 0)
    ksub = kstart + lax.broadcasted_iota(jnp.int32, (1, K_WINDOW), 1)
    valid = jnp.abs(ksub - qsub) <= (K_WINDOW - Q_BLOCK) // 2
    lane = lax.broadcasted_iota(jnp.int32, (1, LANES), 1)
    first = lane < HEAD_DIM_A
    for p in range(N_HEADS_A // 2):
        sl = slice(p * LANES, (p + 1) * LANES)
        qp, kp, vp = q[:, sl], kw[:, sl], vw[:, sl]
        outs, lses = [], []
        for hh in range(2):
            h = 2 * p + hh
            qh = jnp.where(first if hh == 0 else jnp.logical_not(first), qp, jnp.zeros_like(qp))
            s = _dot_nt(qh, kp)
            tbl = jnp.broadcast_to(tbl_ref[h:h + 1, :], (Q_BLOCK, LANES))
            bias = jnp.concatenate(
                [jnp.take_along_axis(tbl, bucket[:, c * LANES:(c + 1) * LANES], axis=1)
                 for c in range(K_WINDOW // LANES)], axis=1)
            s = jnp.where(valid, s + bias, NEG_INF)
            m = jnp.max(s, axis=1, keepdims=True)
            e = jnp.exp(s - m)
            l = jnp.sum(e, axis=1, keepdims=True)
            outs.append(_dot(e.astype(BF16), vp) / l)
            lses.append(m + jnp.log(l))
        o_ref[0, :, sl] = jnp.where(first, outs[0], outs[1])
        lse_ref[0, :, sl] = jnp.where(first, lses[0], lses[1])


def _dilated(qa, ka, va, positions, tbl, dil):
    b, s, w = qa.shape
    sub_len = s // dil
    nq = sub_len // Q_BLOCK
    view = lambda t: t.reshape(b, sub_len, dil * w)
    pos_t = positions.reshape(b, sub_len, dil).transpose(0, 2, 1).reshape(b * dil, sub_len)
    half = (K_WINDOW - Q_BLOCK) // 2
    starts = jnp.clip(jnp.arange(nq) * Q_BLOCK - half, 0, sub_len - K_WINDOW)
    win = starts[:, None] + jnp.arange(K_WINDOW)[None, :]
    kpos = pos_t[:, win].reshape(b * dil, nq, 1, K_WINDOW)
    qpos = pos_t.reshape(b * dil, sub_len, 1)
    qspec = pl.BlockSpec((1, Q_BLOCK, w), lambda bi, r, j: (bi, j, r))
    kvspec = pl.BlockSpec((1, sub_len, w), lambda bi, r, j: (bi, 0, r))
    o, lse = pl.pallas_call(
        functools.partial(_dilated_kernel, sub_len=sub_len),
        grid=(b, dil, nq),
        in_specs=[qspec, kvspec, kvspec,
                  pl.BlockSpec((1, Q_BLOCK, 1), lambda bi, r, j: (bi * dil + r, j, 0)),
                  pl.BlockSpec((1, 1, 1, K_WINDOW), lambda bi, r, j: (bi * dil + r, j, 0, 0)),
                  pl.BlockSpec((N_HEADS_A, LANES), lambda bi, r, j: (0, 0))],
        out_specs=[qspec, qspec],
        out_shape=[jax.ShapeDtypeStruct((b, sub_len, dil * w), F32)] * 2,
        compiler_params=pltpu.CompilerParams(
            dimension_semantics=("parallel", "parallel", "arbitrary"),
            vmem_limit_bytes=VMEM_LIMIT),
        name=f"dilated{dil}",
    )(view(qa), view(ka), view(va), qpos, kpos, tbl)
    return o.reshape(b, s, w), lse.reshape(b, s, w)


def _mla_kernel(q_ref, k_ref, v_ref, o_ref, *, tk):
    q = q_ref[0, 0]
    tq = q.shape[0]
    n_kv = k_ref.shape[2] // tk

    def body(i, carry):
        m, acc = carry
        start = pl.multiple_of(i * tk, tk)
        k = k_ref[0, 0, pl.ds(start, tk), :]
        v = v_ref[0, 0, pl.ds(start, tk), :]
        s = _dot_nt(q, k)
        m_new = jnp.maximum(m, jnp.max(s, axis=1, keepdims=True))
        p = jnp.exp2(s - m_new)
        acc = jnp.exp2(m - m_new) * acc + _dot(p.astype(BF16), v)
        return m_new, acc

    m0 = jnp.full((tq, 1), NEG_INF, F32)
    _, acc = lax.fori_loop(0, n_kv, body, (m0, jnp.zeros((tq, LANES), F32)), unroll=True)
    lane = lax.broadcasted_iota(jnp.int32, (1, LANES), 1)
    denom = jnp.sum(jnp.where(lane == V_DIM_B, acc, 0.0), axis=1, keepdims=True)
    o_ref[0, 0] = jnp.where(lane < V_DIM_B, acc / denom, 0.0).astype(BF16)


def _mla(qm, km, vm, tq, tk):
    b, nh, s, _ = qm.shape
    qspec = pl.BlockSpec((1, 1, tq, LANES), lambda bi, h, i: (bi, h, i, 0))
    kvspec = pl.BlockSpec((1, 1, s, LANES), lambda bi, h, i: (bi, h, 0, 0))
    return pl.pallas_call(
        functools.partial(_mla_kernel, tk=tk),
        grid=(b, nh, s // tq),
        in_specs=[qspec, kvspec, kvspec],
        out_specs=qspec,
        out_shape=jax.ShapeDtypeStruct((b, nh, s, LANES), BF16),
        compiler_params=pltpu.CompilerParams(
            dimension_semantics=("parallel", "parallel", "arbitrary"),
            vmem_limit_bytes=VMEM_LIMIT),
        name="mla",
    )(qm, km, vm)


def _memkv_kernel(mem_ref, g_ref, w_ref, k_ref, v_ref):
    d = mem_ref.shape[2]
    kv = _dot(_rms(mem_ref[0], g_ref[...]).astype(BF16), w_ref[...])
    k_ref[0] = kv[:, :d].astype(BF16)
    v_ref[0] = kv[:, d:].astype(BF16)


def _memkv(mem, g_mem, w_mkv):
    b, m, d = mem.shape
    spec = pl.BlockSpec((1, m, d), lambda bi: (bi, 0, 0))
    return pl.pallas_call(
        _memkv_kernel,
        grid=(b,),
        in_specs=[spec, pl.BlockSpec((1, d), lambda bi: (0, 0)),
                  pl.BlockSpec((d, 2 * d), lambda bi: (0, 0))],
        out_specs=[spec, spec],
        out_shape=[jax.ShapeDtypeStruct((b, m, d), BF16)] * 2,
        compiler_params=pltpu.CompilerParams(
            dimension_semantics=("parallel",), vmem_limit_bytes=VMEM_LIMIT),
        name="memkv",
    )(mem, g_mem, w_mkv)


def _post_kernel(x_ref, o1_ref, o2_ref, o3_ref, l1_ref, l2_ref, l3_ref, ob_ref,
                 goa_ref, gob_ref, woa_ref, wob_ref, gx_ref, wmq_ref, km_ref, vm_ref, wmo_ref,
                 gmoe_ref, wr_ref, br_ref,
                 h_ref, xn_ref, logit_ref):
    l1, l2, l3 = l1_ref[0], l2_ref[0], l3_ref[0]
    mx = jnp.maximum(jnp.maximum(l1, l2), l3)
    w1, w2, w3 = jnp.exp(l1 - mx), jnp.exp(l2 - mx), jnp.exp(l3 - mx)
    oa = (w1 * o1_ref[0] + w2 * o2_ref[0] + w3 * o3_ref[0]) / (w1 + w2 + w3)
    oan = _rms(oa, goa_ref[...]).astype(BF16)
    ob = jnp.concatenate([ob_ref[0, h] for h in range(N_HEADS_B)], axis=1).astype(F32)
    ms_b = jnp.sum(ob * ob, axis=1, keepdims=True) * (1.0 / (N_HEADS_B * V_DIM_B))
    obn = (ob * lax.rsqrt(ms_b + EPS) * gob_ref[...]).astype(BF16)
    h1 = x_ref[0] + _dot(oan, woa_ref[...]) + _dot(obn, wob_ref[...])

    hn = _rms(h1, gx_ref[...]).astype(BF16)
    q = _dot(hn, wmq_ref[...]).astype(BF16)
    dh = q.shape[1] // N_HEADS_MEM
    heads = []
    for h in range(N_HEADS_MEM):
        sl = slice(h * dh, (h + 1) * dh)
        s = _dot_nt(q[:, sl], km_ref[0, :, sl])
        e = jnp.exp(s - jnp.max(s, axis=1, keepdims=True))
        p = e / jnp.sum(e, axis=1, keepdims=True)
        heads.append(_dot(p.astype(BF16), vm_ref[0, :, sl]))
    o = jnp.concatenate(heads, axis=1).astype(BF16)
    h2 = h1 + _dot(o, wmo_ref[...])
    h_ref[0] = h2

    xn = _rms(h2, gmoe_ref[...])
    for c in range(xn_ref.shape[1]):
        xn_ref[:, c, :] = xn[:, c * LANES:(c + 1) * LANES]
    logit_ref[0] = jnp.dot(xn, wr_ref[...], preferred_element_type=F32,
                           precision=lax.Precision.HIGHEST) + br_ref[...]


def _post(x, o_pats, lse_pats, ob, g_out_a, g_out_b_pad, w_o_a, w_o_b_pad, g_xattn, w_mq, kmem,
          vmem, w_mo, g_moe, w_router_pad, b_router_pad, tm):
    b, s, d = x.shape
    m = kmem.shape[1]
    full = lambda shape: pl.BlockSpec(shape, lambda bi, i: (0,) * len(shape))
    row = lambda w: pl.BlockSpec((1, tm, w), lambda bi, i: (bi, i, 0))
    memspec = pl.BlockSpec((1, m, d), lambda bi, i: (bi, 0, 0))
    hw = N_HEADS_B * LANES
    return pl.pallas_call(
        _post_kernel,
        grid=(b, s // tm),
        in_specs=[row(d)] + [row(WIDTH_A)] * 6
        + [pl.BlockSpec((1, N_HEADS_B, tm, LANES), lambda bi, i: (bi, 0, i, 0)),
           full((1, WIDTH_A)), full((1, hw)), full((WIDTH_A, d)), full((hw, d)),
           full((1, d)), full((d, d)), memspec, memspec, full((d, d)),
           full((1, d)), full((d, LANES)), full((1, LANES))],
        out_specs=[row(d), pl.BlockSpec((tm, d // LANES, LANES), lambda bi, i: (bi * (s // tm) + i, 0, 0)),
                   row(LANES)],
        out_shape=[jax.ShapeDtypeStruct((b, s, d), F32),
                   jax.ShapeDtypeStruct((b * s, d // LANES, LANES), F32),
                   jax.ShapeDtypeStruct((b, s, LANES), F32)],
        compiler_params=pltpu.CompilerParams(
            dimension_semantics=("parallel", "parallel"), vmem_limit_bytes=VMEM_LIMIT),
        name="post",
    )(x, *o_pats, *lse_pats, ob, g_out_a, g_out_b_pad, w_o_a, w_o_b_pad, g_xattn, w_mq, kmem,
      vmem, w_mo, g_moe, w_router_pad, b_router_pad)


def _route_kernel(logit_ref, gate_ref, eidx_ref, rank_ref, count_ref, carry_ref):
    @pl.when(pl.program_id(0) == 0)
    def _():
        carry_ref[...] = jnp.zeros_like(carry_ref)

    l = logit_ref[...]
    tr = l.shape[0]
    lane_i = lax.broadcasted_iota(jnp.int32, l.shape, 1)
    lane = lane_i.astype(F32)
    vals, idxs = [], []
    for _ in range(TOP_K):
        m = jnp.max(l, axis=1, keepdims=True)
        idx = jnp.min(jnp.where(l == m, lane, float(LANES)), axis=1, keepdims=True)
        vals.append(m)
        idxs.append(idx)
        l = jnp.where(lane == idx, -jnp.inf, l)
    exps = [jnp.exp(v - vals[0]) for v in vals]
    denom = exps[0] + exps[1] + exps[2] + exps[3]
    onehot = jnp.zeros(l.shape, F32)
    for idx in idxs:
        onehot = onehot + jnp.where(lane == idx, 1.0, 0.0)
    r = lax.broadcasted_iota(jnp.int32, (tr, tr), 0)
    c = lax.broadcasted_iota(jnp.int32, (tr, tr), 1)
    tri = jnp.where(c < r, 1.0, 0.0).astype(BF16)
    before = _dot(tri, onehot.astype(BF16)) + carry_ref[...]
    gate = jnp.zeros(l.shape, F32)
    eidx = jnp.zeros(l.shape, jnp.int32)
    rank = jnp.zeros(l.shape, jnp.int32)
    for k in range(TOP_K):
        rk = jnp.sum(jnp.where(lane == idxs[k], before, 0.0), axis=1, keepdims=True)
        gate = jnp.where(lane_i == k, exps[k] / denom, gate)
        eidx = jnp.where(lane_i == k, idxs[k].astype(jnp.int32), eidx)
        rank = jnp.where(lane_i == k, rk.astype(jnp.int32), rank)
    gate_ref[...] = gate
    eidx_ref[...] = eidx
    rank_ref[...] = rank
    total = carry_ref[...] + jnp.sum(onehot, axis=0, keepdims=True)
    carry_ref[...] = total
    count_ref[...] = total.astype(jnp.int32)


def _route(logits, tr):
    t = logits.shape[0]
    spec = pl.BlockSpec((tr, LANES), lambda i: (i, 0))
    return pl.pallas_call(
        _route_kernel,
        grid=(t // tr,),
        in_specs=[spec],
        out_specs=[spec, spec, spec, pl.BlockSpec((1, LANES), lambda i: (0, 0))],
        out_shape=[jax.ShapeDtypeStruct((t, LANES), F32), jax.ShapeDtypeStruct((t, LANES), jnp.int32),
                   jax.ShapeDtypeStruct((t, LANES), jnp.int32),
                   jax.ShapeDtypeStruct((1, LANES), jnp.int32)],
        scratch_shapes=[pltpu.VMEM((1, LANES), F32)],
        compiler_params=pltpu.CompilerParams(
            dimension_semantics=("arbitrary",), vmem_limit_bytes=VMEM_LIMIT),
        name="route",
    )(logits)


def _dispatch_kernel(pos_ref, ends_ref, x_ref, xs_hbm, zbuf, sem, zsem):
    c = pl.program_id(0)
    blk = zbuf.shape[0]

    @pl.when(c == 0)
    def _():
        zbuf[...] = jnp.zeros_like(zbuf)

        def zero_copy(start):
            return pltpu.make_async_copy(zbuf, xs_hbm.at[pl.ds(start, blk)], zsem)

        used_end = ends_ref[N_EXPERTS]
        starts = [(ends_ref[e + 1] > ends_ref[e], ends_ref[e + 1] - blk) for e in range(N_EXPERTS)]
        starts += [(used_end + (j + 1) * blk <= xs_hbm.shape[0], used_end + j * blk)
                   for j in range(N_EXPERTS)]
        for cond, start in starts:
            @pl.when(cond)
            def _():
                zero_copy(start).start()
        for cond, start in starts:
            @pl.when(cond)
            def _():
                zero_copy(start).wait()

    def body(t, carry):
        tok = c * DISPATCH_CHUNK + t
        for k in range(TOP_K):
            pltpu.make_async_copy(x_ref.at[t], xs_hbm.at[pos_ref[tok * TOP_K + k]], sem).start()
        return carry
    lax.fori_loop(0, DISPATCH_CHUNK, body, 0, unroll=4)

    for k in range(TOP_K):
        pltpu.make_async_copy(x_ref, xs_hbm.at[pl.ds(0, DISPATCH_CHUNK)], sem).wait()


def _dispatch(pos, ends0, xn3, n_rows):
    grid_spec = pltpu.PrefetchScalarGridSpec(
        num_scalar_prefetch=2,
        grid=(xn3.shape[0] // DISPATCH_CHUNK,),
        in_specs=[pl.BlockSpec((DISPATCH_CHUNK,) + xn3.shape[1:], lambda c, pos, ends: (c, 0, 0))],
        out_specs=pl.BlockSpec(memory_space=pl.ANY),
        scratch_shapes=[pltpu.VMEM((EXPERT_BLOCK,) + xn3.shape[1:], F32),
                        pltpu.SemaphoreType.DMA(()), pltpu.SemaphoreType.DMA(())],
    )
    return pl.pallas_call(
        _dispatch_kernel,
        grid_spec=grid_spec,
        out_shape=jax.ShapeDtypeStruct((n_rows,) + xn3.shape[1:], F32),
        compiler_params=pltpu.CompilerParams(
            dimension_semantics=("arbitrary",), vmem_limit_bytes=VMEM_LIMIT),
        name="dispatch",
    )(pos, ends0, xn3)


def _expert_kernel(be_ref, nused_ref, xs_ref, wgu_ref, bgu_ref, wd_ref, bd_ref, y_ref, wgu_bf, wd_bf):
    i = pl.program_id(0)
    n_chunk = xs_ref.shape[1]

    @pl.when(i < nused_ref[0])
    def _():
        @pl.when(jnp.logical_or(i == 0, be_ref[i] != be_ref[jnp.maximum(i - 1, 0)]))
        def _():
            wgu_bf[...] = wgu_ref[0].astype(BF16)
            wd_bf[...] = wd_ref[0].astype(BF16)

        x = jnp.concatenate([xs_ref[:, c, :] for c in range(n_chunk)], axis=1).astype(BF16)
        gu = _dot(x, wgu_bf[...]) + bgu_ref[0]
        de = gu.shape[1] // 2
        gate = jnp.minimum(gu[:, :de], SWIGLU_LIMIT)
        up = jnp.clip(gu[:, de:], -SWIGLU_LIMIT, SWIGLU_LIMIT)
        hmid = (up + 1.0) * (gate * jax.nn.sigmoid(SWIGLU_ALPHA * gate))
        out = _dot(hmid.astype(BF16), wd_bf[...]) + bd_ref[0]
        for c in range(n_chunk):
            y_ref[:, c, :] = out[:, c * LANES:(c + 1) * LANES]

    @pl.when(i >= nused_ref[0])
    def _():
        y_ref[...] = jnp.zeros_like(y_ref)


def _experts(block_expert, n_used, xs3, w_gate_up, b_gate_up, w_down, b_down):
    n_rows, n_chunk, _ = xs3.shape
    d = n_chunk * LANES
    blk = EXPERT_BLOCK
    n_blk = n_rows // blk
    de2 = w_gate_up.shape[2]
    grid_spec = pltpu.PrefetchScalarGridSpec(
        num_scalar_prefetch=2,
        grid=(n_blk,),
        in_specs=[pl.BlockSpec((blk, n_chunk, LANES),
                               lambda i, be, nu: (jnp.minimum(i, nu[0] - 1), 0, 0)),
                  pl.BlockSpec((1, d, de2), lambda i, be, nu: (be[i], 0, 0)),
                  pl.BlockSpec((1, 1, de2), lambda i, be, nu: (be[i], 0, 0)),
                  pl.BlockSpec((1, de2 // 2, d), lambda i, be, nu: (be[i], 0, 0)),
                  pl.BlockSpec((1, 1, d), lambda i, be, nu: (be[i], 0, 0))],
        out_specs=pl.BlockSpec((blk, n_chunk, LANES), lambda i, be, nu: (i, 0, 0)),
        scratch_shapes=[pltpu.VMEM((d, de2), BF16), pltpu.VMEM((de2 // 2, d), BF16)],
    )
    return pl.pallas_call(
        _expert_kernel,
        grid_spec=grid_spec,
        out_shape=jax.ShapeDtypeStruct((n_rows, n_chunk, LANES), F32),
        compiler_params=pltpu.CompilerParams(
            dimension_semantics=("arbitrary",), vmem_limit_bytes=VMEM_LIMIT),
        name="experts",
    )(block_expert, n_used, xs3, w_gate_up, b_gate_up.reshape(N_EXPERTS, 1, de2),
      w_down, b_down.reshape(N_EXPERTS, 1, d))


def _combine_kernel(pos_ref, h_ref, gate_ref, gfin_ref, y_hbm, out_ref, buf, sems):
    i = pl.program_id(0)
    n = pl.num_programs(0)
    slot = lax.rem(i, 2)
    tc, n_chunk = buf.shape[2], buf.shape[3]

    def issue(tile, s):
        def body(r, carry):
            for k in range(TOP_K):
                row = pos_ref[(tile * tc + r) * TOP_K + k]
                pltpu.make_async_copy(y_hbm.at[row], buf.at[s, k, r], sems.at[s]).start()
            return carry
        lax.fori_loop(0, tc, body, 0, unroll=4)

    @pl.when(i == 0)
    def _():
        issue(0, 0)

    @pl.when(i + 1 < n)
    def _():
        issue(i + 1, 1 - slot)

    for k in range(TOP_K):
        pltpu.make_async_copy(y_hbm.at[pl.ds(0, tc)], buf.at[slot, k], sems.at[slot]).wait()
    gate = gate_ref[...]
    pieces = []
    for c in range(n_chunk):
        y = h_ref[:, c * LANES:(c + 1) * LANES]
        for k in range(TOP_K):
            y = y + gate[:, k:k + 1] * buf[slot, k, :, c, :]
        pieces.append(y)
    out_ref[...] = _rms(jnp.concatenate(pieces, axis=1), gfin_ref[...])


def _combine(pos, h2, gates, g_final, y3, tc):
    t, d = h2.shape
    n_chunk = y3.shape[1]
    grid_spec = pltpu.PrefetchScalarGridSpec(
        num_scalar_prefetch=1,
        grid=(t // tc,),
        in_specs=[pl.BlockSpec((tc, d), lambda i, pos: (i, 0)),
                  pl.BlockSpec((tc, LANES), lambda i, pos: (i, 0)),
                  pl.BlockSpec((1, d), lambda i, pos: (0, 0)),
                  pl.BlockSpec(memory_space=pl.ANY)],
        out_specs=pl.BlockSpec((tc, d), lambda i, pos: (i, 0)),
        scratch_shapes=[pltpu.VMEM((2, TOP_K, tc, n_chunk, LANES), F32),
                        pltpu.SemaphoreType.DMA((2,))],
    )
    return pl.pallas_call(
        _combine_kernel,
        grid_spec=grid_spec,
        out_shape=jax.ShapeDtypeStruct((t, d), F32),
        compiler_params=pltpu.CompilerParams(
            dimension_semantics=("arbitrary",), vmem_limit_bytes=VMEM_LIMIT),
        name="combine",
    )(pos, h2, gates, g_final, y3)


def _pad_heads(w, n_heads, width, offset=0):
    k = w.shape[0]
    w = w.reshape(k, n_heads, width)
    w = jnp.pad(w, ((0, 0), (0, 0), (offset, LANES - width - offset)))
    return w.reshape(k, n_heads * LANES)


def _rot_cols(w):
    half = w.shape[-1] // 2
    return jnp.concatenate([-w[..., half:], w[..., :half]], axis=-1)


def kernel(x, mem, positions, g_mix, w_in, g_q_a, w_q_b, g_kv_a, w_kv_b, rel_bias, g_out_a, g_out_b,
           w_o, g_xattn, g_mem, w_mq, w_mkv, w_mo, g_moe, w_router, b_router, w_gate_up, b_gate_up,
           w_down, b_down, g_final):
    b, s, d = x.shape
    t = b * s
    assert g_mix.shape[0] == 1, "single-layer block: the final norm is fused into the last stage"

    inv_freq = ROPE_THETA ** (-jnp.arange(0, QK_ROPE_DIM, 2, dtype=F32) / QK_ROPE_DIM)
    ang = positions.astype(F32)[..., None] * inv_freq
    cos, sin = jnp.cos(ang), jnp.sin(ang)
    pad_lo, pad_hi = QK_NOPE_DIM, LANES - QK_NOPE_DIM - QK_ROPE_DIM
    cos128 = jnp.concatenate([jnp.ones((b, s, pad_lo), F32), cos, cos, jnp.ones((b, s, pad_hi), F32)], -1)
    sin128 = jnp.concatenate([jnp.zeros((b, s, pad_lo), F32), sin, sin, jnp.zeros((b, s, pad_hi), F32)], -1)
    tbl = jnp.pad(rel_bias.T.astype(F32), ((0, 0), (0, LANES - NUM_BUCKETS)))

    h = x
    for l in range(1):
        c0 = 3 * WIDTH_A + Q_LORA_RANK + KV_LORA_RANK
        w_kpe = w_in[l][:, c0:c0 + QK_ROPE_DIM]
        place = lambda w: jnp.pad(w, ((0, 0), (QK_NOPE_DIM, LANES - QK_NOPE_DIM - QK_ROPE_DIM)))
        w_in_ext = jnp.concatenate([w_in[l][:, :c0], place(w_kpe), place(_rot_cols(w_kpe))], 1).astype(BF16)
        dq = QK_NOPE_DIM + QK_ROPE_DIM
        wq3 = w_q_b[l].reshape(Q_LORA_RANK, N_HEADS_B, dq)
        wq = _pad_heads(wq3.reshape(Q_LORA_RANK, -1), N_HEADS_B, dq).astype(BF16)
        wq_rot3 = jnp.concatenate([jnp.zeros_like(wq3[..., :QK_NOPE_DIM]), _rot_cols(wq3[..., QK_NOPE_DIM:])], -1)
        wqr = _pad_heads(wq_rot3.reshape(Q_LORA_RANK, -1), N_HEADS_B, dq).astype(BF16)
        wkv3 = w_kv_b[l].reshape(KV_LORA_RANK, N_HEADS_B, QK_NOPE_DIM + V_DIM_B)
        wk = _pad_heads(wkv3[..., :QK_NOPE_DIM].reshape(KV_LORA_RANK, -1), N_HEADS_B, QK_NOPE_DIM).astype(BF16)
        wv = _pad_heads(wkv3[..., QK_NOPE_DIM:].reshape(KV_LORA_RANK, -1), N_HEADS_B, V_DIM_B).astype(BF16)

        qa, ka, va, qm, km, vm = _proj(h, cos128, sin128, g_mix[l][None], w_in_ext, g_q_a[l][None], wq, wqr,
                                       g_kv_a[l][None], wk, wv, tm=256)
        pats = [_dilated(qa, ka, va, positions, tbl, dil) for _, dil in DILATED_PATTERNS]
        ob = _mla(qm, km, vm, tq=1024, tk=512)
        kmem, vmem = _memkv(mem, g_mem[l][None], w_mkv[l].astype(BF16))

        g_out_b_pad = _pad_heads(g_out_b[l][None], N_HEADS_B, V_DIM_B)
        w_o_b_pad = _pad_heads(w_o[l][WIDTH_A:].T, N_HEADS_B, V_DIM_B).T.astype(BF16)
        w_mq_s = (w_mq[l] * ((d // N_HEADS_MEM) ** -0.5)).astype(BF16)
        w_router_pad = jnp.pad(w_router[l], ((0, 0), (0, LANES - N_EXPERTS)))
        b_router_pad = jnp.pad(b_router[l][None], ((0, 0), (0, LANES - N_EXPERTS)), constant_values=NEG_INF)
        h2, xn3, logits = _post(h, [p[0] for p in pats], [p[1] for p in pats], ob, g_out_a[l][None],
                               g_out_b_pad, w_o[l][:WIDTH_A].astype(BF16), w_o_b_pad, g_xattn[l][None],
                               w_mq_s, kmem, vmem, w_mo[l].astype(BF16), g_moe[l][None], w_router_pad,
                               b_router_pad, tm=256)

        gates, eidx, rank, counts = _route(logits.reshape(t, LANES), tr=256)
        counts = counts[0, :N_EXPERTS]
        blk = EXPERT_BLOCK
        padded = ((counts + blk - 1) // blk) * blk
        ends = jnp.cumsum(padded)
        pad_start = ends - padded
        eidx4, rank4 = eidx[:, :TOP_K], rank[:, :TOP_K]
        expert_ids = jnp.arange(N_EXPERTS, dtype=jnp.int32)
        pos = (jnp.sum(jnp.where(eidx4[..., None] == expert_ids, pad_start, 0), axis=-1)
               + rank4).astype(jnp.int32).reshape(-1)
        n_blk = t * TOP_K // blk + N_EXPERTS
        block_expert = jnp.minimum(
            jnp.sum(ends[None, :] <= (jnp.arange(n_blk) * blk)[:, None], axis=1),
            N_EXPERTS - 1).astype(jnp.int32)
        n_used = (ends[-1] // blk).astype(jnp.int32)[None]
        ends0 = jnp.concatenate([jnp.zeros((1,), jnp.int32), ends.astype(jnp.int32)])
        xs3 = _dispatch(pos, ends0, xn3, n_blk * blk)
        y3 = _experts(block_expert, n_used, xs3, w_gate_up[l], b_gate_up[l], w_down[l], b_down[l])
        out = _combine(pos, h2.reshape(t, d), gates, g_final[None], y3, tc=128)
        h = out.reshape(b, s, d)
    return h
```

```python
import functools
import math

import jax
import jax.numpy as jnp
from jax import lax
from jax.experimental import pallas as pl
from jax.experimental.pallas import tpu as pltpu

F32 = jnp.float32
BF16 = jnp.bfloat16

LANES = 128
EPS = 1e-6
NEG_INF = -1e30
LOG2_E = math.log2(math.e)

N_HEADS_A = 8
HEAD_DIM_A = 64
WIDTH_A = N_HEADS_A * HEAD_DIM_A
DILATED_PATTERNS = ((128, 1), (512, 4), (2048, 16))
N_HEADS_B = 8
QK_NOPE_DIM = 64
QK_ROPE_DIM = 32
V_DIM_B = 64
Q_LORA_RANK = 256
KV_LORA_RANK = 128
ROPE_THETA = 10000.0
NUM_BUCKETS = 32
MAX_DISTANCE = 1024
N_HEADS_MEM = 4
N_EXPERTS = 32
TOP_K = 4
SWIGLU_LIMIT = 7.0
SWIGLU_ALPHA = 1.702

Q_BLOCK = 128
K_WINDOW = 256
HALF_WINDOW = (K_WINDOW - Q_BLOCK) // 2
EXPERT_BLOCK = 256
DISPATCH_CHUNK = 512
VMEM_LIMIT = 56 * 1024 * 1024


def _rms(x, g):
    return x * lax.rsqrt(jnp.mean(x * x, axis=-1, keepdims=True) + EPS) * g


def _dot(a, b):
    return jnp.dot(a, b, preferred_element_type=F32)


def _dot_nt(a, b):
    return lax.dot_general(a, b, (((1,), (1,)), ((), ())), preferred_element_type=F32)


def _proj_kernel(x_ref, cos_ref, sin_ref, gmix_ref, win_ref, gq_ref, wq_ref, wqr_ref,
                 gkv_ref, wk_ref, wv_ref,
                 q1_ref, k1_ref, v1_ref, q4_ref, k4_ref, v4_ref, q16_ref, k16_ref, v16_ref,
                 qm_ref, km_ref, vm_ref, za_scr):
    xn = _rms(x_ref[0], gmix_ref[...])
    z = _dot(xn.astype(BF16), win_ref[...])
    tm = z.shape[0]
    c0 = 3 * WIDTH_A
    per_group = WIDTH_A // LANES
    for c in range(3 * per_group):
        chunk = z[:, c * LANES:(c + 1) * LANES]
        za_scr[c] = chunk * (HEAD_DIM_A ** -0.5 * LOG2_E) if c < per_group else chunk
    for dil, refs in ((1, (q1_ref, k1_ref, v1_ref)), (4, (q4_ref, k4_ref, v4_ref)),
                      (16, (q16_ref, k16_ref, v16_ref))):
        for r in range(dil):
            for c in range(3 * per_group):
                rows = za_scr[c, pl.ds(r, tm // dil, stride=dil), :].astype(BF16)
                col = r * WIDTH_A + (c % per_group) * LANES
                refs[c // per_group][0, :, col:col + LANES] = rows
    zqn = _rms(z[:, c0:c0 + Q_LORA_RANK], gq_ref[...]).astype(BF16)
    zkvn = _rms(z[:, c0 + Q_LORA_RANK:c0 + Q_LORA_RANK + KV_LORA_RANK], gkv_ref[...]).astype(BF16)
    c1 = c0 + Q_LORA_RANK + KV_LORA_RANK
    cos = cos_ref[0]
    sin = sin_ref[0]
    kpe = z[:, c1:c1 + LANES] * cos + z[:, c1 + LANES:c1 + 2 * LANES] * sin
    q = _dot(zqn, wq_ref[...])
    qr = _dot(zqn, wqr_ref[...])
    kn = _dot(zkvn, wk_ref[...])
    vv = _dot(zkvn, wv_ref[...])
    lane = lax.broadcasted_iota(jnp.int32, (1, LANES), 1)
    ones_col = jnp.where(lane == V_DIM_B, 1.0, 0.0).astype(F32)
    scale = (QK_NOPE_DIM + QK_ROPE_DIM) ** -0.5 * LOG2_E
    for h in range(N_HEADS_B):
        sl = slice(h * LANES, (h + 1) * LANES)
        qm_ref[0, h] = ((q[:, sl] * cos + qr[:, sl] * sin) * scale).astype(BF16)
        km_ref[0, h] = (kn[:, sl] + kpe).astype(BF16)
        vm_ref[0, h] = (vv[:, sl] + ones_col).astype(BF16)


def _proj(x, cos128, sin128, g_mix, w_in_ext, g_q, wq, wqr, g_kv, wk, wv, tm):
    b, s, d = x.shape
    n_in = w_in_ext.shape[1]
    hw = N_HEADS_B * LANES
    full = lambda shape: pl.BlockSpec(shape, lambda bi, i: (0,) * len(shape))
    row = lambda w: pl.BlockSpec((1, tm, w), lambda bi, i: (bi, i, 0))
    head = pl.BlockSpec((1, N_HEADS_B, tm, LANES), lambda bi, i: (bi, 0, i, 0))
    dils = [dil for _, dil in DILATED_PATTERNS]
    view_specs = [pl.BlockSpec((1, tm // dil, dil * WIDTH_A), lambda bi, i: (bi, i, 0))
                  for dil in dils for _ in range(3)]
    view_shapes = [jax.ShapeDtypeStruct((b, s // dil, dil * WIDTH_A), BF16)
                   for dil in dils for _ in range(3)]
    return pl.pallas_call(
        _proj_kernel,
        grid=(b, s // tm),
        in_specs=[row(d), row(LANES), row(LANES), full((1, d)), full((d, n_in)),
                  full((1, Q_LORA_RANK)), full((Q_LORA_RANK, hw)), full((Q_LORA_RANK, hw)),
                  full((1, KV_LORA_RANK)), full((KV_LORA_RANK, hw)), full((KV_LORA_RANK, hw))],
        out_specs=view_specs + [head, head, head],
        out_shape=view_shapes + [jax.ShapeDtypeStruct((b, N_HEADS_B, s, LANES), BF16)] * 3,
        scratch_shapes=[pltpu.VMEM((3 * WIDTH_A // LANES, tm, LANES), F32)],
        compiler_params=pltpu.CompilerParams(
            dimension_semantics=("parallel", "parallel"), vmem_limit_bytes=VMEM_LIMIT),
        name="proj",
    )(x, cos128, sin128, g_mix, w_in_ext, g_q, wq, wqr, g_kv, wk, wv)


_LOG_BUCKET_STARTS = tuple(
    next(n for n in range(8, 4096)
         if int(math.log(n / 8) / math.log(MAX_DISTANCE / 8) * 8) >= t)
    for t in range(1, 8))


def _rel_bucket(rel):
    n = jnp.abs(rel)
    large = jnp.full(rel.shape, NUM_BUCKETS // 4, jnp.int32)
    for start in _LOG_BUCKET_STARTS:
        large = large + jnp.where(n >= start, 1, 0)
    mag = jnp.where(n < NUM_BUCKETS // 4, n, large)
    return mag + jnp.where(rel > 0, NUM_BUCKETS // 2, 0)


def _window_start(j, sub_len):
    return jnp.clip(j * Q_BLOCK - HALF_WINDOW, 0, sub_len - K_WINDOW)


def _bias_kernel(off_ref, qpos_ref, kpos_ref, tbl_ref, out_ref):
    rel = kpos_ref[0] - qpos_ref[0]
    bucket = _rel_bucket(rel)
    delta = (off_ref[pl.program_id(0)] + lax.broadcasted_iota(jnp.int32, (1, K_WINDOW), 1)
             - lax.broadcasted_iota(jnp.int32, (Q_BLOCK, 1), 0))
    valid = jnp.abs(delta) <= HALF_WINDOW
    for h in range(N_HEADS_A):
        tbl = jnp.broadcast_to(tbl_ref[h:h + 1, :], (Q_BLOCK, LANES))
        bias = jnp.concatenate(
            [jnp.take_along_axis(tbl, bucket[:, c * LANES:(c + 1) * LANES], axis=1)
             for c in range(K_WINDOW // LANES)], axis=1)
        out_ref[0, h] = jnp.where(valid, bias * LOG2_E, NEG_INF)


def _bias_tiles(offs, qpos, kpos, tbl):
    n = offs.shape[0]
    grid_spec = pltpu.PrefetchScalarGridSpec(
        num_scalar_prefetch=1,
        grid=(n,),
        in_specs=[pl.BlockSpec((1, Q_BLOCK, 1), lambda t, off: (t, 0, 0)),
                  pl.BlockSpec((1, 1, K_WINDOW), lambda t, off: (t, 0, 0)),
                  pl.BlockSpec((N_HEADS_A, LANES), lambda t, off: (0, 0))],
        out_specs=pl.BlockSpec((1, N_HEADS_A, Q_BLOCK, K_WINDOW), lambda t, off: (t, 0, 0, 0)),
    )
    return pl.pallas_call(
        _bias_kernel,
        grid_spec=grid_spec,
        out_shape=jax.ShapeDtypeStruct((n, N_HEADS_A, Q_BLOCK, K_WINDOW), F32),
        compiler_params=pltpu.CompilerParams(
            dimension_semantics=("arbitrary",), vmem_limit_bytes=VMEM_LIMIT),
        name="bias_tiles",
    )(offs, qpos, kpos, tbl)


def _dilated_kernel(q_ref, k_ref, v_ref, bias_ref, o_ref, lse_ref, s_scr, p_scr, *, sub_len):
    j = pl.program_id(2)
    kstart = pl.multiple_of(_window_start(j, sub_len), HALF_WINDOW)
    q = q_ref[0]
    kw = k_ref[0, pl.ds(kstart, K_WINDOW), :]
    vw = v_ref[0, pl.ds(kstart, K_WINDOW), :]
    first = lax.broadcasted_iota(jnp.int32, (1, LANES), 1) < HEAD_DIM_A
    pair = lambda h: slice((h // 2) * LANES, (h // 2 + 1) * LANES)
    for h in range(N_HEADS_A):
        own = first if h % 2 == 0 else jnp.logical_not(first)
        qh = jnp.where(own, q[:, pair(h)], jnp.zeros_like(q[:, pair(h)]))
        s_scr[h] = _dot_nt(qh, kw[:, pair(h)]) + bias_ref[0, h]
    stats = []
    for h in range(N_HEADS_A):
        s = s_scr[h]
        m = jnp.max(s, axis=1, keepdims=True)
        e = jnp.exp2(s - m)
        l = jnp.sum(e, axis=1, keepdims=True)
        p_scr[h] = e.astype(BF16)
        stats.append((1.0 / l, m + jnp.log2(l)))
    for h in range(0, N_HEADS_A, 2):
        (r0, lse0), (r1, lse1) = stats[h], stats[h + 1]
        o0 = _dot(p_scr[h], vw[:, pair(h)])
        o1 = _dot(p_scr[h + 1], vw[:, pair(h)])
        o_ref[0, :, pair(h)] = jnp.where(first, o0 * r0, o1 * r1)
        lse_ref[0, :, pair(h)] = jnp.where(first, lse0, lse1)


def _dilated(qv, kv, vv, bias, bias_index, dil):
    b, sub_len, _ = qv.shape
    w = WIDTH_A
    assert sub_len >= K_WINDOW and sub_len % Q_BLOCK == 0
    nq = sub_len // Q_BLOCK
    qspec = pl.BlockSpec((1, Q_BLOCK, w), lambda bi, r, j: (bi, j, r))
    kvspec = pl.BlockSpec((1, sub_len, w), lambda bi, r, j: (bi, 0, r))
    bspec = pl.BlockSpec((1, N_HEADS_A, Q_BLOCK, K_WINDOW),
                         lambda bi, r, j: (bias_index(bi, r, j, nq), 0, 0, 0))
    return pl.pallas_call(
        functools.partial(_dilated_kernel, sub_len=sub_len),
        grid=(b, dil, nq),
        in_specs=[qspec, kvspec, kvspec, bspec],
        out_specs=[qspec, qspec],
        out_shape=[jax.ShapeDtypeStruct((b, sub_len, dil * w), F32)] * 2,
        scratch_shapes=[pltpu.VMEM((N_HEADS_A, Q_BLOCK, K_WINDOW), F32),
                        pltpu.VMEM((N_HEADS_A, Q_BLOCK, K_WINDOW), BF16)],
        compiler_params=pltpu.CompilerParams(
            dimension_semantics=("parallel", "parallel", "arbitrary"),
            vmem_limit_bytes=VMEM_LIMIT),
        name=f"dilated{dil}",
    )(qv, kv, vv, bias)


def _dilated_all(views, positions, tbl):
    b, s = positions.shape
    dils = [dil for _, dil in DILATED_PATTERNS]
    consecutive = jnp.all(positions[:, 1:] - positions[:, :-1] == 1)

    def run(bias, index_fns):
        outs = []
        for g, dil in enumerate(dils):
            qv, kv, vv = views[3 * g:3 * g + 3]
            outs.extend(_dilated(qv, kv, vv, bias, index_fns[g], dil))
        return tuple(outs)

    def shared_tiles():
        offs, qpos, kpos, fns = [], [], [], []
        for g, dil in enumerate(dils):
            for off in (0, -HALF_WINDOW, -2 * HALF_WINDOW):
                offs.append(off)
                qpos.append(dil * jnp.arange(Q_BLOCK, dtype=jnp.int32))
                kpos.append(dil * (off + jnp.arange(K_WINDOW, dtype=jnp.int32)))
            fns.append(lambda bi, r, j, nq, g=g:
                       3 * g + jnp.where(j == 0, 0, jnp.where(j == nq - 1, 2, 1)))
        bias = _bias_tiles(jnp.array(offs, jnp.int32), jnp.stack(qpos)[:, :, None],
                           jnp.stack(kpos)[:, None, :], tbl)
        return run(bias, fns)

    def per_block_tiles():
        offs, qpos, kpos, fns = [], [], [], []
        base = 0
        for g, dil in enumerate(dils):
            sub_len = s // dil
            nq = sub_len // Q_BLOCK
            pos_t = positions.reshape(b, sub_len, dil).transpose(0, 2, 1).reshape(b * dil, sub_len)
            starts = _window_start(jnp.arange(nq), sub_len)
            win = starts[:, None] + jnp.arange(K_WINDOW)[None, :]
            qpos.append(pos_t.reshape(b * dil * nq, Q_BLOCK))
            kpos.append(pos_t[:, win].reshape(b * dil * nq, K_WINDOW))
            offs.append(jnp.tile(starts - jnp.arange(nq) * Q_BLOCK, b * dil))
            fns.append(lambda bi, r, j, nq, base=base, dil=dil: base + (bi * dil + r) * nq + j)
            base += b * dil * nq
        bias = _bias_tiles(jnp.concatenate(offs).astype(jnp.int32),
                           jnp.concatenate(qpos)[:, :, None], jnp.concatenate(kpos)[:, None, :], tbl)
        return run(bias, fns)

    return lax.cond(consecutive, shared_tiles, per_block_tiles)


def _mla_kernel(q_ref, k_ref, v_ref, o_ref, *, tk):
    q = q_ref[0, 0]
    tq = q.shape[0]
    n_kv = k_ref.shape[2] // tk

    def body(i, carry):
        m, acc = carry
        start = pl.multiple_of(i * tk, tk)
        k = k_ref[0, 0, pl.ds(start, tk), :]
        v = v_ref[0, 0, pl.ds(start, tk), :]
        s = _dot_nt(q, k)
        m_new = jnp.maximum(m, jnp.max(s, axis=1, keepdims=True))
        p = jnp.exp2(s - m_new)
        acc = jnp.exp2(m - m_new) * acc + _dot(p.astype(BF16), v)
        return m_new, acc

    m0 = jnp.full((tq, 1), NEG_INF, F32)
    _, acc = lax.fori_loop(0, n_kv, body, (m0, jnp.zeros((tq, LANES), F32)), unroll=True)
    lane = lax.broadcasted_iota(jnp.int32, (1, LANES), 1)
    denom = jnp.sum(jnp.where(lane == V_DIM_B, acc, 0.0), axis=1, keepdims=True)
    o_ref[0, 0] = jnp.where(lane < V_DIM_B, acc / denom, 0.0).astype(BF16)


def _mla(qm, km, vm, tq, tk):
    b, nh, s, _ = qm.shape
    qspec = pl.BlockSpec((1, 1, tq, LANES), lambda bi, h, i: (bi, h, i, 0))
    kvspec = pl.BlockSpec((1, 1, s, LANES), lambda bi, h, i: (bi, h, 0, 0))
    return pl.pallas_call(
        functools.partial(_mla_kernel, tk=tk),
        grid=(b, nh, s // tq),
        in_specs=[qspec, kvspec, kvspec],
        out_specs=qspec,
        out_shape=jax.ShapeDtypeStruct((b, nh, s, LANES), BF16),
        compiler_params=pltpu.CompilerParams(
            dimension_semantics=("parallel", "parallel", "arbitrary"),
            vmem_limit_bytes=VMEM_LIMIT),
        name="mla",
    )(qm, km, vm)


def _memkv_kernel(mem_ref, g_ref, w_ref, k_ref, v_ref):
    d = mem_ref.shape[2]
    kv = _dot(_rms(mem_ref[0], g_ref[...]).astype(BF16), w_ref[...])
    k_ref[0] = kv[:, :d].astype(BF16)
    v_ref[0] = kv[:, d:].astype(BF16)


def _memkv(mem, g_mem, w_mkv):
    b, m, d = mem.shape
    spec = pl.BlockSpec((1, m, d), lambda bi: (bi, 0, 0))
    return pl.pallas_call(
        _memkv_kernel,
        grid=(b,),
        in_specs=[spec, pl.BlockSpec((1, d), lambda bi: (0, 0)),
                  pl.BlockSpec((d, 2 * d), lambda bi: (0, 0))],
        out_specs=[spec, spec],
        out_shape=[jax.ShapeDtypeStruct((b, m, d), BF16)] * 2,
        compiler_params=pltpu.CompilerParams(
            dimension_semantics=("parallel",), vmem_limit_bytes=VMEM_LIMIT),
        name="memkv",
    )(mem, g_mem, w_mkv)


def _post_kernel(x_ref, o1_ref, o2_ref, o3_ref, l1_ref, l2_ref, l3_ref, ob_ref,
                 goa_ref, gob_ref, woa_ref, wob_ref, gx_ref, wmq_ref, km_ref, vm_ref, wmo_ref,
                 gmoe_ref, wr_ref, br_ref,
                 h_ref, xn_ref, logit_ref, *nat_scr):
    tm = x_ref.shape[1]

    def natural(view_ref, scr):
        dil = view_ref.shape[2] // WIDTH_A
        if dil == 1:
            return view_ref[0]
        n_chunk = WIDTH_A // LANES
        for r in range(dil):
            for c in range(n_chunk):
                col = r * WIDTH_A + c * LANES
                scr[c, pl.ds(r, tm // dil, stride=dil), :] = view_ref[0, :, col:col + LANES]
        return jnp.concatenate([scr[c] for c in range(n_chunk)], axis=1)

    o1, o2, o3 = o1_ref[0], natural(o2_ref, nat_scr[0]), natural(o3_ref, nat_scr[1])
    l1, l2, l3 = l1_ref[0], natural(l2_ref, nat_scr[2]), natural(l3_ref, nat_scr[3])
    mx = jnp.maximum(jnp.maximum(l1, l2), l3)
    w1, w2, w3 = jnp.exp2(l1 - mx), jnp.exp2(l2 - mx), jnp.exp2(l3 - mx)
    oa = (w1 * o1 + w2 * o2 + w3 * o3) / (w1 + w2 + w3)
    oan = _rms(oa, goa_ref[...]).astype(BF16)
    ob = jnp.concatenate([ob_ref[0, h] for h in range(N_HEADS_B)], axis=1).astype(F32)
    ms_b = jnp.sum(ob * ob, axis=1, keepdims=True) * (1.0 / (N_HEADS_B * V_DIM_B))
    obn = (ob * lax.rsqrt(ms_b + EPS) * gob_ref[...]).astype(BF16)
    h1 = x_ref[0] + _dot(oan, woa_ref[...]) + _dot(obn, wob_ref[...])

    hn = _rms(h1, gx_ref[...]).astype(BF16)
    q = _dot(hn, wmq_ref[...]).astype(BF16)
    dh = q.shape[1] // N_HEADS_MEM
    heads = []
    for h in range(N_HEADS_MEM):
        sl = slice(h * dh, (h + 1) * dh)
        s = _dot_nt(q[:, sl], km_ref[0, :, sl])
        e = jnp.exp(s - jnp.max(s, axis=1, keepdims=True))
        p = e / jnp.sum(e, axis=1, keepdims=True)
        heads.append(_dot(p.astype(BF16), vm_ref[0, :, sl]))
    o = jnp.concatenate(heads, axis=1).astype(BF16)
    h2 = h1 + _dot(o, wmo_ref[...])
    h_ref[0] = h2

    xn = _rms(h2, gmoe_ref[...])
    for c in range(xn_ref.shape[1]):
        xn_ref[:, c, :] = xn[:, c * LANES:(c + 1) * LANES]
    logit_ref[0] = jnp.dot(xn, wr_ref[...], preferred_element_type=F32,
                           precision=lax.Precision.HIGHEST) + br_ref[...]


def _post(x, o_pats, lse_pats, ob, g_out_a, g_out_b_pad, w_o_a, w_o_b_pad, g_xattn, w_mq, kmem,
          vmem, w_mo, g_moe, w_router_pad, b_router_pad, tm):
    b, s, d = x.shape
    m = kmem.shape[1]
    full = lambda shape: pl.BlockSpec(shape, lambda bi, i: (0,) * len(shape))
    row = lambda w: pl.BlockSpec((1, tm, w), lambda bi, i: (bi, i, 0))
    memspec = pl.BlockSpec((1, m, d), lambda bi, i: (bi, 0, 0))
    hw = N_HEADS_B * LANES
    views = [pl.BlockSpec((1, tm // dil, dil * WIDTH_A), lambda bi, i: (bi, i, 0))
             for _, dil in DILATED_PATTERNS]
    return pl.pallas_call(
        _post_kernel,
        grid=(b, s // tm),
        in_specs=[row(d)] + views + views
        + [pl.BlockSpec((1, N_HEADS_B, tm, LANES), lambda bi, i: (bi, 0, i, 0)),
           full((1, WIDTH_A)), full((1, hw)), full((WIDTH_A, d)), full((hw, d)),
           full((1, d)), full((d, d)), memspec, memspec, full((d, d)),
           full((1, d)), full((d, LANES)), full((1, LANES))],
        out_specs=[row(d), pl.BlockSpec((tm, d // LANES, LANES), lambda bi, i: (bi * (s // tm) + i, 0, 0)),
                   row(LANES)],
        out_shape=[jax.ShapeDtypeStruct((b, s, d), F32),
                   jax.ShapeDtypeStruct((b * s, d // LANES, LANES), F32),
                   jax.ShapeDtypeStruct((b, s, LANES), F32)],
        scratch_shapes=[pltpu.VMEM((WIDTH_A // LANES, tm, LANES), F32)] * 4,
        compiler_params=pltpu.CompilerParams(
            dimension_semantics=("parallel", "parallel"), vmem_limit_bytes=VMEM_LIMIT),
        name="post",
    )(x, *o_pats, *lse_pats, ob, g_out_a, g_out_b_pad, w_o_a, w_o_b_pad, g_xattn, w_mq, kmem,
      vmem, w_mo, g_moe, w_router_pad, b_router_pad)


def _route_kernel(logit_ref, gate_ref, eidx_ref, rank_ref, count_ref, carry_ref):
    @pl.when(pl.program_id(0) == 0)
    def _():
        carry_ref[...] = jnp.zeros_like(carry_ref)

    l = logit_ref[...]
    tr = l.shape[0]
    lane_i = lax.broadcasted_iota(jnp.int32, l.shape, 1)
    lane = lane_i.astype(F32)
    vals, idxs = [], []
    for _ in range(TOP_K):
        m = jnp.max(l, axis=1, keepdims=True)
        idx = jnp.min(jnp.where(l == m, lane, float(LANES)), axis=1, keepdims=True)
        vals.append(m)
        idxs.append(idx)
        l = jnp.where(lane == idx, -jnp.inf, l)
    exps = [jnp.exp(v - vals[0]) for v in vals]
    denom = exps[0] + exps[1] + exps[2] + exps[3]
    onehot = jnp.zeros(l.shape, F32)
    for idx in idxs:
        onehot = onehot + jnp.where(lane == idx, 1.0, 0.0)
    r = lax.broadcasted_iota(jnp.int32, (tr, tr), 0)
    c = lax.broadcasted_iota(jnp.int32, (tr, tr), 1)
    tri = jnp.where(c < r, 1.0, 0.0).astype(BF16)
    before = _dot(tri, onehot.astype(BF16)) + carry_ref[...]
    gate = jnp.zeros(l.shape, F32)
    eidx = jnp.zeros(l.shape, jnp.int32)
    rank = jnp.zeros(l.shape, jnp.int32)
    for k in range(TOP_K):
        rk = jnp.sum(jnp.where(lane == idxs[k], before, 0.0), axis=1, keepdims=True)
        gate = jnp.where(lane_i == k, exps[k] / denom, gate)
        eidx = jnp.where(lane_i == k, idxs[k].astype(jnp.int32), eidx)
        rank = jnp.where(lane_i == k, rk.astype(jnp.int32), rank)
    gate_ref[...] = gate
    eidx_ref[...] = eidx
    rank_ref[...] = rank
    total = carry_ref[...] + jnp.sum(onehot, axis=0, keepdims=True)
    carry_ref[...] = total
    count_ref[...] = total.astype(jnp.int32)


def _route(logits, tr):
    t = logits.shape[0]
    spec = pl.BlockSpec((tr, LANES), lambda i: (i, 0))
    return pl.pallas_call(
        _route_kernel,
        grid=(t // tr,),
        in_specs=[spec],
        out_specs=[spec, spec, spec, pl.BlockSpec((1, LANES), lambda i: (0, 0))],
        out_shape=[jax.ShapeDtypeStruct((t, LANES), F32), jax.ShapeDtypeStruct((t, LANES), jnp.int32),
                   jax.ShapeDtypeStruct((t, LANES), jnp.int32),
                   jax.ShapeDtypeStruct((1, LANES), jnp.int32)],
        scratch_shapes=[pltpu.VMEM((1, LANES), F32)],
        compiler_params=pltpu.CompilerParams(
            dimension_semantics=("arbitrary",), vmem_limit_bytes=VMEM_LIMIT),
        name="route",
    )(logits)


def _dispatch_kernel(pos_ref, ends_ref, x_ref, xs_hbm, zbuf, sem, zsem):
    c = pl.program_id(0)
    blk = zbuf.shape[0]

    @pl.when(c == 0)
    def _():
        zbuf[...] = jnp.zeros_like(zbuf)

        def zero_copy(start):
            return pltpu.make_async_copy(zbuf, xs_hbm.at[pl.ds(start, blk)], zsem)

        used_end = ends_ref[N_EXPERTS]
        starts = [(ends_ref[e + 1] > ends_ref[e], ends_ref[e + 1] - blk) for e in range(N_EXPERTS)]
        starts += [(used_end + (j + 1) * blk <= xs_hbm.shape[0], used_end + j * blk)
                   for j in range(N_EXPERTS)]
        for cond, start in starts:
            @pl.when(cond)
            def _():
                zero_copy(start).start()
        for cond, start in starts:
            @pl.when(cond)
            def _():
                zero_copy(start).wait()

    def body(t, carry):
        tok = c * DISPATCH_CHUNK + t
        for k in range(TOP_K):
            pltpu.make_async_copy(x_ref.at[t], xs_hbm.at[pos_ref[tok * TOP_K + k]], sem).start()
        return carry
    lax.fori_loop(0, DISPATCH_CHUNK, body, 0, unroll=4)

    for k in range(TOP_K):
        pltpu.make_async_copy(x_ref, xs_hbm.at[pl.ds(0, DISPATCH_CHUNK)], sem).wait()


def _dispatch(pos, ends0, xn3, n_rows):
    grid_spec = pltpu.PrefetchScalarGridSpec(
        num_scalar_prefetch=2,
        grid=(xn3.shape[0] // DISPATCH_CHUNK,),
        in_specs=[pl.BlockSpec((DISPATCH_CHUNK,) + xn3.shape[1:], lambda c, pos, ends: (c, 0, 0))],
        out_specs=pl.BlockSpec(memory_space=pl.ANY),
        scratch_shapes=[pltpu.VMEM((EXPERT_BLOCK,) + xn3.shape[1:], F32),
                        pltpu.SemaphoreType.DMA(()), pltpu.SemaphoreType.DMA(())],
    )
    return pl.pallas_call(
        _dispatch_kernel,
        grid_spec=grid_spec,
        out_shape=jax.ShapeDtypeStruct((n_rows,) + xn3.shape[1:], F32),
        compiler_params=pltpu.CompilerParams(
            dimension_semantics=("arbitrary",), vmem_limit_bytes=VMEM_LIMIT),
        name="dispatch",
    )(pos, ends0, xn3)


def _expert_kernel(be_ref, nused_ref, xs_ref, wgu_ref, bgu_ref, wd_ref, bd_ref, y_ref, wgu_bf, wd_bf):
    i = pl.program_id(0)
    n_chunk = xs_ref.shape[1]

    @pl.when(i < nused_ref[0])
    def _():
        @pl.when(jnp.logical_or(i == 0, be_ref[i] != be_ref[jnp.maximum(i - 1, 0)]))
        def _():
            wgu_bf[...] = wgu_ref[0].astype(BF16)
            wd_bf[...] = wd_ref[0].astype(BF16)

        x = jnp.concatenate([xs_ref[:, c, :] for c in range(n_chunk)], axis=1).astype(BF16)
        gu = _dot(x, wgu_bf[...]) + bgu_ref[0]
        de = gu.shape[1] // 2
        gate = jnp.minimum(gu[:, :de], SWIGLU_LIMIT)
        up = jnp.clip(gu[:, de:], -SWIGLU_LIMIT, SWIGLU_LIMIT)
        hmid = (up + 1.0) * (gate * jax.nn.sigmoid(SWIGLU_ALPHA * gate))
        out = _dot(hmid.astype(BF16), wd_bf[...]) + bd_ref[0]
        for c in range(n_chunk):
            y_ref[:, c, :] = out[:, c * LANES:(c + 1) * LANES]

    @pl.when(i >= nused_ref[0])
    def _():
        y_ref[...] = jnp.zeros_like(y_ref)


def _experts(block_expert, n_used, xs3, w_gate_up, b_gate_up, w_down, b_down):
    n_rows, n_chunk, _ = xs3.shape
    d = n_chunk * LANES
    blk = EXPERT_BLOCK
    n_blk = n_rows // blk
    de2 = w_gate_up.shape[2]
    grid_spec = pltpu.PrefetchScalarGridSpec(
        num_scalar_prefetch=2,
        grid=(n_blk,),
        in_specs=[pl.BlockSpec((blk, n_chunk, LANES),
                               lambda i, be, nu: (jnp.minimum(i, nu[0] - 1), 0, 0)),
                  pl.BlockSpec((1, d, de2), lambda i, be, nu: (be[i], 0, 0)),
                  pl.BlockSpec((1, 1, de2), lambda i, be, nu: (be[i], 0, 0)),
                  pl.BlockSpec((1, de2 // 2, d), lambda i, be, nu: (be[i], 0, 0)),
                  pl.BlockSpec((1, 1, d), lambda i, be, nu: (be[i], 0, 0))],
        out_specs=pl.BlockSpec((blk, n_chunk, LANES), lambda i, be, nu: (i, 0, 0)),
        scratch_shapes=[pltpu.VMEM((d, de2), BF16), pltpu.VMEM((de2 // 2, d), BF16)],
    )
    return pl.pallas_call(
        _expert_kernel,
        grid_spec=grid_spec,
        out_shape=jax.ShapeDtypeStruct((n_rows, n_chunk, LANES), F32),
        compiler_params=pltpu.CompilerParams(
            dimension_semantics=("arbitrary",), vmem_limit_bytes=VMEM_LIMIT),
        name="experts",
    )(block_expert, n_used, xs3, w_gate_up, b_gate_up.reshape(N_EXPERTS, 1, de2),
      w_down, b_down.reshape(N_EXPERTS, 1, d))


def _combine_kernel(pos_ref, h_ref, gate_ref, gfin_ref, y_hbm, out_ref, buf, sems):
    i = pl.program_id(0)
    n = pl.num_programs(0)
    slot = lax.rem(i, 2)
    tc, n_chunk = buf.shape[2], buf.shape[3]

    def issue(tile, s):
        def body(r, carry):
            for k in range(TOP_K):
                row = pos_ref[(tile * tc + r) * TOP_K + k]
                pltpu.make_async_copy(y_hbm.at[row], buf.at[s, k, r], sems.at[s]).start()
            return carry
        lax.fori_loop(0, tc, body, 0, unroll=4)

    @pl.when(i == 0)
    def _():
        issue(0, 0)

    @pl.when(i + 1 < n)
    def _():
        issue(i + 1, 1 - slot)

    for k in range(TOP_K):
        pltpu.make_async_copy(y_hbm.at[pl.ds(0, tc)], buf.at[slot, k], sems.at[slot]).wait()
    gate = gate_ref[...]
    pieces = []
    for c in range(n_chunk):
        y = h_ref[:, c * LANES:(c + 1) * LANES]
        for k in range(TOP_K):
            y = y + gate[:, k:k + 1] * buf[slot, k, :, c, :]
        pieces.append(y)
    out_ref[...] = _rms(jnp.concatenate(pieces, axis=1), gfin_ref[...])


def _combine(pos, h2, gates, g_final, y3, tc):
    t, d = h2.shape
    n_chunk = y3.shape[1]
    grid_spec = pltpu.PrefetchScalarGridSpec(
        num_scalar_prefetch=1,
        grid=(t // tc,),
        in_specs=[pl.BlockSpec((tc, d), lambda i, pos: (i, 0)),
                  pl.BlockSpec((tc, LANES), lambda i, pos: (i, 0)),
                  pl.BlockSpec((1, d), lambda i, pos: (0, 0)),
                  pl.BlockSpec(memory_space=pl.ANY)],
        out_specs=pl.BlockSpec((tc, d), lambda i, pos: (i, 0)),
        scratch_shapes=[pltpu.VMEM((2, TOP_K, tc, n_chunk, LANES), F32),
                        pltpu.SemaphoreType.DMA((2,))],
    )
    return pl.pallas_call(
        _combine_kernel,
        grid_spec=grid_spec,
        out_shape=jax.ShapeDtypeStruct((t, d), F32),
        compiler_params=pltpu.CompilerParams(
            dimension_semantics=("arbitrary",), vmem_limit_bytes=VMEM_LIMIT),
        name="combine",
    )(pos, h2, gates, g_final, y3)


def _pad_heads(w, n_heads, width, offset=0):
    k = w.shape[0]
    w = w.reshape(k, n_heads, width)
    w = jnp.pad(w, ((0, 0), (0, 0), (offset, LANES - width - offset)))
    return w.reshape(k, n_heads * LANES)


def _rot_cols(w):
    half = w.shape[-1] // 2
    return jnp.concatenate([-w[..., half:], w[..., :half]], axis=-1)


def kernel(x, mem, positions, g_mix, w_in, g_q_a, w_q_b, g_kv_a, w_kv_b, rel_bias, g_out_a, g_out_b,
           w_o, g_xattn, g_mem, w_mq, w_mkv, w_mo, g_moe, w_router, b_router, w_gate_up, b_gate_up,
           w_down, b_down, g_final):
    b, s, d = x.shape
    t = b * s
    assert g_mix.shape[0] == 1, "single-layer block: the final norm is fused into the last stage"

    inv_freq = ROPE_THETA ** (-jnp.arange(0, QK_ROPE_DIM, 2, dtype=F32) / QK_ROPE_DIM)
    ang = positions.astype(F32)[..., None] * inv_freq
    cos, sin = jnp.cos(ang), jnp.sin(ang)
    pad_lo, pad_hi = QK_NOPE_DIM, LANES - QK_NOPE_DIM - QK_ROPE_DIM
    cos128 = jnp.concatenate([jnp.ones((b, s, pad_lo), F32), cos, cos, jnp.ones((b, s, pad_hi), F32)], -1)
    sin128 = jnp.concatenate([jnp.zeros((b, s, pad_lo), F32), sin, sin, jnp.zeros((b, s, pad_hi), F32)], -1)
    tbl = jnp.pad(rel_bias.T.astype(F32), ((0, 0), (0, LANES - NUM_BUCKETS)))

    h = x
    for l in range(1):
        c0 = 3 * WIDTH_A + Q_LORA_RANK + KV_LORA_RANK
        w_kpe = w_in[l][:, c0:c0 + QK_ROPE_DIM]
        place = lambda w: jnp.pad(w, ((0, 0), (QK_NOPE_DIM, LANES - QK_NOPE_DIM - QK_ROPE_DIM)))
        w_in_ext = jnp.concatenate([w_in[l][:, :c0], place(w_kpe), place(_rot_cols(w_kpe))], 1).astype(BF16)
        dq = QK_NOPE_DIM + QK_ROPE_DIM
        wq3 = w_q_b[l].reshape(Q_LORA_RANK, N_HEADS_B, dq)
        wq = _pad_heads(wq3.reshape(Q_LORA_RANK, -1), N_HEADS_B, dq).astype(BF16)
        wq_rot3 = jnp.concatenate([jnp.zeros_like(wq3[..., :QK_NOPE_DIM]), _rot_cols(wq3[..., QK_NOPE_DIM:])], -1)
        wqr = _pad_heads(wq_rot3.reshape(Q_LORA_RANK, -1), N_HEADS_B, dq).astype(BF16)
        wkv3 = w_kv_b[l].reshape(KV_LORA_RANK, N_HEADS_B, QK_NOPE_DIM + V_DIM_B)
        wk = _pad_heads(wkv3[..., :QK_NOPE_DIM].reshape(KV_LORA_RANK, -1), N_HEADS_B, QK_NOPE_DIM).astype(BF16)
        wv = _pad_heads(wkv3[..., QK_NOPE_DIM:].reshape(KV_LORA_RANK, -1), N_HEADS_B, V_DIM_B).astype(BF16)

        *views, qm, km, vm = _proj(h, cos128, sin128, g_mix[l][None], w_in_ext, g_q_a[l][None], wq, wqr,
                                   g_kv_a[l][None], wk, wv, tm=256)
        pats = _dilated_all(views, positions, tbl)
        ob = _mla(qm, km, vm, tq=1024, tk=512)
        kmem, vmem = _memkv(mem, g_mem[l][None], w_mkv[l].astype(BF16))

        g_out_b_pad = _pad_heads(g_out_b[l][None], N_HEADS_B, V_DIM_B)
        w_o_b_pad = _pad_heads(w_o[l][WIDTH_A:].T, N_HEADS_B, V_DIM_B).T.astype(BF16)
        w_mq_s = (w_mq[l] * ((d // N_HEADS_MEM) ** -0.5)).astype(BF16)
        w_router_pad = jnp.pad(w_router[l], ((0, 0), (0, LANES - N_EXPERTS)))
        b_router_pad = jnp.pad(b_router[l][None], ((0, 0), (0, LANES - N_EXPERTS)), constant_values=NEG_INF)
        h2, xn3, logits = _post(h, pats[0::2], pats[1::2], ob, g_out_a[l][None],
                               g_out_b_pad, w_o[l][:WIDTH_A].astype(BF16), w_o_b_pad, g_xattn[l][None],
                               w_mq_s, kmem, vmem, w_mo[l].astype(BF16), g_moe[l][None], w_router_pad,
                               b_router_pad, tm=256)

        gates, eidx, rank, counts = _route(logits.reshape(t, LANES), tr=256)
        counts = counts[0, :N_EXPERTS]
        blk = EXPERT_BLOCK
        padded = ((counts + blk - 1) // blk) * blk
        ends = jnp.cumsum(padded)
        pad_start = ends - padded
        eidx4, rank4 = eidx[:, :TOP_K], rank[:, :TOP_K]
        expert_ids = jnp.arange(N_EXPERTS, dtype=jnp.int32)
        pos = (jnp.sum(jnp.where(eidx4[..., None] == expert_ids, pad_start, 0), axis=-1)
               + rank4).astype(jnp.int32).reshape(-1)
        n_blk = t * TOP_K // blk + N_EXPERTS
        block_expert = jnp.minimum(
            jnp.sum(ends[None, :] <= (jnp.arange(n_blk) * blk)[:, None], axis=1),
            N_EXPERTS - 1).astype(jnp.int32)
        n_used = (ends[-1] // blk).astype(jnp.int32)[None]
        ends0 = jnp.concatenate([jnp.zeros((1,), jnp.int32), ends.astype(jnp.int32)])
        xs3 = _dispatch(pos, ends0, xn3, n_blk * blk)
        y3 = _experts(block_expert, n_used, xs3, w_gate_up[l], b_gate_up[l], w_down[l], b_down[l])
        out = _combine(pos, h2.reshape(t, d), gates, g_final[None], y3, tc=128)
        h = out.reshape(b, s, d)
    return h
```

```python
import functools
import math

import jax
import jax.numpy as jnp
from jax import lax
from jax.experimental import pallas as pl
from jax.experimental.pallas import tpu as pltpu

F32 = jnp.float32
BF16 = jnp.bfloat16

LANES = 128
EPS = 1e-6
NEG_INF = -1e30
LOG2_E = math.log2(math.e)

N_HEADS_A = 8
HEAD_DIM_A = 64
WIDTH_A = N_HEADS_A * HEAD_DIM_A
DILATED_PATTERNS = ((128, 1), (512, 4), (2048, 16))
N_HEADS_B = 8
QK_NOPE_DIM = 64
QK_ROPE_DIM = 32
V_DIM_B = 64
Q_LORA_RANK = 256
KV_LORA_RANK = 128
ROPE_THETA = 10000.0
NUM_BUCKETS = 32
MAX_DISTANCE = 1024
N_HEADS_MEM = 4
N_EXPERTS = 32
TOP_K = 4
SWIGLU_LIMIT = 7.0
SWIGLU_ALPHA = 1.702

Q_BLOCK = 128
K_WINDOW = 256
HALF_WINDOW = (K_WINDOW - Q_BLOCK) // 2
EXPERT_BLOCK = 256
DISPATCH_CHUNK = 512
VMEM_LIMIT = 56 * 1024 * 1024


def _rms(x, g):
    return x * lax.rsqrt(jnp.mean(x * x, axis=-1, keepdims=True) + EPS) * g


def _dot(a, b):
    return jnp.dot(a, b, preferred_element_type=F32)


def _dot_nt(a, b):
    return lax.dot_general(a, b, (((1,), (1,)), ((), ())), preferred_element_type=F32)


def _load_token_rows(ref, n_tok):
    n_chunk = ref.shape[0] // n_tok
    return jnp.concatenate([ref[pl.ds(c, n_tok, stride=n_chunk), :] for c in range(n_chunk)], axis=1)


def _store_token_rows(ref, value):
    n_tok = value.shape[0]
    n_chunk = value.shape[1] // LANES
    for c in range(n_chunk):
        ref[pl.ds(c, n_tok, stride=n_chunk), :] = value[:, c * LANES:(c + 1) * LANES]


def _proj_kernel(x_ref, cos_ref, sin_ref, gmix_ref, win_ref, gq_ref, wq_ref, wqr_ref,
                 gkv_ref, wk_ref, wv_ref,
                 q1_ref, k1_ref, v1_ref, q4_ref, k4_ref, v4_ref, q16_ref, k16_ref, v16_ref,
                 qm_ref, km_ref, vm_ref, za_scr):
    xn = _rms(x_ref[0], gmix_ref[...])
    z = _dot(xn.astype(BF16), win_ref[...])
    tm = z.shape[0]
    c0 = 3 * WIDTH_A
    per_group = WIDTH_A // LANES
    for c in range(3 * per_group):
        chunk = z[:, c * LANES:(c + 1) * LANES]
        za_scr[c] = chunk * (HEAD_DIM_A ** -0.5 * LOG2_E) if c < per_group else chunk
    for dil, refs in ((1, (q1_ref, k1_ref, v1_ref)), (4, (q4_ref, k4_ref, v4_ref)),
                      (16, (q16_ref, k16_ref, v16_ref))):
        for r in range(dil):
            for c in range(3 * per_group):
                rows = za_scr[c, pl.ds(r, tm // dil, stride=dil), :].astype(BF16)
                col = r * WIDTH_A + (c % per_group) * LANES
                refs[c // per_group][0, :, col:col + LANES] = rows
    zqn = _rms(z[:, c0:c0 + Q_LORA_RANK], gq_ref[...]).astype(BF16)
    zkvn = _rms(z[:, c0 + Q_LORA_RANK:c0 + Q_LORA_RANK + KV_LORA_RANK], gkv_ref[...]).astype(BF16)
    c1 = c0 + Q_LORA_RANK + KV_LORA_RANK
    cos = cos_ref[0]
    sin = sin_ref[0]
    kpe = z[:, c1:c1 + LANES] * cos + z[:, c1 + LANES:c1 + 2 * LANES] * sin
    q = _dot(zqn, wq_ref[...])
    qr = _dot(zqn, wqr_ref[...])
    kn = _dot(zkvn, wk_ref[...])
    vv = _dot(zkvn, wv_ref[...])
    lane = lax.broadcasted_iota(jnp.int32, (1, LANES), 1)
    ones_col = jnp.where(lane == V_DIM_B, 1.0, 0.0).astype(F32)
    scale = (QK_NOPE_DIM + QK_ROPE_DIM) ** -0.5 * LOG2_E
    for h in range(N_HEADS_B):
        sl = slice(h * LANES, (h + 1) * LANES)
        qm_ref[0, h] = ((q[:, sl] * cos + qr[:, sl] * sin) * scale).astype(BF16)
        km_ref[0, h] = (kn[:, sl] + kpe).astype(BF16)
        vm_ref[0, h] = (vv[:, sl] + ones_col).astype(BF16)


def _proj(x, cos128, sin128, g_mix, w_in_ext, g_q, wq, wqr, g_kv, wk, wv, tm):
    b, s, d = x.shape
    n_in = w_in_ext.shape[1]
    hw = N_HEADS_B * LANES
    full = lambda shape: pl.BlockSpec(shape, lambda bi, i: (0,) * len(shape))
    row = lambda w: pl.BlockSpec((1, tm, w), lambda bi, i: (bi, i, 0))
    head = pl.BlockSpec((1, N_HEADS_B, tm, LANES), lambda bi, i: (bi, 0, i, 0))
    dils = [dil for _, dil in DILATED_PATTERNS]
    view_specs = [pl.BlockSpec((1, tm // dil, dil * WIDTH_A), lambda bi, i: (bi, i, 0))
                  for dil in dils for _ in range(3)]
    view_shapes = [jax.ShapeDtypeStruct((b, s // dil, dil * WIDTH_A), BF16)
                   for dil in dils for _ in range(3)]
    return pl.pallas_call(
        _proj_kernel,
        grid=(b, s // tm),
        in_specs=[row(d), row(LANES), row(LANES), full((1, d)), full((d, n_in)),
                  full((1, Q_LORA_RANK)), full((Q_LORA_RANK, hw)), full((Q_LORA_RANK, hw)),
                  full((1, KV_LORA_RANK)), full((KV_LORA_RANK, hw)), full((KV_LORA_RANK, hw))],
        out_specs=view_specs + [head, head, head],
        out_shape=view_shapes + [jax.ShapeDtypeStruct((b, N_HEADS_B, s, LANES), BF16)] * 3,
        scratch_shapes=[pltpu.VMEM((3 * WIDTH_A // LANES, tm, LANES), F32)],
        compiler_params=pltpu.CompilerParams(
            dimension_semantics=("parallel", "parallel"), vmem_limit_bytes=VMEM_LIMIT),
        name="proj",
    )(x, cos128, sin128, g_mix, w_in_ext, g_q, wq, wqr, g_kv, wk, wv)


_LOG_BUCKET_STARTS = tuple(
    next(n for n in range(8, 4096)
         if int(math.log(n / 8) / math.log(MAX_DISTANCE / 8) * 8) >= t)
    for t in range(1, 8))


def _rel_bucket(rel):
    n = jnp.abs(rel)
    large = jnp.full(rel.shape, NUM_BUCKETS // 4, jnp.int32)
    for start in _LOG_BUCKET_STARTS:
        large = large + jnp.where(n >= start, 1, 0)
    mag = jnp.where(n < NUM_BUCKETS // 4, n, large)
    return mag + jnp.where(rel > 0, NUM_BUCKETS // 2, 0)


def _window_start(j, sub_len):
    return jnp.clip(j * Q_BLOCK - HALF_WINDOW, 0, sub_len - K_WINDOW)


def _bias_kernel(off_ref, qpos_ref, kpos_ref, tbl_ref, out_ref):
    rel = kpos_ref[0] - qpos_ref[0]
    bucket = _rel_bucket(rel)
    delta = (off_ref[pl.program_id(0)] + lax.broadcasted_iota(jnp.int32, (1, K_WINDOW), 1)
             - lax.broadcasted_iota(jnp.int32, (Q_BLOCK, 1), 0))
    valid = jnp.abs(delta) <= HALF_WINDOW
    for h in range(N_HEADS_A):
        tbl = jnp.broadcast_to(tbl_ref[h:h + 1, :], (Q_BLOCK, LANES))
        bias = jnp.concatenate(
            [jnp.take_along_axis(tbl, bucket[:, c * LANES:(c + 1) * LANES], axis=1)
             for c in range(K_WINDOW // LANES)], axis=1)
        out_ref[0, h] = jnp.where(valid, bias * LOG2_E, NEG_INF)


def _bias_tiles(offs, qpos, kpos, tbl):
    n = offs.shape[0]
    grid_spec = pltpu.PrefetchScalarGridSpec(
        num_scalar_prefetch=1,
        grid=(n,),
        in_specs=[pl.BlockSpec((1, Q_BLOCK, 1), lambda t, off: (t, 0, 0)),
                  pl.BlockSpec((1, 1, K_WINDOW), lambda t, off: (t, 0, 0)),
                  pl.BlockSpec((N_HEADS_A, LANES), lambda t, off: (0, 0))],
        out_specs=pl.BlockSpec((1, N_HEADS_A, Q_BLOCK, K_WINDOW), lambda t, off: (t, 0, 0, 0)),
    )
    return pl.pallas_call(
        _bias_kernel,
        grid_spec=grid_spec,
        out_shape=jax.ShapeDtypeStruct((n, N_HEADS_A, Q_BLOCK, K_WINDOW), F32),
        compiler_params=pltpu.CompilerParams(
            dimension_semantics=("arbitrary",), vmem_limit_bytes=VMEM_LIMIT),
        name="bias_tiles",
    )(offs, qpos, kpos, tbl)


def _dilated_kernel(q_ref, k_ref, v_ref, bias_ref, o_ref, lse_ref, s_scr, p_scr, *, sub_len):
    j = pl.program_id(2)
    kstart = pl.multiple_of(_window_start(j, sub_len), HALF_WINDOW)
    q = q_ref[0]
    kw = k_ref[0, pl.ds(kstart, K_WINDOW), :]
    vw = v_ref[0, pl.ds(kstart, K_WINDOW), :]
    first = lax.broadcasted_iota(jnp.int32, (1, LANES), 1) < HEAD_DIM_A
    pair = lambda h: slice((h // 2) * LANES, (h // 2 + 1) * LANES)
    for h in range(N_HEADS_A):
        own = first if h % 2 == 0 else jnp.logical_not(first)
        qh = jnp.where(own, q[:, pair(h)], jnp.zeros_like(q[:, pair(h)]))
        s_scr[h] = _dot_nt(qh, kw[:, pair(h)]) + bias_ref[0, h]
    stats = []
    for h in range(N_HEADS_A):
        s = s_scr[h]
        m = jnp.max(s, axis=1, keepdims=True)
        e = jnp.exp2(s - m)
        l = jnp.sum(e, axis=1, keepdims=True)
        p_scr[h] = e.astype(BF16)
        stats.append((1.0 / l, m + jnp.log2(l)))
    for h in range(0, N_HEADS_A, 2):
        (r0, lse0), (r1, lse1) = stats[h], stats[h + 1]
        o0 = _dot(p_scr[h], vw[:, pair(h)])
        o1 = _dot(p_scr[h + 1], vw[:, pair(h)])
        o_ref[0, :, pair(h)] = jnp.where(first, o0 * r0, o1 * r1)
        lse_ref[0, :, pair(h)] = jnp.where(first, lse0, lse1)


def _dilated(qv, kv, vv, bias, bias_index, dil):
    b, sub_len, _ = qv.shape
    w = WIDTH_A
    assert sub_len >= K_WINDOW and sub_len % Q_BLOCK == 0
    nq = sub_len // Q_BLOCK
    qspec = pl.BlockSpec((1, Q_BLOCK, w), lambda bi, r, j: (bi, j, r))
    kvspec = pl.BlockSpec((1, sub_len, w), lambda bi, r, j: (bi, 0, r))
    bspec = pl.BlockSpec((1, N_HEADS_A, Q_BLOCK, K_WINDOW),
                         lambda bi, r, j: (bias_index(bi, r, j, nq), 0, 0, 0))
    return pl.pallas_call(
        functools.partial(_dilated_kernel, sub_len=sub_len),
        grid=(b, dil, nq),
        in_specs=[qspec, kvspec, kvspec, bspec],
        out_specs=[qspec, qspec],
        out_shape=[jax.ShapeDtypeStruct((b, sub_len, dil * w), F32)] * 2,
        scratch_shapes=[pltpu.VMEM((N_HEADS_A, Q_BLOCK, K_WINDOW), F32),
                        pltpu.VMEM((N_HEADS_A, Q_BLOCK, K_WINDOW), BF16)],
        compiler_params=pltpu.CompilerParams(
            dimension_semantics=("parallel", "parallel", "arbitrary"),
            vmem_limit_bytes=VMEM_LIMIT),
        name=f"dilated{dil}",
    )(qv, kv, vv, bias)


def _dilated_all(views, positions, tbl):
    b, s = positions.shape
    dils = [dil for _, dil in DILATED_PATTERNS]
    consecutive = jnp.all(positions[:, 1:] - positions[:, :-1] == 1)

    def run(bias, index_fns):
        outs = []
        for g, dil in enumerate(dils):
            qv, kv, vv = views[3 * g:3 * g + 3]
            outs.extend(_dilated(qv, kv, vv, bias, index_fns[g], dil))
        return tuple(outs)

    def shared_tiles():
        offs, qpos, kpos, fns = [], [], [], []
        for g, dil in enumerate(dils):
            for off in (0, -HALF_WINDOW, -2 * HALF_WINDOW):
                offs.append(off)
                qpos.append(dil * jnp.arange(Q_BLOCK, dtype=jnp.int32))
                kpos.append(dil * (off + jnp.arange(K_WINDOW, dtype=jnp.int32)))
            fns.append(lambda bi, r, j, nq, g=g:
                       3 * g + jnp.where(j == 0, 0, jnp.where(j == nq - 1, 2, 1)))
        bias = _bias_tiles(jnp.array(offs, jnp.int32), jnp.stack(qpos)[:, :, None],
                           jnp.stack(kpos)[:, None, :], tbl)
        return run(bias, fns)

    def per_block_tiles():
        offs, qpos, kpos, fns = [], [], [], []
        base = 0
        for g, dil in enumerate(dils):
            sub_len = s // dil
            nq = sub_len // Q_BLOCK
            pos_t = positions.reshape(b, sub_len, dil).transpose(0, 2, 1).reshape(b * dil, sub_len)
            starts = _window_start(jnp.arange(nq), sub_len)
            win = starts[:, None] + jnp.arange(K_WINDOW)[None, :]
            qpos.append(pos_t.reshape(b * dil * nq, Q_BLOCK))
            kpos.append(pos_t[:, win].reshape(b * dil * nq, K_WINDOW))
            offs.append(jnp.tile(starts - jnp.arange(nq) * Q_BLOCK, b * dil))
            fns.append(lambda bi, r, j, nq, base=base, dil=dil: base + (bi * dil + r) * nq + j)
            base += b * dil * nq
        bias = _bias_tiles(jnp.concatenate(offs).astype(jnp.int32),
                           jnp.concatenate(qpos)[:, :, None], jnp.concatenate(kpos)[:, None, :], tbl)
        return run(bias, fns)

    return lax.cond(consecutive, shared_tiles, per_block_tiles)


def _mla_kernel(q_ref, k_ref, v_ref, o_ref, *, tk):
    q = q_ref[0, 0]
    tq = q.shape[0]
    n_kv = k_ref.shape[2] // tk

    def body(i, carry):
        m, acc = carry
        start = pl.multiple_of(i * tk, tk)
        k = k_ref[0, 0, pl.ds(start, tk), :]
        v = v_ref[0, 0, pl.ds(start, tk), :]
        s = _dot_nt(q, k)
        m_new = jnp.maximum(m, jnp.max(s, axis=1, keepdims=True))
        p = jnp.exp2(s - m_new)
        acc = jnp.exp2(m - m_new) * acc + _dot(p.astype(BF16), v)
        return m_new, acc

    m0 = jnp.full((tq, 1), NEG_INF, F32)
    _, acc = lax.fori_loop(0, n_kv, body, (m0, jnp.zeros((tq, LANES), F32)), unroll=True)
    lane = lax.broadcasted_iota(jnp.int32, (1, LANES), 1)
    denom = jnp.sum(jnp.where(lane == V_DIM_B, acc, 0.0), axis=1, keepdims=True)
    o_ref[0, 0] = jnp.where(lane < V_DIM_B, acc / denom, 0.0).astype(BF16)


def _mla(qm, km, vm, tq, tk):
    b, nh, s, _ = qm.shape
    qspec = pl.BlockSpec((1, 1, tq, LANES), lambda bi, h, i: (bi, h, i, 0))
    kvspec = pl.BlockSpec((1, 1, s, LANES), lambda bi, h, i: (bi, h, 0, 0))
    return pl.pallas_call(
        functools.partial(_mla_kernel, tk=tk),
        grid=(b, nh, s // tq),
        in_specs=[qspec, kvspec, kvspec],
        out_specs=qspec,
        out_shape=jax.ShapeDtypeStruct((b, nh, s, LANES), BF16),
        compiler_params=pltpu.CompilerParams(
            dimension_semantics=("parallel", "parallel", "arbitrary"),
            vmem_limit_bytes=VMEM_LIMIT),
        name="mla",
    )(qm, km, vm)


def _memkv_kernel(mem_ref, g_ref, w_ref, k_ref, v_ref):
    d = mem_ref.shape[2]
    kv = _dot(_rms(mem_ref[0], g_ref[...]).astype(BF16), w_ref[...])
    k_ref[0] = kv[:, :d].astype(BF16)
    v_ref[0] = kv[:, d:].astype(BF16)


def _memkv(mem, g_mem, w_mkv):
    b, m, d = mem.shape
    spec = pl.BlockSpec((1, m, d), lambda bi: (bi, 0, 0))
    return pl.pallas_call(
        _memkv_kernel,
        grid=(b,),
        in_specs=[spec, pl.BlockSpec((1, d), lambda bi: (0, 0)),
                  pl.BlockSpec((d, 2 * d), lambda bi: (0, 0))],
        out_specs=[spec, spec],
        out_shape=[jax.ShapeDtypeStruct((b, m, d), BF16)] * 2,
        compiler_params=pltpu.CompilerParams(
            dimension_semantics=("parallel",), vmem_limit_bytes=VMEM_LIMIT),
        name="memkv",
    )(mem, g_mem, w_mkv)


def _post_kernel(x_ref, o1_ref, o2_ref, o3_ref, l1_ref, l2_ref, l3_ref, ob_ref,
                 goa_ref, gob_ref, woa_ref, wob_ref, gx_ref, wmq_ref, km_ref, vm_ref, wmo_ref,
                 gmoe_ref, wr_ref, br_ref,
                 h_ref, xn_ref, logit_ref, *nat_scr):
    tm = x_ref.shape[1]

    def natural(view_ref, scr):
        dil = view_ref.shape[2] // WIDTH_A
        if dil == 1:
            return view_ref[0]
        n_chunk = WIDTH_A // LANES
        for r in range(dil):
            for c in range(n_chunk):
                col = r * WIDTH_A + c * LANES
                scr[c, pl.ds(r, tm // dil, stride=dil), :] = view_ref[0, :, col:col + LANES]
        return jnp.concatenate([scr[c] for c in range(n_chunk)], axis=1)

    o1, o2, o3 = o1_ref[0], natural(o2_ref, nat_scr[0]), natural(o3_ref, nat_scr[1])
    l1, l2, l3 = l1_ref[0], natural(l2_ref, nat_scr[2]), natural(l3_ref, nat_scr[3])
    mx = jnp.maximum(jnp.maximum(l1, l2), l3)
    w1, w2, w3 = jnp.exp2(l1 - mx), jnp.exp2(l2 - mx), jnp.exp2(l3 - mx)
    oa = (w1 * o1 + w2 * o2 + w3 * o3) / (w1 + w2 + w3)
    oan = _rms(oa, goa_ref[...]).astype(BF16)
    ob = jnp.concatenate([ob_ref[0, h] for h in range(N_HEADS_B)], axis=1).astype(F32)
    ms_b = jnp.sum(ob * ob, axis=1, keepdims=True) * (1.0 / (N_HEADS_B * V_DIM_B))
    obn = (ob * lax.rsqrt(ms_b + EPS) * gob_ref[...]).astype(BF16)
    h1 = x_ref[0] + _dot(oan, woa_ref[...]) + _dot(obn, wob_ref[...])

    hn = _rms(h1, gx_ref[...]).astype(BF16)
    q = _dot(hn, wmq_ref[...]).astype(BF16)
    dh = q.shape[1] // N_HEADS_MEM
    heads = []
    for h in range(N_HEADS_MEM):
        sl = slice(h * dh, (h + 1) * dh)
        s = _dot_nt(q[:, sl], km_ref[0, :, sl])
        e = jnp.exp(s - jnp.max(s, axis=1, keepdims=True))
        p = e / jnp.sum(e, axis=1, keepdims=True)
        heads.append(_dot(p.astype(BF16), vm_ref[0, :, sl]))
    o = jnp.concatenate(heads, axis=1).astype(BF16)
    h2 = h1 + _dot(o, wmo_ref[...])
    h_ref[0] = h2

    xn = _rms(h2, gmoe_ref[...])
    _store_token_rows(xn_ref, xn)
    logit_ref[0] = jnp.dot(xn, wr_ref[...], preferred_element_type=F32,
                           precision=lax.Precision.HIGHEST) + br_ref[...]


def _post(x, o_pats, lse_pats, ob, g_out_a, g_out_b_pad, w_o_a, w_o_b_pad, g_xattn, w_mq, kmem,
          vmem, w_mo, g_moe, w_router_pad, b_router_pad, tm):
    b, s, d = x.shape
    m = kmem.shape[1]
    full = lambda shape: pl.BlockSpec(shape, lambda bi, i: (0,) * len(shape))
    row = lambda w: pl.BlockSpec((1, tm, w), lambda bi, i: (bi, i, 0))
    memspec = pl.BlockSpec((1, m, d), lambda bi, i: (bi, 0, 0))
    hw = N_HEADS_B * LANES
    views = [pl.BlockSpec((1, tm // dil, dil * WIDTH_A), lambda bi, i: (bi, i, 0))
             for _, dil in DILATED_PATTERNS]
    return pl.pallas_call(
        _post_kernel,
        grid=(b, s // tm),
        in_specs=[row(d)] + views + views
        + [pl.BlockSpec((1, N_HEADS_B, tm, LANES), lambda bi, i: (bi, 0, i, 0)),
           full((1, WIDTH_A)), full((1, hw)), full((WIDTH_A, d)), full((hw, d)),
           full((1, d)), full((d, d)), memspec, memspec, full((d, d)),
           full((1, d)), full((d, LANES)), full((1, LANES))],
        out_specs=[row(d), pl.BlockSpec((tm * d // LANES, LANES), lambda bi, i: (bi * (s // tm) + i, 0)),
                   row(LANES)],
        out_shape=[jax.ShapeDtypeStruct((b, s, d), F32),
                   jax.ShapeDtypeStruct((b * s * d // LANES, LANES), F32),
                   jax.ShapeDtypeStruct((b, s, LANES), F32)],
        scratch_shapes=[pltpu.VMEM((WIDTH_A // LANES, tm, LANES), F32)] * 4,
        compiler_params=pltpu.CompilerParams(
            dimension_semantics=("parallel", "parallel"), vmem_limit_bytes=VMEM_LIMIT),
        name="post",
    )(x, *o_pats, *lse_pats, ob, g_out_a, g_out_b_pad, w_o_a, w_o_b_pad, g_xattn, w_mq, kmem,
      vmem, w_mo, g_moe, w_router_pad, b_router_pad)


def _route_kernel(logit_ref, gate_ref, eidx_ref, rank_ref, count_ref, carry_ref):
    @pl.when(pl.program_id(0) == 0)
    def _():
        carry_ref[...] = jnp.zeros_like(carry_ref)

    l = logit_ref[...]
    tr = l.shape[0]
    lane_i = lax.broadcasted_iota(jnp.int32, l.shape, 1)
    lane = lane_i.astype(F32)
    vals, idxs = [], []
    for _ in range(TOP_K):
        m = jnp.max(l, axis=1, keepdims=True)
        idx = jnp.min(jnp.where(l == m, lane, float(LANES)), axis=1, keepdims=True)
        vals.append(m)
        idxs.append(idx)
        l = jnp.where(lane == idx, -jnp.inf, l)
    exps = [jnp.exp(v - vals[0]) for v in vals]
    denom = exps[0] + exps[1] + exps[2] + exps[3]
    onehot = jnp.zeros(l.shape, F32)
    for idx in idxs:
        onehot = onehot + jnp.where(lane == idx, 1.0, 0.0)
    r = lax.broadcasted_iota(jnp.int32, (tr, tr), 0)
    c = lax.broadcasted_iota(jnp.int32, (tr, tr), 1)
    tri = jnp.where(c < r, 1.0, 0.0).astype(BF16)
    before = _dot(tri, onehot.astype(BF16)) + carry_ref[...]
    gate = jnp.zeros(l.shape, F32)
    eidx = jnp.zeros(l.shape, jnp.int32)
    rank = jnp.zeros(l.shape, jnp.int32)
    for k in range(TOP_K):
        rk = jnp.sum(jnp.where(lane == idxs[k], before, 0.0), axis=1, keepdims=True)
        gate = jnp.where(lane_i == k, exps[k] / denom, gate)
        eidx = jnp.where(lane_i == k, idxs[k].astype(jnp.int32), eidx)
        rank = jnp.where(lane_i == k, rk.astype(jnp.int32), rank)
    gate_ref[...] = gate
    eidx_ref[...] = eidx
    rank_ref[...] = rank
    total = carry_ref[...] + jnp.sum(onehot, axis=0, keepdims=True)
    carry_ref[...] = total
    count_ref[...] = total.astype(jnp.int32)


def _route(logits, tr):
    t = logits.shape[0]
    spec = pl.BlockSpec((tr, LANES), lambda i: (i, 0))
    return pl.pallas_call(
        _route_kernel,
        grid=(t // tr,),
        in_specs=[spec],
        out_specs=[spec, spec, spec, pl.BlockSpec((1, LANES), lambda i: (0, 0))],
        out_shape=[jax.ShapeDtypeStruct((t, LANES), F32), jax.ShapeDtypeStruct((t, LANES), jnp.int32),
                   jax.ShapeDtypeStruct((t, LANES), jnp.int32),
                   jax.ShapeDtypeStruct((1, LANES), jnp.int32)],
        scratch_shapes=[pltpu.VMEM((1, LANES), F32)],
        compiler_params=pltpu.CompilerParams(
            dimension_semantics=("arbitrary",), vmem_limit_bytes=VMEM_LIMIT),
        name="route",
    )(logits)


def _dispatch_kernel(pos_ref, ends_ref, x_ref, xs_hbm, zbuf, sem, zsem):
    c = pl.program_id(0)
    n_chunk = x_ref.shape[0] // DISPATCH_CHUNK
    blk_rows = zbuf.shape[0]

    def token(ref, t):
        return ref.at[pl.ds(pl.multiple_of(t * n_chunk, n_chunk), n_chunk)]

    @pl.when(c == 0)
    def _():
        zbuf[...] = jnp.zeros_like(zbuf)

        def zero_copy(start):
            return pltpu.make_async_copy(
                zbuf, xs_hbm.at[pl.ds(pl.multiple_of(start * n_chunk, n_chunk), blk_rows)], zsem)

        blk = blk_rows // n_chunk
        used_end = ends_ref[N_EXPERTS]
        starts = [(ends_ref[e + 1] > ends_ref[e], ends_ref[e + 1] - blk) for e in range(N_EXPERTS)]
        starts += [((used_end + (j + 1) * blk) * n_chunk <= xs_hbm.shape[0], used_end + j * blk)
                   for j in range(N_EXPERTS)]
        for cond, start in starts:
            @pl.when(cond)
            def _():
                zero_copy(start).start()
        for cond, start in starts:
            @pl.when(cond)
            def _():
                zero_copy(start).wait()

    def body(t, carry):
        tok = c * DISPATCH_CHUNK + t
        for k in range(TOP_K):
            pltpu.make_async_copy(token(x_ref, t), token(xs_hbm, pos_ref[tok * TOP_K + k]), sem).start()
        return carry
    lax.fori_loop(0, DISPATCH_CHUNK, body, 0, unroll=4)

    for k in range(TOP_K):
        pltpu.make_async_copy(x_ref, xs_hbm.at[pl.ds(0, x_ref.shape[0])], sem).wait()


def _dispatch(pos, ends0, xn_rows, n_tok, n_slots):
    n_chunk = xn_rows.shape[0] // n_tok
    grid_spec = pltpu.PrefetchScalarGridSpec(
        num_scalar_prefetch=2,
        grid=(n_tok // DISPATCH_CHUNK,),
        in_specs=[pl.BlockSpec((DISPATCH_CHUNK * n_chunk, LANES), lambda c, pos, ends: (c, 0))],
        out_specs=pl.BlockSpec(memory_space=pl.ANY),
        scratch_shapes=[pltpu.VMEM((EXPERT_BLOCK * n_chunk, LANES), F32),
                        pltpu.SemaphoreType.DMA(()), pltpu.SemaphoreType.DMA(())],
    )
    return pl.pallas_call(
        _dispatch_kernel,
        grid_spec=grid_spec,
        out_shape=jax.ShapeDtypeStruct((n_slots * n_chunk, LANES), F32),
        compiler_params=pltpu.CompilerParams(
            dimension_semantics=("arbitrary",), vmem_limit_bytes=VMEM_LIMIT),
        name="dispatch",
    )(pos, ends0, xn_rows)


def _expert_kernel(be_ref, nused_ref, xs_ref, wgu_ref, bgu_ref, wd_ref, bd_ref, y_ref, wgu_bf, wd_bf):
    i = pl.program_id(0)

    @pl.when(i < nused_ref[0])
    def _():
        @pl.when(jnp.logical_or(i == 0, be_ref[i] != be_ref[jnp.maximum(i - 1, 0)]))
        def _():
            wgu_bf[...] = wgu_ref[0].astype(BF16)
            wd_bf[...] = wd_ref[0].astype(BF16)

        x = _load_token_rows(xs_ref, EXPERT_BLOCK).astype(BF16)
        gu = _dot(x, wgu_bf[...]) + bgu_ref[0]
        de = gu.shape[1] // 2
        gate = jnp.minimum(gu[:, :de], SWIGLU_LIMIT)
        up = jnp.clip(gu[:, de:], -SWIGLU_LIMIT, SWIGLU_LIMIT)
        hmid = (up + 1.0) * (gate * jax.nn.sigmoid(SWIGLU_ALPHA * gate))
        _store_token_rows(y_ref, _dot(hmid.astype(BF16), wd_bf[...]) + bd_ref[0])

    @pl.when(i >= nused_ref[0])
    def _():
        y_ref[...] = jnp.zeros_like(y_ref)


def _experts(block_expert, n_used, xs_rows, w_gate_up, b_gate_up, w_down, b_down):
    d, de2 = w_gate_up.shape[1:]
    n_chunk = d // LANES
    blk = EXPERT_BLOCK
    n_blk = xs_rows.shape[0] // (blk * n_chunk)
    grid_spec = pltpu.PrefetchScalarGridSpec(
        num_scalar_prefetch=2,
        grid=(n_blk,),
        in_specs=[pl.BlockSpec((blk * n_chunk, LANES),
                               lambda i, be, nu: (jnp.minimum(i, nu[0] - 1), 0)),
                  pl.BlockSpec((1, d, de2), lambda i, be, nu: (be[i], 0, 0)),
                  pl.BlockSpec((1, 1, de2), lambda i, be, nu: (be[i], 0, 0)),
                  pl.BlockSpec((1, de2 // 2, d), lambda i, be, nu: (be[i], 0, 0)),
                  pl.BlockSpec((1, 1, d), lambda i, be, nu: (be[i], 0, 0))],
        out_specs=pl.BlockSpec((blk * n_chunk, LANES), lambda i, be, nu: (i, 0)),
        scratch_shapes=[pltpu.VMEM((d, de2), BF16), pltpu.VMEM((de2 // 2, d), BF16)],
    )
    return pl.pallas_call(
        _expert_kernel,
        grid_spec=grid_spec,
        out_shape=jax.ShapeDtypeStruct(xs_rows.shape, F32),
        compiler_params=pltpu.CompilerParams(
            dimension_semantics=("arbitrary",), vmem_limit_bytes=VMEM_LIMIT),
        name="experts",
    )(block_expert, n_used, xs_rows, w_gate_up, b_gate_up.reshape(N_EXPERTS, 1, de2),
      w_down, b_down.reshape(N_EXPERTS, 1, d))


def _combine_kernel(pos_ref, h_ref, gate_ref, gfin_ref, y_hbm, out_ref, buf, sems):
    i = pl.program_id(0)
    n = pl.num_programs(0)
    slot = lax.rem(i, 2)
    tc = h_ref.shape[0]
    n_chunk = buf.shape[2] // tc

    def token(ref, t):
        return ref.at[pl.ds(pl.multiple_of(t * n_chunk, n_chunk), n_chunk)]

    def issue(tile, s):
        def body(r, carry):
            for k in range(TOP_K):
                row = pos_ref[(tile * tc + r) * TOP_K + k]
                pltpu.make_async_copy(token(y_hbm, row), token(buf.at[s, k], r), sems.at[s]).start()
            return carry
        lax.fori_loop(0, tc, body, 0, unroll=4)

    @pl.when(i == 0)
    def _():
        issue(0, 0)

    @pl.when(i + 1 < n)
    def _():
        issue(i + 1, 1 - slot)

    for k in range(TOP_K):
        pltpu.make_async_copy(y_hbm.at[pl.ds(0, tc * n_chunk)], buf.at[slot, k], sems.at[slot]).wait()
    gate = gate_ref[...]
    pieces = []
    for c in range(n_chunk):
        y = h_ref[:, c * LANES:(c + 1) * LANES]
        for k in range(TOP_K):
            y = y + gate[:, k:k + 1] * buf[slot, k, pl.ds(c, tc, stride=n_chunk), :]
        pieces.append(y)
    out_ref[...] = _rms(jnp.concatenate(pieces, axis=1), gfin_ref[...])


def _combine(pos, h2, gates, g_final, y_rows, tc):
    t, d = h2.shape
    n_chunk = d // LANES
    grid_spec = pltpu.PrefetchScalarGridSpec(
        num_scalar_prefetch=1,
        grid=(t // tc,),
        in_specs=[pl.BlockSpec((tc, d), lambda i, pos: (i, 0)),
                  pl.BlockSpec((tc, LANES), lambda i, pos: (i, 0)),
                  pl.BlockSpec((1, d), lambda i, pos: (0, 0)),
                  pl.BlockSpec(memory_space=pl.ANY)],
        out_specs=pl.BlockSpec((tc, d), lambda i, pos: (i, 0)),
        scratch_shapes=[pltpu.VMEM((2, TOP_K, tc * n_chunk, LANES), F32),
                        pltpu.SemaphoreType.DMA((2,))],
    )
    return pl.pallas_call(
        _combine_kernel,
        grid_spec=grid_spec,
        out_shape=jax.ShapeDtypeStruct((t, d), F32),
        compiler_params=pltpu.CompilerParams(
            dimension_semantics=("arbitrary",), vmem_limit_bytes=VMEM_LIMIT),
        name="combine",
    )(pos, h2, gates, g_final, y_rows)


def _pad_heads(w, n_heads, width, offset=0):
    k = w.shape[0]
    w = w.reshape(k, n_heads, width)
    w = jnp.pad(w, ((0, 0), (0, 0), (offset, LANES - width - offset)))
    return w.reshape(k, n_heads * LANES)


def _rot_cols(w):
    half = w.shape[-1] // 2
    return jnp.concatenate([-w[..., half:], w[..., :half]], axis=-1)


def kernel(x, mem, positions, g_mix, w_in, g_q_a, w_q_b, g_kv_a, w_kv_b, rel_bias, g_out_a, g_out_b,
           w_o, g_xattn, g_mem, w_mq, w_mkv, w_mo, g_moe, w_router, b_router, w_gate_up, b_gate_up,
           w_down, b_down, g_final):
    b, s, d = x.shape
    t = b * s
    assert g_mix.shape[0] == 1, "single-layer block: the final norm is fused into the last stage"

    inv_freq = ROPE_THETA ** (-jnp.arange(0, QK_ROPE_DIM, 2, dtype=F32) / QK_ROPE_DIM)
    ang = positions.astype(F32)[..., None] * inv_freq
    cos, sin = jnp.cos(ang), jnp.sin(ang)
    pad_lo, pad_hi = QK_NOPE_DIM, LANES - QK_NOPE_DIM - QK_ROPE_DIM
    cos128 = jnp.concatenate([jnp.ones((b, s, pad_lo), F32), cos, cos, jnp.ones((b, s, pad_hi), F32)], -1)
    sin128 = jnp.concatenate([jnp.zeros((b, s, pad_lo), F32), sin, sin, jnp.zeros((b, s, pad_hi), F32)], -1)
    tbl = jnp.pad(rel_bias.T.astype(F32), ((0, 0), (0, LANES - NUM_BUCKETS)))

    h = x
    for l in range(1):
        c0 = 3 * WIDTH_A + Q_LORA_RANK + KV_LORA_RANK
        w_kpe = w_in[l][:, c0:c0 + QK_ROPE_DIM]
        place = lambda w: jnp.pad(w, ((0, 0), (QK_NOPE_DIM, LANES - QK_NOPE_DIM - QK_ROPE_DIM)))
        w_in_ext = jnp.concatenate([w_in[l][:, :c0], place(w_kpe), place(_rot_cols(w_kpe))], 1).astype(BF16)
        dq = QK_NOPE_DIM + QK_ROPE_DIM
        wq3 = w_q_b[l].reshape(Q_LORA_RANK, N_HEADS_B, dq)
        wq = _pad_heads(wq3.reshape(Q_LORA_RANK, -1), N_HEADS_B, dq).astype(BF16)
        wq_rot3 = jnp.concatenate([jnp.zeros_like(wq3[..., :QK_NOPE_DIM]), _rot_cols(wq3[..., QK_NOPE_DIM:])], -1)
        wqr = _pad_heads(wq_rot3.reshape(Q_LORA_RANK, -1), N_HEADS_B, dq).astype(BF16)
        wkv3 = w_kv_b[l].reshape(KV_LORA_RANK, N_HEADS_B, QK_NOPE_DIM + V_DIM_B)
        wk = _pad_heads(wkv3[..., :QK_NOPE_DIM].reshape(KV_LORA_RANK, -1), N_HEADS_B, QK_NOPE_DIM).astype(BF16)
        wv = _pad_heads(wkv3[..., QK_NOPE_DIM:].reshape(KV_LORA_RANK, -1), N_HEADS_B, V_DIM_B).astype(BF16)

        *views, qm, km, vm = _proj(h, cos128, sin128, g_mix[l][None], w_in_ext, g_q_a[l][None], wq, wqr,
                                   g_kv_a[l][None], wk, wv, tm=256)
        pats = _dilated_all(views, positions, tbl)
        ob = _mla(qm, km, vm, tq=1024, tk=512)
        kmem, vmem = _memkv(mem, g_mem[l][None], w_mkv[l].astype(BF16))

        g_out_b_pad = _pad_heads(g_out_b[l][None], N_HEADS_B, V_DIM_B)
        w_o_b_pad = _pad_heads(w_o[l][WIDTH_A:].T, N_HEADS_B, V_DIM_B).T.astype(BF16)
        w_mq_s = (w_mq[l] * ((d // N_HEADS_MEM) ** -0.5)).astype(BF16)
        w_router_pad = jnp.pad(w_router[l], ((0, 0), (0, LANES - N_EXPERTS)))
        b_router_pad = jnp.pad(b_router[l][None], ((0, 0), (0, LANES - N_EXPERTS)), constant_values=NEG_INF)
        h2, xn_rows, logits = _post(h, pats[0::2], pats[1::2], ob, g_out_a[l][None],
                               g_out_b_pad, w_o[l][:WIDTH_A].astype(BF16), w_o_b_pad, g_xattn[l][None],
                               w_mq_s, kmem, vmem, w_mo[l].astype(BF16), g_moe[l][None], w_router_pad,
                               b_router_pad, tm=256)

        gates, eidx, rank, counts = _route(logits.reshape(t, LANES), tr=256)
        counts = counts[0, :N_EXPERTS]
        blk = EXPERT_BLOCK
        padded = ((counts + blk - 1) // blk) * blk
        ends = jnp.cumsum(padded)
        pad_start = ends - padded
        eidx4, rank4 = eidx[:, :TOP_K], rank[:, :TOP_K]
        expert_ids = jnp.arange(N_EXPERTS, dtype=jnp.int32)
        pos = (jnp.sum(jnp.where(eidx4[..., None] == expert_ids, pad_start, 0), axis=-1)
               + rank4).astype(jnp.int32).reshape(-1)
        n_blk = t * TOP_K // blk + N_EXPERTS
        block_expert = jnp.minimum(
            jnp.sum(ends[None, :] <= (jnp.arange(n_blk) * blk)[:, None], axis=1),
            N_EXPERTS - 1).astype(jnp.int32)
        n_used = (ends[-1] // blk).astype(jnp.int32)[None]
        ends0 = jnp.concatenate([jnp.zeros((1,), jnp.int32), ends.astype(jnp.int32)])
        xs_rows = _dispatch(pos, ends0, xn_rows, t, n_blk * blk)
        y_rows = _experts(block_expert, n_used, xs_rows, w_gate_up[l], b_gate_up[l], w_down[l], b_down[l])
        out = _combine(pos, h2.reshape(t, d), gates, g_final[None], y_rows, tc=128)
        h = out.reshape(b, s, d)
    return h
```

```python
import functools
import math

import jax
import jax.numpy as jnp
from jax import lax
from jax.experimental import pallas as pl
from jax.experimental.pallas import tpu as pltpu

F32 = jnp.float32
BF16 = jnp.bfloat16

LANES = 128
EPS = 1e-6
NEG_INF = -1e30
LOG2_E = math.log2(math.e)

N_HEADS_A = 8
HEAD_DIM_A = 64
WIDTH_A = N_HEADS_A * HEAD_DIM_A
DILATED_PATTERNS = ((128, 1), (512, 4), (2048, 16))
N_HEADS_B = 8
QK_NOPE_DIM = 64
QK_ROPE_DIM = 32
V_DIM_B = 64
Q_LORA_RANK = 256
KV_LORA_RANK = 128
ROPE_THETA = 10000.0
NUM_BUCKETS = 32
MAX_DISTANCE = 1024
N_HEADS_MEM = 4
N_EXPERTS = 32
TOP_K = 4
SWIGLU_LIMIT = 7.0
SWIGLU_ALPHA = 1.702

Q_BLOCK = 128
K_WINDOW = 256
HALF_WINDOW = (K_WINDOW - Q_BLOCK) // 2
EXPERT_BLOCK = 256
DISPATCH_CHUNK = 512
VMEM_LIMIT = 56 * 1024 * 1024


def _rms(x, g):
    return x * lax.rsqrt(jnp.mean(x * x, axis=-1, keepdims=True) + EPS) * g


def _dot(a, b):
    return jnp.dot(a, b, preferred_element_type=F32)


def _dot_nt(a, b):
    return lax.dot_general(a, b, (((1,), (1,)), ((), ())), preferred_element_type=F32)


def _load_token_rows(ref, n_tok):
    n_chunk = ref.shape[0] // n_tok
    return jnp.concatenate([ref[pl.ds(c, n_tok, stride=n_chunk), :] for c in range(n_chunk)], axis=1)


def _pack_bf16_pairs(x):
    half = x.shape[1] // 2
    bits = lambda v: lax.bitcast_convert_type(v.astype(BF16).astype(F32), jnp.uint32)
    return bits(x[:, half:]) | (bits(x[:, :half]) >> 16)


def _unpack_bf16_pairs(w):
    lo = lax.bitcast_convert_type(w << 16, F32)
    hi = lax.bitcast_convert_type(w & jnp.uint32(0xFFFF0000), F32)
    return lo, hi


def _store_token_rows(ref, value):
    n_tok = value.shape[0]
    n_chunk = value.shape[1] // LANES
    for c in range(n_chunk):
        ref[pl.ds(c, n_tok, stride=n_chunk), :] = value[:, c * LANES:(c + 1) * LANES]


def _proj_kernel(x_ref, cos_ref, sin_ref, gmix_ref, win_ref, gq_ref, wq_ref, wqr_ref,
                 gkv_ref, wk_ref, wv_ref,
                 q1_ref, k1_ref, v1_ref, q4_ref, k4_ref, v4_ref, q16_ref, k16_ref, v16_ref,
                 qm_ref, km_ref, vm_ref, za_scr):
    xn = _rms(x_ref[0], gmix_ref[...])
    z = _dot(xn.astype(BF16), win_ref[...])
    tm = z.shape[0]
    c0 = 3 * WIDTH_A
    per_group = WIDTH_A // LANES
    for c in range(3 * per_group):
        chunk = z[:, c * LANES:(c + 1) * LANES]
        za_scr[c] = chunk * (HEAD_DIM_A ** -0.5 * LOG2_E) if c < per_group else chunk
    for dil, refs in ((1, (q1_ref, k1_ref, v1_ref)), (4, (q4_ref, k4_ref, v4_ref)),
                      (16, (q16_ref, k16_ref, v16_ref))):
        for r in range(dil):
            for c in range(3 * per_group):
                rows = za_scr[c, pl.ds(r, tm // dil, stride=dil), :].astype(BF16)
                col = r * WIDTH_A + (c % per_group) * LANES
                refs[c // per_group][0, :, col:col + LANES] = rows
    zqn = _rms(z[:, c0:c0 + Q_LORA_RANK], gq_ref[...]).astype(BF16)
    zkvn = _rms(z[:, c0 + Q_LORA_RANK:c0 + Q_LORA_RANK + KV_LORA_RANK], gkv_ref[...]).astype(BF16)
    c1 = c0 + Q_LORA_RANK + KV_LORA_RANK
    cos = cos_ref[0]
    sin = sin_ref[0]
    kpe = z[:, c1:c1 + LANES] * cos + z[:, c1 + LANES:c1 + 2 * LANES] * sin
    q = _dot(zqn, wq_ref[...])
    qr = _dot(zqn, wqr_ref[...])
    kn = _dot(zkvn, wk_ref[...])
    vv = _dot(zkvn, wv_ref[...])
    lane = lax.broadcasted_iota(jnp.int32, (1, LANES), 1)
    ones_col = jnp.where(lane == V_DIM_B, 1.0, 0.0).astype(F32)
    scale = (QK_NOPE_DIM + QK_ROPE_DIM) ** -0.5 * LOG2_E
    for h in range(N_HEADS_B):
        sl = slice(h * LANES, (h + 1) * LANES)
        qm_ref[0, h] = ((q[:, sl] * cos + qr[:, sl] * sin) * scale).astype(BF16)
        km_ref[0, h] = (kn[:, sl] + kpe).astype(BF16)
        vm_ref[0, h] = (vv[:, sl] + ones_col).astype(BF16)


def _proj(x, cos128, sin128, g_mix, w_in_ext, g_q, wq, wqr, g_kv, wk, wv, tm):
    b, s, d = x.shape
    n_in = w_in_ext.shape[1]
    hw = N_HEADS_B * LANES
    full = lambda shape: pl.BlockSpec(shape, lambda bi, i: (0,) * len(shape))
    row = lambda w: pl.BlockSpec((1, tm, w), lambda bi, i: (bi, i, 0))
    head = pl.BlockSpec((1, N_HEADS_B, tm, LANES), lambda bi, i: (bi, 0, i, 0))
    dils = [dil for _, dil in DILATED_PATTERNS]
    view_specs = [pl.BlockSpec((1, tm // dil, dil * WIDTH_A), lambda bi, i: (bi, i, 0))
                  for dil in dils for _ in range(3)]
    view_shapes = [jax.ShapeDtypeStruct((b, s // dil, dil * WIDTH_A), BF16)
                   for dil in dils for _ in range(3)]
    return pl.pallas_call(
        _proj_kernel,
        grid=(b, s // tm),
        in_specs=[row(d), row(LANES), row(LANES), full((1, d)), full((d, n_in)),
                  full((1, Q_LORA_RANK)), full((Q_LORA_RANK, hw)), full((Q_LORA_RANK, hw)),
                  full((1, KV_LORA_RANK)), full((KV_LORA_RANK, hw)), full((KV_LORA_RANK, hw))],
        out_specs=view_specs + [head, head, head],
        out_shape=view_shapes + [jax.ShapeDtypeStruct((b, N_HEADS_B, s, LANES), BF16)] * 3,
        scratch_shapes=[pltpu.VMEM((3 * WIDTH_A // LANES, tm, LANES), F32)],
        compiler_params=pltpu.CompilerParams(
            dimension_semantics=("parallel", "parallel"), vmem_limit_bytes=VMEM_LIMIT),
        name="proj",
    )(x, cos128, sin128, g_mix, w_in_ext, g_q, wq, wqr, g_kv, wk, wv)


_LOG_BUCKET_STARTS = tuple(
    next(n for n in range(8, 4096)
         if int(math.log(n / 8) / math.log(MAX_DISTANCE / 8) * 8) >= t)
    for t in range(1, 8))


def _rel_bucket(rel):
    n = jnp.abs(rel)
    large = jnp.full(rel.shape, NUM_BUCKETS // 4, jnp.int32)
    for start in _LOG_BUCKET_STARTS:
        large = large + jnp.where(n >= start, 1, 0)
    mag = jnp.where(n < NUM_BUCKETS // 4, n, large)
    return mag + jnp.where(rel > 0, NUM_BUCKETS // 2, 0)


def _window_start(j, sub_len):
    return jnp.clip(j * Q_BLOCK - HALF_WINDOW, 0, sub_len - K_WINDOW)


def _bias_kernel(off_ref, qpos_ref, kpos_ref, tbl_ref, out_ref):
    rel = kpos_ref[0] - qpos_ref[0]
    bucket = _rel_bucket(rel)
    delta = (off_ref[pl.program_id(0)] + lax.broadcasted_iota(jnp.int32, (1, K_WINDOW), 1)
             - lax.broadcasted_iota(jnp.int32, (Q_BLOCK, 1), 0))
    valid = jnp.abs(delta) <= HALF_WINDOW
    for h in range(N_HEADS_A):
        tbl = jnp.broadcast_to(tbl_ref[h:h + 1, :], (Q_BLOCK, LANES))
        bias = jnp.concatenate(
            [jnp.take_along_axis(tbl, bucket[:, c * LANES:(c + 1) * LANES], axis=1)
             for c in range(K_WINDOW // LANES)], axis=1)
        out_ref[0, h] = jnp.where(valid, bias * LOG2_E, NEG_INF)


def _bias_tiles(offs, qpos, kpos, tbl):
    n = offs.shape[0]
    grid_spec = pltpu.PrefetchScalarGridSpec(
        num_scalar_prefetch=1,
        grid=(n,),
        in_specs=[pl.BlockSpec((1, Q_BLOCK, 1), lambda t, off: (t, 0, 0)),
                  pl.BlockSpec((1, 1, K_WINDOW), lambda t, off: (t, 0, 0)),
                  pl.BlockSpec((N_HEADS_A, LANES), lambda t, off: (0, 0))],
        out_specs=pl.BlockSpec((1, N_HEADS_A, Q_BLOCK, K_WINDOW), lambda t, off: (t, 0, 0, 0)),
    )
    return pl.pallas_call(
        _bias_kernel,
        grid_spec=grid_spec,
        out_shape=jax.ShapeDtypeStruct((n, N_HEADS_A, Q_BLOCK, K_WINDOW), F32),
        compiler_params=pltpu.CompilerParams(
            dimension_semantics=("arbitrary",), vmem_limit_bytes=VMEM_LIMIT),
        name="bias_tiles",
    )(offs, qpos, kpos, tbl)


def _dilated_kernel(q_ref, k_ref, v_ref, bias_ref, o_ref, lse_ref, s_scr, p_scr, *, sub_len):
    j = pl.program_id(2)
    kstart = pl.multiple_of(_window_start(j, sub_len), HALF_WINDOW)
    q = q_ref[0]
    kw = k_ref[0, pl.ds(kstart, K_WINDOW), :]
    vw = v_ref[0, pl.ds(kstart, K_WINDOW), :]
    first = lax.broadcasted_iota(jnp.int32, (1, LANES), 1) < HEAD_DIM_A
    pair = lambda h: slice((h // 2) * LANES, (h // 2 + 1) * LANES)
    for h in range(N_HEADS_A):
        own = first if h % 2 == 0 else jnp.logical_not(first)
        qh = jnp.where(own, q[:, pair(h)], jnp.zeros_like(q[:, pair(h)]))
        s_scr[h] = _dot_nt(qh, kw[:, pair(h)]) + bias_ref[0, h]
    stats = []
    for h in range(N_HEADS_A):
        s = s_scr[h]
        m = jnp.max(s, axis=1, keepdims=True)
        e = jnp.exp2(s - m)
        l = jnp.sum(e, axis=1, keepdims=True)
        p_scr[h] = e.astype(BF16)
        stats.append((1.0 / l, m + jnp.log2(l)))
    for h in range(0, N_HEADS_A, 2):
        (r0, lse0), (r1, lse1) = stats[h], stats[h + 1]
        o0 = _dot(p_scr[h], vw[:, pair(h)])
        o1 = _dot(p_scr[h + 1], vw[:, pair(h)])
        o_ref[0, :, pair(h)] = jnp.where(first, o0 * r0, o1 * r1)
        lse_ref[0, :, pair(h)] = jnp.where(first, lse0, lse1)


def _dilated(qv, kv, vv, bias, bias_index, dil):
    b, sub_len, _ = qv.shape
    w = WIDTH_A
    assert sub_len >= K_WINDOW and sub_len % Q_BLOCK == 0
    nq = sub_len // Q_BLOCK
    qspec = pl.BlockSpec((1, Q_BLOCK, w), lambda bi, r, j: (bi, j, r))
    kvspec = pl.BlockSpec((1, sub_len, w), lambda bi, r, j: (bi, 0, r))
    bspec = pl.BlockSpec((1, N_HEADS_A, Q_BLOCK, K_WINDOW),
                         lambda bi, r, j: (bias_index(bi, r, j, nq), 0, 0, 0))
    return pl.pallas_call(
        functools.partial(_dilated_kernel, sub_len=sub_len),
        grid=(b, dil, nq),
        in_specs=[qspec, kvspec, kvspec, bspec],
        out_specs=[qspec, qspec],
        out_shape=[jax.ShapeDtypeStruct((b, sub_len, dil * w), F32)] * 2,
        scratch_shapes=[pltpu.VMEM((N_HEADS_A, Q_BLOCK, K_WINDOW), F32),
                        pltpu.VMEM((N_HEADS_A, Q_BLOCK, K_WINDOW), BF16)],
        compiler_params=pltpu.CompilerParams(
            dimension_semantics=("parallel", "parallel", "arbitrary"),
            vmem_limit_bytes=VMEM_LIMIT),
        name=f"dilated{dil}",
    )(qv, kv, vv, bias)


def _dilated_all(views, positions, tbl):
    b, s = positions.shape
    dils = [dil for _, dil in DILATED_PATTERNS]
    consecutive = jnp.all(positions[:, 1:] - positions[:, :-1] == 1)

    def run(bias, index_fns):
        outs = []
        for g, dil in enumerate(dils):
            qv, kv, vv = views[3 * g:3 * g + 3]
            outs.extend(_dilated(qv, kv, vv, bias, index_fns[g], dil))
        return tuple(outs)

    def shared_tiles():
        offs, qpos, kpos, fns = [], [], [], []
        for g, dil in enumerate(dils):
            for off in (0, -HALF_WINDOW, -2 * HALF_WINDOW):
                offs.append(off)
                qpos.append(dil * jnp.arange(Q_BLOCK, dtype=jnp.int32))
                kpos.append(dil * (off + jnp.arange(K_WINDOW, dtype=jnp.int32)))
            fns.append(lambda bi, r, j, nq, g=g:
                       3 * g + jnp.where(j == 0, 0, jnp.where(j == nq - 1, 2, 1)))
        bias = _bias_tiles(jnp.array(offs, jnp.int32), jnp.stack(qpos)[:, :, None],
                           jnp.stack(kpos)[:, None, :], tbl)
        return run(bias, fns)

    def per_block_tiles():
        offs, qpos, kpos, fns = [], [], [], []
        base = 0
        for g, dil in enumerate(dils):
            sub_len = s // dil
            nq = sub_len // Q_BLOCK
            pos_t = positions.reshape(b, sub_len, dil).transpose(0, 2, 1).reshape(b * dil, sub_len)
            starts = _window_start(jnp.arange(nq), sub_len)
            win = starts[:, None] + jnp.arange(K_WINDOW)[None, :]
            qpos.append(pos_t.reshape(b * dil * nq, Q_BLOCK))
            kpos.append(pos_t[:, win].reshape(b * dil * nq, K_WINDOW))
            offs.append(jnp.tile(starts - jnp.arange(nq) * Q_BLOCK, b * dil))
            fns.append(lambda bi, r, j, nq, base=base, dil=dil: base + (bi * dil + r) * nq + j)
            base += b * dil * nq
        bias = _bias_tiles(jnp.concatenate(offs).astype(jnp.int32),
                           jnp.concatenate(qpos)[:, :, None], jnp.concatenate(kpos)[:, None, :], tbl)
        return run(bias, fns)

    return lax.cond(consecutive, shared_tiles, per_block_tiles)


def _mla_kernel(q_ref, k_ref, v_ref, o_ref, *, tk):
    q = q_ref[0, 0]
    tq = q.shape[0]
    n_kv = k_ref.shape[2] // tk

    def body(i, carry):
        m, acc = carry
        start = pl.multiple_of(i * tk, tk)
        k = k_ref[0, 0, pl.ds(start, tk), :]
        v = v_ref[0, 0, pl.ds(start, tk), :]
        s = _dot_nt(q, k)
        m_new = jnp.maximum(m, jnp.max(s, axis=1, keepdims=True))
        p = jnp.exp2(s - m_new)
        acc = jnp.exp2(m - m_new) * acc + _dot(p.astype(BF16), v)
        return m_new, acc

    m0 = jnp.full((tq, 1), NEG_INF, F32)
    _, acc = lax.fori_loop(0, n_kv, body, (m0, jnp.zeros((tq, LANES), F32)), unroll=True)
    lane = lax.broadcasted_iota(jnp.int32, (1, LANES), 1)
    denom = jnp.sum(jnp.where(lane == V_DIM_B, acc, 0.0), axis=1, keepdims=True)
    o_ref[0, 0] = jnp.where(lane < V_DIM_B, acc / denom, 0.0).astype(BF16)


def _mla(qm, km, vm, tq, tk):
    b, nh, s, _ = qm.shape
    qspec = pl.BlockSpec((1, 1, tq, LANES), lambda bi, h, i: (bi, h, i, 0))
    kvspec = pl.BlockSpec((1, 1, s, LANES), lambda bi, h, i: (bi, h, 0, 0))
    return pl.pallas_call(
        functools.partial(_mla_kernel, tk=tk),
        grid=(b, nh, s // tq),
        in_specs=[qspec, kvspec, kvspec],
        out_specs=qspec,
        out_shape=jax.ShapeDtypeStruct((b, nh, s, LANES), BF16),
        compiler_params=pltpu.CompilerParams(
            dimension_semantics=("parallel", "parallel", "arbitrary"),
            vmem_limit_bytes=VMEM_LIMIT),
        name="mla",
    )(qm, km, vm)


def _memkv_kernel(mem_ref, g_ref, w_ref, k_ref, v_ref):
    d = mem_ref.shape[2]
    kv = _dot(_rms(mem_ref[0], g_ref[...]).astype(BF16), w_ref[...])
    k_ref[0] = kv[:, :d].astype(BF16)
    v_ref[0] = kv[:, d:].astype(BF16)


def _memkv(mem, g_mem, w_mkv):
    b, m, d = mem.shape
    spec = pl.BlockSpec((1, m, d), lambda bi: (bi, 0, 0))
    return pl.pallas_call(
        _memkv_kernel,
        grid=(b,),
        in_specs=[spec, pl.BlockSpec((1, d), lambda bi: (0, 0)),
                  pl.BlockSpec((d, 2 * d), lambda bi: (0, 0))],
        out_specs=[spec, spec],
        out_shape=[jax.ShapeDtypeStruct((b, m, d), BF16)] * 2,
        compiler_params=pltpu.CompilerParams(
            dimension_semantics=("parallel",), vmem_limit_bytes=VMEM_LIMIT),
        name="memkv",
    )(mem, g_mem, w_mkv)


def _post_kernel(x_ref, o1_ref, o2_ref, o3_ref, l1_ref, l2_ref, l3_ref, ob_ref,
                 goa_ref, gob_ref, woa_ref, wob_ref, gx_ref, wmq_ref, km_ref, vm_ref, wmo_ref,
                 gmoe_ref, wr_ref, br_ref,
                 h_ref, xn_ref, logit_ref, *nat_scr):
    tm = x_ref.shape[1]

    def natural(view_ref, scr):
        dil = view_ref.shape[2] // WIDTH_A
        if dil == 1:
            return view_ref[0]
        n_chunk = WIDTH_A // LANES
        for r in range(dil):
            for c in range(n_chunk):
                col = r * WIDTH_A + c * LANES
                scr[c, pl.ds(r, tm // dil, stride=dil), :] = view_ref[0, :, col:col + LANES]
        return jnp.concatenate([scr[c] for c in range(n_chunk)], axis=1)

    o1, o2, o3 = o1_ref[0], natural(o2_ref, nat_scr[0]), natural(o3_ref, nat_scr[1])
    l1, l2, l3 = l1_ref[0], natural(l2_ref, nat_scr[2]), natural(l3_ref, nat_scr[3])
    mx = jnp.maximum(jnp.maximum(l1, l2), l3)
    w1, w2, w3 = jnp.exp2(l1 - mx), jnp.exp2(l2 - mx), jnp.exp2(l3 - mx)
    oa = (w1 * o1 + w2 * o2 + w3 * o3) / (w1 + w2 + w3)
    oan = _rms(oa, goa_ref[...]).astype(BF16)
    ob = jnp.concatenate([ob_ref[0, h] for h in range(N_HEADS_B)], axis=1).astype(F32)
    ms_b = jnp.sum(ob * ob, axis=1, keepdims=True) * (1.0 / (N_HEADS_B * V_DIM_B))
    obn = (ob * lax.rsqrt(ms_b + EPS) * gob_ref[...]).astype(BF16)
    h1 = x_ref[0] + _dot(oan, woa_ref[...]) + _dot(obn, wob_ref[...])

    hn = _rms(h1, gx_ref[...]).astype(BF16)
    q = _dot(hn, wmq_ref[...]).astype(BF16)
    dh = q.shape[1] // N_HEADS_MEM
    heads = []
    for h in range(N_HEADS_MEM):
        sl = slice(h * dh, (h + 1) * dh)
        s = _dot_nt(q[:, sl], km_ref[0, :, sl])
        e = jnp.exp(s - jnp.max(s, axis=1, keepdims=True))
        p = e / jnp.sum(e, axis=1, keepdims=True)
        heads.append(_dot(p.astype(BF16), vm_ref[0, :, sl]))
    o = jnp.concatenate(heads, axis=1).astype(BF16)
    h2 = h1 + _dot(o, wmo_ref[...])
    h_ref[0] = h2

    xn = _rms(h2, gmoe_ref[...])
    _store_token_rows(xn_ref, _pack_bf16_pairs(xn))
    logit_ref[0] = jnp.dot(xn, wr_ref[...], preferred_element_type=F32,
                           precision=lax.Precision.HIGHEST) + br_ref[...]


def _post(x, o_pats, lse_pats, ob, g_out_a, g_out_b_pad, w_o_a, w_o_b_pad, g_xattn, w_mq, kmem,
          vmem, w_mo, g_moe, w_router_pad, b_router_pad, tm):
    b, s, d = x.shape
    m = kmem.shape[1]
    full = lambda shape: pl.BlockSpec(shape, lambda bi, i: (0,) * len(shape))
    row = lambda w: pl.BlockSpec((1, tm, w), lambda bi, i: (bi, i, 0))
    memspec = pl.BlockSpec((1, m, d), lambda bi, i: (bi, 0, 0))
    hw = N_HEADS_B * LANES
    views = [pl.BlockSpec((1, tm // dil, dil * WIDTH_A), lambda bi, i: (bi, i, 0))
             for _, dil in DILATED_PATTERNS]
    return pl.pallas_call(
        _post_kernel,
        grid=(b, s // tm),
        in_specs=[row(d)] + views + views
        + [pl.BlockSpec((1, N_HEADS_B, tm, LANES), lambda bi, i: (bi, 0, i, 0)),
           full((1, WIDTH_A)), full((1, hw)), full((WIDTH_A, d)), full((hw, d)),
           full((1, d)), full((d, d)), memspec, memspec, full((d, d)),
           full((1, d)), full((d, LANES)), full((1, LANES))],
        out_specs=[row(d), pl.BlockSpec((tm * d // 2 // LANES, LANES), lambda bi, i: (bi * (s // tm) + i, 0)),
                   row(LANES)],
        out_shape=[jax.ShapeDtypeStruct((b, s, d), F32),
                   jax.ShapeDtypeStruct((b * s * d // 2 // LANES, LANES), jnp.uint32),
                   jax.ShapeDtypeStruct((b, s, LANES), F32)],
        scratch_shapes=[pltpu.VMEM((WIDTH_A // LANES, tm, LANES), F32)] * 4,
        compiler_params=pltpu.CompilerParams(
            dimension_semantics=("parallel", "parallel"), vmem_limit_bytes=VMEM_LIMIT),
        name="post",
    )(x, *o_pats, *lse_pats, ob, g_out_a, g_out_b_pad, w_o_a, w_o_b_pad, g_xattn, w_mq, kmem,
      vmem, w_mo, g_moe, w_router_pad, b_router_pad)


def _route_kernel(logit_ref, gate_ref, eidx_ref, rank_ref, count_ref, carry_ref):
    @pl.when(pl.program_id(0) == 0)
    def _():
        carry_ref[...] = jnp.zeros_like(carry_ref)

    l = logit_ref[...]
    tr = l.shape[0]
    lane_i = lax.broadcasted_iota(jnp.int32, l.shape, 1)
    lane = lane_i.astype(F32)
    vals, idxs = [], []
    for _ in range(TOP_K):
        m = jnp.max(l, axis=1, keepdims=True)
        idx = jnp.min(jnp.where(l == m, lane, float(LANES)), axis=1, keepdims=True)
        vals.append(m)
        idxs.append(idx)
        l = jnp.where(lane == idx, -jnp.inf, l)
    exps = [jnp.exp(v - vals[0]) for v in vals]
    denom = exps[0] + exps[1] + exps[2] + exps[3]
    onehot = jnp.zeros(l.shape, F32)
    for idx in idxs:
        onehot = onehot + jnp.where(lane == idx, 1.0, 0.0)
    r = lax.broadcasted_iota(jnp.int32, (tr, tr), 0)
    c = lax.broadcasted_iota(jnp.int32, (tr, tr), 1)
    tri = jnp.where(c < r, 1.0, 0.0).astype(BF16)
    before = _dot(tri, onehot.astype(BF16)) + carry_ref[...]
    gate = jnp.zeros(l.shape, F32)
    eidx = jnp.zeros(l.shape, jnp.int32)
    rank = jnp.zeros(l.shape, jnp.int32)
    for k in range(TOP_K):
        rk = jnp.sum(jnp.where(lane == idxs[k], before, 0.0), axis=1, keepdims=True)
        gate = jnp.where(lane_i == k, exps[k] / denom, gate)
        eidx = jnp.where(lane_i == k, idxs[k].astype(jnp.int32), eidx)
        rank = jnp.where(lane_i == k, rk.astype(jnp.int32), rank)
    gate_ref[...] = gate
    eidx_ref[...] = eidx
    rank_ref[...] = rank
    total = carry_ref[...] + jnp.sum(onehot, axis=0, keepdims=True)
    carry_ref[...] = total
    count_ref[...] = total.astype(jnp.int32)


def _route(logits, tr):
    t = logits.shape[0]
    spec = pl.BlockSpec((tr, LANES), lambda i: (i, 0))
    return pl.pallas_call(
        _route_kernel,
        grid=(t // tr,),
        in_specs=[spec],
        out_specs=[spec, spec, spec, pl.BlockSpec((1, LANES), lambda i: (0, 0))],
        out_shape=[jax.ShapeDtypeStruct((t, LANES), F32), jax.ShapeDtypeStruct((t, LANES), jnp.int32),
                   jax.ShapeDtypeStruct((t, LANES), jnp.int32),
                   jax.ShapeDtypeStruct((1, LANES), jnp.int32)],
        scratch_shapes=[pltpu.VMEM((1, LANES), F32)],
        compiler_params=pltpu.CompilerParams(
            dimension_semantics=("arbitrary",), vmem_limit_bytes=VMEM_LIMIT),
        name="route",
    )(logits)


def _dispatch_kernel(pos_ref, ends_ref, x_ref, xs_hbm, zbuf, sem, zsem):
    c = pl.program_id(0)
    n_chunk = x_ref.shape[0] // DISPATCH_CHUNK
    blk_rows = zbuf.shape[0]

    def token(ref, t):
        return ref.at[pl.ds(pl.multiple_of(t * n_chunk, n_chunk), n_chunk)]

    @pl.when(c == 0)
    def _():
        zbuf[...] = jnp.zeros_like(zbuf)

        def zero_copy(start):
            return pltpu.make_async_copy(
                zbuf, xs_hbm.at[pl.ds(pl.multiple_of(start * n_chunk, n_chunk), blk_rows)], zsem)

        blk = blk_rows // n_chunk
        used_end = ends_ref[N_EXPERTS]
        starts = [(ends_ref[e + 1] > ends_ref[e], ends_ref[e + 1] - blk) for e in range(N_EXPERTS)]
        starts += [((used_end + (j + 1) * blk) * n_chunk <= xs_hbm.shape[0], used_end + j * blk)
                   for j in range(N_EXPERTS)]
        for cond, start in starts:
            @pl.when(cond)
            def _():
                zero_copy(start).start()
        for cond, start in starts:
            @pl.when(cond)
            def _():
                zero_copy(start).wait()

    def body(t, carry):
        tok = c * DISPATCH_CHUNK + t
        for k in range(TOP_K):
            pltpu.make_async_copy(token(x_ref, t), token(xs_hbm, pos_ref[tok * TOP_K + k]), sem).start()
        return carry
    lax.fori_loop(0, DISPATCH_CHUNK, body, 0, unroll=4)

    for k in range(TOP_K):
        pltpu.make_async_copy(x_ref, xs_hbm.at[pl.ds(0, x_ref.shape[0])], sem).wait()


def _dispatch(pos, ends0, xn_rows, n_tok, n_slots):
    n_chunk = xn_rows.shape[0] // n_tok
    grid_spec = pltpu.PrefetchScalarGridSpec(
        num_scalar_prefetch=2,
        grid=(n_tok // DISPATCH_CHUNK,),
        in_specs=[pl.BlockSpec((DISPATCH_CHUNK * n_chunk, LANES), lambda c, pos, ends: (c, 0))],
        out_specs=pl.BlockSpec(memory_space=pl.ANY),
        scratch_shapes=[pltpu.VMEM((EXPERT_BLOCK * n_chunk, LANES), xn_rows.dtype),
                        pltpu.SemaphoreType.DMA(()), pltpu.SemaphoreType.DMA(())],
    )
    return pl.pallas_call(
        _dispatch_kernel,
        grid_spec=grid_spec,
        out_shape=jax.ShapeDtypeStruct((n_slots * n_chunk, LANES), xn_rows.dtype),
        compiler_params=pltpu.CompilerParams(
            dimension_semantics=("arbitrary",), vmem_limit_bytes=VMEM_LIMIT),
        name="dispatch",
    )(pos, ends0, xn_rows)


def _expert_kernel(be_ref, nused_ref, xs_ref, wgu_ref, bgu_ref, wd_ref, bd_ref, y_ref, wgu_bf, wd_bf):
    i = pl.program_id(0)

    @pl.when(i < nused_ref[0])
    def _():
        @pl.when(jnp.logical_or(i == 0, be_ref[i] != be_ref[jnp.maximum(i - 1, 0)]))
        def _():
            wgu_bf[...] = wgu_ref[0].astype(BF16)
            wd_bf[...] = wd_ref[0].astype(BF16)

        x_lo, x_hi = _unpack_bf16_pairs(_load_token_rows(xs_ref, EXPERT_BLOCK))
        x = jnp.concatenate([x_lo.astype(BF16), x_hi.astype(BF16)], axis=1)
        gu = _dot(x, wgu_bf[...]) + bgu_ref[0]
        de = gu.shape[1] // 2
        gate = jnp.minimum(gu[:, :de], SWIGLU_LIMIT)
        up = jnp.clip(gu[:, de:], -SWIGLU_LIMIT, SWIGLU_LIMIT)
        hmid = (up + 1.0) * (gate * jax.nn.sigmoid(SWIGLU_ALPHA * gate))
        y = _dot(hmid.astype(BF16), wd_bf[...]) + bd_ref[0]
        _store_token_rows(y_ref, _pack_bf16_pairs(y))

    @pl.when(i >= nused_ref[0])
    def _():
        y_ref[...] = jnp.zeros_like(y_ref)


def _experts(block_expert, n_used, xs_rows, w_gate_up, b_gate_up, w_down, b_down):
    d, de2 = w_gate_up.shape[1:]
    n_chunk = d // 2 // LANES
    blk = EXPERT_BLOCK
    n_blk = xs_rows.shape[0] // (blk * n_chunk)
    grid_spec = pltpu.PrefetchScalarGridSpec(
        num_scalar_prefetch=2,
        grid=(n_blk,),
        in_specs=[pl.BlockSpec((blk * n_chunk, LANES),
                               lambda i, be, nu: (jnp.minimum(i, nu[0] - 1), 0)),
                  pl.BlockSpec((1, d, de2), lambda i, be, nu: (be[i], 0, 0)),
                  pl.BlockSpec((1, 1, de2), lambda i, be, nu: (be[i], 0, 0)),
                  pl.BlockSpec((1, de2 // 2, d), lambda i, be, nu: (be[i], 0, 0)),
                  pl.BlockSpec((1, 1, d), lambda i, be, nu: (be[i], 0, 0))],
        out_specs=pl.BlockSpec((blk * n_chunk, LANES), lambda i, be, nu: (i, 0)),
        scratch_shapes=[pltpu.VMEM((d, de2), BF16), pltpu.VMEM((de2 // 2, d), BF16)],
    )
    return pl.pallas_call(
        _expert_kernel,
        grid_spec=grid_spec,
        out_shape=jax.ShapeDtypeStruct(xs_rows.shape, xs_rows.dtype),
        compiler_params=pltpu.CompilerParams(
            dimension_semantics=("arbitrary",), vmem_limit_bytes=VMEM_LIMIT),
        name="experts",
    )(block_expert, n_used, xs_rows, w_gate_up, b_gate_up.reshape(N_EXPERTS, 1, de2),
      w_down, b_down.reshape(N_EXPERTS, 1, d))


def _combine_kernel(pos_ref, h_ref, gate_ref, gfin_ref, y_hbm, out_ref, buf, sems):
    i = pl.program_id(0)
    n = pl.num_programs(0)
    slot = lax.rem(i, 2)
    tc = h_ref.shape[0]
    n_chunk = buf.shape[2] // tc

    def token(ref, t):
        return ref.at[pl.ds(pl.multiple_of(t * n_chunk, n_chunk), n_chunk)]

    def issue(tile, s):
        def body(r, carry):
            for k in range(TOP_K):
                row = pos_ref[(tile * tc + r) * TOP_K + k]
                pltpu.make_async_copy(token(y_hbm, row), token(buf.at[s, k], r), sems.at[s]).start()
            return carry
        lax.fori_loop(0, tc, body, 0, unroll=4)

    @pl.when(i == 0)
    def _():
        issue(0, 0)

    @pl.when(i + 1 < n)
    def _():
        issue(i + 1, 1 - slot)

    for k in range(TOP_K):
        pltpu.make_async_copy(y_hbm.at[pl.ds(0, tc * n_chunk)], buf.at[slot, k], sems.at[slot]).wait()
    gate = gate_ref[...]
    half = h_ref.shape[1] // 2
    lo_pieces, hi_pieces = [], []
    for c in range(n_chunk):
        y_lo = h_ref[:, c * LANES:(c + 1) * LANES]
        y_hi = h_ref[:, half + c * LANES:half + (c + 1) * LANES]
        for k in range(TOP_K):
            lo, hi = _unpack_bf16_pairs(buf[slot, k, pl.ds(c, tc, stride=n_chunk), :])
            y_lo = y_lo + gate[:, k:k + 1] * lo
            y_hi = y_hi + gate[:, k:k + 1] * hi
        lo_pieces.append(y_lo)
        hi_pieces.append(y_hi)
    out_ref[...] = _rms(jnp.concatenate(lo_pieces + hi_pieces, axis=1), gfin_ref[...])


def _combine(pos, h2, gates, g_final, y_rows, tc):
    t, d = h2.shape
    n_chunk = d // 2 // LANES
    grid_spec = pltpu.PrefetchScalarGridSpec(
        num_scalar_prefetch=1,
        grid=(t // tc,),
        in_specs=[pl.BlockSpec((tc, d), lambda i, pos: (i, 0)),
                  pl.BlockSpec((tc, LANES), lambda i, pos: (i, 0)),
                  pl.BlockSpec((1, d), lambda i, pos: (0, 0)),
                  pl.BlockSpec(memory_space=pl.ANY)],
        out_specs=pl.BlockSpec((tc, d), lambda i, pos: (i, 0)),
        scratch_shapes=[pltpu.VMEM((2, TOP_K, tc * n_chunk, LANES), y_rows.dtype),
                        pltpu.SemaphoreType.DMA((2,))],
    )
    return pl.pallas_call(
        _combine_kernel,
        grid_spec=grid_spec,
        out_shape=jax.ShapeDtypeStruct((t, d), F32),
        compiler_params=pltpu.CompilerParams(
            dimension_semantics=("arbitrary",), vmem_limit_bytes=VMEM_LIMIT),
        name="combine",
    )(pos, h2, gates, g_final, y_rows)


def _pad_heads(w, n_heads, width, offset=0):
    k = w.shape[0]
    w = w.reshape(k, n_heads, width)
    w = jnp.pad(w, ((0, 0), (0, 0), (offset, LANES - width - offset)))
    return w.reshape(k, n_heads * LANES)


def _rot_cols(w):
    half = w.shape[-1] // 2
    return jnp.concatenate([-w[..., half:], w[..., :half]], axis=-1)


def kernel(x, mem, positions, g_mix, w_in, g_q_a, w_q_b, g_kv_a, w_kv_b, rel_bias, g_out_a, g_out_b,
           w_o, g_xattn, g_mem, w_mq, w_mkv, w_mo, g_moe, w_router, b_router, w_gate_up, b_gate_up,
           w_down, b_down, g_final):
    b, s, d = x.shape
    t = b * s
    assert g_mix.shape[0] == 1, "single-layer block: the final norm is fused into the last stage"

    inv_freq = ROPE_THETA ** (-jnp.arange(0, QK_ROPE_DIM, 2, dtype=F32) / QK_ROPE_DIM)
    ang = positions.astype(F32)[..., None] * inv_freq
    cos, sin = jnp.cos(ang), jnp.sin(ang)
    pad_lo, pad_hi = QK_NOPE_DIM, LANES - QK_NOPE_DIM - QK_ROPE_DIM
    cos128 = jnp.concatenate([jnp.ones((b, s, pad_lo), F32), cos, cos, jnp.ones((b, s, pad_hi), F32)], -1)
    sin128 = jnp.concatenate([jnp.zeros((b, s, pad_lo), F32), sin, sin, jnp.zeros((b, s, pad_hi), F32)], -1)
    tbl = jnp.pad(rel_bias.T.astype(F32), ((0, 0), (0, LANES - NUM_BUCKETS)))

    h = x
    for l in range(1):
        c0 = 3 * WIDTH_A + Q_LORA_RANK + KV_LORA_RANK
        w_kpe = w_in[l][:, c0:c0 + QK_ROPE_DIM]
        place = lambda w: jnp.pad(w, ((0, 0), (QK_NOPE_DIM, LANES - QK_NOPE_DIM - QK_ROPE_DIM)))
        w_in_ext = jnp.concatenate([w_in[l][:, :c0], place(w_kpe), place(_rot_cols(w_kpe))], 1).astype(BF16)
        dq = QK_NOPE_DIM + QK_ROPE_DIM
        wq3 = w_q_b[l].reshape(Q_LORA_RANK, N_HEADS_B, dq)
        wq = _pad_heads(wq3.reshape(Q_LORA_RANK, -1), N_HEADS_B, dq).astype(BF16)
        wq_rot3 = jnp.concatenate([jnp.zeros_like(wq3[..., :QK_NOPE_DIM]), _rot_cols(wq3[..., QK_NOPE_DIM:])], -1)
        wqr = _pad_heads(wq_rot3.reshape(Q_LORA_RANK, -1), N_HEADS_B, dq).astype(BF16)
        wkv3 = w_kv_b[l].reshape(KV_LORA_RANK, N_HEADS_B, QK_NOPE_DIM + V_DIM_B)
        wk = _pad_heads(wkv3[..., :QK_NOPE_DIM].reshape(KV_LORA_RANK, -1), N_HEADS_B, QK_NOPE_DIM).astype(BF16)
        wv = _pad_heads(wkv3[..., QK_NOPE_DIM:].reshape(KV_LORA_RANK, -1), N_HEADS_B, V_DIM_B).astype(BF16)

        *views, qm, km, vm = _proj(h, cos128, sin128, g_mix[l][None], w_in_ext, g_q_a[l][None], wq, wqr,
                                   g_kv_a[l][None], wk, wv, tm=256)
        pats = _dilated_all(views, positions, tbl)
        ob = _mla(qm, km, vm, tq=1024, tk=512)
        kmem, vmem = _memkv(mem, g_mem[l][None], w_mkv[l].astype(BF16))

        g_out_b_pad = _pad_heads(g_out_b[l][None], N_HEADS_B, V_DIM_B)
        w_o_b_pad = _pad_heads(w_o[l][WIDTH_A:].T, N_HEADS_B, V_DIM_B).T.astype(BF16)
        w_mq_s = (w_mq[l] * ((d // N_HEADS_MEM) ** -0.5)).astype(BF16)
        w_router_pad = jnp.pad(w_router[l], ((0, 0), (0, LANES - N_EXPERTS)))
        b_router_pad = jnp.pad(b_router[l][None], ((0, 0), (0, LANES - N_EXPERTS)), constant_values=NEG_INF)
        h2, xn_rows, logits = _post(h, pats[0::2], pats[1::2], ob, g_out_a[l][None],
                               g_out_b_pad, w_o[l][:WIDTH_A].astype(BF16), w_o_b_pad, g_xattn[l][None],
                               w_mq_s, kmem, vmem, w_mo[l].astype(BF16), g_moe[l][None], w_router_pad,
                               b_router_pad, tm=256)

        gates, eidx, rank, counts = _route(logits.reshape(t, LANES), tr=256)
        counts = counts[0, :N_EXPERTS]
        blk = EXPERT_BLOCK
        padded = ((counts + blk - 1) // blk) * blk
        ends = jnp.cumsum(padded)
        pad_start = ends - padded
        eidx4, rank4 = eidx[:, :TOP_K], rank[:, :TOP_K]
        expert_ids = jnp.arange(N_EXPERTS, dtype=jnp.int32)
        pos = (jnp.sum(jnp.where(eidx4[..., None] == expert_ids, pad_start, 0), axis=-1)
               + rank4).astype(jnp.int32).reshape(-1)
        n_blk = t * TOP_K // blk + N_EXPERTS
        block_expert = jnp.minimum(
            jnp.sum(ends[None, :] <= (jnp.arange(n_blk) * blk)[:, None], axis=1),
            N_EXPERTS - 1).astype(jnp.int32)
        n_used = (ends[-1] // blk).astype(jnp.int32)[None]
        ends0 = jnp.concatenate([jnp.zeros((1,), jnp.int32), ends.astype(jnp.int32)])
        xs_rows = _dispatch(pos, ends0, xn_rows, t, n_blk * blk)
        y_rows = _experts(block_expert, n_used, xs_rows, w_gate_up[l], b_gate_up[l], w_down[l], b_down[l])
        out = _combine(pos, h2.reshape(t, d), gates, g_final[None], y_rows, tc=128)
        h = out.reshape(b, s, d)
    return h
```

```python
import functools
import math

import jax
import jax.numpy as jnp
from jax import lax
from jax.experimental import pallas as pl
from jax.experimental.pallas import tpu as pltpu

F32 = jnp.float32
BF16 = jnp.bfloat16

LANES = 128
EPS = 1e-6
NEG_INF = -1e30
LOG2_E = math.log2(math.e)

N_HEADS_A = 8
HEAD_DIM_A = 64
WIDTH_A = N_HEADS_A * HEAD_DIM_A
DILATED_PATTERNS = ((128, 1), (512, 4), (2048, 16))
N_HEADS_B = 8
QK_NOPE_DIM = 64
QK_ROPE_DIM = 32
V_DIM_B = 64
Q_LORA_RANK = 256
KV_LORA_RANK = 128
ROPE_THETA = 10000.0
NUM_BUCKETS = 32
MAX_DISTANCE = 1024
N_HEADS_MEM = 4
N_EXPERTS = 32
TOP_K = 4
SWIGLU_LIMIT = 7.0
SWIGLU_ALPHA = 1.702

Q_BLOCK = 128
K_WINDOW = 256
HALF_WINDOW = (K_WINDOW - Q_BLOCK) // 2
EXPERT_BLOCK = 512
DISPATCH_CHUNK = 512
POST_CHAINS = 2
VMEM_LIMIT = 56 * 1024 * 1024


def _rms(x, g):
    return x * lax.rsqrt(jnp.mean(x * x, axis=-1, keepdims=True) + EPS) * g


def _dot(a, b):
    return jnp.dot(a, b, preferred_element_type=F32)


def _dot_nt(a, b):
    return lax.dot_general(a, b, (((1,), (1,)), ((), ())), preferred_element_type=F32)


def _load_token_rows(ref, n_tok):
    n_chunk = ref.shape[0] // n_tok
    return jnp.concatenate([ref[pl.ds(c, n_tok, stride=n_chunk), :] for c in range(n_chunk)], axis=1)


def _pack_bf16_pairs(x):
    half = x.shape[1] // 2
    bits = lambda v: lax.bitcast_convert_type(v.astype(BF16).astype(F32), jnp.uint32)
    return bits(x[:, half:]) | (bits(x[:, :half]) >> 16)


def _unpack_bf16_pairs(w):
    lo = lax.bitcast_convert_type(w << 16, F32)
    hi = lax.bitcast_convert_type(w & jnp.uint32(0xFFFF0000), F32)
    return lo, hi


def _store_token_rows(ref, value):
    n_tok = value.shape[0]
    n_chunk = value.shape[1] // LANES
    for c in range(n_chunk):
        ref[pl.ds(c, n_tok, stride=n_chunk), :] = value[:, c * LANES:(c + 1) * LANES]


def _proj_kernel(x_ref, cos_ref, sin_ref, gmix_ref, win_ref, gq_ref, wq_ref, wqr_ref,
                 gkv_ref, wk_ref, wv_ref,
                 q1_ref, k1_ref, v1_ref, q4_ref, k4_ref, v4_ref, q16_ref, k16_ref, v16_ref,
                 qm_ref, km_ref, vm_ref, za_scr):
    xn = _rms(x_ref[0], gmix_ref[...])
    z = _dot(xn.astype(BF16), win_ref[...])
    tm = z.shape[0]
    c0 = 3 * WIDTH_A
    per_group = WIDTH_A // LANES
    for c in range(3 * per_group):
        chunk = z[:, c * LANES:(c + 1) * LANES]
        za_scr[c] = chunk * (HEAD_DIM_A ** -0.5 * LOG2_E) if c < per_group else chunk
    for dil, refs in ((1, (q1_ref, k1_ref, v1_ref)), (4, (q4_ref, k4_ref, v4_ref)),
                      (16, (q16_ref, k16_ref, v16_ref))):
        for r in range(dil):
            for c in range(3 * per_group):
                rows = za_scr[c, pl.ds(r, tm // dil, stride=dil), :].astype(BF16)
                col = r * WIDTH_A + (c % per_group) * LANES
                refs[c // per_group][0, :, col:col + LANES] = rows
    zqn = _rms(z[:, c0:c0 + Q_LORA_RANK], gq_ref[...]).astype(BF16)
    zkvn = _rms(z[:, c0 + Q_LORA_RANK:c0 + Q_LORA_RANK + KV_LORA_RANK], gkv_ref[...]).astype(BF16)
    c1 = c0 + Q_LORA_RANK + KV_LORA_RANK
    cos = cos_ref[0]
    sin = sin_ref[0]
    kpe = z[:, c1:c1 + LANES] * cos + z[:, c1 + LANES:c1 + 2 * LANES] * sin
    q = _dot(zqn, wq_ref[...])
    qr = _dot(zqn, wqr_ref[...])
    kn = _dot(zkvn, wk_ref[...])
    vv = _dot(zkvn, wv_ref[...])
    lane = lax.broadcasted_iota(jnp.int32, (1, LANES), 1)
    ones_col = jnp.where(lane == V_DIM_B, 1.0, 0.0).astype(F32)
    scale = (QK_NOPE_DIM + QK_ROPE_DIM) ** -0.5 * LOG2_E
    for h in range(N_HEADS_B):
        sl = slice(h * LANES, (h + 1) * LANES)
        qm_ref[0, h] = ((q[:, sl] * cos + qr[:, sl] * sin) * scale).astype(BF16)
        km_ref[0, h] = (kn[:, sl] + kpe).astype(BF16)
        vm_ref[0, h] = (vv[:, sl] + ones_col).astype(BF16)


def _proj(x, cos128, sin128, g_mix, w_in_ext, g_q, wq, wqr, g_kv, wk, wv, tm):
    b, s, d = x.shape
    n_in = w_in_ext.shape[1]
    hw = N_HEADS_B * LANES
    full = lambda shape: pl.BlockSpec(shape, lambda bi, i: (0,) * len(shape))
    row = lambda w: pl.BlockSpec((1, tm, w), lambda bi, i: (bi, i, 0))
    head = pl.BlockSpec((1, N_HEADS_B, tm, LANES), lambda bi, i: (bi, 0, i, 0))
    dils = [dil for _, dil in DILATED_PATTERNS]
    view_specs = [pl.BlockSpec((1, tm // dil, dil * WIDTH_A), lambda bi, i: (bi, i, 0))
                  for dil in dils for _ in range(3)]
    view_shapes = [jax.ShapeDtypeStruct((b, s // dil, dil * WIDTH_A), BF16)
                   for dil in dils for _ in range(3)]
    return pl.pallas_call(
        _proj_kernel,
        grid=(b, s // tm),
        in_specs=[row(d), row(LANES), row(LANES), full((1, d)), full((d, n_in)),
                  full((1, Q_LORA_RANK)), full((Q_LORA_RANK, hw)), full((Q_LORA_RANK, hw)),
                  full((1, KV_LORA_RANK)), full((KV_LORA_RANK, hw)), full((KV_LORA_RANK, hw))],
        out_specs=view_specs + [head, head, head],
        out_shape=view_shapes + [jax.ShapeDtypeStruct((b, N_HEADS_B, s, LANES), BF16)] * 3,
        scratch_shapes=[pltpu.VMEM((3 * WIDTH_A // LANES, tm, LANES), F32)],
        compiler_params=pltpu.CompilerParams(
            dimension_semantics=("parallel", "parallel"), vmem_limit_bytes=VMEM_LIMIT),
        name="proj",
    )(x, cos128, sin128, g_mix, w_in_ext, g_q, wq, wqr, g_kv, wk, wv)


_LOG_BUCKET_STARTS = tuple(
    next(n for n in range(8, 4096)
         if int(math.log(n / 8) / math.log(MAX_DISTANCE / 8) * 8) >= t)
    for t in range(1, 8))


def _rel_bucket(rel):
    n = jnp.abs(rel)
    large = jnp.full(rel.shape, NUM_BUCKETS // 4, jnp.int32)
    for start in _LOG_BUCKET_STARTS:
        large = large + jnp.where(n >= start, 1, 0)
    mag = jnp.where(n < NUM_BUCKETS // 4, n, large)
    return mag + jnp.where(rel > 0, NUM_BUCKETS // 2, 0)


def _window_start(j, sub_len):
    return jnp.clip(j * Q_BLOCK - HALF_WINDOW, 0, sub_len - K_WINDOW)


def _bias_kernel(off_ref, qpos_ref, kpos_ref, tbl_ref, out_ref):
    rel = kpos_ref[0] - qpos_ref[0]
    bucket = _rel_bucket(rel)
    delta = (off_ref[pl.program_id(0)] + lax.broadcasted_iota(jnp.int32, (1, K_WINDOW), 1)
             - lax.broadcasted_iota(jnp.int32, (Q_BLOCK, 1), 0))
    valid = jnp.abs(delta) <= HALF_WINDOW
    for h in range(N_HEADS_A):
        tbl = jnp.broadcast_to(tbl_ref[h:h + 1, :], (Q_BLOCK, LANES))
        bias = jnp.concatenate(
            [jnp.take_along_axis(tbl, bucket[:, c * LANES:(c + 1) * LANES], axis=1)
             for c in range(K_WINDOW // LANES)], axis=1)
        out_ref[0, h] = jnp.where(valid, bias * LOG2_E, NEG_INF)


def _bias_tiles(offs, qpos, kpos, tbl):
    n = offs.shape[0]
    grid_spec = pltpu.PrefetchScalarGridSpec(
        num_scalar_prefetch=1,
        grid=(n,),
        in_specs=[pl.BlockSpec((1, Q_BLOCK, 1), lambda t, off: (t, 0, 0)),
                  pl.BlockSpec((1, 1, K_WINDOW), lambda t, off: (t, 0, 0)),
                  pl.BlockSpec((N_HEADS_A, LANES), lambda t, off: (0, 0))],
        out_specs=pl.BlockSpec((1, N_HEADS_A, Q_BLOCK, K_WINDOW), lambda t, off: (t, 0, 0, 0)),
    )
    return pl.pallas_call(
        _bias_kernel,
        grid_spec=grid_spec,
        out_shape=jax.ShapeDtypeStruct((n, N_HEADS_A, Q_BLOCK, K_WINDOW), F32),
        compiler_params=pltpu.CompilerParams(
            dimension_semantics=("arbitrary",), vmem_limit_bytes=VMEM_LIMIT),
        name="bias_tiles",
    )(offs, qpos, kpos, tbl)


def _dilated_kernel(q_ref, k_ref, v_ref, bias_ref, o_ref, lse_ref, s_scr, p_scr, *, sub_len):
    j = pl.program_id(2)
    kstart = pl.multiple_of(_window_start(j, sub_len), HALF_WINDOW)
    q = q_ref[0]
    kw = k_ref[0, pl.ds(kstart, K_WINDOW), :]
    vw = v_ref[0, pl.ds(kstart, K_WINDOW), :]
    first = lax.broadcasted_iota(jnp.int32, (1, LANES), 1) < HEAD_DIM_A
    pair = lambda h: slice((h // 2) * LANES, (h // 2 + 1) * LANES)
    for h in range(N_HEADS_A):
        own = first if h % 2 == 0 else jnp.logical_not(first)
        qh = jnp.where(own, q[:, pair(h)], jnp.zeros_like(q[:, pair(h)]))
        s_scr[h] = _dot_nt(qh, kw[:, pair(h)]) + bias_ref[0, h]
    stats = []
    for h in range(N_HEADS_A):
        s = s_scr[h]
        m = jnp.max(s, axis=1, keepdims=True)
        e = jnp.exp2(s - m)
        l = jnp.sum(e, axis=1, keepdims=True)
        p_scr[h] = e.astype(BF16)
        stats.append((1.0 / l, m + jnp.log2(l)))
    for h in range(0, N_HEADS_A, 2):
        (r0, lse0), (r1, lse1) = stats[h], stats[h + 1]
        o0 = _dot(p_scr[h], vw[:, pair(h)])
        o1 = _dot(p_scr[h + 1], vw[:, pair(h)])
        o_ref[0, :, pair(h)] = jnp.where(first, o0 * r0, o1 * r1)
        lse_ref[0, :, pair(h)] = jnp.where(first, lse0, lse1)


def _dilated(qv, kv, vv, bias, bias_index, dil):
    b, sub_len, _ = qv.shape
    w = WIDTH_A
    assert sub_len >= K_WINDOW and sub_len % Q_BLOCK == 0
    nq = sub_len // Q_BLOCK
    qspec = pl.BlockSpec((1, Q_BLOCK, w), lambda bi, r, j: (bi, j, r))
    kvspec = pl.BlockSpec((1, sub_len, w), lambda bi, r, j: (bi, 0, r))
    bspec = pl.BlockSpec((1, N_HEADS_A, Q_BLOCK, K_WINDOW),
                         lambda bi, r, j: (bias_index(bi, r, j, nq), 0, 0, 0))
    return pl.pallas_call(
        functools.partial(_dilated_kernel, sub_len=sub_len),
        grid=(b, dil, nq),
        in_specs=[qspec, kvspec, kvspec, bspec],
        out_specs=[qspec, qspec],
        out_shape=[jax.ShapeDtypeStruct((b, sub_len, dil * w), F32)] * 2,
        scratch_shapes=[pltpu.VMEM((N_HEADS_A, Q_BLOCK, K_WINDOW), F32),
                        pltpu.VMEM((N_HEADS_A, Q_BLOCK, K_WINDOW), BF16)],
        compiler_params=pltpu.CompilerParams(
            dimension_semantics=("parallel", "parallel", "arbitrary"),
            vmem_limit_bytes=VMEM_LIMIT),
        name=f"dilated{dil}",
    )(qv, kv, vv, bias)


def _dilated_all(views, positions, tbl):
    b, s = positions.shape
    dils = [dil for _, dil in DILATED_PATTERNS]
    consecutive = jnp.all(positions[:, 1:] - positions[:, :-1] == 1)

    def run(bias, index_fns):
        outs = []
        for g, dil in enumerate(dils):
            qv, kv, vv = views[3 * g:3 * g + 3]
            outs.extend(_dilated(qv, kv, vv, bias, index_fns[g], dil))
        return tuple(outs)

    def shared_tiles():
        offs, qpos, kpos, fns = [], [], [], []
        for g, dil in enumerate(dils):
            for off in (0, -HALF_WINDOW, -2 * HALF_WINDOW):
                offs.append(off)
                qpos.append(dil * jnp.arange(Q_BLOCK, dtype=jnp.int32))
                kpos.append(dil * (off + jnp.arange(K_WINDOW, dtype=jnp.int32)))
            fns.append(lambda bi, r, j, nq, g=g:
                       3 * g + jnp.where(j == 0, 0, jnp.where(j == nq - 1, 2, 1)))
        bias = _bias_tiles(jnp.array(offs, jnp.int32), jnp.stack(qpos)[:, :, None],
                           jnp.stack(kpos)[:, None, :], tbl)
        return run(bias, fns)

    def per_block_tiles():
        offs, qpos, kpos, fns = [], [], [], []
        base = 0
        for g, dil in enumerate(dils):
            sub_len = s // dil
            nq = sub_len // Q_BLOCK
            pos_t = positions.reshape(b, sub_len, dil).transpose(0, 2, 1).reshape(b * dil, sub_len)
            starts = _window_start(jnp.arange(nq), sub_len)
            win = starts[:, None] + jnp.arange(K_WINDOW)[None, :]
            qpos.append(pos_t.reshape(b * dil * nq, Q_BLOCK))
            kpos.append(pos_t[:, win].reshape(b * dil * nq, K_WINDOW))
            offs.append(jnp.tile(starts - jnp.arange(nq) * Q_BLOCK, b * dil))
            fns.append(lambda bi, r, j, nq, base=base, dil=dil: base + (bi * dil + r) * nq + j)
            base += b * dil * nq
        bias = _bias_tiles(jnp.concatenate(offs).astype(jnp.int32),
                           jnp.concatenate(qpos)[:, :, None], jnp.concatenate(kpos)[:, None, :], tbl)
        return run(bias, fns)

    return lax.cond(consecutive, shared_tiles, per_block_tiles)


def _mla_kernel(q_ref, k_ref, v_ref, o_ref, *, tk):
    q = q_ref[0, 0]
    tq = q.shape[0]
    n_kv = k_ref.shape[2] // tk

    def body(i, carry):
        m, acc = carry
        start = pl.multiple_of(i * tk, tk)
        k = k_ref[0, 0, pl.ds(start, tk), :]
        v = v_ref[0, 0, pl.ds(start, tk), :]
        s = _dot_nt(q, k)
        m_new = jnp.maximum(m, jnp.max(s, axis=1, keepdims=True))
        p = jnp.exp2(s - m_new)
        acc = jnp.exp2(m - m_new) * acc + _dot(p.astype(BF16), v)
        return m_new, acc

    m0 = jnp.full((tq, 1), NEG_INF, F32)
    _, acc = lax.fori_loop(0, n_kv, body, (m0, jnp.zeros((tq, LANES), F32)), unroll=True)
    lane = lax.broadcasted_iota(jnp.int32, (1, LANES), 1)
    denom = jnp.sum(jnp.where(lane == V_DIM_B, acc, 0.0), axis=1, keepdims=True)
    o_ref[0, 0] = jnp.where(lane < V_DIM_B, acc / denom, 0.0).astype(BF16)


def _mla(qm, km, vm, tq, tk):
    b, nh, s, _ = qm.shape
    qspec = pl.BlockSpec((1, 1, tq, LANES), lambda bi, h, i: (bi, h, i, 0))
    kvspec = pl.BlockSpec((1, 1, s, LANES), lambda bi, h, i: (bi, h, 0, 0))
    return pl.pallas_call(
        functools.partial(_mla_kernel, tk=tk),
        grid=(b, nh, s // tq),
        in_specs=[qspec, kvspec, kvspec],
        out_specs=qspec,
        out_shape=jax.ShapeDtypeStruct((b, nh, s, LANES), BF16),
        compiler_params=pltpu.CompilerParams(
            dimension_semantics=("parallel", "parallel", "arbitrary"),
            vmem_limit_bytes=VMEM_LIMIT),
        name="mla",
    )(qm, km, vm)


def _memkv_kernel(mem_ref, g_ref, w_ref, k_ref, v_ref):
    d = mem_ref.shape[2]
    kv = _dot(_rms(mem_ref[0], g_ref[...]).astype(BF16), w_ref[...])
    k_ref[0] = kv[:, :d].astype(BF16)
    v_ref[0] = kv[:, d:].astype(BF16)


def _memkv(mem, g_mem, w_mkv):
    b, m, d = mem.shape
    spec = pl.BlockSpec((1, m, d), lambda bi: (bi, 0, 0))
    return pl.pallas_call(
        _memkv_kernel,
        grid=(b,),
        in_specs=[spec, pl.BlockSpec((1, d), lambda bi: (0, 0)),
                  pl.BlockSpec((d, 2 * d), lambda bi: (0, 0))],
        out_specs=[spec, spec],
        out_shape=[jax.ShapeDtypeStruct((b, m, d), BF16)] * 2,
        compiler_params=pltpu.CompilerParams(
            dimension_semantics=("parallel",), vmem_limit_bytes=VMEM_LIMIT),
        name="memkv",
    )(mem, g_mem, w_mkv)


def _post_kernel(x_ref, o1_ref, o2_ref, o3_ref, l1_ref, l2_ref, l3_ref, ob_ref,
                 goa_ref, gob_ref, woa_ref, wob_ref, gx_ref, wmq_ref, km_ref, vm_ref, wmo_ref,
                 gmoe_ref, wr_ref, br_ref,
                 h_ref, xn_ref, logit_ref, *nat_scr):
    tm = x_ref.shape[1]
    n_chunk = WIDTH_A // LANES

    for view_ref, scr in zip((o2_ref, o3_ref, l2_ref, l3_ref), nat_scr):
        dil = view_ref.shape[2] // WIDTH_A
        for r in range(dil):
            for c in range(n_chunk):
                col = r * WIDTH_A + c * LANES
                scr[c, pl.ds(r, tm // dil, stride=dil), :] = view_ref[0, :, col:col + LANES]

    def natural(scr, rows):
        return jnp.concatenate([scr[c, rows, :] for c in range(n_chunk)], axis=1)

    rows_per_chain = tm // POST_CHAINS
    for ci in range(POST_CHAINS):
        rows = slice(ci * rows_per_chain, (ci + 1) * rows_per_chain)
        o1, o2, o3 = o1_ref[0, rows, :], natural(nat_scr[0], rows), natural(nat_scr[1], rows)
        l1, l2, l3 = l1_ref[0, rows, :], natural(nat_scr[2], rows), natural(nat_scr[3], rows)
        mx = jnp.maximum(jnp.maximum(l1, l2), l3)
        w1, w2, w3 = jnp.exp2(l1 - mx), jnp.exp2(l2 - mx), jnp.exp2(l3 - mx)
        oa = (w1 * o1 + w2 * o2 + w3 * o3) / (w1 + w2 + w3)
        oan = _rms(oa, goa_ref[...]).astype(BF16)
        ob = jnp.concatenate([ob_ref[0, h, rows, :] for h in range(N_HEADS_B)], axis=1).astype(F32)
        ms_b = jnp.sum(ob * ob, axis=1, keepdims=True) * (1.0 / (N_HEADS_B * V_DIM_B))
        obn = (ob * lax.rsqrt(ms_b + EPS) * gob_ref[...]).astype(BF16)
        h1 = x_ref[0, rows, :] + _dot(oan, woa_ref[...]) + _dot(obn, wob_ref[...])

        hn = _rms(h1, gx_ref[...]).astype(BF16)
        q = _dot(hn, wmq_ref[...]).astype(BF16)
        dh = q.shape[1] // N_HEADS_MEM
        heads = []
        for h in range(N_HEADS_MEM):
            sl = slice(h * dh, (h + 1) * dh)
            s = _dot_nt(q[:, sl], km_ref[0, :, sl])
            e = jnp.exp(s - jnp.max(s, axis=1, keepdims=True))
            p = e / jnp.sum(e, axis=1, keepdims=True)
            heads.append(_dot(p.astype(BF16), vm_ref[0, :, sl]))
        o = jnp.concatenate(heads, axis=1).astype(BF16)
        h2 = h1 + _dot(o, wmo_ref[...])
        h_ref[0, rows, :] = h2

        xn = _rms(h2, gmoe_ref[...])
        packed = _pack_bf16_pairs(xn)
        rows_per_token = packed.shape[1] // LANES
        _store_token_rows(xn_ref.at[pl.ds(ci * rows_per_chain * rows_per_token,
                                          rows_per_chain * rows_per_token)], packed)
        logit_ref[0, rows, :] = jnp.dot(xn, wr_ref[...], preferred_element_type=F32,
                                        precision=lax.Precision.HIGHEST) + br_ref[...]


def _post(x, o_pats, lse_pats, ob, g_out_a, g_out_b_pad, w_o_a, w_o_b_pad, g_xattn, w_mq, kmem,
          vmem, w_mo, g_moe, w_router_pad, b_router_pad, tm):
    b, s, d = x.shape
    m = kmem.shape[1]
    full = lambda shape: pl.BlockSpec(shape, lambda bi, i: (0,) * len(shape))
    row = lambda w: pl.BlockSpec((1, tm, w), lambda bi, i: (bi, i, 0))
    memspec = pl.BlockSpec((1, m, d), lambda bi, i: (bi, 0, 0))
    hw = N_HEADS_B * LANES
    views = [pl.BlockSpec((1, tm // dil, dil * WIDTH_A), lambda bi, i: (bi, i, 0))
             for _, dil in DILATED_PATTERNS]
    return pl.pallas_call(
        _post_kernel,
        grid=(b, s // tm),
        in_specs=[row(d)] + views + views
        + [pl.BlockSpec((1, N_HEADS_B, tm, LANES), lambda bi, i: (bi, 0, i, 0)),
           full((1, WIDTH_A)), full((1, hw)), full((WIDTH_A, d)), full((hw, d)),
           full((1, d)), full((d, d)), memspec, memspec, full((d, d)),
           full((1, d)), full((d, LANES)), full((1, LANES))],
        out_specs=[row(d), pl.BlockSpec((tm * d // 2 // LANES, LANES), lambda bi, i: (bi * (s // tm) + i, 0)),
                   row(LANES)],
        out_shape=[jax.ShapeDtypeStruct((b, s, d), F32),
                   jax.ShapeDtypeStruct((b * s * d // 2 // LANES, LANES), jnp.uint32),
                   jax.ShapeDtypeStruct((b, s, LANES), F32)],
        scratch_shapes=[pltpu.VMEM((WIDTH_A // LANES, tm, LANES), F32)] * 4,
        compiler_params=pltpu.CompilerParams(
            dimension_semantics=("parallel", "parallel"), vmem_limit_bytes=VMEM_LIMIT),
        name="post",
    )(x, *o_pats, *lse_pats, ob, g_out_a, g_out_b_pad, w_o_a, w_o_b_pad, g_xattn, w_mq, kmem,
      vmem, w_mo, g_moe, w_router_pad, b_router_pad)


def _route_kernel(logit_ref, gate_ref, eidx_ref, rank_ref, count_ref, carry_ref):
    @pl.when(pl.program_id(0) == 0)
    def _():
        carry_ref[...] = jnp.zeros_like(carry_ref)

    l = logit_ref[...]
    tr = l.shape[0]
    lane_i = lax.broadcasted_iota(jnp.int32, l.shape, 1)
    lane = lane_i.astype(F32)
    vals, idxs = [], []
    for _ in range(TOP_K):
        m = jnp.max(l, axis=1, keepdims=True)
        idx = jnp.min(jnp.where(l == m, lane, float(LANES)), axis=1, keepdims=True)
        vals.append(m)
        idxs.append(idx)
        l = jnp.where(lane == idx, -jnp.inf, l)
    exps = [jnp.exp(v - vals[0]) for v in vals]
    denom = exps[0] + exps[1] + exps[2] + exps[3]
    onehot = jnp.zeros(l.shape, F32)
    for idx in idxs:
        onehot = onehot + jnp.where(lane == idx, 1.0, 0.0)
    r = lax.broadcasted_iota(jnp.int32, (tr, tr), 0)
    c = lax.broadcasted_iota(jnp.int32, (tr, tr), 1)
    tri = jnp.where(c < r, 1.0, 0.0).astype(BF16)
    before = _dot(tri, onehot.astype(BF16)) + carry_ref[...]
    gate = jnp.zeros(l.shape, F32)
    eidx = jnp.zeros(l.shape, jnp.int32)
    rank = jnp.zeros(l.shape, jnp.int32)
    for k in range(TOP_K):
        rk = jnp.sum(jnp.where(lane == idxs[k], before, 0.0), axis=1, keepdims=True)
        gate = jnp.where(lane_i == k, exps[k] / denom, gate)
        eidx = jnp.where(lane_i == k, idxs[k].astype(jnp.int32), eidx)
        rank = jnp.where(lane_i == k, rk.astype(jnp.int32), rank)
    gate_ref[...] = gate
    eidx_ref[...] = eidx
    rank_ref[...] = rank
    total = carry_ref[...] + jnp.sum(onehot, axis=0, keepdims=True)
    carry_ref[...] = total
    count_ref[...] = total.astype(jnp.int32)


def _route(logits, tr):
    t = logits.shape[0]
    spec = pl.BlockSpec((tr, LANES), lambda i: (i, 0))
    return pl.pallas_call(
        _route_kernel,
        grid=(t // tr,),
        in_specs=[spec],
        out_specs=[spec, spec, spec, pl.BlockSpec((1, LANES), lambda i: (0, 0))],
        out_shape=[jax.ShapeDtypeStruct((t, LANES), F32), jax.ShapeDtypeStruct((t, LANES), jnp.int32),
                   jax.ShapeDtypeStruct((t, LANES), jnp.int32),
                   jax.ShapeDtypeStruct((1, LANES), jnp.int32)],
        scratch_shapes=[pltpu.VMEM((1, LANES), F32)],
        compiler_params=pltpu.CompilerParams(
            dimension_semantics=("arbitrary",), vmem_limit_bytes=VMEM_LIMIT),
        name="route",
    )(logits)


def _dispatch_kernel(pos_ref, ends_ref, x_ref, xs_hbm, zbuf, sem, zsem):
    c = pl.program_id(0)
    n_chunk = x_ref.shape[0] // DISPATCH_CHUNK
    blk_rows = zbuf.shape[0]

    def token(ref, t):
        return ref.at[pl.ds(pl.multiple_of(t * n_chunk, n_chunk), n_chunk)]

    @pl.when(c == 0)
    def _():
        zbuf[...] = jnp.zeros_like(zbuf)

        def zero_copy(start):
            return pltpu.make_async_copy(
                zbuf, xs_hbm.at[pl.ds(pl.multiple_of(start * n_chunk, n_chunk), blk_rows)], zsem)

        blk = blk_rows // n_chunk
        used_end = ends_ref[N_EXPERTS]
        starts = [(ends_ref[e + 1] > ends_ref[e], ends_ref[e + 1] - blk) for e in range(N_EXPERTS)]
        starts += [((used_end + (j + 1) * blk) * n_chunk <= xs_hbm.shape[0], used_end + j * blk)
                   for j in range(N_EXPERTS)]
        for cond, start in starts:
            @pl.when(cond)
            def _():
                zero_copy(start).start()
        for cond, start in starts:
            @pl.when(cond)
            def _():
                zero_copy(start).wait()

    def body(t, carry):
        tok = c * DISPATCH_CHUNK + t
        for k in range(TOP_K):
            pltpu.make_async_copy(token(x_ref, t), token(xs_hbm, pos_ref[tok * TOP_K + k]), sem).start()
        return carry
    lax.fori_loop(0, DISPATCH_CHUNK, body, 0, unroll=4)

    for k in range(TOP_K):
        pltpu.make_async_copy(x_ref, xs_hbm.at[pl.ds(0, x_ref.shape[0])], sem).wait()


def _dispatch(pos, ends0, xn_rows, n_tok, n_slots):
    n_chunk = xn_rows.shape[0] // n_tok
    grid_spec = pltpu.PrefetchScalarGridSpec(
        num_scalar_prefetch=2,
        grid=(n_tok // DISPATCH_CHUNK,),
        in_specs=[pl.BlockSpec((DISPATCH_CHUNK * n_chunk, LANES), lambda c, pos, ends: (c, 0))],
        out_specs=pl.BlockSpec(memory_space=pl.ANY),
        scratch_shapes=[pltpu.VMEM((EXPERT_BLOCK * n_chunk, LANES), xn_rows.dtype),
                        pltpu.SemaphoreType.DMA(()), pltpu.SemaphoreType.DMA(())],
    )
    return pl.pallas_call(
        _dispatch_kernel,
        grid_spec=grid_spec,
        out_shape=jax.ShapeDtypeStruct((n_slots * n_chunk, LANES), xn_rows.dtype),
        compiler_params=pltpu.CompilerParams(
            dimension_semantics=("arbitrary",), vmem_limit_bytes=VMEM_LIMIT),
        name="dispatch",
    )(pos, ends0, xn_rows)


def _expert_kernel(be_ref, nused_ref, xs_ref, wgu_ref, bgu_ref, wd_ref, bd_ref, y_ref, wgu_bf, wd_bf):
    i = pl.program_id(0)

    @pl.when(i < nused_ref[0])
    def _():
        @pl.when(jnp.logical_or(i == 0, be_ref[i] != be_ref[jnp.maximum(i - 1, 0)]))
        def _():
            wgu_bf[...] = wgu_ref[0].astype(BF16)
            wd_bf[...] = wd_ref[0].astype(BF16)

        x_lo, x_hi = _unpack_bf16_pairs(_load_token_rows(xs_ref, EXPERT_BLOCK))
        x = jnp.concatenate([x_lo.astype(BF16), x_hi.astype(BF16)], axis=1)
        gu = _dot(x, wgu_bf[...]) + bgu_ref[0]
        de = gu.shape[1] // 2
        gate = jnp.minimum(gu[:, :de], SWIGLU_LIMIT)
        up = jnp.clip(gu[:, de:], -SWIGLU_LIMIT, SWIGLU_LIMIT)
        hmid = (up + 1.0) * (gate * jax.nn.sigmoid(SWIGLU_ALPHA * gate))
        y = _dot(hmid.astype(BF16), wd_bf[...]) + bd_ref[0]
        _store_token_rows(y_ref, _pack_bf16_pairs(y))

    @pl.when(i >= nused_ref[0])
    def _():
        y_ref[...] = jnp.zeros_like(y_ref)


def _experts(block_expert, n_used, xs_rows, w_gate_up, b_gate_up, w_down, b_down):
    d, de2 = w_gate_up.shape[1:]
    n_chunk = d // 2 // LANES
    blk = EXPERT_BLOCK
    n_blk = xs_rows.shape[0] // (blk * n_chunk)
    grid_spec = pltpu.PrefetchScalarGridSpec(
        num_scalar_prefetch=2,
        grid=(n_blk,),
        in_specs=[pl.BlockSpec((blk * n_chunk, LANES),
                               lambda i, be, nu: (jnp.minimum(i, nu[0] - 1), 0)),
                  pl.BlockSpec((1, d, de2), lambda i, be, nu: (be[i], 0, 0)),
                  pl.BlockSpec((1, 1, de2), lambda i, be, nu: (be[i], 0, 0)),
                  pl.BlockSpec((1, de2 // 2, d), lambda i, be, nu: (be[i], 0, 0)),
                  pl.BlockSpec((1, 1, d), lambda i, be, nu: (be[i], 0, 0))],
        out_specs=pl.BlockSpec((blk * n_chunk, LANES), lambda i, be, nu: (i, 0)),
        scratch_shapes=[pltpu.VMEM((d, de2), BF16), pltpu.VMEM((de2 // 2, d), BF16)],
    )
    return pl.pallas_call(
        _expert_kernel,
        grid_spec=grid_spec,
        out_shape=jax.ShapeDtypeStruct(xs_rows.shape, xs_rows.dtype),
        compiler_params=pltpu.CompilerParams(
            dimension_semantics=("arbitrary",), vmem_limit_bytes=VMEM_LIMIT),
        name="experts",
    )(block_expert, n_used, xs_rows, w_gate_up, b_gate_up.reshape(N_EXPERTS, 1, de2),
      w_down, b_down.reshape(N_EXPERTS, 1, d))


def _combine_kernel(pos_ref, h_ref, gate_ref, gfin_ref, y_hbm, out_ref, buf, sems):
    i = pl.program_id(0)
    n = pl.num_programs(0)
    slot = lax.rem(i, 2)
    tc = h_ref.shape[0]
    n_chunk = buf.shape[2] // tc

    def token(ref, t):
        return ref.at[pl.ds(pl.multiple_of(t * n_chunk, n_chunk), n_chunk)]

    def issue(tile, s):
        def body(r, carry):
            for k in range(TOP_K):
                row = pos_ref[(tile * tc + r) * TOP_K + k]
                pltpu.make_async_copy(token(y_hbm, row), token(buf.at[s, k], r), sems.at[s]).start()
            return carry
        lax.fori_loop(0, tc, body, 0, unroll=4)

    @pl.when(i == 0)
    def _():
        issue(0, 0)

    @pl.when(i + 1 < n)
    def _():
        issue(i + 1, 1 - slot)

    for k in range(TOP_K):
        pltpu.make_async_copy(y_hbm.at[pl.ds(0, tc * n_chunk)], buf.at[slot, k], sems.at[slot]).wait()
    gate = gate_ref[...]
    half = h_ref.shape[1] // 2
    lo_pieces, hi_pieces = [], []
    for c in range(n_chunk):
        y_lo = h_ref[:, c * LANES:(c + 1) * LANES]
        y_hi = h_ref[:, half + c * LANES:half + (c + 1) * LANES]
        for k in range(TOP_K):
            lo, hi = _unpack_bf16_pairs(buf[slot, k, pl.ds(c, tc, stride=n_chunk), :])
            y_lo = y_lo + gate[:, k:k + 1] * lo
            y_hi = y_hi + gate[:, k:k + 1] * hi
        lo_pieces.append(y_lo)
        hi_pieces.append(y_hi)
    out_ref[...] = _rms(jnp.concatenate(lo_pieces + hi_pieces, axis=1), gfin_ref[...])


def _combine(pos, h2, gates, g_final, y_rows, tc):
    t, d = h2.shape
    n_chunk = d // 2 // LANES
    grid_spec = pltpu.PrefetchScalarGridSpec(
        num_scalar_prefetch=1,
        grid=(t // tc,),
        in_specs=[pl.BlockSpec((tc, d), lambda i, pos: (i, 0)),
                  pl.BlockSpec((tc, LANES), lambda i, pos: (i, 0)),
                  pl.BlockSpec((1, d), lambda i, pos: (0, 0)),
                  pl.BlockSpec(memory_space=pl.ANY)],
        out_specs=pl.BlockSpec((tc, d), lambda i, pos: (i, 0)),
        scratch_shapes=[pltpu.VMEM((2, TOP_K, tc * n_chunk, LANES), y_rows.dtype),
                        pltpu.SemaphoreType.DMA((2,))],
    )
    return pl.pallas_call(
        _combine_kernel,
        grid_spec=grid_spec,
        out_shape=jax.ShapeDtypeStruct((t, d), F32),
        compiler_params=pltpu.CompilerParams(
            dimension_semantics=("arbitrary",), vmem_limit_bytes=VMEM_LIMIT),
        name="combine",
    )(pos, h2, gates, g_final, y_rows)


def _pad_heads(w, n_heads, width, offset=0):
    k = w.shape[0]
    w = w.reshape(k, n_heads, width)
    w = jnp.pad(w, ((0, 0), (0, 0), (offset, LANES - width - offset)))
    return w.reshape(k, n_heads * LANES)


def _rot_cols(w):
    half = w.shape[-1] // 2
    return jnp.concatenate([-w[..., half:], w[..., :half]], axis=-1)


def kernel(x, mem, positions, g_mix, w_in, g_q_a, w_q_b, g_kv_a, w_kv_b, rel_bias, g_out_a, g_out_b,
           w_o, g_xattn, g_mem, w_mq, w_mkv, w_mo, g_moe, w_router, b_router, w_gate_up, b_gate_up,
           w_down, b_down, g_final):
    b, s, d = x.shape
    t = b * s
    assert g_mix.shape[0] == 1, "single-layer block: the final norm is fused into the last stage"

    inv_freq = ROPE_THETA ** (-jnp.arange(0, QK_ROPE_DIM, 2, dtype=F32) / QK_ROPE_DIM)
    ang = positions.astype(F32)[..., None] * inv_freq
    cos, sin = jnp.cos(ang), jnp.sin(ang)
    pad_lo, pad_hi = QK_NOPE_DIM, LANES - QK_NOPE_DIM - QK_ROPE_DIM
    cos128 = jnp.concatenate([jnp.ones((b, s, pad_lo), F32), cos, cos, jnp.ones((b, s, pad_hi), F32)], -1)
    sin128 = jnp.concatenate([jnp.zeros((b, s, pad_lo), F32), sin, sin, jnp.zeros((b, s, pad_hi), F32)], -1)
    tbl = jnp.pad(rel_bias.T.astype(F32), ((0, 0), (0, LANES - NUM_BUCKETS)))

    h = x
    for l in range(1):
        c0 = 3 * WIDTH_A + Q_LORA_RANK + KV_LORA_RANK
        w_kpe = w_in[l][:, c0:c0 + QK_ROPE_DIM]
        place = lambda w: jnp.pad(w, ((0, 0), (QK_NOPE_DIM, LANES - QK_NOPE_DIM - QK_ROPE_DIM)))
        w_in_ext = jnp.concatenate([w_in[l][:, :c0], place(w_kpe), place(_rot_cols(w_kpe))], 1).astype(BF16)
        dq = QK_NOPE_DIM + QK_ROPE_DIM
        wq3 = w_q_b[l].reshape(Q_LORA_RANK, N_HEADS_B, dq)
        wq = _pad_heads(wq3.reshape(Q_LORA_RANK, -1), N_HEADS_B, dq).astype(BF16)
        wq_rot3 = jnp.concatenate([jnp.zeros_like(wq3[..., :QK_NOPE_DIM]), _rot_cols(wq3[..., QK_NOPE_DIM:])], -1)
        wqr = _pad_heads(wq_rot3.reshape(Q_LORA_RANK, -1), N_HEADS_B, dq).astype(BF16)
        wkv3 = w_kv_b[l].reshape(KV_LORA_RANK, N_HEADS_B, QK_NOPE_DIM + V_DIM_B)
        wk = _pad_heads(wkv3[..., :QK_NOPE_DIM].reshape(KV_LORA_RANK, -1), N_HEADS_B, QK_NOPE_DIM).astype(BF16)
        wv = _pad_heads(wkv3[..., QK_NOPE_DIM:].reshape(KV_LORA_RANK, -1), N_HEADS_B, V_DIM_B).astype(BF16)

        *views, qm, km, vm = _proj(h, cos128, sin128, g_mix[l][None], w_in_ext, g_q_a[l][None], wq, wqr,
                                   g_kv_a[l][None], wk, wv, tm=256)
        pats = _dilated_all(views, positions, tbl)
        ob = _mla(qm, km, vm, tq=1024, tk=512)
        kmem, vmem = _memkv(mem, g_mem[l][None], w_mkv[l].astype(BF16))

        g_out_b_pad = _pad_heads(g_out_b[l][None], N_HEADS_B, V_DIM_B)
        w_o_b_pad = _pad_heads(w_o[l][WIDTH_A:].T, N_HEADS_B, V_DIM_B).T.astype(BF16)
        w_mq_s = (w_mq[l] * ((d // N_HEADS_MEM) ** -0.5)).astype(BF16)
        w_router_pad = jnp.pad(w_router[l], ((0, 0), (0, LANES - N_EXPERTS)))
        b_router_pad = jnp.pad(b_router[l][None], ((0, 0), (0, LANES - N_EXPERTS)), constant_values=NEG_INF)
        h2, xn_rows, logits = _post(h, pats[0::2], pats[1::2], ob, g_out_a[l][None],
                               g_out_b_pad, w_o[l][:WIDTH_A].astype(BF16), w_o_b_pad, g_xattn[l][None],
                               w_mq_s, kmem, vmem, w_mo[l].astype(BF16), g_moe[l][None], w_router_pad,
                               b_router_pad, tm=256 * POST_CHAINS)

        gates, eidx, rank, counts = _route(logits.reshape(t, LANES), tr=256)
        counts = counts[0, :N_EXPERTS]
        blk = EXPERT_BLOCK
        padded = ((counts + blk - 1) // blk) * blk
        ends = jnp.cumsum(padded)
        pad_start = ends - padded
        eidx4, rank4 = eidx[:, :TOP_K], rank[:, :TOP_K]
        expert_ids = jnp.arange(N_EXPERTS, dtype=jnp.int32)
        pos = (jnp.sum(jnp.where(eidx4[..., None] == expert_ids, pad_start, 0), axis=-1)
               + rank4).astype(jnp.int32).reshape(-1)
        n_blk = t * TOP_K // blk + N_EXPERTS
        block_expert = jnp.minimum(
            jnp.sum(ends[None, :] <= (jnp.arange(n_blk) * blk)[:, None], axis=1),
            N_EXPERTS - 1).astype(jnp.int32)
        n_used = (ends[-1] // blk).astype(jnp.int32)[None]
        ends0 = jnp.concatenate([jnp.zeros((1,), jnp.int32), ends.astype(jnp.int32)])
        xs_rows = _dispatch(pos, ends0, xn_rows, t, n_blk * blk)
        y_rows = _experts(block_expert, n_used, xs_rows, w_gate_up[l], b_gate_up[l], w_down[l], b_down[l])
        out = _combine(pos, h2.reshape(t, d), gates, g_final[None], y_rows, tc=128)
        h = out.reshape(b, s, d)
    return h
```

```python
import functools
import math

import jax
import jax.numpy as jnp
from jax import lax
from jax.experimental import pallas as pl
from jax.experimental.pallas import tpu as pltpu

F32 = jnp.float32
BF16 = jnp.bfloat16

LANES = 128
EPS = 1e-6
NEG_INF = -1e30
LOG2_E = math.log2(math.e)

N_HEADS_A = 8
HEAD_DIM_A = 64
WIDTH_A = N_HEADS_A * HEAD_DIM_A
DILATED_PATTERNS = ((128, 1), (512, 4), (2048, 16))
N_HEADS_B = 8
QK_NOPE_DIM = 64
QK_ROPE_DIM = 32
V_DIM_B = 64
Q_LORA_RANK = 256
KV_LORA_RANK = 128
ROPE_THETA = 10000.0
NUM_BUCKETS = 32
MAX_DISTANCE = 1024
N_HEADS_MEM = 4
N_EXPERTS = 32
TOP_K = 4
SWIGLU_LIMIT = 7.0
SWIGLU_ALPHA = 1.702

Q_BLOCK = 128
K_WINDOW = 256
HALF_WINDOW = (K_WINDOW - Q_BLOCK) // 2
Q_BLOCKS_PER_STEP = 2
EXPERT_BLOCK = 512
DISPATCH_CHUNK = 512
POST_CHAINS = 2
VMEM_LIMIT = 56 * 1024 * 1024


def _rms(x, g):
    return x * lax.rsqrt(jnp.mean(x * x, axis=-1, keepdims=True) + EPS) * g


def _dot(a, b):
    return jnp.dot(a, b, preferred_element_type=F32)


def _dot_nt(a, b):
    return lax.dot_general(a, b, (((1,), (1,)), ((), ())), preferred_element_type=F32)


def _load_token_rows(ref, n_tok):
    n_chunk = ref.shape[0] // n_tok
    return jnp.concatenate([ref[pl.ds(c, n_tok, stride=n_chunk), :] for c in range(n_chunk)], axis=1)


def _pack_bf16_pairs(x):
    half = x.shape[1] // 2
    bits = lambda v: lax.bitcast_convert_type(v.astype(BF16).astype(F32), jnp.uint32)
    return bits(x[:, half:]) | (bits(x[:, :half]) >> 16)


def _unpack_bf16_pairs(w):
    lo = lax.bitcast_convert_type(w << 16, F32)
    hi = lax.bitcast_convert_type(w & jnp.uint32(0xFFFF0000), F32)
    return lo, hi


def _store_token_rows(ref, value):
    n_tok = value.shape[0]
    n_chunk = value.shape[1] // LANES
    for c in range(n_chunk):
        ref[pl.ds(c, n_tok, stride=n_chunk), :] = value[:, c * LANES:(c + 1) * LANES]


def _proj_kernel(x_ref, cs_ref, expand_ref, gmix_ref, win_ref, gq_ref, wq_ref, wqr_ref,
                 gkv_ref, wk_ref, wv_ref,
                 q1_ref, k1_ref, v1_ref, q4_ref, k4_ref, v4_ref, q16_ref, k16_ref, v16_ref,
                 qm_ref, km_ref, vm_ref, za_scr):
    xn = _rms(x_ref[0], gmix_ref[...])
    z = _dot(xn.astype(BF16), win_ref[...])
    tm = z.shape[0]
    c0 = 3 * WIDTH_A
    per_group = WIDTH_A // LANES
    for c in range(3 * per_group):
        chunk = z[:, c * LANES:(c + 1) * LANES]
        za_scr[c] = chunk * (HEAD_DIM_A ** -0.5 * LOG2_E) if c < per_group else chunk
    for dil, refs in ((1, (q1_ref, k1_ref, v1_ref)), (4, (q4_ref, k4_ref, v4_ref)),
                      (16, (q16_ref, k16_ref, v16_ref))):
        for r in range(dil):
            for c in range(3 * per_group):
                rows = za_scr[c, pl.ds(r, tm // dil, stride=dil), :].astype(BF16)
                col = r * WIDTH_A + (c % per_group) * LANES
                refs[c // per_group][0, :, col:col + LANES] = rows
    zqn = _rms(z[:, c0:c0 + Q_LORA_RANK], gq_ref[...]).astype(BF16)
    zkvn = _rms(z[:, c0 + Q_LORA_RANK:c0 + Q_LORA_RANK + KV_LORA_RANK], gkv_ref[...]).astype(BF16)
    c1 = c0 + Q_LORA_RANK + KV_LORA_RANK
    spread = jnp.dot(cs_ref[0], expand_ref[...], preferred_element_type=F32,
                     precision=lax.Precision.HIGHEST)
    lane128 = lax.broadcasted_iota(jnp.int32, (1, LANES), 1)
    in_rope = jnp.logical_and(lane128 >= QK_NOPE_DIM, lane128 < QK_NOPE_DIM + QK_ROPE_DIM)
    cos = jnp.where(in_rope, spread[:, :LANES], 1.0)
    sin = spread[:, LANES:]
    kpe = z[:, c1:c1 + LANES] * cos + z[:, c1 + LANES:c1 + 2 * LANES] * sin
    q = _dot(zqn, wq_ref[...])
    qr = _dot(zqn, wqr_ref[...])
    kn = _dot(zkvn, wk_ref[...])
    vv = _dot(zkvn, wv_ref[...])
    lane = lax.broadcasted_iota(jnp.int32, (1, LANES), 1)
    ones_col = jnp.where(lane == V_DIM_B, 1.0, 0.0).astype(F32)
    scale = (QK_NOPE_DIM + QK_ROPE_DIM) ** -0.5 * LOG2_E
    for h in range(N_HEADS_B):
        sl = slice(h * LANES, (h + 1) * LANES)
        qm_ref[0, h] = ((q[:, sl] * cos + qr[:, sl] * sin) * scale).astype(BF16)
        km_ref[0, h] = (kn[:, sl] + kpe).astype(BF16)
        vm_ref[0, h] = (vv[:, sl] + ones_col).astype(BF16)


def _proj(x, cos_sin, expand, g_mix, w_in_ext, g_q, wq, wqr, g_kv, wk, wv, tm):
    b, s, d = x.shape
    n_in = w_in_ext.shape[1]
    hw = N_HEADS_B * LANES
    full = lambda shape: pl.BlockSpec(shape, lambda bi, i: (0,) * len(shape))
    row = lambda w: pl.BlockSpec((1, tm, w), lambda bi, i: (bi, i, 0))
    head = pl.BlockSpec((1, N_HEADS_B, tm, LANES), lambda bi, i: (bi, 0, i, 0))
    dils = [dil for _, dil in DILATED_PATTERNS]
    view_specs = [pl.BlockSpec((1, tm // dil, dil * WIDTH_A), lambda bi, i: (bi, i, 0))
                  for dil in dils for _ in range(3)]
    view_shapes = [jax.ShapeDtypeStruct((b, s // dil, dil * WIDTH_A), BF16)
                   for dil in dils for _ in range(3)]
    return pl.pallas_call(
        _proj_kernel,
        grid=(b, s // tm),
        in_specs=[row(d), row(cos_sin.shape[2]), full(expand.shape), full((1, d)), full((d, n_in)),
                  full((1, Q_LORA_RANK)), full((Q_LORA_RANK, hw)), full((Q_LORA_RANK, hw)),
                  full((1, KV_LORA_RANK)), full((KV_LORA_RANK, hw)), full((KV_LORA_RANK, hw))],
        out_specs=view_specs + [head, head, head],
        out_shape=view_shapes + [jax.ShapeDtypeStruct((b, N_HEADS_B, s, LANES), BF16)] * 3,
        scratch_shapes=[pltpu.VMEM((3 * WIDTH_A // LANES, tm, LANES), F32)],
        compiler_params=pltpu.CompilerParams(
            dimension_semantics=("parallel", "parallel"), vmem_limit_bytes=VMEM_LIMIT),
        name="proj",
    )(x, cos_sin, expand, g_mix, w_in_ext, g_q, wq, wqr, g_kv, wk, wv)


_LOG_BUCKET_STARTS = tuple(
    next(n for n in range(8, 4096)
         if int(math.log(n / 8) / math.log(MAX_DISTANCE / 8) * 8) >= t)
    for t in range(1, 8))


def _rel_bucket(rel):
    n = jnp.abs(rel)
    large = jnp.full(rel.shape, NUM_BUCKETS // 4, jnp.int32)
    for start in _LOG_BUCKET_STARTS:
        large = large + jnp.where(n >= start, 1, 0)
    mag = jnp.where(n < NUM_BUCKETS // 4, n, large)
    return mag + jnp.where(rel > 0, NUM_BUCKETS // 2, 0)


def _window_start(j, sub_len):
    return jnp.clip(j * Q_BLOCK - HALF_WINDOW, 0, sub_len - K_WINDOW)


def _bias_kernel(off_ref, qpos_ref, kpos_ref, tbl_ref, out_ref):
    rel = kpos_ref[0] - qpos_ref[0]
    bucket = _rel_bucket(rel)
    delta = (off_ref[pl.program_id(0)] + lax.broadcasted_iota(jnp.int32, (1, K_WINDOW), 1)
             - lax.broadcasted_iota(jnp.int32, (Q_BLOCK, 1), 0))
    valid = jnp.abs(delta) <= HALF_WINDOW
    for h in range(N_HEADS_A):
        tbl = jnp.broadcast_to(tbl_ref[h:h + 1, :], (Q_BLOCK, LANES))
        bias = jnp.concatenate(
            [jnp.take_along_axis(tbl, bucket[:, c * LANES:(c + 1) * LANES], axis=1)
             for c in range(K_WINDOW // LANES)], axis=1)
        out_ref[0, h] = jnp.where(valid, bias * LOG2_E, NEG_INF)


def _bias_tiles(offs, qpos, kpos, tbl):
    n = offs.shape[0]
    grid_spec = pltpu.PrefetchScalarGridSpec(
        num_scalar_prefetch=1,
        grid=(n,),
        in_specs=[pl.BlockSpec((1, Q_BLOCK, 1), lambda t, off: (t, 0, 0)),
                  pl.BlockSpec((1, 1, K_WINDOW), lambda t, off: (t, 0, 0)),
                  pl.BlockSpec((N_HEADS_A, LANES), lambda t, off: (0, 0))],
        out_specs=pl.BlockSpec((1, N_HEADS_A, Q_BLOCK, K_WINDOW), lambda t, off: (t, 0, 0, 0)),
    )
    return pl.pallas_call(
        _bias_kernel,
        grid_spec=grid_spec,
        out_shape=jax.ShapeDtypeStruct((n, N_HEADS_A, Q_BLOCK, K_WINDOW), F32),
        compiler_params=pltpu.CompilerParams(
            dimension_semantics=("arbitrary",), vmem_limit_bytes=VMEM_LIMIT),
        name="bias_tiles",
    )(offs, qpos, kpos, tbl)


def _dilated_kernel(q_ref, k_ref, v_ref, *refs, sub_len):
    bias_refs = refs[:Q_BLOCKS_PER_STEP]
    o_ref, lse_ref, s_scr, p_scr = refs[Q_BLOCKS_PER_STEP:]
    first = lax.broadcasted_iota(jnp.int32, (1, LANES), 1) < HEAD_DIM_A
    pair = lambda h: slice((h // 2) * LANES, (h // 2 + 1) * LANES)
    for sub in range(Q_BLOCKS_PER_STEP):
        j = pl.program_id(2) * Q_BLOCKS_PER_STEP + sub
        rows = slice(sub * Q_BLOCK, (sub + 1) * Q_BLOCK)
        kstart = pl.multiple_of(_window_start(j, sub_len), HALF_WINDOW)
        q = q_ref[0, rows, :]
        kw = k_ref[0, pl.ds(kstart, K_WINDOW), :]
        vw = v_ref[0, pl.ds(kstart, K_WINDOW), :]
        for h in range(N_HEADS_A):
            own = first if h % 2 == 0 else jnp.logical_not(first)
            qh = jnp.where(own, q[:, pair(h)], jnp.zeros_like(q[:, pair(h)]))
            s_scr[sub, h] = _dot_nt(qh, kw[:, pair(h)]) + bias_refs[sub][0, h]
        stats = []
        for h in range(N_HEADS_A):
            s = s_scr[sub, h]
            m = jnp.max(s, axis=1, keepdims=True)
            e = jnp.exp2(s - m)
            l = jnp.sum(e, axis=1, keepdims=True)
            p_scr[sub, h] = e.astype(BF16)
            stats.append((1.0 / l, m + jnp.log2(l)))
        for h in range(0, N_HEADS_A, 2):
            (r0, lse0), (r1, lse1) = stats[h], stats[h + 1]
            o0 = _dot(p_scr[sub, h], vw[:, pair(h)])
            o1 = _dot(p_scr[sub, h + 1], vw[:, pair(h)])
            o_ref[0, rows, pair(h)] = jnp.where(first, o0 * r0, o1 * r1)
            lse_ref[0, rows, pair(h)] = jnp.where(first, lse0, lse1)


def _dilated(qv, kv, vv, bias, bias_index, dil):
    b, sub_len, _ = qv.shape
    w = WIDTH_A
    nq = sub_len // Q_BLOCK
    assert sub_len >= K_WINDOW and sub_len % (Q_BLOCK * Q_BLOCKS_PER_STEP) == 0
    qspec = pl.BlockSpec((1, Q_BLOCK * Q_BLOCKS_PER_STEP, w), lambda bi, r, j: (bi, j, r))
    kvspec = pl.BlockSpec((1, sub_len, w), lambda bi, r, j: (bi, 0, r))
    bspecs = [pl.BlockSpec((1, N_HEADS_A, Q_BLOCK, K_WINDOW),
                           lambda bi, r, j, sub=sub:
                           (bias_index(bi, r, j * Q_BLOCKS_PER_STEP + sub, nq), 0, 0, 0))
              for sub in range(Q_BLOCKS_PER_STEP)]
    return pl.pallas_call(
        functools.partial(_dilated_kernel, sub_len=sub_len),
        grid=(b, dil, nq // Q_BLOCKS_PER_STEP),
        in_specs=[qspec, kvspec, kvspec] + bspecs,
        out_specs=[qspec, qspec],
        out_shape=[jax.ShapeDtypeStruct((b, sub_len, dil * w), F32)] * 2,
        scratch_shapes=[pltpu.VMEM((Q_BLOCKS_PER_STEP, N_HEADS_A, Q_BLOCK, K_WINDOW), F32),
                        pltpu.VMEM((Q_BLOCKS_PER_STEP, N_HEADS_A, Q_BLOCK, K_WINDOW), BF16)],
        compiler_params=pltpu.CompilerParams(
            dimension_semantics=("parallel", "parallel", "arbitrary"),
            vmem_limit_bytes=VMEM_LIMIT),
        name=f"dilated{dil}",
    )(qv, kv, vv, *([bias] * Q_BLOCKS_PER_STEP))


def _dilated_all(views, positions, tbl):
    b, s = positions.shape
    dils = [dil for _, dil in DILATED_PATTERNS]
    consecutive = jnp.all(positions[:, 1:] - positions[:, :-1] == 1)

    def run(bias, index_fns):
        outs = []
        for g, dil in enumerate(dils):
            qv, kv, vv = views[3 * g:3 * g + 3]
            outs.extend(_dilated(qv, kv, vv, bias, index_fns[g], dil))
        return tuple(outs)

    def shared_tiles():
        offs, qpos, kpos, fns = [], [], [], []
        for g, dil in enumerate(dils):
            for off in (0, -HALF_WINDOW, -2 * HALF_WINDOW):
                offs.append(off)
                qpos.append(dil * jnp.arange(Q_BLOCK, dtype=jnp.int32))
                kpos.append(dil * (off + jnp.arange(K_WINDOW, dtype=jnp.int32)))
            fns.append(lambda bi, r, j, nq, g=g:
                       3 * g + jnp.where(j == 0, 0, jnp.where(j == nq - 1, 2, 1)))
        bias = _bias_tiles(jnp.array(offs, jnp.int32), jnp.stack(qpos)[:, :, None],
                           jnp.stack(kpos)[:, None, :], tbl)
        return run(bias, fns)

    def per_block_tiles():
        offs, qpos, kpos, fns = [], [], [], []
        base = 0
        for g, dil in enumerate(dils):
            sub_len = s // dil
            nq = sub_len // Q_BLOCK
            pos_t = positions.reshape(b, sub_len, dil).transpose(0, 2, 1).reshape(b * dil, sub_len)
            starts = _window_start(jnp.arange(nq), sub_len)
            win = starts[:, None] + jnp.arange(K_WINDOW)[None, :]
            qpos.append(pos_t.reshape(b * dil * nq, Q_BLOCK))
            kpos.append(pos_t[:, win].reshape(b * dil * nq, K_WINDOW))
            offs.append(jnp.tile(starts - jnp.arange(nq) * Q_BLOCK, b * dil))
            fns.append(lambda bi, r, j, nq, base=base, dil=dil: base + (bi * dil + r) * nq + j)
            base += b * dil * nq
        bias = _bias_tiles(jnp.concatenate(offs).astype(jnp.int32),
                           jnp.concatenate(qpos)[:, :, None], jnp.concatenate(kpos)[:, None, :], tbl)
        return run(bias, fns)

    return lax.cond(consecutive, shared_tiles, per_block_tiles)


def _mla_kernel(q_ref, k_ref, v_ref, o_ref, *, tk):
    q = q_ref[0, 0]
    tq = q.shape[0]
    n_kv = k_ref.shape[2] // tk

    def body(i, carry):
        m, acc = carry
        start = pl.multiple_of(i * tk, tk)
        k = k_ref[0, 0, pl.ds(start, tk), :]
        v = v_ref[0, 0, pl.ds(start, tk), :]
        s = _dot_nt(q, k)
        m_new = jnp.maximum(m, jnp.max(s, axis=1, keepdims=True))
        p = jnp.exp2(s - m_new)
        acc = jnp.exp2(m - m_new) * acc + _dot(p.astype(BF16), v)
        return m_new, acc

    m0 = jnp.full((tq, 1), NEG_INF, F32)
    _, acc = lax.fori_loop(0, n_kv, body, (m0, jnp.zeros((tq, LANES), F32)), unroll=True)
    lane = lax.broadcasted_iota(jnp.int32, (1, LANES), 1)
    denom = jnp.sum(jnp.where(lane == V_DIM_B, acc, 0.0), axis=1, keepdims=True)
    o_ref[0, 0] = jnp.where(lane < V_DIM_B, acc / denom, 0.0).astype(BF16)


def _mla(qm, km, vm, tq, tk):
    b, nh, s, _ = qm.shape
    qspec = pl.BlockSpec((1, 1, tq, LANES), lambda bi, h, i: (bi, h, i, 0))
    kvspec = pl.BlockSpec((1, 1, s, LANES), lambda bi, h, i: (bi, h, 0, 0))
    return pl.pallas_call(
        functools.partial(_mla_kernel, tk=tk),
        grid=(b, nh, s // tq),
        in_specs=[qspec, kvspec, kvspec],
        out_specs=qspec,
        out_shape=jax.ShapeDtypeStruct((b, nh, s, LANES), BF16),
        compiler_params=pltpu.CompilerParams(
            dimension_semantics=("parallel", "parallel", "arbitrary"),
            vmem_limit_bytes=VMEM_LIMIT),
        name="mla",
    )(qm, km, vm)


def _memkv_kernel(mem_ref, g_ref, w_ref, k_ref, v_ref):
    d = mem_ref.shape[2]
    kv = _dot(_rms(mem_ref[0], g_ref[...]).astype(BF16), w_ref[...])
    k_ref[0] = kv[:, :d].astype(BF16)
    v_ref[0] = kv[:, d:].astype(BF16)


def _memkv(mem, g_mem, w_mkv):
    b, m, d = mem.shape
    spec = pl.BlockSpec((1, m, d), lambda bi: (bi, 0, 0))
    return pl.pallas_call(
        _memkv_kernel,
        grid=(b,),
        in_specs=[spec, pl.BlockSpec((1, d), lambda bi: (0, 0)),
                  pl.BlockSpec((d, 2 * d), lambda bi: (0, 0))],
        out_specs=[spec, spec],
        out_shape=[jax.ShapeDtypeStruct((b, m, d), BF16)] * 2,
        compiler_params=pltpu.CompilerParams(
            dimension_semantics=("parallel",), vmem_limit_bytes=VMEM_LIMIT),
        name="memkv",
    )(mem, g_mem, w_mkv)


def _post_kernel(x_ref, o1_ref, o2_ref, o3_ref, l1_ref, l2_ref, l3_ref, ob_ref,
                 goa_ref, gob_ref, woa_ref, wob_ref, gx_ref, wmq_ref, km_ref, vm_ref, wmo_ref,
                 gmoe_ref, wr_ref, br_ref,
                 h_ref, xn_ref, logit_ref, *nat_scr):
    tm = x_ref.shape[1]
    n_chunk = WIDTH_A // LANES

    for view_ref, scr in zip((o2_ref, o3_ref, l2_ref, l3_ref), nat_scr):
        dil = view_ref.shape[2] // WIDTH_A
        for r in range(dil):
            for c in range(n_chunk):
                col = r * WIDTH_A + c * LANES
                scr[c, pl.ds(r, tm // dil, stride=dil), :] = view_ref[0, :, col:col + LANES]

    def natural(scr, rows):
        return jnp.concatenate([scr[c, rows, :] for c in range(n_chunk)], axis=1)

    rows_per_chain = tm // POST_CHAINS
    for ci in range(POST_CHAINS):
        rows = slice(ci * rows_per_chain, (ci + 1) * rows_per_chain)
        o1, o2, o3 = o1_ref[0, rows, :], natural(nat_scr[0], rows), natural(nat_scr[1], rows)
        l1, l2, l3 = l1_ref[0, rows, :], natural(nat_scr[2], rows), natural(nat_scr[3], rows)
        mx = jnp.maximum(jnp.maximum(l1, l2), l3)
        w1, w2, w3 = jnp.exp2(l1 - mx), jnp.exp2(l2 - mx), jnp.exp2(l3 - mx)
        oa = (w1 * o1 + w2 * o2 + w3 * o3) / (w1 + w2 + w3)
        oan = _rms(oa, goa_ref[...]).astype(BF16)
        ob = jnp.concatenate([ob_ref[0, h, rows, :] for h in range(N_HEADS_B)], axis=1).astype(F32)
        ms_b = jnp.sum(ob * ob, axis=1, keepdims=True) * (1.0 / (N_HEADS_B * V_DIM_B))
        obn = (ob * lax.rsqrt(ms_b + EPS) * gob_ref[...]).astype(BF16)
        h1 = x_ref[0, rows, :] + _dot(oan, woa_ref[...]) + _dot(obn, wob_ref[...])

        hn = _rms(h1, gx_ref[...]).astype(BF16)
        q = _dot(hn, wmq_ref[...]).astype(BF16)
        dh = q.shape[1] // N_HEADS_MEM
        heads = []
        for h in range(N_HEADS_MEM):
            sl = slice(h * dh, (h + 1) * dh)
            s = _dot_nt(q[:, sl], km_ref[0, :, sl])
            e = jnp.exp(s - jnp.max(s, axis=1, keepdims=True))
            p = e / jnp.sum(e, axis=1, keepdims=True)
            heads.append(_dot(p.astype(BF16), vm_ref[0, :, sl]))
        o = jnp.concatenate(heads, axis=1).astype(BF16)
        h2 = h1 + _dot(o, wmo_ref[...])
        h_ref[0, rows, :] = h2

        xn = _rms(h2, gmoe_ref[...])
        packed = _pack_bf16_pairs(xn)
        rows_per_token = packed.shape[1] // LANES
        _store_token_rows(xn_ref.at[pl.ds(ci * rows_per_chain * rows_per_token,
                                          rows_per_chain * rows_per_token)], packed)
        logit_ref[0, rows, :] = jnp.dot(xn, wr_ref[...], preferred_element_type=F32,
                                        precision=lax.Precision.HIGHEST) + br_ref[...]


def _post(x, o_pats, lse_pats, ob, g_out_a, g_out_b_pad, w_o_a, w_o_b_pad, g_xattn, w_mq, kmem,
          vmem, w_mo, g_moe, w_router_pad, b_router_pad, tm):
    b, s, d = x.shape
    m = kmem.shape[1]
    full = lambda shape: pl.BlockSpec(shape, lambda bi, i: (0,) * len(shape))
    row = lambda w: pl.BlockSpec((1, tm, w), lambda bi, i: (bi, i, 0))
    memspec = pl.BlockSpec((1, m, d), lambda bi, i: (bi, 0, 0))
    hw = N_HEADS_B * LANES
    views = [pl.BlockSpec((1, tm // dil, dil * WIDTH_A), lambda bi, i: (bi, i, 0))
             for _, dil in DILATED_PATTERNS]
    return pl.pallas_call(
        _post_kernel,
        grid=(b, s // tm),
        in_specs=[row(d)] + views + views
        + [pl.BlockSpec((1, N_HEADS_B, tm, LANES), lambda bi, i: (bi, 0, i, 0)),
           full((1, WIDTH_A)), full((1, hw)), full((WIDTH_A, d)), full((hw, d)),
           full((1, d)), full((d, d)), memspec, memspec, full((d, d)),
           full((1, d)), full((d, LANES)), full((1, LANES))],
        out_specs=[row(d), pl.BlockSpec((tm * d // 2 // LANES, LANES), lambda bi, i: (bi * (s // tm) + i, 0)),
                   row(LANES)],
        out_shape=[jax.ShapeDtypeStruct((b, s, d), F32),
                   jax.ShapeDtypeStruct((b * s * d // 2 // LANES, LANES), jnp.uint32),
                   jax.ShapeDtypeStruct((b, s, LANES), F32)],
        scratch_shapes=[pltpu.VMEM((WIDTH_A // LANES, tm, LANES), F32)] * 4,
        compiler_params=pltpu.CompilerParams(
            dimension_semantics=("parallel", "parallel"), vmem_limit_bytes=VMEM_LIMIT),
        name="post",
    )(x, *o_pats, *lse_pats, ob, g_out_a, g_out_b_pad, w_o_a, w_o_b_pad, g_xattn, w_mq, kmem,
      vmem, w_mo, g_moe, w_router_pad, b_router_pad)


def _route_kernel(logit_ref, gate_ref, eidx_ref, rank_ref, count_ref, carry_ref):
    @pl.when(pl.program_id(0) == 0)
    def _():
        carry_ref[...] = jnp.zeros_like(carry_ref)

    l = logit_ref[...]
    tr = l.shape[0]
    lane_i = lax.broadcasted_iota(jnp.int32, l.shape, 1)
    lane = lane_i.astype(F32)
    vals, idxs = [], []
    for _ in range(TOP_K):
        m = jnp.max(l, axis=1, keepdims=True)
        idx = jnp.min(jnp.where(l == m, lane, float(LANES)), axis=1, keepdims=True)
        vals.append(m)
        idxs.append(idx)
        l = jnp.where(lane == idx, -jnp.inf, l)
    exps = [jnp.exp(v - vals[0]) for v in vals]
    denom = exps[0] + exps[1] + exps[2] + exps[3]
    onehot = jnp.zeros(l.shape, F32)
    for idx in idxs:
        onehot = onehot + jnp.where(lane == idx, 1.0, 0.0)
    r = lax.broadcasted_iota(jnp.int32, (tr, tr), 0)
    c = lax.broadcasted_iota(jnp.int32, (tr, tr), 1)
    tri = jnp.where(c < r, 1.0, 0.0).astype(BF16)
    before = _dot(tri, onehot.astype(BF16)) + carry_ref[...]
    gate = jnp.zeros(l.shape, F32)
    eidx = jnp.zeros(l.shape, jnp.int32)
    rank = jnp.zeros(l.shape, jnp.int32)
    for k in range(TOP_K):
        rk = jnp.sum(jnp.where(lane == idxs[k], before, 0.0), axis=1, keepdims=True)
        gate = jnp.where(lane_i == k, exps[k] / denom, gate)
        eidx = jnp.where(lane_i == k, idxs[k].astype(jnp.int32), eidx)
        rank = jnp.where(lane_i == k, rk.astype(jnp.int32), rank)
    gate_ref[...] = gate
    eidx_ref[...] = eidx
    rank_ref[...] = rank
    total = carry_ref[...] + jnp.sum(onehot, axis=0, keepdims=True)
    carry_ref[...] = total
    count_ref[...] = total.astype(jnp.int32)


def _route(logits, tr):
    t = logits.shape[0]
    spec = pl.BlockSpec((tr, LANES), lambda i: (i, 0))
    return pl.pallas_call(
        _route_kernel,
        grid=(t // tr,),
        in_specs=[spec],
        out_specs=[spec, spec, spec, pl.BlockSpec((1, LANES), lambda i: (0, 0))],
        out_shape=[jax.ShapeDtypeStruct((t, LANES), F32), jax.ShapeDtypeStruct((t, LANES), jnp.int32),
                   jax.ShapeDtypeStruct((t, LANES), jnp.int32),
                   jax.ShapeDtypeStruct((1, LANES), jnp.int32)],
        scratch_shapes=[pltpu.VMEM((1, LANES), F32)],
        compiler_params=pltpu.CompilerParams(
            dimension_semantics=("arbitrary",), vmem_limit_bytes=VMEM_LIMIT),
        name="route",
    )(logits)


def _dispatch_kernel(pos_ref, ends_ref, x_ref, xs_hbm, zbuf, sem, zsem):
    c = pl.program_id(0)
    n_chunk = x_ref.shape[0] // DISPATCH_CHUNK
    blk_rows = zbuf.shape[0]

    def token(ref, t):
        return ref.at[pl.ds(pl.multiple_of(t * n_chunk, n_chunk), n_chunk)]

    @pl.when(c == 0)
    def _():
        zbuf[...] = jnp.zeros_like(zbuf)

        def zero_copy(start):
            return pltpu.make_async_copy(
                zbuf, xs_hbm.at[pl.ds(pl.multiple_of(start * n_chunk, n_chunk), blk_rows)], zsem)

        blk = blk_rows // n_chunk
        used_end = ends_ref[N_EXPERTS]
        starts = [(ends_ref[e + 1] > ends_ref[e], ends_ref[e + 1] - blk) for e in range(N_EXPERTS)]
        starts += [((used_end + (j + 1) * blk) * n_chunk <= xs_hbm.shape[0], used_end + j * blk)
                   for j in range(N_EXPERTS)]
        for cond, start in starts:
            @pl.when(cond)
            def _():
                zero_copy(start).start()
        for cond, start in starts:
            @pl.when(cond)
            def _():
                zero_copy(start).wait()

    def body(t, carry):
        tok = c * DISPATCH_CHUNK + t
        for k in range(TOP_K):
            pltpu.make_async_copy(token(x_ref, t), token(xs_hbm, pos_ref[tok * TOP_K + k]), sem).start()
        return carry
    lax.fori_loop(0, DISPATCH_CHUNK, body, 0, unroll=4)

    for k in range(TOP_K):
        pltpu.make_async_copy(x_ref, xs_hbm.at[pl.ds(0, x_ref.shape[0])], sem).wait()


def _dispatch(pos, ends0, xn_rows, n_tok, n_slots):
    n_chunk = xn_rows.shape[0] // n_tok
    grid_spec = pltpu.PrefetchScalarGridSpec(
        num_scalar_prefetch=2,
        grid=(n_tok // DISPATCH_CHUNK,),
        in_specs=[pl.BlockSpec((DISPATCH_CHUNK * n_chunk, LANES), lambda c, pos, ends: (c, 0))],
        out_specs=pl.BlockSpec(memory_space=pl.ANY),
        scratch_shapes=[pltpu.VMEM((EXPERT_BLOCK * n_chunk, LANES), xn_rows.dtype),
                        pltpu.SemaphoreType.DMA(()), pltpu.SemaphoreType.DMA(())],
    )
    return pl.pallas_call(
        _dispatch_kernel,
        grid_spec=grid_spec,
        out_shape=jax.ShapeDtypeStruct((n_slots * n_chunk, LANES), xn_rows.dtype),
        compiler_params=pltpu.CompilerParams(
            dimension_semantics=("arbitrary",), vmem_limit_bytes=VMEM_LIMIT),
        name="dispatch",
    )(pos, ends0, xn_rows)


def _expert_kernel(be_ref, nused_ref, xs_ref, wgu_ref, bgu_ref, wd_ref, bd_ref, y_ref, wgu_bf, wd_bf):
    i = pl.program_id(0)

    @pl.when(i < nused_ref[0])
    def _():
        @pl.when(jnp.logical_or(i == 0, be_ref[i] != be_ref[jnp.maximum(i - 1, 0)]))
        def _():
            wgu_bf[...] = wgu_ref[0].astype(BF16)
            wd_bf[...] = wd_ref[0].astype(BF16)

        x_lo, x_hi = _unpack_bf16_pairs(_load_token_rows(xs_ref, EXPERT_BLOCK))
        x = jnp.concatenate([x_lo.astype(BF16), x_hi.astype(BF16)], axis=1)
        gu = _dot(x, wgu_bf[...]) + bgu_ref[0]
        de = gu.shape[1] // 2
        gate = jnp.minimum(gu[:, :de], SWIGLU_LIMIT)
        up = jnp.clip(gu[:, de:], -SWIGLU_LIMIT, SWIGLU_LIMIT)
        hmid = (up + 1.0) * (gate * jax.nn.sigmoid(SWIGLU_ALPHA * gate))
        y = _dot(hmid.astype(BF16), wd_bf[...]) + bd_ref[0]
        _store_token_rows(y_ref, _pack_bf16_pairs(y))

    @pl.when(i >= nused_ref[0])
    def _():
        y_ref[...] = jnp.zeros_like(y_ref)


def _experts(block_expert, n_used, xs_rows, w_gate_up, b_gate_up, w_down, b_down):
    d, de2 = w_gate_up.shape[1:]
    n_chunk = d // 2 // LANES
    blk = EXPERT_BLOCK
    n_blk = xs_rows.shape[0] // (blk * n_chunk)
    grid_spec = pltpu.PrefetchScalarGridSpec(
        num_scalar_prefetch=2,
        grid=(n_blk,),
        in_specs=[pl.BlockSpec((blk * n_chunk, LANES),
                               lambda i, be, nu: (jnp.minimum(i, nu[0] - 1), 0)),
                  pl.BlockSpec((1, d, de2), lambda i, be, nu: (be[i], 0, 0)),
                  pl.BlockSpec((1, 1, de2), lambda i, be, nu: (be[i], 0, 0)),
                  pl.BlockSpec((1, de2 // 2, d), lambda i, be, nu: (be[i], 0, 0)),
                  pl.BlockSpec((1, 1, d), lambda i, be, nu: (be[i], 0, 0))],
        out_specs=pl.BlockSpec((blk * n_chunk, LANES), lambda i, be, nu: (i, 0)),
        scratch_shapes=[pltpu.VMEM((d, de2), BF16), pltpu.VMEM((de2 // 2, d), BF16)],
    )
    return pl.pallas_call(
        _expert_kernel,
        grid_spec=grid_spec,
        out_shape=jax.ShapeDtypeStruct(xs_rows.shape, xs_rows.dtype),
        compiler_params=pltpu.CompilerParams(
            dimension_semantics=("arbitrary",), vmem_limit_bytes=VMEM_LIMIT),
        name="experts",
    )(block_expert, n_used, xs_rows, w_gate_up, b_gate_up.reshape(N_EXPERTS, 1, de2),
      w_down, b_down.reshape(N_EXPERTS, 1, d))


def _combine_kernel(pos_ref, h_ref, gate_ref, gfin_ref, y_hbm, out_ref, buf, sems):
    i = pl.program_id(0)
    n = pl.num_programs(0)
    slot = lax.rem(i, 2)
    tc = h_ref.shape[0]
    n_chunk = buf.shape[2] // tc

    def token(ref, t):
        return ref.at[pl.ds(pl.multiple_of(t * n_chunk, n_chunk), n_chunk)]

    def issue(tile, s):
        def body(r, carry):
            for k in range(TOP_K):
                row = pos_ref[(tile * tc + r) * TOP_K + k]
                pltpu.make_async_copy(token(y_hbm, row), token(buf.at[s, k], r), sems.at[s]).start()
            return carry
        lax.fori_loop(0, tc, body, 0, unroll=4)

    @pl.when(i == 0)
    def _():
        issue(0, 0)

    @pl.when(i + 1 < n)
    def _():
        issue(i + 1, 1 - slot)

    for k in range(TOP_K):
        pltpu.make_async_copy(y_hbm.at[pl.ds(0, tc * n_chunk)], buf.at[slot, k], sems.at[slot]).wait()
    gate = gate_ref[...]
    half = h_ref.shape[1] // 2
    lo_pieces, hi_pieces = [], []
    for c in range(n_chunk):
        y_lo = h_ref[:, c * LANES:(c + 1) * LANES]
        y_hi = h_ref[:, half + c * LANES:half + (c + 1) * LANES]
        for k in range(TOP_K):
            lo, hi = _unpack_bf16_pairs(buf[slot, k, pl.ds(c, tc, stride=n_chunk), :])
            y_lo = y_lo + gate[:, k:k + 1] * lo
            y_hi = y_hi + gate[:, k:k + 1] * hi
        lo_pieces.append(y_lo)
        hi_pieces.append(y_hi)
    out_ref[...] = _rms(jnp.concatenate(lo_pieces + hi_pieces, axis=1), gfin_ref[...])


def _combine(pos, h2, gates, g_final, y_rows, tc):
    t, d = h2.shape
    n_chunk = d // 2 // LANES
    grid_spec = pltpu.PrefetchScalarGridSpec(
        num_scalar_prefetch=1,
        grid=(t // tc,),
        in_specs=[pl.BlockSpec((tc, d), lambda i, pos: (i, 0)),
                  pl.BlockSpec((tc, LANES), lambda i, pos: (i, 0)),
                  pl.BlockSpec((1, d), lambda i, pos: (0, 0)),
                  pl.BlockSpec(memory_space=pl.ANY)],
        out_specs=pl.BlockSpec((tc, d), lambda i, pos: (i, 0)),
        scratch_shapes=[pltpu.VMEM((2, TOP_K, tc * n_chunk, LANES), y_rows.dtype),
                        pltpu.SemaphoreType.DMA((2,))],
    )
    return pl.pallas_call(
        _combine_kernel,
        grid_spec=grid_spec,
        out_shape=jax.ShapeDtypeStruct((t, d), F32),
        compiler_params=pltpu.CompilerParams(
            dimension_semantics=("arbitrary",), vmem_limit_bytes=VMEM_LIMIT),
        name="combine",
    )(pos, h2, gates, g_final, y_rows)


def _pad_heads(w, n_heads, width, offset=0):
    k = w.shape[0]
    w = w.reshape(k, n_heads, width)
    w = jnp.pad(w, ((0, 0), (0, 0), (offset, LANES - width - offset)))
    return w.reshape(k, n_heads * LANES)


def _rot_cols(w):
    half = w.shape[-1] // 2
    return jnp.concatenate([-w[..., half:], w[..., :half]], axis=-1)


def kernel(x, mem, positions, g_mix, w_in, g_q_a, w_q_b, g_kv_a, w_kv_b, rel_bias, g_out_a, g_out_b,
           w_o, g_xattn, g_mem, w_mq, w_mkv, w_mo, g_moe, w_router, b_router, w_gate_up, b_gate_up,
           w_down, b_down, g_final):
    b, s, d = x.shape
    t = b * s
    assert g_mix.shape[0] == 1, "single-layer block: the final norm is fused into the last stage"

    inv_freq = ROPE_THETA ** (-jnp.arange(0, QK_ROPE_DIM, 2, dtype=F32) / QK_ROPE_DIM)
    ang = positions.astype(F32)[..., None] * inv_freq
    cos_sin = jnp.concatenate([jnp.cos(ang), jnp.sin(ang)], axis=-1)
    half = QK_ROPE_DIM // 2
    src = jnp.arange(QK_ROPE_DIM)[:, None]
    lane = jnp.arange(LANES)[None, :]
    on_rope = (lane >= QK_NOPE_DIM) & (lane < QK_NOPE_DIM + QK_ROPE_DIM)
    place = ((lane - QK_NOPE_DIM) % half == src % half) & on_rope
    expand = jnp.concatenate([place & (src < half), place & (src >= half)], axis=1).astype(F32)
    tbl = jnp.pad(rel_bias.T.astype(F32), ((0, 0), (0, LANES - NUM_BUCKETS)))

    h = x
    for l in range(1):
        c0 = 3 * WIDTH_A + Q_LORA_RANK + KV_LORA_RANK
        w_kpe = w_in[l][:, c0:c0 + QK_ROPE_DIM]
        place = lambda w: jnp.pad(w, ((0, 0), (QK_NOPE_DIM, LANES - QK_NOPE_DIM - QK_ROPE_DIM)))
        w_in_ext = jnp.concatenate([w_in[l][:, :c0], place(w_kpe), place(_rot_cols(w_kpe))], 1).astype(BF16)
        dq = QK_NOPE_DIM + QK_ROPE_DIM
        wq3 = w_q_b[l].reshape(Q_LORA_RANK, N_HEADS_B, dq)
        wq = _pad_heads(wq3.reshape(Q_LORA_RANK, -1), N_HEADS_B, dq).astype(BF16)
        wq_rot3 = jnp.concatenate([jnp.zeros_like(wq3[..., :QK_NOPE_DIM]), _rot_cols(wq3[..., QK_NOPE_DIM:])], -1)
        wqr = _pad_heads(wq_rot3.reshape(Q_LORA_RANK, -1), N_HEADS_B, dq).astype(BF16)
        wkv3 = w_kv_b[l].reshape(KV_LORA_RANK, N_HEADS_B, QK_NOPE_DIM + V_DIM_B)
        wk = _pad_heads(wkv3[..., :QK_NOPE_DIM].reshape(KV_LORA_RANK, -1), N_HEADS_B, QK_NOPE_DIM).astype(BF16)
        wv = _pad_heads(wkv3[..., QK_NOPE_DIM:].reshape(KV_LORA_RANK, -1), N_HEADS_B, V_DIM_B).astype(BF16)

        *views, qm, km, vm = _proj(h, cos_sin, expand, g_mix[l][None], w_in_ext, g_q_a[l][None], wq, wqr,
                                   g_kv_a[l][None], wk, wv, tm=256)
        pats = _dilated_all(views, positions, tbl)
        ob = _mla(qm, km, vm, tq=1024, tk=512)
        kmem, vmem = _memkv(mem, g_mem[l][None], w_mkv[l].astype(BF16))

        g_out_b_pad = _pad_heads(g_out_b[l][None], N_HEADS_B, V_DIM_B)
        w_o_b_pad = _pad_heads(w_o[l][WIDTH_A:].T, N_HEADS_B, V_DIM_B).T.astype(BF16)
        w_mq_s = (w_mq[l] * ((d // N_HEADS_MEM) ** -0.5)).astype(BF16)
        w_router_pad = jnp.pad(w_router[l], ((0, 0), (0, LANES - N_EXPERTS)))
        b_router_pad = jnp.pad(b_router[l][None], ((0, 0), (0, LANES - N_EXPERTS)), constant_values=NEG_INF)
        h2, xn_rows, logits = _post(h, pats[0::2], pats[1::2], ob, g_out_a[l][None],
                               g_out_b_pad, w_o[l][:WIDTH_A].astype(BF16), w_o_b_pad, g_xattn[l][None],
                               w_mq_s, kmem, vmem, w_mo[l].astype(BF16), g_moe[l][None], w_router_pad,
                               b_router_pad, tm=256 * POST_CHAINS)

        gates, eidx, rank, counts = _route(logits.reshape(t, LANES), tr=512)
        counts = counts[0, :N_EXPERTS]
        blk = EXPERT_BLOCK
        padded = ((counts + blk - 1) // blk) * blk
        ends = jnp.cumsum(padded)
        pad_start = ends - padded
        eidx4, rank4 = eidx[:, :TOP_K], rank[:, :TOP_K]
        expert_ids = jnp.arange(N_EXPERTS, dtype=jnp.int32)
        pos = (jnp.sum(jnp.where(eidx4[..., None] == expert_ids, pad_start, 0), axis=-1)
               + rank4).astype(jnp.int32).reshape(-1)
        n_blk = t * TOP_K // blk + N_EXPERTS
        block_expert = jnp.minimum(
            jnp.sum(ends[None, :] <= (jnp.arange(n_blk) * blk)[:, None], axis=1),
            N_EXPERTS - 1).astype(jnp.int32)
        n_used = (ends[-1] // blk).astype(jnp.int32)[None]
        ends0 = jnp.concatenate([jnp.zeros((1,), jnp.int32), ends.astype(jnp.int32)])
        xs_rows = _dispatch(pos, ends0, xn_rows, t, n_blk * blk)
        y_rows = _experts(block_expert, n_used, xs_rows, w_gate_up[l], b_gate_up[l], w_down[l], b_down[l])
        out = _combine(pos, h2.reshape(t, d), gates, g_final[None], y_rows, tc=128)
        h = out.reshape(b, s, d)
    return h
```

```python
import functools
import math

import jax
import jax.numpy as jnp
from jax import lax
from jax.experimental import pallas as pl
from jax.experimental.pallas import tpu as pltpu

F32 = jnp.float32
BF16 = jnp.bfloat16

LANES = 128
EPS = 1e-6
NEG_INF = -1e30
LOG2_E = math.log2(math.e)

N_HEADS_A = 8
HEAD_DIM_A = 64
WIDTH_A = N_HEADS_A * HEAD_DIM_A
DILATED_PATTERNS = ((128, 1), (512, 4), (2048, 16))
N_HEADS_B = 8
QK_NOPE_DIM = 64
QK_ROPE_DIM = 32
V_DIM_B = 64
Q_LORA_RANK = 256
KV_LORA_RANK = 128
ROPE_THETA = 10000.0
NUM_BUCKETS = 32
MAX_DISTANCE = 1024
N_HEADS_MEM = 4
N_EXPERTS = 32
TOP_K = 4
SWIGLU_LIMIT = 7.0
SWIGLU_ALPHA = 1.702

Q_BLOCK = 128
K_WINDOW = 256
HALF_WINDOW = (K_WINDOW - Q_BLOCK) // 2
Q_BLOCKS_PER_STEP = 2
EXPERT_BLOCK = 512
DISPATCH_CHUNK = 512
DISPATCH_PIECE = 64
POST_CHAINS = 2
VMEM_LIMIT = 56 * 1024 * 1024


def _rms(x, g):
    return x * lax.rsqrt(jnp.mean(x * x, axis=-1, keepdims=True) + EPS) * g


def _dot(a, b):
    return jnp.dot(a, b, preferred_element_type=F32)


def _dot_nt(a, b):
    return lax.dot_general(a, b, (((1,), (1,)), ((), ())), preferred_element_type=F32)


def _load_token_rows(ref, n_tok):
    n_chunk = ref.shape[0] // n_tok
    return jnp.concatenate([ref[pl.ds(c, n_tok, stride=n_chunk), :] for c in range(n_chunk)], axis=1)


def _pack_bf16_pairs(x):
    half = x.shape[1] // 2
    bits = lambda v: lax.bitcast_convert_type(v.astype(BF16).astype(F32), jnp.uint32)
    return bits(x[:, half:]) | (bits(x[:, :half]) >> 16)


def _unpack_bf16_pairs(w):
    lo = lax.bitcast_convert_type(w << 16, F32)
    hi = lax.bitcast_convert_type(w & jnp.uint32(0xFFFF0000), F32)
    return lo, hi


def _store_token_rows(ref, value):
    n_tok = value.shape[0]
    n_chunk = value.shape[1] // LANES
    for c in range(n_chunk):
        ref[pl.ds(c, n_tok, stride=n_chunk), :] = value[:, c * LANES:(c + 1) * LANES]


def _proj_kernel(x_ref, cs_ref, expand_ref, gmix_ref, win_ref, gq_ref, wq_ref, wqr_ref,
                 gkv_ref, wk_ref, wv_ref,
                 q1_ref, k1_ref, v1_ref, q4_ref, k4_ref, v4_ref, q16_ref, k16_ref, v16_ref,
                 qm_ref, km_ref, vm_ref, za_scr):
    xn = _rms(x_ref[0], gmix_ref[...])
    z = _dot(xn.astype(BF16), win_ref[...])
    tm = z.shape[0]
    c0 = 3 * WIDTH_A
    per_group = WIDTH_A // LANES
    for c in range(3 * per_group):
        chunk = z[:, c * LANES:(c + 1) * LANES]
        za_scr[c] = chunk * (HEAD_DIM_A ** -0.5 * LOG2_E) if c < per_group else chunk
    for dil, refs in ((1, (q1_ref, k1_ref, v1_ref)), (4, (q4_ref, k4_ref, v4_ref)),
                      (16, (q16_ref, k16_ref, v16_ref))):
        for r in range(dil):
            for c in range(3 * per_group):
                rows = za_scr[c, pl.ds(r, tm // dil, stride=dil), :].astype(BF16)
                col = r * WIDTH_A + (c % per_group) * LANES
                refs[c // per_group][0, :, col:col + LANES] = rows
    zqn = _rms(z[:, c0:c0 + Q_LORA_RANK], gq_ref[...]).astype(BF16)
    zkvn = _rms(z[:, c0 + Q_LORA_RANK:c0 + Q_LORA_RANK + KV_LORA_RANK], gkv_ref[...]).astype(BF16)
    c1 = c0 + Q_LORA_RANK + KV_LORA_RANK
    spread = jnp.dot(cs_ref[0], expand_ref[...], preferred_element_type=F32,
                     precision=lax.Precision.HIGHEST)
    lane128 = lax.broadcasted_iota(jnp.int32, (1, LANES), 1)
    in_rope = jnp.logical_and(lane128 >= QK_NOPE_DIM, lane128 < QK_NOPE_DIM + QK_ROPE_DIM)
    cos = jnp.where(in_rope, spread[:, :LANES], 1.0)
    sin = spread[:, LANES:]
    kpe = z[:, c1:c1 + LANES] * cos + z[:, c1 + LANES:c1 + 2 * LANES] * sin
    q = _dot(zqn, wq_ref[...])
    qr = _dot(zqn, wqr_ref[...])
    kn = _dot(zkvn, wk_ref[...])
    vv = _dot(zkvn, wv_ref[...])
    lane = lax.broadcasted_iota(jnp.int32, (1, LANES), 1)
    ones_col = jnp.where(lane == V_DIM_B, 1.0, 0.0).astype(F32)
    scale = (QK_NOPE_DIM + QK_ROPE_DIM) ** -0.5 * LOG2_E
    for h in range(N_HEADS_B):
        sl = slice(h * LANES, (h + 1) * LANES)
        qm_ref[0, h] = ((q[:, sl] * cos + qr[:, sl] * sin) * scale).astype(BF16)
        km_ref[0, h] = (kn[:, sl] + kpe).astype(BF16)
        vm_ref[0, h] = (vv[:, sl] + ones_col).astype(BF16)


def _proj(x, cos_sin, expand, g_mix, w_in_ext, g_q, wq, wqr, g_kv, wk, wv, tm):
    b, s, d = x.shape
    n_in = w_in_ext.shape[1]
    hw = N_HEADS_B * LANES
    full = lambda shape: pl.BlockSpec(shape, lambda bi, i: (0,) * len(shape))
    row = lambda w: pl.BlockSpec((1, tm, w), lambda bi, i: (bi, i, 0))
    head = pl.BlockSpec((1, N_HEADS_B, tm, LANES), lambda bi, i: (bi, 0, i, 0))
    dils = [dil for _, dil in DILATED_PATTERNS]
    view_specs = [pl.BlockSpec((1, tm // dil, dil * WIDTH_A), lambda bi, i: (bi, i, 0))
                  for dil in dils for _ in range(3)]
    view_shapes = [jax.ShapeDtypeStruct((b, s // dil, dil * WIDTH_A), BF16)
                   for dil in dils for _ in range(3)]
    return pl.pallas_call(
        _proj_kernel,
        grid=(b, s // tm),
        in_specs=[row(d), row(cos_sin.shape[2]), full(expand.shape), full((1, d)), full((d, n_in)),
                  full((1, Q_LORA_RANK)), full((Q_LORA_RANK, hw)), full((Q_LORA_RANK, hw)),
                  full((1, KV_LORA_RANK)), full((KV_LORA_RANK, hw)), full((KV_LORA_RANK, hw))],
        out_specs=view_specs + [head, head, head],
        out_shape=view_shapes + [jax.ShapeDtypeStruct((b, N_HEADS_B, s, LANES), BF16)] * 3,
        scratch_shapes=[pltpu.VMEM((3 * WIDTH_A // LANES, tm, LANES), F32)],
        compiler_params=pltpu.CompilerParams(
            dimension_semantics=("parallel", "parallel"), vmem_limit_bytes=VMEM_LIMIT),
        name="proj",
    )(x, cos_sin, expand, g_mix, w_in_ext, g_q, wq, wqr, g_kv, wk, wv)


_LOG_BUCKET_STARTS = tuple(
    next(n for n in range(8, 4096)
         if int(math.log(n / 8) / math.log(MAX_DISTANCE / 8) * 8) >= t)
    for t in range(1, 8))


def _rel_bucket(rel):
    n = jnp.abs(rel)
    large = jnp.full(rel.shape, NUM_BUCKETS // 4, jnp.int32)
    for start in _LOG_BUCKET_STARTS:
        large = large + jnp.where(n >= start, 1, 0)
    mag = jnp.where(n < NUM_BUCKETS // 4, n, large)
    return mag + jnp.where(rel > 0, NUM_BUCKETS // 2, 0)


def _window_start(j, sub_len):
    return jnp.clip(j * Q_BLOCK - HALF_WINDOW, 0, sub_len - K_WINDOW)


def _bias_kernel(off_ref, qpos_ref, kpos_ref, tbl_ref, out_ref):
    rel = kpos_ref[0] - qpos_ref[0]
    bucket = _rel_bucket(rel)
    delta = (off_ref[pl.program_id(0)] + lax.broadcasted_iota(jnp.int32, (1, K_WINDOW), 1)
             - lax.broadcasted_iota(jnp.int32, (Q_BLOCK, 1), 0))
    valid = jnp.abs(delta) <= HALF_WINDOW
    for h in range(N_HEADS_A):
        tbl = jnp.broadcast_to(tbl_ref[h:h + 1, :], (Q_BLOCK, LANES))
        bias = jnp.concatenate(
            [jnp.take_along_axis(tbl, bucket[:, c * LANES:(c + 1) * LANES], axis=1)
             for c in range(K_WINDOW // LANES)], axis=1)
        out_ref[0, h] = jnp.where(valid, bias * LOG2_E, NEG_INF)


def _bias_tiles(offs, qpos, kpos, tbl):
    n = offs.shape[0]
    grid_spec = pltpu.PrefetchScalarGridSpec(
        num_scalar_prefetch=1,
        grid=(n,),
        in_specs=[pl.BlockSpec((1, Q_BLOCK, 1), lambda t, off: (t, 0, 0)),
                  pl.BlockSpec((1, 1, K_WINDOW), lambda t, off: (t, 0, 0)),
                  pl.BlockSpec((N_HEADS_A, LANES), lambda t, off: (0, 0))],
        out_specs=pl.BlockSpec((1, N_HEADS_A, Q_BLOCK, K_WINDOW), lambda t, off: (t, 0, 0, 0)),
    )
    return pl.pallas_call(
        _bias_kernel,
        grid_spec=grid_spec,
        out_shape=jax.ShapeDtypeStruct((n, N_HEADS_A, Q_BLOCK, K_WINDOW), F32),
        compiler_params=pltpu.CompilerParams(
            dimension_semantics=("arbitrary",), vmem_limit_bytes=VMEM_LIMIT),
        name="bias_tiles",
    )(offs, qpos, kpos, tbl)


def _dilated_kernel(q_ref, k_ref, v_ref, *refs, sub_len):
    bias_refs = refs[:Q_BLOCKS_PER_STEP]
    o_ref, lse_ref, s_scr, p_scr = refs[Q_BLOCKS_PER_STEP:]
    first = lax.broadcasted_iota(jnp.int32, (1, LANES), 1) < HEAD_DIM_A
    pair = lambda h: slice((h // 2) * LANES, (h // 2 + 1) * LANES)
    for sub in range(Q_BLOCKS_PER_STEP):
        j = pl.program_id(2) * Q_BLOCKS_PER_STEP + sub
        rows = slice(sub * Q_BLOCK, (sub + 1) * Q_BLOCK)
        kstart = pl.multiple_of(_window_start(j, sub_len), HALF_WINDOW)
        q = q_ref[0, rows, :]
        kw = k_ref[0, pl.ds(kstart, K_WINDOW), :]
        vw = v_ref[0, pl.ds(kstart, K_WINDOW), :]
        for h in range(N_HEADS_A):
            own = first if h % 2 == 0 else jnp.logical_not(first)
            qh = jnp.where(own, q[:, pair(h)], jnp.zeros_like(q[:, pair(h)]))
            s_scr[sub, h] = _dot_nt(qh, kw[:, pair(h)]) + bias_refs[sub][0, h]
        stats = []
        for h in range(N_HEADS_A):
            s = s_scr[sub, h]
            m = jnp.max(s, axis=1, keepdims=True)
            e = jnp.exp2(s - m)
            l = jnp.sum(e, axis=1, keepdims=True)
            p_scr[sub, h] = e.astype(BF16)
            stats.append((1.0 / l, m + jnp.log2(l)))
        for h in range(0, N_HEADS_A, 2):
            (r0, lse0), (r1, lse1) = stats[h], stats[h + 1]
            o0 = _dot(p_scr[sub, h], vw[:, pair(h)])
            o1 = _dot(p_scr[sub, h + 1], vw[:, pair(h)])
            o_ref[0, rows, pair(h)] = jnp.where(first, o0 * r0, o1 * r1)
            lse_ref[0, rows, pair(h)] = jnp.where(first, lse0, lse1)


def _dilated(qv, kv, vv, bias, bias_index, dil):
    b, sub_len, _ = qv.shape
    w = WIDTH_A
    nq = sub_len // Q_BLOCK
    assert sub_len >= K_WINDOW and sub_len % (Q_BLOCK * Q_BLOCKS_PER_STEP) == 0
    qspec = pl.BlockSpec((1, Q_BLOCK * Q_BLOCKS_PER_STEP, w), lambda bi, r, j: (bi, j, r))
    kvspec = pl.BlockSpec((1, sub_len, w), lambda bi, r, j: (bi, 0, r))
    bspecs = [pl.BlockSpec((1, N_HEADS_A, Q_BLOCK, K_WINDOW),
                           lambda bi, r, j, sub=sub:
                           (bias_index(bi, r, j * Q_BLOCKS_PER_STEP + sub, nq), 0, 0, 0))
              for sub in range(Q_BLOCKS_PER_STEP)]
    return pl.pallas_call(
        functools.partial(_dilated_kernel, sub_len=sub_len),
        grid=(b, dil, nq // Q_BLOCKS_PER_STEP),
        in_specs=[qspec, kvspec, kvspec] + bspecs,
        out_specs=[qspec, qspec],
        out_shape=[jax.ShapeDtypeStruct((b, sub_len, dil * w), F32)] * 2,
        scratch_shapes=[pltpu.VMEM((Q_BLOCKS_PER_STEP, N_HEADS_A, Q_BLOCK, K_WINDOW), F32),
                        pltpu.VMEM((Q_BLOCKS_PER_STEP, N_HEADS_A, Q_BLOCK, K_WINDOW), BF16)],
        compiler_params=pltpu.CompilerParams(
            dimension_semantics=("parallel", "parallel", "arbitrary"),
            vmem_limit_bytes=VMEM_LIMIT),
        name=f"dilated{dil}",
    )(qv, kv, vv, *([bias] * Q_BLOCKS_PER_STEP))


def _dilated_all(views, positions, tbl):
    b, s = positions.shape
    dils = [dil for _, dil in DILATED_PATTERNS]
    consecutive = jnp.all(positions[:, 1:] - positions[:, :-1] == 1)

    def run(bias, index_fns):
        outs = []
        for g, dil in enumerate(dils):
            qv, kv, vv = views[3 * g:3 * g + 3]
            outs.extend(_dilated(qv, kv, vv, bias, index_fns[g], dil))
        return tuple(outs)

    def shared_tiles():
        offs, qpos, kpos, fns = [], [], [], []
        for g, dil in enumerate(dils):
            for off in (0, -HALF_WINDOW, -2 * HALF_WINDOW):
                offs.append(off)
                qpos.append(dil * jnp.arange(Q_BLOCK, dtype=jnp.int32))
                kpos.append(dil * (off + jnp.arange(K_WINDOW, dtype=jnp.int32)))
            fns.append(lambda bi, r, j, nq, g=g:
                       3 * g + jnp.where(j == 0, 0, jnp.where(j == nq - 1, 2, 1)))
        bias = _bias_tiles(jnp.array(offs, jnp.int32), jnp.stack(qpos)[:, :, None],
                           jnp.stack(kpos)[:, None, :], tbl)
        return run(bias, fns)

    def per_block_tiles():
        offs, qpos, kpos, fns = [], [], [], []
        base = 0
        for g, dil in enumerate(dils):
            sub_len = s // dil
            nq = sub_len // Q_BLOCK
            pos_t = positions.reshape(b, sub_len, dil).transpose(0, 2, 1).reshape(b * dil, sub_len)
            starts = _window_start(jnp.arange(nq), sub_len)
            win = starts[:, None] + jnp.arange(K_WINDOW)[None, :]
            qpos.append(pos_t.reshape(b * dil * nq, Q_BLOCK))
            kpos.append(pos_t[:, win].reshape(b * dil * nq, K_WINDOW))
            offs.append(jnp.tile(starts - jnp.arange(nq) * Q_BLOCK, b * dil))
            fns.append(lambda bi, r, j, nq, base=base, dil=dil: base + (bi * dil + r) * nq + j)
            base += b * dil * nq
        bias = _bias_tiles(jnp.concatenate(offs).astype(jnp.int32),
                           jnp.concatenate(qpos)[:, :, None], jnp.concatenate(kpos)[:, None, :], tbl)
        return run(bias, fns)

    return lax.cond(consecutive, shared_tiles, per_block_tiles)


def _mla_kernel(q_ref, k_ref, v_ref, o_ref, *, tk):
    q = q_ref[0, 0]
    tq = q.shape[0]
    n_kv = k_ref.shape[2] // tk

    def body(i, carry):
        m, acc = carry
        start = pl.multiple_of(i * tk, tk)
        k = k_ref[0, 0, pl.ds(start, tk), :]
        v = v_ref[0, 0, pl.ds(start, tk), :]
        s = _dot_nt(q, k)
        m_new = jnp.maximum(m, jnp.max(s, axis=1, keepdims=True))
        p = jnp.exp2(s - m_new)
        acc = jnp.exp2(m - m_new) * acc + _dot(p.astype(BF16), v)
        return m_new, acc

    m0 = jnp.full((tq, 1), NEG_INF, F32)
    _, acc = lax.fori_loop(0, n_kv, body, (m0, jnp.zeros((tq, LANES), F32)), unroll=True)
    lane = lax.broadcasted_iota(jnp.int32, (1, LANES), 1)
    denom = jnp.sum(jnp.where(lane == V_DIM_B, acc, 0.0), axis=1, keepdims=True)
    o_ref[0, 0] = jnp.where(lane < V_DIM_B, acc / denom, 0.0).astype(BF16)


def _mla(qm, km, vm, tq, tk):
    b, nh, s, _ = qm.shape
    qspec = pl.BlockSpec((1, 1, tq, LANES), lambda bi, h, i: (bi, h, i, 0))
    kvspec = pl.BlockSpec((1, 1, s, LANES), lambda bi, h, i: (bi, h, 0, 0))
    return pl.pallas_call(
        functools.partial(_mla_kernel, tk=tk),
        grid=(b, nh, s // tq),
        in_specs=[qspec, kvspec, kvspec],
        out_specs=qspec,
        out_shape=jax.ShapeDtypeStruct((b, nh, s, LANES), BF16),
        compiler_params=pltpu.CompilerParams(
            dimension_semantics=("parallel", "parallel", "arbitrary"),
            vmem_limit_bytes=VMEM_LIMIT),
        name="mla",
    )(qm, km, vm)


def _memkv_kernel(mem_ref, g_ref, w_ref, k_ref, v_ref):
    d = mem_ref.shape[2]
    kv = _dot(_rms(mem_ref[0], g_ref[...]).astype(BF16), w_ref[...])
    k_ref[0] = kv[:, :d].astype(BF16)
    v_ref[0] = kv[:, d:].astype(BF16)


def _memkv(mem, g_mem, w_mkv):
    b, m, d = mem.shape
    spec = pl.BlockSpec((1, m, d), lambda bi: (bi, 0, 0))
    return pl.pallas_call(
        _memkv_kernel,
        grid=(b,),
        in_specs=[spec, pl.BlockSpec((1, d), lambda bi: (0, 0)),
                  pl.BlockSpec((d, 2 * d), lambda bi: (0, 0))],
        out_specs=[spec, spec],
        out_shape=[jax.ShapeDtypeStruct((b, m, d), BF16)] * 2,
        compiler_params=pltpu.CompilerParams(
            dimension_semantics=("parallel",), vmem_limit_bytes=VMEM_LIMIT),
        name="memkv",
    )(mem, g_mem, w_mkv)


def _post_kernel(x_ref, o1_ref, o2_ref, o3_ref, l1_ref, l2_ref, l3_ref, ob_ref,
                 goa_ref, gob_ref, woa_ref, wob_ref, gx_ref, wmq_ref, km_ref, vm_ref, wmo_ref,
                 gmoe_ref, wr_ref, br_ref,
                 h_ref, xn_ref, logit_ref, *nat_scr):
    tm = x_ref.shape[1]
    n_chunk = WIDTH_A // LANES

    for view_ref, scr in zip((o2_ref, o3_ref, l2_ref, l3_ref), nat_scr):
        dil = view_ref.shape[2] // WIDTH_A
        for r in range(dil):
            for c in range(n_chunk):
                col = r * WIDTH_A + c * LANES
                scr[c, pl.ds(r, tm // dil, stride=dil), :] = view_ref[0, :, col:col + LANES]

    def natural(scr, rows):
        return jnp.concatenate([scr[c, rows, :] for c in range(n_chunk)], axis=1)

    rows_per_chain = tm // POST_CHAINS
    for ci in range(POST_CHAINS):
        rows = slice(ci * rows_per_chain, (ci + 1) * rows_per_chain)
        o1, o2, o3 = o1_ref[0, rows, :], natural(nat_scr[0], rows), natural(nat_scr[1], rows)
        l1, l2, l3 = l1_ref[0, rows, :], natural(nat_scr[2], rows), natural(nat_scr[3], rows)
        mx = jnp.maximum(jnp.maximum(l1, l2), l3)
        w1, w2, w3 = jnp.exp2(l1 - mx), jnp.exp2(l2 - mx), jnp.exp2(l3 - mx)
        oa = (w1 * o1 + w2 * o2 + w3 * o3) / (w1 + w2 + w3)
        oan = _rms(oa, goa_ref[...]).astype(BF16)
        ob = jnp.concatenate([ob_ref[0, h, rows, :] for h in range(N_HEADS_B)], axis=1).astype(F32)
        ms_b = jnp.sum(ob * ob, axis=1, keepdims=True) * (1.0 / (N_HEADS_B * V_DIM_B))
        obn = (ob * lax.rsqrt(ms_b + EPS) * gob_ref[...]).astype(BF16)
        h1 = x_ref[0, rows, :] + _dot(oan, woa_ref[...]) + _dot(obn, wob_ref[...])

        hn = _rms(h1, gx_ref[...]).astype(BF16)
        q = _dot(hn, wmq_ref[...]).astype(BF16)
        dh = q.shape[1] // N_HEADS_MEM
        heads = []
        for h in range(N_HEADS_MEM):
            sl = slice(h * dh, (h + 1) * dh)
            s = _dot_nt(q[:, sl], km_ref[0, :, sl])
            e = jnp.exp(s - jnp.max(s, axis=1, keepdims=True))
            p = e / jnp.sum(e, axis=1, keepdims=True)
            heads.append(_dot(p.astype(BF16), vm_ref[0, :, sl]))
        o = jnp.concatenate(heads, axis=1).astype(BF16)
        h2 = h1 + _dot(o, wmo_ref[...])
        h_ref[0, rows, :] = h2

        xn = _rms(h2, gmoe_ref[...])
        packed = _pack_bf16_pairs(xn)
        rows_per_token = packed.shape[1] // LANES
        _store_token_rows(xn_ref.at[pl.ds(ci * rows_per_chain * rows_per_token,
                                          rows_per_chain * rows_per_token)], packed)
        logit_ref[0, rows, :] = jnp.dot(xn, wr_ref[...], preferred_element_type=F32,
                                        precision=lax.Precision.HIGHEST) + br_ref[...]


def _post(x, o_pats, lse_pats, ob, g_out_a, g_out_b_pad, w_o_a, w_o_b_pad, g_xattn, w_mq, kmem,
          vmem, w_mo, g_moe, w_router_pad, b_router_pad, tm):
    b, s, d = x.shape
    m = kmem.shape[1]
    full = lambda shape: pl.BlockSpec(shape, lambda bi, i: (0,) * len(shape))
    row = lambda w: pl.BlockSpec((1, tm, w), lambda bi, i: (bi, i, 0))
    memspec = pl.BlockSpec((1, m, d), lambda bi, i: (bi, 0, 0))
    hw = N_HEADS_B * LANES
    views = [pl.BlockSpec((1, tm // dil, dil * WIDTH_A), lambda bi, i: (bi, i, 0))
             for _, dil in DILATED_PATTERNS]
    return pl.pallas_call(
        _post_kernel,
        grid=(b, s // tm),
        in_specs=[row(d)] + views + views
        + [pl.BlockSpec((1, N_HEADS_B, tm, LANES), lambda bi, i: (bi, 0, i, 0)),
           full((1, WIDTH_A)), full((1, hw)), full((WIDTH_A, d)), full((hw, d)),
           full((1, d)), full((d, d)), memspec, memspec, full((d, d)),
           full((1, d)), full((d, LANES)), full((1, LANES))],
        out_specs=[row(d), pl.BlockSpec((tm * d // 2 // LANES, LANES), lambda bi, i: (bi * (s // tm) + i, 0)),
                   row(LANES)],
        out_shape=[jax.ShapeDtypeStruct((b, s, d), F32),
                   jax.ShapeDtypeStruct((b * s * d // 2 // LANES, LANES), jnp.uint32),
                   jax.ShapeDtypeStruct((b, s, LANES), F32)],
        scratch_shapes=[pltpu.VMEM((WIDTH_A // LANES, tm, LANES), F32)] * 4,
        compiler_params=pltpu.CompilerParams(
            dimension_semantics=("parallel", "parallel"), vmem_limit_bytes=VMEM_LIMIT),
        name="post",
    )(x, *o_pats, *lse_pats, ob, g_out_a, g_out_b_pad, w_o_a, w_o_b_pad, g_xattn, w_mq, kmem,
      vmem, w_mo, g_moe, w_router_pad, b_router_pad)


def _route_kernel(logit_ref, gate_ref, eidx_ref, rank_ref, slot_ref, count_ref, tcnt_ref, tcarry_ref,
                  carry_ref):
    @pl.when(pl.program_id(0) == 0)
    def _():
        carry_ref[...] = jnp.zeros_like(carry_ref)

    l = logit_ref[...]
    tr = l.shape[0]
    lane_i = lax.broadcasted_iota(jnp.int32, l.shape, 1)
    lane = lane_i.astype(F32)
    vals, idxs = [], []
    for _ in range(TOP_K):
        m = jnp.max(l, axis=1, keepdims=True)
        idx = jnp.min(jnp.where(l == m, lane, float(LANES)), axis=1, keepdims=True)
        vals.append(m)
        idxs.append(idx)
        l = jnp.where(lane == idx, -jnp.inf, l)
    exps = [jnp.exp(v - vals[0]) for v in vals]
    denom = exps[0] + exps[1] + exps[2] + exps[3]
    onehot = jnp.zeros(l.shape, F32)
    for idx in idxs:
        onehot = onehot + jnp.where(lane == idx, 1.0, 0.0)
    r = lax.broadcasted_iota(jnp.int32, (tr, tr), 0)
    c = lax.broadcasted_iota(jnp.int32, (tr, tr), 1)
    tri = jnp.where(c < r, 1.0, 0.0).astype(BF16)
    local = _dot(tri, onehot.astype(BF16))
    before = local + carry_ref[...]
    tile_cnt = jnp.sum(onehot, axis=0, keepdims=True)
    li = lax.broadcasted_iota(jnp.int32, (LANES, LANES), 0)
    lj = lax.broadcasted_iota(jnp.int32, (LANES, LANES), 1)
    prefix = jnp.dot(jnp.broadcast_to(tile_cnt, (8, LANES)), jnp.where(li < lj, 1.0, 0.0),
                     preferred_element_type=F32, precision=lax.Precision.HIGHEST)[0:1]
    grouped = local + prefix
    gate = jnp.zeros(l.shape, F32)
    eidx = jnp.zeros(l.shape, jnp.int32)
    rank = jnp.zeros(l.shape, jnp.int32)
    slot = jnp.zeros(l.shape, jnp.int32)
    for k in range(TOP_K):
        mine = lane == idxs[k]
        rk = jnp.sum(jnp.where(mine, before, 0.0), axis=1, keepdims=True)
        sk = jnp.sum(jnp.where(mine, grouped, 0.0), axis=1, keepdims=True)
        gate = jnp.where(lane_i == k, exps[k] / denom, gate)
        eidx = jnp.where(lane_i == k, idxs[k].astype(jnp.int32), eidx)
        rank = jnp.where(lane_i == k, rk.astype(jnp.int32), rank)
        slot = jnp.where(lane_i == k, sk.astype(jnp.int32), slot)
    gate_ref[...] = gate
    eidx_ref[...] = eidx
    rank_ref[...] = rank
    slot_ref[...] = slot
    tcnt_ref[...] = jnp.broadcast_to(tile_cnt, tcnt_ref.shape).astype(jnp.int32)
    tcarry_ref[...] = jnp.broadcast_to(carry_ref[...], tcarry_ref.shape).astype(jnp.int32)
    total = carry_ref[...] + tile_cnt
    carry_ref[...] = total
    count_ref[...] = total.astype(jnp.int32)


def _route(logits, tr):
    t = logits.shape[0]
    spec = pl.BlockSpec((tr, LANES), lambda i: (i, 0))
    return pl.pallas_call(
        _route_kernel,
        grid=(t // tr,),
        in_specs=[spec],
        out_specs=[spec, spec, spec, spec, pl.BlockSpec((1, LANES), lambda i: (0, 0)),
                   pl.BlockSpec((8, LANES), lambda i: (i, 0)), pl.BlockSpec((8, LANES), lambda i: (i, 0))],
        out_shape=[jax.ShapeDtypeStruct((t, LANES), F32)] + [jax.ShapeDtypeStruct((t, LANES), jnp.int32)] * 3
        + [jax.ShapeDtypeStruct((1, LANES), jnp.int32)]
        + [jax.ShapeDtypeStruct((t // tr * 8, LANES), jnp.int32)] * 2,
        scratch_shapes=[pltpu.VMEM((1, LANES), F32)],
        compiler_params=pltpu.CompilerParams(
            dimension_semantics=("arbitrary",), vmem_limit_bytes=VMEM_LIMIT),
        name="route",
    )(logits)


def _dispatch_kernel(dst_ref, cnt_ref, ends_ref, x_ref, slot_ref, xs_hbm, zbuf, stage, sem, zsem):
    c = pl.program_id(0)
    buf = lax.rem(c, 2)
    n_chunk = x_ref.shape[0] // DISPATCH_CHUNK
    blk_rows = zbuf.shape[0]
    n_staged = DISPATCH_CHUNK * TOP_K

    @pl.when(c == 0)
    def _():
        zbuf[...] = jnp.zeros_like(zbuf)
        for b in range(2):
            stage[b, pl.ds(n_staged * n_chunk, DISPATCH_PIECE * n_chunk), :] = jnp.zeros(
                (DISPATCH_PIECE * n_chunk, LANES), stage.dtype)

        def zero_copy(start):
            return pltpu.make_async_copy(
                zbuf, xs_hbm.at[pl.ds(pl.multiple_of(start * n_chunk, n_chunk), blk_rows)], zsem)

        blk = blk_rows // n_chunk
        used_end = ends_ref[N_EXPERTS]
        starts = [(ends_ref[e + 1] - ends_ref[e] >= back * blk, ends_ref[e + 1] - back * blk)
                  for e in range(N_EXPERTS) for back in (1, 2)]
        starts += [((used_end + (j + 1) * blk) * n_chunk <= xs_hbm.shape[0], used_end + j * blk)
                   for j in range(N_EXPERTS)]
        for cond, start in starts:
            @pl.when(cond)
            def _():
                zero_copy(start).start()
        for cond, start in starts:
            @pl.when(cond)
            def _():
                zero_copy(start).wait()

    x_lo, x_hi = _unpack_bf16_pairs(_load_token_rows(x_ref, DISPATCH_CHUNK))
    x_lo, x_hi = x_lo.astype(BF16), x_hi.astype(BF16)
    slot_t = jnp.transpose(slot_ref[...].astype(F32))
    bits = lambda v: lax.bitcast_convert_type(v, jnp.uint32)
    for part in range(TOP_K):
        j = (lax.broadcasted_iota(jnp.int32, (DISPATCH_CHUNK, 1), 0) + part * DISPATCH_CHUNK).astype(F32)
        pick = jnp.zeros((DISPATCH_CHUNK, DISPATCH_CHUNK), F32)
        for k in range(TOP_K):
            pick = jnp.where(slot_t[k:k + 1, :] == j, 1.0, pick)
        pick = pick.astype(BF16)
        packed = bits(_dot(pick, x_hi)) | (bits(_dot(pick, x_lo)) >> 16)
        _store_token_rows(
            stage.at[buf, pl.ds(part * DISPATCH_CHUNK * n_chunk, DISPATCH_CHUNK * n_chunk)], packed)

    def piece(src_tok, dst_tok):
        rows = DISPATCH_PIECE * n_chunk
        return pltpu.make_async_copy(
            stage.at[buf, pl.ds(pl.multiple_of(src_tok * n_chunk, n_chunk), rows)],
            xs_hbm.at[pl.ds(pl.multiple_of(dst_tok * n_chunk, n_chunk), rows)], sem)

    def drain(step):
        n_pieces = 0
        for e in range(N_EXPERTS):
            n_pieces = n_pieces + (cnt_ref[step * N_EXPERTS + e] + DISPATCH_PIECE - 1) // DISPATCH_PIECE

        def wait_one(i, carry):
            piece(0, 0).wait()
            return carry
        lax.fori_loop(0, n_pieces, wait_one, 0)

    @pl.when(c > 0)
    def _():
        drain(c - 1)

    staged = 0
    for e in range(N_EXPERTS):
        cnt = cnt_ref[c * N_EXPERTS + e]
        dst = dst_ref[c * N_EXPERTS + e]
        for p in range(DISPATCH_CHUNK // DISPATCH_PIECE):
            @pl.when(p * DISPATCH_PIECE < cnt)
            def _():
                piece(staged + p * DISPATCH_PIECE, dst + p * DISPATCH_PIECE).start()
        staged = staged + cnt

    @pl.when(c == pl.num_programs(0) - 1)
    def _():
        drain(c)


def _dispatch(dst, cnt, ends0, xn_rows, slots, n_tok, n_slots):
    n_chunk = xn_rows.shape[0] // n_tok
    staged_rows = (DISPATCH_CHUNK * TOP_K + DISPATCH_PIECE) * n_chunk
    grid_spec = pltpu.PrefetchScalarGridSpec(
        num_scalar_prefetch=3,
        grid=(n_tok // DISPATCH_CHUNK,),
        in_specs=[pl.BlockSpec((DISPATCH_CHUNK * n_chunk, LANES), lambda c, *_: (c, 0)),
                  pl.BlockSpec((DISPATCH_CHUNK, LANES), lambda c, *_: (c, 0))],
        out_specs=pl.BlockSpec(memory_space=pl.ANY),
        scratch_shapes=[pltpu.VMEM((EXPERT_BLOCK * n_chunk, LANES), xn_rows.dtype),
                        pltpu.VMEM((2, staged_rows, LANES), xn_rows.dtype),
                        pltpu.SemaphoreType.DMA(()), pltpu.SemaphoreType.DMA(())],
    )
    return pl.pallas_call(
        _dispatch_kernel,
        grid_spec=grid_spec,
        out_shape=jax.ShapeDtypeStruct((n_slots * n_chunk, LANES), xn_rows.dtype),
        compiler_params=pltpu.CompilerParams(
            dimension_semantics=("arbitrary",), vmem_limit_bytes=VMEM_LIMIT),
        name="dispatch",
    )(dst, cnt, ends0, xn_rows, slots)


def _expert_kernel(be_ref, nused_ref, xs_ref, wgu_ref, bgu_ref, wd_ref, bd_ref, y_ref, wgu_bf, wd_bf):
    i = pl.program_id(0)

    @pl.when(i < nused_ref[0])
    def _():
        @pl.when(jnp.logical_or(i == 0, be_ref[i] != be_ref[jnp.maximum(i - 1, 0)]))
        def _():
            wgu_bf[...] = wgu_ref[0].astype(BF16)
            wd_bf[...] = wd_ref[0].astype(BF16)

        x_lo, x_hi = _unpack_bf16_pairs(_load_token_rows(xs_ref, EXPERT_BLOCK))
        x = jnp.concatenate([x_lo.astype(BF16), x_hi.astype(BF16)], axis=1)
        gu = _dot(x, wgu_bf[...]) + bgu_ref[0]
        de = gu.shape[1] // 2
        gate = jnp.minimum(gu[:, :de], SWIGLU_LIMIT)
        up = jnp.clip(gu[:, de:], -SWIGLU_LIMIT, SWIGLU_LIMIT)
        hmid = (up + 1.0) * (gate * jax.nn.sigmoid(SWIGLU_ALPHA * gate))
        y = _dot(hmid.astype(BF16), wd_bf[...]) + bd_ref[0]
        _store_token_rows(y_ref, _pack_bf16_pairs(y))

    @pl.when(i >= nused_ref[0])
    def _():
        y_ref[...] = jnp.zeros_like(y_ref)


def _experts(block_expert, n_used, xs_rows, w_gate_up, b_gate_up, w_down, b_down):
    d, de2 = w_gate_up.shape[1:]
    n_chunk = d // 2 // LANES
    blk = EXPERT_BLOCK
    n_blk = xs_rows.shape[0] // (blk * n_chunk)
    grid_spec = pltpu.PrefetchScalarGridSpec(
        num_scalar_prefetch=2,
        grid=(n_blk,),
        in_specs=[pl.BlockSpec((blk * n_chunk, LANES),
                               lambda i, be, nu: (jnp.minimum(i, nu[0] - 1), 0)),
                  pl.BlockSpec((1, d, de2), lambda i, be, nu: (be[i], 0, 0)),
                  pl.BlockSpec((1, 1, de2), lambda i, be, nu: (be[i], 0, 0)),
                  pl.BlockSpec((1, de2 // 2, d), lambda i, be, nu: (be[i], 0, 0)),
                  pl.BlockSpec((1, 1, d), lambda i, be, nu: (be[i], 0, 0))],
        out_specs=pl.BlockSpec((blk * n_chunk, LANES), lambda i, be, nu: (i, 0)),
        scratch_shapes=[pltpu.VMEM((d, de2), BF16), pltpu.VMEM((de2 // 2, d), BF16)],
    )
    return pl.pallas_call(
        _expert_kernel,
        grid_spec=grid_spec,
        out_shape=jax.ShapeDtypeStruct(xs_rows.shape, xs_rows.dtype),
        compiler_params=pltpu.CompilerParams(
            dimension_semantics=("arbitrary",), vmem_limit_bytes=VMEM_LIMIT),
        name="experts",
    )(block_expert, n_used, xs_rows, w_gate_up, b_gate_up.reshape(N_EXPERTS, 1, de2),
      w_down, b_down.reshape(N_EXPERTS, 1, d))


def _combine_kernel(pos_ref, h_ref, gate_ref, gfin_ref, y_hbm, out_ref, buf, sems):
    i = pl.program_id(0)
    n = pl.num_programs(0)
    slot = lax.rem(i, 2)
    tc = h_ref.shape[0]
    n_chunk = buf.shape[2] // tc

    def token(ref, t):
        return ref.at[pl.ds(pl.multiple_of(t * n_chunk, n_chunk), n_chunk)]

    def issue(tile, s):
        def body(r, carry):
            for k in range(TOP_K):
                row = pos_ref[(tile * tc + r) * TOP_K + k]
                pltpu.make_async_copy(token(y_hbm, row), token(buf.at[s, k], r), sems.at[s]).start()
            return carry
        lax.fori_loop(0, tc, body, 0, unroll=4)

    @pl.when(i == 0)
    def _():
        issue(0, 0)

    @pl.when(i + 1 < n)
    def _():
        issue(i + 1, 1 - slot)

    for k in range(TOP_K):
        pltpu.make_async_copy(y_hbm.at[pl.ds(0, tc * n_chunk)], buf.at[slot, k], sems.at[slot]).wait()
    gate = gate_ref[...]
    half = h_ref.shape[1] // 2
    lo_pieces, hi_pieces = [], []
    for c in range(n_chunk):
        y_lo = h_ref[:, c * LANES:(c + 1) * LANES]
        y_hi = h_ref[:, half + c * LANES:half + (c + 1) * LANES]
        for k in range(TOP_K):
            lo, hi = _unpack_bf16_pairs(buf[slot, k, pl.ds(c, tc, stride=n_chunk), :])
            y_lo = y_lo + gate[:, k:k + 1] * lo
            y_hi = y_hi + gate[:, k:k + 1] * hi
        lo_pieces.append(y_lo)
        hi_pieces.append(y_hi)
    out_ref[...] = _rms(jnp.concatenate(lo_pieces + hi_pieces, axis=1), gfin_ref[...])


def _combine(pos, h2, gates, g_final, y_rows, tc):
    t, d = h2.shape
    n_chunk = d // 2 // LANES
    grid_spec = pltpu.PrefetchScalarGridSpec(
        num_scalar_prefetch=1,
        grid=(t // tc,),
        in_specs=[pl.BlockSpec((tc, d), lambda i, pos: (i, 0)),
                  pl.BlockSpec((tc, LANES), lambda i, pos: (i, 0)),
                  pl.BlockSpec((1, d), lambda i, pos: (0, 0)),
                  pl.BlockSpec(memory_space=pl.ANY)],
        out_specs=pl.BlockSpec((tc, d), lambda i, pos: (i, 0)),
        scratch_shapes=[pltpu.VMEM((2, TOP_K, tc * n_chunk, LANES), y_rows.dtype),
                        pltpu.SemaphoreType.DMA((2,))],
    )
    return pl.pallas_call(
        _combine_kernel,
        grid_spec=grid_spec,
        out_shape=jax.ShapeDtypeStruct((t, d), F32),
        compiler_params=pltpu.CompilerParams(
            dimension_semantics=("arbitrary",), vmem_limit_bytes=VMEM_LIMIT),
        name="combine",
    )(pos, h2, gates, g_final, y_rows)


def _pad_heads(w, n_heads, width, offset=0):
    k = w.shape[0]
    w = w.reshape(k, n_heads, width)
    w = jnp.pad(w, ((0, 0), (0, 0), (offset, LANES - width - offset)))
    return w.reshape(k, n_heads * LANES)


def _rot_cols(w):
    half = w.shape[-1] // 2
    return jnp.concatenate([-w[..., half:], w[..., :half]], axis=-1)


def kernel(x, mem, positions, g_mix, w_in, g_q_a, w_q_b, g_kv_a, w_kv_b, rel_bias, g_out_a, g_out_b,
           w_o, g_xattn, g_mem, w_mq, w_mkv, w_mo, g_moe, w_router, b_router, w_gate_up, b_gate_up,
           w_down, b_down, g_final):
    b, s, d = x.shape
    t = b * s
    assert g_mix.shape[0] == 1, "single-layer block: the final norm is fused into the last stage"

    inv_freq = ROPE_THETA ** (-jnp.arange(0, QK_ROPE_DIM, 2, dtype=F32) / QK_ROPE_DIM)
    ang = positions.astype(F32)[..., None] * inv_freq
    cos_sin = jnp.concatenate([jnp.cos(ang), jnp.sin(ang)], axis=-1)
    half = QK_ROPE_DIM // 2
    src = jnp.arange(QK_ROPE_DIM)[:, None]
    lane = jnp.arange(LANES)[None, :]
    on_rope = (lane >= QK_NOPE_DIM) & (lane < QK_NOPE_DIM + QK_ROPE_DIM)
    place = ((lane - QK_NOPE_DIM) % half == src % half) & on_rope
    expand = jnp.concatenate([place & (src < half), place & (src >= half)], axis=1).astype(F32)
    tbl = jnp.pad(rel_bias.T.astype(F32), ((0, 0), (0, LANES - NUM_BUCKETS)))

    h = x
    for l in range(1):
        c0 = 3 * WIDTH_A + Q_LORA_RANK + KV_LORA_RANK
        w_kpe = w_in[l][:, c0:c0 + QK_ROPE_DIM]
        place = lambda w: jnp.pad(w, ((0, 0), (QK_NOPE_DIM, LANES - QK_NOPE_DIM - QK_ROPE_DIM)))
        w_in_ext = jnp.concatenate([w_in[l][:, :c0], place(w_kpe), place(_rot_cols(w_kpe))], 1).astype(BF16)
        dq = QK_NOPE_DIM + QK_ROPE_DIM
        wq3 = w_q_b[l].reshape(Q_LORA_RANK, N_HEADS_B, dq)
        wq = _pad_heads(wq3.reshape(Q_LORA_RANK, -1), N_HEADS_B, dq).astype(BF16)
        wq_rot3 = jnp.concatenate([jnp.zeros_like(wq3[..., :QK_NOPE_DIM]), _rot_cols(wq3[..., QK_NOPE_DIM:])], -1)
        wqr = _pad_heads(wq_rot3.reshape(Q_LORA_RANK, -1), N_HEADS_B, dq).astype(BF16)
        wkv3 = w_kv_b[l].reshape(KV_LORA_RANK, N_HEADS_B, QK_NOPE_DIM + V_DIM_B)
        wk = _pad_heads(wkv3[..., :QK_NOPE_DIM].reshape(KV_LORA_RANK, -1), N_HEADS_B, QK_NOPE_DIM).astype(BF16)
        wv = _pad_heads(wkv3[..., QK_NOPE_DIM:].reshape(KV_LORA_RANK, -1), N_HEADS_B, V_DIM_B).astype(BF16)

        *views, qm, km, vm = _proj(h, cos_sin, expand, g_mix[l][None], w_in_ext, g_q_a[l][None], wq, wqr,
                                   g_kv_a[l][None], wk, wv, tm=256)
        pats = _dilated_all(views, positions, tbl)
        ob = _mla(qm, km, vm, tq=1024, tk=512)
        kmem, vmem = _memkv(mem, g_mem[l][None], w_mkv[l].astype(BF16))

        g_out_b_pad = _pad_heads(g_out_b[l][None], N_HEADS_B, V_DIM_B)
        w_o_b_pad = _pad_heads(w_o[l][WIDTH_A:].T, N_HEADS_B, V_DIM_B).T.astype(BF16)
        w_mq_s = (w_mq[l] * ((d // N_HEADS_MEM) ** -0.5)).astype(BF16)
        w_router_pad = jnp.pad(w_router[l], ((0, 0), (0, LANES - N_EXPERTS)))
        b_router_pad = jnp.pad(b_router[l][None], ((0, 0), (0, LANES - N_EXPERTS)), constant_values=NEG_INF)
        h2, xn_rows, logits = _post(h, pats[0::2], pats[1::2], ob, g_out_a[l][None],
                               g_out_b_pad, w_o[l][:WIDTH_A].astype(BF16), w_o_b_pad, g_xattn[l][None],
                               w_mq_s, kmem, vmem, w_mo[l].astype(BF16), g_moe[l][None], w_router_pad,
                               b_router_pad, tm=256 * POST_CHAINS)

        gates, eidx, rank, slots, counts, tile_cnt, tile_carry = _route(
            logits.reshape(t, LANES), tr=DISPATCH_CHUNK)
        counts = counts[0, :N_EXPERTS]
        blk = EXPERT_BLOCK
        padded = jnp.where(counts > 0, (counts + DISPATCH_PIECE + blk - 1) // blk * blk, 0)
        ends = jnp.cumsum(padded)
        pad_start = ends - padded
        eidx4, rank4 = eidx[:, :TOP_K], rank[:, :TOP_K]
        expert_ids = jnp.arange(N_EXPERTS, dtype=jnp.int32)
        pos = (jnp.sum(jnp.where(eidx4[..., None] == expert_ids, pad_start, 0), axis=-1)
               + rank4).astype(jnp.int32).reshape(-1)
        n_blk = t * TOP_K // blk + N_EXPERTS + -(-N_EXPERTS * DISPATCH_PIECE // blk)
        block_expert = jnp.minimum(
            jnp.sum(ends[None, :] <= (jnp.arange(n_blk) * blk)[:, None], axis=1),
            N_EXPERTS - 1).astype(jnp.int32)
        n_used = (ends[-1] // blk).astype(jnp.int32)[None]
        ends0 = jnp.concatenate([jnp.zeros((1,), jnp.int32), ends.astype(jnp.int32)])
        run_start = (pad_start[None, :] + tile_carry[::8, :N_EXPERTS]).astype(jnp.int32).reshape(-1)
        run_len = tile_cnt[::8, :N_EXPERTS].reshape(-1)
        xs_rows = _dispatch(run_start, run_len, ends0, xn_rows, slots, t, n_blk * blk)
        y_rows = _experts(block_expert, n_used, xs_rows, w_gate_up[l], b_gate_up[l], w_down[l], b_down[l])
        out = _combine(pos, h2.reshape(t, d), gates, g_final[None], y_rows, tc=128)
        h = out.reshape(b, s, d)
    return h
```

```python
import functools
import math

import jax
import jax.numpy as jnp
from jax import lax
from jax.experimental import pallas as pl
from jax.experimental.pallas import tpu as pltpu

F32 = jnp.float32
BF16 = jnp.bfloat16

LANES = 128
EPS = 1e-6
NEG_INF = -1e30
LOG2_E = math.log2(math.e)

N_HEADS_A = 8
HEAD_DIM_A = 64
WIDTH_A = N_HEADS_A * HEAD_DIM_A
DILATED_PATTERNS = ((128, 1), (512, 4), (2048, 16))
N_HEADS_B = 8
QK_NOPE_DIM = 64
QK_ROPE_DIM = 32
V_DIM_B = 64
Q_LORA_RANK = 256
KV_LORA_RANK = 128
ROPE_THETA = 10000.0
NUM_BUCKETS = 32
MAX_DISTANCE = 1024
N_HEADS_MEM = 4
N_EXPERTS = 32
TOP_K = 4
SWIGLU_LIMIT = 7.0
SWIGLU_ALPHA = 1.702

Q_BLOCK = 128
K_WINDOW = 256
HALF_WINDOW = (K_WINDOW - Q_BLOCK) // 2
MAX_Q_BLOCKS_PER_STEP = 4
EXPERT_BLOCK = 512
DISPATCH_CHUNK = 512
DISPATCH_PIECE = 64
POST_CHAINS = 2
VMEM_LIMIT = 56 * 1024 * 1024


def _rms(x, g):
    return x * lax.rsqrt(jnp.mean(x * x, axis=-1, keepdims=True) + EPS) * g


def _dot(a, b):
    return jnp.dot(a, b, preferred_element_type=F32)


def _dot_nt(a, b):
    return lax.dot_general(a, b, (((1,), (1,)), ((), ())), preferred_element_type=F32)


def _load_token_rows(ref, n_tok):
    n_chunk = ref.shape[0] // n_tok
    return jnp.concatenate([ref[pl.ds(c, n_tok, stride=n_chunk), :] for c in range(n_chunk)], axis=1)


def _pack_bf16_pairs(x):
    half = x.shape[1] // 2
    bits = lambda v: lax.bitcast_convert_type(v.astype(BF16).astype(F32), jnp.uint32)
    return bits(x[:, half:]) | (bits(x[:, :half]) >> 16)


def _unpack_bf16_pairs(w):
    lo = lax.bitcast_convert_type(w << 16, F32)
    hi = lax.bitcast_convert_type(w & jnp.uint32(0xFFFF0000), F32)
    return lo, hi


def _store_token_rows(ref, value):
    n_tok = value.shape[0]
    n_chunk = value.shape[1] // LANES
    for c in range(n_chunk):
        ref[pl.ds(c, n_tok, stride=n_chunk), :] = value[:, c * LANES:(c + 1) * LANES]


def _proj_kernel(x_ref, cs_ref, expand_ref, gmix_ref, win_ref, gq_ref, wq_ref, wqr_ref,
                 gkv_ref, wk_ref, wv_ref,
                 q1_ref, k1_ref, v1_ref, q4_ref, k4_ref, v4_ref, q16_ref, k16_ref, v16_ref,
                 qm_ref, km_ref, vm_ref, za_scr):
    xn = _rms(x_ref[0], gmix_ref[...])
    z = _dot(xn.astype(BF16), win_ref[...])
    tm = z.shape[0]
    c0 = 3 * WIDTH_A
    per_group = WIDTH_A // LANES
    for c in range(3 * per_group):
        chunk = z[:, c * LANES:(c + 1) * LANES]
        za_scr[c] = chunk * (HEAD_DIM_A ** -0.5 * LOG2_E) if c < per_group else chunk
    for dil, refs in ((1, (q1_ref, k1_ref, v1_ref)), (4, (q4_ref, k4_ref, v4_ref)),
                      (16, (q16_ref, k16_ref, v16_ref))):
        for r in range(dil):
            for c in range(3 * per_group):
                rows = za_scr[c, pl.ds(r, tm // dil, stride=dil), :].astype(BF16)
                col = r * WIDTH_A + (c % per_group) * LANES
                refs[c // per_group][0, :, col:col + LANES] = rows
    zqn = _rms(z[:, c0:c0 + Q_LORA_RANK], gq_ref[...]).astype(BF16)
    zkvn = _rms(z[:, c0 + Q_LORA_RANK:c0 + Q_LORA_RANK + KV_LORA_RANK], gkv_ref[...]).astype(BF16)
    c1 = c0 + Q_LORA_RANK + KV_LORA_RANK
    spread = jnp.dot(cs_ref[0], expand_ref[...], preferred_element_type=F32,
                     precision=lax.Precision.HIGHEST)
    lane128 = lax.broadcasted_iota(jnp.int32, (1, LANES), 1)
    in_rope = jnp.logical_and(lane128 >= QK_NOPE_DIM, lane128 < QK_NOPE_DIM + QK_ROPE_DIM)
    cos = jnp.where(in_rope, spread[:, :LANES], 1.0)
    sin = spread[:, LANES:]
    kpe = z[:, c1:c1 + LANES] * cos + z[:, c1 + LANES:c1 + 2 * LANES] * sin
    q = _dot(zqn, wq_ref[...])
    qr = _dot(zqn, wqr_ref[...])
    kn = _dot(zkvn, wk_ref[...])
    vv = _dot(zkvn, wv_ref[...])
    lane = lax.broadcasted_iota(jnp.int32, (1, LANES), 1)
    ones_col = jnp.where(lane == V_DIM_B, 1.0, 0.0).astype(F32)
    scale = (QK_NOPE_DIM + QK_ROPE_DIM) ** -0.5 * LOG2_E
    for h in range(N_HEADS_B):
        sl = slice(h * LANES, (h + 1) * LANES)
        qm_ref[0, h] = ((q[:, sl] * cos + qr[:, sl] * sin) * scale).astype(BF16)
        km_ref[0, h] = (kn[:, sl] + kpe).astype(BF16)
        vm_ref[0, h] = (vv[:, sl] + ones_col).astype(BF16)


def _proj(x, cos_sin, expand, g_mix, w_in_ext, g_q, wq, wqr, g_kv, wk, wv, tm):
    b, s, d = x.shape
    n_in = w_in_ext.shape[1]
    hw = N_HEADS_B * LANES
    full = lambda shape: pl.BlockSpec(shape, lambda bi, i: (0,) * len(shape))
    row = lambda w: pl.BlockSpec((1, tm, w), lambda bi, i: (bi, i, 0))
    head = pl.BlockSpec((1, N_HEADS_B, tm, LANES), lambda bi, i: (bi, 0, i, 0))
    dils = [dil for _, dil in DILATED_PATTERNS]
    view_specs = [pl.BlockSpec((1, tm // dil, dil * WIDTH_A), lambda bi, i: (bi, i, 0))
                  for dil in dils for _ in range(3)]
    view_shapes = [jax.ShapeDtypeStruct((b, s // dil, dil * WIDTH_A), BF16)
                   for dil in dils for _ in range(3)]
    return pl.pallas_call(
        _proj_kernel,
        grid=(b, s // tm),
        in_specs=[row(d), row(cos_sin.shape[2]), full(expand.shape), full((1, d)), full((d, n_in)),
                  full((1, Q_LORA_RANK)), full((Q_LORA_RANK, hw)), full((Q_LORA_RANK, hw)),
                  full((1, KV_LORA_RANK)), full((KV_LORA_RANK, hw)), full((KV_LORA_RANK, hw))],
        out_specs=view_specs + [head, head, head],
        out_shape=view_shapes + [jax.ShapeDtypeStruct((b, N_HEADS_B, s, LANES), BF16)] * 3,
        scratch_shapes=[pltpu.VMEM((3 * WIDTH_A // LANES, tm, LANES), F32)],
        compiler_params=pltpu.CompilerParams(
            dimension_semantics=("parallel", "parallel"), vmem_limit_bytes=VMEM_LIMIT),
        name="proj",
    )(x, cos_sin, expand, g_mix, w_in_ext, g_q, wq, wqr, g_kv, wk, wv)


_LOG_BUCKET_STARTS = tuple(
    next(n for n in range(8, 4096)
         if int(math.log(n / 8) / math.log(MAX_DISTANCE / 8) * 8) >= t)
    for t in range(1, 8))


def _rel_bucket(rel):
    n = jnp.abs(rel)
    large = jnp.full(rel.shape, NUM_BUCKETS // 4, jnp.int32)
    for start in _LOG_BUCKET_STARTS:
        large = large + jnp.where(n >= start, 1, 0)
    mag = jnp.where(n < NUM_BUCKETS // 4, n, large)
    return mag + jnp.where(rel > 0, NUM_BUCKETS // 2, 0)


def _window_start(j, sub_len):
    return jnp.clip(j * Q_BLOCK - HALF_WINDOW, 0, sub_len - K_WINDOW)


def _bias_kernel(off_ref, qpos_ref, kpos_ref, tbl_ref, out_ref):
    rel = kpos_ref[0] - qpos_ref[0]
    bucket = _rel_bucket(rel)
    delta = (off_ref[pl.program_id(0)] + lax.broadcasted_iota(jnp.int32, (1, K_WINDOW), 1)
             - lax.broadcasted_iota(jnp.int32, (Q_BLOCK, 1), 0))
    valid = jnp.abs(delta) <= HALF_WINDOW
    for h in range(N_HEADS_A):
        tbl = jnp.broadcast_to(tbl_ref[h:h + 1, :], (Q_BLOCK, LANES))
        bias = jnp.concatenate(
            [jnp.take_along_axis(tbl, bucket[:, c * LANES:(c + 1) * LANES], axis=1)
             for c in range(K_WINDOW // LANES)], axis=1)
        out_ref[0, h] = jnp.where(valid, bias * LOG2_E, NEG_INF)


def _bias_tiles(offs, qpos, kpos, tbl):
    n = offs.shape[0]
    grid_spec = pltpu.PrefetchScalarGridSpec(
        num_scalar_prefetch=1,
        grid=(n,),
        in_specs=[pl.BlockSpec((1, Q_BLOCK, 1), lambda t, off: (t, 0, 0)),
                  pl.BlockSpec((1, 1, K_WINDOW), lambda t, off: (t, 0, 0)),
                  pl.BlockSpec((N_HEADS_A, LANES), lambda t, off: (0, 0))],
        out_specs=pl.BlockSpec((1, N_HEADS_A, Q_BLOCK, K_WINDOW), lambda t, off: (t, 0, 0, 0)),
    )
    return pl.pallas_call(
        _bias_kernel,
        grid_spec=grid_spec,
        out_shape=jax.ShapeDtypeStruct((n, N_HEADS_A, Q_BLOCK, K_WINDOW), F32),
        compiler_params=pltpu.CompilerParams(
            dimension_semantics=("arbitrary",), vmem_limit_bytes=VMEM_LIMIT),
        name="bias_tiles",
    )(offs, qpos, kpos, tbl)


def _dilated_kernel(q_ref, k_ref, v_ref, *refs, sub_len, qb):
    bias_refs = refs[:qb]
    o_ref, lse_ref, s_scr, p_scr = refs[qb:]
    first = lax.broadcasted_iota(jnp.int32, (1, LANES), 1) < HEAD_DIM_A
    pair = lambda h: slice((h // 2) * LANES, (h // 2 + 1) * LANES)
    for sub in range(qb):
        j = pl.program_id(2) * qb + sub
        rows = slice(sub * Q_BLOCK, (sub + 1) * Q_BLOCK)
        kstart = pl.multiple_of(_window_start(j, sub_len), HALF_WINDOW)
        q = q_ref[0, rows, :]
        kw = k_ref[0, pl.ds(kstart, K_WINDOW), :]
        vw = v_ref[0, pl.ds(kstart, K_WINDOW), :]
        for h in range(N_HEADS_A):
            own = first if h % 2 == 0 else jnp.logical_not(first)
            qh = jnp.where(own, q[:, pair(h)], jnp.zeros_like(q[:, pair(h)]))
            s_scr[sub, h] = _dot_nt(qh, kw[:, pair(h)]) + bias_refs[sub][0, h]
        stats = []
        for h in range(N_HEADS_A):
            s = s_scr[sub, h]
            m = jnp.max(s, axis=1, keepdims=True)
            e = jnp.exp2(s - m)
            l = jnp.sum(e, axis=1, keepdims=True)
            p_scr[sub, h] = e.astype(BF16)
            stats.append((1.0 / l, m + jnp.log2(l)))
        for h in range(0, N_HEADS_A, 2):
            (r0, lse0), (r1, lse1) = stats[h], stats[h + 1]
            o0 = _dot(p_scr[sub, h], vw[:, pair(h)])
            o1 = _dot(p_scr[sub, h + 1], vw[:, pair(h)])
            o_ref[0, rows, pair(h)] = jnp.where(first, o0 * r0, o1 * r1)
            lse_ref[0, rows, pair(h)] = jnp.where(first, lse0, lse1)


def _dilated(qv, kv, vv, bias, bias_index, dil):
    b, sub_len, _ = qv.shape
    w = WIDTH_A
    nq = sub_len // Q_BLOCK
    qb = min(MAX_Q_BLOCKS_PER_STEP, nq)
    assert sub_len >= K_WINDOW and sub_len % (Q_BLOCK * qb) == 0
    qspec = pl.BlockSpec((1, Q_BLOCK * qb, w), lambda bi, r, j: (bi, j, r))
    kvspec = pl.BlockSpec((1, sub_len, w), lambda bi, r, j: (bi, 0, r))
    bspecs = [pl.BlockSpec((1, N_HEADS_A, Q_BLOCK, K_WINDOW),
                           lambda bi, r, j, sub=sub:
                           (bias_index(bi, r, j * qb + sub, nq), 0, 0, 0))
              for sub in range(qb)]
    return pl.pallas_call(
        functools.partial(_dilated_kernel, sub_len=sub_len, qb=qb),
        grid=(b, dil, nq // qb),
        in_specs=[qspec, kvspec, kvspec] + bspecs,
        out_specs=[qspec, qspec],
        out_shape=[jax.ShapeDtypeStruct((b, sub_len, dil * w), F32)] * 2,
        scratch_shapes=[pltpu.VMEM((qb, N_HEADS_A, Q_BLOCK, K_WINDOW), F32),
                        pltpu.VMEM((qb, N_HEADS_A, Q_BLOCK, K_WINDOW), BF16)],
        compiler_params=pltpu.CompilerParams(
            dimension_semantics=("parallel", "parallel", "arbitrary"),
            vmem_limit_bytes=VMEM_LIMIT),
        name=f"dilated{dil}",
    )(qv, kv, vv, *([bias] * qb))


def _dilated_all(views, positions, tbl):
    b, s = positions.shape
    dils = [dil for _, dil in DILATED_PATTERNS]
    consecutive = jnp.all(positions[:, 1:] - positions[:, :-1] == 1)

    def run(bias, index_fns):
        outs = []
        for g, dil in enumerate(dils):
            qv, kv, vv = views[3 * g:3 * g + 3]
            outs.extend(_dilated(qv, kv, vv, bias, index_fns[g], dil))
        return tuple(outs)

    def shared_tiles():
        offs, qpos, kpos, fns = [], [], [], []
        for g, dil in enumerate(dils):
            for off in (0, -HALF_WINDOW, -2 * HALF_WINDOW):
                offs.append(off)
                qpos.append(dil * jnp.arange(Q_BLOCK, dtype=jnp.int32))
                kpos.append(dil * (off + jnp.arange(K_WINDOW, dtype=jnp.int32)))
            fns.append(lambda bi, r, j, nq, g=g:
                       3 * g + jnp.where(j == 0, 0, jnp.where(j == nq - 1, 2, 1)))
        bias = _bias_tiles(jnp.array(offs, jnp.int32), jnp.stack(qpos)[:, :, None],
                           jnp.stack(kpos)[:, None, :], tbl)
        return run(bias, fns)

    def per_block_tiles():
        offs, qpos, kpos, fns = [], [], [], []
        base = 0
        for g, dil in enumerate(dils):
            sub_len = s // dil
            nq = sub_len // Q_BLOCK
            pos_t = positions.reshape(b, sub_len, dil).transpose(0, 2, 1).reshape(b * dil, sub_len)
            starts = _window_start(jnp.arange(nq), sub_len)
            win = starts[:, None] + jnp.arange(K_WINDOW)[None, :]
            qpos.append(pos_t.reshape(b * dil * nq, Q_BLOCK))
            kpos.append(pos_t[:, win].reshape(b * dil * nq, K_WINDOW))
            offs.append(jnp.tile(starts - jnp.arange(nq) * Q_BLOCK, b * dil))
            fns.append(lambda bi, r, j, nq, base=base, dil=dil: base + (bi * dil + r) * nq + j)
            base += b * dil * nq
        bias = _bias_tiles(jnp.concatenate(offs).astype(jnp.int32),
                           jnp.concatenate(qpos)[:, :, None], jnp.concatenate(kpos)[:, None, :], tbl)
        return run(bias, fns)

    return lax.cond(consecutive, shared_tiles, per_block_tiles)


def _mla_kernel(q_ref, k_ref, v_ref, o_ref, *, tk):
    q = q_ref[0, 0]
    tq = q.shape[0]
    n_kv = k_ref.shape[2] // tk

    def body(i, carry):
        m, acc = carry
        start = pl.multiple_of(i * tk, tk)
        k = k_ref[0, 0, pl.ds(start, tk), :]
        v = v_ref[0, 0, pl.ds(start, tk), :]
        s = _dot_nt(q, k)
        m_new = jnp.maximum(m, jnp.max(s, axis=1, keepdims=True))
        p = jnp.exp2(s - m_new)
        acc = jnp.exp2(m - m_new) * acc + _dot(p.astype(BF16), v)
        return m_new, acc

    m0 = jnp.full((tq, 1), NEG_INF, F32)
    _, acc = lax.fori_loop(0, n_kv, body, (m0, jnp.zeros((tq, LANES), F32)), unroll=True)
    lane = lax.broadcasted_iota(jnp.int32, (1, LANES), 1)
    denom = jnp.sum(jnp.where(lane == V_DIM_B, acc, 0.0), axis=1, keepdims=True)
    o_ref[0, 0] = jnp.where(lane < V_DIM_B, acc / denom, 0.0).astype(BF16)


def _mla(qm, km, vm, tq, tk):
    b, nh, s, _ = qm.shape
    qspec = pl.BlockSpec((1, 1, tq, LANES), lambda bi, h, i: (bi, h, i, 0))
    kvspec = pl.BlockSpec((1, 1, s, LANES), lambda bi, h, i: (bi, h, 0, 0))
    return pl.pallas_call(
        functools.partial(_mla_kernel, tk=tk),
        grid=(b, nh, s // tq),
        in_specs=[qspec, kvspec, kvspec],
        out_specs=qspec,
        out_shape=jax.ShapeDtypeStruct((b, nh, s, LANES), BF16),
        compiler_params=pltpu.CompilerParams(
            dimension_semantics=("parallel", "parallel", "arbitrary"),
            vmem_limit_bytes=VMEM_LIMIT),
        name="mla",
    )(qm, km, vm)


def _memkv_kernel(mem_ref, g_ref, w_ref, k_ref, v_ref):
    d = mem_ref.shape[2]
    kv = _dot(_rms(mem_ref[0], g_ref[...]).astype(BF16), w_ref[...])
    k_ref[0] = kv[:, :d].astype(BF16)
    v_ref[0] = kv[:, d:].astype(BF16)


def _memkv(mem, g_mem, w_mkv):
    b, m, d = mem.shape
    spec = pl.BlockSpec((1, m, d), lambda bi: (bi, 0, 0))
    return pl.pallas_call(
        _memkv_kernel,
        grid=(b,),
        in_specs=[spec, pl.BlockSpec((1, d), lambda bi: (0, 0)),
                  pl.BlockSpec((d, 2 * d), lambda bi: (0, 0))],
        out_specs=[spec, spec],
        out_shape=[jax.ShapeDtypeStruct((b, m, d), BF16)] * 2,
        compiler_params=pltpu.CompilerParams(
            dimension_semantics=("parallel",), vmem_limit_bytes=VMEM_LIMIT),
        name="memkv",
    )(mem, g_mem, w_mkv)


def _post_kernel(x_ref, o1_ref, o2_ref, o3_ref, l1_ref, l2_ref, l3_ref, ob_ref,
                 goa_ref, gob_ref, woa_ref, wob_ref, gx_ref, wmq_ref, km_ref, vm_ref, wmo_ref,
                 gmoe_ref, wr_ref, br_ref,
                 h_ref, xn_ref, logit_ref, *nat_scr):
    tm = x_ref.shape[1]
    n_chunk = WIDTH_A // LANES

    for view_ref, scr in zip((o2_ref, o3_ref, l2_ref, l3_ref), nat_scr):
        dil = view_ref.shape[2] // WIDTH_A
        for r in range(dil):
            for c in range(n_chunk):
                col = r * WIDTH_A + c * LANES
                scr[c, pl.ds(r, tm // dil, stride=dil), :] = view_ref[0, :, col:col + LANES]

    def natural(scr, rows):
        return jnp.concatenate([scr[c, rows, :] for c in range(n_chunk)], axis=1)

    rows_per_chain = tm // POST_CHAINS
    for ci in range(POST_CHAINS):
        rows = slice(ci * rows_per_chain, (ci + 1) * rows_per_chain)
        o1, o2, o3 = o1_ref[0, rows, :], natural(nat_scr[0], rows), natural(nat_scr[1], rows)
        l1, l2, l3 = l1_ref[0, rows, :], natural(nat_scr[2], rows), natural(nat_scr[3], rows)
        mx = jnp.maximum(jnp.maximum(l1, l2), l3)
        w1, w2, w3 = jnp.exp2(l1 - mx), jnp.exp2(l2 - mx), jnp.exp2(l3 - mx)
        oa = (w1 * o1 + w2 * o2 + w3 * o3) / (w1 + w2 + w3)
        oan = _rms(oa, goa_ref[...]).astype(BF16)
        ob = jnp.concatenate([ob_ref[0, h, rows, :] for h in range(N_HEADS_B)], axis=1).astype(F32)
        ms_b = jnp.sum(ob * ob, axis=1, keepdims=True) * (1.0 / (N_HEADS_B * V_DIM_B))
        obn = (ob * lax.rsqrt(ms_b + EPS) * gob_ref[...]).astype(BF16)
        h1 = x_ref[0, rows, :] + _dot(oan, woa_ref[...]) + _dot(obn, wob_ref[...])

        hn = _rms(h1, gx_ref[...]).astype(BF16)
        q = _dot(hn, wmq_ref[...]).astype(BF16)
        dh = q.shape[1] // N_HEADS_MEM
        heads = []
        for h in range(N_HEADS_MEM):
            sl = slice(h * dh, (h + 1) * dh)
            s = _dot_nt(q[:, sl], km_ref[0, :, sl])
            e = jnp.exp(s - jnp.max(s, axis=1, keepdims=True))
            p = e / jnp.sum(e, axis=1, keepdims=True)
            heads.append(_dot(p.astype(BF16), vm_ref[0, :, sl]))
        o = jnp.concatenate(heads, axis=1).astype(BF16)
        h2 = h1 + _dot(o, wmo_ref[...])
        h_ref[0, rows, :] = h2

        xn = _rms(h2, gmoe_ref[...])
        packed = _pack_bf16_pairs(xn)
        rows_per_token = packed.shape[1] // LANES
        _store_token_rows(xn_ref.at[pl.ds(ci * rows_per_chain * rows_per_token,
                                          rows_per_chain * rows_per_token)], packed)
        logit_ref[0, rows, :] = jnp.dot(xn, wr_ref[...], preferred_element_type=F32,
                                        precision=lax.Precision.HIGHEST) + br_ref[...]


def _post(x, o_pats, lse_pats, ob, g_out_a, g_out_b_pad, w_o_a, w_o_b_pad, g_xattn, w_mq, kmem,
          vmem, w_mo, g_moe, w_router_pad, b_router_pad, tm):
    b, s, d = x.shape
    m = kmem.shape[1]
    full = lambda shape: pl.BlockSpec(shape, lambda bi, i: (0,) * len(shape))
    row = lambda w: pl.BlockSpec((1, tm, w), lambda bi, i: (bi, i, 0))
    memspec = pl.BlockSpec((1, m, d), lambda bi, i: (bi, 0, 0))
    hw = N_HEADS_B * LANES
    views = [pl.BlockSpec((1, tm // dil, dil * WIDTH_A), lambda bi, i: (bi, i, 0))
             for _, dil in DILATED_PATTERNS]
    return pl.pallas_call(
        _post_kernel,
        grid=(b, s // tm),
        in_specs=[row(d)] + views + views
        + [pl.BlockSpec((1, N_HEADS_B, tm, LANES), lambda bi, i: (bi, 0, i, 0)),
           full((1, WIDTH_A)), full((1, hw)), full((WIDTH_A, d)), full((hw, d)),
           full((1, d)), full((d, d)), memspec, memspec, full((d, d)),
           full((1, d)), full((d, LANES)), full((1, LANES))],
        out_specs=[row(d), pl.BlockSpec((tm * d // 2 // LANES, LANES), lambda bi, i: (bi * (s // tm) + i, 0)),
                   row(LANES)],
        out_shape=[jax.ShapeDtypeStruct((b, s, d), F32),
                   jax.ShapeDtypeStruct((b * s * d // 2 // LANES, LANES), jnp.uint32),
                   jax.ShapeDtypeStruct((b, s, LANES), F32)],
        scratch_shapes=[pltpu.VMEM((WIDTH_A // LANES, tm, LANES), F32)] * 4,
        compiler_params=pltpu.CompilerParams(
            dimension_semantics=("parallel", "parallel"), vmem_limit_bytes=VMEM_LIMIT),
        name="post",
    )(x, *o_pats, *lse_pats, ob, g_out_a, g_out_b_pad, w_o_a, w_o_b_pad, g_xattn, w_mq, kmem,
      vmem, w_mo, g_moe, w_router_pad, b_router_pad)


def _route_kernel(logit_ref, gate_ref, eidx_ref, rank_ref, slot_ref, count_ref, tcnt_ref, tcarry_ref,
                  carry_ref):
    @pl.when(pl.program_id(0) == 0)
    def _():
        carry_ref[...] = jnp.zeros_like(carry_ref)

    l = logit_ref[...]
    tr = l.shape[0]
    lane_i = lax.broadcasted_iota(jnp.int32, l.shape, 1)
    lane = lane_i.astype(F32)
    vals, idxs = [], []
    for _ in range(TOP_K):
        m = jnp.max(l, axis=1, keepdims=True)
        idx = jnp.min(jnp.where(l == m, lane, float(LANES)), axis=1, keepdims=True)
        vals.append(m)
        idxs.append(idx)
        l = jnp.where(lane == idx, -jnp.inf, l)
    exps = [jnp.exp(v - vals[0]) for v in vals]
    denom = exps[0] + exps[1] + exps[2] + exps[3]
    onehot = jnp.zeros(l.shape, F32)
    for idx in idxs:
        onehot = onehot + jnp.where(lane == idx, 1.0, 0.0)
    r = lax.broadcasted_iota(jnp.int32, (tr, tr), 0)
    c = lax.broadcasted_iota(jnp.int32, (tr, tr), 1)
    tri = jnp.where(c < r, 1.0, 0.0).astype(BF16)
    local = _dot(tri, onehot.astype(BF16))
    before = local + carry_ref[...]
    tile_cnt = jnp.sum(onehot, axis=0, keepdims=True)
    li = lax.broadcasted_iota(jnp.int32, (LANES, LANES), 0)
    lj = lax.broadcasted_iota(jnp.int32, (LANES, LANES), 1)
    prefix = jnp.dot(jnp.broadcast_to(tile_cnt, (8, LANES)), jnp.where(li < lj, 1.0, 0.0),
                     preferred_element_type=F32, precision=lax.Precision.HIGHEST)[0:1]
    grouped = local + prefix
    gate = jnp.zeros(l.shape, F32)
    eidx = jnp.zeros(l.shape, jnp.int32)
    rank = jnp.zeros(l.shape, jnp.int32)
    slot = jnp.zeros(l.shape, jnp.int32)
    for k in range(TOP_K):
        mine = lane == idxs[k]
        rk = jnp.sum(jnp.where(mine, before, 0.0), axis=1, keepdims=True)
        sk = jnp.sum(jnp.where(mine, grouped, 0.0), axis=1, keepdims=True)
        gate = jnp.where(lane_i == k, exps[k] / denom, gate)
        eidx = jnp.where(lane_i == k, idxs[k].astype(jnp.int32), eidx)
        rank = jnp.where(lane_i == k, rk.astype(jnp.int32), rank)
        slot = jnp.where(lane_i == k, sk.astype(jnp.int32), slot)
    gate_ref[...] = gate
    eidx_ref[...] = eidx
    rank_ref[...] = rank
    slot_ref[...] = slot
    tcnt_ref[...] = jnp.broadcast_to(tile_cnt, tcnt_ref.shape).astype(jnp.int32)
    tcarry_ref[...] = jnp.broadcast_to(carry_ref[...], tcarry_ref.shape).astype(jnp.int32)
    total = carry_ref[...] + tile_cnt
    carry_ref[...] = total
    count_ref[...] = total.astype(jnp.int32)


def _route(logits, tr):
    t = logits.shape[0]
    spec = pl.BlockSpec((tr, LANES), lambda i: (i, 0))
    return pl.pallas_call(
        _route_kernel,
        grid=(t // tr,),
        in_specs=[spec],
        out_specs=[spec, spec, spec, spec, pl.BlockSpec((1, LANES), lambda i: (0, 0)),
                   pl.BlockSpec((8, LANES), lambda i: (i, 0)), pl.BlockSpec((8, LANES), lambda i: (i, 0))],
        out_shape=[jax.ShapeDtypeStruct((t, LANES), F32)] + [jax.ShapeDtypeStruct((t, LANES), jnp.int32)] * 3
        + [jax.ShapeDtypeStruct((1, LANES), jnp.int32)]
        + [jax.ShapeDtypeStruct((t // tr * 8, LANES), jnp.int32)] * 2,
        scratch_shapes=[pltpu.VMEM((1, LANES), F32)],
        compiler_params=pltpu.CompilerParams(
            dimension_semantics=("arbitrary",), vmem_limit_bytes=VMEM_LIMIT),
        name="route",
    )(logits)


def _dispatch_kernel(dst_ref, cnt_ref, ends_ref, x_ref, slot_ref, xs_hbm, zbuf, stage, sem, zsem):
    c = pl.program_id(0)
    buf = lax.rem(c, 2)
    n_chunk = x_ref.shape[0] // DISPATCH_CHUNK
    blk_rows = zbuf.shape[0]
    n_staged = DISPATCH_CHUNK * TOP_K

    @pl.when(c == 0)
    def _():
        zbuf[...] = jnp.zeros_like(zbuf)
        for b in range(2):
            stage[b, pl.ds(n_staged * n_chunk, DISPATCH_PIECE * n_chunk), :] = jnp.zeros(
                (DISPATCH_PIECE * n_chunk, LANES), stage.dtype)

        def zero_copy(start):
            return pltpu.make_async_copy(
                zbuf, xs_hbm.at[pl.ds(pl.multiple_of(start * n_chunk, n_chunk), blk_rows)], zsem)

        blk = blk_rows // n_chunk
        used_end = ends_ref[N_EXPERTS]
        starts = [(ends_ref[e + 1] - ends_ref[e] >= back * blk, ends_ref[e + 1] - back * blk)
                  for e in range(N_EXPERTS) for back in (1, 2)]
        starts += [((used_end + (j + 1) * blk) * n_chunk <= xs_hbm.shape[0], used_end + j * blk)
                   for j in range(N_EXPERTS)]
        for cond, start in starts:
            @pl.when(cond)
            def _():
                zero_copy(start).start()
        for cond, start in starts:
            @pl.when(cond)
            def _():
                zero_copy(start).wait()

    x_lo, x_hi = _unpack_bf16_pairs(_load_token_rows(x_ref, DISPATCH_CHUNK))
    x_lo, x_hi = x_lo.astype(BF16), x_hi.astype(BF16)
    slot_t = jnp.transpose(slot_ref[...].astype(F32))
    bits = lambda v: lax.bitcast_convert_type(v, jnp.uint32)
    for part in range(TOP_K):
        j = (lax.broadcasted_iota(jnp.int32, (DISPATCH_CHUNK, 1), 0) + part * DISPATCH_CHUNK).astype(F32)
        pick = jnp.zeros((DISPATCH_CHUNK, DISPATCH_CHUNK), F32)
        for k in range(TOP_K):
            pick = jnp.where(slot_t[k:k + 1, :] == j, 1.0, pick)
        pick = pick.astype(BF16)
        packed = bits(_dot(pick, x_hi)) | (bits(_dot(pick, x_lo)) >> 16)
        _store_token_rows(
            stage.at[buf, pl.ds(part * DISPATCH_CHUNK * n_chunk, DISPATCH_CHUNK * n_chunk)], packed)

    def piece(src_tok, dst_tok):
        rows = DISPATCH_PIECE * n_chunk
        return pltpu.make_async_copy(
            stage.at[buf, pl.ds(pl.multiple_of(src_tok * n_chunk, n_chunk), rows)],
            xs_hbm.at[pl.ds(pl.multiple_of(dst_tok * n_chunk, n_chunk), rows)], sem)

    def drain(step):
        n_pieces = 0
        for e in range(N_EXPERTS):
            n_pieces = n_pieces + (cnt_ref[step * N_EXPERTS + e] + DISPATCH_PIECE - 1) // DISPATCH_PIECE

        def wait_one(i, carry):
            piece(0, 0).wait()
            return carry
        lax.fori_loop(0, n_pieces, wait_one, 0)

    @pl.when(c > 0)
    def _():
        drain(c - 1)

    staged = 0
    for e in range(N_EXPERTS):
        cnt = cnt_ref[c * N_EXPERTS + e]
        dst = dst_ref[c * N_EXPERTS + e]
        for p in range(DISPATCH_CHUNK // DISPATCH_PIECE):
            @pl.when(p * DISPATCH_PIECE < cnt)
            def _():
                piece(staged + p * DISPATCH_PIECE, dst + p * DISPATCH_PIECE).start()
        staged = staged + cnt

    @pl.when(c == pl.num_programs(0) - 1)
    def _():
        drain(c)


def _dispatch(dst, cnt, ends0, xn_rows, slots, n_tok, n_slots):
    n_chunk = xn_rows.shape[0] // n_tok
    staged_rows = (DISPATCH_CHUNK * TOP_K + DISPATCH_PIECE) * n_chunk
    grid_spec = pltpu.PrefetchScalarGridSpec(
        num_scalar_prefetch=3,
        grid=(n_tok // DISPATCH_CHUNK,),
        in_specs=[pl.BlockSpec((DISPATCH_CHUNK * n_chunk, LANES), lambda c, *_: (c, 0)),
                  pl.BlockSpec((DISPATCH_CHUNK, LANES), lambda c, *_: (c, 0))],
        out_specs=pl.BlockSpec(memory_space=pl.ANY),
        scratch_shapes=[pltpu.VMEM((EXPERT_BLOCK * n_chunk, LANES), xn_rows.dtype),
                        pltpu.VMEM((2, staged_rows, LANES), xn_rows.dtype),
                        pltpu.SemaphoreType.DMA(()), pltpu.SemaphoreType.DMA(())],
    )
    return pl.pallas_call(
        _dispatch_kernel,
        grid_spec=grid_spec,
        out_shape=jax.ShapeDtypeStruct((n_slots * n_chunk, LANES), xn_rows.dtype),
        compiler_params=pltpu.CompilerParams(
            dimension_semantics=("arbitrary",), vmem_limit_bytes=VMEM_LIMIT),
        name="dispatch",
    )(dst, cnt, ends0, xn_rows, slots)


def _expert_kernel(be_ref, nused_ref, next_ref, xs_ref, wgu_hbm, bgu_ref, wd_hbm, bd_ref, y_ref,
                   wgu_f32, wd_f32, wgu_bf, wd_bf, sems):
    i = pl.program_id(0)

    def fetch(e):
        return (pltpu.make_async_copy(wgu_hbm.at[e], wgu_f32, sems.at[0]),
                pltpu.make_async_copy(wd_hbm.at[e], wd_f32, sems.at[1]))

    @pl.when(i == 0)
    def _():
        for copy in fetch(be_ref[0]):
            copy.start()

    @pl.when(i < nused_ref[0])
    def _():
        @pl.when(jnp.logical_or(i == 0, be_ref[i] != be_ref[jnp.maximum(i - 1, 0)]))
        def _():
            for copy in fetch(be_ref[i]):
                copy.wait()
            wgu_bf[...] = wgu_f32[...].astype(BF16)
            wd_bf[...] = wd_f32[...].astype(BF16)

            @pl.when(next_ref[i] >= 0)
            def _():
                for copy in fetch(next_ref[i]):
                    copy.start()

        x_lo, x_hi = _unpack_bf16_pairs(_load_token_rows(xs_ref, EXPERT_BLOCK))
        x = jnp.concatenate([x_lo.astype(BF16), x_hi.astype(BF16)], axis=1)
        gu = _dot(x, wgu_bf[...]) + bgu_ref[0]
        de = gu.shape[1] // 2
        gate = jnp.minimum(gu[:, :de], SWIGLU_LIMIT)
        up = jnp.clip(gu[:, de:], -SWIGLU_LIMIT, SWIGLU_LIMIT)
        hmid = (up + 1.0) * (gate * jax.nn.sigmoid(SWIGLU_ALPHA * gate))
        y = _dot(hmid.astype(BF16), wd_bf[...]) + bd_ref[0]
        _store_token_rows(y_ref, _pack_bf16_pairs(y))

    @pl.when(i >= nused_ref[0])
    def _():
        y_ref[...] = jnp.zeros_like(y_ref)


def _experts(block_expert, n_used, next_expert, xs_rows, w_gate_up, b_gate_up, w_down, b_down):
    d, de2 = w_gate_up.shape[1:]
    n_chunk = d // 2 // LANES
    blk = EXPERT_BLOCK
    n_blk = xs_rows.shape[0] // (blk * n_chunk)
    grid_spec = pltpu.PrefetchScalarGridSpec(
        num_scalar_prefetch=3,
        grid=(n_blk,),
        in_specs=[pl.BlockSpec((blk * n_chunk, LANES),
                               lambda i, be, nu, nx: (jnp.maximum(jnp.minimum(i, nu[0] - 1), 0), 0)),
                  pl.BlockSpec(memory_space=pl.ANY),
                  pl.BlockSpec((1, 1, de2), lambda i, be, nu, nx: (be[i], 0, 0)),
                  pl.BlockSpec(memory_space=pl.ANY),
                  pl.BlockSpec((1, 1, d), lambda i, be, nu, nx: (be[i], 0, 0))],
        out_specs=pl.BlockSpec((blk * n_chunk, LANES), lambda i, be, nu, nx: (i, 0)),
        scratch_shapes=[pltpu.VMEM((d, de2), F32), pltpu.VMEM((de2 // 2, d), F32),
                        pltpu.VMEM((d, de2), BF16), pltpu.VMEM((de2 // 2, d), BF16),
                        pltpu.SemaphoreType.DMA((2,))],
    )
    return pl.pallas_call(
        _expert_kernel,
        grid_spec=grid_spec,
        out_shape=jax.ShapeDtypeStruct(xs_rows.shape, xs_rows.dtype),
        compiler_params=pltpu.CompilerParams(
            dimension_semantics=("arbitrary",), vmem_limit_bytes=VMEM_LIMIT),
        name="experts",
    )(block_expert, n_used, next_expert, xs_rows, w_gate_up, b_gate_up.reshape(N_EXPERTS, 1, de2),
      w_down, b_down.reshape(N_EXPERTS, 1, d))


def _combine_kernel(pos_ref, h_ref, gate_ref, gfin_ref, y_hbm, out_ref, buf, sems):
    i = pl.program_id(0)
    n = pl.num_programs(0)
    slot = lax.rem(i, 2)
    tc = h_ref.shape[0]
    n_chunk = buf.shape[2] // tc

    def token(ref, t):
        return ref.at[pl.ds(pl.multiple_of(t * n_chunk, n_chunk), n_chunk)]

    def issue(tile, s):
        def body(r, carry):
            for k in range(TOP_K):
                row = pos_ref[(tile * tc + r) * TOP_K + k]
                pltpu.make_async_copy(token(y_hbm, row), token(buf.at[s, k], r), sems.at[s]).start()
            return carry
        lax.fori_loop(0, tc, body, 0, unroll=4)

    @pl.when(i == 0)
    def _():
        issue(0, 0)

    @pl.when(i + 1 < n)
    def _():
        issue(i + 1, 1 - slot)

    for k in range(TOP_K):
        pltpu.make_async_copy(y_hbm.at[pl.ds(0, tc * n_chunk)], buf.at[slot, k], sems.at[slot]).wait()
    gate = gate_ref[...]
    half = h_ref.shape[1] // 2
    lo_pieces, hi_pieces = [], []
    for c in range(n_chunk):
        y_lo = h_ref[:, c * LANES:(c + 1) * LANES]
        y_hi = h_ref[:, half + c * LANES:half + (c + 1) * LANES]
        for k in range(TOP_K):
            lo, hi = _unpack_bf16_pairs(buf[slot, k, pl.ds(c, tc, stride=n_chunk), :])
            y_lo = y_lo + gate[:, k:k + 1] * lo
            y_hi = y_hi + gate[:, k:k + 1] * hi
        lo_pieces.append(y_lo)
        hi_pieces.append(y_hi)
    out_ref[...] = _rms(jnp.concatenate(lo_pieces + hi_pieces, axis=1), gfin_ref[...])


def _combine(pos, h2, gates, g_final, y_rows, tc):
    t, d = h2.shape
    n_chunk = d // 2 // LANES
    grid_spec = pltpu.PrefetchScalarGridSpec(
        num_scalar_prefetch=1,
        grid=(t // tc,),
        in_specs=[pl.BlockSpec((tc, d), lambda i, pos: (i, 0)),
                  pl.BlockSpec((tc, LANES), lambda i, pos: (i, 0)),
                  pl.BlockSpec((1, d), lambda i, pos: (0, 0)),
                  pl.BlockSpec(memory_space=pl.ANY)],
        out_specs=pl.BlockSpec((tc, d), lambda i, pos: (i, 0)),
        scratch_shapes=[pltpu.VMEM((2, TOP_K, tc * n_chunk, LANES), y_rows.dtype),
                        pltpu.SemaphoreType.DMA((2,))],
    )
    return pl.pallas_call(
        _combine_kernel,
        grid_spec=grid_spec,
        out_shape=jax.ShapeDtypeStruct((t, d), F32),
        compiler_params=pltpu.CompilerParams(
            dimension_semantics=("arbitrary",), vmem_limit_bytes=VMEM_LIMIT),
        name="combine",
    )(pos, h2, gates, g_final, y_rows)


def _pad_heads(w, n_heads, width, offset=0):
    k = w.shape[0]
    w = w.reshape(k, n_heads, width)
    w = jnp.pad(w, ((0, 0), (0, 0), (offset, LANES - width - offset)))
    return w.reshape(k, n_heads * LANES)


def _rot_cols(w):
    half = w.shape[-1] // 2
    return jnp.concatenate([-w[..., half:], w[..., :half]], axis=-1)


def kernel(x, mem, positions, g_mix, w_in, g_q_a, w_q_b, g_kv_a, w_kv_b, rel_bias, g_out_a, g_out_b,
           w_o, g_xattn, g_mem, w_mq, w_mkv, w_mo, g_moe, w_router, b_router, w_gate_up, b_gate_up,
           w_down, b_down, g_final):
    b, s, d = x.shape
    t = b * s
    assert g_mix.shape[0] == 1, "single-layer block: the final norm is fused into the last stage"

    inv_freq = ROPE_THETA ** (-jnp.arange(0, QK_ROPE_DIM, 2, dtype=F32) / QK_ROPE_DIM)
    ang = positions.astype(F32)[..., None] * inv_freq
    cos_sin = jnp.concatenate([jnp.cos(ang), jnp.sin(ang)], axis=-1)
    half = QK_ROPE_DIM // 2
    src = jnp.arange(QK_ROPE_DIM)[:, None]
    lane = jnp.arange(LANES)[None, :]
    on_rope = (lane >= QK_NOPE_DIM) & (lane < QK_NOPE_DIM + QK_ROPE_DIM)
    place = ((lane - QK_NOPE_DIM) % half == src % half) & on_rope
    expand = jnp.concatenate([place & (src < half), place & (src >= half)], axis=1).astype(F32)
    tbl = jnp.pad(rel_bias.T.astype(F32), ((0, 0), (0, LANES - NUM_BUCKETS)))

    h = x
    for l in range(1):
        c0 = 3 * WIDTH_A + Q_LORA_RANK + KV_LORA_RANK
        w_kpe = w_in[l][:, c0:c0 + QK_ROPE_DIM]
        place = lambda w: jnp.pad(w, ((0, 0), (QK_NOPE_DIM, LANES - QK_NOPE_DIM - QK_ROPE_DIM)))
        w_in_ext = jnp.concatenate([w_in[l][:, :c0], place(w_kpe), place(_rot_cols(w_kpe))], 1).astype(BF16)
        dq = QK_NOPE_DIM + QK_ROPE_DIM
        wq3 = w_q_b[l].reshape(Q_LORA_RANK, N_HEADS_B, dq)
        wq = _pad_heads(wq3.reshape(Q_LORA_RANK, -1), N_HEADS_B, dq).astype(BF16)
        wq_rot3 = jnp.concatenate([jnp.zeros_like(wq3[..., :QK_NOPE_DIM]), _rot_cols(wq3[..., QK_NOPE_DIM:])], -1)
        wqr = _pad_heads(wq_rot3.reshape(Q_LORA_RANK, -1), N_HEADS_B, dq).astype(BF16)
        wkv3 = w_kv_b[l].reshape(KV_LORA_RANK, N_HEADS_B, QK_NOPE_DIM + V_DIM_B)
        wk = _pad_heads(wkv3[..., :QK_NOPE_DIM].reshape(KV_LORA_RANK, -1), N_HEADS_B, QK_NOPE_DIM).astype(BF16)
        wv = _pad_heads(wkv3[..., QK_NOPE_DIM:].reshape(KV_LORA_RANK, -1), N_HEADS_B, V_DIM_B).astype(BF16)

        *views, qm, km, vm = _proj(h, cos_sin, expand, g_mix[l][None], w_in_ext, g_q_a[l][None], wq, wqr,
                                   g_kv_a[l][None], wk, wv, tm=256)
        pats = _dilated_all(views, positions, tbl)
        ob = _mla(qm, km, vm, tq=1024, tk=512)
        kmem, vmem = _memkv(mem, g_mem[l][None], w_mkv[l].astype(BF16))

        g_out_b_pad = _pad_heads(g_out_b[l][None], N_HEADS_B, V_DIM_B)
        w_o_b_pad = _pad_heads(w_o[l][WIDTH_A:].T, N_HEADS_B, V_DIM_B).T.astype(BF16)
        w_mq_s = (w_mq[l] * ((d // N_HEADS_MEM) ** -0.5)).astype(BF16)
        w_router_pad = jnp.pad(w_router[l], ((0, 0), (0, LANES - N_EXPERTS)))
        b_router_pad = jnp.pad(b_router[l][None], ((0, 0), (0, LANES - N_EXPERTS)), constant_values=NEG_INF)
        h2, xn_rows, logits = _post(h, pats[0::2], pats[1::2], ob, g_out_a[l][None],
                               g_out_b_pad, w_o[l][:WIDTH_A].astype(BF16), w_o_b_pad, g_xattn[l][None],
                               w_mq_s, kmem, vmem, w_mo[l].astype(BF16), g_moe[l][None], w_router_pad,
                               b_router_pad, tm=256 * POST_CHAINS)

        gates, eidx, rank, slots, counts, tile_cnt, tile_carry = _route(
            logits.reshape(t, LANES), tr=DISPATCH_CHUNK)
        counts = counts[0, :N_EXPERTS]
        blk = EXPERT_BLOCK
        padded = jnp.where(counts > 0, (counts + DISPATCH_PIECE + blk - 1) // blk * blk, 0)
        ends = jnp.cumsum(padded)
        pad_start = ends - padded
        eidx4, rank4 = eidx[:, :TOP_K], rank[:, :TOP_K]
        expert_ids = jnp.arange(N_EXPERTS, dtype=jnp.int32)
        pos = (jnp.sum(jnp.where(eidx4[..., None] == expert_ids, pad_start, 0), axis=-1)
               + rank4).astype(jnp.int32).reshape(-1)
        n_blk = t * TOP_K // blk + N_EXPERTS + -(-N_EXPERTS * DISPATCH_PIECE // blk)
        block_expert = jnp.minimum(
            jnp.sum(ends[None, :] <= (jnp.arange(n_blk) * blk)[:, None], axis=1),
            N_EXPERTS - 1).astype(jnp.int32)
        n_used = (ends[-1] // blk).astype(jnp.int32)[None]
        ends0 = jnp.concatenate([jnp.zeros((1,), jnp.int32), ends.astype(jnp.int32)])
        run_start = (pad_start[None, :] + tile_carry[::8, :N_EXPERTS]).astype(jnp.int32).reshape(-1)
        run_len = tile_cnt[::8, :N_EXPERTS].reshape(-1)
        xs_rows = _dispatch(run_start, run_len, ends0, xn_rows, slots, t, n_blk * blk)
        run_end_blk = ends[block_expert] // blk
        next_expert = jnp.where(run_end_blk < n_used[0],
                                block_expert[jnp.minimum(run_end_blk, n_blk - 1)], -1).astype(jnp.int32)
        y_rows = _experts(block_expert, n_used, next_expert, xs_rows, w_gate_up[l], b_gate_up[l], w_down[l], b_down[l])
        out = _combine(pos, h2.reshape(t, d), gates, g_final[None], y_rows, tc=128)
        h = out.reshape(b, s, d)
    return h
```

```python
import functools
import math

import jax
import jax.numpy as jnp
from jax import lax
from jax.experimental import pallas as pl
from jax.experimental.pallas import tpu as pltpu

F32 = jnp.float32
BF16 = jnp.bfloat16

LANES = 128
EPS = 1e-6
NEG_INF = -1e30
LOG2_E = math.log2(math.e)

N_HEADS_A = 8
HEAD_DIM_A = 64
WIDTH_A = N_HEADS_A * HEAD_DIM_A
DILATED_PATTERNS = ((128, 1), (512, 4), (2048, 16))
N_HEADS_B = 8
QK_NOPE_DIM = 64
QK_ROPE_DIM = 32
V_DIM_B = 64
Q_LORA_RANK = 256
KV_LORA_RANK = 128
ROPE_THETA = 10000.0
NUM_BUCKETS = 32
MAX_DISTANCE = 1024
N_HEADS_MEM = 4
N_EXPERTS = 32
TOP_K = 4
SWIGLU_LIMIT = 7.0
SWIGLU_ALPHA = 1.702

Q_BLOCK = 128
K_WINDOW = 256
HALF_WINDOW = (K_WINDOW - Q_BLOCK) // 2
MAX_Q_BLOCKS_PER_STEP = 4
EXPERT_BLOCK = 512
DISPATCH_CHUNK = 512
DISPATCH_PIECE = 64
COMBINE_TILE = 128
POST_CHAINS = 2
VMEM_LIMIT = 56 * 1024 * 1024


def _rms(x, g):
    return x * lax.rsqrt(jnp.mean(x * x, axis=-1, keepdims=True) + EPS) * g


def _dot(a, b):
    return jnp.dot(a, b, preferred_element_type=F32)


def _dot_nt(a, b):
    return lax.dot_general(a, b, (((1,), (1,)), ((), ())), preferred_element_type=F32)


def _load_token_rows(ref, n_tok):
    n_chunk = ref.shape[0] // n_tok
    return jnp.concatenate([ref[pl.ds(c, n_tok, stride=n_chunk), :] for c in range(n_chunk)], axis=1)


def _pack_bf16_pairs(x):
    half = x.shape[1] // 2
    bits = lambda v: lax.bitcast_convert_type(v.astype(BF16).astype(F32), jnp.uint32)
    return bits(x[:, half:]) | (bits(x[:, :half]) >> 16)


def _unpack_bf16_pairs(w):
    lo = lax.bitcast_convert_type(w << 16, F32)
    hi = lax.bitcast_convert_type(w & jnp.uint32(0xFFFF0000), F32)
    return lo, hi


def _store_token_rows(ref, value):
    n_tok = value.shape[0]
    n_chunk = value.shape[1] // LANES
    for c in range(n_chunk):
        ref[pl.ds(c, n_tok, stride=n_chunk), :] = value[:, c * LANES:(c + 1) * LANES]


def _proj_kernel(x_ref, cs_ref, expand_ref, gmix_ref, win_ref, gq_ref, wq_ref, wqr_ref,
                 gkv_ref, wk_ref, wv_ref,
                 q1_ref, k1_ref, v1_ref, q4_ref, k4_ref, v4_ref, q16_ref, k16_ref, v16_ref,
                 qm_ref, km_ref, vm_ref, za_scr):
    xn = _rms(x_ref[0], gmix_ref[...])
    z = _dot(xn.astype(BF16), win_ref[...])
    tm = z.shape[0]
    c0 = 3 * WIDTH_A
    per_group = WIDTH_A // LANES
    for c in range(3 * per_group):
        chunk = z[:, c * LANES:(c + 1) * LANES]
        za_scr[c] = chunk * (HEAD_DIM_A ** -0.5 * LOG2_E) if c < per_group else chunk
    for dil, refs in ((1, (q1_ref, k1_ref, v1_ref)), (4, (q4_ref, k4_ref, v4_ref)),
                      (16, (q16_ref, k16_ref, v16_ref))):
        for r in range(dil):
            for c in range(3 * per_group):
                rows = za_scr[c, pl.ds(r, tm // dil, stride=dil), :].astype(BF16)
                col = r * WIDTH_A + (c % per_group) * LANES
                refs[c // per_group][0, :, col:col + LANES] = rows
    zqn = _rms(z[:, c0:c0 + Q_LORA_RANK], gq_ref[...]).astype(BF16)
    zkvn = _rms(z[:, c0 + Q_LORA_RANK:c0 + Q_LORA_RANK + KV_LORA_RANK], gkv_ref[...]).astype(BF16)
    c1 = c0 + Q_LORA_RANK + KV_LORA_RANK
    spread = jnp.dot(cs_ref[0], expand_ref[...], preferred_element_type=F32,
                     precision=lax.Precision.HIGHEST)
    lane128 = lax.broadcasted_iota(jnp.int32, (1, LANES), 1)
    in_rope = jnp.logical_and(lane128 >= QK_NOPE_DIM, lane128 < QK_NOPE_DIM + QK_ROPE_DIM)
    cos = jnp.where(in_rope, spread[:, :LANES], 1.0)
    sin = spread[:, LANES:]
    kpe = z[:, c1:c1 + LANES] * cos + z[:, c1 + LANES:c1 + 2 * LANES] * sin
    q = _dot(zqn, wq_ref[...])
    qr = _dot(zqn, wqr_ref[...])
    kn = _dot(zkvn, wk_ref[...])
    vv = _dot(zkvn, wv_ref[...])
    lane = lax.broadcasted_iota(jnp.int32, (1, LANES), 1)
    ones_col = jnp.where(lane == V_DIM_B, 1.0, 0.0).astype(F32)
    scale = (QK_NOPE_DIM + QK_ROPE_DIM) ** -0.5 * LOG2_E
    for h in range(N_HEADS_B):
        sl = slice(h * LANES, (h + 1) * LANES)
        qm_ref[0, h] = ((q[:, sl] * cos + qr[:, sl] * sin) * scale).astype(BF16)
        km_ref[0, h] = (kn[:, sl] + kpe).astype(BF16)
        vm_ref[0, h] = (vv[:, sl] + ones_col).astype(BF16)


def _proj(x, cos_sin, expand, g_mix, w_in_ext, g_q, wq, wqr, g_kv, wk, wv, tm):
    b, s, d = x.shape
    n_in = w_in_ext.shape[1]
    hw = N_HEADS_B * LANES
    full = lambda shape: pl.BlockSpec(shape, lambda bi, i: (0,) * len(shape))
    row = lambda w: pl.BlockSpec((1, tm, w), lambda bi, i: (bi, i, 0))
    head = pl.BlockSpec((1, N_HEADS_B, tm, LANES), lambda bi, i: (bi, 0, i, 0))
    dils = [dil for _, dil in DILATED_PATTERNS]
    view_specs = [pl.BlockSpec((1, tm // dil, dil * WIDTH_A), lambda bi, i: (bi, i, 0))
                  for dil in dils for _ in range(3)]
    view_shapes = [jax.ShapeDtypeStruct((b, s // dil, dil * WIDTH_A), BF16)
                   for dil in dils for _ in range(3)]
    return pl.pallas_call(
        _proj_kernel,
        grid=(b, s // tm),
        in_specs=[row(d), row(cos_sin.shape[2]), full(expand.shape), full((1, d)), full((d, n_in)),
                  full((1, Q_LORA_RANK)), full((Q_LORA_RANK, hw)), full((Q_LORA_RANK, hw)),
                  full((1, KV_LORA_RANK)), full((KV_LORA_RANK, hw)), full((KV_LORA_RANK, hw))],
        out_specs=view_specs + [head, head, head],
        out_shape=view_shapes + [jax.ShapeDtypeStruct((b, N_HEADS_B, s, LANES), BF16)] * 3,
        scratch_shapes=[pltpu.VMEM((3 * WIDTH_A // LANES, tm, LANES), F32)],
        compiler_params=pltpu.CompilerParams(
            dimension_semantics=("parallel", "parallel"), vmem_limit_bytes=VMEM_LIMIT),
        name="proj",
    )(x, cos_sin, expand, g_mix, w_in_ext, g_q, wq, wqr, g_kv, wk, wv)


_LOG_BUCKET_STARTS = tuple(
    next(n for n in range(8, 4096)
         if int(math.log(n / 8) / math.log(MAX_DISTANCE / 8) * 8) >= t)
    for t in range(1, 8))


def _rel_bucket(rel):
    n = jnp.abs(rel)
    large = jnp.full(rel.shape, NUM_BUCKETS // 4, jnp.int32)
    for start in _LOG_BUCKET_STARTS:
        large = large + jnp.where(n >= start, 1, 0)
    mag = jnp.where(n < NUM_BUCKETS // 4, n, large)
    return mag + jnp.where(rel > 0, NUM_BUCKETS // 2, 0)


def _window_start(j, sub_len):
    return jnp.clip(j * Q_BLOCK - HALF_WINDOW, 0, sub_len - K_WINDOW)


def _bias_kernel(off_ref, qpos_ref, kpos_ref, tbl_ref, out_ref):
    rel = kpos_ref[0] - qpos_ref[0]
    bucket = _rel_bucket(rel)
    delta = (off_ref[pl.program_id(0)] + lax.broadcasted_iota(jnp.int32, (1, K_WINDOW), 1)
             - lax.broadcasted_iota(jnp.int32, (Q_BLOCK, 1), 0))
    valid = jnp.abs(delta) <= HALF_WINDOW
    for h in range(N_HEADS_A):
        tbl = jnp.broadcast_to(tbl_ref[h:h + 1, :], (Q_BLOCK, LANES))
        bias = jnp.concatenate(
            [jnp.take_along_axis(tbl, bucket[:, c * LANES:(c + 1) * LANES], axis=1)
             for c in range(K_WINDOW // LANES)], axis=1)
        out_ref[0, h] = jnp.where(valid, bias * LOG2_E, NEG_INF)


def _bias_tiles(offs, qpos, kpos, tbl):
    n = offs.shape[0]
    grid_spec = pltpu.PrefetchScalarGridSpec(
        num_scalar_prefetch=1,
        grid=(n,),
        in_specs=[pl.BlockSpec((1, Q_BLOCK, 1), lambda t, off: (t, 0, 0)),
                  pl.BlockSpec((1, 1, K_WINDOW), lambda t, off: (t, 0, 0)),
                  pl.BlockSpec((N_HEADS_A, LANES), lambda t, off: (0, 0))],
        out_specs=pl.BlockSpec((1, N_HEADS_A, Q_BLOCK, K_WINDOW), lambda t, off: (t, 0, 0, 0)),
    )
    return pl.pallas_call(
        _bias_kernel,
        grid_spec=grid_spec,
        out_shape=jax.ShapeDtypeStruct((n, N_HEADS_A, Q_BLOCK, K_WINDOW), F32),
        compiler_params=pltpu.CompilerParams(
            dimension_semantics=("arbitrary",), vmem_limit_bytes=VMEM_LIMIT),
        name="bias_tiles",
    )(offs, qpos, kpos, tbl)


def _dilated_kernel(q_ref, k_ref, v_ref, *refs, sub_len, qb):
    bias_refs = refs[:qb]
    o_ref, lse_ref, s_scr, p_scr = refs[qb:]
    first = lax.broadcasted_iota(jnp.int32, (1, LANES), 1) < HEAD_DIM_A
    pair = lambda h: slice((h // 2) * LANES, (h // 2 + 1) * LANES)
    for sub in range(qb):
        j = pl.program_id(2) * qb + sub
        rows = slice(sub * Q_BLOCK, (sub + 1) * Q_BLOCK)
        kstart = pl.multiple_of(_window_start(j, sub_len), HALF_WINDOW)
        q = q_ref[0, rows, :]
        kw = k_ref[0, pl.ds(kstart, K_WINDOW), :]
        vw = v_ref[0, pl.ds(kstart, K_WINDOW), :]
        for h in range(N_HEADS_A):
            own = first if h % 2 == 0 else jnp.logical_not(first)
            qh = jnp.where(own, q[:, pair(h)], jnp.zeros_like(q[:, pair(h)]))
            s_scr[sub, h] = _dot_nt(qh, kw[:, pair(h)]) + bias_refs[sub][0, h]
        stats = []
        for h in range(N_HEADS_A):
            s = s_scr[sub, h]
            m = jnp.max(s, axis=1, keepdims=True)
            e = jnp.exp2(s - m)
            l = jnp.sum(e, axis=1, keepdims=True)
            p_scr[sub, h] = e.astype(BF16)
            stats.append((1.0 / l, m + jnp.log2(l)))
        for h in range(0, N_HEADS_A, 2):
            (r0, lse0), (r1, lse1) = stats[h], stats[h + 1]
            o0 = _dot(p_scr[sub, h], vw[:, pair(h)])
            o1 = _dot(p_scr[sub, h + 1], vw[:, pair(h)])
            o_ref[0, rows, pair(h)] = jnp.where(first, o0 * r0, o1 * r1)
            lse_ref[0, rows, pair(h)] = jnp.where(first, lse0, lse1)


def _dilated(qv, kv, vv, bias, bias_index, dil):
    b, sub_len, _ = qv.shape
    w = WIDTH_A
    nq = sub_len // Q_BLOCK
    qb = min(MAX_Q_BLOCKS_PER_STEP, nq)
    assert sub_len >= K_WINDOW and sub_len % (Q_BLOCK * qb) == 0
    qspec = pl.BlockSpec((1, Q_BLOCK * qb, w), lambda bi, r, j: (bi, j, r))
    kvspec = pl.BlockSpec((1, sub_len, w), lambda bi, r, j: (bi, 0, r))
    bspecs = [pl.BlockSpec((1, N_HEADS_A, Q_BLOCK, K_WINDOW),
                           lambda bi, r, j, sub=sub:
                           (bias_index(bi, r, j * qb + sub, nq), 0, 0, 0))
              for sub in range(qb)]
    return pl.pallas_call(
        functools.partial(_dilated_kernel, sub_len=sub_len, qb=qb),
        grid=(b, dil, nq // qb),
        in_specs=[qspec, kvspec, kvspec] + bspecs,
        out_specs=[qspec, qspec],
        out_shape=[jax.ShapeDtypeStruct((b, sub_len, dil * w), F32)] * 2,
        scratch_shapes=[pltpu.VMEM((qb, N_HEADS_A, Q_BLOCK, K_WINDOW), F32),
                        pltpu.VMEM((qb, N_HEADS_A, Q_BLOCK, K_WINDOW), BF16)],
        compiler_params=pltpu.CompilerParams(
            dimension_semantics=("parallel", "parallel", "arbitrary"),
            vmem_limit_bytes=VMEM_LIMIT),
        name=f"dilated{dil}",
    )(qv, kv, vv, *([bias] * qb))


def _dilated_all(views, positions, tbl):
    b, s = positions.shape
    dils = [dil for _, dil in DILATED_PATTERNS]
    consecutive = jnp.all(positions[:, 1:] - positions[:, :-1] == 1)

    def run(bias, index_fns):
        outs = []
        for g, dil in enumerate(dils):
            qv, kv, vv = views[3 * g:3 * g + 3]
            outs.extend(_dilated(qv, kv, vv, bias, index_fns[g], dil))
        return tuple(outs)

    def shared_tiles():
        offs, qpos, kpos, fns = [], [], [], []
        for g, dil in enumerate(dils):
            for off in (0, -HALF_WINDOW, -2 * HALF_WINDOW):
                offs.append(off)
                qpos.append(dil * jnp.arange(Q_BLOCK, dtype=jnp.int32))
                kpos.append(dil * (off + jnp.arange(K_WINDOW, dtype=jnp.int32)))
            fns.append(lambda bi, r, j, nq, g=g:
                       3 * g + jnp.where(j == 0, 0, jnp.where(j == nq - 1, 2, 1)))
        bias = _bias_tiles(jnp.array(offs, jnp.int32), jnp.stack(qpos)[:, :, None],
                           jnp.stack(kpos)[:, None, :], tbl)
        return run(bias, fns)

    def per_block_tiles():
        offs, qpos, kpos, fns = [], [], [], []
        base = 0
        for g, dil in enumerate(dils):
            sub_len = s // dil
            nq = sub_len // Q_BLOCK
            pos_t = positions.reshape(b, sub_len, dil).transpose(0, 2, 1).reshape(b * dil, sub_len)
            starts = _window_start(jnp.arange(nq), sub_len)
            win = starts[:, None] + jnp.arange(K_WINDOW)[None, :]
            qpos.append(pos_t.reshape(b * dil * nq, Q_BLOCK))
            kpos.append(pos_t[:, win].reshape(b * dil * nq, K_WINDOW))
            offs.append(jnp.tile(starts - jnp.arange(nq) * Q_BLOCK, b * dil))
            fns.append(lambda bi, r, j, nq, base=base, dil=dil: base + (bi * dil + r) * nq + j)
            base += b * dil * nq
        bias = _bias_tiles(jnp.concatenate(offs).astype(jnp.int32),
                           jnp.concatenate(qpos)[:, :, None], jnp.concatenate(kpos)[:, None, :], tbl)
        return run(bias, fns)

    return lax.cond(consecutive, shared_tiles, per_block_tiles)


def _mla_kernel(q_ref, k_ref, v_ref, o_ref, *, tk):
    q = q_ref[0, 0]
    tq = q.shape[0]
    n_kv = k_ref.shape[2] // tk

    def body(i, carry):
        m, acc = carry
        start = pl.multiple_of(i * tk, tk)
        k = k_ref[0, 0, pl.ds(start, tk), :]
        v = v_ref[0, 0, pl.ds(start, tk), :]
        s = _dot_nt(q, k)
        m_new = jnp.maximum(m, jnp.max(s, axis=1, keepdims=True))
        p = jnp.exp2(s - m_new)
        acc = jnp.exp2(m - m_new) * acc + _dot(p.astype(BF16), v)
        return m_new, acc

    m0 = jnp.full((tq, 1), NEG_INF, F32)
    _, acc = lax.fori_loop(0, n_kv, body, (m0, jnp.zeros((tq, LANES), F32)), unroll=True)
    lane = lax.broadcasted_iota(jnp.int32, (1, LANES), 1)
    denom = jnp.sum(jnp.where(lane == V_DIM_B, acc, 0.0), axis=1, keepdims=True)
    o_ref[0, 0] = jnp.where(lane < V_DIM_B, acc / denom, 0.0).astype(BF16)


def _mla(qm, km, vm, tq, tk):
    b, nh, s, _ = qm.shape
    qspec = pl.BlockSpec((1, 1, tq, LANES), lambda bi, h, i: (bi, h, i, 0))
    kvspec = pl.BlockSpec((1, 1, s, LANES), lambda bi, h, i: (bi, h, 0, 0))
    return pl.pallas_call(
        functools.partial(_mla_kernel, tk=tk),
        grid=(b, nh, s // tq),
        in_specs=[qspec, kvspec, kvspec],
        out_specs=qspec,
        out_shape=jax.ShapeDtypeStruct((b, nh, s, LANES), BF16),
        compiler_params=pltpu.CompilerParams(
            dimension_semantics=("parallel", "parallel", "arbitrary"),
            vmem_limit_bytes=VMEM_LIMIT),
        name="mla",
    )(qm, km, vm)


def _memkv_kernel(mem_ref, g_ref, w_ref, k_ref, v_ref):
    d = mem_ref.shape[2]
    kv = _dot(_rms(mem_ref[0], g_ref[...]).astype(BF16), w_ref[...])
    k_ref[0] = kv[:, :d].astype(BF16)
    v_ref[0] = kv[:, d:].astype(BF16)


def _memkv(mem, g_mem, w_mkv):
    b, m, d = mem.shape
    spec = pl.BlockSpec((1, m, d), lambda bi: (bi, 0, 0))
    return pl.pallas_call(
        _memkv_kernel,
        grid=(b,),
        in_specs=[spec, pl.BlockSpec((1, d), lambda bi: (0, 0)),
                  pl.BlockSpec((d, 2 * d), lambda bi: (0, 0))],
        out_specs=[spec, spec],
        out_shape=[jax.ShapeDtypeStruct((b, m, d), BF16)] * 2,
        compiler_params=pltpu.CompilerParams(
            dimension_semantics=("parallel",), vmem_limit_bytes=VMEM_LIMIT),
        name="memkv",
    )(mem, g_mem, w_mkv)


def _post_kernel(x_ref, o1_ref, o2_ref, o3_ref, l1_ref, l2_ref, l3_ref, ob_ref,
                 goa_ref, gob_ref, woa_ref, wob_ref, gx_ref, wmq_ref, km_ref, vm_ref, wmo_ref,
                 gmoe_ref, wr_ref, br_ref,
                 h_ref, xn_ref, logit_ref, *nat_scr):
    tm = x_ref.shape[1]
    n_chunk = WIDTH_A // LANES

    for view_ref, scr in zip((o2_ref, o3_ref, l2_ref, l3_ref), nat_scr):
        dil = view_ref.shape[2] // WIDTH_A
        for r in range(dil):
            for c in range(n_chunk):
                col = r * WIDTH_A + c * LANES
                scr[c, pl.ds(r, tm // dil, stride=dil), :] = view_ref[0, :, col:col + LANES]

    def natural(scr, rows):
        return jnp.concatenate([scr[c, rows, :] for c in range(n_chunk)], axis=1)

    rows_per_chain = tm // POST_CHAINS
    for ci in range(POST_CHAINS):
        rows = slice(ci * rows_per_chain, (ci + 1) * rows_per_chain)
        o1, o2, o3 = o1_ref[0, rows, :], natural(nat_scr[0], rows), natural(nat_scr[1], rows)
        l1, l2, l3 = l1_ref[0, rows, :], natural(nat_scr[2], rows), natural(nat_scr[3], rows)
        mx = jnp.maximum(jnp.maximum(l1, l2), l3)
        w1, w2, w3 = jnp.exp2(l1 - mx), jnp.exp2(l2 - mx), jnp.exp2(l3 - mx)
        oa = (w1 * o1 + w2 * o2 + w3 * o3) / (w1 + w2 + w3)
        oan = _rms(oa, goa_ref[...]).astype(BF16)
        ob = jnp.concatenate([ob_ref[0, h, rows, :] for h in range(N_HEADS_B)], axis=1).astype(F32)
        ms_b = jnp.sum(ob * ob, axis=1, keepdims=True) * (1.0 / (N_HEADS_B * V_DIM_B))
        obn = (ob * lax.rsqrt(ms_b + EPS) * gob_ref[...]).astype(BF16)
        h1 = x_ref[0, rows, :] + _dot(oan, woa_ref[...]) + _dot(obn, wob_ref[...])

        hn = _rms(h1, gx_ref[...]).astype(BF16)
        q = _dot(hn, wmq_ref[...]).astype(BF16)
        dh = q.shape[1] // N_HEADS_MEM
        heads = []
        for h in range(N_HEADS_MEM):
            sl = slice(h * dh, (h + 1) * dh)
            s = _dot_nt(q[:, sl], km_ref[0, :, sl])
            e = jnp.exp(s - jnp.max(s, axis=1, keepdims=True))
            p = e / jnp.sum(e, axis=1, keepdims=True)
            heads.append(_dot(p.astype(BF16), vm_ref[0, :, sl]))
        o = jnp.concatenate(heads, axis=1).astype(BF16)
        h2 = h1 + _dot(o, wmo_ref[...])
        h_ref[0, rows, :] = h2

        xn = _rms(h2, gmoe_ref[...])
        packed = _pack_bf16_pairs(xn)
        rows_per_token = packed.shape[1] // LANES
        _store_token_rows(xn_ref.at[pl.ds(ci * rows_per_chain * rows_per_token,
                                          rows_per_chain * rows_per_token)], packed)
        logit_ref[0, rows, :] = jnp.dot(xn, wr_ref[...], preferred_element_type=F32,
                                        precision=lax.Precision.HIGHEST) + br_ref[...]


def _post(x, o_pats, lse_pats, ob, g_out_a, g_out_b_pad, w_o_a, w_o_b_pad, g_xattn, w_mq, kmem,
          vmem, w_mo, g_moe, w_router_pad, b_router_pad, tm):
    b, s, d = x.shape
    m = kmem.shape[1]
    full = lambda shape: pl.BlockSpec(shape, lambda bi, i: (0,) * len(shape))
    row = lambda w: pl.BlockSpec((1, tm, w), lambda bi, i: (bi, i, 0))
    memspec = pl.BlockSpec((1, m, d), lambda bi, i: (bi, 0, 0))
    hw = N_HEADS_B * LANES
    views = [pl.BlockSpec((1, tm // dil, dil * WIDTH_A), lambda bi, i: (bi, i, 0))
             for _, dil in DILATED_PATTERNS]
    return pl.pallas_call(
        _post_kernel,
        grid=(b, s // tm),
        in_specs=[row(d)] + views + views
        + [pl.BlockSpec((1, N_HEADS_B, tm, LANES), lambda bi, i: (bi, 0, i, 0)),
           full((1, WIDTH_A)), full((1, hw)), full((WIDTH_A, d)), full((hw, d)),
           full((1, d)), full((d, d)), memspec, memspec, full((d, d)),
           full((1, d)), full((d, LANES)), full((1, LANES))],
        out_specs=[row(d), pl.BlockSpec((tm * d // 2 // LANES, LANES), lambda bi, i: (bi * (s // tm) + i, 0)),
                   row(LANES)],
        out_shape=[jax.ShapeDtypeStruct((b, s, d), F32),
                   jax.ShapeDtypeStruct((b * s * d // 2 // LANES, LANES), jnp.uint32),
                   jax.ShapeDtypeStruct((b, s, LANES), F32)],
        scratch_shapes=[pltpu.VMEM((WIDTH_A // LANES, tm, LANES), F32)] * 4,
        compiler_params=pltpu.CompilerParams(
            dimension_semantics=("parallel", "parallel"), vmem_limit_bytes=VMEM_LIMIT),
        name="post",
    )(x, *o_pats, *lse_pats, ob, g_out_a, g_out_b_pad, w_o_a, w_o_b_pad, g_xattn, w_mq, kmem,
      vmem, w_mo, g_moe, w_router_pad, b_router_pad)


def _route_kernel(logit_ref, gate_ref, eidx_ref, rank_ref, slot_ref, cslot_ref, count_ref, tcnt_ref,
                  tcarry_ref, ccnt_ref, ccarry_ref, carry_ref):
    @pl.when(pl.program_id(0) == 0)
    def _():
        carry_ref[...] = jnp.zeros_like(carry_ref)

    l = logit_ref[...]
    tr = l.shape[0]
    lane_i = lax.broadcasted_iota(jnp.int32, l.shape, 1)
    lane = lane_i.astype(F32)
    vals, idxs = [], []
    for _ in range(TOP_K):
        m = jnp.max(l, axis=1, keepdims=True)
        idx = jnp.min(jnp.where(l == m, lane, float(LANES)), axis=1, keepdims=True)
        vals.append(m)
        idxs.append(idx)
        l = jnp.where(lane == idx, -jnp.inf, l)
    exps = [jnp.exp(v - vals[0]) for v in vals]
    denom = exps[0] + exps[1] + exps[2] + exps[3]
    onehot = jnp.zeros(l.shape, F32)
    for idx in idxs:
        onehot = onehot + jnp.where(lane == idx, 1.0, 0.0)
    r = lax.broadcasted_iota(jnp.int32, (tr, tr), 0)
    c = lax.broadcasted_iota(jnp.int32, (tr, tr), 1)
    tri = jnp.where(c < r, 1.0, 0.0).astype(BF16)
    local = _dot(tri, onehot.astype(BF16))
    before = local + carry_ref[...]
    tile_cnt = jnp.sum(onehot, axis=0, keepdims=True)
    li = lax.broadcasted_iota(jnp.int32, (LANES, LANES), 0)
    lj = lax.broadcasted_iota(jnp.int32, (LANES, LANES), 1)
    prefix = jnp.dot(jnp.broadcast_to(tile_cnt, (8, LANES)), jnp.where(li < lj, 1.0, 0.0),
                     preferred_element_type=F32, precision=lax.Precision.HIGHEST)[0:1]
    grouped = local + prefix
    n_sub = tr // COMBINE_TILE
    sub_cnt = [jnp.sum(onehot[u * COMBINE_TILE:(u + 1) * COMBINE_TILE], axis=0, keepdims=True)
               for u in range(n_sub)]
    sub_prefix = jnp.dot(jnp.concatenate(sub_cnt + [jnp.zeros((8 - n_sub, LANES), F32)], axis=0),
                         jnp.where(li < lj, 1.0, 0.0), preferred_element_type=F32,
                         precision=lax.Precision.HIGHEST)
    shifts, sub_base = [], jnp.zeros((1, LANES), F32)
    for u in range(n_sub):
        shifts.append(jnp.broadcast_to(sub_prefix[u:u + 1] - sub_base, (COMBINE_TILE, LANES)))
        ccnt_ref[u * 8:(u + 1) * 8, :] = jnp.broadcast_to(sub_cnt[u], (8, LANES)).astype(jnp.int32)
        ccarry_ref[u * 8:(u + 1) * 8, :] = jnp.broadcast_to(carry_ref[...] + sub_base,
                                                            (8, LANES)).astype(jnp.int32)
        sub_base = sub_base + sub_cnt[u]
    sub_grouped = local + jnp.concatenate(shifts, axis=0)
    gate = jnp.zeros(l.shape, F32)
    eidx = jnp.zeros(l.shape, jnp.int32)
    rank = jnp.zeros(l.shape, jnp.int32)
    slot = jnp.zeros(l.shape, jnp.int32)
    cslot = jnp.zeros(l.shape, jnp.int32)
    for k in range(TOP_K):
        mine = lane == idxs[k]
        rk = jnp.sum(jnp.where(mine, before, 0.0), axis=1, keepdims=True)
        sk = jnp.sum(jnp.where(mine, grouped, 0.0), axis=1, keepdims=True)
        ck = jnp.sum(jnp.where(mine, sub_grouped, 0.0), axis=1, keepdims=True)
        gate = jnp.where(lane_i == k, exps[k] / denom, gate)
        eidx = jnp.where(lane_i == k, idxs[k].astype(jnp.int32), eidx)
        rank = jnp.where(lane_i == k, rk.astype(jnp.int32), rank)
        slot = jnp.where(lane_i == k, sk.astype(jnp.int32), slot)
        cslot = jnp.where(lane_i == k, ck.astype(jnp.int32), cslot)
    gate_ref[...] = gate
    eidx_ref[...] = eidx
    rank_ref[...] = rank
    slot_ref[...] = slot
    cslot_ref[...] = cslot
    tcnt_ref[...] = jnp.broadcast_to(tile_cnt, tcnt_ref.shape).astype(jnp.int32)
    tcarry_ref[...] = jnp.broadcast_to(carry_ref[...], tcarry_ref.shape).astype(jnp.int32)
    total = carry_ref[...] + tile_cnt
    carry_ref[...] = total
    count_ref[...] = total.astype(jnp.int32)


def _route(logits, tr):
    t = logits.shape[0]
    spec = pl.BlockSpec((tr, LANES), lambda i: (i, 0))
    return pl.pallas_call(
        _route_kernel,
        grid=(t // tr,),
        in_specs=[spec],
        out_specs=[spec, spec, spec, spec, spec, pl.BlockSpec((1, LANES), lambda i: (0, 0)),
                   pl.BlockSpec((8, LANES), lambda i: (i, 0)), pl.BlockSpec((8, LANES), lambda i: (i, 0)),
                   pl.BlockSpec((8 * tr // COMBINE_TILE, LANES), lambda i: (i, 0)),
                   pl.BlockSpec((8 * tr // COMBINE_TILE, LANES), lambda i: (i, 0))],
        out_shape=[jax.ShapeDtypeStruct((t, LANES), F32)] + [jax.ShapeDtypeStruct((t, LANES), jnp.int32)] * 4
        + [jax.ShapeDtypeStruct((1, LANES), jnp.int32)]
        + [jax.ShapeDtypeStruct((t // tr * 8, LANES), jnp.int32)] * 2
        + [jax.ShapeDtypeStruct((t // COMBINE_TILE * 8, LANES), jnp.int32)] * 2,
        scratch_shapes=[pltpu.VMEM((1, LANES), F32)],
        compiler_params=pltpu.CompilerParams(
            dimension_semantics=("arbitrary",), vmem_limit_bytes=VMEM_LIMIT),
        name="route",
    )(logits)


def _dispatch_kernel(dst_ref, cnt_ref, ends_ref, x_ref, slot_ref, xs_hbm, zbuf, stage, sem, zsem):
    c = pl.program_id(0)
    buf = lax.rem(c, 2)
    n_chunk = x_ref.shape[0] // DISPATCH_CHUNK
    blk_rows = zbuf.shape[0]
    n_staged = DISPATCH_CHUNK * TOP_K

    @pl.when(c == 0)
    def _():
        zbuf[...] = jnp.zeros_like(zbuf)
        for b in range(2):
            stage[b, pl.ds(n_staged * n_chunk, DISPATCH_PIECE * n_chunk), :] = jnp.zeros(
                (DISPATCH_PIECE * n_chunk, LANES), stage.dtype)

        def zero_copy(start):
            return pltpu.make_async_copy(
                zbuf, xs_hbm.at[pl.ds(pl.multiple_of(start * n_chunk, n_chunk), blk_rows)], zsem)

        blk = blk_rows // n_chunk
        used_end = ends_ref[N_EXPERTS]
        starts = [(ends_ref[e + 1] - ends_ref[e] >= back * blk, ends_ref[e + 1] - back * blk)
                  for e in range(N_EXPERTS) for back in (1, 2)]
        starts += [((used_end + (j + 1) * blk) * n_chunk <= xs_hbm.shape[0], used_end + j * blk)
                   for j in range(N_EXPERTS)]
        for cond, start in starts:
            @pl.when(cond)
            def _():
                zero_copy(start).start()
        for cond, start in starts:
            @pl.when(cond)
            def _():
                zero_copy(start).wait()

    x_lo, x_hi = _unpack_bf16_pairs(_load_token_rows(x_ref, DISPATCH_CHUNK))
    x_lo, x_hi = x_lo.astype(BF16), x_hi.astype(BF16)
    slot_t = jnp.transpose(slot_ref[...].astype(F32))
    bits = lambda v: lax.bitcast_convert_type(v, jnp.uint32)
    for part in range(TOP_K):
        j = (lax.broadcasted_iota(jnp.int32, (DISPATCH_CHUNK, 1), 0) + part * DISPATCH_CHUNK).astype(F32)
        pick = jnp.zeros((DISPATCH_CHUNK, DISPATCH_CHUNK), F32)
        for k in range(TOP_K):
            pick = jnp.where(slot_t[k:k + 1, :] == j, 1.0, pick)
        pick = pick.astype(BF16)
        packed = bits(_dot(pick, x_hi)) | (bits(_dot(pick, x_lo)) >> 16)
        _store_token_rows(
            stage.at[buf, pl.ds(part * DISPATCH_CHUNK * n_chunk, DISPATCH_CHUNK * n_chunk)], packed)

    def piece(src_tok, dst_tok):
        rows = DISPATCH_PIECE * n_chunk
        return pltpu.make_async_copy(
            stage.at[buf, pl.ds(pl.multiple_of(src_tok * n_chunk, n_chunk), rows)],
            xs_hbm.at[pl.ds(pl.multiple_of(dst_tok * n_chunk, n_chunk), rows)], sem)

    def drain(step):
        n_pieces = 0
        for e in range(N_EXPERTS):
            n_pieces = n_pieces + (cnt_ref[step * N_EXPERTS + e] + DISPATCH_PIECE - 1) // DISPATCH_PIECE

        def wait_one(i, carry):
            piece(0, 0).wait()
            return carry
        lax.fori_loop(0, n_pieces, wait_one, 0)

    @pl.when(c > 0)
    def _():
        drain(c - 1)

    staged = 0
    for e in range(N_EXPERTS):
        cnt = cnt_ref[c * N_EXPERTS + e]
        dst = dst_ref[c * N_EXPERTS + e]
        for p in range(DISPATCH_CHUNK // DISPATCH_PIECE):
            @pl.when(p * DISPATCH_PIECE < cnt)
            def _():
                piece(staged + p * DISPATCH_PIECE, dst + p * DISPATCH_PIECE).start()
        staged = staged + cnt

    @pl.when(c == pl.num_programs(0) - 1)
    def _():
        drain(c)


def _dispatch(dst, cnt, ends0, xn_rows, slots, n_tok, n_slots):
    n_chunk = xn_rows.shape[0] // n_tok
    staged_rows = (DISPATCH_CHUNK * TOP_K + DISPATCH_PIECE) * n_chunk
    grid_spec = pltpu.PrefetchScalarGridSpec(
        num_scalar_prefetch=3,
        grid=(n_tok // DISPATCH_CHUNK,),
        in_specs=[pl.BlockSpec((DISPATCH_CHUNK * n_chunk, LANES), lambda c, *_: (c, 0)),
                  pl.BlockSpec((DISPATCH_CHUNK, LANES), lambda c, *_: (c, 0))],
        out_specs=pl.BlockSpec(memory_space=pl.ANY),
        scratch_shapes=[pltpu.VMEM((EXPERT_BLOCK * n_chunk, LANES), xn_rows.dtype),
                        pltpu.VMEM((2, staged_rows, LANES), xn_rows.dtype),
                        pltpu.SemaphoreType.DMA(()), pltpu.SemaphoreType.DMA(())],
    )
    return pl.pallas_call(
        _dispatch_kernel,
        grid_spec=grid_spec,
        out_shape=jax.ShapeDtypeStruct((n_slots * n_chunk, LANES), xn_rows.dtype),
        compiler_params=pltpu.CompilerParams(
            dimension_semantics=("arbitrary",), vmem_limit_bytes=VMEM_LIMIT),
        name="dispatch",
    )(dst, cnt, ends0, xn_rows, slots)


def _expert_kernel(be_ref, nused_ref, next_ref, xs_ref, wgu_hbm, bgu_ref, wd_hbm, bd_ref, y_ref,
                   wgu_f32, wd_f32, wgu_bf, wd_bf, sems):
    i = pl.program_id(0)

    def fetch(e):
        return (pltpu.make_async_copy(wgu_hbm.at[e], wgu_f32, sems.at[0]),
                pltpu.make_async_copy(wd_hbm.at[e], wd_f32, sems.at[1]))

    @pl.when(i == 0)
    def _():
        for copy in fetch(be_ref[0]):
            copy.start()

    @pl.when(i < nused_ref[0])
    def _():
        @pl.when(jnp.logical_or(i == 0, be_ref[i] != be_ref[jnp.maximum(i - 1, 0)]))
        def _():
            for copy in fetch(be_ref[i]):
                copy.wait()
            wgu_bf[...] = wgu_f32[...].astype(BF16)
            wd_bf[...] = wd_f32[...].astype(BF16)

            @pl.when(next_ref[i] >= 0)
            def _():
                for copy in fetch(next_ref[i]):
                    copy.start()

        x_lo, x_hi = _unpack_bf16_pairs(_load_token_rows(xs_ref, EXPERT_BLOCK))
        x = jnp.concatenate([x_lo.astype(BF16), x_hi.astype(BF16)], axis=1)
        gu = _dot(x, wgu_bf[...]) + bgu_ref[0]
        de = gu.shape[1] // 2
        gate = jnp.minimum(gu[:, :de], SWIGLU_LIMIT)
        up = jnp.clip(gu[:, de:], -SWIGLU_LIMIT, SWIGLU_LIMIT)
        hmid = (up + 1.0) * (gate * jax.nn.sigmoid(SWIGLU_ALPHA * gate))
        y = _dot(hmid.astype(BF16), wd_bf[...]) + bd_ref[0]
        _store_token_rows(y_ref, _pack_bf16_pairs(y))

    @pl.when(i >= nused_ref[0])
    def _():
        y_ref[...] = jnp.zeros_like(y_ref)


def _experts(block_expert, n_used, next_expert, xs_rows, w_gate_up, b_gate_up, w_down, b_down):
    d, de2 = w_gate_up.shape[1:]
    n_chunk = d // 2 // LANES
    blk = EXPERT_BLOCK
    n_blk = xs_rows.shape[0] // (blk * n_chunk)
    grid_spec = pltpu.PrefetchScalarGridSpec(
        num_scalar_prefetch=3,
        grid=(n_blk,),
        in_specs=[pl.BlockSpec((blk * n_chunk, LANES),
                               lambda i, be, nu, nx: (jnp.maximum(jnp.minimum(i, nu[0] - 1), 0), 0)),
                  pl.BlockSpec(memory_space=pl.ANY),
                  pl.BlockSpec((1, 1, de2), lambda i, be, nu, nx: (be[i], 0, 0)),
                  pl.BlockSpec(memory_space=pl.ANY),
                  pl.BlockSpec((1, 1, d), lambda i, be, nu, nx: (be[i], 0, 0))],
        out_specs=pl.BlockSpec((blk * n_chunk, LANES), lambda i, be, nu, nx: (i, 0)),
        scratch_shapes=[pltpu.VMEM((d, de2), F32), pltpu.VMEM((de2 // 2, d), F32),
                        pltpu.VMEM((d, de2), BF16), pltpu.VMEM((de2 // 2, d), BF16),
                        pltpu.SemaphoreType.DMA((2,))],
    )
    return pl.pallas_call(
        _expert_kernel,
        grid_spec=grid_spec,
        out_shape=jax.ShapeDtypeStruct(xs_rows.shape, xs_rows.dtype),
        compiler_params=pltpu.CompilerParams(
            dimension_semantics=("arbitrary",), vmem_limit_bytes=VMEM_LIMIT),
        name="experts",
    )(block_expert, n_used, next_expert, xs_rows, w_gate_up, b_gate_up.reshape(N_EXPERTS, 1, de2),
      w_down, b_down.reshape(N_EXPERTS, 1, d))


def _combine_kernel(start_ref, len_ref, h_ref, gate_ref, slot_ref, gfin_ref, y_hbm, out_ref, stage, sems):
    i = pl.program_id(0)
    n = pl.num_programs(0)
    buf = lax.rem(i, 2)
    tc = h_ref.shape[0]
    n_staged = tc * TOP_K
    n_chunk = stage.shape[1] // n_staged

    def issue(tile, b):
        staged = 0
        for e in range(N_EXPERTS):
            run_len = len_ref[tile * N_EXPERTS + e]
            run_start = start_ref[tile * N_EXPERTS + e]
            for bit in reversed(range(COMBINE_TILE.bit_length())):
                size = 1 << bit
                done = (run_len >> (bit + 1)) << (bit + 1)

                @pl.when((run_len >> bit) & 1 == 1)
                def _():
                    pltpu.make_async_copy(
                        y_hbm.at[pl.ds(pl.multiple_of((run_start + done) * n_chunk, n_chunk), size * n_chunk)],
                        stage.at[b, pl.ds(pl.multiple_of((staged + done) * n_chunk, n_chunk), size * n_chunk)],
                        sems.at[b]).start()
            staged = staged + run_len

    @pl.when(i == 0)
    def _():
        issue(0, 0)

    @pl.when(i + 1 < n)
    def _():
        issue(i + 1, 1 - buf)

    pltpu.make_async_copy(y_hbm.at[pl.ds(0, n_staged * n_chunk)], stage.at[buf], sems.at[buf]).wait()

    y_lo, y_hi = _unpack_bf16_pairs(_load_token_rows(stage.at[buf], n_staged))
    y_lo, y_hi = y_lo.astype(BF16), y_hi.astype(BF16)
    gate = gate_ref[...]
    slot = slot_ref[...].astype(F32)
    j = lax.broadcasted_iota(jnp.int32, (1, n_staged), 1).astype(F32)
    pick = jnp.zeros((tc, n_staged), F32)
    for k in range(TOP_K):
        pick = jnp.where(slot[:, k:k + 1] == j, gate[:, k:k + 1], pick)
    pick_hi = pick.astype(BF16)
    pick_lo = (pick - pick_hi.astype(F32)).astype(BF16)
    moe = jnp.concatenate([_dot(pick_hi, y_lo) + _dot(pick_lo, y_lo),
                           _dot(pick_hi, y_hi) + _dot(pick_lo, y_hi)], axis=1)
    out_ref[...] = _rms(h_ref[...] + moe, gfin_ref[...])


def _combine(run_start, run_len, h2, gates, slots, g_final, y_rows):
    t, d = h2.shape
    tc = COMBINE_TILE
    n_chunk = d // 2 // LANES
    grid_spec = pltpu.PrefetchScalarGridSpec(
        num_scalar_prefetch=2,
        grid=(t // tc,),
        in_specs=[pl.BlockSpec((tc, d), lambda i, *_: (i, 0)),
                  pl.BlockSpec((tc, LANES), lambda i, *_: (i, 0)),
                  pl.BlockSpec((tc, LANES), lambda i, *_: (i, 0)),
                  pl.BlockSpec((1, d), lambda i, *_: (0, 0)),
                  pl.BlockSpec(memory_space=pl.ANY)],
        out_specs=pl.BlockSpec((tc, d), lambda i, *_: (i, 0)),
        scratch_shapes=[pltpu.VMEM((2, tc * TOP_K * n_chunk, LANES), y_rows.dtype),
                        pltpu.SemaphoreType.DMA((2,))],
    )
    return pl.pallas_call(
        _combine_kernel,
        grid_spec=grid_spec,
        out_shape=jax.ShapeDtypeStruct((t, d), F32),
        compiler_params=pltpu.CompilerParams(
            dimension_semantics=("arbitrary",), vmem_limit_bytes=VMEM_LIMIT),
        name="combine",
    )(run_start, run_len, h2, gates, slots, g_final, y_rows)


def _pad_heads(w, n_heads, width, offset=0):
    k = w.shape[0]
    w = w.reshape(k, n_heads, width)
    w = jnp.pad(w, ((0, 0), (0, 0), (offset, LANES - width - offset)))
    return w.reshape(k, n_heads * LANES)


def _rot_cols(w):
    half = w.shape[-1] // 2
    return jnp.concatenate([-w[..., half:], w[..., :half]], axis=-1)


def kernel(x, mem, positions, g_mix, w_in, g_q_a, w_q_b, g_kv_a, w_kv_b, rel_bias, g_out_a, g_out_b,
           w_o, g_xattn, g_mem, w_mq, w_mkv, w_mo, g_moe, w_router, b_router, w_gate_up, b_gate_up,
           w_down, b_down, g_final):
    b, s, d = x.shape
    t = b * s
    assert g_mix.shape[0] == 1, "single-layer block: the final norm is fused into the last stage"

    inv_freq = ROPE_THETA ** (-jnp.arange(0, QK_ROPE_DIM, 2, dtype=F32) / QK_ROPE_DIM)
    ang = positions.astype(F32)[..., None] * inv_freq
    cos_sin = jnp.concatenate([jnp.cos(ang), jnp.sin(ang)], axis=-1)
    half = QK_ROPE_DIM // 2
    src = jnp.arange(QK_ROPE_DIM)[:, None]
    lane = jnp.arange(LANES)[None, :]
    on_rope = (lane >= QK_NOPE_DIM) & (lane < QK_NOPE_DIM + QK_ROPE_DIM)
    place = ((lane - QK_NOPE_DIM) % half == src % half) & on_rope
    expand = jnp.concatenate([place & (src < half), place & (src >= half)], axis=1).astype(F32)
    tbl = jnp.pad(rel_bias.T.astype(F32), ((0, 0), (0, LANES - NUM_BUCKETS)))

    h = x
    for l in range(1):
        c0 = 3 * WIDTH_A + Q_LORA_RANK + KV_LORA_RANK
        w_kpe = w_in[l][:, c0:c0 + QK_ROPE_DIM]
        place = lambda w: jnp.pad(w, ((0, 0), (QK_NOPE_DIM, LANES - QK_NOPE_DIM - QK_ROPE_DIM)))
        w_in_ext = jnp.concatenate([w_in[l][:, :c0], place(w_kpe), place(_rot_cols(w_kpe))], 1).astype(BF16)
        dq = QK_NOPE_DIM + QK_ROPE_DIM
        wq3 = w_q_b[l].reshape(Q_LORA_RANK, N_HEADS_B, dq)
        wq = _pad_heads(wq3.reshape(Q_LORA_RANK, -1), N_HEADS_B, dq).astype(BF16)
        wq_rot3 = jnp.concatenate([jnp.zeros_like(wq3[..., :QK_NOPE_DIM]), _rot_cols(wq3[..., QK_NOPE_DIM:])], -1)
        wqr = _pad_heads(wq_rot3.reshape(Q_LORA_RANK, -1), N_HEADS_B, dq).astype(BF16)
        wkv3 = w_kv_b[l].reshape(KV_LORA_RANK, N_HEADS_B, QK_NOPE_DIM + V_DIM_B)
        wk = _pad_heads(wkv3[..., :QK_NOPE_DIM].reshape(KV_LORA_RANK, -1), N_HEADS_B, QK_NOPE_DIM).astype(BF16)
        wv = _pad_heads(wkv3[..., QK_NOPE_DIM:].reshape(KV_LORA_RANK, -1), N_HEADS_B, V_DIM_B).astype(BF16)

        *views, qm, km, vm = _proj(h, cos_sin, expand, g_mix[l][None], w_in_ext, g_q_a[l][None], wq, wqr,
                                   g_kv_a[l][None], wk, wv, tm=256)
        pats = _dilated_all(views, positions, tbl)
        ob = _mla(qm, km, vm, tq=1024, tk=512)
        kmem, vmem = _memkv(mem, g_mem[l][None], w_mkv[l].astype(BF16))

        g_out_b_pad = _pad_heads(g_out_b[l][None], N_HEADS_B, V_DIM_B)
        w_o_b_pad = _pad_heads(w_o[l][WIDTH_A:].T, N_HEADS_B, V_DIM_B).T.astype(BF16)
        w_mq_s = (w_mq[l] * ((d // N_HEADS_MEM) ** -0.5)).astype(BF16)
        w_router_pad = jnp.pad(w_router[l], ((0, 0), (0, LANES - N_EXPERTS)))
        b_router_pad = jnp.pad(b_router[l][None], ((0, 0), (0, LANES - N_EXPERTS)), constant_values=NEG_INF)
        h2, xn_rows, logits = _post(h, pats[0::2], pats[1::2], ob, g_out_a[l][None],
                               g_out_b_pad, w_o[l][:WIDTH_A].astype(BF16), w_o_b_pad, g_xattn[l][None],
                               w_mq_s, kmem, vmem, w_mo[l].astype(BF16), g_moe[l][None], w_router_pad,
                               b_router_pad, tm=256 * POST_CHAINS)

        gates, eidx, rank, slots, cslots, counts, tile_cnt, tile_carry, ctile_cnt, ctile_carry = _route(
            logits.reshape(t, LANES), tr=DISPATCH_CHUNK)
        counts = counts[0, :N_EXPERTS]
        blk = EXPERT_BLOCK
        padded = jnp.where(counts > 0, (counts + DISPATCH_PIECE + blk - 1) // blk * blk, 0)
        ends = jnp.cumsum(padded)
        pad_start = ends - padded
        n_blk = t * TOP_K // blk + N_EXPERTS + -(-N_EXPERTS * DISPATCH_PIECE // blk)
        block_expert = jnp.minimum(
            jnp.sum(ends[None, :] <= (jnp.arange(n_blk) * blk)[:, None], axis=1),
            N_EXPERTS - 1).astype(jnp.int32)
        n_used = (ends[-1] // blk).astype(jnp.int32)[None]
        ends0 = jnp.concatenate([jnp.zeros((1,), jnp.int32), ends.astype(jnp.int32)])
        run_start = (pad_start[None, :] + tile_carry[::8, :N_EXPERTS]).astype(jnp.int32).reshape(-1)
        run_len = tile_cnt[::8, :N_EXPERTS].reshape(-1)
        xs_rows = _dispatch(run_start, run_len, ends0, xn_rows, slots, t, n_blk * blk)
        run_end_blk = ends[block_expert] // blk
        next_expert = jnp.where(run_end_blk < n_used[0],
                                block_expert[jnp.minimum(run_end_blk, n_blk - 1)], -1).astype(jnp.int32)
        y_rows = _experts(block_expert, n_used, next_expert, xs_rows, w_gate_up[l], b_gate_up[l], w_down[l], b_down[l])
        gather_start = (pad_start[None, :] + ctile_carry[::8, :N_EXPERTS]).astype(jnp.int32).reshape(-1)
        gather_len = ctile_cnt[::8, :N_EXPERTS].reshape(-1)
        out = _combine(gather_start, gather_len, h2.reshape(t, d), gates, cslots, g_final[None], y_rows)
        h = out.reshape(b, s, d)
    return h
```

```python
import functools
import math

import jax
import jax.numpy as jnp
from jax import lax
from jax.experimental import pallas as pl
from jax.experimental.pallas import tpu as pltpu

F32 = jnp.float32
BF16 = jnp.bfloat16

LANES = 128
EPS = 1e-6
NEG_INF = -1e30
LOG2_E = math.log2(math.e)

N_HEADS_A = 8
HEAD_DIM_A = 64
WIDTH_A = N_HEADS_A * HEAD_DIM_A
DILATED_PATTERNS = ((128, 1), (512, 4), (2048, 16))
N_HEADS_B = 8
QK_NOPE_DIM = 64
QK_ROPE_DIM = 32
V_DIM_B = 64
Q_LORA_RANK = 256
KV_LORA_RANK = 128
ROPE_THETA = 10000.0
NUM_BUCKETS = 32
MAX_DISTANCE = 1024
N_HEADS_MEM = 4
N_EXPERTS = 32
TOP_K = 4
SWIGLU_LIMIT = 7.0
SWIGLU_ALPHA = 1.702

Q_BLOCK = 128
K_WINDOW = 256
HALF_WINDOW = (K_WINDOW - Q_BLOCK) // 2
MAX_Q_BLOCKS_PER_STEP = 4
EXPERT_BLOCK = 512
DISPATCH_CHUNK = 512
DISPATCH_PIECE = 64
COMBINE_TILE = 128
PROJ_CHAINS = 2
POST_CHAINS = 2
VMEM_LIMIT = 56 * 1024 * 1024


def _rms(x, g):
    return x * lax.rsqrt(jnp.mean(x * x, axis=-1, keepdims=True) + EPS) * g


def _dot(a, b):
    return jnp.dot(a, b, preferred_element_type=F32)


def _dot_nt(a, b):
    return lax.dot_general(a, b, (((1,), (1,)), ((), ())), preferred_element_type=F32)


def _run_skewed(chains):
    live = []
    pending = list(chains)
    while live or pending:
        if pending:
            live.append(pending.pop(0))
        for gen in list(live):
            try:
                next(gen)
            except StopIteration:
                live.remove(gen)


def _load_token_rows(ref, n_tok):
    n_chunk = ref.shape[0] // n_tok
    return jnp.concatenate([ref[pl.ds(c, n_tok, stride=n_chunk), :] for c in range(n_chunk)], axis=1)


def _pack_bf16_pairs(x):
    half = x.shape[1] // 2
    bits = lambda v: lax.bitcast_convert_type(v.astype(BF16).astype(F32), jnp.uint32)
    return bits(x[:, half:]) | (bits(x[:, :half]) >> 16)


def _unpack_bf16_pairs(w):
    lo = lax.bitcast_convert_type(w << 16, F32)
    hi = lax.bitcast_convert_type(w & jnp.uint32(0xFFFF0000), F32)
    return lo, hi


def _store_token_rows(ref, value):
    n_tok = value.shape[0]
    n_chunk = value.shape[1] // LANES
    for c in range(n_chunk):
        ref[pl.ds(c, n_tok, stride=n_chunk), :] = value[:, c * LANES:(c + 1) * LANES]


def _proj_kernel(x_ref, cs_ref, expand_ref, gmix_ref, win_ref, gq_ref, wq_ref, wqr_ref,
                 gkv_ref, wk_ref, wv_ref,
                 q1_ref, k1_ref, v1_ref, q4_ref, k4_ref, v4_ref, q16_ref, k16_ref, v16_ref,
                 qm_ref, km_ref, vm_ref, za_scr):
    tm = x_ref.shape[1]
    c0 = 3 * WIDTH_A
    c1 = c0 + Q_LORA_RANK + KV_LORA_RANK
    per_group = WIDTH_A // LANES
    rows_per_chain = tm // PROJ_CHAINS
    lane = lax.broadcasted_iota(jnp.int32, (1, LANES), 1)
    in_rope = jnp.logical_and(lane >= QK_NOPE_DIM, lane < QK_NOPE_DIM + QK_ROPE_DIM)
    ones_col = jnp.where(lane == V_DIM_B, 1.0, 0.0).astype(F32)
    scale = (QK_NOPE_DIM + QK_ROPE_DIM) ** -0.5 * LOG2_E

    def chain(ci):
        r0 = ci * rows_per_chain
        rows = slice(r0, r0 + rows_per_chain)
        xn = _rms(x_ref[0, rows, :], gmix_ref[...]).astype(BF16)
        yield
        z = _dot(xn, win_ref[...])
        yield
        for c in range(3 * per_group):
            chunk = z[:, c * LANES:(c + 1) * LANES]
            za_scr[c, rows, :] = chunk * (HEAD_DIM_A ** -0.5 * LOG2_E) if c < per_group else chunk
        for dil, refs in ((1, (q1_ref, k1_ref, v1_ref)), (4, (q4_ref, k4_ref, v4_ref)),
                          (16, (q16_ref, k16_ref, v16_ref))):
            n = rows_per_chain // dil
            for r in range(dil):
                for c in range(3 * per_group):
                    picked = za_scr[c, pl.ds(r0 + r, n, stride=dil), :].astype(BF16)
                    col = r * WIDTH_A + (c % per_group) * LANES
                    refs[c // per_group][0, r0 // dil:r0 // dil + n, col:col + LANES] = picked
        zqn = _rms(z[:, c0:c0 + Q_LORA_RANK], gq_ref[...]).astype(BF16)
        zkvn = _rms(z[:, c0 + Q_LORA_RANK:c1], gkv_ref[...]).astype(BF16)
        yield
        spread = jnp.dot(cs_ref[0, rows, :], expand_ref[...], preferred_element_type=F32,
                         precision=lax.Precision.HIGHEST)
        q = _dot(zqn, wq_ref[...])
        qr = _dot(zqn, wqr_ref[...])
        kn = _dot(zkvn, wk_ref[...])
        vv = _dot(zkvn, wv_ref[...])
        yield
        cos = jnp.where(in_rope, spread[:, :LANES], 1.0)
        sin = spread[:, LANES:]
        kpe = z[:, c1:c1 + LANES] * cos + z[:, c1 + LANES:c1 + 2 * LANES] * sin
        for h in range(N_HEADS_B):
            sl = slice(h * LANES, (h + 1) * LANES)
            qm_ref[0, h, rows, :] = ((q[:, sl] * cos + qr[:, sl] * sin) * scale).astype(BF16)
            km_ref[0, h, rows, :] = (kn[:, sl] + kpe).astype(BF16)
            vm_ref[0, h, rows, :] = (vv[:, sl] + ones_col).astype(BF16)

    _run_skewed([chain(ci) for ci in range(PROJ_CHAINS)])


def _proj(x, cos_sin, expand, g_mix, w_in_ext, g_q, wq, wqr, g_kv, wk, wv, tm):
    b, s, d = x.shape
    n_in = w_in_ext.shape[1]
    hw = N_HEADS_B * LANES
    full = lambda shape: pl.BlockSpec(shape, lambda bi, i: (0,) * len(shape))
    row = lambda w: pl.BlockSpec((1, tm, w), lambda bi, i: (bi, i, 0))
    head = pl.BlockSpec((1, N_HEADS_B, tm, LANES), lambda bi, i: (bi, 0, i, 0))
    dils = [dil for _, dil in DILATED_PATTERNS]
    view_specs = [pl.BlockSpec((1, tm // dil, dil * WIDTH_A), lambda bi, i: (bi, i, 0))
                  for dil in dils for _ in range(3)]
    view_shapes = [jax.ShapeDtypeStruct((b, s // dil, dil * WIDTH_A), BF16)
                   for dil in dils for _ in range(3)]
    return pl.pallas_call(
        _proj_kernel,
        grid=(b, s // tm),
        in_specs=[row(d), row(cos_sin.shape[2]), full(expand.shape), full((1, d)), full((d, n_in)),
                  full((1, Q_LORA_RANK)), full((Q_LORA_RANK, hw)), full((Q_LORA_RANK, hw)),
                  full((1, KV_LORA_RANK)), full((KV_LORA_RANK, hw)), full((KV_LORA_RANK, hw))],
        out_specs=view_specs + [head, head, head],
        out_shape=view_shapes + [jax.ShapeDtypeStruct((b, N_HEADS_B, s, LANES), BF16)] * 3,
        scratch_shapes=[pltpu.VMEM((3 * WIDTH_A // LANES, tm, LANES), F32)],
        compiler_params=pltpu.CompilerParams(
            dimension_semantics=("parallel", "parallel"), vmem_limit_bytes=VMEM_LIMIT),
        name="proj",
    )(x, cos_sin, expand, g_mix, w_in_ext, g_q, wq, wqr, g_kv, wk, wv)


_LOG_BUCKET_STARTS = tuple(
    next(n for n in range(8, 4096)
         if int(math.log(n / 8) / math.log(MAX_DISTANCE / 8) * 8) >= t)
    for t in range(1, 8))


def _rel_bucket(rel):
    n = jnp.abs(rel)
    large = jnp.full(rel.shape, NUM_BUCKETS // 4, jnp.int32)
    for start in _LOG_BUCKET_STARTS:
        large = large + jnp.where(n >= start, 1, 0)
    mag = jnp.where(n < NUM_BUCKETS // 4, n, large)
    return mag + jnp.where(rel > 0, NUM_BUCKETS // 2, 0)


def _window_start(j, sub_len):
    return jnp.clip(j * Q_BLOCK - HALF_WINDOW, 0, sub_len - K_WINDOW)


def _bias_kernel(off_ref, qpos_ref, kpos_ref, tbl_ref, out_ref):
    rel = kpos_ref[0] - qpos_ref[0]
    bucket = _rel_bucket(rel)
    delta = (off_ref[pl.program_id(0)] + lax.broadcasted_iota(jnp.int32, (1, K_WINDOW), 1)
             - lax.broadcasted_iota(jnp.int32, (Q_BLOCK, 1), 0))
    valid = jnp.abs(delta) <= HALF_WINDOW
    for h in range(N_HEADS_A):
        tbl = jnp.broadcast_to(tbl_ref[h:h + 1, :], (Q_BLOCK, LANES))
        bias = jnp.concatenate(
            [jnp.take_along_axis(tbl, bucket[:, c * LANES:(c + 1) * LANES], axis=1)
             for c in range(K_WINDOW // LANES)], axis=1)
        out_ref[0, h] = jnp.where(valid, bias * LOG2_E, NEG_INF)


def _bias_tiles(offs, qpos, kpos, tbl):
    n = offs.shape[0]
    grid_spec = pltpu.PrefetchScalarGridSpec(
        num_scalar_prefetch=1,
        grid=(n,),
        in_specs=[pl.BlockSpec((1, Q_BLOCK, 1), lambda t, off: (t, 0, 0)),
                  pl.BlockSpec((1, 1, K_WINDOW), lambda t, off: (t, 0, 0)),
                  pl.BlockSpec((N_HEADS_A, LANES), lambda t, off: (0, 0))],
        out_specs=pl.BlockSpec((1, N_HEADS_A, Q_BLOCK, K_WINDOW), lambda t, off: (t, 0, 0, 0)),
    )
    return pl.pallas_call(
        _bias_kernel,
        grid_spec=grid_spec,
        out_shape=jax.ShapeDtypeStruct((n, N_HEADS_A, Q_BLOCK, K_WINDOW), F32),
        compiler_params=pltpu.CompilerParams(
            dimension_semantics=("arbitrary",), vmem_limit_bytes=VMEM_LIMIT),
        name="bias_tiles",
    )(offs, qpos, kpos, tbl)


def _dilated_kernel(q_ref, k_ref, v_ref, *refs, sub_len, qb):
    bias_refs = refs[:qb]
    o_ref, lse_ref, s_scr, p_scr = refs[qb:]
    first = lax.broadcasted_iota(jnp.int32, (1, LANES), 1) < HEAD_DIM_A
    pair = lambda h: slice((h // 2) * LANES, (h // 2 + 1) * LANES)
    for sub in range(qb):
        j = pl.program_id(2) * qb + sub
        rows = slice(sub * Q_BLOCK, (sub + 1) * Q_BLOCK)
        kstart = pl.multiple_of(_window_start(j, sub_len), HALF_WINDOW)
        q = q_ref[0, rows, :]
        kw = k_ref[0, pl.ds(kstart, K_WINDOW), :]
        vw = v_ref[0, pl.ds(kstart, K_WINDOW), :]
        for h in range(N_HEADS_A):
            own = first if h % 2 == 0 else jnp.logical_not(first)
            qh = jnp.where(own, q[:, pair(h)], jnp.zeros_like(q[:, pair(h)]))
            s_scr[sub, h] = _dot_nt(qh, kw[:, pair(h)]) + bias_refs[sub][0, h]
        stats = []
        for h in range(N_HEADS_A):
            s = s_scr[sub, h]
            m = jnp.max(s, axis=1, keepdims=True)
            e = jnp.exp2(s - m)
            l = jnp.sum(e, axis=1, keepdims=True)
            p_scr[sub, h] = e.astype(BF16)
            stats.append((1.0 / l, m + jnp.log2(l)))
        for h in range(0, N_HEADS_A, 2):
            (r0, lse0), (r1, lse1) = stats[h], stats[h + 1]
            o0 = _dot(p_scr[sub, h], vw[:, pair(h)])
            o1 = _dot(p_scr[sub, h + 1], vw[:, pair(h)])
            o_ref[0, rows, pair(h)] = jnp.where(first, o0 * r0, o1 * r1)
            lse_ref[0, rows, pair(h)] = jnp.where(first, lse0, lse1)


def _dilated(qv, kv, vv, bias, bias_index, dil):
    b, sub_len, _ = qv.shape
    w = WIDTH_A
    nq = sub_len // Q_BLOCK
    qb = min(MAX_Q_BLOCKS_PER_STEP, nq)
    assert sub_len >= K_WINDOW and sub_len % (Q_BLOCK * qb) == 0
    qspec = pl.BlockSpec((1, Q_BLOCK * qb, w), lambda bi, r, j: (bi, j, r))
    kvspec = pl.BlockSpec((1, sub_len, w), lambda bi, r, j: (bi, 0, r))
    bspecs = [pl.BlockSpec((1, N_HEADS_A, Q_BLOCK, K_WINDOW),
                           lambda bi, r, j, sub=sub:
                           (bias_index(bi, r, j * qb + sub, nq), 0, 0, 0))
              for sub in range(qb)]
    return pl.pallas_call(
        functools.partial(_dilated_kernel, sub_len=sub_len, qb=qb),
        grid=(b, dil, nq // qb),
        in_specs=[qspec, kvspec, kvspec] + bspecs,
        out_specs=[qspec, qspec],
        out_shape=[jax.ShapeDtypeStruct((b, sub_len, dil * w), F32)] * 2,
        scratch_shapes=[pltpu.VMEM((qb, N_HEADS_A, Q_BLOCK, K_WINDOW), F32),
                        pltpu.VMEM((qb, N_HEADS_A, Q_BLOCK, K_WINDOW), BF16)],
        compiler_params=pltpu.CompilerParams(
            dimension_semantics=("parallel", "parallel", "arbitrary"),
            vmem_limit_bytes=VMEM_LIMIT),
        name=f"dilated{dil}",
    )(qv, kv, vv, *([bias] * qb))


def _dilated_all(views, positions, tbl):
    b, s = positions.shape
    dils = [dil for _, dil in DILATED_PATTERNS]
    consecutive = jnp.all(positions[:, 1:] - positions[:, :-1] == 1)

    def run(bias, index_fns):
        outs = []
        for g, dil in enumerate(dils):
            qv, kv, vv = views[3 * g:3 * g + 3]
            outs.extend(_dilated(qv, kv, vv, bias, index_fns[g], dil))
        return tuple(outs)

    def shared_tiles():
        offs, qpos, kpos, fns = [], [], [], []
        for g, dil in enumerate(dils):
            for off in (0, -HALF_WINDOW, -2 * HALF_WINDOW):
                offs.append(off)
                qpos.append(dil * jnp.arange(Q_BLOCK, dtype=jnp.int32))
                kpos.append(dil * (off + jnp.arange(K_WINDOW, dtype=jnp.int32)))
            fns.append(lambda bi, r, j, nq, g=g:
                       3 * g + jnp.where(j == 0, 0, jnp.where(j == nq - 1, 2, 1)))
        bias = _bias_tiles(jnp.array(offs, jnp.int32), jnp.stack(qpos)[:, :, None],
                           jnp.stack(kpos)[:, None, :], tbl)
        return run(bias, fns)

    def per_block_tiles():
        offs, qpos, kpos, fns = [], [], [], []
        base = 0
        for g, dil in enumerate(dils):
            sub_len = s // dil
            nq = sub_len // Q_BLOCK
            pos_t = positions.reshape(b, sub_len, dil).transpose(0, 2, 1).reshape(b * dil, sub_len)
            starts = _window_start(jnp.arange(nq), sub_len)
            win = starts[:, None] + jnp.arange(K_WINDOW)[None, :]
            qpos.append(pos_t.reshape(b * dil * nq, Q_BLOCK))
            kpos.append(pos_t[:, win].reshape(b * dil * nq, K_WINDOW))
            offs.append(jnp.tile(starts - jnp.arange(nq) * Q_BLOCK, b * dil))
            fns.append(lambda bi, r, j, nq, base=base, dil=dil: base + (bi * dil + r) * nq + j)
            base += b * dil * nq
        bias = _bias_tiles(jnp.concatenate(offs).astype(jnp.int32),
                           jnp.concatenate(qpos)[:, :, None], jnp.concatenate(kpos)[:, None, :], tbl)
        return run(bias, fns)

    return lax.cond(consecutive, shared_tiles, per_block_tiles)


def _mla_kernel(q_ref, k_ref, v_ref, o_ref, *, tk):
    q = q_ref[0, 0]
    tq = q.shape[0]
    n_kv = k_ref.shape[2] // tk

    def body(i, carry):
        m, acc = carry
        start = pl.multiple_of(i * tk, tk)
        k = k_ref[0, 0, pl.ds(start, tk), :]
        v = v_ref[0, 0, pl.ds(start, tk), :]
        s = _dot_nt(q, k)
        m_new = jnp.maximum(m, jnp.max(s, axis=1, keepdims=True))
        p = jnp.exp2(s - m_new)
        acc = jnp.exp2(m - m_new) * acc + _dot(p.astype(BF16), v)
        return m_new, acc

    m0 = jnp.full((tq, 1), NEG_INF, F32)
    _, acc = lax.fori_loop(0, n_kv, body, (m0, jnp.zeros((tq, LANES), F32)), unroll=True)
    lane = lax.broadcasted_iota(jnp.int32, (1, LANES), 1)
    denom = jnp.sum(jnp.where(lane == V_DIM_B, acc, 0.0), axis=1, keepdims=True)
    o_ref[0, 0] = jnp.where(lane < V_DIM_B, acc / denom, 0.0).astype(BF16)


def _mla(qm, km, vm, tq, tk):
    b, nh, s, _ = qm.shape
    qspec = pl.BlockSpec((1, 1, tq, LANES), lambda bi, h, i: (bi, h, i, 0))
    kvspec = pl.BlockSpec((1, 1, s, LANES), lambda bi, h, i: (bi, h, 0, 0))
    return pl.pallas_call(
        functools.partial(_mla_kernel, tk=tk),
        grid=(b, nh, s // tq),
        in_specs=[qspec, kvspec, kvspec],
        out_specs=qspec,
        out_shape=jax.ShapeDtypeStruct((b, nh, s, LANES), BF16),
        compiler_params=pltpu.CompilerParams(
            dimension_semantics=("parallel", "parallel", "arbitrary"),
            vmem_limit_bytes=VMEM_LIMIT),
        name="mla",
    )(qm, km, vm)


def _memkv_kernel(mem_ref, g_ref, w_ref, k_ref, v_ref):
    d = mem_ref.shape[2]
    kv = _dot(_rms(mem_ref[0], g_ref[...]).astype(BF16), w_ref[...])
    k_ref[0] = kv[:, :d].astype(BF16)
    v_ref[0] = kv[:, d:].astype(BF16)


def _memkv(mem, g_mem, w_mkv):
    b, m, d = mem.shape
    spec = pl.BlockSpec((1, m, d), lambda bi: (bi, 0, 0))
    return pl.pallas_call(
        _memkv_kernel,
        grid=(b,),
        in_specs=[spec, pl.BlockSpec((1, d), lambda bi: (0, 0)),
                  pl.BlockSpec((d, 2 * d), lambda bi: (0, 0))],
        out_specs=[spec, spec],
        out_shape=[jax.ShapeDtypeStruct((b, m, d), BF16)] * 2,
        compiler_params=pltpu.CompilerParams(
            dimension_semantics=("parallel",), vmem_limit_bytes=VMEM_LIMIT),
        name="memkv",
    )(mem, g_mem, w_mkv)


def _post_kernel(x_ref, o1_ref, o2_ref, o3_ref, l1_ref, l2_ref, l3_ref, ob_ref,
                 goa_ref, gob_ref, woa_ref, wob_ref, gx_ref, wmq_ref, km_ref, vm_ref, wmo_ref,
                 gmoe_ref, wr_ref, br_ref,
                 h_ref, xn_ref, logit_ref, *nat_scr):
    tm = x_ref.shape[1]
    n_chunk = WIDTH_A // LANES

    for view_ref, scr in zip((o2_ref, o3_ref, l2_ref, l3_ref), nat_scr):
        dil = view_ref.shape[2] // WIDTH_A
        for r in range(dil):
            for c in range(n_chunk):
                col = r * WIDTH_A + c * LANES
                scr[c, pl.ds(r, tm // dil, stride=dil), :] = view_ref[0, :, col:col + LANES]

    def natural(scr, rows):
        return jnp.concatenate([scr[c, rows, :] for c in range(n_chunk)], axis=1)

    rows_per_chain = tm // POST_CHAINS

    def chain(ci):
        rows = slice(ci * rows_per_chain, (ci + 1) * rows_per_chain)
        o1, o2, o3 = o1_ref[0, rows, :], natural(nat_scr[0], rows), natural(nat_scr[1], rows)
        l1, l2, l3 = l1_ref[0, rows, :], natural(nat_scr[2], rows), natural(nat_scr[3], rows)
        mx = jnp.maximum(jnp.maximum(l1, l2), l3)
        w1, w2, w3 = jnp.exp2(l1 - mx), jnp.exp2(l2 - mx), jnp.exp2(l3 - mx)
        oa = (w1 * o1 + w2 * o2 + w3 * o3) / (w1 + w2 + w3)
        oan = _rms(oa, goa_ref[...]).astype(BF16)
        ob = jnp.concatenate([ob_ref[0, h, rows, :] for h in range(N_HEADS_B)], axis=1).astype(F32)
        ms_b = jnp.sum(ob * ob, axis=1, keepdims=True) * (1.0 / (N_HEADS_B * V_DIM_B))
        obn = (ob * lax.rsqrt(ms_b + EPS) * gob_ref[...]).astype(BF16)
        yield
        h1 = x_ref[0, rows, :] + _dot(oan, woa_ref[...]) + _dot(obn, wob_ref[...])
        yield
        hn = _rms(h1, gx_ref[...]).astype(BF16)
        yield
        q = _dot(hn, wmq_ref[...]).astype(BF16)
        yield
        dh = q.shape[1] // N_HEADS_MEM
        heads = []
        for h in range(N_HEADS_MEM):
            sl = slice(h * dh, (h + 1) * dh)
            s = _dot_nt(q[:, sl], km_ref[0, :, sl])
            e = jnp.exp(s - jnp.max(s, axis=1, keepdims=True))
            p = e / jnp.sum(e, axis=1, keepdims=True)
            heads.append(_dot(p.astype(BF16), vm_ref[0, :, sl]))
        o = jnp.concatenate(heads, axis=1).astype(BF16)
        yield
        h2 = h1 + _dot(o, wmo_ref[...])
        h_ref[0, rows, :] = h2
        yield
        xn = _rms(h2, gmoe_ref[...])
        packed = _pack_bf16_pairs(xn)
        rows_per_token = packed.shape[1] // LANES
        _store_token_rows(xn_ref.at[pl.ds(ci * rows_per_chain * rows_per_token,
                                          rows_per_chain * rows_per_token)], packed)
        yield
        logit_ref[0, rows, :] = jnp.dot(xn, wr_ref[...], preferred_element_type=F32,
                                        precision=lax.Precision.HIGHEST) + br_ref[...]

    _run_skewed([chain(ci) for ci in range(POST_CHAINS)])


def _post(x, o_pats, lse_pats, ob, g_out_a, g_out_b_pad, w_o_a, w_o_b_pad, g_xattn, w_mq, kmem,
          vmem, w_mo, g_moe, w_router_pad, b_router_pad, tm):
    b, s, d = x.shape
    m = kmem.shape[1]
    full = lambda shape: pl.BlockSpec(shape, lambda bi, i: (0,) * len(shape))
    row = lambda w: pl.BlockSpec((1, tm, w), lambda bi, i: (bi, i, 0))
    memspec = pl.BlockSpec((1, m, d), lambda bi, i: (bi, 0, 0))
    hw = N_HEADS_B * LANES
    views = [pl.BlockSpec((1, tm // dil, dil * WIDTH_A), lambda bi, i: (bi, i, 0))
             for _, dil in DILATED_PATTERNS]
    return pl.pallas_call(
        _post_kernel,
        grid=(b, s // tm),
        in_specs=[row(d)] + views + views
        + [pl.BlockSpec((1, N_HEADS_B, tm, LANES), lambda bi, i: (bi, 0, i, 0)),
           full((1, WIDTH_A)), full((1, hw)), full((WIDTH_A, d)), full((hw, d)),
           full((1, d)), full((d, d)), memspec, memspec, full((d, d)),
           full((1, d)), full((d, LANES)), full((1, LANES))],
        out_specs=[row(d), pl.BlockSpec((tm * d // 2 // LANES, LANES), lambda bi, i: (bi * (s // tm) + i, 0)),
                   row(LANES)],
        out_shape=[jax.ShapeDtypeStruct((b, s, d), F32),
                   jax.ShapeDtypeStruct((b * s * d // 2 // LANES, LANES), jnp.uint32),
                   jax.ShapeDtypeStruct((b, s, LANES), F32)],
        scratch_shapes=[pltpu.VMEM((WIDTH_A // LANES, tm, LANES), F32)] * 4,
        compiler_params=pltpu.CompilerParams(
            dimension_semantics=("parallel", "parallel"), vmem_limit_bytes=VMEM_LIMIT),
        name="post",
    )(x, *o_pats, *lse_pats, ob, g_out_a, g_out_b_pad, w_o_a, w_o_b_pad, g_xattn, w_mq, kmem,
      vmem, w_mo, g_moe, w_router_pad, b_router_pad)


def _route_kernel(logit_ref, gate_ref, eidx_ref, rank_ref, slot_ref, cslot_ref, count_ref, tcnt_ref,
                  tcarry_ref, ccnt_ref, ccarry_ref, carry_ref):
    @pl.when(pl.program_id(0) == 0)
    def _():
        carry_ref[...] = jnp.zeros_like(carry_ref)

    l = logit_ref[...]
    tr = l.shape[0]
    lane_i = lax.broadcasted_iota(jnp.int32, l.shape, 1)
    lane = lane_i.astype(F32)
    vals, idxs = [], []
    for _ in range(TOP_K):
        m = jnp.max(l, axis=1, keepdims=True)
        idx = jnp.min(jnp.where(l == m, lane, float(LANES)), axis=1, keepdims=True)
        vals.append(m)
        idxs.append(idx)
        l = jnp.where(lane == idx, -jnp.inf, l)
    exps = [jnp.exp(v - vals[0]) for v in vals]
    denom = exps[0] + exps[1] + exps[2] + exps[3]
    onehot = jnp.zeros(l.shape, F32)
    for idx in idxs:
        onehot = onehot + jnp.where(lane == idx, 1.0, 0.0)
    r = lax.broadcasted_iota(jnp.int32, (tr, tr), 0)
    c = lax.broadcasted_iota(jnp.int32, (tr, tr), 1)
    tri = jnp.where(c < r, 1.0, 0.0).astype(BF16)
    local = _dot(tri, onehot.astype(BF16))
    before = local + carry_ref[...]
    tile_cnt = jnp.sum(onehot, axis=0, keepdims=True)
    li = lax.broadcasted_iota(jnp.int32, (LANES, LANES), 0)
    lj = lax.broadcasted_iota(jnp.int32, (LANES, LANES), 1)
    prefix = jnp.dot(jnp.broadcast_to(tile_cnt, (8, LANES)), jnp.where(li < lj, 1.0, 0.0),
                     preferred_element_type=F32, precision=lax.Precision.HIGHEST)[0:1]
    grouped = local + prefix
    n_sub = tr // COMBINE_TILE
    sub_cnt = [jnp.sum(onehot[u * COMBINE_TILE:(u + 1) * COMBINE_TILE], axis=0, keepdims=True)
               for u in range(n_sub)]
    sub_prefix = jnp.dot(jnp.concatenate(sub_cnt + [jnp.zeros((8 - n_sub, LANES), F32)], axis=0),
                         jnp.where(li < lj, 1.0, 0.0), preferred_element_type=F32,
                         precision=lax.Precision.HIGHEST)
    shifts, sub_base = [], jnp.zeros((1, LANES), F32)
    for u in range(n_sub):
        shifts.append(jnp.broadcast_to(sub_prefix[u:u + 1] - sub_base, (COMBINE_TILE, LANES)))
        ccnt_ref[u * 8:(u + 1) * 8, :] = jnp.broadcast_to(sub_cnt[u], (8, LANES)).astype(jnp.int32)
        ccarry_ref[u * 8:(u + 1) * 8, :] = jnp.broadcast_to(carry_ref[...] + sub_base,
                                                            (8, LANES)).astype(jnp.int32)
        sub_base = sub_base + sub_cnt[u]
    sub_grouped = local + jnp.concatenate(shifts, axis=0)
    gate = jnp.zeros(l.shape, F32)
    eidx = jnp.zeros(l.shape, jnp.int32)
    rank = jnp.zeros(l.shape, jnp.int32)
    slot = jnp.zeros(l.shape, jnp.int32)
    cslot = jnp.zeros(l.shape, jnp.int32)
    for k in range(TOP_K):
        mine = lane == idxs[k]
        rk = jnp.sum(jnp.where(mine, before, 0.0), axis=1, keepdims=True)
        sk = jnp.sum(jnp.where(mine, grouped, 0.0), axis=1, keepdims=True)
        ck = jnp.sum(jnp.where(mine, sub_grouped, 0.0), axis=1, keepdims=True)
        gate = jnp.where(lane_i == k, exps[k] / denom, gate)
        eidx = jnp.where(lane_i == k, idxs[k].astype(jnp.int32), eidx)
        rank = jnp.where(lane_i == k, rk.astype(jnp.int32), rank)
        slot = jnp.where(lane_i == k, sk.astype(jnp.int32), slot)
        cslot = jnp.where(lane_i == k, ck.astype(jnp.int32), cslot)
    gate_ref[...] = gate
    eidx_ref[...] = eidx
    rank_ref[...] = rank
    slot_ref[...] = slot
    cslot_ref[...] = cslot
    tcnt_ref[...] = jnp.broadcast_to(tile_cnt, tcnt_ref.shape).astype(jnp.int32)
    tcarry_ref[...] = jnp.broadcast_to(carry_ref[...], tcarry_ref.shape).astype(jnp.int32)
    total = carry_ref[...] + tile_cnt
    carry_ref[...] = total
    count_ref[...] = total.astype(jnp.int32)


def _route(logits, tr):
    t = logits.shape[0]
    spec = pl.BlockSpec((tr, LANES), lambda i: (i, 0))
    return pl.pallas_call(
        _route_kernel,
        grid=(t // tr,),
        in_specs=[spec],
        out_specs=[spec, spec, spec, spec, spec, pl.BlockSpec((1, LANES), lambda i: (0, 0)),
                   pl.BlockSpec((8, LANES), lambda i: (i, 0)), pl.BlockSpec((8, LANES), lambda i: (i, 0)),
                   pl.BlockSpec((8 * tr // COMBINE_TILE, LANES), lambda i: (i, 0)),
                   pl.BlockSpec((8 * tr // COMBINE_TILE, LANES), lambda i: (i, 0))],
        out_shape=[jax.ShapeDtypeStruct((t, LANES), F32)] + [jax.ShapeDtypeStruct((t, LANES), jnp.int32)] * 4
        + [jax.ShapeDtypeStruct((1, LANES), jnp.int32)]
        + [jax.ShapeDtypeStruct((t // tr * 8, LANES), jnp.int32)] * 2
        + [jax.ShapeDtypeStruct((t // COMBINE_TILE * 8, LANES), jnp.int32)] * 2,
        scratch_shapes=[pltpu.VMEM((1, LANES), F32)],
        compiler_params=pltpu.CompilerParams(
            dimension_semantics=("arbitrary",), vmem_limit_bytes=VMEM_LIMIT),
        name="route",
    )(logits)


def _dispatch_kernel(dst_ref, cnt_ref, ends_ref, x_ref, slot_ref, xs_hbm, zbuf, stage, sem, zsem):
    c = pl.program_id(0)
    buf = lax.rem(c, 2)
    n_chunk = x_ref.shape[0] // DISPATCH_CHUNK
    blk_rows = zbuf.shape[0]
    n_staged = DISPATCH_CHUNK * TOP_K

    @pl.when(c == 0)
    def _():
        zbuf[...] = jnp.zeros_like(zbuf)
        for b in range(2):
            stage[b, pl.ds(n_staged * n_chunk, DISPATCH_PIECE * n_chunk), :] = jnp.zeros(
                (DISPATCH_PIECE * n_chunk, LANES), stage.dtype)

        def zero_copy(start):
            return pltpu.make_async_copy(
                zbuf, xs_hbm.at[pl.ds(pl.multiple_of(start * n_chunk, n_chunk), blk_rows)], zsem)

        blk = blk_rows // n_chunk
        used_end = ends_ref[N_EXPERTS]
        starts = [(ends_ref[e + 1] - ends_ref[e] >= back * blk, ends_ref[e + 1] - back * blk)
                  for e in range(N_EXPERTS) for back in (1, 2)]
        starts += [((used_end + (j + 1) * blk) * n_chunk <= xs_hbm.shape[0], used_end + j * blk)
                   for j in range(N_EXPERTS)]
        for cond, start in starts:
            @pl.when(cond)
            def _():
                zero_copy(start).start()
        for cond, start in starts:
            @pl.when(cond)
            def _():
                zero_copy(start).wait()

    x_lo, x_hi = _unpack_bf16_pairs(_load_token_rows(x_ref, DISPATCH_CHUNK))
    x_lo, x_hi = x_lo.astype(BF16), x_hi.astype(BF16)
    slot_t = jnp.transpose(slot_ref[...].astype(F32))
    bits = lambda v: lax.bitcast_convert_type(v, jnp.uint32)
    for part in range(TOP_K):
        j = (lax.broadcasted_iota(jnp.int32, (DISPATCH_CHUNK, 1), 0) + part * DISPATCH_CHUNK).astype(F32)
        pick = jnp.zeros((DISPATCH_CHUNK, DISPATCH_CHUNK), F32)
        for k in range(TOP_K):
            pick = jnp.where(slot_t[k:k + 1, :] == j, 1.0, pick)
        pick = pick.astype(BF16)
        packed = bits(_dot(pick, x_hi)) | (bits(_dot(pick, x_lo)) >> 16)
        _store_token_rows(
            stage.at[buf, pl.ds(part * DISPATCH_CHUNK * n_chunk, DISPATCH_CHUNK * n_chunk)], packed)

    def piece(src_tok, dst_tok):
        rows = DISPATCH_PIECE * n_chunk
        return pltpu.make_async_copy(
            stage.at[buf, pl.ds(pl.multiple_of(src_tok * n_chunk, n_chunk), rows)],
            xs_hbm.at[pl.ds(pl.multiple_of(dst_tok * n_chunk, n_chunk), rows)], sem)

    def drain(step):
        n_pieces = 0
        for e in range(N_EXPERTS):
            n_pieces = n_pieces + (cnt_ref[step * N_EXPERTS + e] + DISPATCH_PIECE - 1) // DISPATCH_PIECE

        def wait_one(i, carry):
            piece(0, 0).wait()
            return carry
        lax.fori_loop(0, n_pieces, wait_one, 0)

    @pl.when(c > 0)
    def _():
        drain(c - 1)

    staged = 0
    for e in range(N_EXPERTS):
        cnt = cnt_ref[c * N_EXPERTS + e]
        dst = dst_ref[c * N_EXPERTS + e]
        for p in range(DISPATCH_CHUNK // DISPATCH_PIECE):
            @pl.when(p * DISPATCH_PIECE < cnt)
            def _():
                piece(staged + p * DISPATCH_PIECE, dst + p * DISPATCH_PIECE).start()
        staged = staged + cnt

    @pl.when(c == pl.num_programs(0) - 1)
    def _():
        drain(c)


def _dispatch(dst, cnt, ends0, xn_rows, slots, n_tok, n_slots):
    n_chunk = xn_rows.shape[0] // n_tok
    staged_rows = (DISPATCH_CHUNK * TOP_K + DISPATCH_PIECE) * n_chunk
    grid_spec = pltpu.PrefetchScalarGridSpec(
        num_scalar_prefetch=3,
        grid=(n_tok // DISPATCH_CHUNK,),
        in_specs=[pl.BlockSpec((DISPATCH_CHUNK * n_chunk, LANES), lambda c, *_: (c, 0)),
                  pl.BlockSpec((DISPATCH_CHUNK, LANES), lambda c, *_: (c, 0))],
        out_specs=pl.BlockSpec(memory_space=pl.ANY),
        scratch_shapes=[pltpu.VMEM((EXPERT_BLOCK * n_chunk, LANES), xn_rows.dtype),
                        pltpu.VMEM((2, staged_rows, LANES), xn_rows.dtype),
                        pltpu.SemaphoreType.DMA(()), pltpu.SemaphoreType.DMA(())],
    )
    return pl.pallas_call(
        _dispatch_kernel,
        grid_spec=grid_spec,
        out_shape=jax.ShapeDtypeStruct((n_slots * n_chunk, LANES), xn_rows.dtype),
        compiler_params=pltpu.CompilerParams(
            dimension_semantics=("arbitrary",), vmem_limit_bytes=VMEM_LIMIT),
        name="dispatch",
    )(dst, cnt, ends0, xn_rows, slots)


def _expert_kernel(be_ref, nused_ref, next_ref, xs_ref, wgu_hbm, bgu_ref, wd_hbm, bd_ref, y_ref,
                   wgu_f32, wd_f32, wgu_bf, wd_bf, sems):
    i = pl.program_id(0)

    def fetch(e):
        return (pltpu.make_async_copy(wgu_hbm.at[e], wgu_f32, sems.at[0]),
                pltpu.make_async_copy(wd_hbm.at[e], wd_f32, sems.at[1]))

    @pl.when(i == 0)
    def _():
        for copy in fetch(be_ref[0]):
            copy.start()

    @pl.when(i < nused_ref[0])
    def _():
        @pl.when(jnp.logical_or(i == 0, be_ref[i] != be_ref[jnp.maximum(i - 1, 0)]))
        def _():
            for copy in fetch(be_ref[i]):
                copy.wait()
            wgu_bf[...] = wgu_f32[...].astype(BF16)
            wd_bf[...] = wd_f32[...].astype(BF16)

            @pl.when(next_ref[i] >= 0)
            def _():
                for copy in fetch(next_ref[i]):
                    copy.start()

        x_lo, x_hi = _unpack_bf16_pairs(_load_token_rows(xs_ref, EXPERT_BLOCK))
        x = jnp.concatenate([x_lo.astype(BF16), x_hi.astype(BF16)], axis=1)
        gu = _dot(x, wgu_bf[...]) + bgu_ref[0]
        de = gu.shape[1] // 2
        gate = jnp.minimum(gu[:, :de], SWIGLU_LIMIT)
        up = jnp.clip(gu[:, de:], -SWIGLU_LIMIT, SWIGLU_LIMIT)
        hmid = (up + 1.0) * (gate * jax.nn.sigmoid(SWIGLU_ALPHA * gate))
        y = _dot(hmid.astype(BF16), wd_bf[...]) + bd_ref[0]
        _store_token_rows(y_ref, _pack_bf16_pairs(y))

    @pl.when(i >= nused_ref[0])
    def _():
        y_ref[...] = jnp.zeros_like(y_ref)


def _experts(block_expert, n_used, next_expert, xs_rows, w_gate_up, b_gate_up, w_down, b_down):
    d, de2 = w_gate_up.shape[1:]
    n_chunk = d // 2 // LANES
    blk = EXPERT_BLOCK
    n_blk = xs_rows.shape[0] // (blk * n_chunk)
    grid_spec = pltpu.PrefetchScalarGridSpec(
        num_scalar_prefetch=3,
        grid=(n_blk,),
        in_specs=[pl.BlockSpec((blk * n_chunk, LANES),
                               lambda i, be, nu, nx: (jnp.maximum(jnp.minimum(i, nu[0] - 1), 0), 0)),
                  pl.BlockSpec(memory_space=pl.ANY),
                  pl.BlockSpec((1, 1, de2), lambda i, be, nu, nx: (be[i], 0, 0)),
                  pl.BlockSpec(memory_space=pl.ANY),
                  pl.BlockSpec((1, 1, d), lambda i, be, nu, nx: (be[i], 0, 0))],
        out_specs=pl.BlockSpec((blk * n_chunk, LANES), lambda i, be, nu, nx: (i, 0)),
        scratch_shapes=[pltpu.VMEM((d, de2), F32), pltpu.VMEM((de2 // 2, d), F32),
                        pltpu.VMEM((d, de2), BF16), pltpu.VMEM((de2 // 2, d), BF16),
                        pltpu.SemaphoreType.DMA((2,))],
    )
    return pl.pallas_call(
        _expert_kernel,
        grid_spec=grid_spec,
        out_shape=jax.ShapeDtypeStruct(xs_rows.shape, xs_rows.dtype),
        compiler_params=pltpu.CompilerParams(
            dimension_semantics=("arbitrary",), vmem_limit_bytes=VMEM_LIMIT),
        name="experts",
    )(block_expert, n_used, next_expert, xs_rows, w_gate_up, b_gate_up.reshape(N_EXPERTS, 1, de2),
      w_down, b_down.reshape(N_EXPERTS, 1, d))


def _combine_kernel(start_ref, len_ref, h_ref, gate_ref, slot_ref, gfin_ref, y_hbm, out_ref, stage, sems):
    i = pl.program_id(0)
    n = pl.num_programs(0)
    buf = lax.rem(i, 2)
    tc = h_ref.shape[0]
    n_staged = tc * TOP_K
    n_chunk = stage.shape[1] // n_staged

    def issue(tile, b):
        staged = 0
        for e in range(N_EXPERTS):
            run_len = len_ref[tile * N_EXPERTS + e]
            run_start = start_ref[tile * N_EXPERTS + e]
            for bit in reversed(range(COMBINE_TILE.bit_length())):
                size = 1 << bit
                done = (run_len >> (bit + 1)) << (bit + 1)

                @pl.when((run_len >> bit) & 1 == 1)
                def _():
                    pltpu.make_async_copy(
                        y_hbm.at[pl.ds(pl.multiple_of((run_start + done) * n_chunk, n_chunk), size * n_chunk)],
                        stage.at[b, pl.ds(pl.multiple_of((staged + done) * n_chunk, n_chunk), size * n_chunk)],
                        sems.at[b]).start()
            staged = staged + run_len

    @pl.when(i == 0)
    def _():
        issue(0, 0)

    @pl.when(i + 1 < n)
    def _():
        issue(i + 1, 1 - buf)

    pltpu.make_async_copy(y_hbm.at[pl.ds(0, n_staged * n_chunk)], stage.at[buf], sems.at[buf]).wait()

    y_lo, y_hi = _unpack_bf16_pairs(_load_token_rows(stage.at[buf], n_staged))
    y_lo, y_hi = y_lo.astype(BF16), y_hi.astype(BF16)
    gate = gate_ref[...]
    slot = slot_ref[...].astype(F32)
    j = lax.broadcasted_iota(jnp.int32, (1, n_staged), 1).astype(F32)
    pick = jnp.zeros((tc, n_staged), F32)
    for k in range(TOP_K):
        pick = jnp.where(slot[:, k:k + 1] == j, gate[:, k:k + 1], pick)
    pick_hi = pick.astype(BF16)
    pick_lo = (pick - pick_hi.astype(F32)).astype(BF16)
    moe = jnp.concatenate([_dot(pick_hi, y_lo) + _dot(pick_lo, y_lo),
                           _dot(pick_hi, y_hi) + _dot(pick_lo, y_hi)], axis=1)
    out_ref[...] = _rms(h_ref[...] + moe, gfin_ref[...])


def _combine(run_start, run_len, h2, gates, slots, g_final, y_rows):
    t, d = h2.shape
    tc = COMBINE_TILE
    n_chunk = d // 2 // LANES
    grid_spec = pltpu.PrefetchScalarGridSpec(
        num_scalar_prefetch=2,
        grid=(t // tc,),
        in_specs=[pl.BlockSpec((tc, d), lambda i, *_: (i, 0)),
                  pl.BlockSpec((tc, LANES), lambda i, *_: (i, 0)),
                  pl.BlockSpec((tc, LANES), lambda i, *_: (i, 0)),
                  pl.BlockSpec((1, d), lambda i, *_: (0, 0)),
                  pl.BlockSpec(memory_space=pl.ANY)],
        out_specs=pl.BlockSpec((tc, d), lambda i, *_: (i, 0)),
        scratch_shapes=[pltpu.VMEM((2, tc * TOP_K * n_chunk, LANES), y_rows.dtype),
                        pltpu.SemaphoreType.DMA((2,))],
    )
    return pl.pallas_call(
        _combine_kernel,
        grid_spec=grid_spec,
        out_shape=jax.ShapeDtypeStruct((t, d), F32),
        compiler_params=pltpu.CompilerParams(
            dimension_semantics=("arbitrary",), vmem_limit_bytes=VMEM_LIMIT),
        name="combine",
    )(run_start, run_len, h2, gates, slots, g_final, y_rows)


def _pad_heads(w, n_heads, width, offset=0):
    k = w.shape[0]
    w = w.reshape(k, n_heads, width)
    w = jnp.pad(w, ((0, 0), (0, 0), (offset, LANES - width - offset)))
    return w.reshape(k, n_heads * LANES)


def _rot_cols(w):
    half = w.shape[-1] // 2
    return jnp.concatenate([-w[..., half:], w[..., :half]], axis=-1)


def kernel(x, mem, positions, g_mix, w_in, g_q_a, w_q_b, g_kv_a, w_kv_b, rel_bias, g_out_a, g_out_b,
           w_o, g_xattn, g_mem, w_mq, w_mkv, w_mo, g_moe, w_router, b_router, w_gate_up, b_gate_up,
           w_down, b_down, g_final):
    b, s, d = x.shape
    t = b * s
    assert g_mix.shape[0] == 1, "single-layer block: the final norm is fused into the last stage"

    inv_freq = ROPE_THETA ** (-jnp.arange(0, QK_ROPE_DIM, 2, dtype=F32) / QK_ROPE_DIM)
    ang = positions.astype(F32)[..., None] * inv_freq
    cos_sin = jnp.concatenate([jnp.cos(ang), jnp.sin(ang)], axis=-1)
    half = QK_ROPE_DIM // 2
    src = jnp.arange(QK_ROPE_DIM)[:, None]
    lane = jnp.arange(LANES)[None, :]
    on_rope = (lane >= QK_NOPE_DIM) & (lane < QK_NOPE_DIM + QK_ROPE_DIM)
    place = ((lane - QK_NOPE_DIM) % half == src % half) & on_rope
    expand = jnp.concatenate([place & (src < half), place & (src >= half)], axis=1).astype(F32)
    tbl = jnp.pad(rel_bias.T.astype(F32), ((0, 0), (0, LANES - NUM_BUCKETS)))

    h = x
    for l in range(1):
        c0 = 3 * WIDTH_A + Q_LORA_RANK + KV_LORA_RANK
        w_kpe = w_in[l][:, c0:c0 + QK_ROPE_DIM]
        place = lambda w: jnp.pad(w, ((0, 0), (QK_NOPE_DIM, LANES - QK_NOPE_DIM - QK_ROPE_DIM)))
        w_in_ext = jnp.concatenate([w_in[l][:, :c0], place(w_kpe), place(_rot_cols(w_kpe))], 1).astype(BF16)
        dq = QK_NOPE_DIM + QK_ROPE_DIM
        wq3 = w_q_b[l].reshape(Q_LORA_RANK, N_HEADS_B, dq)
        wq = _pad_heads(wq3.reshape(Q_LORA_RANK, -1), N_HEADS_B, dq).astype(BF16)
        wq_rot3 = jnp.concatenate([jnp.zeros_like(wq3[..., :QK_NOPE_DIM]), _rot_cols(wq3[..., QK_NOPE_DIM:])], -1)
        wqr = _pad_heads(wq_rot3.reshape(Q_LORA_RANK, -1), N_HEADS_B, dq).astype(BF16)
        wkv3 = w_kv_b[l].reshape(KV_LORA_RANK, N_HEADS_B, QK_NOPE_DIM + V_DIM_B)
        wk = _pad_heads(wkv3[..., :QK_NOPE_DIM].reshape(KV_LORA_RANK, -1), N_HEADS_B, QK_NOPE_DIM).astype(BF16)
        wv = _pad_heads(wkv3[..., QK_NOPE_DIM:].reshape(KV_LORA_RANK, -1), N_HEADS_B, V_DIM_B).astype(BF16)

        *views, qm, km, vm = _proj(h, cos_sin, expand, g_mix[l][None], w_in_ext, g_q_a[l][None], wq, wqr,
                                   g_kv_a[l][None], wk, wv, tm=256 * PROJ_CHAINS)
        pats = _dilated_all(views, positions, tbl)
        ob = _mla(qm, km, vm, tq=1024, tk=512)
        kmem, vmem = _memkv(mem, g_mem[l][None], w_mkv[l].astype(BF16))

        g_out_b_pad = _pad_heads(g_out_b[l][None], N_HEADS_B, V_DIM_B)
        w_o_b_pad = _pad_heads(w_o[l][WIDTH_A:].T, N_HEADS_B, V_DIM_B).T.astype(BF16)
        w_mq_s = (w_mq[l] * ((d // N_HEADS_MEM) ** -0.5)).astype(BF16)
        w_router_pad = jnp.pad(w_router[l], ((0, 0), (0, LANES - N_EXPERTS)))
        b_router_pad = jnp.pad(b_router[l][None], ((0, 0), (0, LANES - N_EXPERTS)), constant_values=NEG_INF)
        h2, xn_rows, logits = _post(h, pats[0::2], pats[1::2], ob, g_out_a[l][None],
                               g_out_b_pad, w_o[l][:WIDTH_A].astype(BF16), w_o_b_pad, g_xattn[l][None],
                               w_mq_s, kmem, vmem, w_mo[l].astype(BF16), g_moe[l][None], w_router_pad,
                               b_router_pad, tm=256 * POST_CHAINS)

        gates, eidx, rank, slots, cslots, counts, tile_cnt, tile_carry, ctile_cnt, ctile_carry = _route(
            logits.reshape(t, LANES), tr=DISPATCH_CHUNK)
        counts = counts[0, :N_EXPERTS]
        blk = EXPERT_BLOCK
        padded = jnp.where(counts > 0, (counts + DISPATCH_PIECE + blk - 1) // blk * blk, 0)
        ends = jnp.cumsum(padded)
        pad_start = ends - padded
        n_blk = t * TOP_K // blk + N_EXPERTS + -(-N_EXPERTS * DISPATCH_PIECE // blk)
        block_expert = jnp.minimum(
            jnp.sum(ends[None, :] <= (jnp.arange(n_blk) * blk)[:, None], axis=1),
            N_EXPERTS - 1).astype(jnp.int32)
        n_used = (ends[-1] // blk).astype(jnp.int32)[None]
        ends0 = jnp.concatenate([jnp.zeros((1,), jnp.int32), ends.astype(jnp.int32)])
        run_start = (pad_start[None, :] + tile_carry[::8, :N_EXPERTS]).astype(jnp.int32).reshape(-1)
        run_len = tile_cnt[::8, :N_EXPERTS].reshape(-1)
        xs_rows = _dispatch(run_start, run_len, ends0, xn_rows, slots, t, n_blk * blk)
        expert_ids = jnp.arange(N_EXPERTS, dtype=jnp.int32)
        run_end = jnp.sum(jnp.where(block_expert[:, None] == expert_ids, ends[None, :], 0), axis=1)
        next_expert = jnp.where(
            run_end < ends[-1],
            jnp.minimum(jnp.sum(ends[None, :] <= run_end[:, None], axis=1), N_EXPERTS - 1), -1).astype(jnp.int32)
        y_rows = _experts(block_expert, n_used, next_expert, xs_rows, w_gate_up[l], b_gate_up[l], w_down[l], b_down[l])
        gather_start = (pad_start[None, :] + ctile_carry[::8, :N_EXPERTS]).astype(jnp.int32).reshape(-1)
        gather_len = ctile_cnt[::8, :N_EXPERTS].reshape(-1)
        out = _combine(gather_start, gather_len, h2.reshape(t, d), gates, cslots, g_final[None], y_rows)
        h = out.reshape(b, s, d)
    return h
```

```python
import functools
import math

import jax
import jax.numpy as jnp
from jax import lax
from jax.experimental import pallas as pl
from jax.experimental.pallas import tpu as pltpu

F32 = jnp.float32
BF16 = jnp.bfloat16

LANES = 128
EPS = 1e-6
NEG_INF = -1e30
LOG2_E = math.log2(math.e)

N_HEADS_A = 8
HEAD_DIM_A = 64
WIDTH_A = N_HEADS_A * HEAD_DIM_A
DILATED_PATTERNS = ((128, 1), (512, 4), (2048, 16))
N_HEADS_B = 8
QK_NOPE_DIM = 64
QK_ROPE_DIM = 32
V_DIM_B = 64
Q_LORA_RANK = 256
KV_LORA_RANK = 128
ROPE_THETA = 10000.0
NUM_BUCKETS = 32
MAX_DISTANCE = 1024
N_HEADS_MEM = 4
N_EXPERTS = 32
TOP_K = 4
SWIGLU_LIMIT = 7.0
SWIGLU_ALPHA = 1.702

Q_BLOCK = 128
K_WINDOW = 256
HALF_WINDOW = (K_WINDOW - Q_BLOCK) // 2
MAX_Q_BLOCKS_PER_STEP = 4
EXPERT_BLOCK = 512
DISPATCH_CHUNK = 512
DISPATCH_PIECE = 64
COMBINE_TILE = 128
PROJ_CHAINS = 2
POST_CHAINS = 2
VMEM_LIMIT = 56 * 1024 * 1024


def _rms(x, g):
    return x * lax.rsqrt(jnp.mean(x * x, axis=-1, keepdims=True) + EPS) * g


def _dot(a, b):
    return jnp.dot(a, b, preferred_element_type=F32)


def _dot_nt(a, b):
    return lax.dot_general(a, b, (((1,), (1,)), ((), ())), preferred_element_type=F32)


def _run_skewed(chains):
    live = []
    pending = list(chains)
    while live or pending:
        if pending:
            live.append(pending.pop(0))
        for gen in list(live):
            try:
                next(gen)
            except StopIteration:
                live.remove(gen)


def _load_token_rows(ref, n_tok):
    n_chunk = ref.shape[0] // n_tok
    return jnp.concatenate([ref[pl.ds(c, n_tok, stride=n_chunk), :] for c in range(n_chunk)], axis=1)


def _pack_bf16_pairs(x):
    half = x.shape[1] // 2
    bits = lambda v: lax.bitcast_convert_type(v.astype(BF16).astype(F32), jnp.uint32)
    return bits(x[:, half:]) | (bits(x[:, :half]) >> 16)


def _unpack_bf16_pairs(w):
    lo = lax.bitcast_convert_type(w << 16, F32)
    hi = lax.bitcast_convert_type(w & jnp.uint32(0xFFFF0000), F32)
    return lo, hi


def _store_token_rows(ref, value):
    n_tok = value.shape[0]
    n_chunk = value.shape[1] // LANES
    for c in range(n_chunk):
        ref[pl.ds(c, n_tok, stride=n_chunk), :] = value[:, c * LANES:(c + 1) * LANES]


def _proj_kernel(x_ref, cs_ref, expand_ref, gmix_ref, win_ref, gq_ref, wq_ref, wqr_ref,
                 gkv_ref, wk_ref, wv_ref,
                 q1_ref, k1_ref, v1_ref, q4_ref, k4_ref, v4_ref, q16_ref, k16_ref, v16_ref,
                 qm_ref, km_ref, vm_ref, za_scr):
    tm = x_ref.shape[1]
    c0 = 3 * WIDTH_A
    c1 = c0 + Q_LORA_RANK + KV_LORA_RANK
    per_group = WIDTH_A // LANES
    rows_per_chain = tm // PROJ_CHAINS
    lane = lax.broadcasted_iota(jnp.int32, (1, LANES), 1)
    in_rope = jnp.logical_and(lane >= QK_NOPE_DIM, lane < QK_NOPE_DIM + QK_ROPE_DIM)
    ones_col = jnp.where(lane == V_DIM_B, 1.0, 0.0).astype(F32)
    scale = (QK_NOPE_DIM + QK_ROPE_DIM) ** -0.5 * LOG2_E

    def chain(ci):
        r0 = ci * rows_per_chain
        rows = slice(r0, r0 + rows_per_chain)
        xn = _rms(x_ref[0, rows, :], gmix_ref[...]).astype(BF16)
        yield
        z = _dot(xn, win_ref[...])
        yield
        for c in range(3 * per_group):
            chunk = z[:, c * LANES:(c + 1) * LANES]
            za_scr[c, rows, :] = chunk * (HEAD_DIM_A ** -0.5 * LOG2_E) if c < per_group else chunk
        for dil, refs in ((1, (q1_ref, k1_ref, v1_ref)), (4, (q4_ref, k4_ref, v4_ref)),
                          (16, (q16_ref, k16_ref, v16_ref))):
            n = rows_per_chain // dil
            for r in range(dil):
                for c in range(3 * per_group):
                    picked = za_scr[c, pl.ds(r0 + r, n, stride=dil), :].astype(BF16)
                    col = r * WIDTH_A + (c % per_group) * LANES
                    refs[c // per_group][0, r0 // dil:r0 // dil + n, col:col + LANES] = picked
        zqn = _rms(z[:, c0:c0 + Q_LORA_RANK], gq_ref[...]).astype(BF16)
        zkvn = _rms(z[:, c0 + Q_LORA_RANK:c1], gkv_ref[...]).astype(BF16)
        yield
        spread = jnp.dot(cs_ref[0, rows, :], expand_ref[...], preferred_element_type=F32,
                         precision=lax.Precision.HIGHEST)
        q = _dot(zqn, wq_ref[...])
        qr = _dot(zqn, wqr_ref[...])
        kn = _dot(zkvn, wk_ref[...])
        vv = _dot(zkvn, wv_ref[...])
        yield
        cos = jnp.where(in_rope, spread[:, :LANES], 1.0)
        sin = spread[:, LANES:]
        kpe = z[:, c1:c1 + LANES] * cos + z[:, c1 + LANES:c1 + 2 * LANES] * sin
        for h in range(N_HEADS_B):
            sl = slice(h * LANES, (h + 1) * LANES)
            qm_ref[0, h, rows, :] = ((q[:, sl] * cos + qr[:, sl] * sin) * scale).astype(BF16)
            km_ref[0, h, rows, :] = (kn[:, sl] + kpe).astype(BF16)
            vm_ref[0, h, rows, :] = (vv[:, sl] + ones_col).astype(BF16)

    _run_skewed([chain(ci) for ci in range(PROJ_CHAINS)])


def _proj(x, cos_sin, expand, g_mix, w_in_ext, g_q, wq, wqr, g_kv, wk, wv, tm):
    b, s, d = x.shape
    n_in = w_in_ext.shape[1]
    hw = N_HEADS_B * LANES
    full = lambda shape: pl.BlockSpec(shape, lambda bi, i: (0,) * len(shape))
    row = lambda w: pl.BlockSpec((1, tm, w), lambda bi, i: (bi, i, 0))
    head = pl.BlockSpec((1, N_HEADS_B, tm, LANES), lambda bi, i: (bi, 0, i, 0))
    dils = [dil for _, dil in DILATED_PATTERNS]
    view_specs = [pl.BlockSpec((1, tm // dil, dil * WIDTH_A), lambda bi, i: (bi, i, 0))
                  for dil in dils for _ in range(3)]
    view_shapes = [jax.ShapeDtypeStruct((b, s // dil, dil * WIDTH_A), BF16)
                   for dil in dils for _ in range(3)]
    return pl.pallas_call(
        _proj_kernel,
        grid=(b, s // tm),
        in_specs=[row(d), row(cos_sin.shape[2]), full(expand.shape), full((1, d)), full((d, n_in)),
                  full((1, Q_LORA_RANK)), full((Q_LORA_RANK, hw)), full((Q_LORA_RANK, hw)),
                  full((1, KV_LORA_RANK)), full((KV_LORA_RANK, hw)), full((KV_LORA_RANK, hw))],
        out_specs=view_specs + [head, head, head],
        out_shape=view_shapes + [jax.ShapeDtypeStruct((b, N_HEADS_B, s, LANES), BF16)] * 3,
        scratch_shapes=[pltpu.VMEM((3 * WIDTH_A // LANES, tm, LANES), F32)],
        compiler_params=pltpu.CompilerParams(
            dimension_semantics=("parallel", "parallel"), vmem_limit_bytes=VMEM_LIMIT),
        name="proj",
    )(x, cos_sin, expand, g_mix, w_in_ext, g_q, wq, wqr, g_kv, wk, wv)


_LOG_BUCKET_STARTS = tuple(
    next(n for n in range(8, 4096)
         if int(math.log(n / 8) / math.log(MAX_DISTANCE / 8) * 8) >= t)
    for t in range(1, 8))


def _rel_bucket(rel):
    n = jnp.abs(rel)
    large = jnp.full(rel.shape, NUM_BUCKETS // 4, jnp.int32)
    for start in _LOG_BUCKET_STARTS:
        large = large + jnp.where(n >= start, 1, 0)
    mag = jnp.where(n < NUM_BUCKETS // 4, n, large)
    return mag + jnp.where(rel > 0, NUM_BUCKETS // 2, 0)


def _window_start(j, sub_len):
    return jnp.clip(j * Q_BLOCK - HALF_WINDOW, 0, sub_len - K_WINDOW)


def _bias_kernel(off_ref, qpos_ref, kpos_ref, tbl_ref, out_ref):
    rel = kpos_ref[0] - qpos_ref[0]
    bucket = _rel_bucket(rel)
    delta = (off_ref[pl.program_id(0)] + lax.broadcasted_iota(jnp.int32, (1, K_WINDOW), 1)
             - lax.broadcasted_iota(jnp.int32, (Q_BLOCK, 1), 0))
    valid = jnp.abs(delta) <= HALF_WINDOW
    for h in range(N_HEADS_A):
        tbl = jnp.broadcast_to(tbl_ref[h:h + 1, :], (Q_BLOCK, LANES))
        bias = jnp.concatenate(
            [jnp.take_along_axis(tbl, bucket[:, c * LANES:(c + 1) * LANES], axis=1)
             for c in range(K_WINDOW // LANES)], axis=1)
        out_ref[0, h] = jnp.where(valid, bias * LOG2_E, NEG_INF)


def _bias_tiles(offs, qpos, kpos, tbl):
    n = offs.shape[0]
    grid_spec = pltpu.PrefetchScalarGridSpec(
        num_scalar_prefetch=1,
        grid=(n,),
        in_specs=[pl.BlockSpec((1, Q_BLOCK, 1), lambda t, off: (t, 0, 0)),
                  pl.BlockSpec((1, 1, K_WINDOW), lambda t, off: (t, 0, 0)),
                  pl.BlockSpec((N_HEADS_A, LANES), lambda t, off: (0, 0))],
        out_specs=pl.BlockSpec((1, N_HEADS_A, Q_BLOCK, K_WINDOW), lambda t, off: (t, 0, 0, 0)),
    )
    return pl.pallas_call(
        _bias_kernel,
        grid_spec=grid_spec,
        out_shape=jax.ShapeDtypeStruct((n, N_HEADS_A, Q_BLOCK, K_WINDOW), F32),
        compiler_params=pltpu.CompilerParams(
            dimension_semantics=("arbitrary",), vmem_limit_bytes=VMEM_LIMIT),
        name="bias_tiles",
    )(offs, qpos, kpos, tbl)


def _dilated_kernel(q_ref, k_ref, v_ref, *refs, sub_len, qb):
    bias_refs = refs[:qb]
    o_ref, lse_ref, s_scr, p_scr = refs[qb:]
    first = lax.broadcasted_iota(jnp.int32, (1, LANES), 1) < HEAD_DIM_A
    pair = lambda h: slice((h // 2) * LANES, (h // 2 + 1) * LANES)
    for sub in range(qb):
        j = pl.program_id(2) * qb + sub
        rows = slice(sub * Q_BLOCK, (sub + 1) * Q_BLOCK)
        kstart = pl.multiple_of(_window_start(j, sub_len), HALF_WINDOW)
        q = q_ref[0, rows, :]
        kw = k_ref[0, pl.ds(kstart, K_WINDOW), :]
        vw = v_ref[0, pl.ds(kstart, K_WINDOW), :]
        for h in range(N_HEADS_A):
            own = first if h % 2 == 0 else jnp.logical_not(first)
            qh = jnp.where(own, q[:, pair(h)], jnp.zeros_like(q[:, pair(h)]))
            s_scr[sub, h] = _dot_nt(qh, kw[:, pair(h)]) + bias_refs[sub][0, h]
        stats = []
        for h in range(N_HEADS_A):
            s = s_scr[sub, h]
            m = jnp.max(s, axis=1, keepdims=True)
            e = jnp.exp2(s - m)
            l = jnp.sum(e, axis=1, keepdims=True)
            p_scr[sub, h] = e.astype(BF16)
            stats.append((1.0 / l, m + jnp.log2(l)))
        for h in range(0, N_HEADS_A, 2):
            (r0, lse0), (r1, lse1) = stats[h], stats[h + 1]
            o0 = _dot(p_scr[sub, h], vw[:, pair(h)])
            o1 = _dot(p_scr[sub, h + 1], vw[:, pair(h)])
            o_ref[0, rows, pair(h)] = jnp.where(first, o0 * r0, o1 * r1)
            lse_ref[0, rows, pair(h)] = jnp.where(first, lse0, lse1)


def _dilated(qv, kv, vv, bias, bias_index, dil):
    b, sub_len, _ = qv.shape
    w = WIDTH_A
    nq = sub_len // Q_BLOCK
    qb = min(MAX_Q_BLOCKS_PER_STEP, nq)
    assert sub_len >= K_WINDOW and sub_len % (Q_BLOCK * qb) == 0
    qspec = pl.BlockSpec((1, Q_BLOCK * qb, w), lambda bi, r, j: (bi, j, r))
    kvspec = pl.BlockSpec((1, sub_len, w), lambda bi, r, j: (bi, 0, r))
    bspecs = [pl.BlockSpec((1, N_HEADS_A, Q_BLOCK, K_WINDOW),
                           lambda bi, r, j, sub=sub:
                           (bias_index(bi, r, j * qb + sub, nq), 0, 0, 0))
              for sub in range(qb)]
    return pl.pallas_call(
        functools.partial(_dilated_kernel, sub_len=sub_len, qb=qb),
        grid=(b, dil, nq // qb),
        in_specs=[qspec, kvspec, kvspec] + bspecs,
        out_specs=[qspec, qspec],
        out_shape=[jax.ShapeDtypeStruct((b, sub_len, dil * w), F32)] * 2,
        scratch_shapes=[pltpu.VMEM((qb, N_HEADS_A, Q_BLOCK, K_WINDOW), F32),
                        pltpu.VMEM((qb, N_HEADS_A, Q_BLOCK, K_WINDOW), BF16)],
        compiler_params=pltpu.CompilerParams(
            dimension_semantics=("parallel", "parallel", "arbitrary"),
            vmem_limit_bytes=VMEM_LIMIT),
        name=f"dilated{dil}",
    )(qv, kv, vv, *([bias] * qb))


def _dilated_all(views, positions, tbl):
    b, s = positions.shape
    dils = [dil for _, dil in DILATED_PATTERNS]
    consecutive = jnp.all(positions[:, 1:] - positions[:, :-1] == 1)

    def run(bias, index_fns):
        outs = []
        for g, dil in enumerate(dils):
            qv, kv, vv = views[3 * g:3 * g + 3]
            outs.extend(_dilated(qv, kv, vv, bias, index_fns[g], dil))
        return tuple(outs)

    def shared_tiles():
        offs, qpos, kpos, fns = [], [], [], []
        for g, dil in enumerate(dils):
            for off in (0, -HALF_WINDOW, -2 * HALF_WINDOW):
                offs.append(off)
                qpos.append(dil * jnp.arange(Q_BLOCK, dtype=jnp.int32))
                kpos.append(dil * (off + jnp.arange(K_WINDOW, dtype=jnp.int32)))
            fns.append(lambda bi, r, j, nq, g=g:
                       3 * g + jnp.where(j == 0, 0, jnp.where(j == nq - 1, 2, 1)))
        bias = _bias_tiles(jnp.array(offs, jnp.int32), jnp.stack(qpos)[:, :, None],
                           jnp.stack(kpos)[:, None, :], tbl)
        return run(bias, fns)

    def per_block_tiles():
        offs, qpos, kpos, fns = [], [], [], []
        base = 0
        for g, dil in enumerate(dils):
            sub_len = s // dil
            nq = sub_len // Q_BLOCK
            pos_t = positions.reshape(b, sub_len, dil).transpose(0, 2, 1).reshape(b * dil, sub_len)
            starts = _window_start(jnp.arange(nq), sub_len)
            win = starts[:, None] + jnp.arange(K_WINDOW)[None, :]
            qpos.append(pos_t.reshape(b * dil * nq, Q_BLOCK))
            kpos.append(pos_t[:, win].reshape(b * dil * nq, K_WINDOW))
            offs.append(jnp.tile(starts - jnp.arange(nq) * Q_BLOCK, b * dil))
            fns.append(lambda bi, r, j, nq, base=base, dil=dil: base + (bi * dil + r) * nq + j)
            base += b * dil * nq
        bias = _bias_tiles(jnp.concatenate(offs).astype(jnp.int32),
                           jnp.concatenate(qpos)[:, :, None], jnp.concatenate(kpos)[:, None, :], tbl)
        return run(bias, fns)

    return lax.cond(consecutive, shared_tiles, per_block_tiles)


def _mla_kernel(q_ref, k_ref, v_ref, o_ref, *, tk):
    q = q_ref[0, 0]
    tq = q.shape[0]
    n_kv = k_ref.shape[2] // tk

    def body(i, carry):
        m, acc = carry
        start = pl.multiple_of(i * tk, tk)
        k = k_ref[0, 0, pl.ds(start, tk), :]
        v = v_ref[0, 0, pl.ds(start, tk), :]
        s = _dot_nt(q, k)
        m_new = jnp.maximum(m, jnp.max(s, axis=1, keepdims=True))
        p = jnp.exp2(s - m_new)
        acc = jnp.exp2(m - m_new) * acc + _dot(p.astype(BF16), v)
        return m_new, acc

    m0 = jnp.full((tq, 1), NEG_INF, F32)
    _, acc = lax.fori_loop(0, n_kv, body, (m0, jnp.zeros((tq, LANES), F32)), unroll=True)
    lane = lax.broadcasted_iota(jnp.int32, (1, LANES), 1)
    denom = jnp.sum(jnp.where(lane == V_DIM_B, acc, 0.0), axis=1, keepdims=True)
    o_ref[0, 0] = jnp.where(lane < V_DIM_B, acc / denom, 0.0).astype(BF16)


def _mla(qm, km, vm, tq, tk):
    b, nh, s, _ = qm.shape
    qspec = pl.BlockSpec((1, 1, tq, LANES), lambda bi, h, i: (bi, h, i, 0))
    kvspec = pl.BlockSpec((1, 1, s, LANES), lambda bi, h, i: (bi, h, 0, 0))
    return pl.pallas_call(
        functools.partial(_mla_kernel, tk=tk),
        grid=(b, nh, s // tq),
        in_specs=[qspec, kvspec, kvspec],
        out_specs=qspec,
        out_shape=jax.ShapeDtypeStruct((b, nh, s, LANES), BF16),
        compiler_params=pltpu.CompilerParams(
            dimension_semantics=("parallel", "parallel", "arbitrary"),
            vmem_limit_bytes=VMEM_LIMIT),
        name="mla",
    )(qm, km, vm)


def _memkv_kernel(mem_ref, g_ref, w_ref, k_ref, v_ref):
    d = mem_ref.shape[2]
    kv = _dot(_rms(mem_ref[0], g_ref[...]).astype(BF16), w_ref[...])
    k_ref[0] = kv[:, :d].astype(BF16)
    v_ref[0] = kv[:, d:].astype(BF16)


def _memkv(mem, g_mem, w_mkv):
    b, m, d = mem.shape
    spec = pl.BlockSpec((1, m, d), lambda bi: (bi, 0, 0))
    return pl.pallas_call(
        _memkv_kernel,
        grid=(b,),
        in_specs=[spec, pl.BlockSpec((1, d), lambda bi: (0, 0)),
                  pl.BlockSpec((d, 2 * d), lambda bi: (0, 0))],
        out_specs=[spec, spec],
        out_shape=[jax.ShapeDtypeStruct((b, m, d), BF16)] * 2,
        compiler_params=pltpu.CompilerParams(
            dimension_semantics=("parallel",), vmem_limit_bytes=VMEM_LIMIT),
        name="memkv",
    )(mem, g_mem, w_mkv)


def _post_kernel(x_ref, o1_ref, o2_ref, o3_ref, l1_ref, l2_ref, l3_ref, ob_ref,
                 goa_ref, gob_ref, woa_ref, wob_ref, gx_ref, wmq_ref, km_ref, vm_ref, wmo_ref,
                 gmoe_ref, wr_ref, br_ref,
                 h_ref, xn_ref, logit_ref, *nat_scr):
    tm = x_ref.shape[1]
    n_chunk = WIDTH_A // LANES

    for view_ref, scr in zip((o2_ref, o3_ref, l2_ref, l3_ref), nat_scr):
        dil = view_ref.shape[2] // WIDTH_A
        for r in range(dil):
            for c in range(n_chunk):
                col = r * WIDTH_A + c * LANES
                scr[c, pl.ds(r, tm // dil, stride=dil), :] = view_ref[0, :, col:col + LANES]

    def natural(scr, rows):
        return jnp.concatenate([scr[c, rows, :] for c in range(n_chunk)], axis=1)

    rows_per_chain = tm // POST_CHAINS

    def chain(ci):
        rows = slice(ci * rows_per_chain, (ci + 1) * rows_per_chain)
        o1, o2, o3 = o1_ref[0, rows, :], natural(nat_scr[0], rows), natural(nat_scr[1], rows)
        l1, l2, l3 = l1_ref[0, rows, :], natural(nat_scr[2], rows), natural(nat_scr[3], rows)
        mx = jnp.maximum(jnp.maximum(l1, l2), l3)
        w1, w2, w3 = jnp.exp2(l1 - mx), jnp.exp2(l2 - mx), jnp.exp2(l3 - mx)
        oa = (w1 * o1 + w2 * o2 + w3 * o3) / (w1 + w2 + w3)
        oan = _rms(oa, goa_ref[...]).astype(BF16)
        ob = jnp.concatenate([ob_ref[0, h, rows, :] for h in range(N_HEADS_B)], axis=1).astype(F32)
        ms_b = jnp.sum(ob * ob, axis=1, keepdims=True) * (1.0 / (N_HEADS_B * V_DIM_B))
        obn = (ob * lax.rsqrt(ms_b + EPS) * gob_ref[...]).astype(BF16)
        yield
        h1 = x_ref[0, rows, :] + _dot(oan, woa_ref[...]) + _dot(obn, wob_ref[...])
        yield
        hn = _rms(h1, gx_ref[...]).astype(BF16)
        yield
        q = _dot(hn, wmq_ref[...]).astype(BF16)
        yield
        dh = q.shape[1] // N_HEADS_MEM
        heads = []
        for h in range(N_HEADS_MEM):
            sl = slice(h * dh, (h + 1) * dh)
            s = _dot_nt(q[:, sl], km_ref[0, :, sl])
            yield
            e = jnp.exp(s - jnp.max(s, axis=1, keepdims=True))
            p = e / jnp.sum(e, axis=1, keepdims=True)
            yield
            heads.append(_dot(p.astype(BF16), vm_ref[0, :, sl]))
        o = jnp.concatenate(heads, axis=1).astype(BF16)
        yield
        h2 = h1 + _dot(o, wmo_ref[...])
        h_ref[0, rows, :] = h2
        yield
        xn = _rms(h2, gmoe_ref[...])
        packed = _pack_bf16_pairs(xn)
        rows_per_token = packed.shape[1] // LANES
        _store_token_rows(xn_ref.at[pl.ds(ci * rows_per_chain * rows_per_token,
                                          rows_per_chain * rows_per_token)], packed)
        yield
        logit_ref[0, rows, :] = jnp.dot(xn, wr_ref[...], preferred_element_type=F32,
                                        precision=lax.Precision.HIGHEST) + br_ref[...]

    _run_skewed([chain(ci) for ci in range(POST_CHAINS)])


def _post(x, o_pats, lse_pats, ob, g_out_a, g_out_b_pad, w_o_a, w_o_b_pad, g_xattn, w_mq, kmem,
          vmem, w_mo, g_moe, w_router_pad, b_router_pad, tm):
    b, s, d = x.shape
    m = kmem.shape[1]
    full = lambda shape: pl.BlockSpec(shape, lambda bi, i: (0,) * len(shape))
    row = lambda w: pl.BlockSpec((1, tm, w), lambda bi, i: (bi, i, 0))
    memspec = pl.BlockSpec((1, m, d), lambda bi, i: (bi, 0, 0))
    hw = N_HEADS_B * LANES
    views = [pl.BlockSpec((1, tm // dil, dil * WIDTH_A), lambda bi, i: (bi, i, 0))
             for _, dil in DILATED_PATTERNS]
    return pl.pallas_call(
        _post_kernel,
        grid=(b, s // tm),
        in_specs=[row(d)] + views + views
        + [pl.BlockSpec((1, N_HEADS_B, tm, LANES), lambda bi, i: (bi, 0, i, 0)),
           full((1, WIDTH_A)), full((1, hw)), full((WIDTH_A, d)), full((hw, d)),
           full((1, d)), full((d, d)), memspec, memspec, full((d, d)),
           full((1, d)), full((d, LANES)), full((1, LANES))],
        out_specs=[row(d), pl.BlockSpec((tm * d // 2 // LANES, LANES), lambda bi, i: (bi * (s // tm) + i, 0)),
                   row(LANES)],
        out_shape=[jax.ShapeDtypeStruct((b, s, d), F32),
                   jax.ShapeDtypeStruct((b * s * d // 2 // LANES, LANES), jnp.uint32),
                   jax.ShapeDtypeStruct((b, s, LANES), F32)],
        scratch_shapes=[pltpu.VMEM((WIDTH_A // LANES, tm, LANES), F32)] * 4,
        compiler_params=pltpu.CompilerParams(
            dimension_semantics=("parallel", "parallel"), vmem_limit_bytes=VMEM_LIMIT),
        name="post",
    )(x, *o_pats, *lse_pats, ob, g_out_a, g_out_b_pad, w_o_a, w_o_b_pad, g_xattn, w_mq, kmem,
      vmem, w_mo, g_moe, w_router_pad, b_router_pad)


def _route_kernel(logit_ref, gate_ref, slot_ref, cslot_ref, count_ref, tcnt_ref, tcarry_ref, ccnt_ref,
                  ccarry_ref, carry_ref):
    @pl.when(pl.program_id(0) == 0)
    def _():
        carry_ref[...] = jnp.zeros_like(carry_ref)

    l = logit_ref[...]
    tr = l.shape[0]
    lane_i = lax.broadcasted_iota(jnp.int32, l.shape, 1)
    lane = lane_i.astype(F32)
    vals, idxs = [], []
    for _ in range(TOP_K):
        m = jnp.max(l, axis=1, keepdims=True)
        idx = jnp.min(jnp.where(l == m, lane, float(LANES)), axis=1, keepdims=True)
        vals.append(m)
        idxs.append(idx)
        l = jnp.where(lane == idx, -jnp.inf, l)
    exps = [jnp.exp(v - vals[0]) for v in vals]
    denom = exps[0] + exps[1] + exps[2] + exps[3]
    onehot = jnp.zeros(l.shape, F32)
    for idx in idxs:
        onehot = onehot + jnp.where(lane == idx, 1.0, 0.0)
    r = lax.broadcasted_iota(jnp.int32, (tr, tr), 0)
    c = lax.broadcasted_iota(jnp.int32, (tr, tr), 1)
    tri = jnp.where(c < r, 1.0, 0.0).astype(BF16)
    local = _dot(tri, onehot.astype(BF16))
    tile_cnt = jnp.sum(onehot, axis=0, keepdims=True)
    li = lax.broadcasted_iota(jnp.int32, (LANES, LANES), 0)
    lj = lax.broadcasted_iota(jnp.int32, (LANES, LANES), 1)
    prefix = jnp.dot(jnp.broadcast_to(tile_cnt, (8, LANES)), jnp.where(li < lj, 1.0, 0.0),
                     preferred_element_type=F32, precision=lax.Precision.HIGHEST)[0:1]
    grouped = local + prefix
    n_sub = tr // COMBINE_TILE
    sub_cnt = [jnp.sum(onehot[u * COMBINE_TILE:(u + 1) * COMBINE_TILE], axis=0, keepdims=True)
               for u in range(n_sub)]
    sub_prefix = jnp.dot(jnp.concatenate(sub_cnt + [jnp.zeros((8 - n_sub, LANES), F32)], axis=0),
                         jnp.where(li < lj, 1.0, 0.0), preferred_element_type=F32,
                         precision=lax.Precision.HIGHEST)
    shifts, sub_base = [], jnp.zeros((1, LANES), F32)
    for u in range(n_sub):
        shifts.append(jnp.broadcast_to(sub_prefix[u:u + 1] - sub_base, (COMBINE_TILE, LANES)))
        ccnt_ref[u * 8:(u + 1) * 8, :] = jnp.broadcast_to(sub_cnt[u], (8, LANES)).astype(jnp.int32)
        ccarry_ref[u * 8:(u + 1) * 8, :] = jnp.broadcast_to(carry_ref[...] + sub_base,
                                                            (8, LANES)).astype(jnp.int32)
        sub_base = sub_base + sub_cnt[u]
    sub_grouped = local + jnp.concatenate(shifts, axis=0)
    gate = jnp.zeros(l.shape, F32)
    slot = jnp.zeros(l.shape, jnp.int32)
    cslot = jnp.zeros(l.shape, jnp.int32)
    for k in range(TOP_K):
        mine = lane == idxs[k]
        sk = jnp.sum(jnp.where(mine, grouped, 0.0), axis=1, keepdims=True)
        ck = jnp.sum(jnp.where(mine, sub_grouped, 0.0), axis=1, keepdims=True)
        gate = jnp.where(lane_i == k, exps[k] / denom, gate)
        slot = jnp.where(lane_i == k, sk.astype(jnp.int32), slot)
        cslot = jnp.where(lane_i == k, ck.astype(jnp.int32), cslot)
    gate_ref[...] = gate
    slot_ref[...] = slot
    cslot_ref[...] = cslot
    tcnt_ref[...] = jnp.broadcast_to(tile_cnt, tcnt_ref.shape).astype(jnp.int32)
    tcarry_ref[...] = jnp.broadcast_to(carry_ref[...], tcarry_ref.shape).astype(jnp.int32)
    total = carry_ref[...] + tile_cnt
    carry_ref[...] = total
    count_ref[...] = total.astype(jnp.int32)


def _route(logits, tr):
    t = logits.shape[0]
    spec = pl.BlockSpec((tr, LANES), lambda i: (i, 0))
    return pl.pallas_call(
        _route_kernel,
        grid=(t // tr,),
        in_specs=[spec],
        out_specs=[spec, spec, spec, pl.BlockSpec((1, LANES), lambda i: (0, 0)),
                   pl.BlockSpec((8, LANES), lambda i: (i, 0)), pl.BlockSpec((8, LANES), lambda i: (i, 0)),
                   pl.BlockSpec((8 * tr // COMBINE_TILE, LANES), lambda i: (i, 0)),
                   pl.BlockSpec((8 * tr // COMBINE_TILE, LANES), lambda i: (i, 0))],
        out_shape=[jax.ShapeDtypeStruct((t, LANES), F32)] + [jax.ShapeDtypeStruct((t, LANES), jnp.int32)] * 2
        + [jax.ShapeDtypeStruct((1, LANES), jnp.int32)]
        + [jax.ShapeDtypeStruct((t // tr * 8, LANES), jnp.int32)] * 2
        + [jax.ShapeDtypeStruct((t // COMBINE_TILE * 8, LANES), jnp.int32)] * 2,
        scratch_shapes=[pltpu.VMEM((1, LANES), F32)],
        compiler_params=pltpu.CompilerParams(
            dimension_semantics=("arbitrary",), vmem_limit_bytes=VMEM_LIMIT),
        name="route",
    )(logits)


def _dispatch_kernel(dst_ref, cnt_ref, ends_ref, x_ref, slot_ref, xs_hbm, zbuf, stage, sem, zsem):
    c = pl.program_id(0)
    buf = lax.rem(c, 2)
    n_chunk = x_ref.shape[0] // DISPATCH_CHUNK
    blk_rows = zbuf.shape[0]
    n_staged = DISPATCH_CHUNK * TOP_K

    @pl.when(c == 0)
    def _():
        zbuf[...] = jnp.zeros_like(zbuf)
        for b in range(2):
            stage[b, pl.ds(n_staged * n_chunk, DISPATCH_PIECE * n_chunk), :] = jnp.zeros(
                (DISPATCH_PIECE * n_chunk, LANES), stage.dtype)

        def zero_copy(start):
            return pltpu.make_async_copy(
                zbuf, xs_hbm.at[pl.ds(pl.multiple_of(start * n_chunk, n_chunk), blk_rows)], zsem)

        blk = blk_rows // n_chunk
        used_end = ends_ref[N_EXPERTS]
        starts = [(ends_ref[e + 1] - ends_ref[e] >= back * blk, ends_ref[e + 1] - back * blk)
                  for e in range(N_EXPERTS) for back in (1, 2)]
        starts += [((used_end + (j + 1) * blk) * n_chunk <= xs_hbm.shape[0], used_end + j * blk)
                   for j in range(N_EXPERTS)]
        for cond, start in starts:
            @pl.when(cond)
            def _():
                zero_copy(start).start()
        for cond, start in starts:
            @pl.when(cond)
            def _():
                zero_copy(start).wait()

    x_lo, x_hi = _unpack_bf16_pairs(_load_token_rows(x_ref, DISPATCH_CHUNK))
    x_lo, x_hi = x_lo.astype(BF16), x_hi.astype(BF16)
    slot_t = jnp.transpose(slot_ref[...].astype(F32))
    bits = lambda v: lax.bitcast_convert_type(v, jnp.uint32)
    for part in range(TOP_K):
        j = (lax.broadcasted_iota(jnp.int32, (DISPATCH_CHUNK, 1), 0) + part * DISPATCH_CHUNK).astype(F32)
        pick = jnp.zeros((DISPATCH_CHUNK, DISPATCH_CHUNK), F32)
        for k in range(TOP_K):
            pick = jnp.where(slot_t[k:k + 1, :] == j, 1.0, pick)
        pick = pick.astype(BF16)
        packed = bits(_dot(pick, x_hi)) | (bits(_dot(pick, x_lo)) >> 16)
        _store_token_rows(
            stage.at[buf, pl.ds(part * DISPATCH_CHUNK * n_chunk, DISPATCH_CHUNK * n_chunk)], packed)

    def piece(src_tok, dst_tok):
        rows = DISPATCH_PIECE * n_chunk
        return pltpu.make_async_copy(
            stage.at[buf, pl.ds(pl.multiple_of(src_tok * n_chunk, n_chunk), rows)],
            xs_hbm.at[pl.ds(pl.multiple_of(dst_tok * n_chunk, n_chunk), rows)], sem)

    def drain(step):
        n_pieces = 0
        for e in range(N_EXPERTS):
            n_pieces = n_pieces + (cnt_ref[step * N_EXPERTS + e] + DISPATCH_PIECE - 1) // DISPATCH_PIECE

        def wait_one(i, carry):
            piece(0, 0).wait()
            return carry
        lax.fori_loop(0, n_pieces, wait_one, 0)

    @pl.when(c > 0)
    def _():
        drain(c - 1)

    staged = 0
    for e in range(N_EXPERTS):
        cnt = cnt_ref[c * N_EXPERTS + e]
        dst = dst_ref[c * N_EXPERTS + e]
        for p in range(DISPATCH_CHUNK // DISPATCH_PIECE):
            @pl.when(p * DISPATCH_PIECE < cnt)
            def _():
                piece(staged + p * DISPATCH_PIECE, dst + p * DISPATCH_PIECE).start()
        staged = staged + cnt

    @pl.when(c == pl.num_programs(0) - 1)
    def _():
        drain(c)


def _dispatch(dst, cnt, ends0, xn_rows, slots, n_tok, n_slots):
    n_chunk = xn_rows.shape[0] // n_tok
    staged_rows = (DISPATCH_CHUNK * TOP_K + DISPATCH_PIECE) * n_chunk
    grid_spec = pltpu.PrefetchScalarGridSpec(
        num_scalar_prefetch=3,
        grid=(n_tok // DISPATCH_CHUNK,),
        in_specs=[pl.BlockSpec((DISPATCH_CHUNK * n_chunk, LANES), lambda c, *_: (c, 0)),
                  pl.BlockSpec((DISPATCH_CHUNK, LANES), lambda c, *_: (c, 0))],
        out_specs=pl.BlockSpec(memory_space=pl.ANY),
        scratch_shapes=[pltpu.VMEM((EXPERT_BLOCK * n_chunk, LANES), xn_rows.dtype),
                        pltpu.VMEM((2, staged_rows, LANES), xn_rows.dtype),
                        pltpu.SemaphoreType.DMA(()), pltpu.SemaphoreType.DMA(())],
    )
    return pl.pallas_call(
        _dispatch_kernel,
        grid_spec=grid_spec,
        out_shape=jax.ShapeDtypeStruct((n_slots * n_chunk, LANES), xn_rows.dtype),
        compiler_params=pltpu.CompilerParams(
            dimension_semantics=("arbitrary",), vmem_limit_bytes=VMEM_LIMIT),
        name="dispatch",
    )(dst, cnt, ends0, xn_rows, slots)


def _expert_kernel(be_ref, nused_ref, next_ref, xs_ref, wgu_hbm, bgu_ref, wd_hbm, bd_ref, y_ref,
                   wgu_f32, wd_f32, wgu_bf, wd_bf, sems):
    i = pl.program_id(0)

    def fetch(e):
        return (pltpu.make_async_copy(wgu_hbm.at[e], wgu_f32, sems.at[0]),
                pltpu.make_async_copy(wd_hbm.at[e], wd_f32, sems.at[1]))

    @pl.when(i == 0)
    def _():
        for copy in fetch(be_ref[0]):
            copy.start()

    @pl.when(i < nused_ref[0])
    def _():
        @pl.when(jnp.logical_or(i == 0, be_ref[i] != be_ref[jnp.maximum(i - 1, 0)]))
        def _():
            for copy in fetch(be_ref[i]):
                copy.wait()
            wgu_bf[...] = wgu_f32[...].astype(BF16)
            wd_bf[...] = wd_f32[...].astype(BF16)

            @pl.when(next_ref[i] >= 0)
            def _():
                for copy in fetch(next_ref[i]):
                    copy.start()

        x_lo, x_hi = _unpack_bf16_pairs(_load_token_rows(xs_ref, EXPERT_BLOCK))
        x = jnp.concatenate([x_lo.astype(BF16), x_hi.astype(BF16)], axis=1)
        gu = _dot(x, wgu_bf[...]) + bgu_ref[0]
        de = gu.shape[1] // 2
        gate = jnp.minimum(gu[:, :de], SWIGLU_LIMIT)
        up = jnp.clip(gu[:, de:], -SWIGLU_LIMIT, SWIGLU_LIMIT)
        hmid = (up + 1.0) * (gate * jax.nn.sigmoid(SWIGLU_ALPHA * gate))
        y = _dot(hmid.astype(BF16), wd_bf[...]) + bd_ref[0]
        _store_token_rows(y_ref, _pack_bf16_pairs(y))

    @pl.when(i >= nused_ref[0])
    def _():
        y_ref[...] = jnp.zeros_like(y_ref)


def _experts(block_expert, n_used, next_expert, xs_rows, w_gate_up, b_gate_up, w_down, b_down):
    d, de2 = w_gate_up.shape[1:]
    n_chunk = d // 2 // LANES
    blk = EXPERT_BLOCK
    n_blk = xs_rows.shape[0] // (blk * n_chunk)
    grid_spec = pltpu.PrefetchScalarGridSpec(
        num_scalar_prefetch=3,
        grid=(n_blk,),
        in_specs=[pl.BlockSpec((blk * n_chunk, LANES),
                               lambda i, be, nu, nx: (jnp.maximum(jnp.minimum(i, nu[0] - 1), 0), 0)),
                  pl.BlockSpec(memory_space=pl.ANY),
                  pl.BlockSpec((1, 1, de2), lambda i, be, nu, nx: (be[i], 0, 0)),
                  pl.BlockSpec(memory_space=pl.ANY),
                  pl.BlockSpec((1, 1, d), lambda i, be, nu, nx: (be[i], 0, 0))],
        out_specs=pl.BlockSpec((blk * n_chunk, LANES), lambda i, be, nu, nx: (i, 0)),
        scratch_shapes=[pltpu.VMEM((d, de2), F32), pltpu.VMEM((de2 // 2, d), F32),
                        pltpu.VMEM((d, de2), BF16), pltpu.VMEM((de2 // 2, d), BF16),
                        pltpu.SemaphoreType.DMA((2,))],
    )
    return pl.pallas_call(
        _expert_kernel,
        grid_spec=grid_spec,
        out_shape=jax.ShapeDtypeStruct(xs_rows.shape, xs_rows.dtype),
        compiler_params=pltpu.CompilerParams(
            dimension_semantics=("arbitrary",), vmem_limit_bytes=VMEM_LIMIT),
        name="experts",
    )(block_expert, n_used, next_expert, xs_rows, w_gate_up, b_gate_up.reshape(N_EXPERTS, 1, de2),
      w_down, b_down.reshape(N_EXPERTS, 1, d))


def _combine_kernel(start_ref, len_ref, h_ref, gate_ref, slot_ref, gfin_ref, y_hbm, out_ref, stage, sems):
    i = pl.program_id(0)
    n = pl.num_programs(0)
    buf = lax.rem(i, 2)
    tc = h_ref.shape[0]
    n_staged = tc * TOP_K
    n_chunk = stage.shape[1] // n_staged

    def issue(tile, b):
        staged = 0
        for e in range(N_EXPERTS):
            run_len = len_ref[tile * N_EXPERTS + e]
            run_start = start_ref[tile * N_EXPERTS + e]
            for bit in reversed(range(COMBINE_TILE.bit_length())):
                size = 1 << bit
                done = (run_len >> (bit + 1)) << (bit + 1)

                @pl.when((run_len >> bit) & 1 == 1)
                def _():
                    pltpu.make_async_copy(
                        y_hbm.at[pl.ds(pl.multiple_of((run_start + done) * n_chunk, n_chunk), size * n_chunk)],
                        stage.at[b, pl.ds(pl.multiple_of((staged + done) * n_chunk, n_chunk), size * n_chunk)],
                        sems.at[b]).start()
            staged = staged + run_len

    @pl.when(i == 0)
    def _():
        issue(0, 0)

    @pl.when(i + 1 < n)
    def _():
        issue(i + 1, 1 - buf)

    pltpu.make_async_copy(y_hbm.at[pl.ds(0, n_staged * n_chunk)], stage.at[buf], sems.at[buf]).wait()

    y_lo, y_hi = _unpack_bf16_pairs(_load_token_rows(stage.at[buf], n_staged))
    y_lo, y_hi = y_lo.astype(BF16), y_hi.astype(BF16)
    gate = gate_ref[...]
    slot = slot_ref[...].astype(F32)
    j = lax.broadcasted_iota(jnp.int32, (1, n_staged), 1).astype(F32)
    pick = jnp.zeros((tc, n_staged), F32)
    for k in range(TOP_K):
        pick = jnp.where(slot[:, k:k + 1] == j, gate[:, k:k + 1], pick)
    pick_hi = pick.astype(BF16)
    pick_lo = (pick - pick_hi.astype(F32)).astype(BF16)
    moe = jnp.concatenate([_dot(pick_hi, y_lo) + _dot(pick_lo, y_lo),
                           _dot(pick_hi, y_hi) + _dot(pick_lo, y_hi)], axis=1)
    out_ref[...] = _rms(h_ref[...] + moe, gfin_ref[...])


def _combine(run_start, run_len, h2, gates, slots, g_final, y_rows):
    t, d = h2.shape
    tc = COMBINE_TILE
    n_chunk = d // 2 // LANES
    grid_spec = pltpu.PrefetchScalarGridSpec(
        num_scalar_prefetch=2,
        grid=(t // tc,),
        in_specs=[pl.BlockSpec((tc, d), lambda i, *_: (i, 0)),
                  pl.BlockSpec((tc, LANES), lambda i, *_: (i, 0)),
                  pl.BlockSpec((tc, LANES), lambda i, *_: (i, 0)),
                  pl.BlockSpec((1, d), lambda i, *_: (0, 0)),
                  pl.BlockSpec(memory_space=pl.ANY)],
        out_specs=pl.BlockSpec((tc, d), lambda i, *_: (i, 0)),
        scratch_shapes=[pltpu.VMEM((2, tc * TOP_K * n_chunk, LANES), y_rows.dtype),
                        pltpu.SemaphoreType.DMA((2,))],
    )
    return pl.pallas_call(
        _combine_kernel,
        grid_spec=grid_spec,
        out_shape=jax.ShapeDtypeStruct((t, d), F32),
        compiler_params=pltpu.CompilerParams(
            dimension_semantics=("arbitrary",), vmem_limit_bytes=VMEM_LIMIT),
        name="combine",
    )(run_start, run_len, h2, gates, slots, g_final, y_rows)


def _pad_heads(w, n_heads, width, offset=0):
    k = w.shape[0]
    w = w.reshape(k, n_heads, width)
    w = jnp.pad(w, ((0, 0), (0, 0), (offset, LANES - width - offset)))
    return w.reshape(k, n_heads * LANES)


def _rot_cols(w):
    half = w.shape[-1] // 2
    return jnp.concatenate([-w[..., half:], w[..., :half]], axis=-1)


def kernel(x, mem, positions, g_mix, w_in, g_q_a, w_q_b, g_kv_a, w_kv_b, rel_bias, g_out_a, g_out_b,
           w_o, g_xattn, g_mem, w_mq, w_mkv, w_mo, g_moe, w_router, b_router, w_gate_up, b_gate_up,
           w_down, b_down, g_final):
    b, s, d = x.shape
    t = b * s
    assert g_mix.shape[0] == 1, "single-layer block: the final norm is fused into the last stage"

    inv_freq = ROPE_THETA ** (-jnp.arange(0, QK_ROPE_DIM, 2, dtype=F32) / QK_ROPE_DIM)
    ang = positions.astype(F32)[..., None] * inv_freq
    cos_sin = jnp.concatenate([jnp.cos(ang), jnp.sin(ang)], axis=-1)
    half = QK_ROPE_DIM // 2
    src = jnp.arange(QK_ROPE_DIM)[:, None]
    lane = jnp.arange(LANES)[None, :]
    on_rope = (lane >= QK_NOPE_DIM) & (lane < QK_NOPE_DIM + QK_ROPE_DIM)
    place = ((lane - QK_NOPE_DIM) % half == src % half) & on_rope
    expand = jnp.concatenate([place & (src < half), place & (src >= half)], axis=1).astype(F32)
    tbl = jnp.pad(rel_bias.T.astype(F32), ((0, 0), (0, LANES - NUM_BUCKETS)))

    h = x
    for l in range(1):
        c0 = 3 * WIDTH_A + Q_LORA_RANK + KV_LORA_RANK
        w_kpe = w_in[l][:, c0:c0 + QK_ROPE_DIM]
        place = lambda w: jnp.pad(w, ((0, 0), (QK_NOPE_DIM, LANES - QK_NOPE_DIM - QK_ROPE_DIM)))
        w_in_ext = jnp.concatenate([w_in[l][:, :c0], place(w_kpe), place(_rot_cols(w_kpe))], 1).astype(BF16)
        dq = QK_NOPE_DIM + QK_ROPE_DIM
        wq3 = w_q_b[l].reshape(Q_LORA_RANK, N_HEADS_B, dq)
        wq = _pad_heads(wq3.reshape(Q_LORA_RANK, -1), N_HEADS_B, dq).astype(BF16)
        wq_rot3 = jnp.concatenate([jnp.zeros_like(wq3[..., :QK_NOPE_DIM]), _rot_cols(wq3[..., QK_NOPE_DIM:])], -1)
        wqr = _pad_heads(wq_rot3.reshape(Q_LORA_RANK, -1), N_HEADS_B, dq).astype(BF16)
        wkv3 = w_kv_b[l].reshape(KV_LORA_RANK, N_HEADS_B, QK_NOPE_DIM + V_DIM_B)
        wk = _pad_heads(wkv3[..., :QK_NOPE_DIM].reshape(KV_LORA_RANK, -1), N_HEADS_B, QK_NOPE_DIM).astype(BF16)
        wv = _pad_heads(wkv3[..., QK_NOPE_DIM:].reshape(KV_LORA_RANK, -1), N_HEADS_B, V_DIM_B).astype(BF16)

        *views, qm, km, vm = _proj(h, cos_sin, expand, g_mix[l][None], w_in_ext, g_q_a[l][None], wq, wqr,
                                   g_kv_a[l][None], wk, wv, tm=256 * PROJ_CHAINS)
        pats = _dilated_all(views, positions, tbl)
        ob = _mla(qm, km, vm, tq=1024, tk=512)
        kmem, vmem = _memkv(mem, g_mem[l][None], w_mkv[l].astype(BF16))

        g_out_b_pad = _pad_heads(g_out_b[l][None], N_HEADS_B, V_DIM_B)
        w_o_b_pad = _pad_heads(w_o[l][WIDTH_A:].T, N_HEADS_B, V_DIM_B).T.astype(BF16)
        w_mq_s = (w_mq[l] * ((d // N_HEADS_MEM) ** -0.5)).astype(BF16)
        w_router_pad = jnp.pad(w_router[l], ((0, 0), (0, LANES - N_EXPERTS)))
        b_router_pad = jnp.pad(b_router[l][None], ((0, 0), (0, LANES - N_EXPERTS)), constant_values=NEG_INF)
        h2, xn_rows, logits = _post(h, pats[0::2], pats[1::2], ob, g_out_a[l][None],
                               g_out_b_pad, w_o[l][:WIDTH_A].astype(BF16), w_o_b_pad, g_xattn[l][None],
                               w_mq_s, kmem, vmem, w_mo[l].astype(BF16), g_moe[l][None], w_router_pad,
                               b_router_pad, tm=256 * POST_CHAINS)

        gates, slots, cslots, counts, tile_cnt, tile_carry, ctile_cnt, ctile_carry = _route(
            logits.reshape(t, LANES), tr=DISPATCH_CHUNK)
        counts = counts[0, :N_EXPERTS]
        blk = EXPERT_BLOCK
        padded = jnp.where(counts > 0, (counts + DISPATCH_PIECE + blk - 1) // blk * blk, 0)
        ends = jnp.cumsum(padded)
        pad_start = ends - padded
        n_blk = t * TOP_K // blk + N_EXPERTS + -(-N_EXPERTS * DISPATCH_PIECE // blk)
        block_expert = jnp.minimum(
            jnp.sum(ends[None, :] <= (jnp.arange(n_blk) * blk)[:, None], axis=1),
            N_EXPERTS - 1).astype(jnp.int32)
        n_used = (ends[-1] // blk).astype(jnp.int32)[None]
        ends0 = jnp.concatenate([jnp.zeros((1,), jnp.int32), ends.astype(jnp.int32)])
        run_start = (pad_start[None, :] + tile_carry[::8, :N_EXPERTS]).astype(jnp.int32).reshape(-1)
        run_len = tile_cnt[::8, :N_EXPERTS].reshape(-1)
        xs_rows = _dispatch(run_start, run_len, ends0, xn_rows, slots, t, n_blk * blk)
        expert_ids = jnp.arange(N_EXPERTS, dtype=jnp.int32)
        run_end = jnp.sum(jnp.where(block_expert[:, None] == expert_ids, ends[None, :], 0), axis=1)
        next_expert = jnp.where(
            run_end < ends[-1],
            jnp.minimum(jnp.sum(ends[None, :] <= run_end[:, None], axis=1), N_EXPERTS - 1), -1).astype(jnp.int32)
        y_rows = _experts(block_expert, n_used, next_expert, xs_rows, w_gate_up[l], b_gate_up[l], w_down[l], b_down[l])
        gather_start = (pad_start[None, :] + ctile_carry[::8, :N_EXPERTS]).astype(jnp.int32).reshape(-1)
        gather_len = ctile_cnt[::8, :N_EXPERTS].reshape(-1)
        out = _combine(gather_start, gather_len, h2.reshape(t, d), gates, cslots, g_final[None], y_rows)
        h = out.reshape(b, s, d)
    return h
```

```python
import functools
import math

import jax
import jax.numpy as jnp
from jax import lax
from jax.experimental import pallas as pl
from jax.experimental.pallas import tpu as pltpu

F32 = jnp.float32
BF16 = jnp.bfloat16

LANES = 128
EPS = 1e-6
NEG_INF = -1e30
LOG2_E = math.log2(math.e)

N_HEADS_A = 8
HEAD_DIM_A = 64
WIDTH_A = N_HEADS_A * HEAD_DIM_A
DILATED_PATTERNS = ((128, 1), (512, 4), (2048, 16))
N_HEADS_B = 8
QK_NOPE_DIM = 64
QK_ROPE_DIM = 32
V_DIM_B = 64
Q_LORA_RANK = 256
KV_LORA_RANK = 128
ROPE_THETA = 10000.0
NUM_BUCKETS = 32
MAX_DISTANCE = 1024
N_HEADS_MEM = 4
N_EXPERTS = 32
TOP_K = 4
SWIGLU_LIMIT = 7.0
SWIGLU_ALPHA = 1.702

Q_BLOCK = 128
K_WINDOW = 256
HALF_WINDOW = (K_WINDOW - Q_BLOCK) // 2
MAX_Q_BLOCKS_PER_STEP = 4
EXPERT_BLOCK = 512
DISPATCH_CHUNK = 512
DISPATCH_PIECE = 64
COMBINE_TILE = 128
PROJ_CHAINS = 2
POST_CHAINS = 2
VMEM_LIMIT = 56 * 1024 * 1024


def _rms(x, g):
    return x * lax.rsqrt(jnp.mean(x * x, axis=-1, keepdims=True) + EPS) * g


def _dot(a, b):
    return jnp.dot(a, b, preferred_element_type=F32)


def _dot_nt(a, b):
    return lax.dot_general(a, b, (((1,), (1,)), ((), ())), preferred_element_type=F32)


def _run_skewed(chains):
    live = []
    pending = list(chains)
    while live or pending:
        if pending:
            live.append(pending.pop(0))
        for gen in list(live):
            try:
                next(gen)
            except StopIteration:
                live.remove(gen)


def _load_token_rows(ref, n_tok):
    n_chunk = ref.shape[0] // n_tok
    return jnp.concatenate([ref[pl.ds(c, n_tok, stride=n_chunk), :] for c in range(n_chunk)], axis=1)


def _pack_bf16_pairs(x):
    half = x.shape[1] // 2
    bits = lambda v: lax.bitcast_convert_type(v.astype(BF16).astype(F32), jnp.uint32)
    return bits(x[:, half:]) | (bits(x[:, :half]) >> 16)


def _unpack_bf16_pairs(w):
    lo = lax.bitcast_convert_type(w << 16, F32)
    hi = lax.bitcast_convert_type(w & jnp.uint32(0xFFFF0000), F32)
    return lo, hi


def _store_token_rows(ref, value):
    n_tok = value.shape[0]
    n_chunk = value.shape[1] // LANES
    for c in range(n_chunk):
        ref[pl.ds(c, n_tok, stride=n_chunk), :] = value[:, c * LANES:(c + 1) * LANES]


def _proj_kernel(x_ref, cs_ref, expand_ref, gmix_ref, win_ref, gq_ref, wq_ref, wqr_ref,
                 gkv_ref, wk_ref, wv_ref,
                 q1_ref, k1_ref, v1_ref, q4_ref, k4_ref, v4_ref, q16_ref, k16_ref, v16_ref,
                 qm_ref, km_ref, vm_ref, za_scr):
    tm = x_ref.shape[1]
    c0 = 3 * WIDTH_A
    c1 = c0 + Q_LORA_RANK + KV_LORA_RANK
    per_group = WIDTH_A // LANES
    rows_per_chain = tm // PROJ_CHAINS
    lane = lax.broadcasted_iota(jnp.int32, (1, LANES), 1)
    in_rope = jnp.logical_and(lane >= QK_NOPE_DIM, lane < QK_NOPE_DIM + QK_ROPE_DIM)
    ones_col = jnp.where(lane == V_DIM_B, 1.0, 0.0).astype(F32)
    scale = (QK_NOPE_DIM + QK_ROPE_DIM) ** -0.5 * LOG2_E

    def chain(ci):
        r0 = ci * rows_per_chain
        rows = slice(r0, r0 + rows_per_chain)
        xn = _rms(x_ref[0, rows, :], gmix_ref[...]).astype(BF16)
        yield
        z = _dot(xn, win_ref[...])
        yield
        for c in range(3 * per_group):
            chunk = z[:, c * LANES:(c + 1) * LANES]
            za_scr[c, rows, :] = chunk * (HEAD_DIM_A ** -0.5 * LOG2_E) if c < per_group else chunk
        for dil, refs in ((1, (q1_ref, k1_ref, v1_ref)), (4, (q4_ref, k4_ref, v4_ref)),
                          (16, (q16_ref, k16_ref, v16_ref))):
            n = rows_per_chain // dil
            for r in range(dil):
                for c in range(3 * per_group):
                    picked = za_scr[c, pl.ds(r0 + r, n, stride=dil), :].astype(BF16)
                    col = r * WIDTH_A + (c % per_group) * LANES
                    refs[c // per_group][0, r0 // dil:r0 // dil + n, col:col + LANES] = picked
        zqn = _rms(z[:, c0:c0 + Q_LORA_RANK], gq_ref[...]).astype(BF16)
        zkvn = _rms(z[:, c0 + Q_LORA_RANK:c1], gkv_ref[...]).astype(BF16)
        yield
        spread = jnp.dot(cs_ref[0, rows, :], expand_ref[...], preferred_element_type=F32,
                         precision=lax.Precision.HIGHEST)
        q = _dot(zqn, wq_ref[...])
        qr = _dot(zqn, wqr_ref[...])
        kn = _dot(zkvn, wk_ref[...])
        vv = _dot(zkvn, wv_ref[...])
        yield
        cos = jnp.where(in_rope, spread[:, :LANES], 1.0)
        sin = spread[:, LANES:]
        kpe = z[:, c1:c1 + LANES] * cos + z[:, c1 + LANES:c1 + 2 * LANES] * sin
        for h in range(N_HEADS_B):
            sl = slice(h * LANES, (h + 1) * LANES)
            qm_ref[0, h, rows, :] = ((q[:, sl] * cos + qr[:, sl] * sin) * scale).astype(BF16)
            km_ref[0, h, rows, :] = (kn[:, sl] + kpe).astype(BF16)
            vm_ref[0, h, rows, :] = (vv[:, sl] + ones_col).astype(BF16)

    _run_skewed([chain(ci) for ci in range(PROJ_CHAINS)])


def _proj(x, cos_sin, expand, g_mix, w_in_ext, g_q, wq, wqr, g_kv, wk, wv, tm):
    b, s, d = x.shape
    n_in = w_in_ext.shape[1]
    hw = N_HEADS_B * LANES
    full = lambda shape: pl.BlockSpec(shape, lambda bi, i: (0,) * len(shape))
    row = lambda w: pl.BlockSpec((1, tm, w), lambda bi, i: (bi, i, 0))
    head = pl.BlockSpec((1, N_HEADS_B, tm, LANES), lambda bi, i: (bi, 0, i, 0))
    dils = [dil for _, dil in DILATED_PATTERNS]
    view_specs = [pl.BlockSpec((1, tm // dil, dil * WIDTH_A), lambda bi, i: (bi, i, 0))
                  for dil in dils for _ in range(3)]
    view_shapes = [jax.ShapeDtypeStruct((b, s // dil, dil * WIDTH_A), BF16)
                   for dil in dils for _ in range(3)]
    return pl.pallas_call(
        _proj_kernel,
        grid=(b, s // tm),
        in_specs=[row(d), row(cos_sin.shape[2]), full(expand.shape), full((1, d)), full((d, n_in)),
                  full((1, Q_LORA_RANK)), full((Q_LORA_RANK, hw)), full((Q_LORA_RANK, hw)),
                  full((1, KV_LORA_RANK)), full((KV_LORA_RANK, hw)), full((KV_LORA_RANK, hw))],
        out_specs=view_specs + [head, head, head],
        out_shape=view_shapes + [jax.ShapeDtypeStruct((b, N_HEADS_B, s, LANES), BF16)] * 3,
        scratch_shapes=[pltpu.VMEM((3 * WIDTH_A // LANES, tm, LANES), F32)],
        compiler_params=pltpu.CompilerParams(
            dimension_semantics=("parallel", "parallel"), vmem_limit_bytes=VMEM_LIMIT),
        name="proj",
    )(x, cos_sin, expand, g_mix, w_in_ext, g_q, wq, wqr, g_kv, wk, wv)


_LOG_BUCKET_STARTS = tuple(
    next(n for n in range(8, 4096)
         if int(math.log(n / 8) / math.log(MAX_DISTANCE / 8) * 8) >= t)
    for t in range(1, 8))


def _rel_bucket(rel):
    n = jnp.abs(rel)
    large = jnp.full(rel.shape, NUM_BUCKETS // 4, jnp.int32)
    for start in _LOG_BUCKET_STARTS:
        large = large + jnp.where(n >= start, 1, 0)
    mag = jnp.where(n < NUM_BUCKETS // 4, n, large)
    return mag + jnp.where(rel > 0, NUM_BUCKETS // 2, 0)


def _window_start(j, sub_len):
    return jnp.clip(j * Q_BLOCK - HALF_WINDOW, 0, sub_len - K_WINDOW)


def _bias_kernel(off_ref, qpos_ref, kpos_ref, tbl_ref, out_ref):
    rel = kpos_ref[0] - qpos_ref[0]
    bucket = _rel_bucket(rel)
    delta = (off_ref[pl.program_id(0)] + lax.broadcasted_iota(jnp.int32, (1, K_WINDOW), 1)
             - lax.broadcasted_iota(jnp.int32, (Q_BLOCK, 1), 0))
    valid = jnp.abs(delta) <= HALF_WINDOW
    for h in range(N_HEADS_A):
        tbl = jnp.broadcast_to(tbl_ref[h:h + 1, :], (Q_BLOCK, LANES))
        bias = jnp.concatenate(
            [jnp.take_along_axis(tbl, bucket[:, c * LANES:(c + 1) * LANES], axis=1)
             for c in range(K_WINDOW // LANES)], axis=1)
        out_ref[0, h] = jnp.where(valid, bias * LOG2_E, NEG_INF)


def _bias_tiles(offs, qpos, kpos, tbl):
    n = offs.shape[0]
    grid_spec = pltpu.PrefetchScalarGridSpec(
        num_scalar_prefetch=1,
        grid=(n,),
        in_specs=[pl.BlockSpec((1, Q_BLOCK, 1), lambda t, off: (t, 0, 0)),
                  pl.BlockSpec((1, 1, K_WINDOW), lambda t, off: (t, 0, 0)),
                  pl.BlockSpec((N_HEADS_A, LANES), lambda t, off: (0, 0))],
        out_specs=pl.BlockSpec((1, N_HEADS_A, Q_BLOCK, K_WINDOW), lambda t, off: (t, 0, 0, 0)),
    )
    return pl.pallas_call(
        _bias_kernel,
        grid_spec=grid_spec,
        out_shape=jax.ShapeDtypeStruct((n, N_HEADS_A, Q_BLOCK, K_WINDOW), F32),
        compiler_params=pltpu.CompilerParams(
            dimension_semantics=("arbitrary",), vmem_limit_bytes=VMEM_LIMIT),
        name="bias_tiles",
    )(offs, qpos, kpos, tbl)


def _dilated_kernel(q_ref, k_ref, v_ref, *refs, sub_len, qb):
    bias_refs = refs[:qb]
    o_ref, lse_ref, s_scr, p_scr = refs[qb:]
    first = lax.broadcasted_iota(jnp.int32, (1, LANES), 1) < HEAD_DIM_A
    pair = lambda h: slice((h // 2) * LANES, (h // 2 + 1) * LANES)
    for sub in range(qb):
        j = pl.program_id(2) * qb + sub
        rows = slice(sub * Q_BLOCK, (sub + 1) * Q_BLOCK)
        kstart = pl.multiple_of(_window_start(j, sub_len), HALF_WINDOW)
        q = q_ref[0, rows, :]
        kw = k_ref[0, pl.ds(kstart, K_WINDOW), :]
        vw = v_ref[0, pl.ds(kstart, K_WINDOW), :]
        for h in range(N_HEADS_A):
            own = first if h % 2 == 0 else jnp.logical_not(first)
            qh = jnp.where(own, q[:, pair(h)], jnp.zeros_like(q[:, pair(h)]))
            s_scr[sub, h] = _dot_nt(qh, kw[:, pair(h)]) + bias_refs[sub][0, h]
        stats = []
        for h in range(N_HEADS_A):
            s = s_scr[sub, h]
            m = jnp.max(s, axis=1, keepdims=True)
            e = jnp.exp2(s - m)
            l = jnp.sum(e, axis=1, keepdims=True)
            p_scr[sub, h] = e.astype(BF16)
            stats.append((1.0 / l, m + jnp.log2(l)))
        for h in range(0, N_HEADS_A, 2):
            (r0, lse0), (r1, lse1) = stats[h], stats[h + 1]
            o0 = _dot(p_scr[sub, h], vw[:, pair(h)])
            o1 = _dot(p_scr[sub, h + 1], vw[:, pair(h)])
            o_ref[0, rows, pair(h)] = jnp.where(first, o0 * r0, o1 * r1)
            lse_ref[0, rows, pair(h)] = jnp.where(first, lse0, lse1)


def _dilated(qv, kv, vv, bias, bias_index, dil):
    b, sub_len, _ = qv.shape
    w = WIDTH_A
    nq = sub_len // Q_BLOCK
    qb = min(MAX_Q_BLOCKS_PER_STEP, nq)
    assert sub_len >= K_WINDOW and sub_len % (Q_BLOCK * qb) == 0
    qspec = pl.BlockSpec((1, Q_BLOCK * qb, w), lambda bi, r, j: (bi, j, r))
    kvspec = pl.BlockSpec((1, sub_len, w), lambda bi, r, j: (bi, 0, r))
    bspecs = [pl.BlockSpec((1, N_HEADS_A, Q_BLOCK, K_WINDOW),
                           lambda bi, r, j, sub=sub:
                           (bias_index(bi, r, j * qb + sub, nq), 0, 0, 0))
              for sub in range(qb)]
    return pl.pallas_call(
        functools.partial(_dilated_kernel, sub_len=sub_len, qb=qb),
        grid=(b, dil, nq // qb),
        in_specs=[qspec, kvspec, kvspec] + bspecs,
        out_specs=[qspec, qspec],
        out_shape=[jax.ShapeDtypeStruct((b, sub_len, dil * w), F32)] * 2,
        scratch_shapes=[pltpu.VMEM((qb, N_HEADS_A, Q_BLOCK, K_WINDOW), F32),
                        pltpu.VMEM((qb, N_HEADS_A, Q_BLOCK, K_WINDOW), BF16)],
        compiler_params=pltpu.CompilerParams(
            dimension_semantics=("parallel", "parallel", "arbitrary"),
            vmem_limit_bytes=VMEM_LIMIT),
        name=f"dilated{dil}",
    )(qv, kv, vv, *([bias] * qb))


def _dilated_all(views, positions, tbl):
    b, s = positions.shape
    dils = [dil for _, dil in DILATED_PATTERNS]
    consecutive = jnp.all(positions[:, 1:] - positions[:, :-1] == 1)

    def run(bias, index_fns):
        outs = []
        for g, dil in enumerate(dils):
            qv, kv, vv = views[3 * g:3 * g + 3]
            outs.extend(_dilated(qv, kv, vv, bias, index_fns[g], dil))
        return tuple(outs)

    def shared_tiles():
        offs, qpos, kpos, fns = [], [], [], []
        for g, dil in enumerate(dils):
            for off in (0, -HALF_WINDOW, -2 * HALF_WINDOW):
                offs.append(off)
                qpos.append(dil * jnp.arange(Q_BLOCK, dtype=jnp.int32))
                kpos.append(dil * (off + jnp.arange(K_WINDOW, dtype=jnp.int32)))
            fns.append(lambda bi, r, j, nq, g=g:
                       3 * g + jnp.where(j == 0, 0, jnp.where(j == nq - 1, 2, 1)))
        bias = _bias_tiles(jnp.array(offs, jnp.int32), jnp.stack(qpos)[:, :, None],
                           jnp.stack(kpos)[:, None, :], tbl)
        return run(bias, fns)

    def per_block_tiles():
        offs, qpos, kpos, fns = [], [], [], []
        base = 0
        for g, dil in enumerate(dils):
            sub_len = s // dil
            nq = sub_len // Q_BLOCK
            pos_t = positions.reshape(b, sub_len, dil).transpose(0, 2, 1).reshape(b * dil, sub_len)
            starts = _window_start(jnp.arange(nq), sub_len)
            win = starts[:, None] + jnp.arange(K_WINDOW)[None, :]
            qpos.append(pos_t.reshape(b * dil * nq, Q_BLOCK))
            kpos.append(pos_t[:, win].reshape(b * dil * nq, K_WINDOW))
            offs.append(jnp.tile(starts - jnp.arange(nq) * Q_BLOCK, b * dil))
            fns.append(lambda bi, r, j, nq, base=base, dil=dil: base + (bi * dil + r) * nq + j)
            base += b * dil * nq
        bias = _bias_tiles(jnp.concatenate(offs).astype(jnp.int32),
                           jnp.concatenate(qpos)[:, :, None], jnp.concatenate(kpos)[:, None, :], tbl)
        return run(bias, fns)

    return lax.cond(consecutive, shared_tiles, per_block_tiles)


def _mla_kernel(q_ref, k_ref, v_ref, o_ref, *, tk):
    q = q_ref[0, 0]
    tq = q.shape[0]
    n_kv = k_ref.shape[2] // tk

    def body(i, carry):
        m, acc = carry
        start = pl.multiple_of(i * tk, tk)
        k = k_ref[0, 0, pl.ds(start, tk), :]
        v = v_ref[0, 0, pl.ds(start, tk), :]
        s = _dot_nt(q, k)
        m_new = jnp.maximum(m, jnp.max(s, axis=1, keepdims=True))
        p = jnp.exp2(s - m_new)
        acc = jnp.exp2(m - m_new) * acc + _dot(p.astype(BF16), v)
        return m_new, acc

    m0 = jnp.full((tq, 1), NEG_INF, F32)
    _, acc = lax.fori_loop(0, n_kv, body, (m0, jnp.zeros((tq, LANES), F32)), unroll=True)
    lane = lax.broadcasted_iota(jnp.int32, (1, LANES), 1)
    denom = jnp.sum(jnp.where(lane == V_DIM_B, acc, 0.0), axis=1, keepdims=True)
    o_ref[0, 0] = jnp.where(lane < V_DIM_B, acc / denom, 0.0).astype(BF16)


def _mla(qm, km, vm, tq, tk):
    b, nh, s, _ = qm.shape
    qspec = pl.BlockSpec((1, 1, tq, LANES), lambda bi, h, i: (bi, h, i, 0))
    kvspec = pl.BlockSpec((1, 1, s, LANES), lambda bi, h, i: (bi, h, 0, 0))
    return pl.pallas_call(
        functools.partial(_mla_kernel, tk=tk),
        grid=(b, nh, s // tq),
        in_specs=[qspec, kvspec, kvspec],
        out_specs=qspec,
        out_shape=jax.ShapeDtypeStruct((b, nh, s, LANES), BF16),
        compiler_params=pltpu.CompilerParams(
            dimension_semantics=("parallel", "parallel", "arbitrary"),
            vmem_limit_bytes=VMEM_LIMIT),
        name="mla",
    )(qm, km, vm)


def _memkv_kernel(mem_ref, g_ref, w_ref, k_ref, v_ref):
    d = mem_ref.shape[2]
    kv = _dot(_rms(mem_ref[0], g_ref[...]).astype(BF16), w_ref[...])
    k_ref[0] = kv[:, :d].astype(BF16)
    v_ref[0] = kv[:, d:].astype(BF16)


def _memkv(mem, g_mem, w_mkv):
    b, m, d = mem.shape
    spec = pl.BlockSpec((1, m, d), lambda bi: (bi, 0, 0))
    return pl.pallas_call(
        _memkv_kernel,
        grid=(b,),
        in_specs=[spec, pl.BlockSpec((1, d), lambda bi: (0, 0)),
                  pl.BlockSpec((d, 2 * d), lambda bi: (0, 0))],
        out_specs=[spec, spec],
        out_shape=[jax.ShapeDtypeStruct((b, m, d), BF16)] * 2,
        compiler_params=pltpu.CompilerParams(
            dimension_semantics=("parallel",), vmem_limit_bytes=VMEM_LIMIT),
        name="memkv",
    )(mem, g_mem, w_mkv)


def _post_kernel(x_ref, o1_ref, o2_ref, o3_ref, l1_ref, l2_ref, l3_ref, ob_ref,
                 goa_ref, gob_ref, woa_ref, wob_ref, gx_ref, wmq_ref, km_ref, vm_ref, wmo_ref,
                 gmoe_ref, wr_ref, br_ref,
                 h_ref, xn_ref, logit_ref, *nat_scr):
    tm = x_ref.shape[1]
    n_chunk = WIDTH_A // LANES

    for view_ref, scr in zip((o2_ref, o3_ref, l2_ref, l3_ref), nat_scr):
        dil = view_ref.shape[2] // WIDTH_A
        for r in range(dil):
            for c in range(n_chunk):
                col = r * WIDTH_A + c * LANES
                scr[c, pl.ds(r, tm // dil, stride=dil), :] = view_ref[0, :, col:col + LANES]

    def natural(scr, rows):
        return jnp.concatenate([scr[c, rows, :] for c in range(n_chunk)], axis=1)

    rows_per_chain = tm // POST_CHAINS

    def chain(ci):
        rows = slice(ci * rows_per_chain, (ci + 1) * rows_per_chain)
        o1, o2, o3 = o1_ref[0, rows, :], natural(nat_scr[0], rows), natural(nat_scr[1], rows)
        l1, l2, l3 = l1_ref[0, rows, :], natural(nat_scr[2], rows), natural(nat_scr[3], rows)
        mx = jnp.maximum(jnp.maximum(l1, l2), l3)
        w1, w2, w3 = jnp.exp2(l1 - mx), jnp.exp2(l2 - mx), jnp.exp2(l3 - mx)
        oa = (w1 * o1 + w2 * o2 + w3 * o3) / (w1 + w2 + w3)
        oan = _rms(oa, goa_ref[...]).astype(BF16)
        ob = jnp.concatenate([ob_ref[0, h, rows, :] for h in range(N_HEADS_B)], axis=1).astype(F32)
        ms_b = jnp.sum(ob * ob, axis=1, keepdims=True) * (1.0 / (N_HEADS_B * V_DIM_B))
        obn = (ob * lax.rsqrt(ms_b + EPS) * gob_ref[...]).astype(BF16)
        yield
        h1 = x_ref[0, rows, :] + _dot(oan, woa_ref[...]) + _dot(obn, wob_ref[...])
        yield
        hn = _rms(h1, gx_ref[...]).astype(BF16)
        yield
        q = _dot(hn, wmq_ref[...]).astype(BF16)
        yield
        dh = q.shape[1] // N_HEADS_MEM
        heads = []
        for h in range(N_HEADS_MEM):
            sl = slice(h * dh, (h + 1) * dh)
            s = _dot_nt(q[:, sl], km_ref[0, :, sl])
            yield
            e = jnp.exp(s - jnp.max(s, axis=1, keepdims=True))
            p = e / jnp.sum(e, axis=1, keepdims=True)
            yield
            heads.append(_dot(p.astype(BF16), vm_ref[0, :, sl]))
        o = jnp.concatenate(heads, axis=1).astype(BF16)
        yield
        h2 = h1 + _dot(o, wmo_ref[...])
        h_ref[0, rows, :] = h2
        yield
        xn = _rms(h2, gmoe_ref[...])
        packed = _pack_bf16_pairs(xn)
        rows_per_token = packed.shape[1] // LANES
        _store_token_rows(xn_ref.at[pl.ds(ci * rows_per_chain * rows_per_token,
                                          rows_per_chain * rows_per_token)], packed)
        yield
        logit_ref[0, rows, :] = jnp.dot(xn, wr_ref[...], preferred_element_type=F32,
                                        precision=lax.Precision.HIGHEST) + br_ref[...]

    _run_skewed([chain(ci) for ci in range(POST_CHAINS)])


def _post(x, o_pats, lse_pats, ob, g_out_a, g_out_b_pad, w_o_a, w_o_b_pad, g_xattn, w_mq, kmem,
          vmem, w_mo, g_moe, w_router_pad, b_router_pad, tm):
    b, s, d = x.shape
    m = kmem.shape[1]
    full = lambda shape: pl.BlockSpec(shape, lambda bi, i: (0,) * len(shape))
    row = lambda w: pl.BlockSpec((1, tm, w), lambda bi, i: (bi, i, 0))
    memspec = pl.BlockSpec((1, m, d), lambda bi, i: (bi, 0, 0))
    hw = N_HEADS_B * LANES
    views = [pl.BlockSpec((1, tm // dil, dil * WIDTH_A), lambda bi, i: (bi, i, 0))
             for _, dil in DILATED_PATTERNS]
    return pl.pallas_call(
        _post_kernel,
        grid=(b, s // tm),
        in_specs=[row(d)] + views + views
        + [pl.BlockSpec((1, N_HEADS_B, tm, LANES), lambda bi, i: (bi, 0, i, 0)),
           full((1, WIDTH_A)), full((1, hw)), full((WIDTH_A, d)), full((hw, d)),
           full((1, d)), full((d, d)), memspec, memspec, full((d, d)),
           full((1, d)), full((d, LANES)), full((1, LANES))],
        out_specs=[row(d), pl.BlockSpec((tm * d // 2 // LANES, LANES), lambda bi, i: (bi * (s // tm) + i, 0)),
                   row(LANES)],
        out_shape=[jax.ShapeDtypeStruct((b, s, d), F32),
                   jax.ShapeDtypeStruct((b * s * d // 2 // LANES, LANES), jnp.uint32),
                   jax.ShapeDtypeStruct((b, s, LANES), F32)],
        scratch_shapes=[pltpu.VMEM((WIDTH_A // LANES, tm, LANES), F32)] * 4,
        compiler_params=pltpu.CompilerParams(
            dimension_semantics=("parallel", "parallel"), vmem_limit_bytes=VMEM_LIMIT),
        name="post",
    )(x, *o_pats, *lse_pats, ob, g_out_a, g_out_b_pad, w_o_a, w_o_b_pad, g_xattn, w_mq, kmem,
      vmem, w_mo, g_moe, w_router_pad, b_router_pad)


def _route_kernel(logit_ref, gate_ref, slot_ref, cslot_ref, count_ref, tcnt_ref, tcarry_ref, ccnt_ref,
                  ccarry_ref, carry_ref):
    @pl.when(pl.program_id(0) == 0)
    def _():
        carry_ref[...] = jnp.zeros_like(carry_ref)

    l = logit_ref[...]
    tr = l.shape[0]
    lane_i = lax.broadcasted_iota(jnp.int32, l.shape, 1)
    lane = lane_i.astype(F32)
    vals, idxs = [], []
    for _ in range(TOP_K):
        m = jnp.max(l, axis=1, keepdims=True)
        idx = jnp.min(jnp.where(l == m, lane, float(LANES)), axis=1, keepdims=True)
        vals.append(m)
        idxs.append(idx)
        l = jnp.where(lane == idx, -jnp.inf, l)
    exps = [jnp.exp(v - vals[0]) for v in vals]
    denom = exps[0] + exps[1] + exps[2] + exps[3]
    onehot = jnp.zeros(l.shape, F32)
    for idx in idxs:
        onehot = onehot + jnp.where(lane == idx, 1.0, 0.0)
    r = lax.broadcasted_iota(jnp.int32, (tr, tr), 0)
    c = lax.broadcasted_iota(jnp.int32, (tr, tr), 1)
    tri = jnp.where(c < r, 1.0, 0.0).astype(BF16)
    local = _dot(tri, onehot.astype(BF16))
    tile_cnt = jnp.sum(onehot, axis=0, keepdims=True)
    li = lax.broadcasted_iota(jnp.int32, (LANES, LANES), 0)
    lj = lax.broadcasted_iota(jnp.int32, (LANES, LANES), 1)
    prefix = jnp.dot(jnp.broadcast_to(tile_cnt, (8, LANES)), jnp.where(li < lj, 1.0, 0.0),
                     preferred_element_type=F32, precision=lax.Precision.HIGHEST)[0:1]
    grouped = local + prefix
    n_sub = tr // COMBINE_TILE
    sub_cnt = [jnp.sum(onehot[u * COMBINE_TILE:(u + 1) * COMBINE_TILE], axis=0, keepdims=True)
               for u in range(n_sub)]
    sub_prefix = jnp.dot(jnp.concatenate(sub_cnt + [jnp.zeros((8 - n_sub, LANES), F32)], axis=0),
                         jnp.where(li < lj, 1.0, 0.0), preferred_element_type=F32,
                         precision=lax.Precision.HIGHEST)
    shifts, sub_base = [], jnp.zeros((1, LANES), F32)
    for u in range(n_sub):
        shifts.append(jnp.broadcast_to(sub_prefix[u:u + 1] - sub_base, (COMBINE_TILE, LANES)))
        ccnt_ref[u * 8:(u + 1) * 8, :] = jnp.broadcast_to(sub_cnt[u], (8, LANES)).astype(jnp.int32)
        ccarry_ref[u * 8:(u + 1) * 8, :] = jnp.broadcast_to(carry_ref[...] + sub_base,
                                                            (8, LANES)).astype(jnp.int32)
        sub_base = sub_base + sub_cnt[u]
    sub_grouped = local + jnp.concatenate(shifts, axis=0)
    gate = jnp.zeros(l.shape, F32)
    slot = jnp.zeros(l.shape, jnp.int32)
    cslot = jnp.zeros(l.shape, jnp.int32)
    for k in range(TOP_K):
        mine = lane == idxs[k]
        sk = jnp.sum(jnp.where(mine, grouped, 0.0), axis=1, keepdims=True)
        ck = jnp.sum(jnp.where(mine, sub_grouped, 0.0), axis=1, keepdims=True)
        gate = jnp.where(lane_i == k, exps[k] / denom, gate)
        slot = jnp.where(lane_i == k, sk.astype(jnp.int32), slot)
        cslot = jnp.where(lane_i == k, ck.astype(jnp.int32), cslot)
    gate_ref[...] = gate
    slot_ref[...] = slot
    cslot_ref[...] = cslot
    tcnt_ref[...] = jnp.broadcast_to(tile_cnt, tcnt_ref.shape).astype(jnp.int32)
    tcarry_ref[...] = jnp.broadcast_to(carry_ref[...], tcarry_ref.shape).astype(jnp.int32)
    total = carry_ref[...] + tile_cnt
    carry_ref[...] = total
    count_ref[...] = total.astype(jnp.int32)


def _route(logits, tr):
    t = logits.shape[0]
    spec = pl.BlockSpec((tr, LANES), lambda i: (i, 0))
    return pl.pallas_call(
        _route_kernel,
        grid=(t // tr,),
        in_specs=[spec],
        out_specs=[spec, spec, spec, pl.BlockSpec((1, LANES), lambda i: (0, 0)),
                   pl.BlockSpec((8, LANES), lambda i: (i, 0)), pl.BlockSpec((8, LANES), lambda i: (i, 0)),
                   pl.BlockSpec((8 * tr // COMBINE_TILE, LANES), lambda i: (i, 0)),
                   pl.BlockSpec((8 * tr // COMBINE_TILE, LANES), lambda i: (i, 0))],
        out_shape=[jax.ShapeDtypeStruct((t, LANES), F32)] + [jax.ShapeDtypeStruct((t, LANES), jnp.int32)] * 2
        + [jax.ShapeDtypeStruct((1, LANES), jnp.int32)]
        + [jax.ShapeDtypeStruct((t // tr * 8, LANES), jnp.int32)] * 2
        + [jax.ShapeDtypeStruct((t // COMBINE_TILE * 8, LANES), jnp.int32)] * 2,
        scratch_shapes=[pltpu.VMEM((1, LANES), F32)],
        compiler_params=pltpu.CompilerParams(
            dimension_semantics=("arbitrary",), vmem_limit_bytes=VMEM_LIMIT),
        name="route",
    )(logits)


def _dispatch_kernel(dst_ref, cnt_ref, ends_ref, x_ref, slot_ref, xs_hbm, zbuf, stage, sem, zsem):
    c = pl.program_id(0)
    buf = lax.rem(c, 2)
    n_chunk = x_ref.shape[0] // DISPATCH_CHUNK
    blk_rows = zbuf.shape[0]
    n_staged = DISPATCH_CHUNK * TOP_K

    @pl.when(c == 0)
    def _():
        zbuf[...] = jnp.zeros_like(zbuf)
        for b in range(2):
            stage[b, pl.ds(n_staged * n_chunk, DISPATCH_PIECE * n_chunk), :] = jnp.zeros(
                (DISPATCH_PIECE * n_chunk, LANES), stage.dtype)

        def zero_copy(start):
            return pltpu.make_async_copy(
                zbuf, xs_hbm.at[pl.ds(pl.multiple_of(start * n_chunk, n_chunk), blk_rows)], zsem)

        blk = blk_rows // n_chunk
        used_end = ends_ref[N_EXPERTS]
        starts = [(ends_ref[e + 1] - ends_ref[e] >= back * blk, ends_ref[e + 1] - back * blk)
                  for e in range(N_EXPERTS) for back in (1, 2)]
        starts += [((used_end + (j + 1) * blk) * n_chunk <= xs_hbm.shape[0], used_end + j * blk)
                   for j in range(N_EXPERTS)]
        for cond, start in starts:
            @pl.when(cond)
            def _():
                zero_copy(start).start()
        for cond, start in starts:
            @pl.when(cond)
            def _():
                zero_copy(start).wait()

    x_lo, x_hi = _unpack_bf16_pairs(_load_token_rows(x_ref, DISPATCH_CHUNK))
    x_lo, x_hi = x_lo.astype(BF16), x_hi.astype(BF16)
    slot_t = jnp.transpose(slot_ref[...].astype(F32))
    bits = lambda v: lax.bitcast_convert_type(v, jnp.uint32)
    for part in range(TOP_K):
        j = (lax.broadcasted_iota(jnp.int32, (DISPATCH_CHUNK, 1), 0) + part * DISPATCH_CHUNK).astype(F32)
        pick = jnp.zeros((DISPATCH_CHUNK, DISPATCH_CHUNK), F32)
        for k in range(TOP_K):
            pick = jnp.where(slot_t[k:k + 1, :] == j, 1.0, pick)
        pick = pick.astype(BF16)
        packed = bits(_dot(pick, x_hi)) | (bits(_dot(pick, x_lo)) >> 16)
        _store_token_rows(
            stage.at[buf, pl.ds(part * DISPATCH_CHUNK * n_chunk, DISPATCH_CHUNK * n_chunk)], packed)

    def piece(src_tok, dst_tok):
        rows = DISPATCH_PIECE * n_chunk
        return pltpu.make_async_copy(
            stage.at[buf, pl.ds(pl.multiple_of(src_tok * n_chunk, n_chunk), rows)],
            xs_hbm.at[pl.ds(pl.multiple_of(dst_tok * n_chunk, n_chunk), rows)], sem)

    def drain(step):
        n_pieces = 0
        for e in range(N_EXPERTS):
            n_pieces = n_pieces + (cnt_ref[step * N_EXPERTS + e] + DISPATCH_PIECE - 1) // DISPATCH_PIECE

        def wait_one(i, carry):
            piece(0, 0).wait()
            return carry
        lax.fori_loop(0, n_pieces, wait_one, 0)

    @pl.when(c > 0)
    def _():
        drain(c - 1)

    staged = 0
    for e in range(N_EXPERTS):
        cnt = cnt_ref[c * N_EXPERTS + e]
        dst = dst_ref[c * N_EXPERTS + e]
        for p in range(DISPATCH_CHUNK // DISPATCH_PIECE):
            @pl.when(p * DISPATCH_PIECE < cnt)
            def _():
                piece(staged + p * DISPATCH_PIECE, dst + p * DISPATCH_PIECE).start()
        staged = staged + cnt

    @pl.when(c == pl.num_programs(0) - 1)
    def _():
        drain(c)


def _dispatch(dst, cnt, ends0, xn_rows, slots, n_tok, n_slots):
    n_chunk = xn_rows.shape[0] // n_tok
    staged_rows = (DISPATCH_CHUNK * TOP_K + DISPATCH_PIECE) * n_chunk
    grid_spec = pltpu.PrefetchScalarGridSpec(
        num_scalar_prefetch=3,
        grid=(n_tok // DISPATCH_CHUNK,),
        in_specs=[pl.BlockSpec((DISPATCH_CHUNK * n_chunk, LANES), lambda c, *_: (c, 0)),
                  pl.BlockSpec((DISPATCH_CHUNK, LANES), lambda c, *_: (c, 0))],
        out_specs=pl.BlockSpec(memory_space=pl.ANY),
        scratch_shapes=[pltpu.VMEM((EXPERT_BLOCK * n_chunk, LANES), xn_rows.dtype),
                        pltpu.VMEM((2, staged_rows, LANES), xn_rows.dtype),
                        pltpu.SemaphoreType.DMA(()), pltpu.SemaphoreType.DMA(())],
    )
    return pl.pallas_call(
        _dispatch_kernel,
        grid_spec=grid_spec,
        out_shape=jax.ShapeDtypeStruct((n_slots * n_chunk, LANES), xn_rows.dtype),
        compiler_params=pltpu.CompilerParams(
            dimension_semantics=("arbitrary",), vmem_limit_bytes=VMEM_LIMIT),
        name="dispatch",
    )(dst, cnt, ends0, xn_rows, slots)


def _expert_kernel(be_ref, nused_ref, next_ref, valid_ref, xs_ref, wgu_hbm, bgu_ref, wd_hbm, bd_ref, y_ref,
                   wgu_f32, wd_f32, wgu_bf, wd_bf, sems):
    i = pl.program_id(0)
    n_chunk = xs_ref.shape[0] // EXPERT_BLOCK
    half = EXPERT_BLOCK // 2

    def mlp(n_rows):
        part = pl.ds(0, n_rows * n_chunk)
        x_lo, x_hi = _unpack_bf16_pairs(_load_token_rows(xs_ref.at[part], n_rows))
        x = jnp.concatenate([x_lo.astype(BF16), x_hi.astype(BF16)], axis=1)
        gu = _dot(x, wgu_bf[...]) + bgu_ref[0]
        de = gu.shape[1] // 2
        gate = jnp.minimum(gu[:, :de], SWIGLU_LIMIT)
        up = jnp.clip(gu[:, de:], -SWIGLU_LIMIT, SWIGLU_LIMIT)
        hmid = (up + 1.0) * (gate * jax.nn.sigmoid(SWIGLU_ALPHA * gate))
        y = _dot(hmid.astype(BF16), wd_bf[...]) + bd_ref[0]
        _store_token_rows(y_ref.at[part], _pack_bf16_pairs(y))

    def fetch(e):
        return (pltpu.make_async_copy(wgu_hbm.at[e], wgu_f32, sems.at[0]),
                pltpu.make_async_copy(wd_hbm.at[e], wd_f32, sems.at[1]))

    @pl.when(i == 0)
    def _():
        for copy in fetch(be_ref[0]):
            copy.start()

    @pl.when(i < nused_ref[0])
    def _():
        @pl.when(jnp.logical_or(i == 0, be_ref[i] != be_ref[jnp.maximum(i - 1, 0)]))
        def _():
            for copy in fetch(be_ref[i]):
                copy.wait()
            wgu_bf[...] = wgu_f32[...].astype(BF16)
            wd_bf[...] = wd_f32[...].astype(BF16)

            @pl.when(next_ref[i] >= 0)
            def _():
                for copy in fetch(next_ref[i]):
                    copy.start()

        valid = valid_ref[i]

        @pl.when(valid > half)
        def _():
            mlp(EXPERT_BLOCK)

        @pl.when(jnp.logical_and(valid > 0, valid <= half))
        def _():
            mlp(half)
            y_ref[pl.ds(half * n_chunk, half * n_chunk), :] = jnp.zeros((half * n_chunk, LANES), y_ref.dtype)

    @pl.when(jnp.logical_or(i >= nused_ref[0], valid_ref[i] == 0))
    def _():
        y_ref[...] = jnp.zeros_like(y_ref)


def _experts(block_expert, n_used, next_expert, valid_rows, xs_rows, w_gate_up, b_gate_up, w_down, b_down):
    d, de2 = w_gate_up.shape[1:]
    n_chunk = d // 2 // LANES
    blk = EXPERT_BLOCK
    n_blk = xs_rows.shape[0] // (blk * n_chunk)
    grid_spec = pltpu.PrefetchScalarGridSpec(
        num_scalar_prefetch=4,
        grid=(n_blk,),
        in_specs=[pl.BlockSpec((blk * n_chunk, LANES),
                               lambda i, be, nu, *_: (jnp.maximum(jnp.minimum(i, nu[0] - 1), 0), 0)),
                  pl.BlockSpec(memory_space=pl.ANY),
                  pl.BlockSpec((1, 1, de2), lambda i, be, *_: (be[i], 0, 0)),
                  pl.BlockSpec(memory_space=pl.ANY),
                  pl.BlockSpec((1, 1, d), lambda i, be, *_: (be[i], 0, 0))],
        out_specs=pl.BlockSpec((blk * n_chunk, LANES), lambda i, *_: (i, 0)),
        scratch_shapes=[pltpu.VMEM((d, de2), F32), pltpu.VMEM((de2 // 2, d), F32),
                        pltpu.VMEM((d, de2), BF16), pltpu.VMEM((de2 // 2, d), BF16),
                        pltpu.SemaphoreType.DMA((2,))],
    )
    return pl.pallas_call(
        _expert_kernel,
        grid_spec=grid_spec,
        out_shape=jax.ShapeDtypeStruct(xs_rows.shape, xs_rows.dtype),
        compiler_params=pltpu.CompilerParams(
            dimension_semantics=("arbitrary",), vmem_limit_bytes=VMEM_LIMIT),
        name="experts",
    )(block_expert, n_used, next_expert, valid_rows, xs_rows, w_gate_up, b_gate_up.reshape(N_EXPERTS, 1, de2),
      w_down, b_down.reshape(N_EXPERTS, 1, d))


def _combine_kernel(start_ref, len_ref, h_ref, gate_ref, slot_ref, gfin_ref, y_hbm, out_ref, stage, sems):
    i = pl.program_id(0)
    n = pl.num_programs(0)
    buf = lax.rem(i, 2)
    tc = h_ref.shape[0]
    n_staged = tc * TOP_K
    n_chunk = stage.shape[1] // n_staged

    def issue(tile, b):
        staged = 0
        for e in range(N_EXPERTS):
            run_len = len_ref[tile * N_EXPERTS + e]
            run_start = start_ref[tile * N_EXPERTS + e]
            for bit in reversed(range(COMBINE_TILE.bit_length())):
                size = 1 << bit
                done = (run_len >> (bit + 1)) << (bit + 1)

                @pl.when((run_len >> bit) & 1 == 1)
                def _():
                    pltpu.make_async_copy(
                        y_hbm.at[pl.ds(pl.multiple_of((run_start + done) * n_chunk, n_chunk), size * n_chunk)],
                        stage.at[b, pl.ds(pl.multiple_of((staged + done) * n_chunk, n_chunk), size * n_chunk)],
                        sems.at[b]).start()
            staged = staged + run_len

    @pl.when(i == 0)
    def _():
        issue(0, 0)

    @pl.when(i + 1 < n)
    def _():
        issue(i + 1, 1 - buf)

    pltpu.make_async_copy(y_hbm.at[pl.ds(0, n_staged * n_chunk)], stage.at[buf], sems.at[buf]).wait()

    y_lo, y_hi = _unpack_bf16_pairs(_load_token_rows(stage.at[buf], n_staged))
    y_lo, y_hi = y_lo.astype(BF16), y_hi.astype(BF16)
    gate = gate_ref[...]
    slot = slot_ref[...].astype(F32)
    j = lax.broadcasted_iota(jnp.int32, (1, n_staged), 1).astype(F32)
    pick = jnp.zeros((tc, n_staged), F32)
    for k in range(TOP_K):
        pick = jnp.where(slot[:, k:k + 1] == j, gate[:, k:k + 1], pick)
    pick_hi = pick.astype(BF16)
    pick_lo = (pick - pick_hi.astype(F32)).astype(BF16)
    moe = jnp.concatenate([_dot(pick_hi, y_lo) + _dot(pick_lo, y_lo),
                           _dot(pick_hi, y_hi) + _dot(pick_lo, y_hi)], axis=1)
    out_ref[...] = _rms(h_ref[...] + moe, gfin_ref[...])


def _combine(run_start, run_len, h2, gates, slots, g_final, y_rows):
    t, d = h2.shape
    tc = COMBINE_TILE
    n_chunk = d // 2 // LANES
    grid_spec = pltpu.PrefetchScalarGridSpec(
        num_scalar_prefetch=2,
        grid=(t // tc,),
        in_specs=[pl.BlockSpec((tc, d), lambda i, *_: (i, 0)),
                  pl.BlockSpec((tc, LANES), lambda i, *_: (i, 0)),
                  pl.BlockSpec((tc, LANES), lambda i, *_: (i, 0)),
                  pl.BlockSpec((1, d), lambda i, *_: (0, 0)),
                  pl.BlockSpec(memory_space=pl.ANY)],
        out_specs=pl.BlockSpec((tc, d), lambda i, *_: (i, 0)),
        scratch_shapes=[pltpu.VMEM((2, tc * TOP_K * n_chunk, LANES), y_rows.dtype),
                        pltpu.SemaphoreType.DMA((2,))],
    )
    return pl.pallas_call(
        _combine_kernel,
        grid_spec=grid_spec,
        out_shape=jax.ShapeDtypeStruct((t, d), F32),
        compiler_params=pltpu.CompilerParams(
            dimension_semantics=("arbitrary",), vmem_limit_bytes=VMEM_LIMIT),
        name="combine",
    )(run_start, run_len, h2, gates, slots, g_final, y_rows)


def _pad_heads(w, n_heads, width, offset=0):
    k = w.shape[0]
    w = w.reshape(k, n_heads, width)
    w = jnp.pad(w, ((0, 0), (0, 0), (offset, LANES - width - offset)))
    return w.reshape(k, n_heads * LANES)


def _rot_cols(w):
    half = w.shape[-1] // 2
    return jnp.concatenate([-w[..., half:], w[..., :half]], axis=-1)


def kernel(x, mem, positions, g_mix, w_in, g_q_a, w_q_b, g_kv_a, w_kv_b, rel_bias, g_out_a, g_out_b,
           w_o, g_xattn, g_mem, w_mq, w_mkv, w_mo, g_moe, w_router, b_router, w_gate_up, b_gate_up,
           w_down, b_down, g_final):
    b, s, d = x.shape
    t = b * s
    assert g_mix.shape[0] == 1, "single-layer block: the final norm is fused into the last stage"

    inv_freq = ROPE_THETA ** (-jnp.arange(0, QK_ROPE_DIM, 2, dtype=F32) / QK_ROPE_DIM)
    ang = positions.astype(F32)[..., None] * inv_freq
    cos_sin = jnp.concatenate([jnp.cos(ang), jnp.sin(ang)], axis=-1)
    half = QK_ROPE_DIM // 2
    src = jnp.arange(QK_ROPE_DIM)[:, None]
    lane = jnp.arange(LANES)[None, :]
    on_rope = (lane >= QK_NOPE_DIM) & (lane < QK_NOPE_DIM + QK_ROPE_DIM)
    place = ((lane - QK_NOPE_DIM) % half == src % half) & on_rope
    expand = jnp.concatenate([place & (src < half), place & (src >= half)], axis=1).astype(F32)
    tbl = jnp.pad(rel_bias.T.astype(F32), ((0, 0), (0, LANES - NUM_BUCKETS)))

    h = x
    for l in range(1):
        c0 = 3 * WIDTH_A + Q_LORA_RANK + KV_LORA_RANK
        w_kpe = w_in[l][:, c0:c0 + QK_ROPE_DIM]
        place = lambda w: jnp.pad(w, ((0, 0), (QK_NOPE_DIM, LANES - QK_NOPE_DIM - QK_ROPE_DIM)))
        w_in_ext = jnp.concatenate([w_in[l][:, :c0], place(w_kpe), place(_rot_cols(w_kpe))], 1).astype(BF16)
        dq = QK_NOPE_DIM + QK_ROPE_DIM
        wq3 = w_q_b[l].reshape(Q_LORA_RANK, N_HEADS_B, dq)
        wq = _pad_heads(wq3.reshape(Q_LORA_RANK, -1), N_HEADS_B, dq).astype(BF16)
        wq_rot3 = jnp.concatenate([jnp.zeros_like(wq3[..., :QK_NOPE_DIM]), _rot_cols(wq3[..., QK_NOPE_DIM:])], -1)
        wqr = _pad_heads(wq_rot3.reshape(Q_LORA_RANK, -1), N_HEADS_B, dq).astype(BF16)
        wkv3 = w_kv_b[l].reshape(KV_LORA_RANK, N_HEADS_B, QK_NOPE_DIM + V_DIM_B)
        wk = _pad_heads(wkv3[..., :QK_NOPE_DIM].reshape(KV_LORA_RANK, -1), N_HEADS_B, QK_NOPE_DIM).astype(BF16)
        wv = _pad_heads(wkv3[..., QK_NOPE_DIM:].reshape(KV_LORA_RANK, -1), N_HEADS_B, V_DIM_B).astype(BF16)

        *views, qm, km, vm = _proj(h, cos_sin, expand, g_mix[l][None], w_in_ext, g_q_a[l][None], wq, wqr,
                                   g_kv_a[l][None], wk, wv, tm=256 * PROJ_CHAINS)
        pats = _dilated_all(views, positions, tbl)
        ob = _mla(qm, km, vm, tq=1024, tk=512)
        kmem, vmem = _memkv(mem, g_mem[l][None], w_mkv[l].astype(BF16))

        g_out_b_pad = _pad_heads(g_out_b[l][None], N_HEADS_B, V_DIM_B)
        w_o_b_pad = _pad_heads(w_o[l][WIDTH_A:].T, N_HEADS_B, V_DIM_B).T.astype(BF16)
        w_mq_s = (w_mq[l] * ((d // N_HEADS_MEM) ** -0.5)).astype(BF16)
        w_router_pad = jnp.pad(w_router[l], ((0, 0), (0, LANES - N_EXPERTS)))
        b_router_pad = jnp.pad(b_router[l][None], ((0, 0), (0, LANES - N_EXPERTS)), constant_values=NEG_INF)
        h2, xn_rows, logits = _post(h, pats[0::2], pats[1::2], ob, g_out_a[l][None],
                               g_out_b_pad, w_o[l][:WIDTH_A].astype(BF16), w_o_b_pad, g_xattn[l][None],
                               w_mq_s, kmem, vmem, w_mo[l].astype(BF16), g_moe[l][None], w_router_pad,
                               b_router_pad, tm=256 * POST_CHAINS)

        gates, slots, cslots, counts, tile_cnt, tile_carry, ctile_cnt, ctile_carry = _route(
            logits.reshape(t, LANES), tr=DISPATCH_CHUNK)
        counts = counts[0, :N_EXPERTS]
        blk = EXPERT_BLOCK
        padded = jnp.where(counts > 0, (counts + DISPATCH_PIECE + blk - 1) // blk * blk, 0)
        ends = jnp.cumsum(padded)
        pad_start = ends - padded
        n_blk = t * TOP_K // blk + N_EXPERTS + -(-N_EXPERTS * DISPATCH_PIECE // blk)
        block_expert = jnp.minimum(
            jnp.sum(ends[None, :] <= (jnp.arange(n_blk) * blk)[:, None], axis=1),
            N_EXPERTS - 1).astype(jnp.int32)
        n_used = (ends[-1] // blk).astype(jnp.int32)[None]
        ends0 = jnp.concatenate([jnp.zeros((1,), jnp.int32), ends.astype(jnp.int32)])
        run_start = (pad_start[None, :] + tile_carry[::8, :N_EXPERTS]).astype(jnp.int32).reshape(-1)
        run_len = tile_cnt[::8, :N_EXPERTS].reshape(-1)
        xs_rows = _dispatch(run_start, run_len, ends0, xn_rows, slots, t, n_blk * blk)
        expert_ids = jnp.arange(N_EXPERTS, dtype=jnp.int32)
        run_end = jnp.sum(jnp.where(block_expert[:, None] == expert_ids, ends[None, :], 0), axis=1)
        next_expert = jnp.where(
            run_end < ends[-1],
            jnp.minimum(jnp.sum(ends[None, :] <= run_end[:, None], axis=1), N_EXPERTS - 1), -1).astype(jnp.int32)
        block_of = block_expert[:, None] == expert_ids
        real_end = jnp.sum(jnp.where(block_of, (pad_start + counts)[None, :], 0), axis=1)
        valid_rows = jnp.clip(real_end - jnp.arange(n_blk) * blk, 0, blk).astype(jnp.int32)
        y_rows = _experts(block_expert, n_used, next_expert, valid_rows, xs_rows, w_gate_up[l], b_gate_up[l], w_down[l], b_down[l])
        gather_start = (pad_start[None, :] + ctile_carry[::8, :N_EXPERTS]).astype(jnp.int32).reshape(-1)
        gather_len = ctile_cnt[::8, :N_EXPERTS].reshape(-1)
        out = _combine(gather_start, gather_len, h2.reshape(t, d), gates, cslots, g_final[None], y_rows)
        h = out.reshape(b, s, d)
    return h
```

```python
import functools
import math

import jax
import jax.numpy as jnp
from jax import lax
from jax.experimental import pallas as pl
from jax.experimental.pallas import tpu as pltpu

F32 = jnp.float32
BF16 = jnp.bfloat16

LANES = 128
EPS = 1e-6
NEG_INF = -1e30
LOG2_E = math.log2(math.e)

N_HEADS_A = 8
HEAD_DIM_A = 64
WIDTH_A = N_HEADS_A * HEAD_DIM_A
DILATED_PATTERNS = ((128, 1), (512, 4), (2048, 16))
N_HEADS_B = 8
QK_NOPE_DIM = 64
QK_ROPE_DIM = 32
V_DIM_B = 64
Q_LORA_RANK = 256
KV_LORA_RANK = 128
ROPE_THETA = 10000.0
NUM_BUCKETS = 32
MAX_DISTANCE = 1024
N_HEADS_MEM = 4
N_EXPERTS = 32
TOP_K = 4
SWIGLU_LIMIT = 7.0
SWIGLU_ALPHA = 1.702

Q_BLOCK = 128
K_WINDOW = 256
HALF_WINDOW = (K_WINDOW - Q_BLOCK) // 2
MAX_Q_BLOCKS_PER_STEP = 4
EXPERT_BLOCK = 512
DISPATCH_CHUNK = 512
DISPATCH_PIECE = 64
COMBINE_TILE = 128
PROJ_CHAINS = 2
POST_CHAINS = 2
VMEM_LIMIT = 56 * 1024 * 1024


def _rms(x, g):
    return x * lax.rsqrt(jnp.mean(x * x, axis=-1, keepdims=True) + EPS) * g


def _dot(a, b):
    return jnp.dot(a, b, preferred_element_type=F32)


def _dot_nt(a, b):
    return lax.dot_general(a, b, (((1,), (1,)), ((), ())), preferred_element_type=F32)


def _run_skewed(chains):
    live = []
    pending = list(chains)
    while live or pending:
        if pending:
            live.append(pending.pop(0))
        for gen in list(live):
            try:
                next(gen)
            except StopIteration:
                live.remove(gen)


def _load_token_rows(ref, n_tok):
    n_chunk = ref.shape[0] // n_tok
    return jnp.concatenate([ref[pl.ds(c, n_tok, stride=n_chunk), :] for c in range(n_chunk)], axis=1)


def _pack_bf16_pairs(x):
    half = x.shape[1] // 2
    bits = lambda v: lax.bitcast_convert_type(v.astype(BF16).astype(F32), jnp.uint32)
    return bits(x[:, half:]) | (bits(x[:, :half]) >> 16)


def _unpack_bf16_pairs(w):
    lo = lax.bitcast_convert_type(w << 16, F32)
    hi = lax.bitcast_convert_type(w & jnp.uint32(0xFFFF0000), F32)
    return lo, hi


def _store_token_rows(ref, value):
    n_tok = value.shape[0]
    n_chunk = value.shape[1] // LANES
    for c in range(n_chunk):
        ref[pl.ds(c, n_tok, stride=n_chunk), :] = value[:, c * LANES:(c + 1) * LANES]


def _proj_kernel(x_ref, cs_ref, expand_ref, gmix_ref, win_ref, gq_ref, wq_ref, wqr_ref,
                 gkv_ref, wk_ref, wv_ref,
                 q1_ref, k1_ref, v1_ref, q4_ref, k4_ref, v4_ref, q16_ref, k16_ref, v16_ref,
                 qm_ref, km_ref, vm_ref, za_scr):
    tm = x_ref.shape[1]
    c0 = 3 * WIDTH_A
    c1 = c0 + Q_LORA_RANK + KV_LORA_RANK
    per_group = WIDTH_A // LANES
    rows_per_chain = tm // PROJ_CHAINS
    lane = lax.broadcasted_iota(jnp.int32, (1, LANES), 1)
    in_rope = jnp.logical_and(lane >= QK_NOPE_DIM, lane < QK_NOPE_DIM + QK_ROPE_DIM)
    ones_col = jnp.where(lane == V_DIM_B, 1.0, 0.0).astype(F32)
    scale = (QK_NOPE_DIM + QK_ROPE_DIM) ** -0.5 * LOG2_E

    def chain(ci):
        r0 = ci * rows_per_chain
        rows = slice(r0, r0 + rows_per_chain)
        xn = _rms(x_ref[0, rows, :], gmix_ref[...]).astype(BF16)
        yield
        z = _dot(xn, win_ref[...])
        yield
        for c in range(3 * per_group):
            chunk = z[:, c * LANES:(c + 1) * LANES]
            za_scr[c, rows, :] = chunk * (HEAD_DIM_A ** -0.5 * LOG2_E) if c < per_group else chunk
        for dil, refs in ((1, (q1_ref, k1_ref, v1_ref)), (4, (q4_ref, k4_ref, v4_ref)),
                          (16, (q16_ref, k16_ref, v16_ref))):
            n = rows_per_chain // dil
            for r in range(dil):
                for c in range(3 * per_group):
                    picked = za_scr[c, pl.ds(r0 + r, n, stride=dil), :].astype(BF16)
                    col = r * WIDTH_A + (c % per_group) * LANES
                    refs[c // per_group][0, r0 // dil:r0 // dil + n, col:col + LANES] = picked
        zqn = _rms(z[:, c0:c0 + Q_LORA_RANK], gq_ref[...]).astype(BF16)
        zkvn = _rms(z[:, c0 + Q_LORA_RANK:c1], gkv_ref[...]).astype(BF16)
        yield
        spread = jnp.dot(cs_ref[0, rows, :], expand_ref[...], preferred_element_type=F32,
                         precision=lax.Precision.HIGHEST)
        q = _dot(zqn, wq_ref[...])
        qr = _dot(zqn, wqr_ref[...])
        kn = _dot(zkvn, wk_ref[...])
        vv = _dot(zkvn, wv_ref[...])
        yield
        cos = jnp.where(in_rope, spread[:, :LANES], 1.0)
        sin = spread[:, LANES:]
        kpe = z[:, c1:c1 + LANES] * cos + z[:, c1 + LANES:c1 + 2 * LANES] * sin
        for h in range(N_HEADS_B):
            sl = slice(h * LANES, (h + 1) * LANES)
            qm_ref[0, h, rows, :] = ((q[:, sl] * cos + qr[:, sl] * sin) * scale).astype(BF16)
            km_ref[0, h, rows, :] = (kn[:, sl] + kpe).astype(BF16)
            vm_ref[0, h, rows, :] = (vv[:, sl] + ones_col).astype(BF16)

    _run_skewed([chain(ci) for ci in range(PROJ_CHAINS)])


def _proj(x, cos_sin, expand, g_mix, w_in_ext, g_q, wq, wqr, g_kv, wk, wv, tm):
    b, s, d = x.shape
    n_in = w_in_ext.shape[1]
    hw = N_HEADS_B * LANES
    full = lambda shape: pl.BlockSpec(shape, lambda bi, i: (0,) * len(shape))
    row = lambda w: pl.BlockSpec((1, tm, w), lambda bi, i: (bi, i, 0))
    head = pl.BlockSpec((1, N_HEADS_B, tm, LANES), lambda bi, i: (bi, 0, i, 0))
    dils = [dil for _, dil in DILATED_PATTERNS]
    view_specs = [pl.BlockSpec((1, tm // dil, dil * WIDTH_A), lambda bi, i: (bi, i, 0))
                  for dil in dils for _ in range(3)]
    view_shapes = [jax.ShapeDtypeStruct((b, s // dil, dil * WIDTH_A), BF16)
                   for dil in dils for _ in range(3)]
    return pl.pallas_call(
        _proj_kernel,
        grid=(b, s // tm),
        in_specs=[row(d), row(cos_sin.shape[2]), full(expand.shape), full((1, d)), full((d, n_in)),
                  full((1, Q_LORA_RANK)), full((Q_LORA_RANK, hw)), full((Q_LORA_RANK, hw)),
                  full((1, KV_LORA_RANK)), full((KV_LORA_RANK, hw)), full((KV_LORA_RANK, hw))],
        out_specs=view_specs + [head, head, head],
        out_shape=view_shapes + [jax.ShapeDtypeStruct((b, N_HEADS_B, s, LANES), BF16)] * 3,
        scratch_shapes=[pltpu.VMEM((3 * WIDTH_A // LANES, tm, LANES), F32)],
        compiler_params=pltpu.CompilerParams(
            dimension_semantics=("parallel", "parallel"), vmem_limit_bytes=VMEM_LIMIT),
        name="proj",
    )(x, cos_sin, expand, g_mix, w_in_ext, g_q, wq, wqr, g_kv, wk, wv)


_LOG_BUCKET_STARTS = tuple(
    next(n for n in range(8, 4096)
         if int(math.log(n / 8) / math.log(MAX_DISTANCE / 8) * 8) >= t)
    for t in range(1, 8))


def _rel_bucket(rel):
    n = jnp.abs(rel)
    large = jnp.full(rel.shape, NUM_BUCKETS // 4, jnp.int32)
    for start in _LOG_BUCKET_STARTS:
        large = large + jnp.where(n >= start, 1, 0)
    mag = jnp.where(n < NUM_BUCKETS // 4, n, large)
    return mag + jnp.where(rel > 0, NUM_BUCKETS // 2, 0)


def _window_start(j, sub_len):
    return jnp.clip(j * Q_BLOCK - HALF_WINDOW, 0, sub_len - K_WINDOW)


def _bias_kernel(off_ref, qpos_ref, kpos_ref, tbl_ref, out_ref):
    rel = kpos_ref[0] - qpos_ref[0]
    bucket = _rel_bucket(rel)
    delta = (off_ref[pl.program_id(0)] + lax.broadcasted_iota(jnp.int32, (1, K_WINDOW), 1)
             - lax.broadcasted_iota(jnp.int32, (Q_BLOCK, 1), 0))
    valid = jnp.abs(delta) <= HALF_WINDOW
    for h in range(N_HEADS_A):
        tbl = jnp.broadcast_to(tbl_ref[h:h + 1, :], (Q_BLOCK, LANES))
        bias = jnp.concatenate(
            [jnp.take_along_axis(tbl, bucket[:, c * LANES:(c + 1) * LANES], axis=1)
             for c in range(K_WINDOW // LANES)], axis=1)
        out_ref[0, h] = jnp.where(valid, bias * LOG2_E, NEG_INF)


def _bias_tiles(offs, qpos, kpos, tbl):
    n = offs.shape[0]
    grid_spec = pltpu.PrefetchScalarGridSpec(
        num_scalar_prefetch=1,
        grid=(n,),
        in_specs=[pl.BlockSpec((1, Q_BLOCK, 1), lambda t, off: (t, 0, 0)),
                  pl.BlockSpec((1, 1, K_WINDOW), lambda t, off: (t, 0, 0)),
                  pl.BlockSpec((N_HEADS_A, LANES), lambda t, off: (0, 0))],
        out_specs=pl.BlockSpec((1, N_HEADS_A, Q_BLOCK, K_WINDOW), lambda t, off: (t, 0, 0, 0)),
    )
    return pl.pallas_call(
        _bias_kernel,
        grid_spec=grid_spec,
        out_shape=jax.ShapeDtypeStruct((n, N_HEADS_A, Q_BLOCK, K_WINDOW), F32),
        compiler_params=pltpu.CompilerParams(
            dimension_semantics=("arbitrary",), vmem_limit_bytes=VMEM_LIMIT),
        name="bias_tiles",
    )(offs, qpos, kpos, tbl)


def _dilated_kernel(q_ref, k_ref, v_ref, *refs, sub_len, qb):
    bias_refs = refs[:qb]
    o_ref, lse_ref, s_scr, p_scr = refs[qb:]
    first = lax.broadcasted_iota(jnp.int32, (1, LANES), 1) < HEAD_DIM_A
    pair = lambda h: slice((h // 2) * LANES, (h // 2 + 1) * LANES)
    for sub in range(qb):
        j = pl.program_id(2) * qb + sub
        rows = slice(sub * Q_BLOCK, (sub + 1) * Q_BLOCK)
        kstart = pl.multiple_of(_window_start(j, sub_len), HALF_WINDOW)
        q = q_ref[0, rows, :]
        kw = k_ref[0, pl.ds(kstart, K_WINDOW), :]
        vw = v_ref[0, pl.ds(kstart, K_WINDOW), :]
        for h in range(N_HEADS_A):
            own = first if h % 2 == 0 else jnp.logical_not(first)
            qh = jnp.where(own, q[:, pair(h)], jnp.zeros_like(q[:, pair(h)]))
            s_scr[sub, h] = _dot_nt(qh, kw[:, pair(h)]) + bias_refs[sub][0, h]
        stats = []
        for h in range(N_HEADS_A):
            s = s_scr[sub, h]
            m = jnp.max(s, axis=1, keepdims=True)
            e = jnp.exp2(s - m)
            l = jnp.sum(e, axis=1, keepdims=True)
            p_scr[sub, h] = e.astype(BF16)
            stats.append((1.0 / l, m + jnp.log2(l)))
        for h in range(0, N_HEADS_A, 2):
            (r0, lse0), (r1, lse1) = stats[h], stats[h + 1]
            o0 = _dot(p_scr[sub, h], vw[:, pair(h)])
            o1 = _dot(p_scr[sub, h + 1], vw[:, pair(h)])
            o_ref[0, rows, pair(h)] = jnp.where(first, o0 * r0, o1 * r1)
            lse_ref[0, rows, pair(h)] = jnp.where(first, lse0, lse1)


def _dilated(qv, kv, vv, bias, bias_index, dil):
    b, sub_len, _ = qv.shape
    w = WIDTH_A
    nq = sub_len // Q_BLOCK
    qb = min(MAX_Q_BLOCKS_PER_STEP, nq)
    assert sub_len >= K_WINDOW and sub_len % (Q_BLOCK * qb) == 0
    qspec = pl.BlockSpec((1, Q_BLOCK * qb, w), lambda bi, r, j: (bi, j, r))
    kvspec = pl.BlockSpec((1, sub_len, w), lambda bi, r, j: (bi, 0, r))
    bspecs = [pl.BlockSpec((1, N_HEADS_A, Q_BLOCK, K_WINDOW),
                           lambda bi, r, j, sub=sub:
                           (bias_index(bi, r, j * qb + sub, nq), 0, 0, 0))
              for sub in range(qb)]
    return pl.pallas_call(
        functools.partial(_dilated_kernel, sub_len=sub_len, qb=qb),
        grid=(b, dil, nq // qb),
        in_specs=[qspec, kvspec, kvspec] + bspecs,
        out_specs=[qspec, qspec],
        out_shape=[jax.ShapeDtypeStruct((b, sub_len, dil * w), F32)] * 2,
        scratch_shapes=[pltpu.VMEM((qb, N_HEADS_A, Q_BLOCK, K_WINDOW), F32),
                        pltpu.VMEM((qb, N_HEADS_A, Q_BLOCK, K_WINDOW), BF16)],
        compiler_params=pltpu.CompilerParams(
            dimension_semantics=("parallel", "parallel", "arbitrary"),
            vmem_limit_bytes=VMEM_LIMIT),
        name=f"dilated{dil}",
    )(qv, kv, vv, *([bias] * qb))


def _dilated_all(views, positions, tbl):
    b, s = positions.shape
    dils = [dil for _, dil in DILATED_PATTERNS]
    consecutive = jnp.all(positions[:, 1:] - positions[:, :-1] == 1)

    def run(bias, index_fns):
        outs = []
        for g, dil in enumerate(dils):
            qv, kv, vv = views[3 * g:3 * g + 3]
            outs.extend(_dilated(qv, kv, vv, bias, index_fns[g], dil))
        return tuple(outs)

    def shared_tiles():
        offs, qpos, kpos, fns = [], [], [], []
        for g, dil in enumerate(dils):
            for off in (0, -HALF_WINDOW, -2 * HALF_WINDOW):
                offs.append(off)
                qpos.append(dil * jnp.arange(Q_BLOCK, dtype=jnp.int32))
                kpos.append(dil * (off + jnp.arange(K_WINDOW, dtype=jnp.int32)))
            fns.append(lambda bi, r, j, nq, g=g:
                       3 * g + jnp.where(j == 0, 0, jnp.where(j == nq - 1, 2, 1)))
        bias = _bias_tiles(jnp.array(offs, jnp.int32), jnp.stack(qpos)[:, :, None],
                           jnp.stack(kpos)[:, None, :], tbl)
        return run(bias, fns)

    def per_block_tiles():
        offs, qpos, kpos, fns = [], [], [], []
        base = 0
        for g, dil in enumerate(dils):
            sub_len = s // dil
            nq = sub_len // Q_BLOCK
            pos_t = positions.reshape(b, sub_len, dil).transpose(0, 2, 1).reshape(b * dil, sub_len)
            starts = _window_start(jnp.arange(nq), sub_len)
            win = starts[:, None] + jnp.arange(K_WINDOW)[None, :]
            qpos.append(pos_t.reshape(b * dil * nq, Q_BLOCK))
            kpos.append(pos_t[:, win].reshape(b * dil * nq, K_WINDOW))
            offs.append(jnp.tile(starts - jnp.arange(nq) * Q_BLOCK, b * dil))
            fns.append(lambda bi, r, j, nq, base=base, dil=dil: base + (bi * dil + r) * nq + j)
            base += b * dil * nq
        bias = _bias_tiles(jnp.concatenate(offs).astype(jnp.int32),
                           jnp.concatenate(qpos)[:, :, None], jnp.concatenate(kpos)[:, None, :], tbl)
        return run(bias, fns)

    return lax.cond(consecutive, shared_tiles, per_block_tiles)


def _mla_kernel(q_ref, k_ref, v_ref, o_ref, *, tk):
    q = q_ref[0, 0]
    tq = q.shape[0]
    n_kv = k_ref.shape[2] // tk

    def body(i, carry):
        m, acc = carry
        start = pl.multiple_of(i * tk, tk)
        k = k_ref[0, 0, pl.ds(start, tk), :]
        v = v_ref[0, 0, pl.ds(start, tk), :]
        s = _dot_nt(q, k)
        m_new = jnp.maximum(m, jnp.max(s, axis=1, keepdims=True))
        p = jnp.exp2((s - m_new).astype(BF16))
        acc = jnp.exp2(m - m_new) * acc + _dot(p, v)
        return m_new, acc

    m0 = jnp.full((tq, 1), NEG_INF, F32)
    _, acc = lax.fori_loop(0, n_kv, body, (m0, jnp.zeros((tq, LANES), F32)), unroll=True)
    lane = lax.broadcasted_iota(jnp.int32, (1, LANES), 1)
    denom = jnp.sum(jnp.where(lane == V_DIM_B, acc, 0.0), axis=1, keepdims=True)
    o_ref[0, 0] = jnp.where(lane < V_DIM_B, acc / denom, 0.0).astype(BF16)


def _mla(qm, km, vm, tq, tk):
    b, nh, s, _ = qm.shape
    qspec = pl.BlockSpec((1, 1, tq, LANES), lambda bi, h, i: (bi, h, i, 0))
    kvspec = pl.BlockSpec((1, 1, s, LANES), lambda bi, h, i: (bi, h, 0, 0))
    return pl.pallas_call(
        functools.partial(_mla_kernel, tk=tk),
        grid=(b, nh, s // tq),
        in_specs=[qspec, kvspec, kvspec],
        out_specs=qspec,
        out_shape=jax.ShapeDtypeStruct((b, nh, s, LANES), BF16),
        compiler_params=pltpu.CompilerParams(
            dimension_semantics=("parallel", "parallel", "arbitrary"),
            vmem_limit_bytes=VMEM_LIMIT),
        name="mla",
    )(qm, km, vm)


def _memkv_kernel(mem_ref, g_ref, w_ref, k_ref, v_ref):
    d = mem_ref.shape[2]
    kv = _dot(_rms(mem_ref[0], g_ref[...]).astype(BF16), w_ref[...])
    k_ref[0] = kv[:, :d].astype(BF16)
    v_ref[0] = kv[:, d:].astype(BF16)


def _memkv(mem, g_mem, w_mkv):
    b, m, d = mem.shape
    spec = pl.BlockSpec((1, m, d), lambda bi: (bi, 0, 0))
    return pl.pallas_call(
        _memkv_kernel,
        grid=(b,),
        in_specs=[spec, pl.BlockSpec((1, d), lambda bi: (0, 0)),
                  pl.BlockSpec((d, 2 * d), lambda bi: (0, 0))],
        out_specs=[spec, spec],
        out_shape=[jax.ShapeDtypeStruct((b, m, d), BF16)] * 2,
        compiler_params=pltpu.CompilerParams(
            dimension_semantics=("parallel",), vmem_limit_bytes=VMEM_LIMIT),
        name="memkv",
    )(mem, g_mem, w_mkv)


def _post_kernel(x_ref, o1_ref, o2_ref, o3_ref, l1_ref, l2_ref, l3_ref, ob_ref,
                 goa_ref, gob_ref, woa_ref, wob_ref, gx_ref, wmq_ref, km_ref, vm_ref, wmo_ref,
                 gmoe_ref, wr_ref, br_ref,
                 h_ref, xn_ref, logit_ref, *nat_scr):
    tm = x_ref.shape[1]
    n_chunk = WIDTH_A // LANES

    for view_ref, scr in zip((o2_ref, o3_ref, l2_ref, l3_ref), nat_scr):
        dil = view_ref.shape[2] // WIDTH_A
        for r in range(dil):
            for c in range(n_chunk):
                col = r * WIDTH_A + c * LANES
                scr[c, pl.ds(r, tm // dil, stride=dil), :] = view_ref[0, :, col:col + LANES]

    def natural(scr, rows):
        return jnp.concatenate([scr[c, rows, :] for c in range(n_chunk)], axis=1)

    rows_per_chain = tm // POST_CHAINS

    def chain(ci):
        rows = slice(ci * rows_per_chain, (ci + 1) * rows_per_chain)
        o1, o2, o3 = o1_ref[0, rows, :], natural(nat_scr[0], rows), natural(nat_scr[1], rows)
        l1, l2, l3 = l1_ref[0, rows, :], natural(nat_scr[2], rows), natural(nat_scr[3], rows)
        mx = jnp.maximum(jnp.maximum(l1, l2), l3)
        w1, w2, w3 = jnp.exp2(l1 - mx), jnp.exp2(l2 - mx), jnp.exp2(l3 - mx)
        oa = (w1 * o1 + w2 * o2 + w3 * o3) / (w1 + w2 + w3)
        oan = _rms(oa, goa_ref[...]).astype(BF16)
        ob = jnp.concatenate([ob_ref[0, h, rows, :] for h in range(N_HEADS_B)], axis=1).astype(F32)
        ms_b = jnp.sum(ob * ob, axis=1, keepdims=True) * (1.0 / (N_HEADS_B * V_DIM_B))
        obn = (ob * lax.rsqrt(ms_b + EPS) * gob_ref[...]).astype(BF16)
        yield
        h1 = x_ref[0, rows, :] + _dot(oan, woa_ref[...]) + _dot(obn, wob_ref[...])
        yield
        hn = _rms(h1, gx_ref[...]).astype(BF16)
        yield
        q = _dot(hn, wmq_ref[...]).astype(BF16)
        yield
        dh = q.shape[1] // N_HEADS_MEM
        heads = []
        for h in range(N_HEADS_MEM):
            sl = slice(h * dh, (h + 1) * dh)
            s = _dot_nt(q[:, sl], km_ref[0, :, sl])
            yield
            e = jnp.exp(s - jnp.max(s, axis=1, keepdims=True))
            p = e / jnp.sum(e, axis=1, keepdims=True)
            yield
            heads.append(_dot(p.astype(BF16), vm_ref[0, :, sl]))
        o = jnp.concatenate(heads, axis=1).astype(BF16)
        yield
        h2 = h1 + _dot(o, wmo_ref[...])
        h_ref[0, rows, :] = h2
        yield
        xn = _rms(h2, gmoe_ref[...])
        packed = _pack_bf16_pairs(xn)
        rows_per_token = packed.shape[1] // LANES
        _store_token_rows(xn_ref.at[pl.ds(ci * rows_per_chain * rows_per_token,
                                          rows_per_chain * rows_per_token)], packed)
        yield
        logit_ref[0, rows, :] = jnp.dot(xn, wr_ref[...], preferred_element_type=F32,
                                        precision=lax.Precision.HIGHEST) + br_ref[...]

    _run_skewed([chain(ci) for ci in range(POST_CHAINS)])


def _post(x, o_pats, lse_pats, ob, g_out_a, g_out_b_pad, w_o_a, w_o_b_pad, g_xattn, w_mq, kmem,
          vmem, w_mo, g_moe, w_router_pad, b_router_pad, tm):
    b, s, d = x.shape
    m = kmem.shape[1]
    full = lambda shape: pl.BlockSpec(shape, lambda bi, i: (0,) * len(shape))
    row = lambda w: pl.BlockSpec((1, tm, w), lambda bi, i: (bi, i, 0))
    memspec = pl.BlockSpec((1, m, d), lambda bi, i: (bi, 0, 0))
    hw = N_HEADS_B * LANES
    views = [pl.BlockSpec((1, tm // dil, dil * WIDTH_A), lambda bi, i: (bi, i, 0))
             for _, dil in DILATED_PATTERNS]
    return pl.pallas_call(
        _post_kernel,
        grid=(b, s // tm),
        in_specs=[row(d)] + views + views
        + [pl.BlockSpec((1, N_HEADS_B, tm, LANES), lambda bi, i: (bi, 0, i, 0)),
           full((1, WIDTH_A)), full((1, hw)), full((WIDTH_A, d)), full((hw, d)),
           full((1, d)), full((d, d)), memspec, memspec, full((d, d)),
           full((1, d)), full((d, LANES)), full((1, LANES))],
        out_specs=[row(d), pl.BlockSpec((tm * d // 2 // LANES, LANES), lambda bi, i: (bi * (s // tm) + i, 0)),
                   row(LANES)],
        out_shape=[jax.ShapeDtypeStruct((b, s, d), F32),
                   jax.ShapeDtypeStruct((b * s * d // 2 // LANES, LANES), jnp.uint32),
                   jax.ShapeDtypeStruct((b, s, LANES), F32)],
        scratch_shapes=[pltpu.VMEM((WIDTH_A // LANES, tm, LANES), F32)] * 4,
        compiler_params=pltpu.CompilerParams(
            dimension_semantics=("parallel", "parallel"), vmem_limit_bytes=VMEM_LIMIT),
        name="post",
    )(x, *o_pats, *lse_pats, ob, g_out_a, g_out_b_pad, w_o_a, w_o_b_pad, g_xattn, w_mq, kmem,
      vmem, w_mo, g_moe, w_router_pad, b_router_pad)


def _route_kernel(logit_ref, gate_ref, slot_ref, cslot_ref, count_ref, tcnt_ref, tcarry_ref, ccnt_ref,
                  ccarry_ref, carry_ref):
    @pl.when(pl.program_id(0) == 0)
    def _():
        carry_ref[...] = jnp.zeros_like(carry_ref)

    l = logit_ref[...]
    tr = l.shape[0]
    lane_i = lax.broadcasted_iota(jnp.int32, l.shape, 1)
    lane = lane_i.astype(F32)
    vals, idxs = [], []
    for _ in range(TOP_K):
        m = jnp.max(l, axis=1, keepdims=True)
        idx = jnp.min(jnp.where(l == m, lane, float(LANES)), axis=1, keepdims=True)
        vals.append(m)
        idxs.append(idx)
        l = jnp.where(lane == idx, -jnp.inf, l)
    exps = [jnp.exp(v - vals[0]) for v in vals]
    denom = exps[0] + exps[1] + exps[2] + exps[3]
    onehot = jnp.zeros(l.shape, F32)
    for idx in idxs:
        onehot = onehot + jnp.where(lane == idx, 1.0, 0.0)
    r = lax.broadcasted_iota(jnp.int32, (tr, tr), 0)
    c = lax.broadcasted_iota(jnp.int32, (tr, tr), 1)
    tri = jnp.where(c < r, 1.0, 0.0).astype(BF16)
    local = _dot(tri, onehot.astype(BF16))
    tile_cnt = jnp.sum(onehot, axis=0, keepdims=True)
    li = lax.broadcasted_iota(jnp.int32, (LANES, LANES), 0)
    lj = lax.broadcasted_iota(jnp.int32, (LANES, LANES), 1)
    prefix = jnp.dot(jnp.broadcast_to(tile_cnt, (8, LANES)), jnp.where(li < lj, 1.0, 0.0),
                     preferred_element_type=F32, precision=lax.Precision.HIGHEST)[0:1]
    grouped = local + prefix
    n_sub = tr // COMBINE_TILE
    sub_cnt = [jnp.sum(onehot[u * COMBINE_TILE:(u + 1) * COMBINE_TILE], axis=0, keepdims=True)
               for u in range(n_sub)]
    sub_prefix = jnp.dot(jnp.concatenate(sub_cnt + [jnp.zeros((8 - n_sub, LANES), F32)], axis=0),
                         jnp.where(li < lj, 1.0, 0.0), preferred_element_type=F32,
                         precision=lax.Precision.HIGHEST)
    shifts, sub_base = [], jnp.zeros((1, LANES), F32)
    for u in range(n_sub):
        shifts.append(jnp.broadcast_to(sub_prefix[u:u + 1] - sub_base, (COMBINE_TILE, LANES)))
        ccnt_ref[u * 8:(u + 1) * 8, :] = jnp.broadcast_to(sub_cnt[u], (8, LANES)).astype(jnp.int32)
        ccarry_ref[u * 8:(u + 1) * 8, :] = jnp.broadcast_to(carry_ref[...] + sub_base,
                                                            (8, LANES)).astype(jnp.int32)
        sub_base = sub_base + sub_cnt[u]
    sub_grouped = local + jnp.concatenate(shifts, axis=0)
    gate = jnp.zeros(l.shape, F32)
    slot = jnp.zeros(l.shape, jnp.int32)
    cslot = jnp.zeros(l.shape, jnp.int32)
    for k in range(TOP_K):
        mine = lane == idxs[k]
        sk = jnp.sum(jnp.where(mine, grouped, 0.0), axis=1, keepdims=True)
        ck = jnp.sum(jnp.where(mine, sub_grouped, 0.0), axis=1, keepdims=True)
        gate = jnp.where(lane_i == k, exps[k] / denom, gate)
        slot = jnp.where(lane_i == k, sk.astype(jnp.int32), slot)
        cslot = jnp.where(lane_i == k, ck.astype(jnp.int32), cslot)
    gate_ref[...] = gate
    slot_ref[...] = slot
    cslot_ref[...] = cslot
    tcnt_ref[...] = jnp.broadcast_to(tile_cnt, tcnt_ref.shape).astype(jnp.int32)
    tcarry_ref[...] = jnp.broadcast_to(carry_ref[...], tcarry_ref.shape).astype(jnp.int32)
    total = carry_ref[...] + tile_cnt
    carry_ref[...] = total
    count_ref[...] = total.astype(jnp.int32)


def _route(logits, tr):
    t = logits.shape[0]
    spec = pl.BlockSpec((tr, LANES), lambda i: (i, 0))
    return pl.pallas_call(
        _route_kernel,
        grid=(t // tr,),
        in_specs=[spec],
        out_specs=[spec, spec, spec, pl.BlockSpec((1, LANES), lambda i: (0, 0)),
                   pl.BlockSpec((8, LANES), lambda i: (i, 0)), pl.BlockSpec((8, LANES), lambda i: (i, 0)),
                   pl.BlockSpec((8 * tr // COMBINE_TILE, LANES), lambda i: (i, 0)),
                   pl.BlockSpec((8 * tr // COMBINE_TILE, LANES), lambda i: (i, 0))],
        out_shape=[jax.ShapeDtypeStruct((t, LANES), F32)] + [jax.ShapeDtypeStruct((t, LANES), jnp.int32)] * 2
        + [jax.ShapeDtypeStruct((1, LANES), jnp.int32)]
        + [jax.ShapeDtypeStruct((t // tr * 8, LANES), jnp.int32)] * 2
        + [jax.ShapeDtypeStruct((t // COMBINE_TILE * 8, LANES), jnp.int32)] * 2,
        scratch_shapes=[pltpu.VMEM((1, LANES), F32)],
        compiler_params=pltpu.CompilerParams(
            dimension_semantics=("arbitrary",), vmem_limit_bytes=VMEM_LIMIT),
        name="route",
    )(logits)


def _dispatch_kernel(dst_ref, cnt_ref, ends_ref, x_ref, slot_ref, xs_hbm, zbuf, stage, sem, zsem):
    c = pl.program_id(0)
    buf = lax.rem(c, 2)
    n_chunk = x_ref.shape[0] // DISPATCH_CHUNK
    blk_rows = zbuf.shape[0]
    n_staged = DISPATCH_CHUNK * TOP_K

    @pl.when(c == 0)
    def _():
        zbuf[...] = jnp.zeros_like(zbuf)
        for b in range(2):
            stage[b, pl.ds(n_staged * n_chunk, DISPATCH_PIECE * n_chunk), :] = jnp.zeros(
                (DISPATCH_PIECE * n_chunk, LANES), stage.dtype)

        def zero_copy(start):
            return pltpu.make_async_copy(
                zbuf, xs_hbm.at[pl.ds(pl.multiple_of(start * n_chunk, n_chunk), blk_rows)], zsem)

        blk = blk_rows // n_chunk
        used_end = ends_ref[N_EXPERTS]
        starts = [(ends_ref[e + 1] - ends_ref[e] >= back * blk, ends_ref[e + 1] - back * blk)
                  for e in range(N_EXPERTS) for back in (1, 2)]
        starts += [((used_end + (j + 1) * blk) * n_chunk <= xs_hbm.shape[0], used_end + j * blk)
                   for j in range(N_EXPERTS)]
        for cond, start in starts:
            @pl.when(cond)
            def _():
                zero_copy(start).start()
        for cond, start in starts:
            @pl.when(cond)
            def _():
                zero_copy(start).wait()

    x_lo, x_hi = _unpack_bf16_pairs(_load_token_rows(x_ref, DISPATCH_CHUNK))
    x_lo, x_hi = x_lo.astype(BF16), x_hi.astype(BF16)
    slot_t = jnp.transpose(slot_ref[...].astype(F32))
    bits = lambda v: lax.bitcast_convert_type(v, jnp.uint32)
    for part in range(TOP_K):
        j = (lax.broadcasted_iota(jnp.int32, (DISPATCH_CHUNK, 1), 0) + part * DISPATCH_CHUNK).astype(F32)
        pick = jnp.zeros((DISPATCH_CHUNK, DISPATCH_CHUNK), F32)
        for k in range(TOP_K):
            pick = jnp.where(slot_t[k:k + 1, :] == j, 1.0, pick)
        pick = pick.astype(BF16)
        packed = bits(_dot(pick, x_hi)) | (bits(_dot(pick, x_lo)) >> 16)
        _store_token_rows(
            stage.at[buf, pl.ds(part * DISPATCH_CHUNK * n_chunk, DISPATCH_CHUNK * n_chunk)], packed)

    def piece(src_tok, dst_tok):
        rows = DISPATCH_PIECE * n_chunk
        return pltpu.make_async_copy(
            stage.at[buf, pl.ds(pl.multiple_of(src_tok * n_chunk, n_chunk), rows)],
            xs_hbm.at[pl.ds(pl.multiple_of(dst_tok * n_chunk, n_chunk), rows)], sem)

    def drain(step):
        n_pieces = 0
        for e in range(N_EXPERTS):
            n_pieces = n_pieces + (cnt_ref[step * N_EXPERTS + e] + DISPATCH_PIECE - 1) // DISPATCH_PIECE

        def wait_one(i, carry):
            piece(0, 0).wait()
            return carry
        lax.fori_loop(0, n_pieces, wait_one, 0)

    @pl.when(c > 0)
    def _():
        drain(c - 1)

    staged = 0
    for e in range(N_EXPERTS):
        cnt = cnt_ref[c * N_EXPERTS + e]
        dst = dst_ref[c * N_EXPERTS + e]
        for p in range(DISPATCH_CHUNK // DISPATCH_PIECE):
            @pl.when(p * DISPATCH_PIECE < cnt)
            def _():
                piece(staged + p * DISPATCH_PIECE, dst + p * DISPATCH_PIECE).start()
        staged = staged + cnt

    @pl.when(c == pl.num_programs(0) - 1)
    def _():
        drain(c)


def _dispatch(dst, cnt, ends0, xn_rows, slots, n_tok, n_slots):
    n_chunk = xn_rows.shape[0] // n_tok
    staged_rows = (DISPATCH_CHUNK * TOP_K + DISPATCH_PIECE) * n_chunk
    grid_spec = pltpu.PrefetchScalarGridSpec(
        num_scalar_prefetch=3,
        grid=(n_tok // DISPATCH_CHUNK,),
        in_specs=[pl.BlockSpec((DISPATCH_CHUNK * n_chunk, LANES), lambda c, *_: (c, 0)),
                  pl.BlockSpec((DISPATCH_CHUNK, LANES), lambda c, *_: (c, 0))],
        out_specs=pl.BlockSpec(memory_space=pl.ANY),
        scratch_shapes=[pltpu.VMEM((EXPERT_BLOCK * n_chunk, LANES), xn_rows.dtype),
                        pltpu.VMEM((2, staged_rows, LANES), xn_rows.dtype),
                        pltpu.SemaphoreType.DMA(()), pltpu.SemaphoreType.DMA(())],
    )
    return pl.pallas_call(
        _dispatch_kernel,
        grid_spec=grid_spec,
        out_shape=jax.ShapeDtypeStruct((n_slots * n_chunk, LANES), xn_rows.dtype),
        compiler_params=pltpu.CompilerParams(
            dimension_semantics=("arbitrary",), vmem_limit_bytes=VMEM_LIMIT),
        name="dispatch",
    )(dst, cnt, ends0, xn_rows, slots)


def _expert_kernel(be_ref, nused_ref, next_ref, valid_ref, xs_ref, wgu_hbm, bgu_ref, wd_hbm, bd_ref, y_ref,
                   wgu_f32, wd_f32, wgu_bf, wd_bf, sems):
    i = pl.program_id(0)
    n_chunk = xs_ref.shape[0] // EXPERT_BLOCK
    half = EXPERT_BLOCK // 2

    def mlp(n_rows):
        part = pl.ds(0, n_rows * n_chunk)
        x_lo, x_hi = _unpack_bf16_pairs(_load_token_rows(xs_ref.at[part], n_rows))
        x = jnp.concatenate([x_lo.astype(BF16), x_hi.astype(BF16)], axis=1)
        gu = _dot(x, wgu_bf[...]) + bgu_ref[0]
        de = gu.shape[1] // 2
        gate = jnp.minimum(gu[:, :de], SWIGLU_LIMIT)
        up = jnp.clip(gu[:, de:], -SWIGLU_LIMIT, SWIGLU_LIMIT)
        hmid = (up + 1.0) * (gate * jax.nn.sigmoid(SWIGLU_ALPHA * gate))
        y = _dot(hmid.astype(BF16), wd_bf[...]) + bd_ref[0]
        _store_token_rows(y_ref.at[part], _pack_bf16_pairs(y))

    def fetch(e):
        return (pltpu.make_async_copy(wgu_hbm.at[e], wgu_f32, sems.at[0]),
                pltpu.make_async_copy(wd_hbm.at[e], wd_f32, sems.at[1]))

    @pl.when(i == 0)
    def _():
        for copy in fetch(be_ref[0]):
            copy.start()

    @pl.when(i < nused_ref[0])
    def _():
        @pl.when(jnp.logical_or(i == 0, be_ref[i] != be_ref[jnp.maximum(i - 1, 0)]))
        def _():
            for copy in fetch(be_ref[i]):
                copy.wait()
            wgu_bf[...] = wgu_f32[...].astype(BF16)
            wd_bf[...] = wd_f32[...].astype(BF16)

            @pl.when(next_ref[i] >= 0)
            def _():
                for copy in fetch(next_ref[i]):
                    copy.start()

        valid = valid_ref[i]

        @pl.when(valid > half)
        def _():
            mlp(EXPERT_BLOCK)

        @pl.when(jnp.logical_and(valid > 0, valid <= half))
        def _():
            mlp(half)
            y_ref[pl.ds(half * n_chunk, half * n_chunk), :] = jnp.zeros((half * n_chunk, LANES), y_ref.dtype)

    @pl.when(jnp.logical_or(i >= nused_ref[0], valid_ref[i] == 0))
    def _():
        y_ref[...] = jnp.zeros_like(y_ref)


def _experts(block_expert, n_used, next_expert, valid_rows, xs_rows, w_gate_up, b_gate_up, w_down, b_down):
    d, de2 = w_gate_up.shape[1:]
    n_chunk = d // 2 // LANES
    blk = EXPERT_BLOCK
    n_blk = xs_rows.shape[0] // (blk * n_chunk)
    grid_spec = pltpu.PrefetchScalarGridSpec(
        num_scalar_prefetch=4,
        grid=(n_blk,),
        in_specs=[pl.BlockSpec((blk * n_chunk, LANES),
                               lambda i, be, nu, *_: (jnp.maximum(jnp.minimum(i, nu[0] - 1), 0), 0)),
                  pl.BlockSpec(memory_space=pl.ANY),
                  pl.BlockSpec((1, 1, de2), lambda i, be, *_: (be[i], 0, 0)),
                  pl.BlockSpec(memory_space=pl.ANY),
                  pl.BlockSpec((1, 1, d), lambda i, be, *_: (be[i], 0, 0))],
        out_specs=pl.BlockSpec((blk * n_chunk, LANES), lambda i, *_: (i, 0)),
        scratch_shapes=[pltpu.VMEM((d, de2), F32), pltpu.VMEM((de2 // 2, d), F32),
                        pltpu.VMEM((d, de2), BF16), pltpu.VMEM((de2 // 2, d), BF16),
                        pltpu.SemaphoreType.DMA((2,))],
    )
    return pl.pallas_call(
        _expert_kernel,
        grid_spec=grid_spec,
        out_shape=jax.ShapeDtypeStruct(xs_rows.shape, xs_rows.dtype),
        compiler_params=pltpu.CompilerParams(
            dimension_semantics=("arbitrary",), vmem_limit_bytes=VMEM_LIMIT),
        name="experts",
    )(block_expert, n_used, next_expert, valid_rows, xs_rows, w_gate_up, b_gate_up.reshape(N_EXPERTS, 1, de2),
      w_down, b_down.reshape(N_EXPERTS, 1, d))


def _combine_kernel(start_ref, len_ref, h_ref, gate_ref, slot_ref, gfin_ref, y_hbm, out_ref, stage, sems):
    i = pl.program_id(0)
    n = pl.num_programs(0)
    buf = lax.rem(i, 2)
    tc = h_ref.shape[0]
    n_staged = tc * TOP_K
    n_chunk = stage.shape[1] // n_staged

    def issue(tile, b):
        staged = 0
        for e in range(N_EXPERTS):
            run_len = len_ref[tile * N_EXPERTS + e]
            run_start = start_ref[tile * N_EXPERTS + e]
            for bit in reversed(range(COMBINE_TILE.bit_length())):
                size = 1 << bit
                done = (run_len >> (bit + 1)) << (bit + 1)

                @pl.when((run_len >> bit) & 1 == 1)
                def _():
                    pltpu.make_async_copy(
                        y_hbm.at[pl.ds(pl.multiple_of((run_start + done) * n_chunk, n_chunk), size * n_chunk)],
                        stage.at[b, pl.ds(pl.multiple_of((staged + done) * n_chunk, n_chunk), size * n_chunk)],
                        sems.at[b]).start()
            staged = staged + run_len

    @pl.when(i == 0)
    def _():
        issue(0, 0)

    @pl.when(i + 1 < n)
    def _():
        issue(i + 1, 1 - buf)

    pltpu.make_async_copy(y_hbm.at[pl.ds(0, n_staged * n_chunk)], stage.at[buf], sems.at[buf]).wait()

    y_lo, y_hi = _unpack_bf16_pairs(_load_token_rows(stage.at[buf], n_staged))
    y_lo, y_hi = y_lo.astype(BF16), y_hi.astype(BF16)
    gate = gate_ref[...]
    slot = slot_ref[...].astype(F32)
    j = lax.broadcasted_iota(jnp.int32, (1, n_staged), 1).astype(F32)
    pick = jnp.zeros((tc, n_staged), F32)
    for k in range(TOP_K):
        pick = jnp.where(slot[:, k:k + 1] == j, gate[:, k:k + 1], pick)
    pick_hi = pick.astype(BF16)
    pick_lo = (pick - pick_hi.astype(F32)).astype(BF16)
    moe = jnp.concatenate([_dot(pick_hi, y_lo) + _dot(pick_lo, y_lo),
                           _dot(pick_hi, y_hi) + _dot(pick_lo, y_hi)], axis=1)
    out_ref[...] = _rms(h_ref[...] + moe, gfin_ref[...])


def _combine(run_start, run_len, h2, gates, slots, g_final, y_rows):
    t, d = h2.shape
    tc = COMBINE_TILE
    n_chunk = d // 2 // LANES
    grid_spec = pltpu.PrefetchScalarGridSpec(
        num_scalar_prefetch=2,
        grid=(t // tc,),
        in_specs=[pl.BlockSpec((tc, d), lambda i, *_: (i, 0)),
                  pl.BlockSpec((tc, LANES), lambda i, *_: (i, 0)),
                  pl.BlockSpec((tc, LANES), lambda i, *_: (i, 0)),
                  pl.BlockSpec((1, d), lambda i, *_: (0, 0)),
                  pl.BlockSpec(memory_space=pl.ANY)],
        out_specs=pl.BlockSpec((tc, d), lambda i, *_: (i, 0)),
        scratch_shapes=[pltpu.VMEM((2, tc * TOP_K * n_chunk, LANES), y_rows.dtype),
                        pltpu.SemaphoreType.DMA((2,))],
    )
    return pl.pallas_call(
        _combine_kernel,
        grid_spec=grid_spec,
        out_shape=jax.ShapeDtypeStruct((t, d), F32),
        compiler_params=pltpu.CompilerParams(
            dimension_semantics=("arbitrary",), vmem_limit_bytes=VMEM_LIMIT),
        name="combine",
    )(run_start, run_len, h2, gates, slots, g_final, y_rows)


def _pad_heads(w, n_heads, width, offset=0):
    k = w.shape[0]
    w = w.reshape(k, n_heads, width)
    w = jnp.pad(w, ((0, 0), (0, 0), (offset, LANES - width - offset)))
    return w.reshape(k, n_heads * LANES)


def _rot_cols(w):
    half = w.shape[-1] // 2
    return jnp.concatenate([-w[..., half:], w[..., :half]], axis=-1)


def kernel(x, mem, positions, g_mix, w_in, g_q_a, w_q_b, g_kv_a, w_kv_b, rel_bias, g_out_a, g_out_b,
           w_o, g_xattn, g_mem, w_mq, w_mkv, w_mo, g_moe, w_router, b_router, w_gate_up, b_gate_up,
           w_down, b_down, g_final):
    b, s, d = x.shape
    t = b * s
    assert g_mix.shape[0] == 1, "single-layer block: the final norm is fused into the last stage"

    inv_freq = ROPE_THETA ** (-jnp.arange(0, QK_ROPE_DIM, 2, dtype=F32) / QK_ROPE_DIM)
    ang = positions.astype(F32)[..., None] * inv_freq
    cos_sin = jnp.concatenate([jnp.cos(ang), jnp.sin(ang)], axis=-1)
    half = QK_ROPE_DIM // 2
    src = jnp.arange(QK_ROPE_DIM)[:, None]
    lane = jnp.arange(LANES)[None, :]
    on_rope = (lane >= QK_NOPE_DIM) & (lane < QK_NOPE_DIM + QK_ROPE_DIM)
    place = ((lane - QK_NOPE_DIM) % half == src % half) & on_rope
    expand = jnp.concatenate([place & (src < half), place & (src >= half)], axis=1).astype(F32)
    tbl = jnp.pad(rel_bias.T.astype(F32), ((0, 0), (0, LANES - NUM_BUCKETS)))

    h = x
    for l in range(1):
        c0 = 3 * WIDTH_A + Q_LORA_RANK + KV_LORA_RANK
        w_kpe = w_in[l][:, c0:c0 + QK_ROPE_DIM]
        place = lambda w: jnp.pad(w, ((0, 0), (QK_NOPE_DIM, LANES - QK_NOPE_DIM - QK_ROPE_DIM)))
        w_in_ext = jnp.concatenate([w_in[l][:, :c0], place(w_kpe), place(_rot_cols(w_kpe))], 1).astype(BF16)
        dq = QK_NOPE_DIM + QK_ROPE_DIM
        wq3 = w_q_b[l].reshape(Q_LORA_RANK, N_HEADS_B, dq)
        wq = _pad_heads(wq3.reshape(Q_LORA_RANK, -1), N_HEADS_B, dq).astype(BF16)
        wq_rot3 = jnp.concatenate([jnp.zeros_like(wq3[..., :QK_NOPE_DIM]), _rot_cols(wq3[..., QK_NOPE_DIM:])], -1)
        wqr = _pad_heads(wq_rot3.reshape(Q_LORA_RANK, -1), N_HEADS_B, dq).astype(BF16)
        wkv3 = w_kv_b[l].reshape(KV_LORA_RANK, N_HEADS_B, QK_NOPE_DIM + V_DIM_B)
        wk = _pad_heads(wkv3[..., :QK_NOPE_DIM].reshape(KV_LORA_RANK, -1), N_HEADS_B, QK_NOPE_DIM).astype(BF16)
        wv = _pad_heads(wkv3[..., QK_NOPE_DIM:].reshape(KV_LORA_RANK, -1), N_HEADS_B, V_DIM_B).astype(BF16)

        *views, qm, km, vm = _proj(h, cos_sin, expand, g_mix[l][None], w_in_ext, g_q_a[l][None], wq, wqr,
                                   g_kv_a[l][None], wk, wv, tm=256 * PROJ_CHAINS)
        pats = _dilated_all(views, positions, tbl)
        ob = _mla(qm, km, vm, tq=1024, tk=512)
        kmem, vmem = _memkv(mem, g_mem[l][None], w_mkv[l].astype(BF16))

        g_out_b_pad = _pad_heads(g_out_b[l][None], N_HEADS_B, V_DIM_B)
        w_o_b_pad = _pad_heads(w_o[l][WIDTH_A:].T, N_HEADS_B, V_DIM_B).T.astype(BF16)
        w_mq_s = (w_mq[l] * ((d // N_HEADS_MEM) ** -0.5)).astype(BF16)
        w_router_pad = jnp.pad(w_router[l], ((0, 0), (0, LANES - N_EXPERTS)))
        b_router_pad = jnp.pad(b_router[l][None], ((0, 0), (0, LANES - N_EXPERTS)), constant_values=NEG_INF)
        h2, xn_rows, logits = _post(h, pats[0::2], pats[1::2], ob, g_out_a[l][None],
                               g_out_b_pad, w_o[l][:WIDTH_A].astype(BF16), w_o_b_pad, g_xattn[l][None],
                               w_mq_s, kmem, vmem, w_mo[l].astype(BF16), g_moe[l][None], w_router_pad,
                               b_router_pad, tm=256 * POST_CHAINS)

        gates, slots, cslots, counts, tile_cnt, tile_carry, ctile_cnt, ctile_carry = _route(
            logits.reshape(t, LANES), tr=DISPATCH_CHUNK)
        counts = counts[0, :N_EXPERTS]
        blk = EXPERT_BLOCK
        padded = jnp.where(counts > 0, (counts + DISPATCH_PIECE + blk - 1) // blk * blk, 0)
        ends = jnp.cumsum(padded)
        pad_start = ends - padded
        n_blk = t * TOP_K // blk + N_EXPERTS + -(-N_EXPERTS * DISPATCH_PIECE // blk)
        block_expert = jnp.minimum(
            jnp.sum(ends[None, :] <= (jnp.arange(n_blk) * blk)[:, None], axis=1),
            N_EXPERTS - 1).astype(jnp.int32)
        n_used = (ends[-1] // blk).astype(jnp.int32)[None]
        ends0 = jnp.concatenate([jnp.zeros((1,), jnp.int32), ends.astype(jnp.int32)])
        run_start = (pad_start[None, :] + tile_carry[::8, :N_EXPERTS]).astype(jnp.int32).reshape(-1)
        run_len = tile_cnt[::8, :N_EXPERTS].reshape(-1)
        xs_rows = _dispatch(run_start, run_len, ends0, xn_rows, slots, t, n_blk * blk)
        expert_ids = jnp.arange(N_EXPERTS, dtype=jnp.int32)
        run_end = jnp.sum(jnp.where(block_expert[:, None] == expert_ids, ends[None, :], 0), axis=1)
        next_expert = jnp.where(
            run_end < ends[-1],
            jnp.minimum(jnp.sum(ends[None, :] <= run_end[:, None], axis=1), N_EXPERTS - 1), -1).astype(jnp.int32)
        block_of = block_expert[:, None] == expert_ids
        real_end = jnp.sum(jnp.where(block_of, (pad_start + counts)[None, :], 0), axis=1)
        valid_rows = jnp.clip(real_end - jnp.arange(n_blk) * blk, 0, blk).astype(jnp.int32)
        y_rows = _experts(block_expert, n_used, next_expert, valid_rows, xs_rows, w_gate_up[l], b_gate_up[l], w_down[l], b_down[l])
        gather_start = (pad_start[None, :] + ctile_carry[::8, :N_EXPERTS]).astype(jnp.int32).reshape(-1)
        gather_len = ctile_cnt[::8, :N_EXPERTS].reshape(-1)
        out = _combine(gather_start, gather_len, h2.reshape(t, d), gates, cslots, g_final[None], y_rows)
        h = out.reshape(b, s, d)
    return h
```

```python
import functools
import math

import jax
import jax.numpy as jnp
from jax import lax
from jax.experimental import pallas as pl
from jax.experimental.pallas import tpu as pltpu

F32 = jnp.float32
BF16 = jnp.bfloat16

LANES = 128
EPS = 1e-6
NEG_INF = -1e30
LOG2_E = math.log2(math.e)

N_HEADS_A = 8
HEAD_DIM_A = 64
WIDTH_A = N_HEADS_A * HEAD_DIM_A
DILATED_PATTERNS = ((128, 1), (512, 4), (2048, 16))
N_HEADS_B = 8
QK_NOPE_DIM = 64
QK_ROPE_DIM = 32
V_DIM_B = 64
Q_LORA_RANK = 256
KV_LORA_RANK = 128
ROPE_THETA = 10000.0
NUM_BUCKETS = 32
MAX_DISTANCE = 1024
N_HEADS_MEM = 4
N_EXPERTS = 32
TOP_K = 4
SWIGLU_LIMIT = 7.0
SWIGLU_ALPHA = 1.702

Q_BLOCK = 128
K_WINDOW = 256
HALF_WINDOW = (K_WINDOW - Q_BLOCK) // 2
MAX_Q_BLOCKS_PER_STEP = 4
EXPERT_BLOCK = 512
DISPATCH_CHUNK = 512
DISPATCH_PIECE = 64
COMBINE_TILE = 128
PROJ_CHAINS = 2
POST_CHAINS = 2
VMEM_LIMIT = 56 * 1024 * 1024


def _rms(x, g):
    return x * lax.rsqrt(jnp.mean(x * x, axis=-1, keepdims=True) + EPS) * g


def _dot(a, b):
    return jnp.dot(a, b, preferred_element_type=F32)


def _dot_nt(a, b):
    return lax.dot_general(a, b, (((1,), (1,)), ((), ())), preferred_element_type=F32)


def _run_skewed(chains):
    live = []
    pending = list(chains)
    while live or pending:
        if pending:
            live.append(pending.pop(0))
        for gen in list(live):
            try:
                next(gen)
            except StopIteration:
                live.remove(gen)


def _load_token_rows(ref, n_tok):
    n_chunk = ref.shape[0] // n_tok
    return jnp.concatenate([ref[pl.ds(c, n_tok, stride=n_chunk), :] for c in range(n_chunk)], axis=1)


def _pack_bf16_pairs(x):
    half = x.shape[1] // 2
    bits = lambda v: lax.bitcast_convert_type(v.astype(BF16).astype(F32), jnp.uint32)
    return bits(x[:, half:]) | (bits(x[:, :half]) >> 16)


def _unpack_bf16_pairs(w):
    lo = lax.bitcast_convert_type(w << 16, F32)
    hi = lax.bitcast_convert_type(w & jnp.uint32(0xFFFF0000), F32)
    return lo, hi


def _store_token_rows(ref, value):
    n_tok = value.shape[0]
    n_chunk = value.shape[1] // LANES
    for c in range(n_chunk):
        ref[pl.ds(c, n_tok, stride=n_chunk), :] = value[:, c * LANES:(c + 1) * LANES]


def _proj_kernel(x_ref, cs_ref, expand_ref, gmix_ref, win_ref, gq_ref, wq_ref, wqr_ref,
                 gkv_ref, wk_ref, wv_ref,
                 q1_ref, k1_ref, v1_ref, q4_ref, k4_ref, v4_ref, q16_ref, k16_ref, v16_ref,
                 qm_ref, km_ref, vm_ref, za_scr):
    tm = x_ref.shape[1]
    c0 = 3 * WIDTH_A
    c1 = c0 + Q_LORA_RANK + KV_LORA_RANK
    per_group = WIDTH_A // LANES
    rows_per_chain = tm // PROJ_CHAINS
    lane = lax.broadcasted_iota(jnp.int32, (1, LANES), 1)
    in_rope = jnp.logical_and(lane >= QK_NOPE_DIM, lane < QK_NOPE_DIM + QK_ROPE_DIM)
    ones_col = jnp.where(lane == V_DIM_B, 1.0, 0.0).astype(F32)
    scale = (QK_NOPE_DIM + QK_ROPE_DIM) ** -0.5 * LOG2_E

    def chain(ci):
        r0 = ci * rows_per_chain
        rows = slice(r0, r0 + rows_per_chain)
        xn = _rms(x_ref[0, rows, :], gmix_ref[...]).astype(BF16)
        yield
        z = _dot(xn, win_ref[...])
        yield
        for c in range(3 * per_group):
            chunk = z[:, c * LANES:(c + 1) * LANES]
            za_scr[c, rows, :] = chunk * (HEAD_DIM_A ** -0.5 * LOG2_E) if c < per_group else chunk
        for dil, refs in ((1, (q1_ref, k1_ref, v1_ref)), (4, (q4_ref, k4_ref, v4_ref)),
                          (16, (q16_ref, k16_ref, v16_ref))):
            n = rows_per_chain // dil
            for r in range(dil):
                for c in range(3 * per_group):
                    picked = za_scr[c, pl.ds(r0 + r, n, stride=dil), :].astype(BF16)
                    col = r * WIDTH_A + (c % per_group) * LANES
                    refs[c // per_group][0, r0 // dil:r0 // dil + n, col:col + LANES] = picked
        zqn = _rms(z[:, c0:c0 + Q_LORA_RANK], gq_ref[...]).astype(BF16)
        zkvn = _rms(z[:, c0 + Q_LORA_RANK:c1], gkv_ref[...]).astype(BF16)
        yield
        spread = jnp.dot(cs_ref[0, rows, :], expand_ref[...], preferred_element_type=F32,
                         precision=lax.Precision.HIGHEST)
        q = _dot(zqn, wq_ref[...])
        qr = _dot(zqn, wqr_ref[...])
        kn = _dot(zkvn, wk_ref[...])
        vv = _dot(zkvn, wv_ref[...])
        yield
        cos = jnp.where(in_rope, spread[:, :LANES], 1.0)
        sin = spread[:, LANES:]
        kpe = z[:, c1:c1 + LANES] * cos + z[:, c1 + LANES:c1 + 2 * LANES] * sin
        for h in range(N_HEADS_B):
            sl = slice(h * LANES, (h + 1) * LANES)
            qm_ref[0, h, rows, :] = ((q[:, sl] * cos + qr[:, sl] * sin) * scale).astype(BF16)
            km_ref[0, h, rows, :] = (kn[:, sl] + kpe).astype(BF16)
            vm_ref[0, h, rows, :] = (vv[:, sl] + ones_col).astype(BF16)

    _run_skewed([chain(ci) for ci in range(PROJ_CHAINS)])


def _proj(x, cos_sin, expand, g_mix, w_in_ext, g_q, wq, wqr, g_kv, wk, wv, tm):
    b, s, d = x.shape
    n_in = w_in_ext.shape[1]
    hw = N_HEADS_B * LANES
    full = lambda shape: pl.BlockSpec(shape, lambda bi, i: (0,) * len(shape))
    row = lambda w: pl.BlockSpec((1, tm, w), lambda bi, i: (bi, i, 0))
    head = pl.BlockSpec((1, N_HEADS_B, tm, LANES), lambda bi, i: (bi, 0, i, 0))
    dils = [dil for _, dil in DILATED_PATTERNS]
    view_specs = [pl.BlockSpec((1, tm // dil, dil * WIDTH_A), lambda bi, i: (bi, i, 0))
                  for dil in dils for _ in range(3)]
    view_shapes = [jax.ShapeDtypeStruct((b, s // dil, dil * WIDTH_A), BF16)
                   for dil in dils for _ in range(3)]
    return pl.pallas_call(
        _proj_kernel,
        grid=(b, s // tm),
        in_specs=[row(d), row(cos_sin.shape[2]), full(expand.shape), full((1, d)), full((d, n_in)),
                  full((1, Q_LORA_RANK)), full((Q_LORA_RANK, hw)), full((Q_LORA_RANK, hw)),
                  full((1, KV_LORA_RANK)), full((KV_LORA_RANK, hw)), full((KV_LORA_RANK, hw))],
        out_specs=view_specs + [head, head, head],
        out_shape=view_shapes + [jax.ShapeDtypeStruct((b, N_HEADS_B, s, LANES), BF16)] * 3,
        scratch_shapes=[pltpu.VMEM((3 * WIDTH_A // LANES, tm, LANES), F32)],
        compiler_params=pltpu.CompilerParams(
            dimension_semantics=("parallel", "parallel"), vmem_limit_bytes=VMEM_LIMIT),
        name="proj",
    )(x, cos_sin, expand, g_mix, w_in_ext, g_q, wq, wqr, g_kv, wk, wv)


_LOG_BUCKET_STARTS = tuple(
    next(n for n in range(8, 4096)
         if int(math.log(n / 8) / math.log(MAX_DISTANCE / 8) * 8) >= t)
    for t in range(1, 8))


def _rel_bucket(rel):
    n = jnp.abs(rel)
    large = jnp.full(rel.shape, NUM_BUCKETS // 4, jnp.int32)
    for start in _LOG_BUCKET_STARTS:
        large = large + jnp.where(n >= start, 1, 0)
    mag = jnp.where(n < NUM_BUCKETS // 4, n, large)
    return mag + jnp.where(rel > 0, NUM_BUCKETS // 2, 0)


def _window_start(j, sub_len):
    return jnp.clip(j * Q_BLOCK - HALF_WINDOW, 0, sub_len - K_WINDOW)


def _bias_kernel(off_ref, qpos_ref, kpos_ref, tbl_ref, out_ref):
    rel = kpos_ref[0] - qpos_ref[0]
    bucket = _rel_bucket(rel)
    delta = (off_ref[pl.program_id(0)] + lax.broadcasted_iota(jnp.int32, (1, K_WINDOW), 1)
             - lax.broadcasted_iota(jnp.int32, (Q_BLOCK, 1), 0))
    valid = jnp.abs(delta) <= HALF_WINDOW
    for h in range(N_HEADS_A):
        tbl = jnp.broadcast_to(tbl_ref[h:h + 1, :], (Q_BLOCK, LANES))
        bias = jnp.concatenate(
            [jnp.take_along_axis(tbl, bucket[:, c * LANES:(c + 1) * LANES], axis=1)
             for c in range(K_WINDOW // LANES)], axis=1)
        out_ref[0, h] = jnp.where(valid, bias * LOG2_E, NEG_INF)


def _bias_tiles(offs, qpos, kpos, tbl):
    n = offs.shape[0]
    grid_spec = pltpu.PrefetchScalarGridSpec(
        num_scalar_prefetch=1,
        grid=(n,),
        in_specs=[pl.BlockSpec((1, Q_BLOCK, 1), lambda t, off: (t, 0, 0)),
                  pl.BlockSpec((1, 1, K_WINDOW), lambda t, off: (t, 0, 0)),
                  pl.BlockSpec((N_HEADS_A, LANES), lambda t, off: (0, 0))],
        out_specs=pl.BlockSpec((1, N_HEADS_A, Q_BLOCK, K_WINDOW), lambda t, off: (t, 0, 0, 0)),
    )
    return pl.pallas_call(
        _bias_kernel,
        grid_spec=grid_spec,
        out_shape=jax.ShapeDtypeStruct((n, N_HEADS_A, Q_BLOCK, K_WINDOW), F32),
        compiler_params=pltpu.CompilerParams(
            dimension_semantics=("arbitrary",), vmem_limit_bytes=VMEM_LIMIT),
        name="bias_tiles",
    )(offs, qpos, kpos, tbl)


def _dilated_kernel(q_ref, k_ref, v_ref, *refs, sub_len, qb):
    bias_refs = refs[:qb]
    o_ref, lse_ref, s_scr, p_scr = refs[qb:]
    first = lax.broadcasted_iota(jnp.int32, (1, LANES), 1) < HEAD_DIM_A
    pair = lambda h: slice((h // 2) * LANES, (h // 2 + 1) * LANES)
    for sub in range(qb):
        j = pl.program_id(2) * qb + sub
        rows = slice(sub * Q_BLOCK, (sub + 1) * Q_BLOCK)
        kstart = pl.multiple_of(_window_start(j, sub_len), HALF_WINDOW)
        q = q_ref[0, rows, :]
        kw = k_ref[0, pl.ds(kstart, K_WINDOW), :]
        vw = v_ref[0, pl.ds(kstart, K_WINDOW), :]
        for h in range(N_HEADS_A):
            own = first if h % 2 == 0 else jnp.logical_not(first)
            qh = jnp.where(own, q[:, pair(h)], jnp.zeros_like(q[:, pair(h)]))
            s_scr[sub, h] = _dot_nt(qh, kw[:, pair(h)]) + bias_refs[sub][0, h]
        stats = []
        for h in range(N_HEADS_A):
            s = s_scr[sub, h]
            m = jnp.max(s, axis=1, keepdims=True)
            e = jnp.exp2(s - m)
            l = jnp.sum(e, axis=1, keepdims=True)
            p_scr[sub, h] = e.astype(BF16)
            stats.append((1.0 / l, m + jnp.log2(l)))
        for h in range(0, N_HEADS_A, 2):
            (r0, lse0), (r1, lse1) = stats[h], stats[h + 1]
            o0 = _dot(p_scr[sub, h], vw[:, pair(h)])
            o1 = _dot(p_scr[sub, h + 1], vw[:, pair(h)])
            o_ref[0, rows, pair(h)] = jnp.where(first, o0 * r0, o1 * r1)
            lse_ref[0, rows, pair(h)] = jnp.where(first, lse0, lse1)


def _dilated(qv, kv, vv, bias, bias_index, dil):
    b, sub_len, _ = qv.shape
    w = WIDTH_A
    nq = sub_len // Q_BLOCK
    qb = min(MAX_Q_BLOCKS_PER_STEP, nq)
    assert sub_len >= K_WINDOW and sub_len % (Q_BLOCK * qb) == 0
    qspec = pl.BlockSpec((1, Q_BLOCK * qb, w), lambda bi, r, j: (bi, j, r))
    kvspec = pl.BlockSpec((1, sub_len, w), lambda bi, r, j: (bi, 0, r))
    bspecs = [pl.BlockSpec((1, N_HEADS_A, Q_BLOCK, K_WINDOW),
                           lambda bi, r, j, sub=sub:
                           (bias_index(bi, r, j * qb + sub, nq), 0, 0, 0))
              for sub in range(qb)]
    return pl.pallas_call(
        functools.partial(_dilated_kernel, sub_len=sub_len, qb=qb),
        grid=(b, dil, nq // qb),
        in_specs=[qspec, kvspec, kvspec] + bspecs,
        out_specs=[qspec, qspec],
        out_shape=[jax.ShapeDtypeStruct((b, sub_len, dil * w), F32)] * 2,
        scratch_shapes=[pltpu.VMEM((qb, N_HEADS_A, Q_BLOCK, K_WINDOW), F32),
                        pltpu.VMEM((qb, N_HEADS_A, Q_BLOCK, K_WINDOW), BF16)],
        compiler_params=pltpu.CompilerParams(
            dimension_semantics=("parallel", "parallel", "arbitrary"),
            vmem_limit_bytes=VMEM_LIMIT),
        name=f"dilated{dil}",
    )(qv, kv, vv, *([bias] * qb))


def _dilated_all(views, positions, tbl):
    b, s = positions.shape
    dils = [dil for _, dil in DILATED_PATTERNS]
    consecutive = jnp.all(positions[:, 1:] - positions[:, :-1] == 1)

    def run(bias, index_fns):
        outs = []
        for g, dil in enumerate(dils):
            qv, kv, vv = views[3 * g:3 * g + 3]
            outs.extend(_dilated(qv, kv, vv, bias, index_fns[g], dil))
        return tuple(outs)

    def shared_tiles():
        offs, qpos, kpos, fns = [], [], [], []
        for g, dil in enumerate(dils):
            for off in (0, -HALF_WINDOW, -2 * HALF_WINDOW):
                offs.append(off)
                qpos.append(dil * jnp.arange(Q_BLOCK, dtype=jnp.int32))
                kpos.append(dil * (off + jnp.arange(K_WINDOW, dtype=jnp.int32)))
            fns.append(lambda bi, r, j, nq, g=g:
                       3 * g + jnp.where(j == 0, 0, jnp.where(j == nq - 1, 2, 1)))
        bias = _bias_tiles(jnp.array(offs, jnp.int32), jnp.stack(qpos)[:, :, None],
                           jnp.stack(kpos)[:, None, :], tbl)
        return run(bias, fns)

    def per_block_tiles():
        offs, qpos, kpos, fns = [], [], [], []
        base = 0
        for g, dil in enumerate(dils):
            sub_len = s // dil
            nq = sub_len // Q_BLOCK
            pos_t = positions.reshape(b, sub_len, dil).transpose(0, 2, 1).reshape(b * dil, sub_len)
            starts = _window_start(jnp.arange(nq), sub_len)
            win = starts[:, None] + jnp.arange(K_WINDOW)[None, :]
            qpos.append(pos_t.reshape(b * dil * nq, Q_BLOCK))
            kpos.append(pos_t[:, win].reshape(b * dil * nq, K_WINDOW))
            offs.append(jnp.tile(starts - jnp.arange(nq) * Q_BLOCK, b * dil))
            fns.append(lambda bi, r, j, nq, base=base, dil=dil: base + (bi * dil + r) * nq + j)
            base += b * dil * nq
        bias = _bias_tiles(jnp.concatenate(offs).astype(jnp.int32),
                           jnp.concatenate(qpos)[:, :, None], jnp.concatenate(kpos)[:, None, :], tbl)
        return run(bias, fns)

    return lax.cond(consecutive, shared_tiles, per_block_tiles)


def _mla_kernel(q_ref, k_ref, v_ref, o_ref, *, tk):
    q = q_ref[0, 0]
    tq = q.shape[0]
    n_kv = k_ref.shape[2] // tk

    def body(i, carry):
        m, acc = carry
        start = pl.multiple_of(i * tk, tk)
        k = k_ref[0, 0, pl.ds(start, tk), :]
        v = v_ref[0, 0, pl.ds(start, tk), :]
        s = _dot_nt(q, k)
        m_new = jnp.maximum(m, jnp.max(s, axis=1, keepdims=True))
        p = jnp.exp2(s - m_new)
        acc = jnp.exp2(m - m_new) * acc + _dot(p.astype(BF16), v)
        return m_new, acc

    m0 = jnp.full((tq, 1), NEG_INF, F32)
    _, acc = lax.fori_loop(0, n_kv, body, (m0, jnp.zeros((tq, LANES), F32)), unroll=True)
    lane = lax.broadcasted_iota(jnp.int32, (1, LANES), 1)
    denom = jnp.sum(jnp.where(lane == V_DIM_B, acc, 0.0), axis=1, keepdims=True)
    o_ref[0, 0] = jnp.where(lane < V_DIM_B, acc / denom, 0.0).astype(BF16)


def _mla(qm, km, vm, tq, tk):
    b, nh, s, _ = qm.shape
    qspec = pl.BlockSpec((1, 1, tq, LANES), lambda bi, h, i: (bi, h, i, 0))
    kvspec = pl.BlockSpec((1, 1, s, LANES), lambda bi, h, i: (bi, h, 0, 0))
    return pl.pallas_call(
        functools.partial(_mla_kernel, tk=tk),
        grid=(b, nh, s // tq),
        in_specs=[qspec, kvspec, kvspec],
        out_specs=qspec,
        out_shape=jax.ShapeDtypeStruct((b, nh, s, LANES), BF16),
        compiler_params=pltpu.CompilerParams(
            dimension_semantics=("parallel", "parallel", "arbitrary"),
            vmem_limit_bytes=VMEM_LIMIT),
        name="mla",
    )(qm, km, vm)


def _memkv_kernel(mem_ref, g_ref, w_ref, k_ref, v_ref):
    d = mem_ref.shape[2]
    kv = _dot(_rms(mem_ref[0], g_ref[...]).astype(BF16), w_ref[...])
    k_ref[0] = kv[:, :d].astype(BF16)
    v_ref[0] = kv[:, d:].astype(BF16)


def _memkv(mem, g_mem, w_mkv):
    b, m, d = mem.shape
    spec = pl.BlockSpec((1, m, d), lambda bi: (bi, 0, 0))
    return pl.pallas_call(
        _memkv_kernel,
        grid=(b,),
        in_specs=[spec, pl.BlockSpec((1, d), lambda bi: (0, 0)),
                  pl.BlockSpec((d, 2 * d), lambda bi: (0, 0))],
        out_specs=[spec, spec],
        out_shape=[jax.ShapeDtypeStruct((b, m, d), BF16)] * 2,
        compiler_params=pltpu.CompilerParams(
            dimension_semantics=("parallel",), vmem_limit_bytes=VMEM_LIMIT),
        name="memkv",
    )(mem, g_mem, w_mkv)


def _post_kernel(x_ref, o1_ref, o2_ref, o3_ref, l1_ref, l2_ref, l3_ref, ob_ref,
                 goa_ref, gob_ref, woa_ref, wob_ref, gx_ref, wmq_ref, km_ref, vm_ref, wmo_ref,
                 gmoe_ref, wr_ref, br_ref,
                 h_ref, xn_ref, logit_ref, *nat_scr):
    tm = x_ref.shape[1]
    n_chunk = WIDTH_A // LANES

    for view_ref, scr in zip((o2_ref, o3_ref, l2_ref, l3_ref), nat_scr):
        dil = view_ref.shape[2] // WIDTH_A
        for r in range(dil):
            for c in range(n_chunk):
                col = r * WIDTH_A + c * LANES
                scr[c, pl.ds(r, tm // dil, stride=dil), :] = view_ref[0, :, col:col + LANES]

    def natural(scr, rows):
        return jnp.concatenate([scr[c, rows, :] for c in range(n_chunk)], axis=1)

    rows_per_chain = tm // POST_CHAINS

    def chain(ci):
        rows = slice(ci * rows_per_chain, (ci + 1) * rows_per_chain)
        o1, o2, o3 = o1_ref[0, rows, :], natural(nat_scr[0], rows), natural(nat_scr[1], rows)
        l1, l2, l3 = l1_ref[0, rows, :], natural(nat_scr[2], rows), natural(nat_scr[3], rows)
        mx = jnp.maximum(jnp.maximum(l1, l2), l3)
        w1, w2, w3 = jnp.exp2(l1 - mx), jnp.exp2(l2 - mx), jnp.exp2(l3 - mx)
        oa = (w1 * o1 + w2 * o2 + w3 * o3) / (w1 + w2 + w3)
        oan = _rms(oa, goa_ref[...]).astype(BF16)
        ob = jnp.concatenate([ob_ref[0, h, rows, :] for h in range(N_HEADS_B)], axis=1).astype(F32)
        ms_b = jnp.sum(ob * ob, axis=1, keepdims=True) * (1.0 / (N_HEADS_B * V_DIM_B))
        obn = (ob * lax.rsqrt(ms_b + EPS) * gob_ref[...]).astype(BF16)
        yield
        h1 = x_ref[0, rows, :] + _dot(oan, woa_ref[...]) + _dot(obn, wob_ref[...])
        yield
        hn = _rms(h1, gx_ref[...]).astype(BF16)
        yield
        q = _dot(hn, wmq_ref[...]).astype(BF16)
        yield
        dh = q.shape[1] // N_HEADS_MEM
        heads = []
        for h in range(N_HEADS_MEM):
            sl = slice(h * dh, (h + 1) * dh)
            s = _dot_nt(q[:, sl], km_ref[0, :, sl])
            yield
            e = jnp.exp(s - jnp.max(s, axis=1, keepdims=True))
            p = e / jnp.sum(e, axis=1, keepdims=True)
            yield
            heads.append(_dot(p.astype(BF16), vm_ref[0, :, sl]))
        o = jnp.concatenate(heads, axis=1).astype(BF16)
        yield
        h2 = h1 + _dot(o, wmo_ref[...])
        h_ref[0, rows, :] = h2
        yield
        xn = _rms(h2, gmoe_ref[...])
        packed = _pack_bf16_pairs(xn)
        rows_per_token = packed.shape[1] // LANES
        _store_token_rows(xn_ref.at[pl.ds(ci * rows_per_chain * rows_per_token,
                                          rows_per_chain * rows_per_token)], packed)
        yield
        logit_ref[0, rows, :] = jnp.dot(xn, wr_ref[...], preferred_element_type=F32,
                                        precision=lax.Precision.HIGHEST) + br_ref[...]

    _run_skewed([chain(ci) for ci in range(POST_CHAINS)])


def _post(x, o_pats, lse_pats, ob, g_out_a, g_out_b_pad, w_o_a, w_o_b_pad, g_xattn, w_mq, kmem,
          vmem, w_mo, g_moe, w_router_pad, b_router_pad, tm):
    b, s, d = x.shape
    m = kmem.shape[1]
    full = lambda shape: pl.BlockSpec(shape, lambda bi, i: (0,) * len(shape))
    row = lambda w: pl.BlockSpec((1, tm, w), lambda bi, i: (bi, i, 0))
    memspec = pl.BlockSpec((1, m, d), lambda bi, i: (bi, 0, 0))
    hw = N_HEADS_B * LANES
    views = [pl.BlockSpec((1, tm // dil, dil * WIDTH_A), lambda bi, i: (bi, i, 0))
             for _, dil in DILATED_PATTERNS]
    return pl.pallas_call(
        _post_kernel,
        grid=(b, s // tm),
        in_specs=[row(d)] + views + views
        + [pl.BlockSpec((1, N_HEADS_B, tm, LANES), lambda bi, i: (bi, 0, i, 0)),
           full((1, WIDTH_A)), full((1, hw)), full((WIDTH_A, d)), full((hw, d)),
           full((1, d)), full((d, d)), memspec, memspec, full((d, d)),
           full((1, d)), full((d, LANES)), full((1, LANES))],
        out_specs=[row(d), pl.BlockSpec((tm * d // 2 // LANES, LANES), lambda bi, i: (bi * (s // tm) + i, 0)),
                   row(LANES)],
        out_shape=[jax.ShapeDtypeStruct((b, s, d), F32),
                   jax.ShapeDtypeStruct((b * s * d // 2 // LANES, LANES), jnp.uint32),
                   jax.ShapeDtypeStruct((b, s, LANES), F32)],
        scratch_shapes=[pltpu.VMEM((WIDTH_A // LANES, tm, LANES), F32)] * 4,
        compiler_params=pltpu.CompilerParams(
            dimension_semantics=("parallel", "parallel"), vmem_limit_bytes=VMEM_LIMIT),
        name="post",
    )(x, *o_pats, *lse_pats, ob, g_out_a, g_out_b_pad, w_o_a, w_o_b_pad, g_xattn, w_mq, kmem,
      vmem, w_mo, g_moe, w_router_pad, b_router_pad)


def _route_kernel(logit_ref, gate_ref, slot_ref, cslot_ref, count_ref, tcnt_ref, tcarry_ref, ccnt_ref,
                  ccarry_ref, carry_ref):
    @pl.when(pl.program_id(0) == 0)
    def _():
        carry_ref[...] = jnp.zeros_like(carry_ref)

    l = logit_ref[...]
    tr = l.shape[0]
    lane_i = lax.broadcasted_iota(jnp.int32, l.shape, 1)
    lane = lane_i.astype(F32)
    vals, idxs = [], []
    for _ in range(TOP_K):
        m = jnp.max(l, axis=1, keepdims=True)
        idx = jnp.min(jnp.where(l == m, lane, float(LANES)), axis=1, keepdims=True)
        vals.append(m)
        idxs.append(idx)
        l = jnp.where(lane == idx, -jnp.inf, l)
    exps = [jnp.exp(v - vals[0]) for v in vals]
    denom = exps[0] + exps[1] + exps[2] + exps[3]
    onehot = jnp.zeros(l.shape, F32)
    for idx in idxs:
        onehot = onehot + jnp.where(lane == idx, 1.0, 0.0)
    r = lax.broadcasted_iota(jnp.int32, (tr, tr), 0)
    c = lax.broadcasted_iota(jnp.int32, (tr, tr), 1)
    tri = jnp.where(c < r, 1.0, 0.0).astype(BF16)
    local = _dot(tri, onehot.astype(BF16))
    tile_cnt = jnp.sum(onehot, axis=0, keepdims=True)
    li = lax.broadcasted_iota(jnp.int32, (LANES, LANES), 0)
    lj = lax.broadcasted_iota(jnp.int32, (LANES, LANES), 1)
    prefix = jnp.dot(jnp.broadcast_to(tile_cnt, (8, LANES)), jnp.where(li < lj, 1.0, 0.0),
                     preferred_element_type=F32, precision=lax.Precision.HIGHEST)[0:1]
    grouped = local + prefix
    n_sub = tr // COMBINE_TILE
    sub_cnt = [jnp.sum(onehot[u * COMBINE_TILE:(u + 1) * COMBINE_TILE], axis=0, keepdims=True)
               for u in range(n_sub)]
    sub_prefix = jnp.dot(jnp.concatenate(sub_cnt + [jnp.zeros((8 - n_sub, LANES), F32)], axis=0),
                         jnp.where(li < lj, 1.0, 0.0), preferred_element_type=F32,
                         precision=lax.Precision.HIGHEST)
    shifts, sub_base = [], jnp.zeros((1, LANES), F32)
    for u in range(n_sub):
        shifts.append(jnp.broadcast_to(sub_prefix[u:u + 1] - sub_base, (COMBINE_TILE, LANES)))
        ccnt_ref[u * 8:(u + 1) * 8, :] = jnp.broadcast_to(sub_cnt[u], (8, LANES)).astype(jnp.int32)
        ccarry_ref[u * 8:(u + 1) * 8, :] = jnp.broadcast_to(carry_ref[...] + sub_base,
                                                            (8, LANES)).astype(jnp.int32)
        sub_base = sub_base + sub_cnt[u]
    sub_grouped = local + jnp.concatenate(shifts, axis=0)
    radix = COMBINE_TILE * TOP_K
    assert radix & (radix - 1) == 0
    both = grouped * radix + sub_grouped
    gate = jnp.zeros(l.shape, F32)
    slot = jnp.zeros(l.shape, jnp.int32)
    cslot = jnp.zeros(l.shape, jnp.int32)
    for k in range(TOP_K):
        picked = jnp.sum(jnp.where(lane == idxs[k], both, 0.0), axis=1, keepdims=True).astype(jnp.int32)
        gate = jnp.where(lane_i == k, exps[k] / denom, gate)
        slot = jnp.where(lane_i == k, picked >> (radix.bit_length() - 1), slot)
        cslot = jnp.where(lane_i == k, picked & (radix - 1), cslot)
    gate_ref[...] = gate
    slot_ref[...] = slot
    cslot_ref[...] = cslot
    tcnt_ref[...] = jnp.broadcast_to(tile_cnt, tcnt_ref.shape).astype(jnp.int32)
    tcarry_ref[...] = jnp.broadcast_to(carry_ref[...], tcarry_ref.shape).astype(jnp.int32)
    total = carry_ref[...] + tile_cnt
    carry_ref[...] = total
    count_ref[...] = total.astype(jnp.int32)


def _route(logits, tr):
    t = logits.shape[0]
    spec = pl.BlockSpec((tr, LANES), lambda i: (i, 0))
    return pl.pallas_call(
        _route_kernel,
        grid=(t // tr,),
        in_specs=[spec],
        out_specs=[spec, spec, spec, pl.BlockSpec((1, LANES), lambda i: (0, 0)),
                   pl.BlockSpec((8, LANES), lambda i: (i, 0)), pl.BlockSpec((8, LANES), lambda i: (i, 0)),
                   pl.BlockSpec((8 * tr // COMBINE_TILE, LANES), lambda i: (i, 0)),
                   pl.BlockSpec((8 * tr // COMBINE_TILE, LANES), lambda i: (i, 0))],
        out_shape=[jax.ShapeDtypeStruct((t, LANES), F32)] + [jax.ShapeDtypeStruct((t, LANES), jnp.int32)] * 2
        + [jax.ShapeDtypeStruct((1, LANES), jnp.int32)]
        + [jax.ShapeDtypeStruct((t // tr * 8, LANES), jnp.int32)] * 2
        + [jax.ShapeDtypeStruct((t // COMBINE_TILE * 8, LANES), jnp.int32)] * 2,
        scratch_shapes=[pltpu.VMEM((1, LANES), F32)],
        compiler_params=pltpu.CompilerParams(
            dimension_semantics=("arbitrary",), vmem_limit_bytes=VMEM_LIMIT),
        name="route",
    )(logits)


def _dispatch_kernel(dst_ref, cnt_ref, ends_ref, x_ref, slot_ref, xs_hbm, zbuf, stage, sem, zsem):
    c = pl.program_id(0)
    buf = lax.rem(c, 2)
    n_chunk = x_ref.shape[0] // DISPATCH_CHUNK
    blk_rows = zbuf.shape[0]
    n_staged = DISPATCH_CHUNK * TOP_K

    @pl.when(c == 0)
    def _():
        zbuf[...] = jnp.zeros_like(zbuf)
        for b in range(2):
            stage[b, pl.ds(n_staged * n_chunk, DISPATCH_PIECE * n_chunk), :] = jnp.zeros(
                (DISPATCH_PIECE * n_chunk, LANES), stage.dtype)

        def zero_copy(start):
            return pltpu.make_async_copy(
                zbuf, xs_hbm.at[pl.ds(pl.multiple_of(start * n_chunk, n_chunk), blk_rows)], zsem)

        blk = blk_rows // n_chunk
        used_end = ends_ref[N_EXPERTS]
        starts = [(ends_ref[e + 1] - ends_ref[e] >= back * blk, ends_ref[e + 1] - back * blk)
                  for e in range(N_EXPERTS) for back in (1, 2)]
        starts += [((used_end + (j + 1) * blk) * n_chunk <= xs_hbm.shape[0], used_end + j * blk)
                   for j in range(N_EXPERTS)]
        for cond, start in starts:
            @pl.when(cond)
            def _():
                zero_copy(start).start()
        for cond, start in starts:
            @pl.when(cond)
            def _():
                zero_copy(start).wait()

    x_lo, x_hi = _unpack_bf16_pairs(_load_token_rows(x_ref, DISPATCH_CHUNK))
    x_lo, x_hi = x_lo.astype(BF16), x_hi.astype(BF16)
    slot_t = jnp.transpose(slot_ref[...].astype(F32))
    bits = lambda v: lax.bitcast_convert_type(v, jnp.uint32)
    for part in range(TOP_K):
        j = (lax.broadcasted_iota(jnp.int32, (DISPATCH_CHUNK, 1), 0) + part * DISPATCH_CHUNK).astype(F32)
        pick = jnp.zeros((DISPATCH_CHUNK, DISPATCH_CHUNK), F32)
        for k in range(TOP_K):
            pick = jnp.where(slot_t[k:k + 1, :] == j, 1.0, pick)
        pick = pick.astype(BF16)
        packed = bits(_dot(pick, x_hi)) | (bits(_dot(pick, x_lo)) >> 16)
        _store_token_rows(
            stage.at[buf, pl.ds(part * DISPATCH_CHUNK * n_chunk, DISPATCH_CHUNK * n_chunk)], packed)

    def piece(src_tok, dst_tok):
        rows = DISPATCH_PIECE * n_chunk
        return pltpu.make_async_copy(
            stage.at[buf, pl.ds(pl.multiple_of(src_tok * n_chunk, n_chunk), rows)],
            xs_hbm.at[pl.ds(pl.multiple_of(dst_tok * n_chunk, n_chunk), rows)], sem)

    def drain(step):
        n_pieces = 0
        for e in range(N_EXPERTS):
            n_pieces = n_pieces + (cnt_ref[step * N_EXPERTS + e] + DISPATCH_PIECE - 1) // DISPATCH_PIECE

        def wait_one(i, carry):
            piece(0, 0).wait()
            return carry
        lax.fori_loop(0, n_pieces, wait_one, 0)

    @pl.when(c > 0)
    def _():
        drain(c - 1)

    staged = 0
    for e in range(N_EXPERTS):
        cnt = cnt_ref[c * N_EXPERTS + e]
        dst = dst_ref[c * N_EXPERTS + e]
        for p in range(DISPATCH_CHUNK // DISPATCH_PIECE):
            @pl.when(p * DISPATCH_PIECE < cnt)
            def _():
                piece(staged + p * DISPATCH_PIECE, dst + p * DISPATCH_PIECE).start()
        staged = staged + cnt

    @pl.when(c == pl.num_programs(0) - 1)
    def _():
        drain(c)


def _dispatch(dst, cnt, ends0, xn_rows, slots, n_tok, n_slots):
    n_chunk = xn_rows.shape[0] // n_tok
    staged_rows = (DISPATCH_CHUNK * TOP_K + DISPATCH_PIECE) * n_chunk
    grid_spec = pltpu.PrefetchScalarGridSpec(
        num_scalar_prefetch=3,
        grid=(n_tok // DISPATCH_CHUNK,),
        in_specs=[pl.BlockSpec((DISPATCH_CHUNK * n_chunk, LANES), lambda c, *_: (c, 0)),
                  pl.BlockSpec((DISPATCH_CHUNK, LANES), lambda c, *_: (c, 0))],
        out_specs=pl.BlockSpec(memory_space=pl.ANY),
        scratch_shapes=[pltpu.VMEM((EXPERT_BLOCK * n_chunk, LANES), xn_rows.dtype),
                        pltpu.VMEM((2, staged_rows, LANES), xn_rows.dtype),
                        pltpu.SemaphoreType.DMA(()), pltpu.SemaphoreType.DMA(())],
    )
    return pl.pallas_call(
        _dispatch_kernel,
        grid_spec=grid_spec,
        out_shape=jax.ShapeDtypeStruct((n_slots * n_chunk, LANES), xn_rows.dtype),
        compiler_params=pltpu.CompilerParams(
            dimension_semantics=("arbitrary",), vmem_limit_bytes=VMEM_LIMIT),
        name="dispatch",
    )(dst, cnt, ends0, xn_rows, slots)


def _expert_kernel(be_ref, nused_ref, next_ref, valid_ref, xs_ref, wgu_hbm, bgu_ref, wd_hbm, bd_ref, y_ref,
                   wgu_f32, wd_f32, wgu_bf, wd_bf, sems):
    i = pl.program_id(0)
    n_chunk = xs_ref.shape[0] // EXPERT_BLOCK
    half = EXPERT_BLOCK // 2

    def mlp(n_rows):
        part = pl.ds(0, n_rows * n_chunk)
        x_lo, x_hi = _unpack_bf16_pairs(_load_token_rows(xs_ref.at[part], n_rows))
        x = jnp.concatenate([x_lo.astype(BF16), x_hi.astype(BF16)], axis=1)
        gu = _dot(x, wgu_bf[...]) + bgu_ref[0]
        de = gu.shape[1] // 2
        gate = jnp.minimum(gu[:, :de], SWIGLU_LIMIT)
        up = jnp.clip(gu[:, de:], -SWIGLU_LIMIT, SWIGLU_LIMIT)
        hmid = (up + 1.0) * (gate * jax.nn.sigmoid(SWIGLU_ALPHA * gate))
        y = _dot(hmid.astype(BF16), wd_bf[...]) + bd_ref[0]
        _store_token_rows(y_ref.at[part], _pack_bf16_pairs(y))

    def fetch(e):
        return (pltpu.make_async_copy(wgu_hbm.at[e], wgu_f32, sems.at[0]),
                pltpu.make_async_copy(wd_hbm.at[e], wd_f32, sems.at[1]))

    @pl.when(i == 0)
    def _():
        for copy in fetch(be_ref[0]):
            copy.start()

    @pl.when(i < nused_ref[0])
    def _():
        @pl.when(jnp.logical_or(i == 0, be_ref[i] != be_ref[jnp.maximum(i - 1, 0)]))
        def _():
            for copy in fetch(be_ref[i]):
                copy.wait()
            wgu_bf[...] = wgu_f32[...].astype(BF16)
            wd_bf[...] = wd_f32[...].astype(BF16)

            @pl.when(next_ref[i] >= 0)
            def _():
                for copy in fetch(next_ref[i]):
                    copy.start()

        valid = valid_ref[i]

        @pl.when(valid > half)
        def _():
            mlp(EXPERT_BLOCK)

        @pl.when(jnp.logical_and(valid > 0, valid <= half))
        def _():
            mlp(half)
            y_ref[pl.ds(half * n_chunk, half * n_chunk), :] = jnp.zeros((half * n_chunk, LANES), y_ref.dtype)

    @pl.when(jnp.logical_or(i >= nused_ref[0], valid_ref[i] == 0))
    def _():
        y_ref[...] = jnp.zeros_like(y_ref)


def _experts(block_expert, n_used, next_expert, valid_rows, xs_rows, w_gate_up, b_gate_up, w_down, b_down):
    d, de2 = w_gate_up.shape[1:]
    n_chunk = d // 2 // LANES
    blk = EXPERT_BLOCK
    n_blk = xs_rows.shape[0] // (blk * n_chunk)
    grid_spec = pltpu.PrefetchScalarGridSpec(
        num_scalar_prefetch=4,
        grid=(n_blk,),
        in_specs=[pl.BlockSpec((blk * n_chunk, LANES),
                               lambda i, be, nu, *_: (jnp.maximum(jnp.minimum(i, nu[0] - 1), 0), 0)),
                  pl.BlockSpec(memory_space=pl.ANY),
                  pl.BlockSpec((1, 1, de2), lambda i, be, *_: (be[i], 0, 0)),
                  pl.BlockSpec(memory_space=pl.ANY),
                  pl.BlockSpec((1, 1, d), lambda i, be, *_: (be[i], 0, 0))],
        out_specs=pl.BlockSpec((blk * n_chunk, LANES), lambda i, *_: (i, 0)),
        scratch_shapes=[pltpu.VMEM((d, de2), F32), pltpu.VMEM((de2 // 2, d), F32),
                        pltpu.VMEM((d, de2), BF16), pltpu.VMEM((de2 // 2, d), BF16),
                        pltpu.SemaphoreType.DMA((2,))],
    )
    return pl.pallas_call(
        _expert_kernel,
        grid_spec=grid_spec,
        out_shape=jax.ShapeDtypeStruct(xs_rows.shape, xs_rows.dtype),
        compiler_params=pltpu.CompilerParams(
            dimension_semantics=("arbitrary",), vmem_limit_bytes=VMEM_LIMIT),
        name="experts",
    )(block_expert, n_used, next_expert, valid_rows, xs_rows, w_gate_up, b_gate_up.reshape(N_EXPERTS, 1, de2),
      w_down, b_down.reshape(N_EXPERTS, 1, d))


def _combine_kernel(start_ref, len_ref, h_ref, gate_ref, slot_ref, gfin_ref, y_hbm, out_ref, stage, sems):
    i = pl.program_id(0)
    n = pl.num_programs(0)
    buf = lax.rem(i, 2)
    tc = h_ref.shape[0]
    n_staged = tc * TOP_K
    n_chunk = stage.shape[1] // n_staged

    def issue(tile, b):
        staged = 0
        for e in range(N_EXPERTS):
            run_len = len_ref[tile * N_EXPERTS + e]
            run_start = start_ref[tile * N_EXPERTS + e]
            def pieces(bits):
                for bit in bits:
                    size = 1 << bit
                    done = (run_len >> (bit + 1)) << (bit + 1)

                    @pl.when((run_len >> bit) & 1 == 1)
                    def _():
                        pltpu.make_async_copy(
                            y_hbm.at[pl.ds(pl.multiple_of((run_start + done) * n_chunk, n_chunk),
                                           size * n_chunk)],
                            stage.at[b, pl.ds(pl.multiple_of((staged + done) * n_chunk, n_chunk),
                                              size * n_chunk)],
                            sems.at[b]).start()

            top = COMBINE_TILE.bit_length() - 1
            pl.when(run_len >= (1 << (top - 1)))(lambda: pieces((top, top - 1)))
            pieces(reversed(range(top - 1)))
            staged = staged + run_len

    @pl.when(i == 0)
    def _():
        issue(0, 0)

    @pl.when(i + 1 < n)
    def _():
        issue(i + 1, 1 - buf)

    pltpu.make_async_copy(y_hbm.at[pl.ds(0, n_staged * n_chunk)], stage.at[buf], sems.at[buf]).wait()

    y_lo, y_hi = _unpack_bf16_pairs(_load_token_rows(stage.at[buf], n_staged))
    y_lo, y_hi = y_lo.astype(BF16), y_hi.astype(BF16)
    gate = gate_ref[...]
    slot = slot_ref[...].astype(F32)
    j = lax.broadcasted_iota(jnp.int32, (1, n_staged), 1).astype(F32)
    pick = jnp.zeros((tc, n_staged), F32)
    for k in range(TOP_K):
        pick = jnp.where(slot[:, k:k + 1] == j, gate[:, k:k + 1], pick)
    pick_hi = pick.astype(BF16)
    pick_lo = (pick - pick_hi.astype(F32)).astype(BF16)
    moe = jnp.concatenate([_dot(pick_hi, y_lo) + _dot(pick_lo, y_lo),
                           _dot(pick_hi, y_hi) + _dot(pick_lo, y_hi)], axis=1)
    out_ref[...] = _rms(h_ref[...] + moe, gfin_ref[...])


def _combine(run_start, run_len, h2, gates, slots, g_final, y_rows):
    t, d = h2.shape
    tc = COMBINE_TILE
    n_chunk = d // 2 // LANES
    grid_spec = pltpu.PrefetchScalarGridSpec(
        num_scalar_prefetch=2,
        grid=(t // tc,),
        in_specs=[pl.BlockSpec((tc, d), lambda i, *_: (i, 0)),
                  pl.BlockSpec((tc, LANES), lambda i, *_: (i, 0)),
                  pl.BlockSpec((tc, LANES), lambda i, *_: (i, 0)),
                  pl.BlockSpec((1, d), lambda i, *_: (0, 0)),
                  pl.BlockSpec(memory_space=pl.ANY)],
        out_specs=pl.BlockSpec((tc, d), lambda i, *_: (i, 0)),
        scratch_shapes=[pltpu.VMEM((2, tc * TOP_K * n_chunk, LANES), y_rows.dtype),
                        pltpu.SemaphoreType.DMA((2,))],
    )
    return pl.pallas_call(
        _combine_kernel,
        grid_spec=grid_spec,
        out_shape=jax.ShapeDtypeStruct((t, d), F32),
        compiler_params=pltpu.CompilerParams(
            dimension_semantics=("arbitrary",), vmem_limit_bytes=VMEM_LIMIT),
        name="combine",
    )(run_start, run_len, h2, gates, slots, g_final, y_rows)


def _pad_heads(w, n_heads, width, offset=0):
    k = w.shape[0]
    w = w.reshape(k, n_heads, width)
    w = jnp.pad(w, ((0, 0), (0, 0), (offset, LANES - width - offset)))
    return w.reshape(k, n_heads * LANES)


def _rot_cols(w):
    half = w.shape[-1] // 2
    return jnp.concatenate([-w[..., half:], w[..., :half]], axis=-1)


def kernel(x, mem, positions, g_mix, w_in, g_q_a, w_q_b, g_kv_a, w_kv_b, rel_bias, g_out_a, g_out_b,
           w_o, g_xattn, g_mem, w_mq, w_mkv, w_mo, g_moe, w_router, b_router, w_gate_up, b_gate_up,
           w_down, b_down, g_final):
    b, s, d = x.shape
    t = b * s
    assert g_mix.shape[0] == 1, "single-layer block: the final norm is fused into the last stage"

    inv_freq = ROPE_THETA ** (-jnp.arange(0, QK_ROPE_DIM, 2, dtype=F32) / QK_ROPE_DIM)
    ang = positions.astype(F32)[..., None] * inv_freq
    cos_sin = jnp.concatenate([jnp.cos(ang), jnp.sin(ang)], axis=-1)
    half = QK_ROPE_DIM // 2
    src = jnp.arange(QK_ROPE_DIM)[:, None]
    lane = jnp.arange(LANES)[None, :]
    on_rope = (lane >= QK_NOPE_DIM) & (lane < QK_NOPE_DIM + QK_ROPE_DIM)
    place = ((lane - QK_NOPE_DIM) % half == src % half) & on_rope
    expand = jnp.concatenate([place & (src < half), place & (src >= half)], axis=1).astype(F32)
    tbl = jnp.pad(rel_bias.T.astype(F32), ((0, 0), (0, LANES - NUM_BUCKETS)))

    h = x
    for l in range(1):
        c0 = 3 * WIDTH_A + Q_LORA_RANK + KV_LORA_RANK
        w_kpe = w_in[l][:, c0:c0 + QK_ROPE_DIM]
        place = lambda w: jnp.pad(w, ((0, 0), (QK_NOPE_DIM, LANES - QK_NOPE_DIM - QK_ROPE_DIM)))
        w_in_ext = jnp.concatenate([w_in[l][:, :c0], place(w_kpe), place(_rot_cols(w_kpe))], 1).astype(BF16)
        dq = QK_NOPE_DIM + QK_ROPE_DIM
        wq3 = w_q_b[l].reshape(Q_LORA_RANK, N_HEADS_B, dq)
        wq = _pad_heads(wq3.reshape(Q_LORA_RANK, -1), N_HEADS_B, dq).astype(BF16)
        wq_rot3 = jnp.concatenate([jnp.zeros_like(wq3[..., :QK_NOPE_DIM]), _rot_cols(wq3[..., QK_NOPE_DIM:])], -1)
        wqr = _pad_heads(wq_rot3.reshape(Q_LORA_RANK, -1), N_HEADS_B, dq).astype(BF16)
        wkv3 = w_kv_b[l].reshape(KV_LORA_RANK, N_HEADS_B, QK_NOPE_DIM + V_DIM_B)
        wk = _pad_heads(wkv3[..., :QK_NOPE_DIM].reshape(KV_LORA_RANK, -1), N_HEADS_B, QK_NOPE_DIM).astype(BF16)
        wv = _pad_heads(wkv3[..., QK_NOPE_DIM:].reshape(KV_LORA_RANK, -1), N_HEADS_B, V_DIM_B).astype(BF16)

        *views, qm, km, vm = _proj(h, cos_sin, expand, g_mix[l][None], w_in_ext, g_q_a[l][None], wq, wqr,
                                   g_kv_a[l][None], wk, wv, tm=256 * PROJ_CHAINS)
        pats = _dilated_all(views, positions, tbl)
        ob = _mla(qm, km, vm, tq=1024, tk=512)
        kmem, vmem = _memkv(mem, g_mem[l][None], w_mkv[l].astype(BF16))

        g_out_b_pad = _pad_heads(g_out_b[l][None], N_HEADS_B, V_DIM_B)
        w_o_b_pad = _pad_heads(w_o[l][WIDTH_A:].T, N_HEADS_B, V_DIM_B).T.astype(BF16)
        w_mq_s = (w_mq[l] * ((d // N_HEADS_MEM) ** -0.5)).astype(BF16)
        w_router_pad = jnp.pad(w_router[l], ((0, 0), (0, LANES - N_EXPERTS)))
        b_router_pad = jnp.pad(b_router[l][None], ((0, 0), (0, LANES - N_EXPERTS)), constant_values=NEG_INF)
        h2, xn_rows, logits = _post(h, pats[0::2], pats[1::2], ob, g_out_a[l][None],
                               g_out_b_pad, w_o[l][:WIDTH_A].astype(BF16), w_o_b_pad, g_xattn[l][None],
                               w_mq_s, kmem, vmem, w_mo[l].astype(BF16), g_moe[l][None], w_router_pad,
                               b_router_pad, tm=256 * POST_CHAINS)

        gates, slots, cslots, counts, tile_cnt, tile_carry, ctile_cnt, ctile_carry = _route(
            logits.reshape(t, LANES), tr=DISPATCH_CHUNK)
        counts = counts[0, :N_EXPERTS]
        blk = EXPERT_BLOCK
        padded = jnp.where(counts > 0, (counts + DISPATCH_PIECE + blk - 1) // blk * blk, 0)
        ends = jnp.cumsum(padded)
        pad_start = ends - padded
        n_blk = t * TOP_K // blk + N_EXPERTS + -(-N_EXPERTS * DISPATCH_PIECE // blk)
        block_expert = jnp.minimum(
            jnp.sum(ends[None, :] <= (jnp.arange(n_blk) * blk)[:, None], axis=1),
            N_EXPERTS - 1).astype(jnp.int32)
        n_used = (ends[-1] // blk).astype(jnp.int32)[None]
        ends0 = jnp.concatenate([jnp.zeros((1,), jnp.int32), ends.astype(jnp.int32)])
        run_start = (pad_start[None, :] + tile_carry[::8, :N_EXPERTS]).astype(jnp.int32).reshape(-1)
        run_len = tile_cnt[::8, :N_EXPERTS].reshape(-1)
        xs_rows = _dispatch(run_start, run_len, ends0, xn_rows, slots, t, n_blk * blk)
        expert_ids = jnp.arange(N_EXPERTS, dtype=jnp.int32)
        run_end = jnp.sum(jnp.where(block_expert[:, None] == expert_ids, ends[None, :], 0), axis=1)
        next_expert = jnp.where(
            run_end < ends[-1],
            jnp.minimum(jnp.sum(ends[None, :] <= run_end[:, None], axis=1), N_EXPERTS - 1), -1).astype(jnp.int32)
        block_of = block_expert[:, None] == expert_ids
        real_end = jnp.sum(jnp.where(block_of, (pad_start + counts)[None, :], 0), axis=1)
        valid_rows = jnp.clip(real_end - jnp.arange(n_blk) * blk, 0, blk).astype(jnp.int32)
        y_rows = _experts(block_expert, n_used, next_expert, valid_rows, xs_rows, w_gate_up[l], b_gate_up[l], w_down[l], b_down[l])
        gather_start = (pad_start[None, :] + ctile_carry[::8, :N_EXPERTS]).astype(jnp.int32).reshape(-1)
        gather_len = ctile_cnt[::8, :N_EXPERTS].reshape(-1)
        out = _combine(gather_start, gather_len, h2.reshape(t, d), gates, cslots, g_final[None], y_rows)
        h = out.reshape(b, s, d)
    return h
```

```python
import functools
import math

import jax
import jax.numpy as jnp
from jax import lax
from jax.experimental import pallas as pl
from jax.experimental.pallas import tpu as pltpu

F32 = jnp.float32
BF16 = jnp.bfloat16

LANES = 128
EPS = 1e-6
NEG_INF = -1e30
LOG2_E = math.log2(math.e)

N_HEADS_A = 8
HEAD_DIM_A = 64
WIDTH_A = N_HEADS_A * HEAD_DIM_A
DILATED_PATTERNS = ((128, 1), (512, 4), (2048, 16))
N_HEADS_B = 8
QK_NOPE_DIM = 64
QK_ROPE_DIM = 32
V_DIM_B = 64
Q_LORA_RANK = 256
KV_LORA_RANK = 128
ROPE_THETA = 10000.0
NUM_BUCKETS = 32
MAX_DISTANCE = 1024
N_HEADS_MEM = 4
N_EXPERTS = 32
TOP_K = 4
SWIGLU_LIMIT = 7.0
SWIGLU_ALPHA = 1.702

Q_BLOCK = 128
K_WINDOW = 256
HALF_WINDOW = (K_WINDOW - Q_BLOCK) // 2
MAX_Q_BLOCKS_PER_STEP = 4
EXPERT_BLOCK = 512
DISPATCH_CHUNK = 512
DISPATCH_PIECE = 64
COMBINE_TILE = 128
PROJ_CHAINS = 2
POST_CHAINS = 2
VMEM_LIMIT = 56 * 1024 * 1024


def _rms(x, g):
    return x * lax.rsqrt(jnp.mean(x * x, axis=-1, keepdims=True) + EPS) * g


def _dot(a, b):
    return jnp.dot(a, b, preferred_element_type=F32)


def _dot_nt(a, b):
    return lax.dot_general(a, b, (((1,), (1,)), ((), ())), preferred_element_type=F32)


def _run_skewed(chains):
    live = []
    pending = list(chains)
    while live or pending:
        if pending:
            live.append(pending.pop(0))
        for gen in list(live):
            try:
                next(gen)
            except StopIteration:
                live.remove(gen)


def _load_token_rows(ref, n_tok):
    n_chunk = ref.shape[0] // n_tok
    return jnp.concatenate([ref[pl.ds(c, n_tok, stride=n_chunk), :] for c in range(n_chunk)], axis=1)


def _pack_bf16_pairs(x):
    half = x.shape[1] // 2
    bits = lambda v: lax.bitcast_convert_type(v.astype(BF16).astype(F32), jnp.uint32)
    return bits(x[:, half:]) | (bits(x[:, :half]) >> 16)


def _unpack_bf16_pairs(w):
    lo = lax.bitcast_convert_type(w << 16, F32)
    hi = lax.bitcast_convert_type(w & jnp.uint32(0xFFFF0000), F32)
    return lo, hi


def _store_token_rows(ref, value):
    n_tok = value.shape[0]
    n_chunk = value.shape[1] // LANES
    for c in range(n_chunk):
        ref[pl.ds(c, n_tok, stride=n_chunk), :] = value[:, c * LANES:(c + 1) * LANES]


def _proj_kernel(x_ref, cs_ref, expand_ref, gmix_ref, win_ref, gq_ref, wq_ref, wqr_ref,
                 gkv_ref, wk_ref, wv_ref,
                 q1_ref, k1_ref, v1_ref, q4_ref, k4_ref, v4_ref, q16_ref, k16_ref, v16_ref,
                 qm_ref, km_ref, vm_ref, za_scr):
    tm = x_ref.shape[1]
    c0 = 3 * WIDTH_A
    c1 = c0 + Q_LORA_RANK + KV_LORA_RANK
    per_group = WIDTH_A // LANES
    rows_per_chain = tm // PROJ_CHAINS
    lane = lax.broadcasted_iota(jnp.int32, (1, LANES), 1)
    in_rope = jnp.logical_and(lane >= QK_NOPE_DIM, lane < QK_NOPE_DIM + QK_ROPE_DIM)
    ones_col = jnp.where(lane == V_DIM_B, 1.0, 0.0).astype(F32)
    scale = (QK_NOPE_DIM + QK_ROPE_DIM) ** -0.5 * LOG2_E

    def chain(ci):
        r0 = ci * rows_per_chain
        rows = slice(r0, r0 + rows_per_chain)
        xn = _rms(x_ref[0, rows, :], gmix_ref[...]).astype(BF16)
        yield
        z = _dot(xn, win_ref[...])
        yield
        for c in range(3 * per_group):
            chunk = z[:, c * LANES:(c + 1) * LANES]
            za_scr[c, rows, :] = chunk * (HEAD_DIM_A ** -0.5 * LOG2_E) if c < per_group else chunk
        for dil, refs in ((1, (q1_ref, k1_ref, v1_ref)), (4, (q4_ref, k4_ref, v4_ref)),
                          (16, (q16_ref, k16_ref, v16_ref))):
            n = rows_per_chain // dil
            for r in range(dil):
                for c in range(3 * per_group):
                    picked = za_scr[c, pl.ds(r0 + r, n, stride=dil), :].astype(BF16)
                    col = r * WIDTH_A + (c % per_group) * LANES
                    refs[c // per_group][0, r0 // dil:r0 // dil + n, col:col + LANES] = picked
        zqn = _rms(z[:, c0:c0 + Q_LORA_RANK], gq_ref[...]).astype(BF16)
        zkvn = _rms(z[:, c0 + Q_LORA_RANK:c1], gkv_ref[...]).astype(BF16)
        yield
        spread = jnp.dot(cs_ref[0, rows, :], expand_ref[...], preferred_element_type=F32,
                         precision=lax.Precision.HIGHEST)
        q = _dot(zqn, wq_ref[...])
        qr = _dot(zqn, wqr_ref[...])
        kn = _dot(zkvn, wk_ref[...])
        vv = _dot(zkvn, wv_ref[...])
        yield
        cos = jnp.where(in_rope, spread[:, :LANES], 1.0)
        sin = spread[:, LANES:]
        kpe = z[:, c1:c1 + LANES] * cos + z[:, c1 + LANES:c1 + 2 * LANES] * sin
        for h in range(N_HEADS_B):
            sl = slice(h * LANES, (h + 1) * LANES)
            qm_ref[0, h, rows, :] = ((q[:, sl] * cos + qr[:, sl] * sin) * scale).astype(BF16)
            km_ref[0, h, rows, :] = (kn[:, sl] + kpe).astype(BF16)
            vm_ref[0, h, rows, :] = (vv[:, sl] + ones_col).astype(BF16)

    _run_skewed([chain(ci) for ci in range(PROJ_CHAINS)])


def _proj(x, cos_sin, expand, g_mix, w_in_ext, g_q, wq, wqr, g_kv, wk, wv, tm):
    b, s, d = x.shape
    n_in = w_in_ext.shape[1]
    hw = N_HEADS_B * LANES
    full = lambda shape: pl.BlockSpec(shape, lambda bi, i: (0,) * len(shape))
    row = lambda w: pl.BlockSpec((1, tm, w), lambda bi, i: (bi, i, 0))
    head = pl.BlockSpec((1, N_HEADS_B, tm, LANES), lambda bi, i: (bi, 0, i, 0))
    dils = [dil for _, dil in DILATED_PATTERNS]
    view_specs = [pl.BlockSpec((1, tm // dil, dil * WIDTH_A), lambda bi, i: (bi, i, 0))
                  for dil in dils for _ in range(3)]
    view_shapes = [jax.ShapeDtypeStruct((b, s // dil, dil * WIDTH_A), BF16)
                   for dil in dils for _ in range(3)]
    return pl.pallas_call(
        _proj_kernel,
        grid=(b, s // tm),
        in_specs=[row(d), row(cos_sin.shape[2]), full(expand.shape), full((1, d)), full((d, n_in)),
                  full((1, Q_LORA_RANK)), full((Q_LORA_RANK, hw)), full((Q_LORA_RANK, hw)),
                  full((1, KV_LORA_RANK)), full((KV_LORA_RANK, hw)), full((KV_LORA_RANK, hw))],
        out_specs=view_specs + [head, head, head],
        out_shape=view_shapes + [jax.ShapeDtypeStruct((b, N_HEADS_B, s, LANES), BF16)] * 3,
        scratch_shapes=[pltpu.VMEM((3 * WIDTH_A // LANES, tm, LANES), F32)],
        compiler_params=pltpu.CompilerParams(
            dimension_semantics=("parallel", "parallel"), vmem_limit_bytes=VMEM_LIMIT),
        name="proj",
    )(x, cos_sin, expand, g_mix, w_in_ext, g_q, wq, wqr, g_kv, wk, wv)


_LOG_BUCKET_STARTS = tuple(
    next(n for n in range(8, 4096)
         if int(math.log(n / 8) / math.log(MAX_DISTANCE / 8) * 8) >= t)
    for t in range(1, 8))


def _rel_bucket(rel):
    n = jnp.abs(rel)
    large = jnp.full(rel.shape, NUM_BUCKETS // 4, jnp.int32)
    for start in _LOG_BUCKET_STARTS:
        large = large + jnp.where(n >= start, 1, 0)
    mag = jnp.where(n < NUM_BUCKETS // 4, n, large)
    return mag + jnp.where(rel > 0, NUM_BUCKETS // 2, 0)


def _window_start(j, sub_len):
    return jnp.clip(j * Q_BLOCK - HALF_WINDOW, 0, sub_len - K_WINDOW)


def _bias_kernel(off_ref, qpos_ref, kpos_ref, tbl_ref, out_ref):
    rel = kpos_ref[0] - qpos_ref[0]
    bucket = _rel_bucket(rel)
    delta = (off_ref[pl.program_id(0)] + lax.broadcasted_iota(jnp.int32, (1, K_WINDOW), 1)
             - lax.broadcasted_iota(jnp.int32, (Q_BLOCK, 1), 0))
    valid = jnp.abs(delta) <= HALF_WINDOW
    for h in range(N_HEADS_A):
        tbl = jnp.broadcast_to(tbl_ref[h:h + 1, :], (Q_BLOCK, LANES))
        bias = jnp.concatenate(
            [jnp.take_along_axis(tbl, bucket[:, c * LANES:(c + 1) * LANES], axis=1)
             for c in range(K_WINDOW // LANES)], axis=1)
        out_ref[0, h] = jnp.where(valid, bias * LOG2_E, NEG_INF)


def _bias_tiles(offs, qpos, kpos, tbl):
    n = offs.shape[0]
    grid_spec = pltpu.PrefetchScalarGridSpec(
        num_scalar_prefetch=1,
        grid=(n,),
        in_specs=[pl.BlockSpec((1, Q_BLOCK, 1), lambda t, off: (t, 0, 0)),
                  pl.BlockSpec((1, 1, K_WINDOW), lambda t, off: (t, 0, 0)),
                  pl.BlockSpec((N_HEADS_A, LANES), lambda t, off: (0, 0))],
        out_specs=pl.BlockSpec((1, N_HEADS_A, Q_BLOCK, K_WINDOW), lambda t, off: (t, 0, 0, 0)),
    )
    return pl.pallas_call(
        _bias_kernel,
        grid_spec=grid_spec,
        out_shape=jax.ShapeDtypeStruct((n, N_HEADS_A, Q_BLOCK, K_WINDOW), F32),
        compiler_params=pltpu.CompilerParams(
            dimension_semantics=("arbitrary",), vmem_limit_bytes=VMEM_LIMIT),
        name="bias_tiles",
    )(offs, qpos, kpos, tbl)


def _dilated_kernel(q_ref, k_ref, v_ref, *refs, sub_len, qb):
    bias_refs = refs[:qb]
    o_ref, lse_ref, s_scr, p_scr = refs[qb:]
    first = lax.broadcasted_iota(jnp.int32, (1, LANES), 1) < HEAD_DIM_A
    pair = lambda h: slice((h // 2) * LANES, (h // 2 + 1) * LANES)
    for sub in range(qb):
        j = pl.program_id(2) * qb + sub
        rows = slice(sub * Q_BLOCK, (sub + 1) * Q_BLOCK)
        kstart = pl.multiple_of(_window_start(j, sub_len), HALF_WINDOW)
        q = q_ref[0, rows, :]
        kw = k_ref[0, pl.ds(kstart, K_WINDOW), :]
        vw = v_ref[0, pl.ds(kstart, K_WINDOW), :]
        for h in range(N_HEADS_A):
            own = first if h % 2 == 0 else jnp.logical_not(first)
            qh = jnp.where(own, q[:, pair(h)], jnp.zeros_like(q[:, pair(h)]))
            s_scr[sub, h] = _dot_nt(qh, kw[:, pair(h)]) + bias_refs[sub][0, h]
        stats = []
        for h in range(N_HEADS_A):
            s = s_scr[sub, h]
            m = jnp.max(s, axis=1, keepdims=True)
            e = jnp.exp2(s - m)
            l = jnp.sum(e, axis=1, keepdims=True)
            p_scr[sub, h] = e.astype(BF16)
            stats.append((1.0 / l, m + jnp.log2(l)))
        for h in range(0, N_HEADS_A, 2):
            (r0, lse0), (r1, lse1) = stats[h], stats[h + 1]
            o0 = _dot(p_scr[sub, h], vw[:, pair(h)])
            o1 = _dot(p_scr[sub, h + 1], vw[:, pair(h)])
            o_ref[0, rows, pair(h)] = jnp.where(first, o0 * r0, o1 * r1)
            lse_ref[0, rows, pair(h)] = jnp.where(first, lse0, lse1)


def _dilated(qv, kv, vv, bias, bias_index, dil):
    b, sub_len, _ = qv.shape
    w = WIDTH_A
    nq = sub_len // Q_BLOCK
    qb = min(MAX_Q_BLOCKS_PER_STEP, nq)
    assert sub_len >= K_WINDOW and sub_len % (Q_BLOCK * qb) == 0
    qspec = pl.BlockSpec((1, Q_BLOCK * qb, w), lambda bi, r, j: (bi, j, r))
    kvspec = pl.BlockSpec((1, sub_len, w), lambda bi, r, j: (bi, 0, r))
    bspecs = [pl.BlockSpec((1, N_HEADS_A, Q_BLOCK, K_WINDOW),
                           lambda bi, r, j, sub=sub:
                           (bias_index(bi, r, j * qb + sub, nq), 0, 0, 0))
              for sub in range(qb)]
    return pl.pallas_call(
        functools.partial(_dilated_kernel, sub_len=sub_len, qb=qb),
        grid=(b, dil, nq // qb),
        in_specs=[qspec, kvspec, kvspec] + bspecs,
        out_specs=[qspec, qspec],
        out_shape=[jax.ShapeDtypeStruct((b, sub_len, dil * w), F32)] * 2,
        scratch_shapes=[pltpu.VMEM((qb, N_HEADS_A, Q_BLOCK, K_WINDOW), F32),
                        pltpu.VMEM((qb, N_HEADS_A, Q_BLOCK, K_WINDOW), BF16)],
        compiler_params=pltpu.CompilerParams(
            dimension_semantics=("parallel", "parallel", "arbitrary"),
            vmem_limit_bytes=VMEM_LIMIT),
        name=f"dilated{dil}",
    )(qv, kv, vv, *([bias] * qb))


def _dilated_all(views, positions, tbl):
    b, s = positions.shape
    dils = [dil for _, dil in DILATED_PATTERNS]
    consecutive = jnp.all(positions[:, 1:] - positions[:, :-1] == 1)

    def run(bias, index_fns):
        outs = []
        for g, dil in enumerate(dils):
            qv, kv, vv = views[3 * g:3 * g + 3]
            outs.extend(_dilated(qv, kv, vv, bias, index_fns[g], dil))
        return tuple(outs)

    def shared_tiles():
        offs, qpos, kpos, fns = [], [], [], []
        for g, dil in enumerate(dils):
            for off in (0, -HALF_WINDOW, -2 * HALF_WINDOW):
                offs.append(off)
                qpos.append(dil * jnp.arange(Q_BLOCK, dtype=jnp.int32))
                kpos.append(dil * (off + jnp.arange(K_WINDOW, dtype=jnp.int32)))
            fns.append(lambda bi, r, j, nq, g=g:
                       3 * g + jnp.where(j == 0, 0, jnp.where(j == nq - 1, 2, 1)))
        bias = _bias_tiles(jnp.array(offs, jnp.int32), jnp.stack(qpos)[:, :, None],
                           jnp.stack(kpos)[:, None, :], tbl)
        return run(bias, fns)

    def per_block_tiles():
        offs, qpos, kpos, fns = [], [], [], []
        base = 0
        for g, dil in enumerate(dils):
            sub_len = s // dil
            nq = sub_len // Q_BLOCK
            pos_t = positions.reshape(b, sub_len, dil).transpose(0, 2, 1).reshape(b * dil, sub_len)
            starts = _window_start(jnp.arange(nq), sub_len)
            win = starts[:, None] + jnp.arange(K_WINDOW)[None, :]
            qpos.append(pos_t.reshape(b * dil * nq, Q_BLOCK))
            kpos.append(pos_t[:, win].reshape(b * dil * nq, K_WINDOW))
            offs.append(jnp.tile(starts - jnp.arange(nq) * Q_BLOCK, b * dil))
            fns.append(lambda bi, r, j, nq, base=base, dil=dil: base + (bi * dil + r) * nq + j)
            base += b * dil * nq
        bias = _bias_tiles(jnp.concatenate(offs).astype(jnp.int32),
                           jnp.concatenate(qpos)[:, :, None], jnp.concatenate(kpos)[:, None, :], tbl)
        return run(bias, fns)

    return lax.cond(consecutive, shared_tiles, per_block_tiles)


def _mla_kernel(q_ref, k_ref, v_ref, o_ref, *, tk):
    q = q_ref[0, 0]
    tq = q.shape[0]
    n_kv = k_ref.shape[2] // tk

    def body(i, carry):
        m, acc = carry
        start = pl.multiple_of(i * tk, tk)
        k = k_ref[0, 0, pl.ds(start, tk), :]
        v = v_ref[0, 0, pl.ds(start, tk), :]
        s = _dot_nt(q, k)
        m_new = jnp.maximum(m, jnp.max(s, axis=1, keepdims=True))
        p = jnp.exp2(s - m_new)
        acc = jnp.exp2(m - m_new) * acc + _dot(p.astype(BF16), v)
        return m_new, acc

    m0 = jnp.full((tq, 1), NEG_INF, F32)
    _, acc = lax.fori_loop(0, n_kv, body, (m0, jnp.zeros((tq, LANES), F32)), unroll=True)
    lane = lax.broadcasted_iota(jnp.int32, (1, LANES), 1)
    denom = jnp.sum(jnp.where(lane == V_DIM_B, acc, 0.0), axis=1, keepdims=True)
    o_ref[0, 0] = jnp.where(lane < V_DIM_B, acc / denom, 0.0).astype(BF16)


def _mla(qm, km, vm, tq, tk):
    b, nh, s, _ = qm.shape
    qspec = pl.BlockSpec((1, 1, tq, LANES), lambda bi, h, i: (bi, h, i, 0))
    kvspec = pl.BlockSpec((1, 1, s, LANES), lambda bi, h, i: (bi, h, 0, 0))
    return pl.pallas_call(
        functools.partial(_mla_kernel, tk=tk),
        grid=(b, nh, s // tq),
        in_specs=[qspec, kvspec, kvspec],
        out_specs=qspec,
        out_shape=jax.ShapeDtypeStruct((b, nh, s, LANES), BF16),
        compiler_params=pltpu.CompilerParams(
            dimension_semantics=("parallel", "parallel", "arbitrary"),
            vmem_limit_bytes=VMEM_LIMIT),
        name="mla",
    )(qm, km, vm)


def _memkv_kernel(mem_ref, g_ref, w_ref, k_ref, v_ref):
    d = mem_ref.shape[2]
    kv = _dot(_rms(mem_ref[0], g_ref[...]).astype(BF16), w_ref[...])
    k_ref[0] = kv[:, :d].astype(BF16)
    v_ref[0] = kv[:, d:].astype(BF16)


def _memkv(mem, g_mem, w_mkv):
    b, m, d = mem.shape
    spec = pl.BlockSpec((1, m, d), lambda bi: (bi, 0, 0))
    return pl.pallas_call(
        _memkv_kernel,
        grid=(b,),
        in_specs=[spec, pl.BlockSpec((1, d), lambda bi: (0, 0)),
                  pl.BlockSpec((d, 2 * d), lambda bi: (0, 0))],
        out_specs=[spec, spec],
        out_shape=[jax.ShapeDtypeStruct((b, m, d), BF16)] * 2,
        compiler_params=pltpu.CompilerParams(
            dimension_semantics=("parallel",), vmem_limit_bytes=VMEM_LIMIT),
        name="memkv",
    )(mem, g_mem, w_mkv)


def _post_kernel(x_ref, o1_ref, o2_ref, o3_ref, l1_ref, l2_ref, l3_ref, ob_ref,
                 goa_ref, gob_ref, woa_ref, wob_ref, gx_ref, wmq_ref, km_ref, vm_ref, wmo_ref,
                 gmoe_ref, wr_ref, br_ref,
                 h_ref, xn_ref, logit_ref, *nat_scr):
    tm = x_ref.shape[1]
    n_chunk = WIDTH_A // LANES

    for view_ref, scr in zip((o2_ref, o3_ref, l2_ref, l3_ref), nat_scr):
        dil = view_ref.shape[2] // WIDTH_A
        for r in range(dil):
            for c in range(n_chunk):
                col = r * WIDTH_A + c * LANES
                scr[c, pl.ds(r, tm // dil, stride=dil), :] = view_ref[0, :, col:col + LANES]

    def natural(scr, rows):
        return jnp.concatenate([scr[c, rows, :] for c in range(n_chunk)], axis=1)

    rows_per_chain = tm // POST_CHAINS

    def chain(ci):
        rows = slice(ci * rows_per_chain, (ci + 1) * rows_per_chain)
        o1, o2, o3 = o1_ref[0, rows, :], natural(nat_scr[0], rows), natural(nat_scr[1], rows)
        l1, l2, l3 = l1_ref[0, rows, :], natural(nat_scr[2], rows), natural(nat_scr[3], rows)
        mx = jnp.maximum(jnp.maximum(l1, l2), l3)
        w1, w2, w3 = jnp.exp2(l1 - mx), jnp.exp2(l2 - mx), jnp.exp2(l3 - mx)
        oa = (w1 * o1 + w2 * o2 + w3 * o3) / (w1 + w2 + w3)
        oan = _rms(oa, goa_ref[...]).astype(BF16)
        ob = jnp.concatenate([ob_ref[0, h, rows, :] for h in range(N_HEADS_B)], axis=1).astype(F32)
        ms_b = jnp.sum(ob * ob, axis=1, keepdims=True) * (1.0 / (N_HEADS_B * V_DIM_B))
        obn = (ob * lax.rsqrt(ms_b + EPS) * gob_ref[...]).astype(BF16)
        yield
        h1 = x_ref[0, rows, :] + _dot(oan, woa_ref[...]) + _dot(obn, wob_ref[...])
        yield
        hn = _rms(h1, gx_ref[...]).astype(BF16)
        yield
        q = _dot(hn, wmq_ref[...]).astype(BF16)
        yield
        dh = q.shape[1] // N_HEADS_MEM
        heads = []
        for h in range(N_HEADS_MEM):
            sl = slice(h * dh, (h + 1) * dh)
            s = _dot_nt(q[:, sl], km_ref[0, :, sl])
            yield
            e = jnp.exp(s - jnp.max(s, axis=1, keepdims=True))
            p = e / jnp.sum(e, axis=1, keepdims=True)
            yield
            heads.append(_dot(p.astype(BF16), vm_ref[0, :, sl]))
        o = jnp.concatenate(heads, axis=1).astype(BF16)
        yield
        h2 = h1 + _dot(o, wmo_ref[...])
        h_ref[0, rows, :] = h2
        yield
        xn = _rms(h2, gmoe_ref[...])
        packed = _pack_bf16_pairs(xn)
        rows_per_token = packed.shape[1] // LANES
        _store_token_rows(xn_ref.at[pl.ds(ci * rows_per_chain * rows_per_token,
                                          rows_per_chain * rows_per_token)], packed)
        yield
        logit_ref[0, rows, :] = jnp.dot(xn, wr_ref[...], preferred_element_type=F32,
                                        precision=lax.Precision.HIGHEST) + br_ref[...]

    _run_skewed([chain(ci) for ci in range(POST_CHAINS)])


def _post(x, o_pats, lse_pats, ob, g_out_a, g_out_b_pad, w_o_a, w_o_b_pad, g_xattn, w_mq, kmem,
          vmem, w_mo, g_moe, w_router_pad, b_router_pad, tm):
    b, s, d = x.shape
    m = kmem.shape[1]
    full = lambda shape: pl.BlockSpec(shape, lambda bi, i: (0,) * len(shape))
    row = lambda w: pl.BlockSpec((1, tm, w), lambda bi, i: (bi, i, 0))
    memspec = pl.BlockSpec((1, m, d), lambda bi, i: (bi, 0, 0))
    hw = N_HEADS_B * LANES
    views = [pl.BlockSpec((1, tm // dil, dil * WIDTH_A), lambda bi, i: (bi, i, 0))
             for _, dil in DILATED_PATTERNS]
    return pl.pallas_call(
        _post_kernel,
        grid=(b, s // tm),
        in_specs=[row(d)] + views + views
        + [pl.BlockSpec((1, N_HEADS_B, tm, LANES), lambda bi, i: (bi, 0, i, 0)),
           full((1, WIDTH_A)), full((1, hw)), full((WIDTH_A, d)), full((hw, d)),
           full((1, d)), full((d, d)), memspec, memspec, full((d, d)),
           full((1, d)), full((d, LANES)), full((1, LANES))],
        out_specs=[row(d), pl.BlockSpec((tm * d // 2 // LANES, LANES), lambda bi, i: (bi * (s // tm) + i, 0)),
                   row(LANES)],
        out_shape=[jax.ShapeDtypeStruct((b, s, d), F32),
                   jax.ShapeDtypeStruct((b * s * d // 2 // LANES, LANES), jnp.uint32),
                   jax.ShapeDtypeStruct((b, s, LANES), F32)],
        scratch_shapes=[pltpu.VMEM((WIDTH_A // LANES, tm, LANES), F32)] * 4,
        compiler_params=pltpu.CompilerParams(
            dimension_semantics=("parallel", "parallel"), vmem_limit_bytes=VMEM_LIMIT),
        name="post",
    )(x, *o_pats, *lse_pats, ob, g_out_a, g_out_b_pad, w_o_a, w_o_b_pad, g_xattn, w_mq, kmem,
      vmem, w_mo, g_moe, w_router_pad, b_router_pad)


def _route_kernel(logit_ref, gate_ref, slot_ref, cslot_ref, count_ref, tcnt_ref, tcarry_ref, ccnt_ref,
                  ccarry_ref, carry_ref):
    @pl.when(pl.program_id(0) == 0)
    def _():
        carry_ref[...] = jnp.zeros_like(carry_ref)

    l = logit_ref[...]
    tr = l.shape[0]
    lane_i = lax.broadcasted_iota(jnp.int32, l.shape, 1)
    lane = lane_i.astype(F32)
    vals, idxs = [], []
    for _ in range(TOP_K):
        m = jnp.max(l, axis=1, keepdims=True)
        idx = jnp.min(jnp.where(l == m, lane, float(LANES)), axis=1, keepdims=True)
        vals.append(m)
        idxs.append(idx)
        l = jnp.where(lane == idx, -jnp.inf, l)
    exps = [jnp.exp(v - vals[0]) for v in vals]
    denom = exps[0] + exps[1] + exps[2] + exps[3]
    onehot = jnp.zeros(l.shape, F32)
    for idx in idxs:
        onehot = onehot + jnp.where(lane == idx, 1.0, 0.0)
    r = lax.broadcasted_iota(jnp.int32, (tr, tr), 0)
    c = lax.broadcasted_iota(jnp.int32, (tr, tr), 1)
    tri = jnp.where(c < r, 1.0, 0.0).astype(BF16)
    local = _dot(tri, onehot.astype(BF16))
    tile_cnt = jnp.sum(onehot, axis=0, keepdims=True)
    li = lax.broadcasted_iota(jnp.int32, (LANES, LANES), 0)
    lj = lax.broadcasted_iota(jnp.int32, (LANES, LANES), 1)
    prefix = jnp.dot(jnp.broadcast_to(tile_cnt, (8, LANES)), jnp.where(li < lj, 1.0, 0.0),
                     preferred_element_type=F32, precision=lax.Precision.HIGHEST)[0:1]
    grouped = local + prefix
    n_sub = tr // COMBINE_TILE
    sub_cnt = [jnp.sum(onehot[u * COMBINE_TILE:(u + 1) * COMBINE_TILE], axis=0, keepdims=True)
               for u in range(n_sub)]
    sub_prefix = jnp.dot(jnp.concatenate(sub_cnt + [jnp.zeros((8 - n_sub, LANES), F32)], axis=0),
                         jnp.where(li < lj, 1.0, 0.0), preferred_element_type=F32,
                         precision=lax.Precision.HIGHEST)
    shifts, sub_base = [], jnp.zeros((1, LANES), F32)
    for u in range(n_sub):
        shifts.append(jnp.broadcast_to(sub_prefix[u:u + 1] - sub_base, (COMBINE_TILE, LANES)))
        ccnt_ref[u * 8:(u + 1) * 8, :] = jnp.broadcast_to(sub_cnt[u], (8, LANES)).astype(jnp.int32)
        ccarry_ref[u * 8:(u + 1) * 8, :] = jnp.broadcast_to(carry_ref[...] + sub_base,
                                                            (8, LANES)).astype(jnp.int32)
        sub_base = sub_base + sub_cnt[u]
    sub_grouped = local + jnp.concatenate(shifts, axis=0)
    gate = jnp.zeros(l.shape, F32)
    slot = jnp.zeros(l.shape, jnp.int32)
    cslot = jnp.zeros(l.shape, jnp.int32)
    for k in range(TOP_K):
        mine = lane == idxs[k]
        sk = jnp.sum(jnp.where(mine, grouped, 0.0), axis=1, keepdims=True)
        ck = jnp.sum(jnp.where(mine, sub_grouped, 0.0), axis=1, keepdims=True)
        gate = jnp.where(lane_i == k, exps[k] / denom, gate)
        slot = jnp.where(lane_i == k, sk.astype(jnp.int32), slot)
        cslot = jnp.where(lane_i == k, ck.astype(jnp.int32), cslot)
    gate_ref[...] = gate
    slot_ref[...] = slot
    cslot_ref[...] = cslot
    tcnt_ref[...] = jnp.broadcast_to(tile_cnt, tcnt_ref.shape).astype(jnp.int32)
    tcarry_ref[...] = jnp.broadcast_to(carry_ref[...], tcarry_ref.shape).astype(jnp.int32)
    total = carry_ref[...] + tile_cnt
    carry_ref[...] = total
    count_ref[...] = total.astype(jnp.int32)


def _route(logits, tr):
    t = logits.shape[0]
    spec = pl.BlockSpec((tr, LANES), lambda i: (i, 0))
    return pl.pallas_call(
        _route_kernel,
        grid=(t // tr,),
        in_specs=[spec],
        out_specs=[spec, spec, spec, pl.BlockSpec((1, LANES), lambda i: (0, 0)),
                   pl.BlockSpec((8, LANES), lambda i: (i, 0)), pl.BlockSpec((8, LANES), lambda i: (i, 0)),
                   pl.BlockSpec((8 * tr // COMBINE_TILE, LANES), lambda i: (i, 0)),
                   pl.BlockSpec((8 * tr // COMBINE_TILE, LANES), lambda i: (i, 0))],
        out_shape=[jax.ShapeDtypeStruct((t, LANES), F32)] + [jax.ShapeDtypeStruct((t, LANES), jnp.int32)] * 2
        + [jax.ShapeDtypeStruct((1, LANES), jnp.int32)]
        + [jax.ShapeDtypeStruct((t // tr * 8, LANES), jnp.int32)] * 2
        + [jax.ShapeDtypeStruct((t // COMBINE_TILE * 8, LANES), jnp.int32)] * 2,
        scratch_shapes=[pltpu.VMEM((1, LANES), F32)],
        compiler_params=pltpu.CompilerParams(
            dimension_semantics=("arbitrary",), vmem_limit_bytes=VMEM_LIMIT),
        name="route",
    )(logits)


def _dispatch_kernel(dst_ref, cnt_ref, ends_ref, x_ref, slot_ref, xs_hbm, zbuf, stage, sem, zsem):
    c = pl.program_id(0)
    buf = lax.rem(c, 2)
    n_chunk = x_ref.shape[0] // DISPATCH_CHUNK
    blk_rows = zbuf.shape[0]
    n_staged = DISPATCH_CHUNK * TOP_K

    @pl.when(c == 0)
    def _():
        zbuf[...] = jnp.zeros_like(zbuf)
        for b in range(2):
            stage[b, pl.ds(n_staged * n_chunk, DISPATCH_PIECE * n_chunk), :] = jnp.zeros(
                (DISPATCH_PIECE * n_chunk, LANES), stage.dtype)

        def zero_copy(start):
            return pltpu.make_async_copy(
                zbuf, xs_hbm.at[pl.ds(pl.multiple_of(start * n_chunk, n_chunk), blk_rows)], zsem)

        blk = blk_rows // n_chunk
        used_end = ends_ref[N_EXPERTS]
        starts = [(ends_ref[e + 1] - ends_ref[e] >= back * blk, ends_ref[e + 1] - back * blk)
                  for e in range(N_EXPERTS) for back in (1, 2)]
        starts += [((used_end + (j + 1) * blk) * n_chunk <= xs_hbm.shape[0], used_end + j * blk)
                   for j in range(N_EXPERTS)]
        for cond, start in starts:
            @pl.when(cond)
            def _():
                zero_copy(start).start()
        for cond, start in starts:
            @pl.when(cond)
            def _():
                zero_copy(start).wait()

    x_lo, x_hi = _unpack_bf16_pairs(_load_token_rows(x_ref, DISPATCH_CHUNK))
    x_lo, x_hi = x_lo.astype(BF16), x_hi.astype(BF16)
    slot_t = jnp.transpose(slot_ref[...].astype(F32))
    bits = lambda v: lax.bitcast_convert_type(v, jnp.uint32)
    for part in range(TOP_K):
        j = (lax.broadcasted_iota(jnp.int32, (DISPATCH_CHUNK, 1), 0) + part * DISPATCH_CHUNK).astype(F32)
        pick = jnp.zeros((DISPATCH_CHUNK, DISPATCH_CHUNK), F32)
        for k in range(TOP_K):
            pick = jnp.where(slot_t[k:k + 1, :] == j, 1.0, pick)
        pick = pick.astype(BF16)
        packed = bits(_dot(pick, x_hi)) | (bits(_dot(pick, x_lo)) >> 16)
        _store_token_rows(
            stage.at[buf, pl.ds(part * DISPATCH_CHUNK * n_chunk, DISPATCH_CHUNK * n_chunk)], packed)

    def piece(src_tok, dst_tok):
        rows = DISPATCH_PIECE * n_chunk
        return pltpu.make_async_copy(
            stage.at[buf, pl.ds(pl.multiple_of(src_tok * n_chunk, n_chunk), rows)],
            xs_hbm.at[pl.ds(pl.multiple_of(dst_tok * n_chunk, n_chunk), rows)], sem)

    def drain(step):
        n_pieces = 0
        for e in range(N_EXPERTS):
            n_pieces = n_pieces + (cnt_ref[step * N_EXPERTS + e] + DISPATCH_PIECE - 1) // DISPATCH_PIECE

        def wait_one(i, carry):
            piece(0, 0).wait()
            return carry
        lax.fori_loop(0, n_pieces, wait_one, 0)

    @pl.when(c > 0)
    def _():
        drain(c - 1)

    staged = 0
    for e in range(N_EXPERTS):
        cnt = cnt_ref[c * N_EXPERTS + e]
        dst = dst_ref[c * N_EXPERTS + e]
        for p in range(DISPATCH_CHUNK // DISPATCH_PIECE):
            @pl.when(p * DISPATCH_PIECE < cnt)
            def _():
                piece(staged + p * DISPATCH_PIECE, dst + p * DISPATCH_PIECE).start()
        staged = staged + cnt

    @pl.when(c == pl.num_programs(0) - 1)
    def _():
        drain(c)


def _dispatch(dst, cnt, ends0, xn_rows, slots, n_tok, n_slots):
    n_chunk = xn_rows.shape[0] // n_tok
    staged_rows = (DISPATCH_CHUNK * TOP_K + DISPATCH_PIECE) * n_chunk
    grid_spec = pltpu.PrefetchScalarGridSpec(
        num_scalar_prefetch=3,
        grid=(n_tok // DISPATCH_CHUNK,),
        in_specs=[pl.BlockSpec((DISPATCH_CHUNK * n_chunk, LANES), lambda c, *_: (c, 0)),
                  pl.BlockSpec((DISPATCH_CHUNK, LANES), lambda c, *_: (c, 0))],
        out_specs=pl.BlockSpec(memory_space=pl.ANY),
        scratch_shapes=[pltpu.VMEM((EXPERT_BLOCK * n_chunk, LANES), xn_rows.dtype),
                        pltpu.VMEM((2, staged_rows, LANES), xn_rows.dtype),
                        pltpu.SemaphoreType.DMA(()), pltpu.SemaphoreType.DMA(())],
    )
    return pl.pallas_call(
        _dispatch_kernel,
        grid_spec=grid_spec,
        out_shape=jax.ShapeDtypeStruct((n_slots * n_chunk, LANES), xn_rows.dtype),
        compiler_params=pltpu.CompilerParams(
            dimension_semantics=("arbitrary",), vmem_limit_bytes=VMEM_LIMIT),
        name="dispatch",
    )(dst, cnt, ends0, xn_rows, slots)


def _expert_kernel(be_ref, nused_ref, next_ref, valid_ref, xs_ref, wgu_hbm, bgu_ref, wd_hbm, bd_ref, y_ref,
                   wgu_f32, wd_f32, wgu_bf, wd_bf, sems):
    i = pl.program_id(0)
    n_chunk = xs_ref.shape[0] // EXPERT_BLOCK
    half = EXPERT_BLOCK // 2

    def mlp(n_rows):
        part = pl.ds(0, n_rows * n_chunk)
        x_lo, x_hi = _unpack_bf16_pairs(_load_token_rows(xs_ref.at[part], n_rows))
        x = jnp.concatenate([x_lo.astype(BF16), x_hi.astype(BF16)], axis=1)
        gu = _dot(x, wgu_bf[...]) + bgu_ref[0]
        de = gu.shape[1] // 2
        gate = jnp.minimum(gu[:, :de], SWIGLU_LIMIT)
        up = jnp.clip(gu[:, de:], -SWIGLU_LIMIT, SWIGLU_LIMIT)
        hmid = (up + 1.0) * (gate * jax.nn.sigmoid(SWIGLU_ALPHA * gate))
        y = _dot(hmid.astype(BF16), wd_bf[...]) + bd_ref[0]
        _store_token_rows(y_ref.at[part], _pack_bf16_pairs(y))

    def fetch(e):
        return (pltpu.make_async_copy(wgu_hbm.at[e], wgu_f32, sems.at[0]),
                pltpu.make_async_copy(wd_hbm.at[e], wd_f32, sems.at[1]))

    @pl.when(i == 0)
    def _():
        for copy in fetch(be_ref[0]):
            copy.start()

    @pl.when(i < nused_ref[0])
    def _():
        @pl.when(jnp.logical_or(i == 0, be_ref[i] != be_ref[jnp.maximum(i - 1, 0)]))
        def _():
            for copy in fetch(be_ref[i]):
                copy.wait()
            wgu_bf[...] = wgu_f32[...].astype(BF16)
            wd_bf[...] = wd_f32[...].astype(BF16)

            @pl.when(next_ref[i] >= 0)
            def _():
                for copy in fetch(next_ref[i]):
                    copy.start()

        valid = valid_ref[i]

        @pl.when(valid > half)
        def _():
            mlp(EXPERT_BLOCK)

        @pl.when(jnp.logical_and(valid > 0, valid <= half))
        def _():
            mlp(half)
            y_ref[pl.ds(half * n_chunk, half * n_chunk), :] = jnp.zeros((half * n_chunk, LANES), y_ref.dtype)

    @pl.when(jnp.logical_or(i >= nused_ref[0], valid_ref[i] == 0))
    def _():
        y_ref[...] = jnp.zeros_like(y_ref)


def _experts(block_expert, n_used, next_expert, valid_rows, xs_rows, w_gate_up, b_gate_up, w_down, b_down):
    d, de2 = w_gate_up.shape[1:]
    n_chunk = d // 2 // LANES
    blk = EXPERT_BLOCK
    n_blk = xs_rows.shape[0] // (blk * n_chunk)
    grid_spec = pltpu.PrefetchScalarGridSpec(
        num_scalar_prefetch=4,
        grid=(n_blk,),
        in_specs=[pl.BlockSpec((blk * n_chunk, LANES),
                               lambda i, be, nu, *_: (jnp.maximum(jnp.minimum(i, nu[0] - 1), 0), 0)),
                  pl.BlockSpec(memory_space=pl.ANY),
                  pl.BlockSpec((1, 1, de2), lambda i, be, *_: (be[i], 0, 0)),
                  pl.BlockSpec(memory_space=pl.ANY),
                  pl.BlockSpec((1, 1, d), lambda i, be, *_: (be[i], 0, 0))],
        out_specs=pl.BlockSpec((blk * n_chunk, LANES), lambda i, *_: (i, 0)),
        scratch_shapes=[pltpu.VMEM((d, de2), F32), pltpu.VMEM((de2 // 2, d), F32),
                        pltpu.VMEM((d, de2), BF16), pltpu.VMEM((de2 // 2, d), BF16),
                        pltpu.SemaphoreType.DMA((2,))],
    )
    return pl.pallas_call(
        _expert_kernel,
        grid_spec=grid_spec,
        out_shape=jax.ShapeDtypeStruct(xs_rows.shape, xs_rows.dtype),
        compiler_params=pltpu.CompilerParams(
            dimension_semantics=("arbitrary",), vmem_limit_bytes=VMEM_LIMIT),
        name="experts",
    )(block_expert, n_used, next_expert, valid_rows, xs_rows, w_gate_up, b_gate_up.reshape(N_EXPERTS, 1, de2),
      w_down, b_down.reshape(N_EXPERTS, 1, d))


def _combine_kernel(start_ref, len_ref, h_ref, gate_ref, slot_ref, gfin_ref, y_hbm, out_ref, stage, sems):
    i = pl.program_id(0)
    n = pl.num_programs(0)
    buf = lax.rem(i, 2)
    tc = h_ref.shape[0]
    n_staged = tc * TOP_K
    n_chunk = stage.shape[1] // n_staged

    def issue(tile, b):
        staged = 0
        for e in range(N_EXPERTS):
            run_len = len_ref[tile * N_EXPERTS + e]
            run_start = start_ref[tile * N_EXPERTS + e]
            for bit in reversed(range(COMBINE_TILE.bit_length())):
                size = 1 << bit
                done = (run_len >> (bit + 1)) << (bit + 1)

                @pl.when((run_len >> bit) & 1 == 1)
                def _():
                    pltpu.make_async_copy(
                        y_hbm.at[pl.ds(pl.multiple_of((run_start + done) * n_chunk, n_chunk), size * n_chunk)],
                        stage.at[b, pl.ds(pl.multiple_of((staged + done) * n_chunk, n_chunk), size * n_chunk)],
                        sems.at[b]).start()
            staged = staged + run_len

    @pl.when(i == 0)
    def _():
        issue(0, 0)

    @pl.when(i + 1 < n)
    def _():
        issue(i + 1, 1 - buf)

    pltpu.make_async_copy(y_hbm.at[pl.ds(0, n_staged * n_chunk)], stage.at[buf], sems.at[buf]).wait()

    y_lo, y_hi = _unpack_bf16_pairs(_load_token_rows(stage.at[buf], n_staged))
    y_lo, y_hi = y_lo.astype(BF16), y_hi.astype(BF16)
    gate = gate_ref[...]
    slot = slot_ref[...].astype(F32)
    j = lax.broadcasted_iota(jnp.int32, (1, n_staged), 1).astype(F32)
    pick = jnp.zeros((tc, n_staged), F32)
    for k in range(TOP_K):
        pick = jnp.where(slot[:, k:k + 1] == j, gate[:, k:k + 1], pick)
    pick_hi = pick.astype(BF16)
    pick_lo = (pick - pick_hi.astype(F32)).astype(BF16)
    moe = jnp.concatenate([_dot(pick_hi, y_lo) + _dot(pick_lo, y_lo),
                           _dot(pick_hi, y_hi) + _dot(pick_lo, y_hi)], axis=1)
    out_ref[...] = _rms(h_ref[...] + moe, gfin_ref[...])


def _combine(run_start, run_len, h2, gates, slots, g_final, y_rows):
    t, d = h2.shape
    tc = COMBINE_TILE
    n_chunk = d // 2 // LANES
    grid_spec = pltpu.PrefetchScalarGridSpec(
        num_scalar_prefetch=2,
        grid=(t // tc,),
        in_specs=[pl.BlockSpec((tc, d), lambda i, *_: (i, 0)),
                  pl.BlockSpec((tc, LANES), lambda i, *_: (i, 0)),
                  pl.BlockSpec((tc, LANES), lambda i, *_: (i, 0)),
                  pl.BlockSpec((1, d), lambda i, *_: (0, 0)),
                  pl.BlockSpec(memory_space=pl.ANY)],
        out_specs=pl.BlockSpec((tc, d), lambda i, *_: (i, 0)),
        scratch_shapes=[pltpu.VMEM((2, tc * TOP_K * n_chunk, LANES), y_rows.dtype),
                        pltpu.SemaphoreType.DMA((2,))],
    )
    return pl.pallas_call(
        _combine_kernel,
        grid_spec=grid_spec,
        out_shape=jax.ShapeDtypeStruct((t, d), F32),
        compiler_params=pltpu.CompilerParams(
            dimension_semantics=("arbitrary",), vmem_limit_bytes=VMEM_LIMIT),
        name="combine",
    )(run_start, run_len, h2, gates, slots, g_final, y_rows)


def _pad_heads(w, n_heads, width, offset=0):
    k = w.shape[0]
    w = w.reshape(k, n_heads, width)
    w = jnp.pad(w, ((0, 0), (0, 0), (offset, LANES - width - offset)))
    return w.reshape(k, n_heads * LANES)


def _rot_cols(w):
    half = w.shape[-1] // 2
    return jnp.concatenate([-w[..., half:], w[..., :half]], axis=-1)


def kernel(x, mem, positions, g_mix, w_in, g_q_a, w_q_b, g_kv_a, w_kv_b, rel_bias, g_out_a, g_out_b,
           w_o, g_xattn, g_mem, w_mq, w_mkv, w_mo, g_moe, w_router, b_router, w_gate_up, b_gate_up,
           w_down, b_down, g_final):
    b, s, d = x.shape
    t = b * s
    assert g_mix.shape[0] == 1, "single-layer block: the final norm is fused into the last stage"

    inv_freq = ROPE_THETA ** (-jnp.arange(0, QK_ROPE_DIM, 2, dtype=F32) / QK_ROPE_DIM)
    ang = positions.astype(F32)[..., None] * inv_freq
    cos_sin = jnp.concatenate([jnp.cos(ang), jnp.sin(ang)], axis=-1)
    half = QK_ROPE_DIM // 2
    src = jnp.arange(QK_ROPE_DIM)[:, None]
    lane = jnp.arange(LANES)[None, :]
    on_rope = (lane >= QK_NOPE_DIM) & (lane < QK_NOPE_DIM + QK_ROPE_DIM)
    place = ((lane - QK_NOPE_DIM) % half == src % half) & on_rope
    expand = jnp.concatenate([place & (src < half), place & (src >= half)], axis=1).astype(F32)
    tbl = jnp.pad(rel_bias.T.astype(F32), ((0, 0), (0, LANES - NUM_BUCKETS)))

    h = x
    for l in range(1):
        c0 = 3 * WIDTH_A + Q_LORA_RANK + KV_LORA_RANK
        w_kpe = w_in[l][:, c0:c0 + QK_ROPE_DIM]
        place = lambda w: jnp.pad(w, ((0, 0), (QK_NOPE_DIM, LANES - QK_NOPE_DIM - QK_ROPE_DIM)))
        w_in_ext = jnp.concatenate([w_in[l][:, :c0], place(w_kpe), place(_rot_cols(w_kpe))], 1).astype(BF16)
        dq = QK_NOPE_DIM + QK_ROPE_DIM
        wq3 = w_q_b[l].reshape(Q_LORA_RANK, N_HEADS_B, dq)
        wq = _pad_heads(wq3.reshape(Q_LORA_RANK, -1), N_HEADS_B, dq).astype(BF16)
        wq_rot3 = jnp.concatenate([jnp.zeros_like(wq3[..., :QK_NOPE_DIM]), _rot_cols(wq3[..., QK_NOPE_DIM:])], -1)
        wqr = _pad_heads(wq_rot3.reshape(Q_LORA_RANK, -1), N_HEADS_B, dq).astype(BF16)
        wkv3 = w_kv_b[l].reshape(KV_LORA_RANK, N_HEADS_B, QK_NOPE_DIM + V_DIM_B)
        wk = _pad_heads(wkv3[..., :QK_NOPE_DIM].reshape(KV_LORA_RANK, -1), N_HEADS_B, QK_NOPE_DIM).astype(BF16)
        wv = _pad_heads(wkv3[..., QK_NOPE_DIM:].reshape(KV_LORA_RANK, -1), N_HEADS_B, V_DIM_B).astype(BF16)

        *views, qm, km, vm = _proj(h, cos_sin, expand, g_mix[l][None], w_in_ext, g_q_a[l][None], wq, wqr,
                                   g_kv_a[l][None], wk, wv, tm=256 * PROJ_CHAINS)
        pats = _dilated_all(views, positions, tbl)
        ob = _mla(qm, km, vm, tq=2048, tk=512)
        kmem, vmem = _memkv(mem, g_mem[l][None], w_mkv[l].astype(BF16))

        g_out_b_pad = _pad_heads(g_out_b[l][None], N_HEADS_B, V_DIM_B)
        w_o_b_pad = _pad_heads(w_o[l][WIDTH_A:].T, N_HEADS_B, V_DIM_B).T.astype(BF16)
        w_mq_s = (w_mq[l] * ((d // N_HEADS_MEM) ** -0.5)).astype(BF16)
        w_router_pad = jnp.pad(w_router[l], ((0, 0), (0, LANES - N_EXPERTS)))
        b_router_pad = jnp.pad(b_router[l][None], ((0, 0), (0, LANES - N_EXPERTS)), constant_values=NEG_INF)
        h2, xn_rows, logits = _post(h, pats[0::2], pats[1::2], ob, g_out_a[l][None],
                               g_out_b_pad, w_o[l][:WIDTH_A].astype(BF16), w_o_b_pad, g_xattn[l][None],
                               w_mq_s, kmem, vmem, w_mo[l].astype(BF16), g_moe[l][None], w_router_pad,
                               b_router_pad, tm=256 * POST_CHAINS)

        gates, slots, cslots, counts, tile_cnt, tile_carry, ctile_cnt, ctile_carry = _route(
            logits.reshape(t, LANES), tr=DISPATCH_CHUNK)
        counts = counts[0, :N_EXPERTS]
        blk = EXPERT_BLOCK
        padded = jnp.where(counts > 0, (counts + DISPATCH_PIECE + blk - 1) // blk * blk, 0)
        ends = jnp.cumsum(padded)
        pad_start = ends - padded
        n_blk = t * TOP_K // blk + N_EXPERTS + -(-N_EXPERTS * DISPATCH_PIECE // blk)
        block_expert = jnp.minimum(
            jnp.sum(ends[None, :] <= (jnp.arange(n_blk) * blk)[:, None], axis=1),
            N_EXPERTS - 1).astype(jnp.int32)
        n_used = (ends[-1] // blk).astype(jnp.int32)[None]
        ends0 = jnp.concatenate([jnp.zeros((1,), jnp.int32), ends.astype(jnp.int32)])
        run_start = (pad_start[None, :] + tile_carry[::8, :N_EXPERTS]).astype(jnp.int32).reshape(-1)
        run_len = tile_cnt[::8, :N_EXPERTS].reshape(-1)
        xs_rows = _dispatch(run_start, run_len, ends0, xn_rows, slots, t, n_blk * blk)
        expert_ids = jnp.arange(N_EXPERTS, dtype=jnp.int32)
        run_end = jnp.sum(jnp.where(block_expert[:, None] == expert_ids, ends[None, :], 0), axis=1)
        next_expert = jnp.where(
            run_end < ends[-1],
            jnp.minimum(jnp.sum(ends[None, :] <= run_end[:, None], axis=1), N_EXPERTS - 1), -1).astype(jnp.int32)
        block_of = block_expert[:, None] == expert_ids
        real_end = jnp.sum(jnp.where(block_of, (pad_start + counts)[None, :], 0), axis=1)
        valid_rows = jnp.clip(real_end - jnp.arange(n_blk) * blk, 0, blk).astype(jnp.int32)
        y_rows = _experts(block_expert, n_used, next_expert, valid_rows, xs_rows, w_gate_up[l], b_gate_up[l], w_down[l], b_down[l])
        gather_start = (pad_start[None, :] + ctile_carry[::8, :N_EXPERTS]).astype(jnp.int32).reshape(-1)
        gather_len = ctile_cnt[::8, :N_EXPERTS].reshape(-1)
        out = _combine(gather_start, gather_len, h2.reshape(t, d), gates, cslots, g_final[None], y_rows)
        h = out.reshape(b, s, d)
    return h
```

```python
import functools
import math

import jax
import jax.numpy as jnp
from jax import lax
from jax.experimental import pallas as pl
from jax.experimental.pallas import tpu as pltpu

F32 = jnp.float32
BF16 = jnp.bfloat16

LANES = 128
EPS = 1e-6
NEG_INF = -1e30
LOG2_E = math.log2(math.e)

N_HEADS_A = 8
HEAD_DIM_A = 64
WIDTH_A = N_HEADS_A * HEAD_DIM_A
DILATED_PATTERNS = ((128, 1), (512, 4), (2048, 16))
N_HEADS_B = 8
QK_NOPE_DIM = 64
QK_ROPE_DIM = 32
V_DIM_B = 64
Q_LORA_RANK = 256
KV_LORA_RANK = 128
ROPE_THETA = 10000.0
NUM_BUCKETS = 32
MAX_DISTANCE = 1024
N_HEADS_MEM = 4
N_EXPERTS = 32
TOP_K = 4
SWIGLU_LIMIT = 7.0
SWIGLU_ALPHA = 1.702

Q_BLOCK = 128
K_WINDOW = 256
HALF_WINDOW = (K_WINDOW - Q_BLOCK) // 2
MAX_Q_BLOCKS_PER_STEP = 4
EXPERT_BLOCK = 512
EXPERT_STEP = 128
DISPATCH_CHUNK = 512
DISPATCH_PIECE = 64
COMBINE_TILE = 128
PROJ_CHAINS = 2
POST_CHAINS = 2
VMEM_LIMIT = 56 * 1024 * 1024


def _rms(x, g):
    return x * lax.rsqrt(jnp.mean(x * x, axis=-1, keepdims=True) + EPS) * g


def _dot(a, b):
    return jnp.dot(a, b, preferred_element_type=F32)


def _dot_nt(a, b):
    return lax.dot_general(a, b, (((1,), (1,)), ((), ())), preferred_element_type=F32)


def _run_skewed(chains):
    live = []
    pending = list(chains)
    while live or pending:
        if pending:
            live.append(pending.pop(0))
        for gen in list(live):
            try:
                next(gen)
            except StopIteration:
                live.remove(gen)


def _load_token_rows(ref, n_tok):
    n_chunk = ref.shape[0] // n_tok
    return jnp.concatenate([ref[pl.ds(c, n_tok, stride=n_chunk), :] for c in range(n_chunk)], axis=1)


def _pack_bf16_pairs(x):
    half = x.shape[1] // 2
    bits = lambda v: lax.bitcast_convert_type(v.astype(BF16).astype(F32), jnp.uint32)
    return bits(x[:, half:]) | (bits(x[:, :half]) >> 16)


def _unpack_bf16_pairs(w):
    lo = lax.bitcast_convert_type(w << 16, F32)
    hi = lax.bitcast_convert_type(w & jnp.uint32(0xFFFF0000), F32)
    return lo, hi


def _store_token_rows(ref, value):
    n_tok = value.shape[0]
    n_chunk = value.shape[1] // LANES
    for c in range(n_chunk):
        ref[pl.ds(c, n_tok, stride=n_chunk), :] = value[:, c * LANES:(c + 1) * LANES]


def _proj_kernel(x_ref, cs_ref, expand_ref, gmix_ref, win_ref, gq_ref, wq_ref, wqr_ref,
                 gkv_ref, wk_ref, wv_ref,
                 q1_ref, k1_ref, v1_ref, q4_ref, k4_ref, v4_ref, q16_ref, k16_ref, v16_ref,
                 qm_ref, km_ref, vm_ref, za_scr):
    tm = x_ref.shape[1]
    c0 = 3 * WIDTH_A
    c1 = c0 + Q_LORA_RANK + KV_LORA_RANK
    per_group = WIDTH_A // LANES
    rows_per_chain = tm // PROJ_CHAINS
    lane = lax.broadcasted_iota(jnp.int32, (1, LANES), 1)
    in_rope = jnp.logical_and(lane >= QK_NOPE_DIM, lane < QK_NOPE_DIM + QK_ROPE_DIM)
    ones_col = jnp.where(lane == V_DIM_B, 1.0, 0.0).astype(F32)
    scale = (QK_NOPE_DIM + QK_ROPE_DIM) ** -0.5 * LOG2_E

    def chain(ci):
        r0 = ci * rows_per_chain
        rows = slice(r0, r0 + rows_per_chain)
        xn = _rms(x_ref[0, rows, :], gmix_ref[...]).astype(BF16)
        yield
        z = _dot(xn, win_ref[...])
        yield
        for c in range(3 * per_group):
            chunk = z[:, c * LANES:(c + 1) * LANES]
            za_scr[c, rows, :] = chunk * (HEAD_DIM_A ** -0.5 * LOG2_E) if c < per_group else chunk
        for dil, refs in ((1, (q1_ref, k1_ref, v1_ref)), (4, (q4_ref, k4_ref, v4_ref)),
                          (16, (q16_ref, k16_ref, v16_ref))):
            n = rows_per_chain // dil
            for r in range(dil):
                for c in range(3 * per_group):
                    picked = za_scr[c, pl.ds(r0 + r, n, stride=dil), :].astype(BF16)
                    col = r * WIDTH_A + (c % per_group) * LANES
                    refs[c // per_group][0, r0 // dil:r0 // dil + n, col:col + LANES] = picked
        zqn = _rms(z[:, c0:c0 + Q_LORA_RANK], gq_ref[...]).astype(BF16)
        zkvn = _rms(z[:, c0 + Q_LORA_RANK:c1], gkv_ref[...]).astype(BF16)
        yield
        spread = jnp.dot(cs_ref[0, rows, :], expand_ref[...], preferred_element_type=F32,
                         precision=lax.Precision.HIGHEST)
        q = _dot(zqn, wq_ref[...])
        qr = _dot(zqn, wqr_ref[...])
        kn = _dot(zkvn, wk_ref[...])
        vv = _dot(zkvn, wv_ref[...])
        yield
        cos = jnp.where(in_rope, spread[:, :LANES], 1.0)
        sin = spread[:, LANES:]
        kpe = z[:, c1:c1 + LANES] * cos + z[:, c1 + LANES:c1 + 2 * LANES] * sin
        for h in range(N_HEADS_B):
            sl = slice(h * LANES, (h + 1) * LANES)
            qm_ref[0, h, rows, :] = ((q[:, sl] * cos + qr[:, sl] * sin) * scale).astype(BF16)
            km_ref[0, h, rows, :] = (kn[:, sl] + kpe).astype(BF16)
            vm_ref[0, h, rows, :] = (vv[:, sl] + ones_col).astype(BF16)

    _run_skewed([chain(ci) for ci in range(PROJ_CHAINS)])


def _proj(x, cos_sin, expand, g_mix, w_in_ext, g_q, wq, wqr, g_kv, wk, wv, tm):
    b, s, d = x.shape
    n_in = w_in_ext.shape[1]
    hw = N_HEADS_B * LANES
    full = lambda shape: pl.BlockSpec(shape, lambda bi, i: (0,) * len(shape))
    row = lambda w: pl.BlockSpec((1, tm, w), lambda bi, i: (bi, i, 0))
    head = pl.BlockSpec((1, N_HEADS_B, tm, LANES), lambda bi, i: (bi, 0, i, 0))
    dils = [dil for _, dil in DILATED_PATTERNS]
    view_specs = [pl.BlockSpec((1, tm // dil, dil * WIDTH_A), lambda bi, i: (bi, i, 0))
                  for dil in dils for _ in range(3)]
    view_shapes = [jax.ShapeDtypeStruct((b, s // dil, dil * WIDTH_A), BF16)
                   for dil in dils for _ in range(3)]
    return pl.pallas_call(
        _proj_kernel,
        grid=(b, s // tm),
        in_specs=[row(d), row(cos_sin.shape[2]), full(expand.shape), full((1, d)), full((d, n_in)),
                  full((1, Q_LORA_RANK)), full((Q_LORA_RANK, hw)), full((Q_LORA_RANK, hw)),
                  full((1, KV_LORA_RANK)), full((KV_LORA_RANK, hw)), full((KV_LORA_RANK, hw))],
        out_specs=view_specs + [head, head, head],
        out_shape=view_shapes + [jax.ShapeDtypeStruct((b, N_HEADS_B, s, LANES), BF16)] * 3,
        scratch_shapes=[pltpu.VMEM((3 * WIDTH_A // LANES, tm, LANES), F32)],
        compiler_params=pltpu.CompilerParams(
            dimension_semantics=("parallel", "parallel"), vmem_limit_bytes=VMEM_LIMIT),
        name="proj",
    )(x, cos_sin, expand, g_mix, w_in_ext, g_q, wq, wqr, g_kv, wk, wv)


_LOG_BUCKET_STARTS = tuple(
    next(n for n in range(8, 4096)
         if int(math.log(n / 8) / math.log(MAX_DISTANCE / 8) * 8) >= t)
    for t in range(1, 8))


def _rel_bucket(rel):
    n = jnp.abs(rel)
    large = jnp.full(rel.shape, NUM_BUCKETS // 4, jnp.int32)
    for start in _LOG_BUCKET_STARTS:
        large = large + jnp.where(n >= start, 1, 0)
    mag = jnp.where(n < NUM_BUCKETS // 4, n, large)
    return mag + jnp.where(rel > 0, NUM_BUCKETS // 2, 0)


def _window_start(j, sub_len):
    return jnp.clip(j * Q_BLOCK - HALF_WINDOW, 0, sub_len - K_WINDOW)


def _bias_kernel(off_ref, qpos_ref, kpos_ref, tbl_ref, out_ref):
    rel = kpos_ref[0] - qpos_ref[0]
    bucket = _rel_bucket(rel)
    delta = (off_ref[pl.program_id(0)] + lax.broadcasted_iota(jnp.int32, (1, K_WINDOW), 1)
             - lax.broadcasted_iota(jnp.int32, (Q_BLOCK, 1), 0))
    valid = jnp.abs(delta) <= HALF_WINDOW
    for h in range(N_HEADS_A):
        tbl = jnp.broadcast_to(tbl_ref[h:h + 1, :], (Q_BLOCK, LANES))
        bias = jnp.concatenate(
            [jnp.take_along_axis(tbl, bucket[:, c * LANES:(c + 1) * LANES], axis=1)
             for c in range(K_WINDOW // LANES)], axis=1)
        out_ref[0, h] = jnp.where(valid, bias * LOG2_E, NEG_INF)


def _bias_tiles(offs, qpos, kpos, tbl):
    n = offs.shape[0]
    grid_spec = pltpu.PrefetchScalarGridSpec(
        num_scalar_prefetch=1,
        grid=(n,),
        in_specs=[pl.BlockSpec((1, Q_BLOCK, 1), lambda t, off: (t, 0, 0)),
                  pl.BlockSpec((1, 1, K_WINDOW), lambda t, off: (t, 0, 0)),
                  pl.BlockSpec((N_HEADS_A, LANES), lambda t, off: (0, 0))],
        out_specs=pl.BlockSpec((1, N_HEADS_A, Q_BLOCK, K_WINDOW), lambda t, off: (t, 0, 0, 0)),
    )
    return pl.pallas_call(
        _bias_kernel,
        grid_spec=grid_spec,
        out_shape=jax.ShapeDtypeStruct((n, N_HEADS_A, Q_BLOCK, K_WINDOW), F32),
        compiler_params=pltpu.CompilerParams(
            dimension_semantics=("arbitrary",), vmem_limit_bytes=VMEM_LIMIT),
        name="bias_tiles",
    )(offs, qpos, kpos, tbl)


def _dilated_kernel(q_ref, k_ref, v_ref, *refs, sub_len, qb):
    bias_refs = refs[:qb]
    o_ref, lse_ref, s_scr, p_scr = refs[qb:]
    first = lax.broadcasted_iota(jnp.int32, (1, LANES), 1) < HEAD_DIM_A
    pair = lambda h: slice((h // 2) * LANES, (h // 2 + 1) * LANES)
    for sub in range(qb):
        j = pl.program_id(2) * qb + sub
        rows = slice(sub * Q_BLOCK, (sub + 1) * Q_BLOCK)
        kstart = pl.multiple_of(_window_start(j, sub_len), HALF_WINDOW)
        q = q_ref[0, rows, :]
        kw = k_ref[0, pl.ds(kstart, K_WINDOW), :]
        vw = v_ref[0, pl.ds(kstart, K_WINDOW), :]
        for h in range(N_HEADS_A):
            own = first if h % 2 == 0 else jnp.logical_not(first)
            qh = jnp.where(own, q[:, pair(h)], jnp.zeros_like(q[:, pair(h)]))
            s_scr[sub, h] = _dot_nt(qh, kw[:, pair(h)]) + bias_refs[sub][0, h]
        stats = []
        for h in range(N_HEADS_A):
            s = s_scr[sub, h]
            m = jnp.max(s, axis=1, keepdims=True)
            e = jnp.exp2(s - m)
            l = jnp.sum(e, axis=1, keepdims=True)
            p_scr[sub, h] = e.astype(BF16)
            stats.append((1.0 / l, m + jnp.log2(l)))
        for h in range(0, N_HEADS_A, 2):
            (r0, lse0), (r1, lse1) = stats[h], stats[h + 1]
            o0 = _dot(p_scr[sub, h], vw[:, pair(h)])
            o1 = _dot(p_scr[sub, h + 1], vw[:, pair(h)])
            o_ref[0, rows, pair(h)] = jnp.where(first, o0 * r0, o1 * r1)
            lse_ref[0, rows, pair(h)] = jnp.where(first, lse0, lse1)


def _dilated(qv, kv, vv, bias, bias_index, dil):
    b, sub_len, _ = qv.shape
    w = WIDTH_A
    nq = sub_len // Q_BLOCK
    qb = min(MAX_Q_BLOCKS_PER_STEP, nq)
    assert sub_len >= K_WINDOW and sub_len % (Q_BLOCK * qb) == 0
    qspec = pl.BlockSpec((1, Q_BLOCK * qb, w), lambda bi, r, j: (bi, j, r))
    kvspec = pl.BlockSpec((1, sub_len, w), lambda bi, r, j: (bi, 0, r))
    bspecs = [pl.BlockSpec((1, N_HEADS_A, Q_BLOCK, K_WINDOW),
                           lambda bi, r, j, sub=sub:
                           (bias_index(bi, r, j * qb + sub, nq), 0, 0, 0))
              for sub in range(qb)]
    return pl.pallas_call(
        functools.partial(_dilated_kernel, sub_len=sub_len, qb=qb),
        grid=(b, dil, nq // qb),
        in_specs=[qspec, kvspec, kvspec] + bspecs,
        out_specs=[qspec, qspec],
        out_shape=[jax.ShapeDtypeStruct((b, sub_len, dil * w), F32)] * 2,
        scratch_shapes=[pltpu.VMEM((qb, N_HEADS_A, Q_BLOCK, K_WINDOW), F32),
                        pltpu.VMEM((qb, N_HEADS_A, Q_BLOCK, K_WINDOW), BF16)],
        compiler_params=pltpu.CompilerParams(
            dimension_semantics=("parallel", "parallel", "arbitrary"),
            vmem_limit_bytes=VMEM_LIMIT),
        name=f"dilated{dil}",
    )(qv, kv, vv, *([bias] * qb))


def _dilated_all(views, positions, tbl):
    b, s = positions.shape
    dils = [dil for _, dil in DILATED_PATTERNS]
    consecutive = jnp.all(positions[:, 1:] - positions[:, :-1] == 1)

    def run(bias, index_fns):
        outs = []
        for g, dil in enumerate(dils):
            qv, kv, vv = views[3 * g:3 * g + 3]
            outs.extend(_dilated(qv, kv, vv, bias, index_fns[g], dil))
        return tuple(outs)

    def shared_tiles():
        offs, qpos, kpos, fns = [], [], [], []
        for g, dil in enumerate(dils):
            for off in (0, -HALF_WINDOW, -2 * HALF_WINDOW):
                offs.append(off)
                qpos.append(dil * jnp.arange(Q_BLOCK, dtype=jnp.int32))
                kpos.append(dil * (off + jnp.arange(K_WINDOW, dtype=jnp.int32)))
            fns.append(lambda bi, r, j, nq, g=g:
                       3 * g + jnp.where(j == 0, 0, jnp.where(j == nq - 1, 2, 1)))
        bias = _bias_tiles(jnp.array(offs, jnp.int32), jnp.stack(qpos)[:, :, None],
                           jnp.stack(kpos)[:, None, :], tbl)
        return run(bias, fns)

    def per_block_tiles():
        offs, qpos, kpos, fns = [], [], [], []
        base = 0
        for g, dil in enumerate(dils):
            sub_len = s // dil
            nq = sub_len // Q_BLOCK
            pos_t = positions.reshape(b, sub_len, dil).transpose(0, 2, 1).reshape(b * dil, sub_len)
            starts = _window_start(jnp.arange(nq), sub_len)
            win = starts[:, None] + jnp.arange(K_WINDOW)[None, :]
            qpos.append(pos_t.reshape(b * dil * nq, Q_BLOCK))
            kpos.append(pos_t[:, win].reshape(b * dil * nq, K_WINDOW))
            offs.append(jnp.tile(starts - jnp.arange(nq) * Q_BLOCK, b * dil))
            fns.append(lambda bi, r, j, nq, base=base, dil=dil: base + (bi * dil + r) * nq + j)
            base += b * dil * nq
        bias = _bias_tiles(jnp.concatenate(offs).astype(jnp.int32),
                           jnp.concatenate(qpos)[:, :, None], jnp.concatenate(kpos)[:, None, :], tbl)
        return run(bias, fns)

    return lax.cond(consecutive, shared_tiles, per_block_tiles)


def _mla_kernel(q_ref, k_ref, v_ref, o_ref, *, tk):
    q = q_ref[0, 0]
    tq = q.shape[0]
    n_kv = k_ref.shape[2] // tk

    def body(i, carry):
        m, acc = carry
        start = pl.multiple_of(i * tk, tk)
        k = k_ref[0, 0, pl.ds(start, tk), :]
        v = v_ref[0, 0, pl.ds(start, tk), :]
        s = _dot_nt(q, k)
        m_new = jnp.maximum(m, jnp.max(s, axis=1, keepdims=True))
        p = jnp.exp2(s - m_new)
        acc = jnp.exp2(m - m_new) * acc + _dot(p.astype(BF16), v)
        return m_new, acc

    m0 = jnp.full((tq, 1), NEG_INF, F32)
    _, acc = lax.fori_loop(0, n_kv, body, (m0, jnp.zeros((tq, LANES), F32)), unroll=True)
    lane = lax.broadcasted_iota(jnp.int32, (1, LANES), 1)
    denom = jnp.sum(jnp.where(lane == V_DIM_B, acc, 0.0), axis=1, keepdims=True)
    o_ref[0, 0] = jnp.where(lane < V_DIM_B, acc / denom, 0.0).astype(BF16)


def _mla(qm, km, vm, tq, tk):
    b, nh, s, _ = qm.shape
    qspec = pl.BlockSpec((1, 1, tq, LANES), lambda bi, h, i: (bi, h, i, 0))
    kvspec = pl.BlockSpec((1, 1, s, LANES), lambda bi, h, i: (bi, h, 0, 0))
    return pl.pallas_call(
        functools.partial(_mla_kernel, tk=tk),
        grid=(b, nh, s // tq),
        in_specs=[qspec, kvspec, kvspec],
        out_specs=qspec,
        out_shape=jax.ShapeDtypeStruct((b, nh, s, LANES), BF16),
        compiler_params=pltpu.CompilerParams(
            dimension_semantics=("parallel", "parallel", "arbitrary"),
            vmem_limit_bytes=VMEM_LIMIT),
        name="mla",
    )(qm, km, vm)


def _memkv_kernel(mem_ref, g_ref, w_ref, k_ref, v_ref):
    d = mem_ref.shape[2]
    kv = _dot(_rms(mem_ref[0], g_ref[...]).astype(BF16), w_ref[...])
    k_ref[0] = kv[:, :d].astype(BF16)
    v_ref[0] = kv[:, d:].astype(BF16)


def _memkv(mem, g_mem, w_mkv):
    b, m, d = mem.shape
    spec = pl.BlockSpec((1, m, d), lambda bi: (bi, 0, 0))
    return pl.pallas_call(
        _memkv_kernel,
        grid=(b,),
        in_specs=[spec, pl.BlockSpec((1, d), lambda bi: (0, 0)),
                  pl.BlockSpec((d, 2 * d), lambda bi: (0, 0))],
        out_specs=[spec, spec],
        out_shape=[jax.ShapeDtypeStruct((b, m, d), BF16)] * 2,
        compiler_params=pltpu.CompilerParams(
            dimension_semantics=("parallel",), vmem_limit_bytes=VMEM_LIMIT),
        name="memkv",
    )(mem, g_mem, w_mkv)


def _post_kernel(x_ref, o1_ref, o2_ref, o3_ref, l1_ref, l2_ref, l3_ref, ob_ref,
                 goa_ref, gob_ref, woa_ref, wob_ref, gx_ref, wmq_ref, km_ref, vm_ref, wmo_ref,
                 gmoe_ref, wr_ref, br_ref,
                 h_ref, xn_ref, logit_ref, *nat_scr):
    tm = x_ref.shape[1]
    n_chunk = WIDTH_A // LANES

    for view_ref, scr in zip((o2_ref, o3_ref, l2_ref, l3_ref), nat_scr):
        dil = view_ref.shape[2] // WIDTH_A
        for r in range(dil):
            for c in range(n_chunk):
                col = r * WIDTH_A + c * LANES
                scr[c, pl.ds(r, tm // dil, stride=dil), :] = view_ref[0, :, col:col + LANES]

    def natural(scr, rows):
        return jnp.concatenate([scr[c, rows, :] for c in range(n_chunk)], axis=1)

    rows_per_chain = tm // POST_CHAINS

    def chain(ci):
        rows = slice(ci * rows_per_chain, (ci + 1) * rows_per_chain)
        o1, o2, o3 = o1_ref[0, rows, :], natural(nat_scr[0], rows), natural(nat_scr[1], rows)
        l1, l2, l3 = l1_ref[0, rows, :], natural(nat_scr[2], rows), natural(nat_scr[3], rows)
        mx = jnp.maximum(jnp.maximum(l1, l2), l3)
        w1, w2, w3 = jnp.exp2(l1 - mx), jnp.exp2(l2 - mx), jnp.exp2(l3 - mx)
        oa = (w1 * o1 + w2 * o2 + w3 * o3) / (w1 + w2 + w3)
        oan = _rms(oa, goa_ref[...]).astype(BF16)
        ob = jnp.concatenate([ob_ref[0, h, rows, :] for h in range(N_HEADS_B)], axis=1).astype(F32)
        ms_b = jnp.sum(ob * ob, axis=1, keepdims=True) * (1.0 / (N_HEADS_B * V_DIM_B))
        obn = (ob * lax.rsqrt(ms_b + EPS) * gob_ref[...]).astype(BF16)
        yield
        h1 = x_ref[0, rows, :] + _dot(oan, woa_ref[...]) + _dot(obn, wob_ref[...])
        yield
        hn = _rms(h1, gx_ref[...]).astype(BF16)
        yield
        q = _dot(hn, wmq_ref[...]).astype(BF16)
        yield
        dh = q.shape[1] // N_HEADS_MEM
        heads = []
        for h in range(N_HEADS_MEM):
            sl = slice(h * dh, (h + 1) * dh)
            s = _dot_nt(q[:, sl], km_ref[0, :, sl])
            yield
            e = jnp.exp(s - jnp.max(s, axis=1, keepdims=True))
            p = e / jnp.sum(e, axis=1, keepdims=True)
            yield
            heads.append(_dot(p.astype(BF16), vm_ref[0, :, sl]))
        o = jnp.concatenate(heads, axis=1).astype(BF16)
        yield
        h2 = h1 + _dot(o, wmo_ref[...])
        h_ref[0, rows, :] = h2
        yield
        xn = _rms(h2, gmoe_ref[...])
        packed = _pack_bf16_pairs(xn)
        rows_per_token = packed.shape[1] // LANES
        _store_token_rows(xn_ref.at[pl.ds(ci * rows_per_chain * rows_per_token,
                                          rows_per_chain * rows_per_token)], packed)
        yield
        logit_ref[0, rows, :] = jnp.dot(xn, wr_ref[...], preferred_element_type=F32,
                                        precision=lax.Precision.HIGHEST) + br_ref[...]

    _run_skewed([chain(ci) for ci in range(POST_CHAINS)])


def _post(x, o_pats, lse_pats, ob, g_out_a, g_out_b_pad, w_o_a, w_o_b_pad, g_xattn, w_mq, kmem,
          vmem, w_mo, g_moe, w_router_pad, b_router_pad, tm):
    b, s, d = x.shape
    m = kmem.shape[1]
    full = lambda shape: pl.BlockSpec(shape, lambda bi, i: (0,) * len(shape))
    row = lambda w: pl.BlockSpec((1, tm, w), lambda bi, i: (bi, i, 0))
    memspec = pl.BlockSpec((1, m, d), lambda bi, i: (bi, 0, 0))
    hw = N_HEADS_B * LANES
    views = [pl.BlockSpec((1, tm // dil, dil * WIDTH_A), lambda bi, i: (bi, i, 0))
             for _, dil in DILATED_PATTERNS]
    return pl.pallas_call(
        _post_kernel,
        grid=(b, s // tm),
        in_specs=[row(d)] + views + views
        + [pl.BlockSpec((1, N_HEADS_B, tm, LANES), lambda bi, i: (bi, 0, i, 0)),
           full((1, WIDTH_A)), full((1, hw)), full((WIDTH_A, d)), full((hw, d)),
           full((1, d)), full((d, d)), memspec, memspec, full((d, d)),
           full((1, d)), full((d, LANES)), full((1, LANES))],
        out_specs=[row(d), pl.BlockSpec((tm * d // 2 // LANES, LANES), lambda bi, i: (bi * (s // tm) + i, 0)),
                   row(LANES)],
        out_shape=[jax.ShapeDtypeStruct((b, s, d), F32),
                   jax.ShapeDtypeStruct((b * s * d // 2 // LANES, LANES), jnp.uint32),
                   jax.ShapeDtypeStruct((b, s, LANES), F32)],
        scratch_shapes=[pltpu.VMEM((WIDTH_A // LANES, tm, LANES), F32)] * 4,
        compiler_params=pltpu.CompilerParams(
            dimension_semantics=("parallel", "parallel"), vmem_limit_bytes=VMEM_LIMIT),
        name="post",
    )(x, *o_pats, *lse_pats, ob, g_out_a, g_out_b_pad, w_o_a, w_o_b_pad, g_xattn, w_mq, kmem,
      vmem, w_mo, g_moe, w_router_pad, b_router_pad)


def _route_kernel(logit_ref, gate_ref, slot_ref, cslot_ref, count_ref, tcnt_ref, tcarry_ref, ccnt_ref,
                  ccarry_ref, carry_ref):
    @pl.when(pl.program_id(0) == 0)
    def _():
        carry_ref[...] = jnp.zeros_like(carry_ref)

    l = logit_ref[...]
    tr = l.shape[0]
    lane_i = lax.broadcasted_iota(jnp.int32, l.shape, 1)
    lane = lane_i.astype(F32)
    vals, idxs = [], []
    for _ in range(TOP_K):
        m = jnp.max(l, axis=1, keepdims=True)
        idx = jnp.min(jnp.where(l == m, lane, float(LANES)), axis=1, keepdims=True)
        vals.append(m)
        idxs.append(idx)
        l = jnp.where(lane == idx, -jnp.inf, l)
    exps = [jnp.exp(v - vals[0]) for v in vals]
    denom = exps[0] + exps[1] + exps[2] + exps[3]
    onehot = jnp.zeros(l.shape, F32)
    for idx in idxs:
        onehot = onehot + jnp.where(lane == idx, 1.0, 0.0)
    r = lax.broadcasted_iota(jnp.int32, (tr, tr), 0)
    c = lax.broadcasted_iota(jnp.int32, (tr, tr), 1)
    tri = jnp.where(c < r, 1.0, 0.0).astype(BF16)
    local = _dot(tri, onehot.astype(BF16))
    tile_cnt = jnp.sum(onehot, axis=0, keepdims=True)
    li = lax.broadcasted_iota(jnp.int32, (LANES, LANES), 0)
    lj = lax.broadcasted_iota(jnp.int32, (LANES, LANES), 1)
    prefix = jnp.dot(jnp.broadcast_to(tile_cnt, (8, LANES)), jnp.where(li < lj, 1.0, 0.0),
                     preferred_element_type=F32, precision=lax.Precision.HIGHEST)[0:1]
    grouped = local + prefix
    n_sub = tr // COMBINE_TILE
    sub_cnt = [jnp.sum(onehot[u * COMBINE_TILE:(u + 1) * COMBINE_TILE], axis=0, keepdims=True)
               for u in range(n_sub)]
    sub_prefix = jnp.dot(jnp.concatenate(sub_cnt + [jnp.zeros((8 - n_sub, LANES), F32)], axis=0),
                         jnp.where(li < lj, 1.0, 0.0), preferred_element_type=F32,
                         precision=lax.Precision.HIGHEST)
    shifts, sub_base = [], jnp.zeros((1, LANES), F32)
    for u in range(n_sub):
        shifts.append(jnp.broadcast_to(sub_prefix[u:u + 1] - sub_base, (COMBINE_TILE, LANES)))
        ccnt_ref[u * 8:(u + 1) * 8, :] = jnp.broadcast_to(sub_cnt[u], (8, LANES)).astype(jnp.int32)
        ccarry_ref[u * 8:(u + 1) * 8, :] = jnp.broadcast_to(carry_ref[...] + sub_base,
                                                            (8, LANES)).astype(jnp.int32)
        sub_base = sub_base + sub_cnt[u]
    sub_grouped = local + jnp.concatenate(shifts, axis=0)
    gate = jnp.zeros(l.shape, F32)
    slot = jnp.zeros(l.shape, jnp.int32)
    cslot = jnp.zeros(l.shape, jnp.int32)
    for k in range(TOP_K):
        mine = lane == idxs[k]
        sk = jnp.sum(jnp.where(mine, grouped, 0.0), axis=1, keepdims=True)
        ck = jnp.sum(jnp.where(mine, sub_grouped, 0.0), axis=1, keepdims=True)
        gate = jnp.where(lane_i == k, exps[k] / denom, gate)
        slot = jnp.where(lane_i == k, sk.astype(jnp.int32), slot)
        cslot = jnp.where(lane_i == k, ck.astype(jnp.int32), cslot)
    gate_ref[...] = gate
    slot_ref[...] = slot
    cslot_ref[...] = cslot
    tcnt_ref[...] = jnp.broadcast_to(tile_cnt, tcnt_ref.shape).astype(jnp.int32)
    tcarry_ref[...] = jnp.broadcast_to(carry_ref[...], tcarry_ref.shape).astype(jnp.int32)
    total = carry_ref[...] + tile_cnt
    carry_ref[...] = total
    count_ref[...] = total.astype(jnp.int32)


def _route(logits, tr):
    t = logits.shape[0]
    spec = pl.BlockSpec((tr, LANES), lambda i: (i, 0))
    return pl.pallas_call(
        _route_kernel,
        grid=(t // tr,),
        in_specs=[spec],
        out_specs=[spec, spec, spec, pl.BlockSpec((1, LANES), lambda i: (0, 0)),
                   pl.BlockSpec((8, LANES), lambda i: (i, 0)), pl.BlockSpec((8, LANES), lambda i: (i, 0)),
                   pl.BlockSpec((8 * tr // COMBINE_TILE, LANES), lambda i: (i, 0)),
                   pl.BlockSpec((8 * tr // COMBINE_TILE, LANES), lambda i: (i, 0))],
        out_shape=[jax.ShapeDtypeStruct((t, LANES), F32)] + [jax.ShapeDtypeStruct((t, LANES), jnp.int32)] * 2
        + [jax.ShapeDtypeStruct((1, LANES), jnp.int32)]
        + [jax.ShapeDtypeStruct((t // tr * 8, LANES), jnp.int32)] * 2
        + [jax.ShapeDtypeStruct((t // COMBINE_TILE * 8, LANES), jnp.int32)] * 2,
        scratch_shapes=[pltpu.VMEM((1, LANES), F32)],
        compiler_params=pltpu.CompilerParams(
            dimension_semantics=("arbitrary",), vmem_limit_bytes=VMEM_LIMIT),
        name="route",
    )(logits)


def _dispatch_kernel(dst_ref, cnt_ref, ends_ref, x_ref, slot_ref, xs_hbm, zbuf, stage, sem, zsem):
    c = pl.program_id(0)
    buf = lax.rem(c, 2)
    n_chunk = x_ref.shape[0] // DISPATCH_CHUNK
    blk_rows = zbuf.shape[0]
    n_staged = DISPATCH_CHUNK * TOP_K

    @pl.when(c == 0)
    def _():
        zbuf[...] = jnp.zeros_like(zbuf)
        for b in range(2):
            stage[b, pl.ds(n_staged * n_chunk, DISPATCH_PIECE * n_chunk), :] = jnp.zeros(
                (DISPATCH_PIECE * n_chunk, LANES), stage.dtype)

        def zero_copy(start):
            return pltpu.make_async_copy(
                zbuf, xs_hbm.at[pl.ds(pl.multiple_of(start * n_chunk, n_chunk), blk_rows)], zsem)

        blk = blk_rows // n_chunk
        used_end = ends_ref[N_EXPERTS]
        starts = [(ends_ref[e + 1] - ends_ref[e] >= back * blk, ends_ref[e + 1] - back * blk)
                  for e in range(N_EXPERTS) for back in (1, 2)]
        starts += [((used_end + (j + 1) * blk) * n_chunk <= xs_hbm.shape[0], used_end + j * blk)
                   for j in range(N_EXPERTS)]
        for cond, start in starts:
            @pl.when(cond)
            def _():
                zero_copy(start).start()
        for cond, start in starts:
            @pl.when(cond)
            def _():
                zero_copy(start).wait()

    x_lo, x_hi = _unpack_bf16_pairs(_load_token_rows(x_ref, DISPATCH_CHUNK))
    x_lo, x_hi = x_lo.astype(BF16), x_hi.astype(BF16)
    slot_t = jnp.transpose(slot_ref[...].astype(F32))
    bits = lambda v: lax.bitcast_convert_type(v, jnp.uint32)
    for part in range(TOP_K):
        j = (lax.broadcasted_iota(jnp.int32, (DISPATCH_CHUNK, 1), 0) + part * DISPATCH_CHUNK).astype(F32)
        pick = jnp.zeros((DISPATCH_CHUNK, DISPATCH_CHUNK), F32)
        for k in range(TOP_K):
            pick = jnp.where(slot_t[k:k + 1, :] == j, 1.0, pick)
        pick = pick.astype(BF16)
        packed = bits(_dot(pick, x_hi)) | (bits(_dot(pick, x_lo)) >> 16)
        _store_token_rows(
            stage.at[buf, pl.ds(part * DISPATCH_CHUNK * n_chunk, DISPATCH_CHUNK * n_chunk)], packed)

    def piece(src_tok, dst_tok):
        rows = DISPATCH_PIECE * n_chunk
        return pltpu.make_async_copy(
            stage.at[buf, pl.ds(pl.multiple_of(src_tok * n_chunk, n_chunk), rows)],
            xs_hbm.at[pl.ds(pl.multiple_of(dst_tok * n_chunk, n_chunk), rows)], sem)

    def drain(step):
        n_pieces = 0
        for e in range(N_EXPERTS):
            n_pieces = n_pieces + (cnt_ref[step * N_EXPERTS + e] + DISPATCH_PIECE - 1) // DISPATCH_PIECE

        def wait_one(i, carry):
            piece(0, 0).wait()
            return carry
        lax.fori_loop(0, n_pieces, wait_one, 0)

    @pl.when(c > 0)
    def _():
        drain(c - 1)

    staged = 0
    for e in range(N_EXPERTS):
        cnt = cnt_ref[c * N_EXPERTS + e]
        dst = dst_ref[c * N_EXPERTS + e]
        for p in range(DISPATCH_CHUNK // DISPATCH_PIECE):
            @pl.when(p * DISPATCH_PIECE < cnt)
            def _():
                piece(staged + p * DISPATCH_PIECE, dst + p * DISPATCH_PIECE).start()
        staged = staged + cnt

    @pl.when(c == pl.num_programs(0) - 1)
    def _():
        drain(c)


def _dispatch(dst, cnt, ends0, xn_rows, slots, n_tok, n_slots):
    n_chunk = xn_rows.shape[0] // n_tok
    staged_rows = (DISPATCH_CHUNK * TOP_K + DISPATCH_PIECE) * n_chunk
    grid_spec = pltpu.PrefetchScalarGridSpec(
        num_scalar_prefetch=3,
        grid=(n_tok // DISPATCH_CHUNK,),
        in_specs=[pl.BlockSpec((DISPATCH_CHUNK * n_chunk, LANES), lambda c, *_: (c, 0)),
                  pl.BlockSpec((DISPATCH_CHUNK, LANES), lambda c, *_: (c, 0))],
        out_specs=pl.BlockSpec(memory_space=pl.ANY),
        scratch_shapes=[pltpu.VMEM((EXPERT_BLOCK * n_chunk, LANES), xn_rows.dtype),
                        pltpu.VMEM((2, staged_rows, LANES), xn_rows.dtype),
                        pltpu.SemaphoreType.DMA(()), pltpu.SemaphoreType.DMA(())],
    )
    return pl.pallas_call(
        _dispatch_kernel,
        grid_spec=grid_spec,
        out_shape=jax.ShapeDtypeStruct((n_slots * n_chunk, LANES), xn_rows.dtype),
        compiler_params=pltpu.CompilerParams(
            dimension_semantics=("arbitrary",), vmem_limit_bytes=VMEM_LIMIT),
        name="dispatch",
    )(dst, cnt, ends0, xn_rows, slots)


def _expert_kernel(be_ref, nused_ref, next_ref, valid_ref, xs_ref, wgu_hbm, bgu_ref, wd_hbm, bd_ref, y_ref,
                   wgu_f32, wd_f32, wgu_bf, wd_bf, sems):
    i = pl.program_id(0)
    n_chunk = xs_ref.shape[0] // EXPERT_BLOCK

    def mlp(n_rows):
        part = pl.ds(0, n_rows * n_chunk)
        x_lo, x_hi = _unpack_bf16_pairs(_load_token_rows(xs_ref.at[part], n_rows))
        x = jnp.concatenate([x_lo.astype(BF16), x_hi.astype(BF16)], axis=1)
        gu = _dot(x, wgu_bf[...]) + bgu_ref[0]
        de = gu.shape[1] // 2
        gate = jnp.minimum(gu[:, :de], SWIGLU_LIMIT)
        up = jnp.clip(gu[:, de:], -SWIGLU_LIMIT, SWIGLU_LIMIT)
        hmid = (up + 1.0) * (gate * jax.nn.sigmoid(SWIGLU_ALPHA * gate))
        y = _dot(hmid.astype(BF16), wd_bf[...]) + bd_ref[0]
        _store_token_rows(y_ref.at[part], _pack_bf16_pairs(y))

    def fetch(e):
        return (pltpu.make_async_copy(wgu_hbm.at[e], wgu_f32, sems.at[0]),
                pltpu.make_async_copy(wd_hbm.at[e], wd_f32, sems.at[1]))

    @pl.when(i == 0)
    def _():
        for copy in fetch(be_ref[0]):
            copy.start()

    @pl.when(i < nused_ref[0])
    def _():
        @pl.when(jnp.logical_or(i == 0, be_ref[i] != be_ref[jnp.maximum(i - 1, 0)]))
        def _():
            for copy in fetch(be_ref[i]):
                copy.wait()
            wgu_bf[...] = wgu_f32[...].astype(BF16)
            wd_bf[...] = wd_f32[...].astype(BF16)

            @pl.when(next_ref[i] >= 0)
            def _():
                for copy in fetch(next_ref[i]):
                    copy.start()

        valid = valid_ref[i]
        for n_rows in range(EXPERT_STEP, EXPERT_BLOCK + 1, EXPERT_STEP):
            @pl.when(jnp.logical_and(valid > n_rows - EXPERT_STEP, valid <= n_rows))
            def _():
                mlp(n_rows)
                if n_rows < EXPERT_BLOCK:
                    rest = (EXPERT_BLOCK - n_rows) * n_chunk
                    y_ref[pl.ds(n_rows * n_chunk, rest), :] = jnp.zeros((rest, LANES), y_ref.dtype)

    @pl.when(jnp.logical_or(i >= nused_ref[0], valid_ref[i] == 0))
    def _():
        y_ref[...] = jnp.zeros_like(y_ref)


def _experts(block_expert, n_used, next_expert, valid_rows, xs_rows, w_gate_up, b_gate_up, w_down, b_down):
    d, de2 = w_gate_up.shape[1:]
    n_chunk = d // 2 // LANES
    blk = EXPERT_BLOCK
    n_blk = xs_rows.shape[0] // (blk * n_chunk)
    grid_spec = pltpu.PrefetchScalarGridSpec(
        num_scalar_prefetch=4,
        grid=(n_blk,),
        in_specs=[pl.BlockSpec((blk * n_chunk, LANES),
                               lambda i, be, nu, *_: (jnp.maximum(jnp.minimum(i, nu[0] - 1), 0), 0)),
                  pl.BlockSpec(memory_space=pl.ANY),
                  pl.BlockSpec((1, 1, de2), lambda i, be, *_: (be[i], 0, 0)),
                  pl.BlockSpec(memory_space=pl.ANY),
                  pl.BlockSpec((1, 1, d), lambda i, be, *_: (be[i], 0, 0))],
        out_specs=pl.BlockSpec((blk * n_chunk, LANES), lambda i, *_: (i, 0)),
        scratch_shapes=[pltpu.VMEM((d, de2), F32), pltpu.VMEM((de2 // 2, d), F32),
                        pltpu.VMEM((d, de2), BF16), pltpu.VMEM((de2 // 2, d), BF16),
                        pltpu.SemaphoreType.DMA((2,))],
    )
    return pl.pallas_call(
        _expert_kernel,
        grid_spec=grid_spec,
        out_shape=jax.ShapeDtypeStruct(xs_rows.shape, xs_rows.dtype),
        compiler_params=pltpu.CompilerParams(
            dimension_semantics=("arbitrary",), vmem_limit_bytes=VMEM_LIMIT),
        name="experts",
    )(block_expert, n_used, next_expert, valid_rows, xs_rows, w_gate_up, b_gate_up.reshape(N_EXPERTS, 1, de2),
      w_down, b_down.reshape(N_EXPERTS, 1, d))


def _combine_kernel(start_ref, len_ref, h_ref, gate_ref, slot_ref, gfin_ref, y_hbm, out_ref, stage, sems):
    i = pl.program_id(0)
    n = pl.num_programs(0)
    buf = lax.rem(i, 2)
    tc = h_ref.shape[0]
    n_staged = tc * TOP_K
    n_chunk = stage.shape[1] // n_staged

    def issue(tile, b):
        staged = 0
        for e in range(N_EXPERTS):
            run_len = len_ref[tile * N_EXPERTS + e]
            run_start = start_ref[tile * N_EXPERTS + e]
            for bit in reversed(range(COMBINE_TILE.bit_length())):
                size = 1 << bit
                done = (run_len >> (bit + 1)) << (bit + 1)

                @pl.when((run_len >> bit) & 1 == 1)
                def _():
                    pltpu.make_async_copy(
                        y_hbm.at[pl.ds(pl.multiple_of((run_start + done) * n_chunk, n_chunk), size * n_chunk)],
                        stage.at[b, pl.ds(pl.multiple_of((staged + done) * n_chunk, n_chunk), size * n_chunk)],
                        sems.at[b]).start()
            staged = staged + run_len

    @pl.when(i == 0)
    def _():
        issue(0, 0)

    @pl.when(i + 1 < n)
    def _():
        issue(i + 1, 1 - buf)

    pltpu.make_async_copy(y_hbm.at[pl.ds(0, n_staged * n_chunk)], stage.at[buf], sems.at[buf]).wait()

    y_lo, y_hi = _unpack_bf16_pairs(_load_token_rows(stage.at[buf], n_staged))
    y_lo, y_hi = y_lo.astype(BF16), y_hi.astype(BF16)
    gate = gate_ref[...]
    slot = slot_ref[...].astype(F32)
    j = lax.broadcasted_iota(jnp.int32, (1, n_staged), 1).astype(F32)
    pick = jnp.zeros((tc, n_staged), F32)
    for k in range(TOP_K):
        pick = jnp.where(slot[:, k:k + 1] == j, gate[:, k:k + 1], pick)
    pick_hi = pick.astype(BF16)
    pick_lo = (pick - pick_hi.astype(F32)).astype(BF16)
    moe = jnp.concatenate([_dot(pick_hi, y_lo) + _dot(pick_lo, y_lo),
                           _dot(pick_hi, y_hi) + _dot(pick_lo, y_hi)], axis=1)
    out_ref[...] = _rms(h_ref[...] + moe, gfin_ref[...])


def _combine(run_start, run_len, h2, gates, slots, g_final, y_rows):
    t, d = h2.shape
    tc = COMBINE_TILE
    n_chunk = d // 2 // LANES
    grid_spec = pltpu.PrefetchScalarGridSpec(
        num_scalar_prefetch=2,
        grid=(t // tc,),
        in_specs=[pl.BlockSpec((tc, d), lambda i, *_: (i, 0)),
                  pl.BlockSpec((tc, LANES), lambda i, *_: (i, 0)),
                  pl.BlockSpec((tc, LANES), lambda i, *_: (i, 0)),
                  pl.BlockSpec((1, d), lambda i, *_: (0, 0)),
                  pl.BlockSpec(memory_space=pl.ANY)],
        out_specs=pl.BlockSpec((tc, d), lambda i, *_: (i, 0)),
        scratch_shapes=[pltpu.VMEM((2, tc * TOP_K * n_chunk, LANES), y_rows.dtype),
                        pltpu.SemaphoreType.DMA((2,))],
    )
    return pl.pallas_call(
        _combine_kernel,
        grid_spec=grid_spec,
        out_shape=jax.ShapeDtypeStruct((t, d), F32),
        compiler_params=pltpu.CompilerParams(
            dimension_semantics=("arbitrary",), vmem_limit_bytes=VMEM_LIMIT),
        name="combine",
    )(run_start, run_len, h2, gates, slots, g_final, y_rows)


def _pad_heads(w, n_heads, width, offset=0):
    k = w.shape[0]
    w = w.reshape(k, n_heads, width)
    w = jnp.pad(w, ((0, 0), (0, 0), (offset, LANES - width - offset)))
    return w.reshape(k, n_heads * LANES)


def _rot_cols(w):
    half = w.shape[-1] // 2
    return jnp.concatenate([-w[..., half:], w[..., :half]], axis=-1)


def kernel(x, mem, positions, g_mix, w_in, g_q_a, w_q_b, g_kv_a, w_kv_b, rel_bias, g_out_a, g_out_b,
           w_o, g_xattn, g_mem, w_mq, w_mkv, w_mo, g_moe, w_router, b_router, w_gate_up, b_gate_up,
           w_down, b_down, g_final):
    b, s, d = x.shape
    t = b * s
    assert g_mix.shape[0] == 1, "single-layer block: the final norm is fused into the last stage"

    inv_freq = ROPE_THETA ** (-jnp.arange(0, QK_ROPE_DIM, 2, dtype=F32) / QK_ROPE_DIM)
    ang = positions.astype(F32)[..., None] * inv_freq
    cos_sin = jnp.concatenate([jnp.cos(ang), jnp.sin(ang)], axis=-1)
    half = QK_ROPE_DIM // 2
    src = jnp.arange(QK_ROPE_DIM)[:, None]
    lane = jnp.arange(LANES)[None, :]
    on_rope = (lane >= QK_NOPE_DIM) & (lane < QK_NOPE_DIM + QK_ROPE_DIM)
    place = ((lane - QK_NOPE_DIM) % half == src % half) & on_rope
    expand = jnp.concatenate([place & (src < half), place & (src >= half)], axis=1).astype(F32)
    tbl = jnp.pad(rel_bias.T.astype(F32), ((0, 0), (0, LANES - NUM_BUCKETS)))

    h = x
    for l in range(1):
        c0 = 3 * WIDTH_A + Q_LORA_RANK + KV_LORA_RANK
        w_kpe = w_in[l][:, c0:c0 + QK_ROPE_DIM]
        place = lambda w: jnp.pad(w, ((0, 0), (QK_NOPE_DIM, LANES - QK_NOPE_DIM - QK_ROPE_DIM)))
        w_in_ext = jnp.concatenate([w_in[l][:, :c0], place(w_kpe), place(_rot_cols(w_kpe))], 1).astype(BF16)
        dq = QK_NOPE_DIM + QK_ROPE_DIM
        wq3 = w_q_b[l].reshape(Q_LORA_RANK, N_HEADS_B, dq)
        wq = _pad_heads(wq3.reshape(Q_LORA_RANK, -1), N_HEADS_B, dq).astype(BF16)
        wq_rot3 = jnp.concatenate([jnp.zeros_like(wq3[..., :QK_NOPE_DIM]), _rot_cols(wq3[..., QK_NOPE_DIM:])], -1)
        wqr = _pad_heads(wq_rot3.reshape(Q_LORA_RANK, -1), N_HEADS_B, dq).astype(BF16)
        wkv3 = w_kv_b[l].reshape(KV_LORA_RANK, N_HEADS_B, QK_NOPE_DIM + V_DIM_B)
        wk = _pad_heads(wkv3[..., :QK_NOPE_DIM].reshape(KV_LORA_RANK, -1), N_HEADS_B, QK_NOPE_DIM).astype(BF16)
        wv = _pad_heads(wkv3[..., QK_NOPE_DIM:].reshape(KV_LORA_RANK, -1), N_HEADS_B, V_DIM_B).astype(BF16)

        *views, qm, km, vm = _proj(h, cos_sin, expand, g_mix[l][None], w_in_ext, g_q_a[l][None], wq, wqr,
                                   g_kv_a[l][None], wk, wv, tm=256 * PROJ_CHAINS)
        pats = _dilated_all(views, positions, tbl)
        ob = _mla(qm, km, vm, tq=2048, tk=512)
        kmem, vmem = _memkv(mem, g_mem[l][None], w_mkv[l].astype(BF16))

        g_out_b_pad = _pad_heads(g_out_b[l][None], N_HEADS_B, V_DIM_B)
        w_o_b_pad = _pad_heads(w_o[l][WIDTH_A:].T, N_HEADS_B, V_DIM_B).T.astype(BF16)
        w_mq_s = (w_mq[l] * ((d // N_HEADS_MEM) ** -0.5)).astype(BF16)
        w_router_pad = jnp.pad(w_router[l], ((0, 0), (0, LANES - N_EXPERTS)))
        b_router_pad = jnp.pad(b_router[l][None], ((0, 0), (0, LANES - N_EXPERTS)), constant_values=NEG_INF)
        h2, xn_rows, logits = _post(h, pats[0::2], pats[1::2], ob, g_out_a[l][None],
                               g_out_b_pad, w_o[l][:WIDTH_A].astype(BF16), w_o_b_pad, g_xattn[l][None],
                               w_mq_s, kmem, vmem, w_mo[l].astype(BF16), g_moe[l][None], w_router_pad,
                               b_router_pad, tm=256 * POST_CHAINS)

        gates, slots, cslots, counts, tile_cnt, tile_carry, ctile_cnt, ctile_carry = _route(
            logits.reshape(t, LANES), tr=DISPATCH_CHUNK)
        counts = counts[0, :N_EXPERTS]
        blk = EXPERT_BLOCK
        padded = jnp.where(counts > 0, (counts + DISPATCH_PIECE + blk - 1) // blk * blk, 0)
        ends = jnp.cumsum(padded)
        pad_start = ends - padded
        n_blk = t * TOP_K // blk + N_EXPERTS + -(-N_EXPERTS * DISPATCH_PIECE // blk)
        block_expert = jnp.minimum(
            jnp.sum(ends[None, :] <= (jnp.arange(n_blk) * blk)[:, None], axis=1),
            N_EXPERTS - 1).astype(jnp.int32)
        n_used = (ends[-1] // blk).astype(jnp.int32)[None]
        ends0 = jnp.concatenate([jnp.zeros((1,), jnp.int32), ends.astype(jnp.int32)])
        run_start = (pad_start[None, :] + tile_carry[::8, :N_EXPERTS]).astype(jnp.int32).reshape(-1)
        run_len = tile_cnt[::8, :N_EXPERTS].reshape(-1)
        xs_rows = _dispatch(run_start, run_len, ends0, xn_rows, slots, t, n_blk * blk)
        expert_ids = jnp.arange(N_EXPERTS, dtype=jnp.int32)
        run_end = jnp.sum(jnp.where(block_expert[:, None] == expert_ids, ends[None, :], 0), axis=1)
        next_expert = jnp.where(
            run_end < ends[-1],
            jnp.minimum(jnp.sum(ends[None, :] <= run_end[:, None], axis=1), N_EXPERTS - 1), -1).astype(jnp.int32)
        block_of = block_expert[:, None] == expert_ids
        real_end = jnp.sum(jnp.where(block_of, (pad_start + counts)[None, :], 0), axis=1)
        valid_rows = jnp.clip(real_end - jnp.arange(n_blk) * blk, 0, blk).astype(jnp.int32)
        y_rows = _experts(block_expert, n_used, next_expert, valid_rows, xs_rows, w_gate_up[l], b_gate_up[l], w_down[l], b_down[l])
        gather_start = (pad_start[None, :] + ctile_carry[::8, :N_EXPERTS]).astype(jnp.int32).reshape(-1)
        gather_len = ctile_cnt[::8, :N_EXPERTS].reshape(-1)
        out = _combine(gather_start, gather_len, h2.reshape(t, d), gates, cslots, g_final[None], y_rows)
        h = out.reshape(b, s, d)
    return h
```

```python
import functools
import math

import jax
import jax.numpy as jnp
from jax import lax
from jax.experimental import pallas as pl
from jax.experimental.pallas import tpu as pltpu

F32 = jnp.float32
BF16 = jnp.bfloat16

LANES = 128
EPS = 1e-6
NEG_INF = -1e30
LOG2_E = math.log2(math.e)

N_HEADS_A = 8
HEAD_DIM_A = 64
WIDTH_A = N_HEADS_A * HEAD_DIM_A
DILATED_PATTERNS = ((128, 1), (512, 4), (2048, 16))
N_HEADS_B = 8
QK_NOPE_DIM = 64
QK_ROPE_DIM = 32
V_DIM_B = 64
Q_LORA_RANK = 256
KV_LORA_RANK = 128
ROPE_THETA = 10000.0
NUM_BUCKETS = 32
MAX_DISTANCE = 1024
N_HEADS_MEM = 4
N_EXPERTS = 32
TOP_K = 4
SWIGLU_LIMIT = 7.0
SWIGLU_ALPHA = 1.702

Q_BLOCK = 128
K_WINDOW = 256
HALF_WINDOW = (K_WINDOW - Q_BLOCK) // 2
MAX_Q_BLOCKS_PER_STEP = 4
EXPERT_BLOCK = 512
EXPERT_STEP = 128
DISPATCH_CHUNK = 512
DISPATCH_PIECE = 128
COMBINE_TILE = 128
PROJ_CHAINS = 2
POST_CHAINS = 2
VMEM_LIMIT = 56 * 1024 * 1024


def _rms(x, g):
    return x * lax.rsqrt(jnp.mean(x * x, axis=-1, keepdims=True) + EPS) * g


def _dot(a, b):
    return jnp.dot(a, b, preferred_element_type=F32)


def _dot_nt(a, b):
    return lax.dot_general(a, b, (((1,), (1,)), ((), ())), preferred_element_type=F32)


def _run_skewed(chains):
    live = []
    pending = list(chains)
    while live or pending:
        if pending:
            live.append(pending.pop(0))
        for gen in list(live):
            try:
                next(gen)
            except StopIteration:
                live.remove(gen)


def _load_token_rows(ref, n_tok):
    n_chunk = ref.shape[0] // n_tok
    return jnp.concatenate([ref[pl.ds(c, n_tok, stride=n_chunk), :] for c in range(n_chunk)], axis=1)


def _pack_bf16_pairs(x):
    half = x.shape[1] // 2
    bits = lambda v: lax.bitcast_convert_type(v.astype(BF16).astype(F32), jnp.uint32)
    return bits(x[:, half:]) | (bits(x[:, :half]) >> 16)


def _unpack_bf16_pairs(w):
    lo = lax.bitcast_convert_type(w << 16, F32)
    hi = lax.bitcast_convert_type(w & jnp.uint32(0xFFFF0000), F32)
    return lo, hi


def _store_token_rows(ref, value):
    n_tok = value.shape[0]
    n_chunk = value.shape[1] // LANES
    for c in range(n_chunk):
        ref[pl.ds(c, n_tok, stride=n_chunk), :] = value[:, c * LANES:(c + 1) * LANES]


def _proj_kernel(x_ref, cs_ref, expand_ref, gmix_ref, win_ref, gq_ref, wq_ref, wqr_ref,
                 gkv_ref, wk_ref, wv_ref,
                 q1_ref, k1_ref, v1_ref, q4_ref, k4_ref, v4_ref, q16_ref, k16_ref, v16_ref,
                 qm_ref, km_ref, vm_ref, za_scr):
    tm = x_ref.shape[1]
    c0 = 3 * WIDTH_A
    c1 = c0 + Q_LORA_RANK + KV_LORA_RANK
    per_group = WIDTH_A // LANES
    rows_per_chain = tm // PROJ_CHAINS
    lane = lax.broadcasted_iota(jnp.int32, (1, LANES), 1)
    in_rope = jnp.logical_and(lane >= QK_NOPE_DIM, lane < QK_NOPE_DIM + QK_ROPE_DIM)
    ones_col = jnp.where(lane == V_DIM_B, 1.0, 0.0).astype(F32)
    scale = (QK_NOPE_DIM + QK_ROPE_DIM) ** -0.5 * LOG2_E

    def chain(ci):
        r0 = ci * rows_per_chain
        rows = slice(r0, r0 + rows_per_chain)
        xn = _rms(x_ref[0, rows, :], gmix_ref[...]).astype(BF16)
        yield
        z = _dot(xn, win_ref[...])
        yield
        for c in range(3 * per_group):
            chunk = z[:, c * LANES:(c + 1) * LANES]
            za_scr[c, rows, :] = chunk * (HEAD_DIM_A ** -0.5 * LOG2_E) if c < per_group else chunk
        for dil, refs in ((1, (q1_ref, k1_ref, v1_ref)), (4, (q4_ref, k4_ref, v4_ref)),
                          (16, (q16_ref, k16_ref, v16_ref))):
            n = rows_per_chain // dil
            for r in range(dil):
                for c in range(3 * per_group):
                    picked = za_scr[c, pl.ds(r0 + r, n, stride=dil), :].astype(BF16)
                    col = r * WIDTH_A + (c % per_group) * LANES
                    refs[c // per_group][0, r0 // dil:r0 // dil + n, col:col + LANES] = picked
        zqn = _rms(z[:, c0:c0 + Q_LORA_RANK], gq_ref[...]).astype(BF16)
        zkvn = _rms(z[:, c0 + Q_LORA_RANK:c1], gkv_ref[...]).astype(BF16)
        yield
        spread = jnp.dot(cs_ref[0, rows, :], expand_ref[...], preferred_element_type=F32,
                         precision=lax.Precision.HIGHEST)
        q = _dot(zqn, wq_ref[...])
        qr = _dot(zqn, wqr_ref[...])
        kn = _dot(zkvn, wk_ref[...])
        vv = _dot(zkvn, wv_ref[...])
        yield
        cos = jnp.where(in_rope, spread[:, :LANES], 1.0)
        sin = spread[:, LANES:]
        kpe = z[:, c1:c1 + LANES] * cos + z[:, c1 + LANES:c1 + 2 * LANES] * sin
        for h in range(N_HEADS_B):
            sl = slice(h * LANES, (h + 1) * LANES)
            qm_ref[0, h, rows, :] = ((q[:, sl] * cos + qr[:, sl] * sin) * scale).astype(BF16)
            km_ref[0, h, rows, :] = (kn[:, sl] + kpe).astype(BF16)
            vm_ref[0, h, rows, :] = (vv[:, sl] + ones_col).astype(BF16)

    _run_skewed([chain(ci) for ci in range(PROJ_CHAINS)])


def _proj(x, cos_sin, expand, g_mix, w_in_ext, g_q, wq, wqr, g_kv, wk, wv, tm):
    b, s, d = x.shape
    n_in = w_in_ext.shape[1]
    hw = N_HEADS_B * LANES
    full = lambda shape: pl.BlockSpec(shape, lambda bi, i: (0,) * len(shape))
    row = lambda w: pl.BlockSpec((1, tm, w), lambda bi, i: (bi, i, 0))
    head = pl.BlockSpec((1, N_HEADS_B, tm, LANES), lambda bi, i: (bi, 0, i, 0))
    dils = [dil for _, dil in DILATED_PATTERNS]
    view_specs = [pl.BlockSpec((1, tm // dil, dil * WIDTH_A), lambda bi, i: (bi, i, 0))
                  for dil in dils for _ in range(3)]
    view_shapes = [jax.ShapeDtypeStruct((b, s // dil, dil * WIDTH_A), BF16)
                   for dil in dils for _ in range(3)]
    return pl.pallas_call(
        _proj_kernel,
        grid=(b, s // tm),
        in_specs=[row(d), row(cos_sin.shape[2]), full(expand.shape), full((1, d)), full((d, n_in)),
                  full((1, Q_LORA_RANK)), full((Q_LORA_RANK, hw)), full((Q_LORA_RANK, hw)),
                  full((1, KV_LORA_RANK)), full((KV_LORA_RANK, hw)), full((KV_LORA_RANK, hw))],
        out_specs=view_specs + [head, head, head],
        out_shape=view_shapes + [jax.ShapeDtypeStruct((b, N_HEADS_B, s, LANES), BF16)] * 3,
        scratch_shapes=[pltpu.VMEM((3 * WIDTH_A // LANES, tm, LANES), F32)],
        compiler_params=pltpu.CompilerParams(
            dimension_semantics=("parallel", "parallel"), vmem_limit_bytes=VMEM_LIMIT),
        name="proj",
    )(x, cos_sin, expand, g_mix, w_in_ext, g_q, wq, wqr, g_kv, wk, wv)


_LOG_BUCKET_STARTS = tuple(
    next(n for n in range(8, 4096)
         if int(math.log(n / 8) / math.log(MAX_DISTANCE / 8) * 8) >= t)
    for t in range(1, 8))


def _rel_bucket(rel):
    n = jnp.abs(rel)
    large = jnp.full(rel.shape, NUM_BUCKETS // 4, jnp.int32)
    for start in _LOG_BUCKET_STARTS:
        large = large + jnp.where(n >= start, 1, 0)
    mag = jnp.where(n < NUM_BUCKETS // 4, n, large)
    return mag + jnp.where(rel > 0, NUM_BUCKETS // 2, 0)


def _window_start(j, sub_len):
    return jnp.clip(j * Q_BLOCK - HALF_WINDOW, 0, sub_len - K_WINDOW)


def _bias_kernel(off_ref, qpos_ref, kpos_ref, tbl_ref, out_ref):
    rel = kpos_ref[0] - qpos_ref[0]
    bucket = _rel_bucket(rel)
    delta = (off_ref[pl.program_id(0)] + lax.broadcasted_iota(jnp.int32, (1, K_WINDOW), 1)
             - lax.broadcasted_iota(jnp.int32, (Q_BLOCK, 1), 0))
    valid = jnp.abs(delta) <= HALF_WINDOW
    for h in range(N_HEADS_A):
        tbl = jnp.broadcast_to(tbl_ref[h:h + 1, :], (Q_BLOCK, LANES))
        bias = jnp.concatenate(
            [jnp.take_along_axis(tbl, bucket[:, c * LANES:(c + 1) * LANES], axis=1)
             for c in range(K_WINDOW // LANES)], axis=1)
        out_ref[0, h] = jnp.where(valid, bias * LOG2_E, NEG_INF)


def _bias_tiles(offs, qpos, kpos, tbl):
    n = offs.shape[0]
    grid_spec = pltpu.PrefetchScalarGridSpec(
        num_scalar_prefetch=1,
        grid=(n,),
        in_specs=[pl.BlockSpec((1, Q_BLOCK, 1), lambda t, off: (t, 0, 0)),
                  pl.BlockSpec((1, 1, K_WINDOW), lambda t, off: (t, 0, 0)),
                  pl.BlockSpec((N_HEADS_A, LANES), lambda t, off: (0, 0))],
        out_specs=pl.BlockSpec((1, N_HEADS_A, Q_BLOCK, K_WINDOW), lambda t, off: (t, 0, 0, 0)),
    )
    return pl.pallas_call(
        _bias_kernel,
        grid_spec=grid_spec,
        out_shape=jax.ShapeDtypeStruct((n, N_HEADS_A, Q_BLOCK, K_WINDOW), F32),
        compiler_params=pltpu.CompilerParams(
            dimension_semantics=("arbitrary",), vmem_limit_bytes=VMEM_LIMIT),
        name="bias_tiles",
    )(offs, qpos, kpos, tbl)


def _dilated_kernel(q_ref, k_ref, v_ref, *refs, sub_len, qb):
    bias_refs = refs[:qb]
    o_ref, lse_ref, s_scr, p_scr = refs[qb:]
    first = lax.broadcasted_iota(jnp.int32, (1, LANES), 1) < HEAD_DIM_A
    pair = lambda h: slice((h // 2) * LANES, (h // 2 + 1) * LANES)
    for sub in range(qb):
        j = pl.program_id(2) * qb + sub
        rows = slice(sub * Q_BLOCK, (sub + 1) * Q_BLOCK)
        kstart = pl.multiple_of(_window_start(j, sub_len), HALF_WINDOW)
        q = q_ref[0, rows, :]
        kw = k_ref[0, pl.ds(kstart, K_WINDOW), :]
        vw = v_ref[0, pl.ds(kstart, K_WINDOW), :]
        for h in range(N_HEADS_A):
            own = first if h % 2 == 0 else jnp.logical_not(first)
            qh = jnp.where(own, q[:, pair(h)], jnp.zeros_like(q[:, pair(h)]))
            s_scr[sub, h] = _dot_nt(qh, kw[:, pair(h)]) + bias_refs[sub][0, h]
        stats = []
        for h in range(N_HEADS_A):
            s = s_scr[sub, h]
            m = jnp.max(s, axis=1, keepdims=True)
            e = jnp.exp2(s - m)
            l = jnp.sum(e, axis=1, keepdims=True)
            p_scr[sub, h] = e.astype(BF16)
            stats.append((1.0 / l, m + jnp.log2(l)))
        for h in range(0, N_HEADS_A, 2):
            (r0, lse0), (r1, lse1) = stats[h], stats[h + 1]
            o0 = _dot(p_scr[sub, h], vw[:, pair(h)])
            o1 = _dot(p_scr[sub, h + 1], vw[:, pair(h)])
            o_ref[0, rows, pair(h)] = jnp.where(first, o0 * r0, o1 * r1)
            lse_ref[0, rows, pair(h)] = jnp.where(first, lse0, lse1)


def _dilated(qv, kv, vv, bias, bias_index, dil):
    b, sub_len, _ = qv.shape
    w = WIDTH_A
    nq = sub_len // Q_BLOCK
    qb = min(MAX_Q_BLOCKS_PER_STEP, nq)
    assert sub_len >= K_WINDOW and sub_len % (Q_BLOCK * qb) == 0
    qspec = pl.BlockSpec((1, Q_BLOCK * qb, w), lambda bi, r, j: (bi, j, r))
    kvspec = pl.BlockSpec((1, sub_len, w), lambda bi, r, j: (bi, 0, r))
    bspecs = [pl.BlockSpec((1, N_HEADS_A, Q_BLOCK, K_WINDOW),
                           lambda bi, r, j, sub=sub:
                           (bias_index(bi, r, j * qb + sub, nq), 0, 0, 0))
              for sub in range(qb)]
    return pl.pallas_call(
        functools.partial(_dilated_kernel, sub_len=sub_len, qb=qb),
        grid=(b, dil, nq // qb),
        in_specs=[qspec, kvspec, kvspec] + bspecs,
        out_specs=[qspec, qspec],
        out_shape=[jax.ShapeDtypeStruct((b, sub_len, dil * w), F32)] * 2,
        scratch_shapes=[pltpu.VMEM((qb, N_HEADS_A, Q_BLOCK, K_WINDOW), F32),
                        pltpu.VMEM((qb, N_HEADS_A, Q_BLOCK, K_WINDOW), BF16)],
        compiler_params=pltpu.CompilerParams(
            dimension_semantics=("parallel", "parallel", "arbitrary"),
            vmem_limit_bytes=VMEM_LIMIT),
        name=f"dilated{dil}",
    )(qv, kv, vv, *([bias] * qb))


def _dilated_all(views, positions, tbl):
    b, s = positions.shape
    dils = [dil for _, dil in DILATED_PATTERNS]
    consecutive = jnp.all(positions[:, 1:] - positions[:, :-1] == 1)

    def run(bias, index_fns):
        outs = []
        for g, dil in enumerate(dils):
            qv, kv, vv = views[3 * g:3 * g + 3]
            outs.extend(_dilated(qv, kv, vv, bias, index_fns[g], dil))
        return tuple(outs)

    def shared_tiles():
        offs, qpos, kpos, fns = [], [], [], []
        for g, dil in enumerate(dils):
            for off in (0, -HALF_WINDOW, -2 * HALF_WINDOW):
                offs.append(off)
                qpos.append(dil * jnp.arange(Q_BLOCK, dtype=jnp.int32))
                kpos.append(dil * (off + jnp.arange(K_WINDOW, dtype=jnp.int32)))
            fns.append(lambda bi, r, j, nq, g=g:
                       3 * g + jnp.where(j == 0, 0, jnp.where(j == nq - 1, 2, 1)))
        bias = _bias_tiles(jnp.array(offs, jnp.int32), jnp.stack(qpos)[:, :, None],
                           jnp.stack(kpos)[:, None, :], tbl)
        return run(bias, fns)

    def per_block_tiles():
        offs, qpos, kpos, fns = [], [], [], []
        base = 0
        for g, dil in enumerate(dils):
            sub_len = s // dil
            nq = sub_len // Q_BLOCK
            pos_t = positions.reshape(b, sub_len, dil).transpose(0, 2, 1).reshape(b * dil, sub_len)
            starts = _window_start(jnp.arange(nq), sub_len)
            win = starts[:, None] + jnp.arange(K_WINDOW)[None, :]
            qpos.append(pos_t.reshape(b * dil * nq, Q_BLOCK))
            kpos.append(pos_t[:, win].reshape(b * dil * nq, K_WINDOW))
            offs.append(jnp.tile(starts - jnp.arange(nq) * Q_BLOCK, b * dil))
            fns.append(lambda bi, r, j, nq, base=base, dil=dil: base + (bi * dil + r) * nq + j)
            base += b * dil * nq
        bias = _bias_tiles(jnp.concatenate(offs).astype(jnp.int32),
                           jnp.concatenate(qpos)[:, :, None], jnp.concatenate(kpos)[:, None, :], tbl)
        return run(bias, fns)

    return lax.cond(consecutive, shared_tiles, per_block_tiles)


def _mla_kernel(q_ref, k_ref, v_ref, o_ref, *, tk):
    q = q_ref[0, 0]
    tq = q.shape[0]
    n_kv = k_ref.shape[2] // tk

    def body(i, carry):
        m, acc = carry
        start = pl.multiple_of(i * tk, tk)
        k = k_ref[0, 0, pl.ds(start, tk), :]
        v = v_ref[0, 0, pl.ds(start, tk), :]
        s = _dot_nt(q, k)
        m_new = jnp.maximum(m, jnp.max(s, axis=1, keepdims=True))
        p = jnp.exp2(s - m_new)
        acc = jnp.exp2(m - m_new) * acc + _dot(p.astype(BF16), v)
        return m_new, acc

    m0 = jnp.full((tq, 1), NEG_INF, F32)
    _, acc = lax.fori_loop(0, n_kv, body, (m0, jnp.zeros((tq, LANES), F32)), unroll=True)
    lane = lax.broadcasted_iota(jnp.int32, (1, LANES), 1)
    denom = jnp.sum(jnp.where(lane == V_DIM_B, acc, 0.0), axis=1, keepdims=True)
    o_ref[0, 0] = jnp.where(lane < V_DIM_B, acc / denom, 0.0).astype(BF16)


def _mla(qm, km, vm, tq, tk):
    b, nh, s, _ = qm.shape
    qspec = pl.BlockSpec((1, 1, tq, LANES), lambda bi, h, i: (bi, h, i, 0))
    kvspec = pl.BlockSpec((1, 1, s, LANES), lambda bi, h, i: (bi, h, 0, 0))
    return pl.pallas_call(
        functools.partial(_mla_kernel, tk=tk),
        grid=(b, nh, s // tq),
        in_specs=[qspec, kvspec, kvspec],
        out_specs=qspec,
        out_shape=jax.ShapeDtypeStruct((b, nh, s, LANES), BF16),
        compiler_params=pltpu.CompilerParams(
            dimension_semantics=("parallel", "parallel", "arbitrary"),
            vmem_limit_bytes=VMEM_LIMIT),
        name="mla",
    )(qm, km, vm)


def _memkv_kernel(mem_ref, g_ref, w_ref, k_ref, v_ref):
    d = mem_ref.shape[2]
    kv = _dot(_rms(mem_ref[0], g_ref[...]).astype(BF16), w_ref[...])
    k_ref[0] = kv[:, :d].astype(BF16)
    v_ref[0] = kv[:, d:].astype(BF16)


def _memkv(mem, g_mem, w_mkv):
    b, m, d = mem.shape
    spec = pl.BlockSpec((1, m, d), lambda bi: (bi, 0, 0))
    return pl.pallas_call(
        _memkv_kernel,
        grid=(b,),
        in_specs=[spec, pl.BlockSpec((1, d), lambda bi: (0, 0)),
                  pl.BlockSpec((d, 2 * d), lambda bi: (0, 0))],
        out_specs=[spec, spec],
        out_shape=[jax.ShapeDtypeStruct((b, m, d), BF16)] * 2,
        compiler_params=pltpu.CompilerParams(
            dimension_semantics=("parallel",), vmem_limit_bytes=VMEM_LIMIT),
        name="memkv",
    )(mem, g_mem, w_mkv)


def _post_kernel(x_ref, o1_ref, o2_ref, o3_ref, l1_ref, l2_ref, l3_ref, ob_ref,
                 goa_ref, gob_ref, woa_ref, wob_ref, gx_ref, wmq_ref, km_ref, vm_ref, wmo_ref,
                 gmoe_ref, wr_ref, br_ref,
                 h_ref, xn_ref, logit_ref, *nat_scr):
    tm = x_ref.shape[1]
    n_chunk = WIDTH_A // LANES

    for view_ref, scr in zip((o2_ref, o3_ref, l2_ref, l3_ref), nat_scr):
        dil = view_ref.shape[2] // WIDTH_A
        for r in range(dil):
            for c in range(n_chunk):
                col = r * WIDTH_A + c * LANES
                scr[c, pl.ds(r, tm // dil, stride=dil), :] = view_ref[0, :, col:col + LANES]

    def natural(scr, rows):
        return jnp.concatenate([scr[c, rows, :] for c in range(n_chunk)], axis=1)

    rows_per_chain = tm // POST_CHAINS

    def chain(ci):
        rows = slice(ci * rows_per_chain, (ci + 1) * rows_per_chain)
        o1, o2, o3 = o1_ref[0, rows, :], natural(nat_scr[0], rows), natural(nat_scr[1], rows)
        l1, l2, l3 = l1_ref[0, rows, :], natural(nat_scr[2], rows), natural(nat_scr[3], rows)
        mx = jnp.maximum(jnp.maximum(l1, l2), l3)
        w1, w2, w3 = jnp.exp2(l1 - mx), jnp.exp2(l2 - mx), jnp.exp2(l3 - mx)
        oa = (w1 * o1 + w2 * o2 + w3 * o3) / (w1 + w2 + w3)
        oan = _rms(oa, goa_ref[...]).astype(BF16)
        ob = jnp.concatenate([ob_ref[0, h, rows, :] for h in range(N_HEADS_B)], axis=1).astype(F32)
        ms_b = jnp.sum(ob * ob, axis=1, keepdims=True) * (1.0 / (N_HEADS_B * V_DIM_B))
        obn = (ob * lax.rsqrt(ms_b + EPS) * gob_ref[...]).astype(BF16)
        yield
        h1 = x_ref[0, rows, :] + _dot(oan, woa_ref[...]) + _dot(obn, wob_ref[...])
        yield
        hn = _rms(h1, gx_ref[...]).astype(BF16)
        yield
        q = _dot(hn, wmq_ref[...]).astype(BF16)
        yield
        dh = q.shape[1] // N_HEADS_MEM
        heads = []
        for h in range(N_HEADS_MEM):
            sl = slice(h * dh, (h + 1) * dh)
            s = _dot_nt(q[:, sl], km_ref[0, :, sl])
            yield
            e = jnp.exp(s - jnp.max(s, axis=1, keepdims=True))
            p = e / jnp.sum(e, axis=1, keepdims=True)
            yield
            heads.append(_dot(p.astype(BF16), vm_ref[0, :, sl]))
        o = jnp.concatenate(heads, axis=1).astype(BF16)
        yield
        h2 = h1 + _dot(o, wmo_ref[...])
        h_ref[0, rows, :] = h2
        yield
        xn = _rms(h2, gmoe_ref[...])
        packed = _pack_bf16_pairs(xn)
        rows_per_token = packed.shape[1] // LANES
        _store_token_rows(xn_ref.at[pl.ds(ci * rows_per_chain * rows_per_token,
                                          rows_per_chain * rows_per_token)], packed)
        yield
        logit_ref[0, rows, :] = jnp.dot(xn, wr_ref[...], preferred_element_type=F32,
                                        precision=lax.Precision.HIGHEST) + br_ref[...]

    _run_skewed([chain(ci) for ci in range(POST_CHAINS)])


def _post(x, o_pats, lse_pats, ob, g_out_a, g_out_b_pad, w_o_a, w_o_b_pad, g_xattn, w_mq, kmem,
          vmem, w_mo, g_moe, w_router_pad, b_router_pad, tm):
    b, s, d = x.shape
    m = kmem.shape[1]
    full = lambda shape: pl.BlockSpec(shape, lambda bi, i: (0,) * len(shape))
    row = lambda w: pl.BlockSpec((1, tm, w), lambda bi, i: (bi, i, 0))
    memspec = pl.BlockSpec((1, m, d), lambda bi, i: (bi, 0, 0))
    hw = N_HEADS_B * LANES
    views = [pl.BlockSpec((1, tm // dil, dil * WIDTH_A), lambda bi, i: (bi, i, 0))
             for _, dil in DILATED_PATTERNS]
    return pl.pallas_call(
        _post_kernel,
        grid=(b, s // tm),
        in_specs=[row(d)] + views + views
        + [pl.BlockSpec((1, N_HEADS_B, tm, LANES), lambda bi, i: (bi, 0, i, 0)),
           full((1, WIDTH_A)), full((1, hw)), full((WIDTH_A, d)), full((hw, d)),
           full((1, d)), full((d, d)), memspec, memspec, full((d, d)),
           full((1, d)), full((d, LANES)), full((1, LANES))],
        out_specs=[row(d), pl.BlockSpec((tm * d // 2 // LANES, LANES), lambda bi, i: (bi * (s // tm) + i, 0)),
                   row(LANES)],
        out_shape=[jax.ShapeDtypeStruct((b, s, d), F32),
                   jax.ShapeDtypeStruct((b * s * d // 2 // LANES, LANES), jnp.uint32),
                   jax.ShapeDtypeStruct((b, s, LANES), F32)],
        scratch_shapes=[pltpu.VMEM((WIDTH_A // LANES, tm, LANES), F32)] * 4,
        compiler_params=pltpu.CompilerParams(
            dimension_semantics=("parallel", "parallel"), vmem_limit_bytes=VMEM_LIMIT),
        name="post",
    )(x, *o_pats, *lse_pats, ob, g_out_a, g_out_b_pad, w_o_a, w_o_b_pad, g_xattn, w_mq, kmem,
      vmem, w_mo, g_moe, w_router_pad, b_router_pad)


def _route_kernel(logit_ref, gate_ref, slot_ref, cslot_ref, count_ref, tcnt_ref, tcarry_ref, ccnt_ref,
                  ccarry_ref, carry_ref):
    @pl.when(pl.program_id(0) == 0)
    def _():
        carry_ref[...] = jnp.zeros_like(carry_ref)

    l = logit_ref[...]
    tr = l.shape[0]
    lane_i = lax.broadcasted_iota(jnp.int32, l.shape, 1)
    lane = lane_i.astype(F32)
    vals, idxs = [], []
    for _ in range(TOP_K):
        m = jnp.max(l, axis=1, keepdims=True)
        idx = jnp.min(jnp.where(l == m, lane, float(LANES)), axis=1, keepdims=True)
        vals.append(m)
        idxs.append(idx)
        l = jnp.where(lane == idx, -jnp.inf, l)
    exps = [jnp.exp(v - vals[0]) for v in vals]
    denom = exps[0] + exps[1] + exps[2] + exps[3]
    onehot = jnp.zeros(l.shape, F32)
    for idx in idxs:
        onehot = onehot + jnp.where(lane == idx, 1.0, 0.0)
    r = lax.broadcasted_iota(jnp.int32, (tr, tr), 0)
    c = lax.broadcasted_iota(jnp.int32, (tr, tr), 1)
    tri = jnp.where(c < r, 1.0, 0.0).astype(BF16)
    local = _dot(tri, onehot.astype(BF16))
    tile_cnt = jnp.sum(onehot, axis=0, keepdims=True)
    li = lax.broadcasted_iota(jnp.int32, (LANES, LANES), 0)
    lj = lax.broadcasted_iota(jnp.int32, (LANES, LANES), 1)
    prefix = jnp.dot(jnp.broadcast_to(tile_cnt, (8, LANES)), jnp.where(li < lj, 1.0, 0.0),
                     preferred_element_type=F32, precision=lax.Precision.HIGHEST)[0:1]
    grouped = local + prefix
    n_sub = tr // COMBINE_TILE
    sub_cnt = [jnp.sum(onehot[u * COMBINE_TILE:(u + 1) * COMBINE_TILE], axis=0, keepdims=True)
               for u in range(n_sub)]
    sub_prefix = jnp.dot(jnp.concatenate(sub_cnt + [jnp.zeros((8 - n_sub, LANES), F32)], axis=0),
                         jnp.where(li < lj, 1.0, 0.0), preferred_element_type=F32,
                         precision=lax.Precision.HIGHEST)
    shifts, sub_base = [], jnp.zeros((1, LANES), F32)
    for u in range(n_sub):
        shifts.append(jnp.broadcast_to(sub_prefix[u:u + 1] - sub_base, (COMBINE_TILE, LANES)))
        ccnt_ref[u * 8:(u + 1) * 8, :] = jnp.broadcast_to(sub_cnt[u], (8, LANES)).astype(jnp.int32)
        ccarry_ref[u * 8:(u + 1) * 8, :] = jnp.broadcast_to(carry_ref[...] + sub_base,
                                                            (8, LANES)).astype(jnp.int32)
        sub_base = sub_base + sub_cnt[u]
    sub_grouped = local + jnp.concatenate(shifts, axis=0)
    gate = jnp.zeros(l.shape, F32)
    slot = jnp.zeros(l.shape, jnp.int32)
    cslot = jnp.zeros(l.shape, jnp.int32)
    for k in range(TOP_K):
        mine = lane == idxs[k]
        sk = jnp.sum(jnp.where(mine, grouped, 0.0), axis=1, keepdims=True)
        ck = jnp.sum(jnp.where(mine, sub_grouped, 0.0), axis=1, keepdims=True)
        gate = jnp.where(lane_i == k, exps[k] / denom, gate)
        slot = jnp.where(lane_i == k, sk.astype(jnp.int32), slot)
        cslot = jnp.where(lane_i == k, ck.astype(jnp.int32), cslot)
    gate_ref[...] = gate
    slot_ref[...] = slot
    cslot_ref[...] = cslot
    tcnt_ref[...] = jnp.broadcast_to(tile_cnt, tcnt_ref.shape).astype(jnp.int32)
    tcarry_ref[...] = jnp.broadcast_to(carry_ref[...], tcarry_ref.shape).astype(jnp.int32)
    total = carry_ref[...] + tile_cnt
    carry_ref[...] = total
    count_ref[...] = total.astype(jnp.int32)


def _route(logits, tr):
    t = logits.shape[0]
    spec = pl.BlockSpec((tr, LANES), lambda i: (i, 0))
    return pl.pallas_call(
        _route_kernel,
        grid=(t // tr,),
        in_specs=[spec],
        out_specs=[spec, spec, spec, pl.BlockSpec((1, LANES), lambda i: (0, 0)),
                   pl.BlockSpec((8, LANES), lambda i: (i, 0)), pl.BlockSpec((8, LANES), lambda i: (i, 0)),
                   pl.BlockSpec((8 * tr // COMBINE_TILE, LANES), lambda i: (i, 0)),
                   pl.BlockSpec((8 * tr // COMBINE_TILE, LANES), lambda i: (i, 0))],
        out_shape=[jax.ShapeDtypeStruct((t, LANES), F32)] + [jax.ShapeDtypeStruct((t, LANES), jnp.int32)] * 2
        + [jax.ShapeDtypeStruct((1, LANES), jnp.int32)]
        + [jax.ShapeDtypeStruct((t // tr * 8, LANES), jnp.int32)] * 2
        + [jax.ShapeDtypeStruct((t // COMBINE_TILE * 8, LANES), jnp.int32)] * 2,
        scratch_shapes=[pltpu.VMEM((1, LANES), F32)],
        compiler_params=pltpu.CompilerParams(
            dimension_semantics=("arbitrary",), vmem_limit_bytes=VMEM_LIMIT),
        name="route",
    )(logits)


def _dispatch_kernel(dst_ref, cnt_ref, ends_ref, x_ref, slot_ref, xs_hbm, zbuf, stage, sem, zsem):
    c = pl.program_id(0)
    buf = lax.rem(c, 2)
    n_chunk = x_ref.shape[0] // DISPATCH_CHUNK
    blk_rows = zbuf.shape[0]
    n_staged = DISPATCH_CHUNK * TOP_K

    @pl.when(c == 0)
    def _():
        zbuf[...] = jnp.zeros_like(zbuf)
        for b in range(2):
            stage[b, pl.ds(n_staged * n_chunk, DISPATCH_PIECE * n_chunk), :] = jnp.zeros(
                (DISPATCH_PIECE * n_chunk, LANES), stage.dtype)

        def zero_copy(start):
            return pltpu.make_async_copy(
                zbuf, xs_hbm.at[pl.ds(pl.multiple_of(start * n_chunk, n_chunk), blk_rows)], zsem)

        blk = blk_rows // n_chunk
        used_end = ends_ref[N_EXPERTS]
        starts = [(ends_ref[e + 1] - ends_ref[e] >= back * blk, ends_ref[e + 1] - back * blk)
                  for e in range(N_EXPERTS) for back in (1, 2)]
        starts += [((used_end + (j + 1) * blk) * n_chunk <= xs_hbm.shape[0], used_end + j * blk)
                   for j in range(N_EXPERTS)]
        for cond, start in starts:
            @pl.when(cond)
            def _():
                zero_copy(start).start()
        for cond, start in starts:
            @pl.when(cond)
            def _():
                zero_copy(start).wait()

    x_lo, x_hi = _unpack_bf16_pairs(_load_token_rows(x_ref, DISPATCH_CHUNK))
    x_lo, x_hi = x_lo.astype(BF16), x_hi.astype(BF16)
    slot_t = jnp.transpose(slot_ref[...].astype(F32))
    bits = lambda v: lax.bitcast_convert_type(v, jnp.uint32)
    for part in range(TOP_K):
        j = (lax.broadcasted_iota(jnp.int32, (DISPATCH_CHUNK, 1), 0) + part * DISPATCH_CHUNK).astype(F32)
        pick = jnp.zeros((DISPATCH_CHUNK, DISPATCH_CHUNK), F32)
        for k in range(TOP_K):
            pick = jnp.where(slot_t[k:k + 1, :] == j, 1.0, pick)
        pick = pick.astype(BF16)
        packed = bits(_dot(pick, x_hi)) | (bits(_dot(pick, x_lo)) >> 16)
        _store_token_rows(
            stage.at[buf, pl.ds(part * DISPATCH_CHUNK * n_chunk, DISPATCH_CHUNK * n_chunk)], packed)

    def piece(src_tok, dst_tok):
        rows = DISPATCH_PIECE * n_chunk
        return pltpu.make_async_copy(
            stage.at[buf, pl.ds(pl.multiple_of(src_tok * n_chunk, n_chunk), rows)],
            xs_hbm.at[pl.ds(pl.multiple_of(dst_tok * n_chunk, n_chunk), rows)], sem)

    def drain(step):
        n_pieces = 0
        for e in range(N_EXPERTS):
            n_pieces = n_pieces + (cnt_ref[step * N_EXPERTS + e] + DISPATCH_PIECE - 1) // DISPATCH_PIECE

        def wait_one(i, carry):
            piece(0, 0).wait()
            return carry
        lax.fori_loop(0, n_pieces, wait_one, 0)

    @pl.when(c > 0)
    def _():
        drain(c - 1)

    staged = 0
    for e in range(N_EXPERTS):
        cnt = cnt_ref[c * N_EXPERTS + e]
        dst = dst_ref[c * N_EXPERTS + e]
        for p in range(DISPATCH_CHUNK // DISPATCH_PIECE):
            @pl.when(p * DISPATCH_PIECE < cnt)
            def _():
                piece(staged + p * DISPATCH_PIECE, dst + p * DISPATCH_PIECE).start()
        staged = staged + cnt

    @pl.when(c == pl.num_programs(0) - 1)
    def _():
        drain(c)


def _dispatch(dst, cnt, ends0, xn_rows, slots, n_tok, n_slots):
    n_chunk = xn_rows.shape[0] // n_tok
    staged_rows = (DISPATCH_CHUNK * TOP_K + DISPATCH_PIECE) * n_chunk
    grid_spec = pltpu.PrefetchScalarGridSpec(
        num_scalar_prefetch=3,
        grid=(n_tok // DISPATCH_CHUNK,),
        in_specs=[pl.BlockSpec((DISPATCH_CHUNK * n_chunk, LANES), lambda c, *_: (c, 0)),
                  pl.BlockSpec((DISPATCH_CHUNK, LANES), lambda c, *_: (c, 0))],
        out_specs=pl.BlockSpec(memory_space=pl.ANY),
        scratch_shapes=[pltpu.VMEM((EXPERT_BLOCK * n_chunk, LANES), xn_rows.dtype),
                        pltpu.VMEM((2, staged_rows, LANES), xn_rows.dtype),
                        pltpu.SemaphoreType.DMA(()), pltpu.SemaphoreType.DMA(())],
    )
    return pl.pallas_call(
        _dispatch_kernel,
        grid_spec=grid_spec,
        out_shape=jax.ShapeDtypeStruct((n_slots * n_chunk, LANES), xn_rows.dtype),
        compiler_params=pltpu.CompilerParams(
            dimension_semantics=("arbitrary",), vmem_limit_bytes=VMEM_LIMIT),
        name="dispatch",
    )(dst, cnt, ends0, xn_rows, slots)


def _expert_kernel(be_ref, nused_ref, next_ref, valid_ref, xs_ref, wgu_hbm, bgu_ref, wd_hbm, bd_ref, y_ref,
                   wgu_f32, wd_f32, wgu_bf, wd_bf, sems):
    i = pl.program_id(0)
    n_chunk = xs_ref.shape[0] // EXPERT_BLOCK

    def mlp(n_rows):
        part = pl.ds(0, n_rows * n_chunk)
        x_lo, x_hi = _unpack_bf16_pairs(_load_token_rows(xs_ref.at[part], n_rows))
        x = jnp.concatenate([x_lo.astype(BF16), x_hi.astype(BF16)], axis=1)
        gu = _dot(x, wgu_bf[...]) + bgu_ref[0]
        de = gu.shape[1] // 2
        gate = jnp.minimum(gu[:, :de], SWIGLU_LIMIT)
        up = jnp.clip(gu[:, de:], -SWIGLU_LIMIT, SWIGLU_LIMIT)
        hmid = (up + 1.0) * (gate * jax.nn.sigmoid(SWIGLU_ALPHA * gate))
        y = _dot(hmid.astype(BF16), wd_bf[...]) + bd_ref[0]
        _store_token_rows(y_ref.at[part], _pack_bf16_pairs(y))

    def fetch(e):
        return (pltpu.make_async_copy(wgu_hbm.at[e], wgu_f32, sems.at[0]),
                pltpu.make_async_copy(wd_hbm.at[e], wd_f32, sems.at[1]))

    @pl.when(i == 0)
    def _():
        for copy in fetch(be_ref[0]):
            copy.start()

    @pl.when(i < nused_ref[0])
    def _():
        @pl.when(jnp.logical_or(i == 0, be_ref[i] != be_ref[jnp.maximum(i - 1, 0)]))
        def _():
            for copy in fetch(be_ref[i]):
                copy.wait()
            wgu_bf[...] = wgu_f32[...].astype(BF16)
            wd_bf[...] = wd_f32[...].astype(BF16)

            @pl.when(next_ref[i] >= 0)
            def _():
                for copy in fetch(next_ref[i]):
                    copy.start()

        valid = valid_ref[i]
        for n_rows in range(EXPERT_STEP, EXPERT_BLOCK + 1, EXPERT_STEP):
            @pl.when(jnp.logical_and(valid > n_rows - EXPERT_STEP, valid <= n_rows))
            def _():
                mlp(n_rows)
                if n_rows < EXPERT_BLOCK:
                    rest = (EXPERT_BLOCK - n_rows) * n_chunk
                    y_ref[pl.ds(n_rows * n_chunk, rest), :] = jnp.zeros((rest, LANES), y_ref.dtype)

    @pl.when(jnp.logical_or(i >= nused_ref[0], valid_ref[i] == 0))
    def _():
        y_ref[...] = jnp.zeros_like(y_ref)


def _experts(block_expert, n_used, next_expert, valid_rows, xs_rows, w_gate_up, b_gate_up, w_down, b_down):
    d, de2 = w_gate_up.shape[1:]
    n_chunk = d // 2 // LANES
    blk = EXPERT_BLOCK
    n_blk = xs_rows.shape[0] // (blk * n_chunk)
    grid_spec = pltpu.PrefetchScalarGridSpec(
        num_scalar_prefetch=4,
        grid=(n_blk,),
        in_specs=[pl.BlockSpec((blk * n_chunk, LANES),
                               lambda i, be, nu, *_: (jnp.maximum(jnp.minimum(i, nu[0] - 1), 0), 0)),
                  pl.BlockSpec(memory_space=pl.ANY),
                  pl.BlockSpec((1, 1, de2), lambda i, be, *_: (be[i], 0, 0)),
                  pl.BlockSpec(memory_space=pl.ANY),
                  pl.BlockSpec((1, 1, d), lambda i, be, *_: (be[i], 0, 0))],
        out_specs=pl.BlockSpec((blk * n_chunk, LANES), lambda i, *_: (i, 0)),
        scratch_shapes=[pltpu.VMEM((d, de2), F32), pltpu.VMEM((de2 // 2, d), F32),
                        pltpu.VMEM((d, de2), BF16), pltpu.VMEM((de2 // 2, d), BF16),
                        pltpu.SemaphoreType.DMA((2,))],
    )
    return pl.pallas_call(
        _expert_kernel,
        grid_spec=grid_spec,
        out_shape=jax.ShapeDtypeStruct(xs_rows.shape, xs_rows.dtype),
        compiler_params=pltpu.CompilerParams(
            dimension_semantics=("arbitrary",), vmem_limit_bytes=VMEM_LIMIT),
        name="experts",
    )(block_expert, n_used, next_expert, valid_rows, xs_rows, w_gate_up, b_gate_up.reshape(N_EXPERTS, 1, de2),
      w_down, b_down.reshape(N_EXPERTS, 1, d))


def _combine_kernel(start_ref, len_ref, h_ref, gate_ref, slot_ref, gfin_ref, y_hbm, out_ref, stage, sems):
    i = pl.program_id(0)
    n = pl.num_programs(0)
    buf = lax.rem(i, 2)
    tc = h_ref.shape[0]
    n_staged = tc * TOP_K
    n_chunk = stage.shape[1] // n_staged

    def issue(tile, b):
        staged = 0
        for e in range(N_EXPERTS):
            run_len = len_ref[tile * N_EXPERTS + e]
            run_start = start_ref[tile * N_EXPERTS + e]
            for bit in reversed(range(COMBINE_TILE.bit_length())):
                size = 1 << bit
                done = (run_len >> (bit + 1)) << (bit + 1)

                @pl.when((run_len >> bit) & 1 == 1)
                def _():
                    pltpu.make_async_copy(
                        y_hbm.at[pl.ds(pl.multiple_of((run_start + done) * n_chunk, n_chunk), size * n_chunk)],
                        stage.at[b, pl.ds(pl.multiple_of((staged + done) * n_chunk, n_chunk), size * n_chunk)],
                        sems.at[b]).start()
            staged = staged + run_len

    @pl.when(i == 0)
    def _():
        issue(0, 0)

    @pl.when(i + 1 < n)
    def _():
        issue(i + 1, 1 - buf)

    pltpu.make_async_copy(y_hbm.at[pl.ds(0, n_staged * n_chunk)], stage.at[buf], sems.at[buf]).wait()

    y_lo, y_hi = _unpack_bf16_pairs(_load_token_rows(stage.at[buf], n_staged))
    y_lo, y_hi = y_lo.astype(BF16), y_hi.astype(BF16)
    gate = gate_ref[...]
    slot = slot_ref[...].astype(F32)
    j = lax.broadcasted_iota(jnp.int32, (1, n_staged), 1).astype(F32)
    pick = jnp.zeros((tc, n_staged), F32)
    for k in range(TOP_K):
        pick = jnp.where(slot[:, k:k + 1] == j, gate[:, k:k + 1], pick)
    pick_hi = pick.astype(BF16)
    pick_lo = (pick - pick_hi.astype(F32)).astype(BF16)
    moe = jnp.concatenate([_dot(pick_hi, y_lo) + _dot(pick_lo, y_lo),
                           _dot(pick_hi, y_hi) + _dot(pick_lo, y_hi)], axis=1)
    out_ref[...] = _rms(h_ref[...] + moe, gfin_ref[...])


def _combine(run_start, run_len, h2, gates, slots, g_final, y_rows):
    t, d = h2.shape
    tc = COMBINE_TILE
    n_chunk = d // 2 // LANES
    grid_spec = pltpu.PrefetchScalarGridSpec(
        num_scalar_prefetch=2,
        grid=(t // tc,),
        in_specs=[pl.BlockSpec((tc, d), lambda i, *_: (i, 0)),
                  pl.BlockSpec((tc, LANES), lambda i, *_: (i, 0)),
                  pl.BlockSpec((tc, LANES), lambda i, *_: (i, 0)),
                  pl.BlockSpec((1, d), lambda i, *_: (0, 0)),
                  pl.BlockSpec(memory_space=pl.ANY)],
        out_specs=pl.BlockSpec((tc, d), lambda i, *_: (i, 0)),
        scratch_shapes=[pltpu.VMEM((2, tc * TOP_K * n_chunk, LANES), y_rows.dtype),
                        pltpu.SemaphoreType.DMA((2,))],
    )
    return pl.pallas_call(
        _combine_kernel,
        grid_spec=grid_spec,
        out_shape=jax.ShapeDtypeStruct((t, d), F32),
        compiler_params=pltpu.CompilerParams(
            dimension_semantics=("arbitrary",), vmem_limit_bytes=VMEM_LIMIT),
        name="combine",
    )(run_start, run_len, h2, gates, slots, g_final, y_rows)


def _pad_heads(w, n_heads, width, offset=0):
    k = w.shape[0]
    w = w.reshape(k, n_heads, width)
    w = jnp.pad(w, ((0, 0), (0, 0), (offset, LANES - width - offset)))
    return w.reshape(k, n_heads * LANES)


def _rot_cols(w):
    half = w.shape[-1] // 2
    return jnp.concatenate([-w[..., half:], w[..., :half]], axis=-1)


def kernel(x, mem, positions, g_mix, w_in, g_q_a, w_q_b, g_kv_a, w_kv_b, rel_bias, g_out_a, g_out_b,
           w_o, g_xattn, g_mem, w_mq, w_mkv, w_mo, g_moe, w_router, b_router, w_gate_up, b_gate_up,
           w_down, b_down, g_final):
    b, s, d = x.shape
    t = b * s
    assert g_mix.shape[0] == 1, "single-layer block: the final norm is fused into the last stage"

    inv_freq = ROPE_THETA ** (-jnp.arange(0, QK_ROPE_DIM, 2, dtype=F32) / QK_ROPE_DIM)
    ang = positions.astype(F32)[..., None] * inv_freq
    cos_sin = jnp.concatenate([jnp.cos(ang), jnp.sin(ang)], axis=-1)
    half = QK_ROPE_DIM // 2
    src = jnp.arange(QK_ROPE_DIM)[:, None]
    lane = jnp.arange(LANES)[None, :]
    on_rope = (lane >= QK_NOPE_DIM) & (lane < QK_NOPE_DIM + QK_ROPE_DIM)
    place = ((lane - QK_NOPE_DIM) % half == src % half) & on_rope
    expand = jnp.concatenate([place & (src < half), place & (src >= half)], axis=1).astype(F32)
    tbl = jnp.pad(rel_bias.T.astype(F32), ((0, 0), (0, LANES - NUM_BUCKETS)))

    h = x
    for l in range(1):
        c0 = 3 * WIDTH_A + Q_LORA_RANK + KV_LORA_RANK
        w_kpe = w_in[l][:, c0:c0 + QK_ROPE_DIM]
        place = lambda w: jnp.pad(w, ((0, 0), (QK_NOPE_DIM, LANES - QK_NOPE_DIM - QK_ROPE_DIM)))
        w_in_ext = jnp.concatenate([w_in[l][:, :c0], place(w_kpe), place(_rot_cols(w_kpe))], 1).astype(BF16)
        dq = QK_NOPE_DIM + QK_ROPE_DIM
        wq3 = w_q_b[l].reshape(Q_LORA_RANK, N_HEADS_B, dq)
        wq = _pad_heads(wq3.reshape(Q_LORA_RANK, -1), N_HEADS_B, dq).astype(BF16)
        wq_rot3 = jnp.concatenate([jnp.zeros_like(wq3[..., :QK_NOPE_DIM]), _rot_cols(wq3[..., QK_NOPE_DIM:])], -1)
        wqr = _pad_heads(wq_rot3.reshape(Q_LORA_RANK, -1), N_HEADS_B, dq).astype(BF16)
        wkv3 = w_kv_b[l].reshape(KV_LORA_RANK, N_HEADS_B, QK_NOPE_DIM + V_DIM_B)
        wk = _pad_heads(wkv3[..., :QK_NOPE_DIM].reshape(KV_LORA_RANK, -1), N_HEADS_B, QK_NOPE_DIM).astype(BF16)
        wv = _pad_heads(wkv3[..., QK_NOPE_DIM:].reshape(KV_LORA_RANK, -1), N_HEADS_B, V_DIM_B).astype(BF16)

        *views, qm, km, vm = _proj(h, cos_sin, expand, g_mix[l][None], w_in_ext, g_q_a[l][None], wq, wqr,
                                   g_kv_a[l][None], wk, wv, tm=256 * PROJ_CHAINS)
        pats = _dilated_all(views, positions, tbl)
        ob = _mla(qm, km, vm, tq=2048, tk=512)
        kmem, vmem = _memkv(mem, g_mem[l][None], w_mkv[l].astype(BF16))

        g_out_b_pad = _pad_heads(g_out_b[l][None], N_HEADS_B, V_DIM_B)
        w_o_b_pad = _pad_heads(w_o[l][WIDTH_A:].T, N_HEADS_B, V_DIM_B).T.astype(BF16)
        w_mq_s = (w_mq[l] * ((d // N_HEADS_MEM) ** -0.5)).astype(BF16)
        w_router_pad = jnp.pad(w_router[l], ((0, 0), (0, LANES - N_EXPERTS)))
        b_router_pad = jnp.pad(b_router[l][None], ((0, 0), (0, LANES - N_EXPERTS)), constant_values=NEG_INF)
        h2, xn_rows, logits = _post(h, pats[0::2], pats[1::2], ob, g_out_a[l][None],
                               g_out_b_pad, w_o[l][:WIDTH_A].astype(BF16), w_o_b_pad, g_xattn[l][None],
                               w_mq_s, kmem, vmem, w_mo[l].astype(BF16), g_moe[l][None], w_router_pad,
                               b_router_pad, tm=256 * POST_CHAINS)

        gates, slots, cslots, counts, tile_cnt, tile_carry, ctile_cnt, ctile_carry = _route(
            logits.reshape(t, LANES), tr=DISPATCH_CHUNK)
        counts = counts[0, :N_EXPERTS]
        blk = EXPERT_BLOCK
        padded = jnp.where(counts > 0, (counts + DISPATCH_PIECE + blk - 1) // blk * blk, 0)
        ends = jnp.cumsum(padded)
        pad_start = ends - padded
        n_blk = t * TOP_K // blk + N_EXPERTS + -(-N_EXPERTS * DISPATCH_PIECE // blk)
        block_expert = jnp.minimum(
            jnp.sum(ends[None, :] <= (jnp.arange(n_blk) * blk)[:, None], axis=1),
            N_EXPERTS - 1).astype(jnp.int32)
        n_used = (ends[-1] // blk).astype(jnp.int32)[None]
        ends0 = jnp.concatenate([jnp.zeros((1,), jnp.int32), ends.astype(jnp.int32)])
        run_start = (pad_start[None, :] + tile_carry[::8, :N_EXPERTS]).astype(jnp.int32).reshape(-1)
        run_len = tile_cnt[::8, :N_EXPERTS].reshape(-1)
        xs_rows = _dispatch(run_start, run_len, ends0, xn_rows, slots, t, n_blk * blk)
        expert_ids = jnp.arange(N_EXPERTS, dtype=jnp.int32)
        run_end = jnp.sum(jnp.where(block_expert[:, None] == expert_ids, ends[None, :], 0), axis=1)
        next_expert = jnp.where(
            run_end < ends[-1],
            jnp.minimum(jnp.sum(ends[None, :] <= run_end[:, None], axis=1), N_EXPERTS - 1), -1).astype(jnp.int32)
        block_of = block_expert[:, None] == expert_ids
        real_end = jnp.sum(jnp.where(block_of, (pad_start + counts)[None, :], 0), axis=1)
        valid_rows = jnp.clip(real_end - jnp.arange(n_blk) * blk, 0, blk).astype(jnp.int32)
        y_rows = _experts(block_expert, n_used, next_expert, valid_rows, xs_rows, w_gate_up[l], b_gate_up[l], w_down[l], b_down[l])
        gather_start = (pad_start[None, :] + ctile_carry[::8, :N_EXPERTS]).astype(jnp.int32).reshape(-1)
        gather_len = ctile_cnt[::8, :N_EXPERTS].reshape(-1)
        out = _combine(gather_start, gather_len, h2.reshape(t, d), gates, cslots, g_final[None], y_rows)
        h = out.reshape(b, s, d)
    return h
```

```python
import functools
import math

import jax
import jax.numpy as jnp
from jax import lax
from jax.experimental import pallas as pl
from jax.experimental.pallas import tpu as pltpu

F32 = jnp.float32
BF16 = jnp.bfloat16

LANES = 128
EPS = 1e-6
NEG_INF = -1e30
LOG2_E = math.log2(math.e)

N_HEADS_A = 8
HEAD_DIM_A = 64
WIDTH_A = N_HEADS_A * HEAD_DIM_A
DILATED_PATTERNS = ((128, 1), (512, 4), (2048, 16))
N_HEADS_B = 8
QK_NOPE_DIM = 64
QK_ROPE_DIM = 32
V_DIM_B = 64
Q_LORA_RANK = 256
KV_LORA_RANK = 128
ROPE_THETA = 10000.0
NUM_BUCKETS = 32
MAX_DISTANCE = 1024
N_HEADS_MEM = 4
N_EXPERTS = 32
TOP_K = 4
SWIGLU_LIMIT = 7.0
SWIGLU_ALPHA = 1.702

Q_BLOCK = 128
K_WINDOW = 256
HALF_WINDOW = (K_WINDOW - Q_BLOCK) // 2
MAX_Q_BLOCKS_PER_STEP = 4
EXPERT_BLOCK = 512
EXPERT_STEP = 128
DISPATCH_CHUNK = 512
DISPATCH_PIECE = 64
COMBINE_TILE = 256
PROJ_CHAINS = 2
POST_CHAINS = 2
VMEM_LIMIT = 56 * 1024 * 1024


def _rms(x, g):
    return x * lax.rsqrt(jnp.mean(x * x, axis=-1, keepdims=True) + EPS) * g


def _dot(a, b):
    return jnp.dot(a, b, preferred_element_type=F32)


def _dot_nt(a, b):
    return lax.dot_general(a, b, (((1,), (1,)), ((), ())), preferred_element_type=F32)


def _run_skewed(chains):
    live = []
    pending = list(chains)
    while live or pending:
        if pending:
            live.append(pending.pop(0))
        for gen in list(live):
            try:
                next(gen)
            except StopIteration:
                live.remove(gen)


def _load_token_rows(ref, n_tok):
    n_chunk = ref.shape[0] // n_tok
    return jnp.concatenate([ref[pl.ds(c, n_tok, stride=n_chunk), :] for c in range(n_chunk)], axis=1)


def _pack_bf16_pairs(x):
    half = x.shape[1] // 2
    bits = lambda v: lax.bitcast_convert_type(v.astype(BF16).astype(F32), jnp.uint32)
    return bits(x[:, half:]) | (bits(x[:, :half]) >> 16)


def _unpack_bf16_pairs(w):
    lo = lax.bitcast_convert_type(w << 16, F32)
    hi = lax.bitcast_convert_type(w & jnp.uint32(0xFFFF0000), F32)
    return lo, hi


def _store_token_rows(ref, value):
    n_tok = value.shape[0]
    n_chunk = value.shape[1] // LANES
    for c in range(n_chunk):
        ref[pl.ds(c, n_tok, stride=n_chunk), :] = value[:, c * LANES:(c + 1) * LANES]


def _proj_kernel(x_ref, cs_ref, expand_ref, gmix_ref, win_ref, gq_ref, wq_ref, wqr_ref,
                 gkv_ref, wk_ref, wv_ref,
                 q1_ref, k1_ref, v1_ref, q4_ref, k4_ref, v4_ref, q16_ref, k16_ref, v16_ref,
                 qm_ref, km_ref, vm_ref, za_scr):
    tm = x_ref.shape[1]
    c0 = 3 * WIDTH_A
    c1 = c0 + Q_LORA_RANK + KV_LORA_RANK
    per_group = WIDTH_A // LANES
    rows_per_chain = tm // PROJ_CHAINS
    lane = lax.broadcasted_iota(jnp.int32, (1, LANES), 1)
    in_rope = jnp.logical_and(lane >= QK_NOPE_DIM, lane < QK_NOPE_DIM + QK_ROPE_DIM)
    ones_col = jnp.where(lane == V_DIM_B, 1.0, 0.0).astype(F32)
    scale = (QK_NOPE_DIM + QK_ROPE_DIM) ** -0.5 * LOG2_E

    def chain(ci):
        r0 = ci * rows_per_chain
        rows = slice(r0, r0 + rows_per_chain)
        xn = _rms(x_ref[0, rows, :], gmix_ref[...]).astype(BF16)
        yield
        z = _dot(xn, win_ref[...])
        yield
        for c in range(3 * per_group):
            chunk = z[:, c * LANES:(c + 1) * LANES]
            za_scr[c, rows, :] = chunk * (HEAD_DIM_A ** -0.5 * LOG2_E) if c < per_group else chunk
        for dil, refs in ((1, (q1_ref, k1_ref, v1_ref)), (4, (q4_ref, k4_ref, v4_ref)),
                          (16, (q16_ref, k16_ref, v16_ref))):
            n = rows_per_chain // dil
            for r in range(dil):
                for c in range(3 * per_group):
                    picked = za_scr[c, pl.ds(r0 + r, n, stride=dil), :].astype(BF16)
                    col = r * WIDTH_A + (c % per_group) * LANES
                    refs[c // per_group][0, r0 // dil:r0 // dil + n, col:col + LANES] = picked
        zqn = _rms(z[:, c0:c0 + Q_LORA_RANK], gq_ref[...]).astype(BF16)
        zkvn = _rms(z[:, c0 + Q_LORA_RANK:c1], gkv_ref[...]).astype(BF16)
        yield
        spread = jnp.dot(cs_ref[0, rows, :], expand_ref[...], preferred_element_type=F32,
                         precision=lax.Precision.HIGHEST)
        q = _dot(zqn, wq_ref[...])
        qr = _dot(zqn, wqr_ref[...])
        kn = _dot(zkvn, wk_ref[...])
        vv = _dot(zkvn, wv_ref[...])
        yield
        cos = jnp.where(in_rope, spread[:, :LANES], 1.0)
        sin = spread[:, LANES:]
        kpe = z[:, c1:c1 + LANES] * cos + z[:, c1 + LANES:c1 + 2 * LANES] * sin
        for h in range(N_HEADS_B):
            sl = slice(h * LANES, (h + 1) * LANES)
            qm_ref[0, h, rows, :] = ((q[:, sl] * cos + qr[:, sl] * sin) * scale).astype(BF16)
            km_ref[0, h, rows, :] = (kn[:, sl] + kpe).astype(BF16)
            vm_ref[0, h, rows, :] = (vv[:, sl] + ones_col).astype(BF16)

    _run_skewed([chain(ci) for ci in range(PROJ_CHAINS)])


def _proj(x, cos_sin, expand, g_mix, w_in_ext, g_q, wq, wqr, g_kv, wk, wv, tm):
    b, s, d = x.shape
    n_in = w_in_ext.shape[1]
    hw = N_HEADS_B * LANES
    full = lambda shape: pl.BlockSpec(shape, lambda bi, i: (0,) * len(shape))
    row = lambda w: pl.BlockSpec((1, tm, w), lambda bi, i: (bi, i, 0))
    head = pl.BlockSpec((1, N_HEADS_B, tm, LANES), lambda bi, i: (bi, 0, i, 0))
    dils = [dil for _, dil in DILATED_PATTERNS]
    view_specs = [pl.BlockSpec((1, tm // dil, dil * WIDTH_A), lambda bi, i: (bi, i, 0))
                  for dil in dils for _ in range(3)]
    view_shapes = [jax.ShapeDtypeStruct((b, s // dil, dil * WIDTH_A), BF16)
                   for dil in dils for _ in range(3)]
    return pl.pallas_call(
        _proj_kernel,
        grid=(b, s // tm),
        in_specs=[row(d), row(cos_sin.shape[2]), full(expand.shape), full((1, d)), full((d, n_in)),
                  full((1, Q_LORA_RANK)), full((Q_LORA_RANK, hw)), full((Q_LORA_RANK, hw)),
                  full((1, KV_LORA_RANK)), full((KV_LORA_RANK, hw)), full((KV_LORA_RANK, hw))],
        out_specs=view_specs + [head, head, head],
        out_shape=view_shapes + [jax.ShapeDtypeStruct((b, N_HEADS_B, s, LANES), BF16)] * 3,
        scratch_shapes=[pltpu.VMEM((3 * WIDTH_A // LANES, tm, LANES), F32)],
        compiler_params=pltpu.CompilerParams(
            dimension_semantics=("parallel", "parallel"), vmem_limit_bytes=VMEM_LIMIT),
        name="proj",
    )(x, cos_sin, expand, g_mix, w_in_ext, g_q, wq, wqr, g_kv, wk, wv)


_LOG_BUCKET_STARTS = tuple(
    next(n for n in range(8, 4096)
         if int(math.log(n / 8) / math.log(MAX_DISTANCE / 8) * 8) >= t)
    for t in range(1, 8))


def _rel_bucket(rel):
    n = jnp.abs(rel)
    large = jnp.full(rel.shape, NUM_BUCKETS // 4, jnp.int32)
    for start in _LOG_BUCKET_STARTS:
        large = large + jnp.where(n >= start, 1, 0)
    mag = jnp.where(n < NUM_BUCKETS // 4, n, large)
    return mag + jnp.where(rel > 0, NUM_BUCKETS // 2, 0)


def _window_start(j, sub_len):
    return jnp.clip(j * Q_BLOCK - HALF_WINDOW, 0, sub_len - K_WINDOW)


def _bias_kernel(off_ref, qpos_ref, kpos_ref, tbl_ref, out_ref):
    rel = kpos_ref[0] - qpos_ref[0]
    bucket = _rel_bucket(rel)
    delta = (off_ref[pl.program_id(0)] + lax.broadcasted_iota(jnp.int32, (1, K_WINDOW), 1)
             - lax.broadcasted_iota(jnp.int32, (Q_BLOCK, 1), 0))
    valid = jnp.abs(delta) <= HALF_WINDOW
    for h in range(N_HEADS_A):
        tbl = jnp.broadcast_to(tbl_ref[h:h + 1, :], (Q_BLOCK, LANES))
        bias = jnp.concatenate(
            [jnp.take_along_axis(tbl, bucket[:, c * LANES:(c + 1) * LANES], axis=1)
             for c in range(K_WINDOW // LANES)], axis=1)
        out_ref[0, h] = jnp.where(valid, bias * LOG2_E, NEG_INF)


def _bias_tiles(offs, qpos, kpos, tbl):
    n = offs.shape[0]
    grid_spec = pltpu.PrefetchScalarGridSpec(
        num_scalar_prefetch=1,
        grid=(n,),
        in_specs=[pl.BlockSpec((1, Q_BLOCK, 1), lambda t, off: (t, 0, 0)),
                  pl.BlockSpec((1, 1, K_WINDOW), lambda t, off: (t, 0, 0)),
                  pl.BlockSpec((N_HEADS_A, LANES), lambda t, off: (0, 0))],
        out_specs=pl.BlockSpec((1, N_HEADS_A, Q_BLOCK, K_WINDOW), lambda t, off: (t, 0, 0, 0)),
    )
    return pl.pallas_call(
        _bias_kernel,
        grid_spec=grid_spec,
        out_shape=jax.ShapeDtypeStruct((n, N_HEADS_A, Q_BLOCK, K_WINDOW), F32),
        compiler_params=pltpu.CompilerParams(
            dimension_semantics=("arbitrary",), vmem_limit_bytes=VMEM_LIMIT),
        name="bias_tiles",
    )(offs, qpos, kpos, tbl)


def _dilated_kernel(q_ref, k_ref, v_ref, *refs, sub_len, qb):
    bias_refs = refs[:qb]
    o_ref, lse_ref, s_scr, p_scr = refs[qb:]
    first = lax.broadcasted_iota(jnp.int32, (1, LANES), 1) < HEAD_DIM_A
    pair = lambda h: slice((h // 2) * LANES, (h // 2 + 1) * LANES)
    for sub in range(qb):
        j = pl.program_id(2) * qb + sub
        rows = slice(sub * Q_BLOCK, (sub + 1) * Q_BLOCK)
        kstart = pl.multiple_of(_window_start(j, sub_len), HALF_WINDOW)
        q = q_ref[0, rows, :]
        kw = k_ref[0, pl.ds(kstart, K_WINDOW), :]
        vw = v_ref[0, pl.ds(kstart, K_WINDOW), :]
        for h in range(N_HEADS_A):
            own = first if h % 2 == 0 else jnp.logical_not(first)
            qh = jnp.where(own, q[:, pair(h)], jnp.zeros_like(q[:, pair(h)]))
            s_scr[sub, h] = _dot_nt(qh, kw[:, pair(h)]) + bias_refs[sub][0, h]
        stats = []
        for h in range(N_HEADS_A):
            s = s_scr[sub, h]
            m = jnp.max(s, axis=1, keepdims=True)
            e = jnp.exp2(s - m)
            l = jnp.sum(e, axis=1, keepdims=True)
            p_scr[sub, h] = e.astype(BF16)
            stats.append((1.0 / l, m + jnp.log2(l)))
        for h in range(0, N_HEADS_A, 2):
            (r0, lse0), (r1, lse1) = stats[h], stats[h + 1]
            o0 = _dot(p_scr[sub, h], vw[:, pair(h)])
            o1 = _dot(p_scr[sub, h + 1], vw[:, pair(h)])
            o_ref[0, rows, pair(h)] = jnp.where(first, o0 * r0, o1 * r1)
            lse_ref[0, rows, pair(h)] = jnp.where(first, lse0, lse1)


def _dilated(qv, kv, vv, bias, bias_index, dil):
    b, sub_len, _ = qv.shape
    w = WIDTH_A
    nq = sub_len // Q_BLOCK
    qb = min(MAX_Q_BLOCKS_PER_STEP, nq)
    assert sub_len >= K_WINDOW and sub_len % (Q_BLOCK * qb) == 0
    qspec = pl.BlockSpec((1, Q_BLOCK * qb, w), lambda bi, r, j: (bi, j, r))
    kvspec = pl.BlockSpec((1, sub_len, w), lambda bi, r, j: (bi, 0, r))
    bspecs = [pl.BlockSpec((1, N_HEADS_A, Q_BLOCK, K_WINDOW),
                           lambda bi, r, j, sub=sub:
                           (bias_index(bi, r, j * qb + sub, nq), 0, 0, 0))
              for sub in range(qb)]
    return pl.pallas_call(
        functools.partial(_dilated_kernel, sub_len=sub_len, qb=qb),
        grid=(b, dil, nq // qb),
        in_specs=[qspec, kvspec, kvspec] + bspecs,
        out_specs=[qspec, qspec],
        out_shape=[jax.ShapeDtypeStruct((b, sub_len, dil * w), F32)] * 2,
        scratch_shapes=[pltpu.VMEM((qb, N_HEADS_A, Q_BLOCK, K_WINDOW), F32),
                        pltpu.VMEM((qb, N_HEADS_A, Q_BLOCK, K_WINDOW), BF16)],
        compiler_params=pltpu.CompilerParams(
            dimension_semantics=("parallel", "parallel", "arbitrary"),
            vmem_limit_bytes=VMEM_LIMIT),
        name=f"dilated{dil}",
    )(qv, kv, vv, *([bias] * qb))


def _dilated_all(views, positions, tbl):
    b, s = positions.shape
    dils = [dil for _, dil in DILATED_PATTERNS]
    consecutive = jnp.all(positions[:, 1:] - positions[:, :-1] == 1)

    def run(bias, index_fns):
        outs = []
        for g, dil in enumerate(dils):
            qv, kv, vv = views[3 * g:3 * g + 3]
            outs.extend(_dilated(qv, kv, vv, bias, index_fns[g], dil))
        return tuple(outs)

    def shared_tiles():
        offs, qpos, kpos, fns = [], [], [], []
        for g, dil in enumerate(dils):
            for off in (0, -HALF_WINDOW, -2 * HALF_WINDOW):
                offs.append(off)
                qpos.append(dil * jnp.arange(Q_BLOCK, dtype=jnp.int32))
                kpos.append(dil * (off + jnp.arange(K_WINDOW, dtype=jnp.int32)))
            fns.append(lambda bi, r, j, nq, g=g:
                       3 * g + jnp.where(j == 0, 0, jnp.where(j == nq - 1, 2, 1)))
        bias = _bias_tiles(jnp.array(offs, jnp.int32), jnp.stack(qpos)[:, :, None],
                           jnp.stack(kpos)[:, None, :], tbl)
        return run(bias, fns)

    def per_block_tiles():
        offs, qpos, kpos, fns = [], [], [], []
        base = 0
        for g, dil in enumerate(dils):
            sub_len = s // dil
            nq = sub_len // Q_BLOCK
            pos_t = positions.reshape(b, sub_len, dil).transpose(0, 2, 1).reshape(b * dil, sub_len)
            starts = _window_start(jnp.arange(nq), sub_len)
            win = starts[:, None] + jnp.arange(K_WINDOW)[None, :]
            qpos.append(pos_t.reshape(b * dil * nq, Q_BLOCK))
            kpos.append(pos_t[:, win].reshape(b * dil * nq, K_WINDOW))
            offs.append(jnp.tile(starts - jnp.arange(nq) * Q_BLOCK, b * dil))
            fns.append(lambda bi, r, j, nq, base=base, dil=dil: base + (bi * dil + r) * nq + j)
            base += b * dil * nq
        bias = _bias_tiles(jnp.concatenate(offs).astype(jnp.int32),
                           jnp.concatenate(qpos)[:, :, None], jnp.concatenate(kpos)[:, None, :], tbl)
        return run(bias, fns)

    return lax.cond(consecutive, shared_tiles, per_block_tiles)


def _mla_kernel(q_ref, k_ref, v_ref, o_ref, *, tk):
    q = q_ref[0, 0]
    tq = q.shape[0]
    n_kv = k_ref.shape[2] // tk

    def body(i, carry):
        m, acc = carry
        start = pl.multiple_of(i * tk, tk)
        k = k_ref[0, 0, pl.ds(start, tk), :]
        v = v_ref[0, 0, pl.ds(start, tk), :]
        s = _dot_nt(q, k)
        m_new = jnp.maximum(m, jnp.max(s, axis=1, keepdims=True))
        p = jnp.exp2(s - m_new)
        acc = jnp.exp2(m - m_new) * acc + _dot(p.astype(BF16), v)
        return m_new, acc

    m0 = jnp.full((tq, 1), NEG_INF, F32)
    _, acc = lax.fori_loop(0, n_kv, body, (m0, jnp.zeros((tq, LANES), F32)), unroll=True)
    lane = lax.broadcasted_iota(jnp.int32, (1, LANES), 1)
    denom = jnp.sum(jnp.where(lane == V_DIM_B, acc, 0.0), axis=1, keepdims=True)
    o_ref[0, 0] = jnp.where(lane < V_DIM_B, acc / denom, 0.0).astype(BF16)


def _mla(qm, km, vm, tq, tk):
    b, nh, s, _ = qm.shape
    qspec = pl.BlockSpec((1, 1, tq, LANES), lambda bi, h, i: (bi, h, i, 0))
    kvspec = pl.BlockSpec((1, 1, s, LANES), lambda bi, h, i: (bi, h, 0, 0))
    return pl.pallas_call(
        functools.partial(_mla_kernel, tk=tk),
        grid=(b, nh, s // tq),
        in_specs=[qspec, kvspec, kvspec],
        out_specs=qspec,
        out_shape=jax.ShapeDtypeStruct((b, nh, s, LANES), BF16),
        compiler_params=pltpu.CompilerParams(
            dimension_semantics=("parallel", "parallel", "arbitrary"),
            vmem_limit_bytes=VMEM_LIMIT),
        name="mla",
    )(qm, km, vm)


def _memkv_kernel(mem_ref, g_ref, w_ref, k_ref, v_ref):
    d = mem_ref.shape[2]
    kv = _dot(_rms(mem_ref[0], g_ref[...]).astype(BF16), w_ref[...])
    k_ref[0] = kv[:, :d].astype(BF16)
    v_ref[0] = kv[:, d:].astype(BF16)


def _memkv(mem, g_mem, w_mkv):
    b, m, d = mem.shape
    spec = pl.BlockSpec((1, m, d), lambda bi: (bi, 0, 0))
    return pl.pallas_call(
        _memkv_kernel,
        grid=(b,),
        in_specs=[spec, pl.BlockSpec((1, d), lambda bi: (0, 0)),
                  pl.BlockSpec((d, 2 * d), lambda bi: (0, 0))],
        out_specs=[spec, spec],
        out_shape=[jax.ShapeDtypeStruct((b, m, d), BF16)] * 2,
        compiler_params=pltpu.CompilerParams(
            dimension_semantics=("parallel",), vmem_limit_bytes=VMEM_LIMIT),
        name="memkv",
    )(mem, g_mem, w_mkv)


def _post_kernel(x_ref, o1_ref, o2_ref, o3_ref, l1_ref, l2_ref, l3_ref, ob_ref,
                 goa_ref, gob_ref, woa_ref, wob_ref, gx_ref, wmq_ref, km_ref, vm_ref, wmo_ref,
                 gmoe_ref, wr_ref, br_ref,
                 h_ref, xn_ref, logit_ref, *nat_scr):
    tm = x_ref.shape[1]
    n_chunk = WIDTH_A // LANES

    for view_ref, scr in zip((o2_ref, o3_ref, l2_ref, l3_ref), nat_scr):
        dil = view_ref.shape[2] // WIDTH_A
        for r in range(dil):
            for c in range(n_chunk):
                col = r * WIDTH_A + c * LANES
                scr[c, pl.ds(r, tm // dil, stride=dil), :] = view_ref[0, :, col:col + LANES]

    def natural(scr, rows):
        return jnp.concatenate([scr[c, rows, :] for c in range(n_chunk)], axis=1)

    rows_per_chain = tm // POST_CHAINS

    def chain(ci):
        rows = slice(ci * rows_per_chain, (ci + 1) * rows_per_chain)
        o1, o2, o3 = o1_ref[0, rows, :], natural(nat_scr[0], rows), natural(nat_scr[1], rows)
        l1, l2, l3 = l1_ref[0, rows, :], natural(nat_scr[2], rows), natural(nat_scr[3], rows)
        mx = jnp.maximum(jnp.maximum(l1, l2), l3)
        w1, w2, w3 = jnp.exp2(l1 - mx), jnp.exp2(l2 - mx), jnp.exp2(l3 - mx)
        oa = (w1 * o1 + w2 * o2 + w3 * o3) / (w1 + w2 + w3)
        oan = _rms(oa, goa_ref[...]).astype(BF16)
        ob = jnp.concatenate([ob_ref[0, h, rows, :] for h in range(N_HEADS_B)], axis=1).astype(F32)
        ms_b = jnp.sum(ob * ob, axis=1, keepdims=True) * (1.0 / (N_HEADS_B * V_DIM_B))
        obn = (ob * lax.rsqrt(ms_b + EPS) * gob_ref[...]).astype(BF16)
        yield
        h1 = x_ref[0, rows, :] + _dot(oan, woa_ref[...]) + _dot(obn, wob_ref[...])
        yield
        hn = _rms(h1, gx_ref[...]).astype(BF16)
        yield
        q = _dot(hn, wmq_ref[...]).astype(BF16)
        yield
        dh = q.shape[1] // N_HEADS_MEM
        heads = []
        for h in range(N_HEADS_MEM):
            sl = slice(h * dh, (h + 1) * dh)
            s = _dot_nt(q[:, sl], km_ref[0, :, sl])
            yield
            e = jnp.exp(s - jnp.max(s, axis=1, keepdims=True))
            p = e / jnp.sum(e, axis=1, keepdims=True)
            yield
            heads.append(_dot(p.astype(BF16), vm_ref[0, :, sl]))
        o = jnp.concatenate(heads, axis=1).astype(BF16)
        yield
        h2 = h1 + _dot(o, wmo_ref[...])
        h_ref[0, rows, :] = h2
        yield
        xn = _rms(h2, gmoe_ref[...])
        packed = _pack_bf16_pairs(xn)
        rows_per_token = packed.shape[1] // LANES
        _store_token_rows(xn_ref.at[pl.ds(ci * rows_per_chain * rows_per_token,
                                          rows_per_chain * rows_per_token)], packed)
        yield
        logit_ref[0, rows, :] = jnp.dot(xn, wr_ref[...], preferred_element_type=F32,
                                        precision=lax.Precision.HIGHEST) + br_ref[...]

    _run_skewed([chain(ci) for ci in range(POST_CHAINS)])


def _post(x, o_pats, lse_pats, ob, g_out_a, g_out_b_pad, w_o_a, w_o_b_pad, g_xattn, w_mq, kmem,
          vmem, w_mo, g_moe, w_router_pad, b_router_pad, tm):
    b, s, d = x.shape
    m = kmem.shape[1]
    full = lambda shape: pl.BlockSpec(shape, lambda bi, i: (0,) * len(shape))
    row = lambda w: pl.BlockSpec((1, tm, w), lambda bi, i: (bi, i, 0))
    memspec = pl.BlockSpec((1, m, d), lambda bi, i: (bi, 0, 0))
    hw = N_HEADS_B * LANES
    views = [pl.BlockSpec((1, tm // dil, dil * WIDTH_A), lambda bi, i: (bi, i, 0))
             for _, dil in DILATED_PATTERNS]
    return pl.pallas_call(
        _post_kernel,
        grid=(b, s // tm),
        in_specs=[row(d)] + views + views
        + [pl.BlockSpec((1, N_HEADS_B, tm, LANES), lambda bi, i: (bi, 0, i, 0)),
           full((1, WIDTH_A)), full((1, hw)), full((WIDTH_A, d)), full((hw, d)),
           full((1, d)), full((d, d)), memspec, memspec, full((d, d)),
           full((1, d)), full((d, LANES)), full((1, LANES))],
        out_specs=[row(d), pl.BlockSpec((tm * d // 2 // LANES, LANES), lambda bi, i: (bi * (s // tm) + i, 0)),
                   row(LANES)],
        out_shape=[jax.ShapeDtypeStruct((b, s, d), F32),
                   jax.ShapeDtypeStruct((b * s * d // 2 // LANES, LANES), jnp.uint32),
                   jax.ShapeDtypeStruct((b, s, LANES), F32)],
        scratch_shapes=[pltpu.VMEM((WIDTH_A // LANES, tm, LANES), F32)] * 4,
        compiler_params=pltpu.CompilerParams(
            dimension_semantics=("parallel", "parallel"), vmem_limit_bytes=VMEM_LIMIT),
        name="post",
    )(x, *o_pats, *lse_pats, ob, g_out_a, g_out_b_pad, w_o_a, w_o_b_pad, g_xattn, w_mq, kmem,
      vmem, w_mo, g_moe, w_router_pad, b_router_pad)


def _route_kernel(logit_ref, gate_ref, slot_ref, cslot_ref, count_ref, tcnt_ref, tcarry_ref, ccnt_ref,
                  ccarry_ref, carry_ref):
    @pl.when(pl.program_id(0) == 0)
    def _():
        carry_ref[...] = jnp.zeros_like(carry_ref)

    l = logit_ref[...]
    tr = l.shape[0]
    lane_i = lax.broadcasted_iota(jnp.int32, l.shape, 1)
    lane = lane_i.astype(F32)
    vals, idxs = [], []
    for _ in range(TOP_K):
        m = jnp.max(l, axis=1, keepdims=True)
        idx = jnp.min(jnp.where(l == m, lane, float(LANES)), axis=1, keepdims=True)
        vals.append(m)
        idxs.append(idx)
        l = jnp.where(lane == idx, -jnp.inf, l)
    exps = [jnp.exp(v - vals[0]) for v in vals]
    denom = exps[0] + exps[1] + exps[2] + exps[3]
    onehot = jnp.zeros(l.shape, F32)
    for idx in idxs:
        onehot = onehot + jnp.where(lane == idx, 1.0, 0.0)
    r = lax.broadcasted_iota(jnp.int32, (tr, tr), 0)
    c = lax.broadcasted_iota(jnp.int32, (tr, tr), 1)
    tri = jnp.where(c < r, 1.0, 0.0).astype(BF16)
    local = _dot(tri, onehot.astype(BF16))
    tile_cnt = jnp.sum(onehot, axis=0, keepdims=True)
    li = lax.broadcasted_iota(jnp.int32, (LANES, LANES), 0)
    lj = lax.broadcasted_iota(jnp.int32, (LANES, LANES), 1)
    prefix = jnp.dot(jnp.broadcast_to(tile_cnt, (8, LANES)), jnp.where(li < lj, 1.0, 0.0),
                     preferred_element_type=F32, precision=lax.Precision.HIGHEST)[0:1]
    grouped = local + prefix
    n_sub = tr // COMBINE_TILE
    sub_cnt = [jnp.sum(onehot[u * COMBINE_TILE:(u + 1) * COMBINE_TILE], axis=0, keepdims=True)
               for u in range(n_sub)]
    sub_prefix = jnp.dot(jnp.concatenate(sub_cnt + [jnp.zeros((8 - n_sub, LANES), F32)], axis=0),
                         jnp.where(li < lj, 1.0, 0.0), preferred_element_type=F32,
                         precision=lax.Precision.HIGHEST)
    shifts, sub_base = [], jnp.zeros((1, LANES), F32)
    for u in range(n_sub):
        shifts.append(jnp.broadcast_to(sub_prefix[u:u + 1] - sub_base, (COMBINE_TILE, LANES)))
        ccnt_ref[u * 8:(u + 1) * 8, :] = jnp.broadcast_to(sub_cnt[u], (8, LANES)).astype(jnp.int32)
        ccarry_ref[u * 8:(u + 1) * 8, :] = jnp.broadcast_to(carry_ref[...] + sub_base,
                                                            (8, LANES)).astype(jnp.int32)
        sub_base = sub_base + sub_cnt[u]
    sub_grouped = local + jnp.concatenate(shifts, axis=0)
    gate = jnp.zeros(l.shape, F32)
    slot = jnp.zeros(l.shape, jnp.int32)
    cslot = jnp.zeros(l.shape, jnp.int32)
    for k in range(TOP_K):
        mine = lane == idxs[k]
        sk = jnp.sum(jnp.where(mine, grouped, 0.0), axis=1, keepdims=True)
        ck = jnp.sum(jnp.where(mine, sub_grouped, 0.0), axis=1, keepdims=True)
        gate = jnp.where(lane_i == k, exps[k] / denom, gate)
        slot = jnp.where(lane_i == k, sk.astype(jnp.int32), slot)
        cslot = jnp.where(lane_i == k, ck.astype(jnp.int32), cslot)
    gate_ref[...] = gate
    slot_ref[...] = slot
    cslot_ref[...] = cslot
    tcnt_ref[...] = jnp.broadcast_to(tile_cnt, tcnt_ref.shape).astype(jnp.int32)
    tcarry_ref[...] = jnp.broadcast_to(carry_ref[...], tcarry_ref.shape).astype(jnp.int32)
    total = carry_ref[...] + tile_cnt
    carry_ref[...] = total
    count_ref[...] = total.astype(jnp.int32)


def _route(logits, tr):
    t = logits.shape[0]
    spec = pl.BlockSpec((tr, LANES), lambda i: (i, 0))
    return pl.pallas_call(
        _route_kernel,
        grid=(t // tr,),
        in_specs=[spec],
        out_specs=[spec, spec, spec, pl.BlockSpec((1, LANES), lambda i: (0, 0)),
                   pl.BlockSpec((8, LANES), lambda i: (i, 0)), pl.BlockSpec((8, LANES), lambda i: (i, 0)),
                   pl.BlockSpec((8 * tr // COMBINE_TILE, LANES), lambda i: (i, 0)),
                   pl.BlockSpec((8 * tr // COMBINE_TILE, LANES), lambda i: (i, 0))],
        out_shape=[jax.ShapeDtypeStruct((t, LANES), F32)] + [jax.ShapeDtypeStruct((t, LANES), jnp.int32)] * 2
        + [jax.ShapeDtypeStruct((1, LANES), jnp.int32)]
        + [jax.ShapeDtypeStruct((t // tr * 8, LANES), jnp.int32)] * 2
        + [jax.ShapeDtypeStruct((t // COMBINE_TILE * 8, LANES), jnp.int32)] * 2,
        scratch_shapes=[pltpu.VMEM((1, LANES), F32)],
        compiler_params=pltpu.CompilerParams(
            dimension_semantics=("arbitrary",), vmem_limit_bytes=VMEM_LIMIT),
        name="route",
    )(logits)


def _dispatch_kernel(dst_ref, cnt_ref, ends_ref, x_ref, slot_ref, xs_hbm, zbuf, stage, sem, zsem):
    c = pl.program_id(0)
    buf = lax.rem(c, 2)
    n_chunk = x_ref.shape[0] // DISPATCH_CHUNK
    blk_rows = zbuf.shape[0]
    n_staged = DISPATCH_CHUNK * TOP_K

    @pl.when(c == 0)
    def _():
        zbuf[...] = jnp.zeros_like(zbuf)
        for b in range(2):
            stage[b, pl.ds(n_staged * n_chunk, DISPATCH_PIECE * n_chunk), :] = jnp.zeros(
                (DISPATCH_PIECE * n_chunk, LANES), stage.dtype)

        def zero_copy(start):
            return pltpu.make_async_copy(
                zbuf, xs_hbm.at[pl.ds(pl.multiple_of(start * n_chunk, n_chunk), blk_rows)], zsem)

        blk = blk_rows // n_chunk
        used_end = ends_ref[N_EXPERTS]
        starts = [(ends_ref[e + 1] - ends_ref[e] >= back * blk, ends_ref[e + 1] - back * blk)
                  for e in range(N_EXPERTS) for back in (1, 2)]
        starts += [((used_end + (j + 1) * blk) * n_chunk <= xs_hbm.shape[0], used_end + j * blk)
                   for j in range(N_EXPERTS)]
        for cond, start in starts:
            @pl.when(cond)
            def _():
                zero_copy(start).start()
        for cond, start in starts:
            @pl.when(cond)
            def _():
                zero_copy(start).wait()

    x_lo, x_hi = _unpack_bf16_pairs(_load_token_rows(x_ref, DISPATCH_CHUNK))
    x_lo, x_hi = x_lo.astype(BF16), x_hi.astype(BF16)
    slot_t = jnp.transpose(slot_ref[...].astype(F32))
    bits = lambda v: lax.bitcast_convert_type(v, jnp.uint32)
    for part in range(TOP_K):
        j = (lax.broadcasted_iota(jnp.int32, (DISPATCH_CHUNK, 1), 0) + part * DISPATCH_CHUNK).astype(F32)
        pick = jnp.zeros((DISPATCH_CHUNK, DISPATCH_CHUNK), F32)
        for k in range(TOP_K):
            pick = jnp.where(slot_t[k:k + 1, :] == j, 1.0, pick)
        pick = pick.astype(BF16)
        packed = bits(_dot(pick, x_hi)) | (bits(_dot(pick, x_lo)) >> 16)
        _store_token_rows(
            stage.at[buf, pl.ds(part * DISPATCH_CHUNK * n_chunk, DISPATCH_CHUNK * n_chunk)], packed)

    def piece(src_tok, dst_tok):
        rows = DISPATCH_PIECE * n_chunk
        return pltpu.make_async_copy(
            stage.at[buf, pl.ds(pl.multiple_of(src_tok * n_chunk, n_chunk), rows)],
            xs_hbm.at[pl.ds(pl.multiple_of(dst_tok * n_chunk, n_chunk), rows)], sem)

    def drain(step):
        n_pieces = 0
        for e in range(N_EXPERTS):
            n_pieces = n_pieces + (cnt_ref[step * N_EXPERTS + e] + DISPATCH_PIECE - 1) // DISPATCH_PIECE

        def wait_one(i, carry):
            piece(0, 0).wait()
            return carry
        lax.fori_loop(0, n_pieces, wait_one, 0)

    @pl.when(c > 0)
    def _():
        drain(c - 1)

    staged = 0
    for e in range(N_EXPERTS):
        cnt = cnt_ref[c * N_EXPERTS + e]
        dst = dst_ref[c * N_EXPERTS + e]
        for p in range(DISPATCH_CHUNK // DISPATCH_PIECE):
            @pl.when(p * DISPATCH_PIECE < cnt)
            def _():
                piece(staged + p * DISPATCH_PIECE, dst + p * DISPATCH_PIECE).start()
        staged = staged + cnt

    @pl.when(c == pl.num_programs(0) - 1)
    def _():
        drain(c)


def _dispatch(dst, cnt, ends0, xn_rows, slots, n_tok, n_slots):
    n_chunk = xn_rows.shape[0] // n_tok
    staged_rows = (DISPATCH_CHUNK * TOP_K + DISPATCH_PIECE) * n_chunk
    grid_spec = pltpu.PrefetchScalarGridSpec(
        num_scalar_prefetch=3,
        grid=(n_tok // DISPATCH_CHUNK,),
        in_specs=[pl.BlockSpec((DISPATCH_CHUNK * n_chunk, LANES), lambda c, *_: (c, 0)),
                  pl.BlockSpec((DISPATCH_CHUNK, LANES), lambda c, *_: (c, 0))],
        out_specs=pl.BlockSpec(memory_space=pl.ANY),
        scratch_shapes=[pltpu.VMEM((EXPERT_BLOCK * n_chunk, LANES), xn_rows.dtype),
                        pltpu.VMEM((2, staged_rows, LANES), xn_rows.dtype),
                        pltpu.SemaphoreType.DMA(()), pltpu.SemaphoreType.DMA(())],
    )
    return pl.pallas_call(
        _dispatch_kernel,
        grid_spec=grid_spec,
        out_shape=jax.ShapeDtypeStruct((n_slots * n_chunk, LANES), xn_rows.dtype),
        compiler_params=pltpu.CompilerParams(
            dimension_semantics=("arbitrary",), vmem_limit_bytes=VMEM_LIMIT),
        name="dispatch",
    )(dst, cnt, ends0, xn_rows, slots)


def _expert_kernel(be_ref, nused_ref, next_ref, valid_ref, xs_ref, wgu_hbm, bgu_ref, wd_hbm, bd_ref, y_ref,
                   wgu_f32, wd_f32, wgu_bf, wd_bf, sems):
    i = pl.program_id(0)
    n_chunk = xs_ref.shape[0] // EXPERT_BLOCK

    def mlp(n_rows):
        part = pl.ds(0, n_rows * n_chunk)
        x_lo, x_hi = _unpack_bf16_pairs(_load_token_rows(xs_ref.at[part], n_rows))
        x = jnp.concatenate([x_lo.astype(BF16), x_hi.astype(BF16)], axis=1)
        gu = _dot(x, wgu_bf[...]) + bgu_ref[0]
        de = gu.shape[1] // 2
        gate = jnp.minimum(gu[:, :de], SWIGLU_LIMIT)
        up = jnp.clip(gu[:, de:], -SWIGLU_LIMIT, SWIGLU_LIMIT)
        hmid = (up + 1.0) * (gate * jax.nn.sigmoid(SWIGLU_ALPHA * gate))
        y = _dot(hmid.astype(BF16), wd_bf[...]) + bd_ref[0]
        _store_token_rows(y_ref.at[part], _pack_bf16_pairs(y))

    def fetch(e):
        return (pltpu.make_async_copy(wgu_hbm.at[e], wgu_f32, sems.at[0]),
                pltpu.make_async_copy(wd_hbm.at[e], wd_f32, sems.at[1]))

    @pl.when(i == 0)
    def _():
        for copy in fetch(be_ref[0]):
            copy.start()

    @pl.when(i < nused_ref[0])
    def _():
        @pl.when(jnp.logical_or(i == 0, be_ref[i] != be_ref[jnp.maximum(i - 1, 0)]))
        def _():
            for copy in fetch(be_ref[i]):
                copy.wait()
            wgu_bf[...] = wgu_f32[...].astype(BF16)
            wd_bf[...] = wd_f32[...].astype(BF16)

            @pl.when(next_ref[i] >= 0)
            def _():
                for copy in fetch(next_ref[i]):
                    copy.start()

        valid = valid_ref[i]
        for n_rows in range(EXPERT_STEP, EXPERT_BLOCK + 1, EXPERT_STEP):
            @pl.when(jnp.logical_and(valid > n_rows - EXPERT_STEP, valid <= n_rows))
            def _():
                mlp(n_rows)
                if n_rows < EXPERT_BLOCK:
                    rest = (EXPERT_BLOCK - n_rows) * n_chunk
                    y_ref[pl.ds(n_rows * n_chunk, rest), :] = jnp.zeros((rest, LANES), y_ref.dtype)

    @pl.when(jnp.logical_or(i >= nused_ref[0], valid_ref[i] == 0))
    def _():
        y_ref[...] = jnp.zeros_like(y_ref)


def _experts(block_expert, n_used, next_expert, valid_rows, xs_rows, w_gate_up, b_gate_up, w_down, b_down):
    d, de2 = w_gate_up.shape[1:]
    n_chunk = d // 2 // LANES
    blk = EXPERT_BLOCK
    n_blk = xs_rows.shape[0] // (blk * n_chunk)
    grid_spec = pltpu.PrefetchScalarGridSpec(
        num_scalar_prefetch=4,
        grid=(n_blk,),
        in_specs=[pl.BlockSpec((blk * n_chunk, LANES),
                               lambda i, be, nu, *_: (jnp.maximum(jnp.minimum(i, nu[0] - 1), 0), 0)),
                  pl.BlockSpec(memory_space=pl.ANY),
                  pl.BlockSpec((1, 1, de2), lambda i, be, *_: (be[i], 0, 0)),
                  pl.BlockSpec(memory_space=pl.ANY),
                  pl.BlockSpec((1, 1, d), lambda i, be, *_: (be[i], 0, 0))],
        out_specs=pl.BlockSpec((blk * n_chunk, LANES), lambda i, *_: (i, 0)),
        scratch_shapes=[pltpu.VMEM((d, de2), F32), pltpu.VMEM((de2 // 2, d), F32),
                        pltpu.VMEM((d, de2), BF16), pltpu.VMEM((de2 // 2, d), BF16),
                        pltpu.SemaphoreType.DMA((2,))],
    )
    return pl.pallas_call(
        _expert_kernel,
        grid_spec=grid_spec,
        out_shape=jax.ShapeDtypeStruct(xs_rows.shape, xs_rows.dtype),
        compiler_params=pltpu.CompilerParams(
            dimension_semantics=("arbitrary",), vmem_limit_bytes=VMEM_LIMIT),
        name="experts",
    )(block_expert, n_used, next_expert, valid_rows, xs_rows, w_gate_up, b_gate_up.reshape(N_EXPERTS, 1, de2),
      w_down, b_down.reshape(N_EXPERTS, 1, d))


def _combine_kernel(start_ref, len_ref, h_ref, gate_ref, slot_ref, gfin_ref, y_hbm, out_ref, stage, sems):
    i = pl.program_id(0)
    n = pl.num_programs(0)
    buf = lax.rem(i, 2)
    tc = h_ref.shape[0]
    n_staged = tc * TOP_K
    n_chunk = stage.shape[1] // n_staged

    def issue(tile, b):
        staged = 0
        for e in range(N_EXPERTS):
            run_len = len_ref[tile * N_EXPERTS + e]
            run_start = start_ref[tile * N_EXPERTS + e]
            for bit in reversed(range(COMBINE_TILE.bit_length())):
                size = 1 << bit
                done = (run_len >> (bit + 1)) << (bit + 1)

                @pl.when((run_len >> bit) & 1 == 1)
                def _():
                    pltpu.make_async_copy(
                        y_hbm.at[pl.ds(pl.multiple_of((run_start + done) * n_chunk, n_chunk), size * n_chunk)],
                        stage.at[b, pl.ds(pl.multiple_of((staged + done) * n_chunk, n_chunk), size * n_chunk)],
                        sems.at[b]).start()
            staged = staged + run_len

    @pl.when(i == 0)
    def _():
        issue(0, 0)

    @pl.when(i + 1 < n)
    def _():
        issue(i + 1, 1 - buf)

    pltpu.make_async_copy(y_hbm.at[pl.ds(0, n_staged * n_chunk)], stage.at[buf], sems.at[buf]).wait()

    y_lo, y_hi = _unpack_bf16_pairs(_load_token_rows(stage.at[buf], n_staged))
    y_lo, y_hi = y_lo.astype(BF16), y_hi.astype(BF16)
    gate = gate_ref[...]
    slot = slot_ref[...].astype(F32)
    j = lax.broadcasted_iota(jnp.int32, (1, n_staged), 1).astype(F32)
    pick = jnp.zeros((tc, n_staged), F32)
    for k in range(TOP_K):
        pick = jnp.where(slot[:, k:k + 1] == j, gate[:, k:k + 1], pick)
    pick_hi = pick.astype(BF16)
    pick_lo = (pick - pick_hi.astype(F32)).astype(BF16)
    moe = jnp.concatenate([_dot(pick_hi, y_lo) + _dot(pick_lo, y_lo),
                           _dot(pick_hi, y_hi) + _dot(pick_lo, y_hi)], axis=1)
    out_ref[...] = _rms(h_ref[...] + moe, gfin_ref[...])


def _combine(run_start, run_len, h2, gates, slots, g_final, y_rows):
    t, d = h2.shape
    tc = COMBINE_TILE
    n_chunk = d // 2 // LANES
    grid_spec = pltpu.PrefetchScalarGridSpec(
        num_scalar_prefetch=2,
        grid=(t // tc,),
        in_specs=[pl.BlockSpec((tc, d), lambda i, *_: (i, 0)),
                  pl.BlockSpec((tc, LANES), lambda i, *_: (i, 0)),
                  pl.BlockSpec((tc, LANES), lambda i, *_: (i, 0)),
                  pl.BlockSpec((1, d), lambda i, *_: (0, 0)),
                  pl.BlockSpec(memory_space=pl.ANY)],
        out_specs=pl.BlockSpec((tc, d), lambda i, *_: (i, 0)),
        scratch_shapes=[pltpu.VMEM((2, tc * TOP_K * n_chunk, LANES), y_rows.dtype),
                        pltpu.SemaphoreType.DMA((2,))],
    )
    return pl.pallas_call(
        _combine_kernel,
        grid_spec=grid_spec,
        out_shape=jax.ShapeDtypeStruct((t, d), F32),
        compiler_params=pltpu.CompilerParams(
            dimension_semantics=("arbitrary",), vmem_limit_bytes=VMEM_LIMIT),
        name="combine",
    )(run_start, run_len, h2, gates, slots, g_final, y_rows)


def _pad_heads(w, n_heads, width, offset=0):
    k = w.shape[0]
    w = w.reshape(k, n_heads, width)
    w = jnp.pad(w, ((0, 0), (0, 0), (offset, LANES - width - offset)))
    return w.reshape(k, n_heads * LANES)


def _rot_cols(w):
    half = w.shape[-1] // 2
    return jnp.concatenate([-w[..., half:], w[..., :half]], axis=-1)


def kernel(x, mem, positions, g_mix, w_in, g_q_a, w_q_b, g_kv_a, w_kv_b, rel_bias, g_out_a, g_out_b,
           w_o, g_xattn, g_mem, w_mq, w_mkv, w_mo, g_moe, w_router, b_router, w_gate_up, b_gate_up,
           w_down, b_down, g_final):
    b, s, d = x.shape
    t = b * s
    assert g_mix.shape[0] == 1, "single-layer block: the final norm is fused into the last stage"

    inv_freq = ROPE_THETA ** (-jnp.arange(0, QK_ROPE_DIM, 2, dtype=F32) / QK_ROPE_DIM)
    ang = positions.astype(F32)[..., None] * inv_freq
    cos_sin = jnp.concatenate([jnp.cos(ang), jnp.sin(ang)], axis=-1)
    half = QK_ROPE_DIM // 2
    src = jnp.arange(QK_ROPE_DIM)[:, None]
    lane = jnp.arange(LANES)[None, :]
    on_rope = (lane >= QK_NOPE_DIM) & (lane < QK_NOPE_DIM + QK_ROPE_DIM)
    place = ((lane - QK_NOPE_DIM) % half == src % half) & on_rope
    expand = jnp.concatenate([place & (src < half), place & (src >= half)], axis=1).astype(F32)
    tbl = jnp.pad(rel_bias.T.astype(F32), ((0, 0), (0, LANES - NUM_BUCKETS)))

    h = x
    for l in range(1):
        c0 = 3 * WIDTH_A + Q_LORA_RANK + KV_LORA_RANK
        w_kpe = w_in[l][:, c0:c0 + QK_ROPE_DIM]
        place = lambda w: jnp.pad(w, ((0, 0), (QK_NOPE_DIM, LANES - QK_NOPE_DIM - QK_ROPE_DIM)))
        w_in_ext = jnp.concatenate([w_in[l][:, :c0], place(w_kpe), place(_rot_cols(w_kpe))], 1).astype(BF16)
        dq = QK_NOPE_DIM + QK_ROPE_DIM
        wq3 = w_q_b[l].reshape(Q_LORA_RANK, N_HEADS_B, dq)
        wq = _pad_heads(wq3.reshape(Q_LORA_RANK, -1), N_HEADS_B, dq).astype(BF16)
        wq_rot3 = jnp.concatenate([jnp.zeros_like(wq3[..., :QK_NOPE_DIM]), _rot_cols(wq3[..., QK_NOPE_DIM:])], -1)
        wqr = _pad_heads(wq_rot3.reshape(Q_LORA_RANK, -1), N_HEADS_B, dq).astype(BF16)
        wkv3 = w_kv_b[l].reshape(KV_LORA_RANK, N_HEADS_B, QK_NOPE_DIM + V_DIM_B)
        wk = _pad_heads(wkv3[..., :QK_NOPE_DIM].reshape(KV_LORA_RANK, -1), N_HEADS_B, QK_NOPE_DIM).astype(BF16)
        wv = _pad_heads(wkv3[..., QK_NOPE_DIM:].reshape(KV_LORA_RANK, -1), N_HEADS_B, V_DIM_B).astype(BF16)

        *views, qm, km, vm = _proj(h, cos_sin, expand, g_mix[l][None], w_in_ext, g_q_a[l][None], wq, wqr,
                                   g_kv_a[l][None], wk, wv, tm=256 * PROJ_CHAINS)
        pats = _dilated_all(views, positions, tbl)
        ob = _mla(qm, km, vm, tq=2048, tk=512)
        kmem, vmem = _memkv(mem, g_mem[l][None], w_mkv[l].astype(BF16))

        g_out_b_pad = _pad_heads(g_out_b[l][None], N_HEADS_B, V_DIM_B)
        w_o_b_pad = _pad_heads(w_o[l][WIDTH_A:].T, N_HEADS_B, V_DIM_B).T.astype(BF16)
        w_mq_s = (w_mq[l] * ((d // N_HEADS_MEM) ** -0.5)).astype(BF16)
        w_router_pad = jnp.pad(w_router[l], ((0, 0), (0, LANES - N_EXPERTS)))
        b_router_pad = jnp.pad(b_router[l][None], ((0, 0), (0, LANES - N_EXPERTS)), constant_values=NEG_INF)
        h2, xn_rows, logits = _post(h, pats[0::2], pats[1::2], ob, g_out_a[l][None],
                               g_out_b_pad, w_o[l][:WIDTH_A].astype(BF16), w_o_b_pad, g_xattn[l][None],
                               w_mq_s, kmem, vmem, w_mo[l].astype(BF16), g_moe[l][None], w_router_pad,
                               b_router_pad, tm=256 * POST_CHAINS)

        gates, slots, cslots, counts, tile_cnt, tile_carry, ctile_cnt, ctile_carry = _route(
            logits.reshape(t, LANES), tr=DISPATCH_CHUNK)
        counts = counts[0, :N_EXPERTS]
        blk = EXPERT_BLOCK
        padded = jnp.where(counts > 0, (counts + DISPATCH_PIECE + blk - 1) // blk * blk, 0)
        ends = jnp.cumsum(padded)
        pad_start = ends - padded
        n_blk = t * TOP_K // blk + N_EXPERTS + -(-N_EXPERTS * DISPATCH_PIECE // blk)
        block_expert = jnp.minimum(
            jnp.sum(ends[None, :] <= (jnp.arange(n_blk) * blk)[:, None], axis=1),
            N_EXPERTS - 1).astype(jnp.int32)
        n_used = (ends[-1] // blk).astype(jnp.int32)[None]
        ends0 = jnp.concatenate([jnp.zeros((1,), jnp.int32), ends.astype(jnp.int32)])
        run_start = (pad_start[None, :] + tile_carry[::8, :N_EXPERTS]).astype(jnp.int32).reshape(-1)
        run_len = tile_cnt[::8, :N_EXPERTS].reshape(-1)
        xs_rows = _dispatch(run_start, run_len, ends0, xn_rows, slots, t, n_blk * blk)
        expert_ids = jnp.arange(N_EXPERTS, dtype=jnp.int32)
        run_end = jnp.sum(jnp.where(block_expert[:, None] == expert_ids, ends[None, :], 0), axis=1)
        next_expert = jnp.where(
            run_end < ends[-1],
            jnp.minimum(jnp.sum(ends[None, :] <= run_end[:, None], axis=1), N_EXPERTS - 1), -1).astype(jnp.int32)
        block_of = block_expert[:, None] == expert_ids
        real_end = jnp.sum(jnp.where(block_of, (pad_start + counts)[None, :], 0), axis=1)
        valid_rows = jnp.clip(real_end - jnp.arange(n_blk) * blk, 0, blk).astype(jnp.int32)
        y_rows = _experts(block_expert, n_used, next_expert, valid_rows, xs_rows, w_gate_up[l], b_gate_up[l], w_down[l], b_down[l])
        gather_start = (pad_start[None, :] + ctile_carry[::8, :N_EXPERTS]).astype(jnp.int32).reshape(-1)
        gather_len = ctile_cnt[::8, :N_EXPERTS].reshape(-1)
        out = _combine(gather_start, gather_len, h2.reshape(t, d), gates, cslots, g_final[None], y_rows)
        h = out.reshape(b, s, d)
    return h
```

```python
import functools
import math

import jax
import jax.numpy as jnp
from jax import lax
from jax.experimental import pallas as pl
from jax.experimental.pallas import tpu as pltpu

F32 = jnp.float32
BF16 = jnp.bfloat16

LANES = 128
EPS = 1e-6
NEG_INF = -1e30
LOG2_E = math.log2(math.e)

N_HEADS_A = 8
HEAD_DIM_A = 64
WIDTH_A = N_HEADS_A * HEAD_DIM_A
DILATED_PATTERNS = ((128, 1), (512, 4), (2048, 16))
N_HEADS_B = 8
QK_NOPE_DIM = 64
QK_ROPE_DIM = 32
V_DIM_B = 64
Q_LORA_RANK = 256
KV_LORA_RANK = 128
ROPE_THETA = 10000.0
NUM_BUCKETS = 32
MAX_DISTANCE = 1024
N_HEADS_MEM = 4
N_EXPERTS = 32
TOP_K = 4
SWIGLU_LIMIT = 7.0
SWIGLU_ALPHA = 1.702

Q_BLOCK = 128
K_WINDOW = 256
HALF_WINDOW = (K_WINDOW - Q_BLOCK) // 2
MAX_Q_BLOCKS_PER_STEP = 8
EXPERT_BLOCK = 512
EXPERT_STEP = 128
DISPATCH_CHUNK = 512
DISPATCH_PIECE = 64
COMBINE_TILE = 256
PROJ_CHAINS = 2
POST_CHAINS = 2
VMEM_LIMIT = 56 * 1024 * 1024


def _rms(x, g):
    return x * lax.rsqrt(jnp.mean(x * x, axis=-1, keepdims=True) + EPS) * g


def _dot(a, b):
    return jnp.dot(a, b, preferred_element_type=F32)


def _dot_nt(a, b):
    return lax.dot_general(a, b, (((1,), (1,)), ((), ())), preferred_element_type=F32)


def _run_skewed(chains):
    live = []
    pending = list(chains)
    while live or pending:
        if pending:
            live.append(pending.pop(0))
        for gen in list(live):
            try:
                next(gen)
            except StopIteration:
                live.remove(gen)


def _load_token_rows(ref, n_tok):
    n_chunk = ref.shape[0] // n_tok
    return jnp.concatenate([ref[pl.ds(c, n_tok, stride=n_chunk), :] for c in range(n_chunk)], axis=1)


def _pack_bf16_pairs(x):
    half = x.shape[1] // 2
    bits = lambda v: lax.bitcast_convert_type(v.astype(BF16).astype(F32), jnp.uint32)
    return bits(x[:, half:]) | (bits(x[:, :half]) >> 16)


def _unpack_bf16_pairs(w):
    lo = lax.bitcast_convert_type(w << 16, F32)
    hi = lax.bitcast_convert_type(w & jnp.uint32(0xFFFF0000), F32)
    return lo, hi


def _store_token_rows(ref, value):
    n_tok = value.shape[0]
    n_chunk = value.shape[1] // LANES
    for c in range(n_chunk):
        ref[pl.ds(c, n_tok, stride=n_chunk), :] = value[:, c * LANES:(c + 1) * LANES]


def _proj_kernel(x_ref, cs_ref, expand_ref, gmix_ref, win_ref, gq_ref, wq_ref, wqr_ref,
                 gkv_ref, wk_ref, wv_ref,
                 q1_ref, k1_ref, v1_ref, q4_ref, k4_ref, v4_ref, q16_ref, k16_ref, v16_ref,
                 qm_ref, km_ref, vm_ref, za_scr):
    tm = x_ref.shape[1]
    c0 = 3 * WIDTH_A
    c1 = c0 + Q_LORA_RANK + KV_LORA_RANK
    per_group = WIDTH_A // LANES
    rows_per_chain = tm // PROJ_CHAINS
    lane = lax.broadcasted_iota(jnp.int32, (1, LANES), 1)
    in_rope = jnp.logical_and(lane >= QK_NOPE_DIM, lane < QK_NOPE_DIM + QK_ROPE_DIM)
    ones_col = jnp.where(lane == V_DIM_B, 1.0, 0.0).astype(F32)
    scale = (QK_NOPE_DIM + QK_ROPE_DIM) ** -0.5 * LOG2_E

    def chain(ci):
        r0 = ci * rows_per_chain
        rows = slice(r0, r0 + rows_per_chain)
        xn = _rms(x_ref[0, rows, :], gmix_ref[...]).astype(BF16)
        yield
        z = _dot(xn, win_ref[...])
        yield
        for c in range(3 * per_group):
            chunk = z[:, c * LANES:(c + 1) * LANES]
            za_scr[c, rows, :] = chunk * (HEAD_DIM_A ** -0.5 * LOG2_E) if c < per_group else chunk
        for dil, refs in ((1, (q1_ref, k1_ref, v1_ref)), (4, (q4_ref, k4_ref, v4_ref)),
                          (16, (q16_ref, k16_ref, v16_ref))):
            n = rows_per_chain // dil
            for r in range(dil):
                for c in range(3 * per_group):
                    picked = za_scr[c, pl.ds(r0 + r, n, stride=dil), :].astype(BF16)
                    col = r * WIDTH_A + (c % per_group) * LANES
                    refs[c // per_group][0, r0 // dil:r0 // dil + n, col:col + LANES] = picked
        zqn = _rms(z[:, c0:c0 + Q_LORA_RANK], gq_ref[...]).astype(BF16)
        zkvn = _rms(z[:, c0 + Q_LORA_RANK:c1], gkv_ref[...]).astype(BF16)
        yield
        spread = jnp.dot(cs_ref[0, rows, :], expand_ref[...], preferred_element_type=F32,
                         precision=lax.Precision.HIGHEST)
        q = _dot(zqn, wq_ref[...])
        qr = _dot(zqn, wqr_ref[...])
        kn = _dot(zkvn, wk_ref[...])
        vv = _dot(zkvn, wv_ref[...])
        yield
        cos = jnp.where(in_rope, spread[:, :LANES], 1.0)
        sin = spread[:, LANES:]
        kpe = z[:, c1:c1 + LANES] * cos + z[:, c1 + LANES:c1 + 2 * LANES] * sin
        for h in range(N_HEADS_B):
            sl = slice(h * LANES, (h + 1) * LANES)
            qm_ref[0, h, rows, :] = ((q[:, sl] * cos + qr[:, sl] * sin) * scale).astype(BF16)
            km_ref[0, h, rows, :] = (kn[:, sl] + kpe).astype(BF16)
            vm_ref[0, h, rows, :] = (vv[:, sl] + ones_col).astype(BF16)

    _run_skewed([chain(ci) for ci in range(PROJ_CHAINS)])


def _proj(x, cos_sin, expand, g_mix, w_in_ext, g_q, wq, wqr, g_kv, wk, wv, tm):
    b, s, d = x.shape
    n_in = w_in_ext.shape[1]
    hw = N_HEADS_B * LANES
    full = lambda shape: pl.BlockSpec(shape, lambda bi, i: (0,) * len(shape))
    row = lambda w: pl.BlockSpec((1, tm, w), lambda bi, i: (bi, i, 0))
    head = pl.BlockSpec((1, N_HEADS_B, tm, LANES), lambda bi, i: (bi, 0, i, 0))
    dils = [dil for _, dil in DILATED_PATTERNS]
    view_specs = [pl.BlockSpec((1, tm // dil, dil * WIDTH_A), lambda bi, i: (bi, i, 0))
                  for dil in dils for _ in range(3)]
    view_shapes = [jax.ShapeDtypeStruct((b, s // dil, dil * WIDTH_A), BF16)
                   for dil in dils for _ in range(3)]
    return pl.pallas_call(
        _proj_kernel,
        grid=(b, s // tm),
        in_specs=[row(d), row(cos_sin.shape[2]), full(expand.shape), full((1, d)), full((d, n_in)),
                  full((1, Q_LORA_RANK)), full((Q_LORA_RANK, hw)), full((Q_LORA_RANK, hw)),
                  full((1, KV_LORA_RANK)), full((KV_LORA_RANK, hw)), full((KV_LORA_RANK, hw))],
        out_specs=view_specs + [head, head, head],
        out_shape=view_shapes + [jax.ShapeDtypeStruct((b, N_HEADS_B, s, LANES), BF16)] * 3,
        scratch_shapes=[pltpu.VMEM((3 * WIDTH_A // LANES, tm, LANES), F32)],
        compiler_params=pltpu.CompilerParams(
            dimension_semantics=("parallel", "parallel"), vmem_limit_bytes=VMEM_LIMIT),
        name="proj",
    )(x, cos_sin, expand, g_mix, w_in_ext, g_q, wq, wqr, g_kv, wk, wv)


_LOG_BUCKET_STARTS = tuple(
    next(n for n in range(8, 4096)
         if int(math.log(n / 8) / math.log(MAX_DISTANCE / 8) * 8) >= t)
    for t in range(1, 8))


def _rel_bucket(rel):
    n = jnp.abs(rel)
    large = jnp.full(rel.shape, NUM_BUCKETS // 4, jnp.int32)
    for start in _LOG_BUCKET_STARTS:
        large = large + jnp.where(n >= start, 1, 0)
    mag = jnp.where(n < NUM_BUCKETS // 4, n, large)
    return mag + jnp.where(rel > 0, NUM_BUCKETS // 2, 0)


def _window_start(j, sub_len):
    return jnp.clip(j * Q_BLOCK - HALF_WINDOW, 0, sub_len - K_WINDOW)


def _bias_kernel(off_ref, qpos_ref, kpos_ref, tbl_ref, out_ref):
    rel = kpos_ref[0] - qpos_ref[0]
    bucket = _rel_bucket(rel)
    delta = (off_ref[pl.program_id(0)] + lax.broadcasted_iota(jnp.int32, (1, K_WINDOW), 1)
             - lax.broadcasted_iota(jnp.int32, (Q_BLOCK, 1), 0))
    valid = jnp.abs(delta) <= HALF_WINDOW
    for h in range(N_HEADS_A):
        tbl = jnp.broadcast_to(tbl_ref[h:h + 1, :], (Q_BLOCK, LANES))
        bias = jnp.concatenate(
            [jnp.take_along_axis(tbl, bucket[:, c * LANES:(c + 1) * LANES], axis=1)
             for c in range(K_WINDOW // LANES)], axis=1)
        out_ref[0, h] = jnp.where(valid, bias * LOG2_E, NEG_INF)


def _bias_tiles(offs, qpos, kpos, tbl):
    n = offs.shape[0]
    grid_spec = pltpu.PrefetchScalarGridSpec(
        num_scalar_prefetch=1,
        grid=(n,),
        in_specs=[pl.BlockSpec((1, Q_BLOCK, 1), lambda t, off: (t, 0, 0)),
                  pl.BlockSpec((1, 1, K_WINDOW), lambda t, off: (t, 0, 0)),
                  pl.BlockSpec((N_HEADS_A, LANES), lambda t, off: (0, 0))],
        out_specs=pl.BlockSpec((1, N_HEADS_A, Q_BLOCK, K_WINDOW), lambda t, off: (t, 0, 0, 0)),
    )
    return pl.pallas_call(
        _bias_kernel,
        grid_spec=grid_spec,
        out_shape=jax.ShapeDtypeStruct((n, N_HEADS_A, Q_BLOCK, K_WINDOW), F32),
        compiler_params=pltpu.CompilerParams(
            dimension_semantics=("arbitrary",), vmem_limit_bytes=VMEM_LIMIT),
        name="bias_tiles",
    )(offs, qpos, kpos, tbl)


def _dilated_kernel(q_ref, k_ref, v_ref, *refs, sub_len, qb):
    bias_refs = refs[:qb]
    o_ref, lse_ref, s_scr, p_scr = refs[qb:]
    first = lax.broadcasted_iota(jnp.int32, (1, LANES), 1) < HEAD_DIM_A
    pair = lambda h: slice((h // 2) * LANES, (h // 2 + 1) * LANES)
    for sub in range(qb):
        j = pl.program_id(2) * qb + sub
        rows = slice(sub * Q_BLOCK, (sub + 1) * Q_BLOCK)
        kstart = pl.multiple_of(_window_start(j, sub_len), HALF_WINDOW)
        q = q_ref[0, rows, :]
        kw = k_ref[0, pl.ds(kstart, K_WINDOW), :]
        vw = v_ref[0, pl.ds(kstart, K_WINDOW), :]
        for h in range(N_HEADS_A):
            own = first if h % 2 == 0 else jnp.logical_not(first)
            qh = jnp.where(own, q[:, pair(h)], jnp.zeros_like(q[:, pair(h)]))
            s_scr[sub, h] = _dot_nt(qh, kw[:, pair(h)]) + bias_refs[sub][0, h]
        stats = []
        for h in range(N_HEADS_A):
            s = s_scr[sub, h]
            m = jnp.max(s, axis=1, keepdims=True)
            e = jnp.exp2(s - m)
            l = jnp.sum(e, axis=1, keepdims=True)
            p_scr[sub, h] = e.astype(BF16)
            stats.append((1.0 / l, m + jnp.log2(l)))
        for h in range(0, N_HEADS_A, 2):
            (r0, lse0), (r1, lse1) = stats[h], stats[h + 1]
            o0 = _dot(p_scr[sub, h], vw[:, pair(h)])
            o1 = _dot(p_scr[sub, h + 1], vw[:, pair(h)])
            o_ref[0, rows, pair(h)] = jnp.where(first, o0 * r0, o1 * r1)
            lse_ref[0, rows, pair(h)] = jnp.where(first, lse0, lse1)


def _dilated(qv, kv, vv, bias, bias_index, dil):
    b, sub_len, _ = qv.shape
    w = WIDTH_A
    nq = sub_len // Q_BLOCK
    qb = min(MAX_Q_BLOCKS_PER_STEP, nq)
    assert sub_len >= K_WINDOW and sub_len % (Q_BLOCK * qb) == 0
    qspec = pl.BlockSpec((1, Q_BLOCK * qb, w), lambda bi, r, j: (bi, j, r))
    kvspec = pl.BlockSpec((1, sub_len, w), lambda bi, r, j: (bi, 0, r))
    bspecs = [pl.BlockSpec((1, N_HEADS_A, Q_BLOCK, K_WINDOW),
                           lambda bi, r, j, sub=sub:
                           (bias_index(bi, r, j * qb + sub, nq), 0, 0, 0))
              for sub in range(qb)]
    return pl.pallas_call(
        functools.partial(_dilated_kernel, sub_len=sub_len, qb=qb),
        grid=(b, dil, nq // qb),
        in_specs=[qspec, kvspec, kvspec] + bspecs,
        out_specs=[qspec, qspec],
        out_shape=[jax.ShapeDtypeStruct((b, sub_len, dil * w), F32)] * 2,
        scratch_shapes=[pltpu.VMEM((qb, N_HEADS_A, Q_BLOCK, K_WINDOW), F32),
                        pltpu.VMEM((qb, N_HEADS_A, Q_BLOCK, K_WINDOW), BF16)],
        compiler_params=pltpu.CompilerParams(
            dimension_semantics=("parallel", "parallel", "arbitrary"),
            vmem_limit_bytes=VMEM_LIMIT),
        name=f"dilated{dil}",
    )(qv, kv, vv, *([bias] * qb))


def _dilated_all(views, positions, tbl):
    b, s = positions.shape
    dils = [dil for _, dil in DILATED_PATTERNS]
    consecutive = jnp.all(positions[:, 1:] - positions[:, :-1] == 1)

    def run(bias, index_fns):
        outs = []
        for g, dil in enumerate(dils):
            qv, kv, vv = views[3 * g:3 * g + 3]
            outs.extend(_dilated(qv, kv, vv, bias, index_fns[g], dil))
        return tuple(outs)

    def shared_tiles():
        offs, qpos, kpos, fns = [], [], [], []
        for g, dil in enumerate(dils):
            for off in (0, -HALF_WINDOW, -2 * HALF_WINDOW):
                offs.append(off)
                qpos.append(dil * jnp.arange(Q_BLOCK, dtype=jnp.int32))
                kpos.append(dil * (off + jnp.arange(K_WINDOW, dtype=jnp.int32)))
            fns.append(lambda bi, r, j, nq, g=g:
                       3 * g + jnp.where(j == 0, 0, jnp.where(j == nq - 1, 2, 1)))
        bias = _bias_tiles(jnp.array(offs, jnp.int32), jnp.stack(qpos)[:, :, None],
                           jnp.stack(kpos)[:, None, :], tbl)
        return run(bias, fns)

    def per_block_tiles():
        offs, qpos, kpos, fns = [], [], [], []
        base = 0
        for g, dil in enumerate(dils):
            sub_len = s // dil
            nq = sub_len // Q_BLOCK
            pos_t = positions.reshape(b, sub_len, dil).transpose(0, 2, 1).reshape(b * dil, sub_len)
            starts = _window_start(jnp.arange(nq), sub_len)
            win = starts[:, None] + jnp.arange(K_WINDOW)[None, :]
            qpos.append(pos_t.reshape(b * dil * nq, Q_BLOCK))
            kpos.append(pos_t[:, win].reshape(b * dil * nq, K_WINDOW))
            offs.append(jnp.tile(starts - jnp.arange(nq) * Q_BLOCK, b * dil))
            fns.append(lambda bi, r, j, nq, base=base, dil=dil: base + (bi * dil + r) * nq + j)
            base += b * dil * nq
        bias = _bias_tiles(jnp.concatenate(offs).astype(jnp.int32),
                           jnp.concatenate(qpos)[:, :, None], jnp.concatenate(kpos)[:, None, :], tbl)
        return run(bias, fns)

    return lax.cond(consecutive, shared_tiles, per_block_tiles)


def _mla_kernel(q_ref, k_ref, v_ref, o_ref, *, tk):
    q = q_ref[0, 0]
    tq = q.shape[0]
    n_kv = k_ref.shape[2] // tk

    def body(i, carry):
        m, acc = carry
        start = pl.multiple_of(i * tk, tk)
        k = k_ref[0, 0, pl.ds(start, tk), :]
        v = v_ref[0, 0, pl.ds(start, tk), :]
        s = _dot_nt(q, k)
        m_new = jnp.maximum(m, jnp.max(s, axis=1, keepdims=True))
        p = jnp.exp2(s - m_new)
        acc = jnp.exp2(m - m_new) * acc + _dot(p.astype(BF16), v)
        return m_new, acc

    m0 = jnp.full((tq, 1), NEG_INF, F32)
    _, acc = lax.fori_loop(0, n_kv, body, (m0, jnp.zeros((tq, LANES), F32)), unroll=True)
    lane = lax.broadcasted_iota(jnp.int32, (1, LANES), 1)
    denom = jnp.sum(jnp.where(lane == V_DIM_B, acc, 0.0), axis=1, keepdims=True)
    o_ref[0, 0] = jnp.where(lane < V_DIM_B, acc / denom, 0.0).astype(BF16)


def _mla(qm, km, vm, tq, tk):
    b, nh, s, _ = qm.shape
    qspec = pl.BlockSpec((1, 1, tq, LANES), lambda bi, h, i: (bi, h, i, 0))
    kvspec = pl.BlockSpec((1, 1, s, LANES), lambda bi, h, i: (bi, h, 0, 0))
    return pl.pallas_call(
        functools.partial(_mla_kernel, tk=tk),
        grid=(b, nh, s // tq),
        in_specs=[qspec, kvspec, kvspec],
        out_specs=qspec,
        out_shape=jax.ShapeDtypeStruct((b, nh, s, LANES), BF16),
        compiler_params=pltpu.CompilerParams(
            dimension_semantics=("parallel", "parallel", "arbitrary"),
            vmem_limit_bytes=VMEM_LIMIT),
        name="mla",
    )(qm, km, vm)


def _memkv_kernel(mem_ref, g_ref, w_ref, k_ref, v_ref):
    d = mem_ref.shape[2]
    kv = _dot(_rms(mem_ref[0], g_ref[...]).astype(BF16), w_ref[...])
    k_ref[0] = kv[:, :d].astype(BF16)
    v_ref[0] = kv[:, d:].astype(BF16)


def _memkv(mem, g_mem, w_mkv):
    b, m, d = mem.shape
    spec = pl.BlockSpec((1, m, d), lambda bi: (bi, 0, 0))
    return pl.pallas_call(
        _memkv_kernel,
        grid=(b,),
        in_specs=[spec, pl.BlockSpec((1, d), lambda bi: (0, 0)),
                  pl.BlockSpec((d, 2 * d), lambda bi: (0, 0))],
        out_specs=[spec, spec],
        out_shape=[jax.ShapeDtypeStruct((b, m, d), BF16)] * 2,
        compiler_params=pltpu.CompilerParams(
            dimension_semantics=("parallel",), vmem_limit_bytes=VMEM_LIMIT),
        name="memkv",
    )(mem, g_mem, w_mkv)


def _post_kernel(x_ref, o1_ref, o2_ref, o3_ref, l1_ref, l2_ref, l3_ref, ob_ref,
                 goa_ref, gob_ref, woa_ref, wob_ref, gx_ref, wmq_ref, km_ref, vm_ref, wmo_ref,
                 gmoe_ref, wr_ref, br_ref,
                 h_ref, xn_ref, logit_ref, *nat_scr):
    tm = x_ref.shape[1]
    n_chunk = WIDTH_A // LANES

    for view_ref, scr in zip((o2_ref, o3_ref, l2_ref, l3_ref), nat_scr):
        dil = view_ref.shape[2] // WIDTH_A
        for r in range(dil):
            for c in range(n_chunk):
                col = r * WIDTH_A + c * LANES
                scr[c, pl.ds(r, tm // dil, stride=dil), :] = view_ref[0, :, col:col + LANES]

    def natural(scr, rows):
        return jnp.concatenate([scr[c, rows, :] for c in range(n_chunk)], axis=1)

    rows_per_chain = tm // POST_CHAINS

    def chain(ci):
        rows = slice(ci * rows_per_chain, (ci + 1) * rows_per_chain)
        o1, o2, o3 = o1_ref[0, rows, :], natural(nat_scr[0], rows), natural(nat_scr[1], rows)
        l1, l2, l3 = l1_ref[0, rows, :], natural(nat_scr[2], rows), natural(nat_scr[3], rows)
        mx = jnp.maximum(jnp.maximum(l1, l2), l3)
        w1, w2, w3 = jnp.exp2(l1 - mx), jnp.exp2(l2 - mx), jnp.exp2(l3 - mx)
        oa = (w1 * o1 + w2 * o2 + w3 * o3) / (w1 + w2 + w3)
        oan = _rms(oa, goa_ref[...]).astype(BF16)
        ob = jnp.concatenate([ob_ref[0, h, rows, :] for h in range(N_HEADS_B)], axis=1).astype(F32)
        ms_b = jnp.sum(ob * ob, axis=1, keepdims=True) * (1.0 / (N_HEADS_B * V_DIM_B))
        obn = (ob * lax.rsqrt(ms_b + EPS) * gob_ref[...]).astype(BF16)
        yield
        h1 = x_ref[0, rows, :] + _dot(oan, woa_ref[...]) + _dot(obn, wob_ref[...])
        yield
        hn = _rms(h1, gx_ref[...]).astype(BF16)
        yield
        q = _dot(hn, wmq_ref[...]).astype(BF16)
        yield
        dh = q.shape[1] // N_HEADS_MEM
        heads = []
        for h in range(N_HEADS_MEM):
            sl = slice(h * dh, (h + 1) * dh)
            s = _dot_nt(q[:, sl], km_ref[0, :, sl])
            yield
            e = jnp.exp(s - jnp.max(s, axis=1, keepdims=True))
            p = e / jnp.sum(e, axis=1, keepdims=True)
            yield
            heads.append(_dot(p.astype(BF16), vm_ref[0, :, sl]))
        o = jnp.concatenate(heads, axis=1).astype(BF16)
        yield
        h2 = h1 + _dot(o, wmo_ref[...])
        h_ref[0, rows, :] = h2
        yield
        xn = _rms(h2, gmoe_ref[...])
        packed = _pack_bf16_pairs(xn)
        rows_per_token = packed.shape[1] // LANES
        _store_token_rows(xn_ref.at[pl.ds(ci * rows_per_chain * rows_per_token,
                                          rows_per_chain * rows_per_token)], packed)
        yield
        logit_ref[0, rows, :] = jnp.dot(xn, wr_ref[...], preferred_element_type=F32,
                                        precision=lax.Precision.HIGHEST) + br_ref[...]

    _run_skewed([chain(ci) for ci in range(POST_CHAINS)])


def _post(x, o_pats, lse_pats, ob, g_out_a, g_out_b_pad, w_o_a, w_o_b_pad, g_xattn, w_mq, kmem,
          vmem, w_mo, g_moe, w_router_pad, b_router_pad, tm):
    b, s, d = x.shape
    m = kmem.shape[1]
    full = lambda shape: pl.BlockSpec(shape, lambda bi, i: (0,) * len(shape))
    row = lambda w: pl.BlockSpec((1, tm, w), lambda bi, i: (bi, i, 0))
    memspec = pl.BlockSpec((1, m, d), lambda bi, i: (bi, 0, 0))
    hw = N_HEADS_B * LANES
    views = [pl.BlockSpec((1, tm // dil, dil * WIDTH_A), lambda bi, i: (bi, i, 0))
             for _, dil in DILATED_PATTERNS]
    return pl.pallas_call(
        _post_kernel,
        grid=(b, s // tm),
        in_specs=[row(d)] + views + views
        + [pl.BlockSpec((1, N_HEADS_B, tm, LANES), lambda bi, i: (bi, 0, i, 0)),
           full((1, WIDTH_A)), full((1, hw)), full((WIDTH_A, d)), full((hw, d)),
           full((1, d)), full((d, d)), memspec, memspec, full((d, d)),
           full((1, d)), full((d, LANES)), full((1, LANES))],
        out_specs=[row(d), pl.BlockSpec((tm * d // 2 // LANES, LANES), lambda bi, i: (bi * (s // tm) + i, 0)),
                   row(LANES)],
        out_shape=[jax.ShapeDtypeStruct((b, s, d), F32),
                   jax.ShapeDtypeStruct((b * s * d // 2 // LANES, LANES), jnp.uint32),
                   jax.ShapeDtypeStruct((b, s, LANES), F32)],
        scratch_shapes=[pltpu.VMEM((WIDTH_A // LANES, tm, LANES), F32)] * 4,
        compiler_params=pltpu.CompilerParams(
            dimension_semantics=("parallel", "parallel"), vmem_limit_bytes=VMEM_LIMIT),
        name="post",
    )(x, *o_pats, *lse_pats, ob, g_out_a, g_out_b_pad, w_o_a, w_o_b_pad, g_xattn, w_mq, kmem,
      vmem, w_mo, g_moe, w_router_pad, b_router_pad)


def _route_kernel(logit_ref, gate_ref, slot_ref, cslot_ref, count_ref, tcnt_ref, tcarry_ref, ccnt_ref,
                  ccarry_ref, carry_ref):
    @pl.when(pl.program_id(0) == 0)
    def _():
        carry_ref[...] = jnp.zeros_like(carry_ref)

    l = logit_ref[...]
    tr = l.shape[0]
    lane_i = lax.broadcasted_iota(jnp.int32, l.shape, 1)
    lane = lane_i.astype(F32)
    vals, idxs = [], []
    for _ in range(TOP_K):
        m = jnp.max(l, axis=1, keepdims=True)
        idx = jnp.min(jnp.where(l == m, lane, float(LANES)), axis=1, keepdims=True)
        vals.append(m)
        idxs.append(idx)
        l = jnp.where(lane == idx, -jnp.inf, l)
    exps = [jnp.exp(v - vals[0]) for v in vals]
    denom = exps[0] + exps[1] + exps[2] + exps[3]
    onehot = jnp.zeros(l.shape, F32)
    for idx in idxs:
        onehot = onehot + jnp.where(lane == idx, 1.0, 0.0)
    r = lax.broadcasted_iota(jnp.int32, (tr, tr), 0)
    c = lax.broadcasted_iota(jnp.int32, (tr, tr), 1)
    tri = jnp.where(c < r, 1.0, 0.0).astype(BF16)
    local = _dot(tri, onehot.astype(BF16))
    tile_cnt = jnp.sum(onehot, axis=0, keepdims=True)
    li = lax.broadcasted_iota(jnp.int32, (LANES, LANES), 0)
    lj = lax.broadcasted_iota(jnp.int32, (LANES, LANES), 1)
    prefix = jnp.dot(jnp.broadcast_to(tile_cnt, (8, LANES)), jnp.where(li < lj, 1.0, 0.0),
                     preferred_element_type=F32, precision=lax.Precision.HIGHEST)[0:1]
    grouped = local + prefix
    n_sub = tr // COMBINE_TILE
    sub_cnt = [jnp.sum(onehot[u * COMBINE_TILE:(u + 1) * COMBINE_TILE], axis=0, keepdims=True)
               for u in range(n_sub)]
    sub_prefix = jnp.dot(jnp.concatenate(sub_cnt + [jnp.zeros((8 - n_sub, LANES), F32)], axis=0),
                         jnp.where(li < lj, 1.0, 0.0), preferred_element_type=F32,
                         precision=lax.Precision.HIGHEST)
    shifts, sub_base = [], jnp.zeros((1, LANES), F32)
    for u in range(n_sub):
        shifts.append(jnp.broadcast_to(sub_prefix[u:u + 1] - sub_base, (COMBINE_TILE, LANES)))
        ccnt_ref[u * 8:(u + 1) * 8, :] = jnp.broadcast_to(sub_cnt[u], (8, LANES)).astype(jnp.int32)
        ccarry_ref[u * 8:(u + 1) * 8, :] = jnp.broadcast_to(carry_ref[...] + sub_base,
                                                            (8, LANES)).astype(jnp.int32)
        sub_base = sub_base + sub_cnt[u]
    sub_grouped = local + jnp.concatenate(shifts, axis=0)
    gate = jnp.zeros(l.shape, F32)
    slot = jnp.zeros(l.shape, jnp.int32)
    cslot = jnp.zeros(l.shape, jnp.int32)
    for k in range(TOP_K):
        mine = lane == idxs[k]
        sk = jnp.sum(jnp.where(mine, grouped, 0.0), axis=1, keepdims=True)
        ck = jnp.sum(jnp.where(mine, sub_grouped, 0.0), axis=1, keepdims=True)
        gate = jnp.where(lane_i == k, exps[k] / denom, gate)
        slot = jnp.where(lane_i == k, sk.astype(jnp.int32), slot)
        cslot = jnp.where(lane_i == k, ck.astype(jnp.int32), cslot)
    gate_ref[...] = gate
    slot_ref[...] = slot
    cslot_ref[...] = cslot
    tcnt_ref[...] = jnp.broadcast_to(tile_cnt, tcnt_ref.shape).astype(jnp.int32)
    tcarry_ref[...] = jnp.broadcast_to(carry_ref[...], tcarry_ref.shape).astype(jnp.int32)
    total = carry_ref[...] + tile_cnt
    carry_ref[...] = total
    count_ref[...] = total.astype(jnp.int32)


def _route(logits, tr):
    t = logits.shape[0]
    spec = pl.BlockSpec((tr, LANES), lambda i: (i, 0))
    return pl.pallas_call(
        _route_kernel,
        grid=(t // tr,),
        in_specs=[spec],
        out_specs=[spec, spec, spec, pl.BlockSpec((1, LANES), lambda i: (0, 0)),
                   pl.BlockSpec((8, LANES), lambda i: (i, 0)), pl.BlockSpec((8, LANES), lambda i: (i, 0)),
                   pl.BlockSpec((8 * tr // COMBINE_TILE, LANES), lambda i: (i, 0)),
                   pl.BlockSpec((8 * tr // COMBINE_TILE, LANES), lambda i: (i, 0))],
        out_shape=[jax.ShapeDtypeStruct((t, LANES), F32)] + [jax.ShapeDtypeStruct((t, LANES), jnp.int32)] * 2
        + [jax.ShapeDtypeStruct((1, LANES), jnp.int32)]
        + [jax.ShapeDtypeStruct((t // tr * 8, LANES), jnp.int32)] * 2
        + [jax.ShapeDtypeStruct((t // COMBINE_TILE * 8, LANES), jnp.int32)] * 2,
        scratch_shapes=[pltpu.VMEM((1, LANES), F32)],
        compiler_params=pltpu.CompilerParams(
            dimension_semantics=("arbitrary",), vmem_limit_bytes=VMEM_LIMIT),
        name="route",
    )(logits)


def _dispatch_kernel(dst_ref, cnt_ref, ends_ref, x_ref, slot_ref, xs_hbm, zbuf, stage, sem, zsem):
    c = pl.program_id(0)
    buf = lax.rem(c, 2)
    n_chunk = x_ref.shape[0] // DISPATCH_CHUNK
    blk_rows = zbuf.shape[0]
    n_staged = DISPATCH_CHUNK * TOP_K

    @pl.when(c == 0)
    def _():
        zbuf[...] = jnp.zeros_like(zbuf)
        for b in range(2):
            stage[b, pl.ds(n_staged * n_chunk, DISPATCH_PIECE * n_chunk), :] = jnp.zeros(
                (DISPATCH_PIECE * n_chunk, LANES), stage.dtype)

        def zero_copy(start):
            return pltpu.make_async_copy(
                zbuf, xs_hbm.at[pl.ds(pl.multiple_of(start * n_chunk, n_chunk), blk_rows)], zsem)

        blk = blk_rows // n_chunk
        used_end = ends_ref[N_EXPERTS]
        starts = [(ends_ref[e + 1] - ends_ref[e] >= back * blk, ends_ref[e + 1] - back * blk)
                  for e in range(N_EXPERTS) for back in (1, 2)]
        starts += [((used_end + (j + 1) * blk) * n_chunk <= xs_hbm.shape[0], used_end + j * blk)
                   for j in range(N_EXPERTS)]
        for cond, start in starts:
            @pl.when(cond)
            def _():
                zero_copy(start).start()
        for cond, start in starts:
            @pl.when(cond)
            def _():
                zero_copy(start).wait()

    x_lo, x_hi = _unpack_bf16_pairs(_load_token_rows(x_ref, DISPATCH_CHUNK))
    x_lo, x_hi = x_lo.astype(BF16), x_hi.astype(BF16)
    slot_t = jnp.transpose(slot_ref[...].astype(F32))
    bits = lambda v: lax.bitcast_convert_type(v, jnp.uint32)
    for part in range(TOP_K):
        j = (lax.broadcasted_iota(jnp.int32, (DISPATCH_CHUNK, 1), 0) + part * DISPATCH_CHUNK).astype(F32)
        pick = jnp.zeros((DISPATCH_CHUNK, DISPATCH_CHUNK), F32)
        for k in range(TOP_K):
            pick = jnp.where(slot_t[k:k + 1, :] == j, 1.0, pick)
        pick = pick.astype(BF16)
        packed = bits(_dot(pick, x_hi)) | (bits(_dot(pick, x_lo)) >> 16)
        _store_token_rows(
            stage.at[buf, pl.ds(part * DISPATCH_CHUNK * n_chunk, DISPATCH_CHUNK * n_chunk)], packed)

    def piece(src_tok, dst_tok):
        rows = DISPATCH_PIECE * n_chunk
        return pltpu.make_async_copy(
            stage.at[buf, pl.ds(pl.multiple_of(src_tok * n_chunk, n_chunk), rows)],
            xs_hbm.at[pl.ds(pl.multiple_of(dst_tok * n_chunk, n_chunk), rows)], sem)

    def drain(step):
        n_pieces = 0
        for e in range(N_EXPERTS):
            n_pieces = n_pieces + (cnt_ref[step * N_EXPERTS + e] + DISPATCH_PIECE - 1) // DISPATCH_PIECE

        def wait_one(i, carry):
            piece(0, 0).wait()
            return carry
        lax.fori_loop(0, n_pieces, wait_one, 0)

    @pl.when(c > 0)
    def _():
        drain(c - 1)

    staged = 0
    for e in range(N_EXPERTS):
        cnt = cnt_ref[c * N_EXPERTS + e]
        dst = dst_ref[c * N_EXPERTS + e]
        for p in range(DISPATCH_CHUNK // DISPATCH_PIECE):
            @pl.when(p * DISPATCH_PIECE < cnt)
            def _():
                piece(staged + p * DISPATCH_PIECE, dst + p * DISPATCH_PIECE).start()
        staged = staged + cnt

    @pl.when(c == pl.num_programs(0) - 1)
    def _():
        drain(c)


def _dispatch(dst, cnt, ends0, xn_rows, slots, n_tok, n_slots):
    n_chunk = xn_rows.shape[0] // n_tok
    staged_rows = (DISPATCH_CHUNK * TOP_K + DISPATCH_PIECE) * n_chunk
    grid_spec = pltpu.PrefetchScalarGridSpec(
        num_scalar_prefetch=3,
        grid=(n_tok // DISPATCH_CHUNK,),
        in_specs=[pl.BlockSpec((DISPATCH_CHUNK * n_chunk, LANES), lambda c, *_: (c, 0)),
                  pl.BlockSpec((DISPATCH_CHUNK, LANES), lambda c, *_: (c, 0))],
        out_specs=pl.BlockSpec(memory_space=pl.ANY),
        scratch_shapes=[pltpu.VMEM((EXPERT_BLOCK * n_chunk, LANES), xn_rows.dtype),
                        pltpu.VMEM((2, staged_rows, LANES), xn_rows.dtype),
                        pltpu.SemaphoreType.DMA(()), pltpu.SemaphoreType.DMA(())],
    )
    return pl.pallas_call(
        _dispatch_kernel,
        grid_spec=grid_spec,
        out_shape=jax.ShapeDtypeStruct((n_slots * n_chunk, LANES), xn_rows.dtype),
        compiler_params=pltpu.CompilerParams(
            dimension_semantics=("arbitrary",), vmem_limit_bytes=VMEM_LIMIT),
        name="dispatch",
    )(dst, cnt, ends0, xn_rows, slots)


def _expert_kernel(be_ref, nused_ref, next_ref, valid_ref, xs_ref, wgu_hbm, bgu_ref, wd_hbm, bd_ref, y_ref,
                   wgu_f32, wd_f32, wgu_bf, wd_bf, sems):
    i = pl.program_id(0)
    n_chunk = xs_ref.shape[0] // EXPERT_BLOCK

    def mlp(n_rows):
        part = pl.ds(0, n_rows * n_chunk)
        x_lo, x_hi = _unpack_bf16_pairs(_load_token_rows(xs_ref.at[part], n_rows))
        x = jnp.concatenate([x_lo.astype(BF16), x_hi.astype(BF16)], axis=1)
        gu = _dot(x, wgu_bf[...]) + bgu_ref[0]
        de = gu.shape[1] // 2
        gate = jnp.minimum(gu[:, :de], SWIGLU_LIMIT)
        up = jnp.clip(gu[:, de:], -SWIGLU_LIMIT, SWIGLU_LIMIT)
        hmid = (up + 1.0) * (gate * jax.nn.sigmoid(SWIGLU_ALPHA * gate))
        y = _dot(hmid.astype(BF16), wd_bf[...]) + bd_ref[0]
        _store_token_rows(y_ref.at[part], _pack_bf16_pairs(y))

    def fetch(e):
        return (pltpu.make_async_copy(wgu_hbm.at[e], wgu_f32, sems.at[0]),
                pltpu.make_async_copy(wd_hbm.at[e], wd_f32, sems.at[1]))

    @pl.when(i == 0)
    def _():
        for copy in fetch(be_ref[0]):
            copy.start()

    @pl.when(i < nused_ref[0])
    def _():
        @pl.when(jnp.logical_or(i == 0, be_ref[i] != be_ref[jnp.maximum(i - 1, 0)]))
        def _():
            for copy in fetch(be_ref[i]):
                copy.wait()
            wgu_bf[...] = wgu_f32[...].astype(BF16)
            wd_bf[...] = wd_f32[...].astype(BF16)

            @pl.when(next_ref[i] >= 0)
            def _():
                for copy in fetch(next_ref[i]):
                    copy.start()

        valid = valid_ref[i]
        for n_rows in range(EXPERT_STEP, EXPERT_BLOCK + 1, EXPERT_STEP):
            @pl.when(jnp.logical_and(valid > n_rows - EXPERT_STEP, valid <= n_rows))
            def _():
                mlp(n_rows)
                if n_rows < EXPERT_BLOCK:
                    rest = (EXPERT_BLOCK - n_rows) * n_chunk
                    y_ref[pl.ds(n_rows * n_chunk, rest), :] = jnp.zeros((rest, LANES), y_ref.dtype)

    @pl.when(jnp.logical_or(i >= nused_ref[0], valid_ref[i] == 0))
    def _():
        y_ref[...] = jnp.zeros_like(y_ref)


def _experts(block_expert, n_used, next_expert, valid_rows, xs_rows, w_gate_up, b_gate_up, w_down, b_down):
    d, de2 = w_gate_up.shape[1:]
    n_chunk = d // 2 // LANES
    blk = EXPERT_BLOCK
    n_blk = xs_rows.shape[0] // (blk * n_chunk)
    grid_spec = pltpu.PrefetchScalarGridSpec(
        num_scalar_prefetch=4,
        grid=(n_blk,),
        in_specs=[pl.BlockSpec((blk * n_chunk, LANES),
                               lambda i, be, nu, *_: (jnp.maximum(jnp.minimum(i, nu[0] - 1), 0), 0)),
                  pl.BlockSpec(memory_space=pl.ANY),
                  pl.BlockSpec((1, 1, de2), lambda i, be, *_: (be[i], 0, 0)),
                  pl.BlockSpec(memory_space=pl.ANY),
                  pl.BlockSpec((1, 1, d), lambda i, be, *_: (be[i], 0, 0))],
        out_specs=pl.BlockSpec((blk * n_chunk, LANES), lambda i, *_: (i, 0)),
        scratch_shapes=[pltpu.VMEM((d, de2), F32), pltpu.VMEM((de2 // 2, d), F32),
                        pltpu.VMEM((d, de2), BF16), pltpu.VMEM((de2 // 2, d), BF16),
                        pltpu.SemaphoreType.DMA((2,))],
    )
    return pl.pallas_call(
        _expert_kernel,
        grid_spec=grid_spec,
        out_shape=jax.ShapeDtypeStruct(xs_rows.shape, xs_rows.dtype),
        compiler_params=pltpu.CompilerParams(
            dimension_semantics=("arbitrary",), vmem_limit_bytes=VMEM_LIMIT),
        name="experts",
    )(block_expert, n_used, next_expert, valid_rows, xs_rows, w_gate_up, b_gate_up.reshape(N_EXPERTS, 1, de2),
      w_down, b_down.reshape(N_EXPERTS, 1, d))


def _combine_kernel(start_ref, len_ref, h_ref, gate_ref, slot_ref, gfin_ref, y_hbm, out_ref, stage, sems):
    i = pl.program_id(0)
    n = pl.num_programs(0)
    buf = lax.rem(i, 2)
    tc = h_ref.shape[0]
    n_staged = tc * TOP_K
    n_chunk = stage.shape[1] // n_staged

    def issue(tile, b):
        staged = 0
        for e in range(N_EXPERTS):
            run_len = len_ref[tile * N_EXPERTS + e]
            run_start = start_ref[tile * N_EXPERTS + e]
            for bit in reversed(range(COMBINE_TILE.bit_length())):
                size = 1 << bit
                done = (run_len >> (bit + 1)) << (bit + 1)

                @pl.when((run_len >> bit) & 1 == 1)
                def _():
                    pltpu.make_async_copy(
                        y_hbm.at[pl.ds(pl.multiple_of((run_start + done) * n_chunk, n_chunk), size * n_chunk)],
                        stage.at[b, pl.ds(pl.multiple_of((staged + done) * n_chunk, n_chunk), size * n_chunk)],
                        sems.at[b]).start()
            staged = staged + run_len

    @pl.when(i == 0)
    def _():
        issue(0, 0)

    @pl.when(i + 1 < n)
    def _():
        issue(i + 1, 1 - buf)

    pltpu.make_async_copy(y_hbm.at[pl.ds(0, n_staged * n_chunk)], stage.at[buf], sems.at[buf]).wait()

    y_lo, y_hi = _unpack_bf16_pairs(_load_token_rows(stage.at[buf], n_staged))
    y_lo, y_hi = y_lo.astype(BF16), y_hi.astype(BF16)
    gate = gate_ref[...]
    slot = slot_ref[...].astype(F32)
    j = lax.broadcasted_iota(jnp.int32, (1, n_staged), 1).astype(F32)
    pick = jnp.zeros((tc, n_staged), F32)
    for k in range(TOP_K):
        pick = jnp.where(slot[:, k:k + 1] == j, gate[:, k:k + 1], pick)
    pick_hi = pick.astype(BF16)
    pick_lo = (pick - pick_hi.astype(F32)).astype(BF16)
    moe = jnp.concatenate([_dot(pick_hi, y_lo) + _dot(pick_lo, y_lo),
                           _dot(pick_hi, y_hi) + _dot(pick_lo, y_hi)], axis=1)
    out_ref[...] = _rms(h_ref[...] + moe, gfin_ref[...])


def _combine(run_start, run_len, h2, gates, slots, g_final, y_rows):
    t, d = h2.shape
    tc = COMBINE_TILE
    n_chunk = d // 2 // LANES
    grid_spec = pltpu.PrefetchScalarGridSpec(
        num_scalar_prefetch=2,
        grid=(t // tc,),
        in_specs=[pl.BlockSpec((tc, d), lambda i, *_: (i, 0)),
                  pl.BlockSpec((tc, LANES), lambda i, *_: (i, 0)),
                  pl.BlockSpec((tc, LANES), lambda i, *_: (i, 0)),
                  pl.BlockSpec((1, d), lambda i, *_: (0, 0)),
                  pl.BlockSpec(memory_space=pl.ANY)],
        out_specs=pl.BlockSpec((tc, d), lambda i, *_: (i, 0)),
        scratch_shapes=[pltpu.VMEM((2, tc * TOP_K * n_chunk, LANES), y_rows.dtype),
                        pltpu.SemaphoreType.DMA((2,))],
    )
    return pl.pallas_call(
        _combine_kernel,
        grid_spec=grid_spec,
        out_shape=jax.ShapeDtypeStruct((t, d), F32),
        compiler_params=pltpu.CompilerParams(
            dimension_semantics=("arbitrary",), vmem_limit_bytes=VMEM_LIMIT),
        name="combine",
    )(run_start, run_len, h2, gates, slots, g_final, y_rows)


def _pad_heads(w, n_heads, width, offset=0):
    k = w.shape[0]
    w = w.reshape(k, n_heads, width)
    w = jnp.pad(w, ((0, 0), (0, 0), (offset, LANES - width - offset)))
    return w.reshape(k, n_heads * LANES)


def _rot_cols(w):
    half = w.shape[-1] // 2
    return jnp.concatenate([-w[..., half:], w[..., :half]], axis=-1)


def kernel(x, mem, positions, g_mix, w_in, g_q_a, w_q_b, g_kv_a, w_kv_b, rel_bias, g_out_a, g_out_b,
           w_o, g_xattn, g_mem, w_mq, w_mkv, w_mo, g_moe, w_router, b_router, w_gate_up, b_gate_up,
           w_down, b_down, g_final):
    b, s, d = x.shape
    t = b * s
    assert g_mix.shape[0] == 1, "single-layer block: the final norm is fused into the last stage"

    inv_freq = ROPE_THETA ** (-jnp.arange(0, QK_ROPE_DIM, 2, dtype=F32) / QK_ROPE_DIM)
    ang = positions.astype(F32)[..., None] * inv_freq
    cos_sin = jnp.concatenate([jnp.cos(ang), jnp.sin(ang)], axis=-1)
    half = QK_ROPE_DIM // 2
    src = jnp.arange(QK_ROPE_DIM)[:, None]
    lane = jnp.arange(LANES)[None, :]
    on_rope = (lane >= QK_NOPE_DIM) & (lane < QK_NOPE_DIM + QK_ROPE_DIM)
    place = ((lane - QK_NOPE_DIM) % half == src % half) & on_rope
    expand = jnp.concatenate([place & (src < half), place & (src >= half)], axis=1).astype(F32)
    tbl = jnp.pad(rel_bias.T.astype(F32), ((0, 0), (0, LANES - NUM_BUCKETS)))

    h = x
    for l in range(1):
        c0 = 3 * WIDTH_A + Q_LORA_RANK + KV_LORA_RANK
        w_kpe = w_in[l][:, c0:c0 + QK_ROPE_DIM]
        place = lambda w: jnp.pad(w, ((0, 0), (QK_NOPE_DIM, LANES - QK_NOPE_DIM - QK_ROPE_DIM)))
        w_in_ext = jnp.concatenate([w_in[l][:, :c0], place(w_kpe), place(_rot_cols(w_kpe))], 1).astype(BF16)
        dq = QK_NOPE_DIM + QK_ROPE_DIM
        wq3 = w_q_b[l].reshape(Q_LORA_RANK, N_HEADS_B, dq)
        wq = _pad_heads(wq3.reshape(Q_LORA_RANK, -1), N_HEADS_B, dq).astype(BF16)
        wq_rot3 = jnp.concatenate([jnp.zeros_like(wq3[..., :QK_NOPE_DIM]), _rot_cols(wq3[..., QK_NOPE_DIM:])], -1)
        wqr = _pad_heads(wq_rot3.reshape(Q_LORA_RANK, -1), N_HEADS_B, dq).astype(BF16)
        wkv3 = w_kv_b[l].reshape(KV_LORA_RANK, N_HEADS_B, QK_NOPE_DIM + V_DIM_B)
        wk = _pad_heads(wkv3[..., :QK_NOPE_DIM].reshape(KV_LORA_RANK, -1), N_HEADS_B, QK_NOPE_DIM).astype(BF16)
        wv = _pad_heads(wkv3[..., QK_NOPE_DIM:].reshape(KV_LORA_RANK, -1), N_HEADS_B, V_DIM_B).astype(BF16)

        *views, qm, km, vm = _proj(h, cos_sin, expand, g_mix[l][None], w_in_ext, g_q_a[l][None], wq, wqr,
                                   g_kv_a[l][None], wk, wv, tm=256 * PROJ_CHAINS)
        pats = _dilated_all(views, positions, tbl)
        ob = _mla(qm, km, vm, tq=2048, tk=512)
        kmem, vmem = _memkv(mem, g_mem[l][None], w_mkv[l].astype(BF16))

        g_out_b_pad = _pad_heads(g_out_b[l][None], N_HEADS_B, V_DIM_B)
        w_o_b_pad = _pad_heads(w_o[l][WIDTH_A:].T, N_HEADS_B, V_DIM_B).T.astype(BF16)
        w_mq_s = (w_mq[l] * ((d // N_HEADS_MEM) ** -0.5)).astype(BF16)
        w_router_pad = jnp.pad(w_router[l], ((0, 0), (0, LANES - N_EXPERTS)))
        b_router_pad = jnp.pad(b_router[l][None], ((0, 0), (0, LANES - N_EXPERTS)), constant_values=NEG_INF)
        h2, xn_rows, logits = _post(h, pats[0::2], pats[1::2], ob, g_out_a[l][None],
                               g_out_b_pad, w_o[l][:WIDTH_A].astype(BF16), w_o_b_pad, g_xattn[l][None],
                               w_mq_s, kmem, vmem, w_mo[l].astype(BF16), g_moe[l][None], w_router_pad,
                               b_router_pad, tm=256 * POST_CHAINS)

        gates, slots, cslots, counts, tile_cnt, tile_carry, ctile_cnt, ctile_carry = _route(
            logits.reshape(t, LANES), tr=DISPATCH_CHUNK)
        counts = counts[0, :N_EXPERTS]
        blk = EXPERT_BLOCK
        padded = jnp.where(counts > 0, (counts + DISPATCH_PIECE + blk - 1) // blk * blk, 0)
        ends = jnp.cumsum(padded)
        pad_start = ends - padded
        n_blk = t * TOP_K // blk + N_EXPERTS + -(-N_EXPERTS * DISPATCH_PIECE // blk)
        block_expert = jnp.minimum(
            jnp.sum(ends[None, :] <= (jnp.arange(n_blk) * blk)[:, None], axis=1),
            N_EXPERTS - 1).astype(jnp.int32)
        n_used = (ends[-1] // blk).astype(jnp.int32)[None]
        ends0 = jnp.concatenate([jnp.zeros((1,), jnp.int32), ends.astype(jnp.int32)])
        run_start = (pad_start[None, :] + tile_carry[::8, :N_EXPERTS]).astype(jnp.int32).reshape(-1)
        run_len = tile_cnt[::8, :N_EXPERTS].reshape(-1)
        xs_rows = _dispatch(run_start, run_len, ends0, xn_rows, slots, t, n_blk * blk)
        expert_ids = jnp.arange(N_EXPERTS, dtype=jnp.int32)
        run_end = jnp.sum(jnp.where(block_expert[:, None] == expert_ids, ends[None, :], 0), axis=1)
        next_expert = jnp.where(
            run_end < ends[-1],
            jnp.minimum(jnp.sum(ends[None, :] <= run_end[:, None], axis=1), N_EXPERTS - 1), -1).astype(jnp.int32)
        block_of = block_expert[:, None] == expert_ids
        real_end = jnp.sum(jnp.where(block_of, (pad_start + counts)[None, :], 0), axis=1)
        valid_rows = jnp.clip(real_end - jnp.arange(n_blk) * blk, 0, blk).astype(jnp.int32)
        y_rows = _experts(block_expert, n_used, next_expert, valid_rows, xs_rows, w_gate_up[l], b_gate_up[l], w_down[l], b_down[l])
        gather_start = (pad_start[None, :] + ctile_carry[::8, :N_EXPERTS]).astype(jnp.int32).reshape(-1)
        gather_len = ctile_cnt[::8, :N_EXPERTS].reshape(-1)
        out = _combine(gather_start, gather_len, h2.reshape(t, d), gates, cslots, g_final[None], y_rows)
        h = out.reshape(b, s, d)
    return h
```
